```python
import math
import jax
import jax.numpy as jnp
from jax import lax
import numpy as np

D_MODEL = 1024
BATCH = 8
SEQ = 2048
DEPTH = 1
DEC_BATCH = 128
DEC_SEQ = 4
PAST_LEN = 2048
PAGE_SIZE = 128

D_MIX = D_MODEL
NSA_HEAD_DIM = 64
NSA_WIDTH = D_MIX // 2
NSA_Q_HEADS = NSA_WIDTH // NSA_HEAD_DIM
NSA_KV_HEADS = 2
NSA_GROUP = NSA_Q_HEADS // NSA_KV_HEADS
NSA_KV_WIDTH = NSA_KV_HEADS * NSA_HEAD_DIM
CMP_LEN = 32
CMP_STRIDE = 16
CMP_HIDDEN = 128
SEL_BLOCK = 64
TOP_N = 8
WINDOW = 512
SEL_QBLOCK = 32
WIN_QBLOCK = 128
FORCE_BONUS = 1.0e4
ROT_DIM = NSA_HEAD_DIM // 4
ROPE_THETA = 500000.0
GDN_DK = 128
GDN_DV = 128
GDN_WIDTH = D_MIX - NSA_WIDTH
GDN_HEADS = GDN_WIDTH // GDN_DV
GDN_CONV = 4
GDN_CONV_CH = 2 * GDN_HEADS * GDN_DK + GDN_HEADS * GDN_DV
GDN_CHUNK = 64
IN_SIZES = (NSA_WIDTH, NSA_KV_WIDTH, NSA_KV_WIDTH, NSA_KV_WIDTH, NSA_KV_WIDTH, NSA_KV_WIDTH, NSA_KV_WIDTH, 3 * NSA_Q_HEADS, NSA_WIDTH, GDN_CONV_CH, GDN_HEADS, GDN_HEADS, GDN_WIDTH)
N_IN = 2 * NSA_WIDTH + 6 * NSA_KV_WIDTH + 3 * NSA_Q_HEADS + GDN_CONV_CH + 2 * GDN_HEADS + GDN_WIDTH
NORM_EPS = 1e-6
MASK_VALUE = -1e30

kernel_name = 'hymba_nsa_gated_deltanet_step'


def rmsnorm(x, w):
    xf = x.astype(jnp.float32)
    y = xf * lax.rsqrt(jnp.mean(xf * xf, axis=-1, keepdims=True) + NORM_EPS)
    return (y * w.astype(jnp.float32)).astype(x.dtype)


def l2norm(x):
    xf = x.astype(jnp.float32)
    return xf * lax.rsqrt(jnp.sum(xf * xf, axis=-1, keepdims=True) + NORM_EPS)


def rope_partial(x, pos):
    half = ROT_DIM // 2
    inv = ROPE_THETA ** (-(jnp.arange(half, dtype=jnp.float32) * 2.0 / ROT_DIM))
    ang = pos.astype(jnp.float32)[:, None] * inv[None, :]
    bshape = (pos.shape[0],) + (1,) * (x.ndim - 3) + (half,)
    cos = jnp.cos(ang).reshape(bshape)
    sin = jnp.sin(ang).reshape(bshape)
    xr = x[..., :ROT_DIM].astype(jnp.float32)
    x1, x2 = xr[..., :half], xr[..., half:]
    rot = jnp.concatenate([x1 * cos - x2 * sin, x2 * cos + x1 * sin], axis=-1).astype(x.dtype)
    return jnp.concatenate([rot, x[..., ROT_DIM:]], axis=-1)


def pad_rows(r):
    L = r.shape[1]
    lp = -(-L // SEL_BLOCK) * SEL_BLOCK
    return jnp.pad(r, ((0, 0), (0, lp - L), (0, 0), (0, 0)))


def compress_rows(rows, pe, w1, w2):
    B, Lk, H, Dh = rows.shape
    r = CMP_LEN // CMP_STRIDE
    nsub = Lk // CMP_STRIDE
    nc = nsub - r + 1
    sub = rows.reshape(B, nsub, CMP_STRIDE, H, Dh)
    blk = jnp.concatenate([sub[:, j:j + nc] for j in range(r)], axis=2)
    blk = blk + pe[:, None, :].astype(rows.dtype)
    u = blk.transpose(0, 1, 3, 2, 4).reshape(B, nc, H, CMP_LEN * Dh)
    return jax.nn.silu(u @ w1) @ w2


def nsa_cmp_slc(q_rot, q_raw, ck_rows, cv_rows, sk_rows, sv_rows, q_pos, pe_k, wk1, wk2, pe_v, wv1, wv2):
    B, Lq = q_rot.shape[:2]
    Lk = ck_rows.shape[1]
    scale = NSA_HEAD_DIM ** -0.5
    ck = compress_rows(ck_rows, pe_k, wk1, wk2)
    cv = compress_rows(cv_rows, pe_v, wv1, wv2)
    nc = ck.shape[1]
    c_start = jnp.arange(nc, dtype=jnp.int32) * CMP_STRIDE
    cmask = (c_start + CMP_LEN - 1)[None, :] <= q_pos[:, None]
    s = jnp.einsum('bqhgd,bchd->bhgqc', q_raw, ck).astype(jnp.float32) * scale
    p = jax.nn.softmax(jnp.where(cmask, s, MASK_VALUE), axis=-1) * cmask
    o_cmp = jnp.einsum('bhgqc,bchd->bqhgd', p.astype(cv.dtype), cv)
    ns = Lk // SEL_BLOCK
    b_start = jnp.arange(ns, dtype=jnp.int32) * SEL_BLOCK
    ov = jnp.minimum(c_start[:, None] + CMP_LEN, b_start[None, :] + SEL_BLOCK) - jnp.maximum(c_start[:, None], b_start[None, :])
    overlap = jnp.maximum(ov, 0).astype(jnp.float32) / CMP_LEN
    imp = jnp.einsum('bhgqc,cn->bhqn', p, overlap)
    q_blk = q_pos // SEL_BLOCK
    blk = jnp.arange(ns, dtype=jnp.int32)[None, :]
    forced = (blk == 0) | (blk == q_blk[:, None]) | (blk == q_blk[:, None] - 1)
    imp = jnp.where(blk <= q_blk[:, None], imp + FORCE_BONUS * forced, MASK_VALUE)
    n_sel = min(TOP_N, ns)
    _, idx = lax.top_k(imp, n_sel)
    kb = sk_rows.reshape(B, ns, SEL_BLOCK, NSA_KV_HEADS, NSA_HEAD_DIM).transpose(0, 3, 1, 2, 4)
    vb = sv_rows.reshape(B, ns, SEL_BLOCK, NSA_KV_HEADS, NSA_HEAD_DIM).transpose(0, 3, 1, 2, 4)
    bi = jnp.arange(B)[:, None, None, None]
    hi = jnp.arange(NSA_KV_HEADS)[None, :, None, None]
    offs = jnp.arange(SEL_BLOCK, dtype=jnp.int32)

    def one_block(args):
        q_b, idx_b, pos_b = args
        kg = kb[bi, hi, idx_b]
        vg = vb[bi, hi, idx_b]
        sc = jnp.einsum('bqhgd,bhqnsd->bhgqns', q_b, kg).astype(jnp.float32) * scale
        kpos = idx_b[..., None] * SEL_BLOCK + offs
        m = kpos[:, :, None] <= pos_b[:, None, None]
        sc = jnp.where(m, sc, MASK_VALUE)
        shp = sc.shape
        pr = jax.nn.softmax(sc.reshape(shp[:4] + (-1,)), axis=-1).reshape(shp)
        return jnp.einsum('bhgqns,bhqnsd->bqhgd', pr.astype(vg.dtype), vg)

    qb = SEL_QBLOCK if Lq % SEL_QBLOCK == 0 else Lq
    nq = Lq // qb
    q_blocks = q_rot.reshape(B, nq, qb, NSA_KV_HEADS, NSA_GROUP, NSA_HEAD_DIM).swapaxes(0, 1)
    idx_blocks = idx.reshape(B, NSA_KV_HEADS, nq, qb, n_sel).transpose(2, 0, 1, 3, 4)
    pos_blocks = q_pos.reshape(nq, qb)
    o = lax.map(one_block, (q_blocks, idx_blocks, pos_blocks))
    o_slc = o.swapaxes(0, 1).reshape(B, Lq, NSA_KV_HEADS, NSA_GROUP, NSA_HEAD_DIM)
    return o_cmp, o_slc


def window_banded(q, k, v):
    B, T = q.shape[:2]
    qb = WIN_QBLOCK if T % WIN_QBLOCK == 0 else T
    nb = T // qb
    nprev = -(-WINDOW // qb)
    pad = nprev * qb
    span = (nprev + 1) * qb
    sel = jnp.arange(nb)[:, None] + jnp.arange(nprev + 1)[None, :]

    def band(t):
        tp = jnp.pad(t, ((0, 0), (pad, 0), (0, 0), (0, 0))).reshape(B, nb + nprev, qb, NSA_KV_HEADS, NSA_HEAD_DIM)
        return tp[:, sel].reshape(B, nb, span, NSA_KV_HEADS, NSA_HEAD_DIM)

    kw, vw = band(k), band(v)
    q_b = q.reshape(B, nb, qb, NSA_KV_HEADS, NSA_GROUP, NSA_HEAD_DIM)
    s = jnp.einsum('bnqhgd,bnkhd->bhgnqk', q_b, kw).astype(jnp.float32) * (NSA_HEAD_DIM ** -0.5)
    q_pos = jnp.arange(T, dtype=jnp.int32).reshape(nb, qb)
    k_pos = jnp.arange(nb, dtype=jnp.int32)[:, None] * qb - pad + jnp.arange(span, dtype=jnp.int32)[None, :]
    diff = q_pos[:, :, None] - k_pos[:, None, :]
    m = (diff >= 0) & (diff <= WINDOW) & (k_pos[:, None, :] >= 0)
    p = jax.nn.softmax(jnp.where(m, s, MASK_VALUE), axis=-1)
    o = jnp.einsum('bhgnqk,bnkhd->bnqhgd', p.astype(vw.dtype), vw)
    return o.reshape(B, T, NSA_KV_HEADS, NSA_GROUP, NSA_HEAD_DIM)


def window_dense(q, k, v, q_pos, k_pos):
    s = jnp.einsum('bqhgd,bkhd->bhgqk', q, k).astype(jnp.float32) * (NSA_HEAD_DIM ** -0.5)
    diff = q_pos[:, None] - k_pos[None, :]
    m = (diff >= 0) & (diff <= WINDOW)
    p = jax.nn.softmax(jnp.where(m, s, MASK_VALUE), axis=-1)
    return jnp.einsum('bhgqk,bkhd->bqhgd', p.astype(v.dtype), v)


def nsa_merge(gate, o_cmp, o_slc, o_win, z):
    B, L = gate.shape[:2]
    g = jax.nn.sigmoid(gate.astype(jnp.float32)).reshape(B, L, 3, NSA_KV_HEADS, NSA_GROUP, 1).astype(o_cmp.dtype)
    o = g[:, :, 0] * o_cmp + g[:, :, 1] * o_slc + g[:, :, 2] * o_win
    return o.reshape(B, L, NSA_WIDTH) * jax.nn.silu(z)


def short_conv(x, buf, w):
    L = x.shape[1]
    xp = jnp.concatenate([buf.astype(x.dtype), x], axis=1)
    y = xp[:, 0:L] * w[0]
    for j in range(1, GDN_CONV):
        y = y + xp[:, j:j + L] * w[j]
    return jax.nn.silu(y), xp[:, -(GDN_CONV - 1):]


def gated_delta_chunked(q, k, v, beta, g, s0):
    B, L, H, Dk = q.shape
    Dv = v.shape[-1]
    C = min(GDN_CHUNK, L)
    n = -(-L // C)
    pad = n * C - L
    f32 = jnp.float32

    def to_chunks(t):
        t = jnp.pad(t.astype(f32), ((0, 0), (0, pad)) + ((0, 0),) * (t.ndim - 2))
        t = t.reshape((B, n, C) + t.shape[2:])
        return jnp.moveaxis(t, 3, 1)

    q, k, v, beta, g = to_chunks(q), to_chunks(k), to_chunks(v), to_chunks(beta), to_chunks(g)
    decay = jnp.cumsum(g, axis=-1)
    tri = jnp.tril(jnp.ones((C, C), dtype=bool))
    tri_s = jnp.tril(jnp.ones((C, C), dtype=bool), -1)
    diff = decay[..., :, None] - decay[..., None, :]
    dmask = jnp.where(tri, jnp.exp(jnp.where(tri, diff, 0.0)), 0.0)
    kb = k * beta[..., None]
    a_mat = jnp.where(tri_s, jnp.einsum('bhncd,bhnsd->bhncs', kb, k) * dmask, 0.0)
    t_mat = a_mat + jnp.eye(C, dtype=f32)
    rhs = jnp.concatenate([v * beta[..., None], kb * jnp.exp(decay)[..., None]], axis=-1)
    sol = lax.linalg.triangular_solve(t_mat, rhs, left_side=True, lower=True, unit_diagonal=True)
    u, w = sol[..., :Dv], sol[..., Dv:]
    qk = jnp.einsum('bhncd,bhnsd->bhncs', q, k) * dmask
    qd = q * jnp.exp(decay)[..., None]
    kd = k * jnp.exp(decay[..., -1:] - decay)[..., None]
    gl = jnp.exp(decay[..., -1])
    xs = tuple(jnp.moveaxis(t, 2, 0) for t in (u, w, qk, qd, kd, gl))

    def step(S, xc):
        u_c, w_c, qk_c, qd_c, kd_c, gl_c = xc
        v_new = u_c - jnp.einsum('bhcd,bhde->bhce', w_c, S)
        o = jnp.einsum('bhcd,bhde->bhce', qd_c, S) + jnp.einsum('bhcs,bhse->bhce', qk_c, v_new)
        S = S * gl_c[..., None, None] + jnp.einsum('bhcd,bhce->bhde', kd_c, v_new)
        return S, o

    S, o = lax.scan(step, s0.astype(f32), xs)
    o = o.transpose(1, 0, 3, 2, 4).reshape(B, n * C, H, Dv)[:, :L]
    return o, S.astype(s0.dtype)


def gdn_mixer(qkv, b_l, a_l, z, conv_buf, s0, w_conv, a_log, dt_bias, w_gnorm):
    B, L, _ = qkv.shape
    act, conv_new = short_conv(qkv, conv_buf, w_conv)
    nq = GDN_HEADS * GDN_DK
    q = l2norm(act[..., :nq].reshape(B, L, GDN_HEADS, GDN_DK)) * (GDN_DK ** -0.5)
    k = l2norm(act[..., nq:2 * nq].reshape(B, L, GDN_HEADS, GDN_DK))
    v = act[..., 2 * nq:].reshape(B, L, GDN_HEADS, GDN_DV)
    beta = jax.nn.sigmoid(b_l.astype(jnp.float32))
    g = -jnp.exp(a_log.astype(jnp.float32)) * jax.nn.softplus(a_l.astype(jnp.float32) + dt_bias.astype(jnp.float32))
    o, s_new = gated_delta_chunked(q, k, v, beta, g, s0)
    o = rmsnorm(o.astype(qkv.dtype), w_gnorm) * jax.nn.silu(z.reshape(B, L, GDN_HEADS, GDN_DV))
    return o.reshape(B, L, GDN_WIDTH), conv_new, s_new


def project(x, pos, w_norm, w_in):
    B, L, _ = x.shape
    h = rmsnorm(x, w_norm) @ w_in
    cuts = np.cumsum(np.array(IN_SIZES))[:-1].tolist()
    q, ck, cv, sk, sv, wk, wv, gate, z_nsa, qkv, b_l, a_l, z_gdn = jnp.split(h, cuts, axis=-1)
    kvshape = (B, L, NSA_KV_HEADS, NSA_HEAD_DIM)
    q = q.reshape(B, L, NSA_KV_HEADS, NSA_GROUP, NSA_HEAD_DIM)
    ck, cv, sk, sv, wk, wv = [t.reshape(kvshape) for t in (ck, cv, sk, sv, wk, wv)]
    return (q, rope_partial(q, pos), ck, cv, rope_partial(sk, pos), sv, rope_partial(wk, pos), wv, gate, z_nsa, qkv, b_l, a_l, z_gdn)


def layer_prompt(x, lw):
    w_norm, w_in, pe_k, wk1, wk2, pe_v, wv1, wv2, w_conv, a_log, dt_bias, w_gnorm, w_out = lw
    B, T, _ = x.shape
    pos = jnp.arange(T, dtype=jnp.int32)
    q, q_r, ck, cv, sk_r, sv, wk_r, wv, gate, z_nsa, qkv, b_l, a_l, z_gdn = project(x, pos, w_norm, w_in)
    o_cmp, o_slc = nsa_cmp_slc(q_r, q, pad_rows(ck), pad_rows(cv), pad_rows(sk_r), pad_rows(sv), pos, pe_k, wk1, wk2, pe_v, wv1, wv2)
    o_win = window_banded(q_r, wk_r, wv)
    nsa_out = nsa_merge(gate, o_cmp, o_slc, o_win, z_nsa)
    conv0 = jnp.zeros((B, GDN_CONV - 1, GDN_CONV_CH), x.dtype)
    s0 = jnp.zeros((B, GDN_HEADS, GDN_DK, GDN_DV), x.dtype)
    gdn_out, conv_new, s_new = gdn_mixer(qkv, b_l, a_l, z_gdn, conv0, s0, w_conv, a_log, dt_bias, w_gnorm)
    y = x + jnp.concatenate([nsa_out, gdn_out], axis=-1) @ w_out
    wb = min(WINDOW, PAST_LEN)
    lead = ((0, 0), (max(wb - T, 0), 0), (0, 0), (0, 0))
    win_k = jnp.pad(wk_r, lead)[:, -wb:]
    win_v = jnp.pad(wv, lead)[:, -wb:]
    return y, (ck, cv, sk_r, sv, win_k, win_v, conv_new, s_new)


def layer_sample(x, c_cmp_k, c_cmp_v, c_slc_k, c_slc_v, c_win_k, c_win_v, s_conv, s_gdn, page_table, lw):
    w_norm, w_in, pe_k, wk1, wk2, pe_v, wv1, wv2, w_conv, a_log, dt_bias, w_gnorm, w_out = lw
    B, L, _ = x.shape
    pos = PAST_LEN + jnp.arange(L, dtype=jnp.int32)
    q, q_r, ck, cv, sk_r, sv, wk_r, wv, gate, z_nsa, qkv, b_l, a_l, z_gdn = project(x, pos, w_norm, w_in)

    def with_past(cache, new):
        past = cache[page_table].reshape(B, -1, NSA_KV_HEADS, NSA_HEAD_DIM)
        return pad_rows(jnp.concatenate([past.astype(new.dtype), new], axis=1))

    o_cmp, o_slc = nsa_cmp_slc(q_r, q, with_past(c_cmp_k, ck), with_past(c_cmp_v, cv), with_past(c_slc_k, sk_r), with_past(c_slc_v, sv), pos, pe_k, wk1, wk2, pe_v, wv1, wv2)
    wb = c_win_k.shape[1]
    keys = jnp.concatenate([c_win_k.astype(wk_r.dtype), wk_r], axis=1)
    vals = jnp.concatenate([c_win_v.astype(wv.dtype), wv], axis=1)
    k_pos = PAST_LEN - wb + jnp.arange(wb + L, dtype=jnp.int32)
    o_win = window_dense(q_r, keys, vals, pos, k_pos)
    nsa_out = nsa_merge(gate, o_cmp, o_slc, o_win, z_nsa)
    gdn_out, conv_new, s_new = gdn_mixer(qkv, b_l, a_l, z_gdn, s_conv, s_gdn, w_conv, a_log, dt_bias, w_gnorm)
    y = x + jnp.concatenate([nsa_out, gdn_out], axis=-1) @ w_out
    return y, (ck, cv, sk_r, sv, keys[:, -wb:], vals[:, -wb:], conv_new, s_new)


def setup_inputs(seed: int = 0) -> dict:
    key = jax.random.key(seed)
    ks = jax.random.split(key, 24)
    f32 = jnp.float32
    n_pages = PAST_LEN // PAGE_SIZE
    n_pool = (DEC_BATCH * n_pages * 5) // 4
    win_buf = min(WINDOW, PAST_LEN)

    def nrm(k, shape, scale):
        return jax.random.normal(k, shape, f32) * scale

    kv_pool = (DEPTH, n_pool, PAGE_SIZE, NSA_KV_HEADS, NSA_HEAD_DIM)
    win_shape = (DEPTH, DEC_BATCH, win_buf, NSA_KV_HEADS, NSA_HEAD_DIM)
    perm = jax.random.permutation(ks[10], n_pool)
    page_table = perm[:DEC_BATCH * n_pages].reshape(DEC_BATCH, n_pages).astype(jnp.int32)
    dt = jnp.exp(jax.random.uniform(ks[20], (DEPTH, GDN_HEADS), f32, math.log(1e-3), math.log(1e-1)))
    return {
        'x_prompt': nrm(ks[0], (BATCH, SEQ, D_MODEL), 1.0),
        'x_sample': nrm(ks[1], (DEC_BATCH, DEC_SEQ, D_MODEL), 1.0),
        'cache_cmp_k': nrm(ks[2], kv_pool, 1.0),
        'cache_cmp_v': nrm(ks[3], kv_pool, 1.0),
        'cache_slc_k': nrm(ks[4], kv_pool, 1.0),
        'cache_slc_v': nrm(ks[5], kv_pool, 1.0),
        'cache_win_k': nrm(ks[6], win_shape, 1.0),
        'cache_win_v': nrm(ks[7], win_shape, 1.0),
        'state_conv': nrm(ks[8], (DEPTH, DEC_BATCH, GDN_CONV - 1, GDN_CONV_CH), 1.0),
        'state_gdn': nrm(ks[9], (DEPTH, DEC_BATCH, GDN_HEADS, GDN_DK, GDN_DV), 0.5),
        'page_table': page_table,
        'w_norm': 1.0 + nrm(ks[11], (DEPTH, D_MODEL), 0.02),
        'w_in': nrm(ks[12], (DEPTH, D_MODEL, N_IN), D_MODEL ** -0.5),
        'pe_cmp_k': nrm(ks[13], (DEPTH, CMP_LEN, NSA_HEAD_DIM), 0.1),
        'w_cmp_k1': nrm(ks[14], (DEPTH, CMP_LEN * NSA_HEAD_DIM, CMP_HIDDEN), (CMP_LEN * NSA_HEAD_DIM) ** -0.5),
        'w_cmp_k2': nrm(ks[15], (DEPTH, CMP_HIDDEN, NSA_HEAD_DIM), CMP_HIDDEN ** -0.5),
        'pe_cmp_v': nrm(ks[16], (DEPTH, CMP_LEN, NSA_HEAD_DIM), 0.1),
        'w_cmp_v1': nrm(ks[17], (DEPTH, CMP_LEN * NSA_HEAD_DIM, CMP_HIDDEN), (CMP_LEN * NSA_HEAD_DIM) ** -0.5),
        'w_cmp_v2': nrm(ks[18], (DEPTH, CMP_HIDDEN, NSA_HEAD_DIM), CMP_HIDDEN ** -0.5),
        'w_conv': nrm(ks[19], (DEPTH, GDN_CONV, GDN_CONV_CH), GDN_CONV ** -0.5),
        'a_log': jnp.log(jax.random.uniform(ks[21], (DEPTH, GDN_HEADS), f32, 1.0, 16.0)),
        'dt_bias': jnp.log(jnp.expm1(dt)),
        'w_gdn_norm': 1.0 + nrm(ks[22], (DEPTH, GDN_DV), 0.02),
        'w_out': nrm(ks[23], (DEPTH, D_MIX, D_MODEL), D_MIX ** -0.5),
        'w_final_norm': 1.0 + nrm(jax.random.fold_in(key, 99), (D_MODEL,), 0.02),
    }


def reference(x_prompt, x_sample, cache_cmp_k, cache_cmp_v, cache_slc_k, cache_slc_v, cache_win_k, cache_win_v, state_conv, state_gdn, page_table, w_norm, w_in, pe_cmp_k, w_cmp_k1, w_cmp_k2, pe_cmp_v, w_cmp_v1, w_cmp_v2, w_conv, a_log, dt_bias, w_gdn_norm, w_out, w_final_norm):
    h_p, h_s = x_prompt, x_sample
    st_p, st_s = [], []
    for layer in range(DEPTH):
        lw = (w_norm[layer], w_in[layer], pe_cmp_k[layer], w_cmp_k1[layer], w_cmp_k2[layer], pe_cmp_v[layer], w_cmp_v1[layer], w_cmp_v2[layer], w_conv[layer], a_log[layer], dt_bias[layer], w_gdn_norm[layer], w_out[layer])
        h_p, sp = layer_prompt(h_p, lw)
        h_s, ss = layer_sample(h_s, cache_cmp_k[layer], cache_cmp_v[layer], cache_slc_k[layer], cache_slc_v[layer], cache_win_k[layer], cache_win_v[layer], state_conv[layer], state_gdn[layer], page_table, lw)
        st_p.append(sp)
        st_s.append(ss)

    def stk(states, i):
        return jnp.stack([s[i] for s in states])

    y_prompt = rmsnorm(h_p, w_final_norm)
    y_sample = rmsnorm(h_s, w_final_norm)
    return (y_prompt, y_sample, stk(st_p, 0), stk(st_s, 0), stk(st_p, 1), stk(st_s, 1), stk(st_p, 2), stk(st_s, 2), stk(st_p, 3), stk(st_s, 3), stk(st_p, 4), stk(st_s, 4), stk(st_p, 5), stk(st_s, 5), stk(st_p, 6), stk(st_s, 6), stk(st_p, 7), stk(st_s, 7))
```

```python
import functools

import numpy as np
import jax
import jax.numpy as jnp
from jax import lax
from jax.experimental import pallas as pl
from jax.experimental.pallas import tpu as pltpu

F32 = jnp.float32
BF16 = jnp.bfloat16

D_MODEL = 1024
HEAD_DIM = 64
Q_HEADS = 8
KV_HEADS = 2
GROUP = Q_HEADS // KV_HEADS
NSA_WIDTH = Q_HEADS * HEAD_DIM
KV_WIDTH = KV_HEADS * HEAD_DIM
CMP_LEN = 32
CMP_STRIDE = 16
CMP_HIDDEN = 128
SEL_BLOCK = 64
TOP_N = 8
WINDOW = 512
FORCE_BONUS = 1.0e4
ROT_DIM = HEAD_DIM // 4
ROPE_THETA = 500000.0
GDN_DK = 128
GDN_DV = 128
GDN_HEADS = 4
GDN_WIDTH = GDN_HEADS * GDN_DV
GDN_CONV = 4
GDN_CONV_CH = 3 * GDN_WIDTH
GDN_CHUNK = 64
NORM_EPS = 1e-6
MASK_VALUE = -1e30

LANES = 128
SUBLANES = 8
VMEM_LIMIT_BYTES = 56 * 1024 * 1024

QPAD_WIDTH = Q_HEADS * LANES
C_Q = 0
C_KV = C_Q + QPAD_WIDTH
C_ZN = C_KV + 6 * KV_WIDTH
C_QKV = C_ZN + NSA_WIDTH
C_ZG = C_QKV + GDN_CONV_CH
C_SM = C_ZG + GDN_WIDTH
N_PACK = C_SM + LANES
SM_B = 3 * Q_HEADS
SM_A = SM_B + GDN_HEADS

TQ = 128
CK_SLC = 256
CK_WIN = 128
N_SUB = 128
N_CMP = N_SUB - CMP_LEN // CMP_STRIDE + 1
SAMPLE_ROWS = 8


def _pack_w_in(w_in):
    o_gate = NSA_WIDTH + 6 * KV_WIDTH
    o_zn = o_gate + 3 * Q_HEADS
    o_qkv = o_zn + NSA_WIDTH
    o_b = o_qkv + GDN_CONV_CH
    o_a = o_b + GDN_HEADS
    o_zg = o_a + GDN_HEADS
    d = w_in.shape[0]
    z64 = jnp.zeros((d, HEAD_DIM), w_in.dtype)
    qcols = []
    for hq in range(Q_HEADS):
        wq = w_in[:, hq * HEAD_DIM:(hq + 1) * HEAD_DIM]
        qcols += [wq, z64] if hq // GROUP == 0 else [z64, wq]
    pad = jnp.zeros((d, LANES - SM_A - GDN_HEADS), w_in.dtype)
    return jnp.concatenate(
        qcols + [w_in[:, NSA_WIDTH:o_gate], w_in[:, o_zn:o_qkv], w_in[:, o_qkv:o_b], w_in[:, o_zg:],
                 w_in[:, o_gate:o_zn], w_in[:, o_b:o_a], w_in[:, o_a:o_zg], pad], axis=1)


def _rope_tables(pos):
    half = ROT_DIM // 2
    inv = ROPE_THETA ** (-(jnp.arange(half, dtype=F32) * 2.0 / ROT_DIM))
    ang = pos.astype(F32)[:, None] * inv[None, :]
    cos, sin = jnp.cos(ang), jnp.sin(ang)
    n = pos.shape[0]
    one = jnp.ones((n, HEAD_DIM - ROT_DIM), F32)
    zero = jnp.zeros((n, HEAD_DIM - ROT_DIM), F32)
    zh = jnp.zeros((n, half), F32)
    c64 = jnp.concatenate([cos, cos, one], axis=1)
    a64 = jnp.concatenate([zh, sin, zero], axis=1)
    b64 = jnp.concatenate([-sin, zh, zero], axis=1)
    tile = lambda t: jnp.concatenate([t, t], axis=1)
    return tile(c64), tile(a64), tile(b64)


def _rope128(x, c, a, b):
    half = ROT_DIM // 2
    return x * c + pltpu.roll(x, half, 1) * a + pltpu.roll(x, LANES - half, 1) * b


_NN = (((1,), (0,)), ((), ()))
_NT = (((1,), (1,)), ((), ()))
_TN = (((0,), (0,)), ((), ()))


def _mm(a, b, dims=_NN, hi=False):
    if hi:
        return lax.dot_general(a.astype(F32), b.astype(F32), dims, precision=lax.Precision.HIGHEST,
                               preferred_element_type=F32)
    return lax.dot_general(a.astype(BF16), b.astype(BF16), dims, preferred_element_type=F32)


def _mm_exact_lhs(a_bf16, x, dims=_NN):
    x1 = x.astype(BF16)
    r1 = x - x1.astype(F32)
    x2 = r1.astype(BF16)
    x3 = (r1 - x2.astype(F32)).astype(BF16)
    dot = lambda t: lax.dot_general(a_bf16, t, dims, preferred_element_type=F32)
    return dot(x1) + dot(x2) + dot(x3)


def _sigmoid(x):
    return 1.0 / (1.0 + jnp.exp(-x))


def _silu(x):
    return x * _sigmoid(x)


def _tile_rows(x, n):
    return jnp.concatenate([x] * n, axis=0)


def _tile_lanes(x, n):
    return x if n == 1 else jnp.concatenate([x] * n, axis=1)


def _eye(n, dtype):
    r = lax.broadcasted_iota(jnp.int32, (n, n), 0)
    c = lax.broadcasted_iota(jnp.int32, (n, n), 1)
    return (r == c).astype(dtype)


def _proj_kernel(x_ref, wn_ref, w_ref, c_ref, a_ref, b_ref,
                 q_ref, qr_ref, ck_ref, cv_ref, sk_ref, sv_ref, wk_ref, wv_ref, kvb_ref,
                 zn_ref, qkv_ref, zg_ref, sm_ref, *, hi):
    x = x_ref[...]
    ms = jnp.mean(x * x, axis=-1, keepdims=True)
    xn = x * lax.rsqrt(ms + NORM_EPS) * wn_ref[...]
    if not hi:
        xn = xn.astype(BF16)
    c, a, b = c_ref[...], a_ref[...], b_ref[...]
    scale = HEAD_DIM ** -0.5
    for j in range(Q_HEADS):
        qj = _mm(xn, w_ref[:, C_Q + j * LANES:C_Q + (j + 1) * LANES], hi=hi)
        q_ref[:, j * LANES:(j + 1) * LANES] = (qj * scale).astype(q_ref.dtype)
        qr_ref[:, j * LANES:(j + 1) * LANES] = (_rope128(qj, c, a, b) * scale).astype(qr_ref.dtype)
    kv_refs = (ck_ref, cv_ref, sk_ref, sv_ref, wk_ref, wv_ref)
    for j in range(6):
        kj = _mm(xn, w_ref[:, C_KV + j * LANES:C_KV + (j + 1) * LANES], hi=hi)
        if j in (2, 4):
            kj = _rope128(kj, c, a, b)
        kv_refs[j][...] = kj
        if j >= 2:
            kvb_ref[:, (j - 2) * LANES:(j - 1) * LANES] = kj.astype(BF16)
    zn_ref[...] = _mm(xn, w_ref[:, C_ZN:C_QKV], hi=hi)
    for j in range(3):
        lo = C_QKV + j * GDN_WIDTH
        qkv_ref[:, j * GDN_WIDTH:(j + 1) * GDN_WIDTH] = _mm(xn, w_ref[:, lo:lo + GDN_WIDTH], hi=hi)
    zg_ref[...] = _mm(xn, w_ref[:, C_ZG:C_SM], hi=hi)
    sm_ref[...] = _mm(xn, w_ref[:, C_SM:N_PACK], hi=hi)


def _project(x2d, w_norm, w_pack, tables, rows_per_seq, tm, hi, q_dtype):
    n = x2d.shape[0]
    nt = rows_per_seq // tm
    row = lambda i: (i, 0)
    tab = lambda i: (i % nt, 0)
    fix = lambda i: (0, 0)
    widths = (QPAD_WIDTH, QPAD_WIDTH) + (KV_WIDTH,) * 6 + (4 * KV_WIDTH, NSA_WIDTH, GDN_CONV_CH, GDN_WIDTH, LANES)
    dtypes = (q_dtype, q_dtype) + (F32,) * 6 + (BF16, F32, F32, F32, F32)
    return pl.pallas_call(
        functools.partial(_proj_kernel, hi=hi),
        grid=(n // tm,),
        in_specs=[pl.BlockSpec((tm, D_MODEL), row),
                  pl.BlockSpec((1, D_MODEL), fix),
                  pl.BlockSpec((D_MODEL, N_PACK), fix),
                  pl.BlockSpec((tm, LANES), tab),
                  pl.BlockSpec((tm, LANES), tab),
                  pl.BlockSpec((tm, LANES), tab)],
        out_specs=[pl.BlockSpec((tm, w), row) for w in widths],
        out_shape=[jax.ShapeDtypeStruct((n, w), d) for w, d in zip(widths, dtypes)],
        compiler_params=pltpu.CompilerParams(dimension_semantics=("arbitrary",),
                                             vmem_limit_bytes=VMEM_LIMIT_BYTES),
        name="in_proj",
    )(x2d, w_norm.reshape(1, D_MODEL), w_pack, *tables)


def _out_kernel(x_ref, nsa_ref, gdn_ref, w_ref, wf_ref, y_ref, *, hi, final):
    acc = (_mm(nsa_ref[...], w_ref[:NSA_WIDTH, :], hi=hi)
           + _mm(gdn_ref[...], w_ref[NSA_WIDTH:, :], hi=hi))
    h = x_ref[...] + acc
    if final:
        ms = jnp.mean(h * h, axis=-1, keepdims=True)
        h = h * lax.rsqrt(ms + NORM_EPS) * wf_ref[...]
    y_ref[...] = h


def _out_project(x2d, nsa, gdn, w_out, w_final, tm, hi, final):
    n = x2d.shape[0]
    row = lambda i: (i, 0)
    fix = lambda i: (0, 0)
    return pl.pallas_call(
        functools.partial(_out_kernel, hi=hi, final=final),
        grid=(n // tm,),
        in_specs=[pl.BlockSpec((tm, D_MODEL), row),
                  pl.BlockSpec((tm, NSA_WIDTH), row),
                  pl.BlockSpec((tm, GDN_WIDTH), row),
                  pl.BlockSpec((D_MODEL, D_MODEL), fix),
                  pl.BlockSpec((1, D_MODEL), fix)],
        out_specs=pl.BlockSpec((tm, D_MODEL), row),
        out_shape=jax.ShapeDtypeStruct((n, D_MODEL), F32),
        compiler_params=pltpu.CompilerParams(dimension_semantics=("arbitrary",),
                                             vmem_limit_bytes=VMEM_LIMIT_BYTES),
        name="out_proj",
    )(x2d, nsa, gdn, w_out, w_final.reshape(1, D_MODEL))


def _compress_weights(pe, w1, w2):
    half = CMP_STRIDE * HEAD_DIM
    z = jnp.zeros((CMP_STRIDE, HEAD_DIM, CMP_HIDDEN), w1.dtype)

    def place(wpart, h):
        wp = wpart.reshape(CMP_STRIDE, HEAD_DIM, CMP_HIDDEN)
        parts = [wp, z] if h == 0 else [z, wp]
        return jnp.stack(parts, axis=1).reshape(CMP_STRIDE * KV_WIDTH, CMP_HIDDEN)

    w1big = jnp.concatenate([place(w1[:half], 0), place(w1[:half], 1),
                             place(w1[half:], 0), place(w1[half:], 1)], axis=1)
    zz = jnp.zeros_like(w2)
    w2big = jnp.concatenate([jnp.concatenate([w2, zz], axis=1), jnp.concatenate([zz, w2], axis=1)], axis=0)
    pe_a = jnp.tile(pe[:CMP_STRIDE], (1, KV_HEADS)).reshape(1, CMP_STRIDE * KV_WIDTH)
    pe_b = jnp.tile(pe[CMP_STRIDE:], (1, KV_HEADS)).reshape(1, CMP_STRIDE * KV_WIDTH)
    pe2 = jnp.concatenate([pe_a, pe_b, jnp.zeros((SUBLANES - 2, CMP_STRIDE * KV_WIDTH), pe.dtype)], axis=0)
    return w1big.astype(BF16), w2big.astype(BF16), pe2


def _compress(sub_rows, w1_ref, w2_ref, pe_ref):
    hid = 2 * CMP_HIDDEN
    w1 = w1_ref[...]
    ab = _mm(sub_rows, w1)
    pe = pe_ref[...]
    pe_hi = pe.astype(BF16)
    pe_lo = (pe - pe_hi.astype(F32)).astype(BF16)
    r = _mm(pe_hi, w1) + _mm(pe_lo, w1)
    bias = r[0:1, :hid] + r[1:2, hid:]
    h = ab[:, :hid] + pltpu.roll(ab[:, hid:], N_SUB - 1, 0) + bias
    return _mm(_silu(h), w2_ref[...])


def _overlap_t(ns):
    c0 = np.arange(N_CMP)[None, :] * CMP_STRIDE
    b0 = np.arange(ns)[:, None] * SEL_BLOCK
    ov = np.minimum(c0 + CMP_LEN, b0 + SEL_BLOCK) - np.maximum(c0, b0)
    out = np.zeros((LANES, LANES), np.float32)
    out[:ns, :N_CMP] = np.maximum(ov, 0) / CMP_LEN
    return jnp.asarray(out, BF16)


def _cmp_attention(q_stack, ck, cv, cvalid, groups):
    s = _mm(q_stack, ck, _NT)
    cm = _tile_rows(cvalid, groups)
    s = jnp.where(cm, s, MASK_VALUE)
    m = jnp.max(s, axis=1, keepdims=True)
    e = jnp.exp(s - m)
    p = e / jnp.sum(e, axis=1, keepdims=True) * cm.astype(F32)
    return _mm(p, cv), p


def _select_blocks(imp_t, n_idx, q_blk, n_rows):
    forced = (n_idx == 0) | (n_idx == q_blk) | (n_idx == q_blk - 1)
    allowed = n_idx <= q_blk
    v = jnp.where(allowed, imp_t + FORCE_BONUS * forced.astype(F32), MASK_VALUE)
    rank = jnp.zeros(v.shape, F32)
    for j in range(n_rows):
        vj = v[j:j + 1, :]
        ge = jnp.where(vj >= v, 1.0, 0.0)
        gt = jnp.where(vj > v, 1.0, 0.0)
        rank = rank + jnp.where(n_idx > j, ge, gt)
    return ((rank < TOP_N) & allowed).astype(F32)


def _flash_init(m_s, l_s, acc_s):
    m_s[...] = jnp.full(m_s.shape, MASK_VALUE, F32)
    l_s[...] = jnp.zeros(l_s.shape, F32)
    acc_s[...] = jnp.zeros(acc_s.shape, F32)


def _flash_step(s, v_c, m_s, l_s, acc_s):
    m_prev = m_s[...]
    m_next = jnp.maximum(m_prev, jnp.max(s, axis=1, keepdims=True))
    alpha = jnp.exp(m_prev - m_next)
    p = jnp.exp(s - _tile_lanes(m_next, s.shape[1] // LANES))
    l_s[...] = alpha * l_s[...] + jnp.sum(p, axis=1, keepdims=True)
    acc_s[...] = acc_s[...] * alpha + _mm(p, v_c)
    m_s[...] = m_next


def _merge_heads(o_sum, kvh, tq):
    lane = lax.broadcasted_iota(jnp.int32, (tq, LANES), 1)
    slabs = []
    for pair in range(GROUP // 2):
        halves = []
        for par in range(2):
            o = o_sum[(2 * pair + par) * tq:(2 * pair + par + 1) * tq]
            halves.append(o if par == kvh else pltpu.roll(o, HEAD_DIM, 1))
        slabs.append(jnp.where(lane < HEAD_DIM, halves[0], halves[1]))
    return slabs


def _nsa_prompt_kernel(q_ref, qr_ref, kvb_ref, ckr_ref, cvr_ref, sm_ref, zn_ref,
                       wk1_ref, wk2_ref, pek_ref, wv1_ref, wv2_ref, pev_ref, ovt_ref, e_ref,
                       o_ref, ck_s, cv_s, m_s, l_s, acc_s, osum_s):
    i = pl.program_id(1)

    @pl.when(i == 0)
    def _():
        ck_s[...] = _compress(ckr_ref[0], wk1_ref, wk2_ref, pek_ref).astype(BF16)
        cv_s[...] = _compress(cvr_ref[0], wv1_ref, wv2_ref, pev_ref).astype(BF16)

    t0 = i * TQ
    row = lax.broadcasted_iota(jnp.int32, (TQ, LANES), 0)
    col = lax.broadcasted_iota(jnp.int32, (TQ, LANES), 1)
    q_pos = t0 + row
    cvalid = (CMP_STRIDE * col + (CMP_LEN - 1) <= q_pos) & (col < N_CMP)
    sig = _sigmoid(sm_ref[...])
    eye = _eye(TQ, BF16)
    ns_rows = 32
    n_idx = lax.broadcasted_iota(jnp.int32, (ns_rows, TQ), 0)
    qb_t = (t0 + lax.broadcasted_iota(jnp.int32, (ns_rows, TQ), 1)) // SEL_BLOCK
    col_s = lax.broadcasted_iota(jnp.int32, (TQ, CK_SLC), 1)
    row_s = lax.broadcasted_iota(jnp.int32, (TQ, CK_SLC), 0)

    for kvh in range(KV_HEADS):
        heads = range(kvh * GROUP, (kvh + 1) * GROUP)
        q_raw = jnp.concatenate([q_ref[:, h * LANES:(h + 1) * LANES] for h in heads], axis=0)
        q_rot = jnp.concatenate([qr_ref[:, h * LANES:(h + 1) * LANES] for h in heads], axis=0)

        def gate(branch):
            cols = [sig[:, branch * Q_HEADS + h:branch * Q_HEADS + h + 1] for h in heads]
            return jnp.concatenate([jnp.broadcast_to(c, (TQ, LANES)) for c in cols], axis=0)

        o_cmp, p = _cmp_attention(q_raw, ck_s[...], cv_s[...], cvalid, GROUP)
        osum_s[...] = gate(0) * o_cmp
        p_sum = p[0:TQ] + p[TQ:2 * TQ] + p[2 * TQ:3 * TQ] + p[3 * TQ:4 * TQ]
        imp_t = _mm_exact_lhs(ovt_ref[...], p_sum, _NT)[0:ns_rows]
        sel_t = _select_blocks(imp_t, n_idx, qb_t, ns_rows)
        sel_t = jnp.concatenate([sel_t, jnp.zeros((LANES - ns_rows, TQ), F32)], axis=0).astype(BF16)
        sel = _mm(eye, sel_t, _NT).astype(BF16)

        _flash_init(m_s, l_s, acc_s)

        def slc_body(j, carry):
            k0 = pl.multiple_of(j * CK_SLC, CK_SLC)
            k_c = kvb_ref[pl.ds(k0, CK_SLC), 0:LANES]
            v_c = kvb_ref[pl.ds(k0, CK_SLC), LANES:2 * LANES]
            s = _mm(q_rot, k_c, _NT)
            picked = _mm(sel, e_ref[j])
            ok = (picked > 0.5) & (k0 + col_s <= t0 + row_s)
            s = s + _tile_rows(jnp.where(ok, 0.0, MASK_VALUE), GROUP)
            _flash_step(s, v_c, m_s, l_s, acc_s)
            return carry

        lax.fori_loop(0, (t0 + TQ + CK_SLC - 1) // CK_SLC, slc_body, 0)
        osum_s[...] += gate(1) * (acc_s[...] / l_s[...])

        _flash_init(m_s, l_s, acc_s)
        n_back = WINDOW // CK_WIN

        def win_body(j, carry):
            ks = pl.multiple_of(t0 + (j - n_back) * CK_WIN, CK_WIN)
            k_c = kvb_ref[pl.ds(ks, CK_WIN), 2 * LANES:3 * LANES]
            v_c = kvb_ref[pl.ds(ks, CK_WIN), 3 * LANES:4 * LANES]
            s = _mm(q_rot, k_c, _NT)
            diff = q_pos - (ks + col)
            ok = (diff >= 0) & (diff <= WINDOW)
            s = s + _tile_rows(jnp.where(ok, 0.0, MASK_VALUE), GROUP)
            _flash_step(s, v_c, m_s, l_s, acc_s)
            return carry

        lax.fori_loop(jnp.maximum(n_back - i * (TQ // CK_WIN), 0), n_back + TQ // CK_WIN, win_body, 0)
        o_all = osum_s[...] + gate(2) * (acc_s[...] / l_s[...])

        for pair, slab in enumerate(_merge_heads(o_all, kvh, TQ)):
            lo = (kvh * (GROUP // 2) + pair) * LANES
            o_ref[:, lo:lo + LANES] = (slab * _silu(zn_ref[:, lo:lo + LANES])).astype(o_ref.dtype)


def _nsa_prompt(q, qr, kvb, ck, cv, sm, zn, cmp_k_w, cmp_v_w, batch, seq):
    nt = seq // TQ
    ns = seq // SEL_BLOCK
    sub_w = CMP_STRIDE * KV_WIDTH
    keys = np.arange(seq)
    e = (np.arange(LANES)[None, :, None] == (keys // SEL_BLOCK).reshape(seq // CK_SLC, 1, CK_SLC))
    e = jnp.asarray(e, BF16)
    tile = lambda b, i: (b * nt + i, 0)
    per_b = lambda b, i: (b, 0)
    per_b3 = lambda b, i: (b, 0, 0)
    fix2 = lambda b, i: (0, 0)
    fix3 = lambda b, i: (0, 0, 0)
    wspecs = [pl.BlockSpec((sub_w, 4 * CMP_HIDDEN), fix2),
              pl.BlockSpec((2 * CMP_HIDDEN, KV_WIDTH), fix2),
              pl.BlockSpec((SUBLANES, sub_w), fix2)]
    rows = GROUP * TQ
    return pl.pallas_call(
        _nsa_prompt_kernel,
        grid=(batch, nt),
        in_specs=[pl.BlockSpec((TQ, QPAD_WIDTH), tile),
                  pl.BlockSpec((TQ, QPAD_WIDTH), tile),
                  pl.BlockSpec((seq, 4 * KV_WIDTH), per_b),
                  pl.BlockSpec((1, seq // CMP_STRIDE, sub_w), per_b3),
                  pl.BlockSpec((1, seq // CMP_STRIDE, sub_w), per_b3),
                  pl.BlockSpec((TQ, LANES), tile),
                  pl.BlockSpec((TQ, NSA_WIDTH), tile)] + wspecs + wspecs + [
                  pl.BlockSpec((LANES, LANES), fix2),
                  pl.BlockSpec((seq // CK_SLC, LANES, CK_SLC), fix3)],
        out_specs=pl.BlockSpec((TQ, NSA_WIDTH), tile),
        out_shape=jax.ShapeDtypeStruct((batch * seq, NSA_WIDTH), BF16),
        scratch_shapes=[pltpu.VMEM((N_SUB, KV_WIDTH), BF16), pltpu.VMEM((N_SUB, KV_WIDTH), BF16),
                        pltpu.VMEM((rows, LANES), F32), pltpu.VMEM((rows, LANES), F32),
                        pltpu.VMEM((rows, LANES), F32), pltpu.VMEM((rows, LANES), F32)],
        compiler_params=pltpu.CompilerParams(dimension_semantics=("arbitrary", "arbitrary"),
                                             vmem_limit_bytes=VMEM_LIMIT_BYTES),
        name="nsa_prompt",
    )(q, qr, kvb, ck.reshape(batch, seq // CMP_STRIDE, sub_w), cv.reshape(batch, seq // CMP_STRIDE, sub_w),
      sm, zn, *cmp_k_w, *cmp_v_w, _overlap_t(ns), e)


def _softmax_rows(s):
    m = jnp.max(s, axis=1, keepdims=True)
    e = jnp.exp(s - m)
    return e / jnp.sum(e, axis=1, keepdims=True)


def _nsa_sample_kernel(pt_ref, q_ref, qr_ref, skn_ref, svn_ref, wkn_ref, wvn_ref, sm_ref, zn_ref,
                       wkc_ref, wvc_ref, *rest, n_pages, page, past, n_new, win_keys):
    del pt_ref
    cmpk, cmpv = rest[0:n_pages], rest[n_pages:2 * n_pages]
    slck, slcv = rest[2 * n_pages:3 * n_pages], rest[3 * n_pages:4 * n_pages]
    wk1_ref, wk2_ref, pek_ref, wv1_ref, wv2_ref, pev_ref, ovt_ref, e_ref = rest[4 * n_pages:4 * n_pages + 8]
    o_ref, wko_ref, wvo_ref = rest[4 * n_pages + 8:4 * n_pages + 11]
    kall, vall, kwin, vwin = rest[4 * n_pages + 11:]
    nr = SAMPLE_ROWS
    rows = Q_HEADS * nr
    all_keys = kall.shape[0]
    win_buf = wkc_ref.shape[1]
    zeros8 = jnp.zeros((nr, LANES), F32)

    for p in range(n_pages):
        kall[p * page:(p + 1) * page, :] = slck[p][0].astype(BF16)
        vall[p * page:(p + 1) * page, :] = slcv[p][0].astype(BF16)
    kall[past:past + 2 * nr, :] = jnp.concatenate([skn_ref[...], zeros8], axis=0).astype(BF16)
    vall[past:past + 2 * nr, :] = jnp.concatenate([svn_ref[...], zeros8], axis=0).astype(BF16)
    kall[past + 2 * nr:, :] = jnp.zeros((all_keys - past - 2 * nr, LANES), BF16)
    vall[past + 2 * nr:, :] = jnp.zeros((all_keys - past - 2 * nr, LANES), BF16)
    kwin[0:win_buf, :] = wkc_ref[0].astype(BF16)
    vwin[0:win_buf, :] = wvc_ref[0].astype(BF16)
    kwin[win_buf:win_buf + 2 * nr, :] = jnp.concatenate([wkn_ref[...], zeros8], axis=0).astype(BF16)
    vwin[win_buf:win_buf + 2 * nr, :] = jnp.concatenate([wvn_ref[...], zeros8], axis=0).astype(BF16)
    kwin[win_buf + 2 * nr:, :] = jnp.zeros((win_keys - win_buf - 2 * nr, LANES), BF16)
    vwin[win_buf + 2 * nr:, :] = jnp.zeros((win_keys - win_buf - 2 * nr, LANES), BF16)

    ck = _compress(jnp.concatenate([r[0] for r in cmpk], axis=0), wk1_ref, wk2_ref, pek_ref)
    cv = _compress(jnp.concatenate([r[0] for r in cmpv], axis=0), wv1_ref, wv2_ref, pev_ref)

    q_raw = jnp.concatenate([q_ref[:, h * LANES:(h + 1) * LANES] for h in range(Q_HEADS)], axis=0)
    q_rot = jnp.concatenate([qr_ref[:, h * LANES:(h + 1) * LANES] for h in range(Q_HEADS)], axis=0)

    tok = lax.broadcasted_iota(jnp.int32, (nr, LANES), 0)
    col = lax.broadcasted_iota(jnp.int32, (nr, LANES), 1)
    cvalid = (CMP_STRIDE * col + (CMP_LEN - 1) <= past + tok) & (col < N_CMP)
    o_cmp, p = _cmp_attention(q_raw, ck, cv, cvalid, Q_HEADS)

    p_sum = []
    for kvh in range(KV_HEADS):
        base = kvh * GROUP * nr
        p_sum.append(sum(p[base + g * nr:base + (g + 1) * nr] for g in range(GROUP)))
    p_sum = jnp.concatenate(p_sum, axis=0)
    ns = (past + n_new + SEL_BLOCK - 1) // SEL_BLOCK
    ns_rows = -(-ns // SUBLANES) * SUBLANES
    nq = KV_HEADS * nr
    imp_t = _mm_exact_lhs(ovt_ref[...], p_sum, _NT)[0:ns_rows]
    n_idx = lax.broadcasted_iota(jnp.int32, (ns_rows, nq), 0)
    qb_t = (past + lax.broadcasted_iota(jnp.int32, (ns_rows, nq), 1) % nr) // SEL_BLOCK
    sel_t = _select_blocks(imp_t, n_idx, qb_t, ns)
    sel_t = jnp.concatenate([sel_t, jnp.zeros((LANES - ns_rows, nq), F32)], axis=0)
    sel = _mm(_eye(nq, BF16), sel_t, _NT)
    sel_rows = jnp.concatenate([_tile_rows(sel[kvh * nr:(kvh + 1) * nr], GROUP) for kvh in range(KV_HEADS)], axis=0)

    picked = _mm(sel_rows, e_ref[...])
    q_pos = past + lax.broadcasted_iota(jnp.int32, (rows, all_keys), 0) % nr
    k_pos = lax.broadcasted_iota(jnp.int32, (rows, all_keys), 1)
    ok = (picked > 0.5) & (k_pos <= q_pos)
    s = _mm(q_rot, kall[...], _NT) + jnp.where(ok, 0.0, MASK_VALUE)
    o_slc = _mm(_softmax_rows(s), vall[...])

    q_pos = past + lax.broadcasted_iota(jnp.int32, (rows, win_keys), 0) % nr
    k_pos = past - win_buf + lax.broadcasted_iota(jnp.int32, (rows, win_keys), 1)
    diff = q_pos - k_pos
    ok = (diff >= 0) & (diff <= WINDOW)
    s = _mm(q_rot, kwin[...], _NT) + jnp.where(ok, 0.0, MASK_VALUE)
    o_win = _mm(_softmax_rows(s), vwin[...])

    sig = _sigmoid(sm_ref[...])

    def gate(branch):
        cols = [sig[:, branch * Q_HEADS + h:branch * Q_HEADS + h + 1] for h in range(Q_HEADS)]
        return jnp.concatenate([jnp.broadcast_to(c, (nr, LANES)) for c in cols], axis=0)

    o_all = gate(0) * o_cmp + gate(1) * o_slc + gate(2) * o_win
    for kvh in range(KV_HEADS):
        o_kv = o_all[kvh * GROUP * nr:(kvh + 1) * GROUP * nr]
        for pair, slab in enumerate(_merge_heads(o_kv, kvh, nr)):
            lo = (kvh * (GROUP // 2) + pair) * LANES
            o_ref[:, lo:lo + LANES] = (slab * _silu(zn_ref[:, lo:lo + LANES])).astype(o_ref.dtype)

    for cache_ref, new_ref, out_ref in ((wkc_ref, wkn_ref, wko_ref), (wvc_ref, wvn_ref, wvo_ref)):
        shifted = pltpu.roll(cache_ref[0], win_buf - n_new, 0)
        out_ref[0] = shifted
        tail = jnp.where(tok < nr - n_new, shifted[win_buf - nr:], pltpu.roll(new_ref[...], nr - n_new, 0))
        out_ref[0, win_buf - nr:, :] = tail


def _nsa_sample(q, qr, sk, sv, wk, wv, sm, zn, win_k, win_v, cmp_k, cmp_v, slc_k, slc_v, page_table,
                cmp_k_w, cmp_v_w, n_new):
    batch, n_pages = page_table.shape
    n_pool, page = cmp_k.shape[0], cmp_k.shape[1]
    past = n_pages * page
    nr = SAMPLE_ROWS
    sub_w = CMP_STRIDE * KV_WIDTH
    win_buf = win_k.shape[1]
    all_keys = -(-(past + 2 * nr) // LANES) * LANES
    win_keys = -(-(win_buf + 2 * nr) // LANES) * LANES
    ns = (past + n_new + SEL_BLOCK - 1) // SEL_BLOCK
    e = jnp.asarray(np.arange(LANES)[:, None] == (np.arange(all_keys) // SEL_BLOCK)[None, :], BF16)
    cmp_k = cmp_k.reshape(n_pool, page // CMP_STRIDE, sub_w)
    cmp_v = cmp_v.reshape(n_pool, page // CMP_STRIDE, sub_w)
    slc_k = slc_k.reshape(n_pool, page, KV_WIDTH)
    slc_v = slc_v.reshape(n_pool, page, KV_WIDTH)
    win_k = win_k.reshape(batch, win_buf, KV_WIDTH)
    win_v = win_v.reshape(batch, win_buf, KV_WIDTH)

    row = lambda b, pt: (b, 0)
    per_b3 = lambda b, pt: (b, 0, 0)
    fix2 = lambda b, pt: (0, 0)
    page_map = lambda p: (lambda b, pt: (pt[b, p], 0, 0))
    wspecs = [pl.BlockSpec((sub_w, 4 * CMP_HIDDEN), fix2),
              pl.BlockSpec((2 * CMP_HIDDEN, KV_WIDTH), fix2),
              pl.BlockSpec((SUBLANES, sub_w), fix2)]
    in_specs = ([pl.BlockSpec((nr, QPAD_WIDTH), row)] * 2 + [pl.BlockSpec((nr, KV_WIDTH), row)] * 4
                + [pl.BlockSpec((nr, LANES), row), pl.BlockSpec((nr, NSA_WIDTH), row)]
                + [pl.BlockSpec((1, win_buf, KV_WIDTH), per_b3)] * 2
                + [pl.BlockSpec((1, page // CMP_STRIDE, sub_w), page_map(p)) for p in range(n_pages)] * 1
                + [pl.BlockSpec((1, page // CMP_STRIDE, sub_w), page_map(p)) for p in range(n_pages)]
                + [pl.BlockSpec((1, page, KV_WIDTH), page_map(p)) for p in range(n_pages)]
                + [pl.BlockSpec((1, page, KV_WIDTH), page_map(p)) for p in range(n_pages)]
                + wspecs + wspecs
                + [pl.BlockSpec((LANES, LANES), fix2), pl.BlockSpec((LANES, all_keys), fix2)])
    grid_spec = pltpu.PrefetchScalarGridSpec(
        num_scalar_prefetch=1,
        grid=(batch,),
        in_specs=in_specs,
        out_specs=[pl.BlockSpec((nr, NSA_WIDTH), row),
                   pl.BlockSpec((1, win_buf, KV_WIDTH), per_b3),
                   pl.BlockSpec((1, win_buf, KV_WIDTH), per_b3)],
        scratch_shapes=[pltpu.VMEM((all_keys, KV_WIDTH), BF16), pltpu.VMEM((all_keys, KV_WIDTH), BF16),
                        pltpu.VMEM((win_keys, KV_WIDTH), BF16), pltpu.VMEM((win_keys, KV_WIDTH), BF16)])
    return pl.pallas_call(
        functools.partial(_nsa_sample_kernel, n_pages=n_pages, page=page, past=past, n_new=n_new,
                          win_keys=win_keys),
        grid_spec=grid_spec,
        out_shape=[jax.ShapeDtypeStruct((batch * nr, NSA_WIDTH), F32),
                   jax.ShapeDtypeStruct((batch, win_buf, KV_WIDTH), F32),
                   jax.ShapeDtypeStruct((batch, win_buf, KV_WIDTH), F32)],
        compiler_params=pltpu.CompilerParams(dimension_semantics=("arbitrary",),
                                             vmem_limit_bytes=VMEM_LIMIT_BYTES),
        name="nsa_sample",
    )(page_table, q, qr, sk, sv, wk, wv, sm, zn, win_k, win_v,
      *([cmp_k] * n_pages), *([cmp_v] * n_pages), *([slc_k] * n_pages), *([slc_v] * n_pages),
      *cmp_k_w, *cmp_v_w, _overlap_t(ns), e)


def _unit_lower_inverse(a, n_valid, hi):
    c = a.shape[0]
    inv = _eye(c, F32) - a
    power = a
    span = 2
    while span < n_valid:
        power = _mm(power, power, hi=hi)
        inv = inv + _mm(inv, power, hi=hi)
        span *= 2
    return inv


def _gdn_kernel(qkv_ref, sm_ref, zg_ref, conv0_ref, s0_ref, wc_ref, vec_ref, wg_ref,
                go_ref, xp_out_ref, st_ref, xp_s, *, chunk, n_valid, hi):
    c_idx = pl.program_id(1)

    @pl.when(c_idx == 0)
    def _():
        xp_s[0:SUBLANES, :] = conv0_ref[0]
        st_ref[...] = s0_ref[...]

    xp_s[SUBLANES:SUBLANES + chunk, :] = qkv_ref[...]
    wc = wc_ref[...]
    y = xp_s[SUBLANES:SUBLANES + chunk, :] * wc[GDN_CONV - 1:GDN_CONV, :]
    for j in range(GDN_CONV - 1):
        lo = SUBLANES - (GDN_CONV - 1) + j
        y = y + xp_s[lo:lo + chunk, :] * wc[j:j + 1, :]
    xp_out_ref[0] = xp_s[...]
    xp_s[0:SUBLANES, :] = xp_s[chunk:chunk + SUBLANES, :]
    act = _silu(y)

    row1 = lax.broadcasted_iota(jnp.int32, (chunk, LANES), 0)
    small = sm_ref[...]
    vec = vec_ref[...]
    beta_all = _sigmoid(small)
    z = small + vec[1:2, :]
    softplus = jnp.maximum(z, 0.0) + jnp.log1p(jnp.exp(-jnp.abs(z)))
    g_all = -jnp.exp(vec[0:1, :]) * softplus
    if n_valid < chunk:
        valid = row1 < n_valid
        act = act * _tile_lanes(valid.astype(F32), GDN_CONV_CH // LANES)
        g_all = jnp.where(valid, g_all, 0.0)

    rr = lax.broadcasted_iota(jnp.int32, (chunk, chunk), 0)
    cc = lax.broadcasted_iota(jnp.int32, (chunk, chunk), 1)
    tri = rr >= cc
    decay = _mm_exact_lhs(tri.astype(BF16), g_all)
    decay_t = _mm_exact_lhs(_eye(LANES, BF16), decay, _NT)
    e_dec = jnp.exp(decay)

    for h in range(GDN_HEADS):
        qh = act[:, h * GDN_DK:(h + 1) * GDN_DK]
        kh = act[:, GDN_WIDTH + h * GDN_DK:GDN_WIDTH + (h + 1) * GDN_DK]
        vh = act[:, 2 * GDN_WIDTH + h * GDN_DV:2 * GDN_WIDTH + (h + 1) * GDN_DV]
        qh = qh * lax.rsqrt(jnp.sum(qh * qh, axis=-1, keepdims=True) + NORM_EPS) * (GDN_DK ** -0.5)
        kh = kh * lax.rsqrt(jnp.sum(kh * kh, axis=-1, keepdims=True) + NORM_EPS)
        beta = beta_all[:, SM_B + h:SM_B + h + 1]
        dcol = decay[:, SM_A + h:SM_A + h + 1]
        drow = decay_t[SM_A + h:SM_A + h + 1, :]
        ed = e_dec[:, SM_A + h:SM_A + h + 1]
        dlast = decay[chunk - 1:chunk, SM_A + h:SM_A + h + 1]
        dmask = jnp.where(tri, jnp.exp(jnp.where(tri, dcol - drow, 0.0)), 0.0)
        kb = kh * beta
        a_mat = jnp.where(rr > cc, _mm(kb, kh, _NT, hi=hi) * dmask, 0.0)
        inv = _unit_lower_inverse(a_mat, n_valid, hi)
        u = _mm(inv, vh * beta, hi=hi)
        w = _mm(inv, kb * ed, hi=hi)
        qk = _mm(qh, kh, _NT, hi=hi) * dmask
        qd = qh * ed
        kd = kh * jnp.exp(dlast - dcol)
        st = st_ref[0, h]
        v_new = u - _mm(w, st, hi=hi)
        o = _mm(qd, st, hi=hi) + _mm(qk, v_new, hi=hi)
        st_ref[0, h] = st * jnp.exp(dlast) + _mm(kd, v_new, _TN, hi=hi)
        o = o * lax.rsqrt(jnp.mean(o * o, axis=-1, keepdims=True) + NORM_EPS) * wg_ref[...]
        o = o * _silu(zg_ref[:, h * GDN_DV:(h + 1) * GDN_DV])
        go_ref[:, h * GDN_DV:(h + 1) * GDN_DV] = o.astype(go_ref.dtype)


def _gdn(qkv, sm, zg, conv0, s0, w_conv, a_log, dt_bias, w_gnorm, batch, rows, chunk, n_valid, hi, out_dtype):
    nc = rows // chunk
    tile = lambda b, c: (b * nc + c, 0)
    per_b3 = lambda b, c: (b, 0, 0)
    per_b4 = lambda b, c: (b, 0, 0, 0)
    fix2 = lambda b, c: (0, 0)
    wc = jnp.concatenate([w_conv, jnp.zeros((SUBLANES - GDN_CONV, GDN_CONV_CH), w_conv.dtype)], axis=0)
    vec = jnp.zeros((SUBLANES, LANES), F32)
    vec = vec.at[0, SM_A:SM_A + GDN_HEADS].set(a_log).at[1, SM_A:SM_A + GDN_HEADS].set(dt_bias)
    return pl.pallas_call(
        functools.partial(_gdn_kernel, chunk=chunk, n_valid=n_valid, hi=hi),
        grid=(batch, nc),
        in_specs=[pl.BlockSpec((chunk, GDN_CONV_CH), tile),
                  pl.BlockSpec((chunk, LANES), tile),
                  pl.BlockSpec((chunk, GDN_WIDTH), tile),
                  pl.BlockSpec((1, SUBLANES, GDN_CONV_CH), per_b3),
                  pl.BlockSpec((1, GDN_HEADS, GDN_DK, GDN_DV), per_b4),
                  pl.BlockSpec((SUBLANES, GDN_CONV_CH), fix2),
                  pl.BlockSpec((SUBLANES, LANES), fix2),
                  pl.BlockSpec((1, GDN_DV), fix2)],
        out_specs=[pl.BlockSpec((chunk, GDN_WIDTH), tile),
                   pl.BlockSpec((1, SUBLANES + chunk, GDN_CONV_CH), per_b3),
                   pl.BlockSpec((1, GDN_HEADS, GDN_DK, GDN_DV), per_b4)],
        out_shape=[jax.ShapeDtypeStruct((batch * rows, GDN_WIDTH), out_dtype),
                   jax.ShapeDtypeStruct((batch, SUBLANES + chunk, GDN_CONV_CH), F32),
                   jax.ShapeDtypeStruct((batch, GDN_HEADS, GDN_DK, GDN_DV), F32)],
        scratch_shapes=[pltpu.VMEM((SUBLANES + chunk, GDN_CONV_CH), F32)],
        compiler_params=pltpu.CompilerParams(dimension_semantics=("arbitrary", "arbitrary"),
                                             vmem_limit_bytes=VMEM_LIMIT_BYTES),
        name="gdn",
    )(qkv, sm, zg, conv0, s0, wc, vec, w_gnorm.reshape(1, GDN_DV))


PROMPT_TM = 512
SAMPLE_TM = 128


def _layer_prompt(h, lw, final, w_final, win_buf):
    w_norm, w_pack, cmp_k_w, cmp_v_w, w_conv, a_log, dt_bias, w_gnorm, w_out = lw
    batch, seq, d = h.shape
    x2d = h.reshape(batch * seq, d)
    tables = _rope_tables(jnp.arange(seq, dtype=jnp.int32))
    (q, qr, ck, cv, sk, sv, wk, wv, kvb, zn, qkv, zg, sm) = _project(
        x2d, w_norm, w_pack.astype(BF16), tables, seq, PROMPT_TM, False, BF16)
    nsa = _nsa_prompt(q, qr, kvb, ck, cv, sm, zn, cmp_k_w, cmp_v_w, batch, seq)
    conv0 = jnp.zeros((batch, SUBLANES, GDN_CONV_CH), F32)
    s0 = jnp.zeros((batch, GDN_HEADS, GDN_DK, GDN_DV), F32)
    go, xp, st = _gdn(qkv, sm, zg, conv0, s0, w_conv, a_log, dt_bias, w_gnorm,
                      batch, seq, GDN_CHUNK, GDN_CHUNK, False, BF16)
    y = _out_project(x2d, nsa, go, w_out.astype(BF16), w_final, PROMPT_TM, False, final)
    kv4 = lambda t: t.reshape(batch, seq, KV_HEADS, HEAD_DIM)
    lead = ((0, 0), (max(win_buf - seq, 0), 0), (0, 0), (0, 0))
    win = lambda t: jnp.pad(kv4(t), lead)[:, -win_buf:]
    conv_new = xp[:, SUBLANES + GDN_CHUNK - (GDN_CONV - 1):SUBLANES + GDN_CHUNK]
    return y.reshape(batch, seq, d), (kv4(ck), kv4(cv), kv4(sk), kv4(sv), win(wk), win(wv), conv_new, st)


def _layer_sample(h8, n_new, caches, page_table, lw, final, w_final):
    w_norm, w_pack, cmp_k_w, cmp_v_w, w_conv, a_log, dt_bias, w_gnorm, w_out = lw
    c_cmp_k, c_cmp_v, c_slc_k, c_slc_v, c_win_k, c_win_v, s_conv, s_gdn = caches
    batch, nr, d = h8.shape
    past = page_table.shape[1] * c_cmp_k.shape[1]
    x2d = h8.reshape(batch * nr, d)
    tables = _rope_tables(past + jnp.arange(nr, dtype=jnp.int32))
    tables = tuple(jnp.tile(t, (SAMPLE_TM // nr, 1)) for t in tables)
    (q, qr, ck, cv, sk, sv, wk, wv, _, zn, qkv, zg, sm) = _project(
        x2d, w_norm, w_pack, tables, SAMPLE_TM, SAMPLE_TM, True, F32)
    nsa, win_k, win_v = _nsa_sample(q, qr, sk, sv, wk, wv, sm, zn, c_win_k, c_win_v,
                                    c_cmp_k, c_cmp_v, c_slc_k, c_slc_v, page_table, cmp_k_w, cmp_v_w, n_new)
    conv0 = jnp.pad(s_conv, ((0, 0), (SUBLANES - (GDN_CONV - 1), 0), (0, 0)))
    go, xp, st = _gdn(qkv, sm, zg, conv0, s_gdn, w_conv, a_log, dt_bias, w_gnorm,
                      batch, nr, nr, n_new, True, F32)
    y = _out_project(x2d, nsa, go, w_out, w_final, SAMPLE_TM, True, final)
    kv4 = lambda t: t.reshape(batch, nr, KV_HEADS, HEAD_DIM)[:, :n_new]
    win4 = lambda t: t.reshape(batch, -1, KV_HEADS, HEAD_DIM)
    conv_new = xp[:, SUBLANES + n_new - (GDN_CONV - 1):SUBLANES + n_new]
    return y.reshape(batch, nr, d), (kv4(ck), kv4(cv), kv4(sk), kv4(sv), win4(win_k), win4(win_v), conv_new, st)


def kernel(x_prompt, x_sample, cache_cmp_k, cache_cmp_v, cache_slc_k, cache_slc_v, cache_win_k, cache_win_v, state_conv, state_gdn, page_table, w_norm, w_in, pe_cmp_k, w_cmp_k1, w_cmp_k2, pe_cmp_v, w_cmp_v1, w_cmp_v2, w_conv, a_log, dt_bias, w_gdn_norm, w_out, w_final_norm):
    depth = w_in.shape[0]
    n_new = x_sample.shape[1]
    win_buf = cache_win_k.shape[2]
    h_p = x_prompt
    h_s = jnp.pad(x_sample, ((0, 0), (0, SAMPLE_ROWS - n_new), (0, 0)))
    st_p, st_s = [], []
    for layer in range(depth):
        lw = (w_norm[layer], _pack_w_in(w_in[layer]),
              _compress_weights(pe_cmp_k[layer], w_cmp_k1[layer], w_cmp_k2[layer]),
              _compress_weights(pe_cmp_v[layer], w_cmp_v1[layer], w_cmp_v2[layer]),
              w_conv[layer], a_log[layer], dt_bias[layer], w_gdn_norm[layer], w_out[layer])
        final = layer == depth - 1
        h_p, sp = _layer_prompt(h_p, lw, final, w_final_norm, win_buf)
        caches = (cache_cmp_k[layer], cache_cmp_v[layer], cache_slc_k[layer], cache_slc_v[layer],
                  cache_win_k[layer], cache_win_v[layer], state_conv[layer], state_gdn[layer])
        h_s, ss = _layer_sample(h_s, n_new, caches, page_table, lw, final, w_final_norm)
        st_p.append(sp)
        st_s.append(ss)
    outs = [h_p, h_s[:, :n_new]]
    for i in range(8):
        outs.append(jnp.stack([s[i] for s in st_p]))
        outs.append(jnp.stack([s[i] for s in st_s]))
    return tuple(outs)
```

```python
import functools

import numpy as np
import jax
import jax.numpy as jnp
from jax import lax
from jax.experimental import pallas as pl
from jax.experimental.pallas import tpu as pltpu

F32 = jnp.float32
BF16 = jnp.bfloat16

D_MODEL = 1024
HEAD_DIM = 64
Q_HEADS = 8
KV_HEADS = 2
GROUP = Q_HEADS // KV_HEADS
NSA_WIDTH = Q_HEADS * HEAD_DIM
KV_WIDTH = KV_HEADS * HEAD_DIM
CMP_LEN = 32
CMP_STRIDE = 16
CMP_HIDDEN = 128
SEL_BLOCK = 64
TOP_N = 8
WINDOW = 512
FORCE_BONUS = 1.0e4
ROT_DIM = HEAD_DIM // 4
ROPE_THETA = 500000.0
GDN_DK = 128
GDN_DV = 128
GDN_HEADS = 4
GDN_WIDTH = GDN_HEADS * GDN_DV
GDN_CONV = 4
GDN_CONV_CH = 3 * GDN_WIDTH
GDN_CHUNK = 64
NORM_EPS = 1e-6
MASK_VALUE = -1e30

LANES = 128
SUBLANES = 8
VMEM_LIMIT_BYTES = 56 * 1024 * 1024

QPAD_WIDTH = Q_HEADS * LANES
C_Q = 0
C_KV = C_Q + QPAD_WIDTH
C_ZN = C_KV + 6 * KV_WIDTH
C_QKV = C_ZN + NSA_WIDTH
C_ZG = C_QKV + GDN_CONV_CH
C_SM = C_ZG + GDN_WIDTH
N_PACK = C_SM + LANES
SM_B = 3 * Q_HEADS
SM_A = SM_B + GDN_HEADS

TQ = 128
CK_SLC = 256
CK_WIN = 128
N_SUB = 128
N_CMP = N_SUB - CMP_LEN // CMP_STRIDE + 1
SAMPLE_ROWS = 8


def _pack_w_in(w_in):
    o_gate = NSA_WIDTH + 6 * KV_WIDTH
    o_zn = o_gate + 3 * Q_HEADS
    o_qkv = o_zn + NSA_WIDTH
    o_b = o_qkv + GDN_CONV_CH
    o_a = o_b + GDN_HEADS
    o_zg = o_a + GDN_HEADS
    d = w_in.shape[0]
    z64 = jnp.zeros((d, HEAD_DIM), w_in.dtype)
    qcols = []
    for hq in range(Q_HEADS):
        wq = w_in[:, hq * HEAD_DIM:(hq + 1) * HEAD_DIM]
        qcols += [wq, z64] if hq // GROUP == 0 else [z64, wq]
    pad = jnp.zeros((d, LANES - SM_A - GDN_HEADS), w_in.dtype)
    return jnp.concatenate(
        qcols + [w_in[:, NSA_WIDTH:o_gate], w_in[:, o_zn:o_qkv], w_in[:, o_qkv:o_b], w_in[:, o_zg:],
                 w_in[:, o_gate:o_zn], w_in[:, o_b:o_a], w_in[:, o_a:o_zg], pad], axis=1)


def _rope_tables(pos):
    half = ROT_DIM // 2
    inv = ROPE_THETA ** (-(jnp.arange(half, dtype=F32) * 2.0 / ROT_DIM))
    ang = pos.astype(F32)[:, None] * inv[None, :]
    cos, sin = jnp.cos(ang), jnp.sin(ang)
    n = pos.shape[0]
    one = jnp.ones((n, HEAD_DIM - ROT_DIM), F32)
    zero = jnp.zeros((n, HEAD_DIM - ROT_DIM), F32)
    zh = jnp.zeros((n, half), F32)
    c64 = jnp.concatenate([cos, cos, one], axis=1)
    a64 = jnp.concatenate([zh, sin, zero], axis=1)
    b64 = jnp.concatenate([-sin, zh, zero], axis=1)
    tile = lambda t: jnp.concatenate([t, t], axis=1)
    return tile(c64), tile(a64), tile(b64)


def _rope128(x, c, a, b):
    half = ROT_DIM // 2
    return x * c + pltpu.roll(x, half, 1) * a + pltpu.roll(x, LANES - half, 1) * b


_NN = (((1,), (0,)), ((), ()))
_NT = (((1,), (1,)), ((), ()))
_TN = (((0,), (0,)), ((), ()))


def _mm(a, b, dims=_NN, hi=False):
    if hi:
        return lax.dot_general(a.astype(F32), b.astype(F32), dims, precision=lax.Precision.HIGHEST,
                               preferred_element_type=F32)
    return lax.dot_general(a.astype(BF16), b.astype(BF16), dims, preferred_element_type=F32)


def _mm_exact_lhs(a_bf16, x, dims=_NN):
    x1 = x.astype(BF16)
    r1 = x - x1.astype(F32)
    x2 = r1.astype(BF16)
    x3 = (r1 - x2.astype(F32)).astype(BF16)
    dot = lambda t: lax.dot_general(a_bf16, t, dims, preferred_element_type=F32)
    return dot(x1) + dot(x2) + dot(x3)


def _sigmoid(x):
    return 1.0 / (1.0 + jnp.exp(-x))


def _silu(x):
    return x * _sigmoid(x)


def _tile_rows(x, n):
    return jnp.concatenate([x] * n, axis=0)


def _tile_lanes(x, n):
    return x if n == 1 else jnp.concatenate([x] * n, axis=1)


def _eye(n, dtype):
    r = lax.broadcasted_iota(jnp.int32, (n, n), 0)
    c = lax.broadcasted_iota(jnp.int32, (n, n), 1)
    return (r == c).astype(dtype)


def _proj_kernel(x_ref, wn_ref, w_ref, c_ref, a_ref, b_ref,
                 q_ref, qr_ref, ck_ref, cv_ref, sk_ref, sv_ref, wk_ref, wv_ref, kvb_ref,
                 zn_ref, qkv_ref, zg_ref, sm_ref, *, hi):
    x = x_ref[...]
    ms = jnp.mean(x * x, axis=-1, keepdims=True)
    xn = x * lax.rsqrt(ms + NORM_EPS) * wn_ref[...]
    if not hi:
        xn = xn.astype(BF16)
    c, a, b = c_ref[...], a_ref[...], b_ref[...]
    scale = HEAD_DIM ** -0.5
    for j in range(Q_HEADS):
        qj = _mm(xn, w_ref[:, C_Q + j * LANES:C_Q + (j + 1) * LANES], hi=hi)
        q_ref[:, j * LANES:(j + 1) * LANES] = (qj * scale).astype(q_ref.dtype)
        qr_ref[:, j * LANES:(j + 1) * LANES] = (_rope128(qj, c, a, b) * scale).astype(qr_ref.dtype)
    kv_refs = (ck_ref, cv_ref, sk_ref, sv_ref, wk_ref, wv_ref)
    for j in range(6):
        kj = _mm(xn, w_ref[:, C_KV + j * LANES:C_KV + (j + 1) * LANES], hi=hi)
        if j in (2, 4):
            kj = _rope128(kj, c, a, b)
        kv_refs[j][...] = kj
        if j >= 2:
            kvb_ref[:, (j - 2) * LANES:(j - 1) * LANES] = kj.astype(BF16)
    zn_ref[...] = _mm(xn, w_ref[:, C_ZN:C_QKV], hi=hi)
    for j in range(3):
        lo = C_QKV + j * GDN_WIDTH
        qkv_ref[:, j * GDN_WIDTH:(j + 1) * GDN_WIDTH] = _mm(xn, w_ref[:, lo:lo + GDN_WIDTH], hi=hi)
    zg_ref[...] = _mm(xn, w_ref[:, C_ZG:C_SM], hi=hi)
    sm_ref[...] = _mm(xn, w_ref[:, C_SM:N_PACK], hi=hi)


def _project(x2d, w_norm, w_pack, tables, rows_per_seq, tm, hi, q_dtype):
    n = x2d.shape[0]
    nt = rows_per_seq // tm
    row = lambda i: (i, 0)
    tab = lambda i: (i % nt, 0)
    fix = lambda i: (0, 0)
    widths = (QPAD_WIDTH, QPAD_WIDTH) + (KV_WIDTH,) * 6 + (4 * KV_WIDTH, NSA_WIDTH, GDN_CONV_CH, GDN_WIDTH, LANES)
    dtypes = (q_dtype, q_dtype) + (F32,) * 6 + (BF16, F32, F32, F32, F32)
    return pl.pallas_call(
        functools.partial(_proj_kernel, hi=hi),
        grid=(n // tm,),
        in_specs=[pl.BlockSpec((tm, D_MODEL), row),
                  pl.BlockSpec((1, D_MODEL), fix),
                  pl.BlockSpec((D_MODEL, N_PACK), fix),
                  pl.BlockSpec((tm, LANES), tab),
                  pl.BlockSpec((tm, LANES), tab),
                  pl.BlockSpec((tm, LANES), tab)],
        out_specs=[pl.BlockSpec((tm, w), row) for w in widths],
        out_shape=[jax.ShapeDtypeStruct((n, w), d) for w, d in zip(widths, dtypes)],
        compiler_params=pltpu.CompilerParams(dimension_semantics=("arbitrary",),
                                             vmem_limit_bytes=VMEM_LIMIT_BYTES),
        name="in_proj",
    )(x2d, w_norm.reshape(1, D_MODEL), w_pack, *tables)


def _out_kernel(x_ref, nsa_ref, gdn_ref, w_ref, wf_ref, y_ref, *, hi, final):
    acc = (_mm(nsa_ref[...], w_ref[:NSA_WIDTH, :], hi=hi)
           + _mm(gdn_ref[...], w_ref[NSA_WIDTH:, :], hi=hi))
    h = x_ref[...] + acc
    if final:
        ms = jnp.mean(h * h, axis=-1, keepdims=True)
        h = h * lax.rsqrt(ms + NORM_EPS) * wf_ref[...]
    y_ref[...] = h


def _out_project(x2d, nsa, gdn, w_out, w_final, tm, hi, final):
    n = x2d.shape[0]
    row = lambda i: (i, 0)
    fix = lambda i: (0, 0)
    return pl.pallas_call(
        functools.partial(_out_kernel, hi=hi, final=final),
        grid=(n // tm,),
        in_specs=[pl.BlockSpec((tm, D_MODEL), row),
                  pl.BlockSpec((tm, NSA_WIDTH), row),
                  pl.BlockSpec((tm, GDN_WIDTH), row),
                  pl.BlockSpec((D_MODEL, D_MODEL), fix),
                  pl.BlockSpec((1, D_MODEL), fix)],
        out_specs=pl.BlockSpec((tm, D_MODEL), row),
        out_shape=jax.ShapeDtypeStruct((n, D_MODEL), F32),
        compiler_params=pltpu.CompilerParams(dimension_semantics=("arbitrary",),
                                             vmem_limit_bytes=VMEM_LIMIT_BYTES),
        name="out_proj",
    )(x2d, nsa, gdn, w_out, w_final.reshape(1, D_MODEL))


def _compress_weights(pe, w1, w2):
    half = CMP_STRIDE * HEAD_DIM
    z = jnp.zeros((CMP_STRIDE, HEAD_DIM, CMP_HIDDEN), w1.dtype)

    def place(wpart, h):
        wp = wpart.reshape(CMP_STRIDE, HEAD_DIM, CMP_HIDDEN)
        parts = [wp, z] if h == 0 else [z, wp]
        return jnp.stack(parts, axis=1).reshape(CMP_STRIDE * KV_WIDTH, CMP_HIDDEN)

    w1big = jnp.concatenate([place(w1[:half], 0), place(w1[:half], 1),
                             place(w1[half:], 0), place(w1[half:], 1)], axis=1)
    zz = jnp.zeros_like(w2)
    w2big = jnp.concatenate([jnp.concatenate([w2, zz], axis=1), jnp.concatenate([zz, w2], axis=1)], axis=0)
    pe_a = jnp.tile(pe[:CMP_STRIDE], (1, KV_HEADS)).reshape(1, CMP_STRIDE * KV_WIDTH)
    pe_b = jnp.tile(pe[CMP_STRIDE:], (1, KV_HEADS)).reshape(1, CMP_STRIDE * KV_WIDTH)
    pe2 = jnp.concatenate([pe_a, pe_b, jnp.zeros((SUBLANES - 2, CMP_STRIDE * KV_WIDTH), pe.dtype)], axis=0)
    return w1big.astype(BF16), w2big.astype(BF16), pe2


def _compress(sub_rows, w1_ref, w2_ref, pe_ref):
    hid = 2 * CMP_HIDDEN
    w1 = w1_ref[...]
    ab = _mm(sub_rows, w1)
    pe = pe_ref[...]
    pe_hi = pe.astype(BF16)
    pe_lo = (pe - pe_hi.astype(F32)).astype(BF16)
    r = _mm(pe_hi, w1) + _mm(pe_lo, w1)
    bias = r[0:1, :hid] + r[1:2, hid:]
    h = ab[:, :hid] + pltpu.roll(ab[:, hid:], N_SUB - 1, 0) + bias
    return _mm(_silu(h), w2_ref[...])


def _overlap_t(ns):
    c0 = np.arange(N_CMP)[None, :] * CMP_STRIDE
    b0 = np.arange(ns)[:, None] * SEL_BLOCK
    ov = np.minimum(c0 + CMP_LEN, b0 + SEL_BLOCK) - np.maximum(c0, b0)
    out = np.zeros((LANES, LANES), np.float32)
    out[:ns, :N_CMP] = np.maximum(ov, 0) / CMP_LEN
    return jnp.asarray(out, BF16)


def _cmp_attention(q_stack, ck, cv, cvalid, groups):
    s = _mm(q_stack, ck, _NT)
    cm = _tile_rows(cvalid, groups)
    s = jnp.where(cm, s, MASK_VALUE)
    m = jnp.max(s, axis=1, keepdims=True)
    e = jnp.exp(s - m)
    p = e / jnp.sum(e, axis=1, keepdims=True) * cm.astype(F32)
    return _mm(p, cv), p


def _select_blocks(imp_t, n_idx, q_blk, n_rows):
    forced = (n_idx == 0) | (n_idx == q_blk) | (n_idx == q_blk - 1)
    allowed = n_idx <= q_blk
    v = jnp.where(allowed, imp_t + FORCE_BONUS * forced.astype(F32), MASK_VALUE)
    rank = jnp.zeros(v.shape, F32)
    for j in range(n_rows):
        vj = v[j:j + 1, :]
        ge = jnp.where(vj >= v, 1.0, 0.0)
        gt = jnp.where(vj > v, 1.0, 0.0)
        rank = rank + jnp.where(n_idx > j, ge, gt)
    return ((rank < TOP_N) & allowed).astype(F32)


def _flash_init(m_s, l_s, acc_s):
    m_s[...] = jnp.full(m_s.shape, MASK_VALUE, F32)
    l_s[...] = jnp.zeros(l_s.shape, F32)
    acc_s[...] = jnp.zeros(acc_s.shape, F32)


def _flash_step(s, v_c, m_s, l_s, acc_s):
    m_prev = m_s[...]
    m_next = jnp.maximum(m_prev, jnp.max(s, axis=1, keepdims=True))
    alpha = jnp.exp(m_prev - m_next)
    p = jnp.exp(s - _tile_lanes(m_next, s.shape[1] // LANES))
    l_s[...] = alpha * l_s[...] + jnp.sum(p, axis=1, keepdims=True)
    acc_s[...] = acc_s[...] * alpha + _mm(p, v_c)
    m_s[...] = m_next


def _merge_heads(o_sum, kvh, tq):
    lane = lax.broadcasted_iota(jnp.int32, (tq, LANES), 1)
    slabs = []
    for pair in range(GROUP // 2):
        halves = []
        for par in range(2):
            o = o_sum[(2 * pair + par) * tq:(2 * pair + par + 1) * tq]
            halves.append(o if par == kvh else pltpu.roll(o, HEAD_DIM, 1))
        slabs.append(jnp.where(lane < HEAD_DIM, halves[0], halves[1]))
    return slabs


def _nsa_prompt_kernel(q_ref, qr_ref, kvb_ref, ckr_ref, cvr_ref, sm_ref, zn_ref,
                       wk1_ref, wk2_ref, pek_ref, wv1_ref, wv2_ref, pev_ref, ovt_ref, e_ref,
                       o_ref, ck_s, cv_s, m_s, l_s, acc_s, osum_s):
    i = pl.program_id(1)

    @pl.when(i == 0)
    def _():
        ck_s[...] = _compress(ckr_ref[0], wk1_ref, wk2_ref, pek_ref).astype(BF16)
        cv_s[...] = _compress(cvr_ref[0], wv1_ref, wv2_ref, pev_ref).astype(BF16)

    t0 = i * TQ
    row = lax.broadcasted_iota(jnp.int32, (TQ, LANES), 0)
    col = lax.broadcasted_iota(jnp.int32, (TQ, LANES), 1)
    q_pos = t0 + row
    cvalid = (CMP_STRIDE * col + (CMP_LEN - 1) <= q_pos) & (col < N_CMP)
    sig = _sigmoid(sm_ref[...])
    eye = _eye(TQ, BF16)
    ns_rows = 32
    n_idx = lax.broadcasted_iota(jnp.int32, (ns_rows, TQ), 0)
    qb_t = (t0 + lax.broadcasted_iota(jnp.int32, (ns_rows, TQ), 1)) // SEL_BLOCK
    col_s = lax.broadcasted_iota(jnp.int32, (TQ, CK_SLC), 1)
    row_s = lax.broadcasted_iota(jnp.int32, (TQ, CK_SLC), 0)

    for kvh in range(KV_HEADS):
        heads = range(kvh * GROUP, (kvh + 1) * GROUP)
        q_raw = jnp.concatenate([q_ref[:, h * LANES:(h + 1) * LANES] for h in heads], axis=0)
        q_rot = jnp.concatenate([qr_ref[:, h * LANES:(h + 1) * LANES] for h in heads], axis=0)

        def gate(branch):
            cols = [sig[:, branch * Q_HEADS + h:branch * Q_HEADS + h + 1] for h in heads]
            return jnp.concatenate([jnp.broadcast_to(c, (TQ, LANES)) for c in cols], axis=0)

        o_cmp, p = _cmp_attention(q_raw, ck_s[...], cv_s[...], cvalid, GROUP)
        osum_s[...] = gate(0) * o_cmp
        p_sum = p[0:TQ] + p[TQ:2 * TQ] + p[2 * TQ:3 * TQ] + p[3 * TQ:4 * TQ]
        imp_t = _mm_exact_lhs(ovt_ref[...], p_sum, _NT)[0:ns_rows]
        sel_t = _select_blocks(imp_t, n_idx, qb_t, ns_rows)
        sel_t = jnp.concatenate([sel_t, jnp.zeros((LANES - ns_rows, TQ), F32)], axis=0).astype(BF16)
        sel = _mm(eye, sel_t, _NT).astype(BF16)

        _flash_init(m_s, l_s, acc_s)

        def slc_body(j, carry):
            k0 = pl.multiple_of(j * CK_SLC, CK_SLC)
            k_c = kvb_ref[pl.ds(k0, CK_SLC), 0:LANES]
            v_c = kvb_ref[pl.ds(k0, CK_SLC), LANES:2 * LANES]
            s = _mm(q_rot, k_c, _NT)
            picked = _mm(sel, e_ref[j])
            ok = (picked > 0.5) & (k0 + col_s <= t0 + row_s)
            s = s + _tile_rows(jnp.where(ok, 0.0, MASK_VALUE), GROUP)
            _flash_step(s, v_c, m_s, l_s, acc_s)
            return carry

        lax.fori_loop(0, (t0 + TQ + CK_SLC - 1) // CK_SLC, slc_body, 0)
        osum_s[...] += gate(1) * (acc_s[...] / l_s[...])

        _flash_init(m_s, l_s, acc_s)
        n_back = WINDOW // CK_WIN

        def win_body(j, carry):
            ks = pl.multiple_of(t0 + (j - n_back) * CK_WIN, CK_WIN)
            k_c = kvb_ref[pl.ds(ks, CK_WIN), 2 * LANES:3 * LANES]
            v_c = kvb_ref[pl.ds(ks, CK_WIN), 3 * LANES:4 * LANES]
            s = _mm(q_rot, k_c, _NT)
            diff = q_pos - (ks + col)
            ok = (diff >= 0) & (diff <= WINDOW)
            s = s + _tile_rows(jnp.where(ok, 0.0, MASK_VALUE), GROUP)
            _flash_step(s, v_c, m_s, l_s, acc_s)
            return carry

        lax.fori_loop(jnp.maximum(n_back - i * (TQ // CK_WIN), 0), n_back + TQ // CK_WIN, win_body, 0)
        o_all = osum_s[...] + gate(2) * (acc_s[...] / l_s[...])

        for pair, slab in enumerate(_merge_heads(o_all, kvh, TQ)):
            lo = (kvh * (GROUP // 2) + pair) * LANES
            o_ref[:, lo:lo + LANES] = (slab * _silu(zn_ref[:, lo:lo + LANES])).astype(o_ref.dtype)


def _nsa_prompt(q, qr, kvb, ck, cv, sm, zn, cmp_k_w, cmp_v_w, batch, seq):
    nt = seq // TQ
    ns = seq // SEL_BLOCK
    sub_w = CMP_STRIDE * KV_WIDTH
    keys = np.arange(seq)
    e = (np.arange(LANES)[None, :, None] == (keys // SEL_BLOCK).reshape(seq // CK_SLC, 1, CK_SLC))
    e = jnp.asarray(e, BF16)
    tile = lambda b, i: (b * nt + i, 0)
    per_b = lambda b, i: (b, 0)
    per_b3 = lambda b, i: (b, 0, 0)
    fix2 = lambda b, i: (0, 0)
    fix3 = lambda b, i: (0, 0, 0)
    wspecs = [pl.BlockSpec((sub_w, 4 * CMP_HIDDEN), fix2),
              pl.BlockSpec((2 * CMP_HIDDEN, KV_WIDTH), fix2),
              pl.BlockSpec((SUBLANES, sub_w), fix2)]
    rows = GROUP * TQ
    return pl.pallas_call(
        _nsa_prompt_kernel,
        grid=(batch, nt),
        in_specs=[pl.BlockSpec((TQ, QPAD_WIDTH), tile),
                  pl.BlockSpec((TQ, QPAD_WIDTH), tile),
                  pl.BlockSpec((seq, 4 * KV_WIDTH), per_b),
                  pl.BlockSpec((1, seq // CMP_STRIDE, sub_w), per_b3),
                  pl.BlockSpec((1, seq // CMP_STRIDE, sub_w), per_b3),
                  pl.BlockSpec((TQ, LANES), tile),
                  pl.BlockSpec((TQ, NSA_WIDTH), tile)] + wspecs + wspecs + [
                  pl.BlockSpec((LANES, LANES), fix2),
                  pl.BlockSpec((seq // CK_SLC, LANES, CK_SLC), fix3)],
        out_specs=pl.BlockSpec((TQ, NSA_WIDTH), tile),
        out_shape=jax.ShapeDtypeStruct((batch * seq, NSA_WIDTH), BF16),
        scratch_shapes=[pltpu.VMEM((N_SUB, KV_WIDTH), BF16), pltpu.VMEM((N_SUB, KV_WIDTH), BF16),
                        pltpu.VMEM((rows, LANES), F32), pltpu.VMEM((rows, LANES), F32),
                        pltpu.VMEM((rows, LANES), F32), pltpu.VMEM((rows, LANES), F32)],
        compiler_params=pltpu.CompilerParams(dimension_semantics=("arbitrary", "arbitrary"),
                                             vmem_limit_bytes=VMEM_LIMIT_BYTES),
        name="nsa_prompt",
    )(q, qr, kvb, ck.reshape(batch, seq // CMP_STRIDE, sub_w), cv.reshape(batch, seq // CMP_STRIDE, sub_w),
      sm, zn, *cmp_k_w, *cmp_v_w, _overlap_t(ns), e)


def _softmax_rows(s):
    m = jnp.max(s, axis=1, keepdims=True)
    e = jnp.exp(s - m)
    return e / jnp.sum(e, axis=1, keepdims=True)


def _nsa_sample_kernel(pt_ref, q_ref, qr_ref, skn_ref, svn_ref, wkn_ref, wvn_ref, sm_ref, zn_ref,
                       wkc_ref, wvc_ref, *rest, n_pages, page, past, n_new, win_keys):
    del pt_ref
    cmpk, cmpv = rest[0:n_pages], rest[n_pages:2 * n_pages]
    slck, slcv = rest[2 * n_pages:3 * n_pages], rest[3 * n_pages:4 * n_pages]
    wk1_ref, wk2_ref, pek_ref, wv1_ref, wv2_ref, pev_ref, ovt_ref, e_ref = rest[4 * n_pages:4 * n_pages + 8]
    o_ref, wko_ref, wvo_ref = rest[4 * n_pages + 8:4 * n_pages + 11]
    kall, vall, kwin, vwin = rest[4 * n_pages + 11:]
    nr = SAMPLE_ROWS
    rows = Q_HEADS * nr
    all_keys = kall.shape[0]
    win_buf = wkc_ref.shape[1]
    zeros8 = jnp.zeros((nr, LANES), F32)

    for p in range(n_pages):
        kall[p * page:(p + 1) * page, :] = slck[p][0].astype(BF16)
        vall[p * page:(p + 1) * page, :] = slcv[p][0].astype(BF16)
    kall[past:past + 2 * nr, :] = jnp.concatenate([skn_ref[...], zeros8], axis=0).astype(BF16)
    vall[past:past + 2 * nr, :] = jnp.concatenate([svn_ref[...], zeros8], axis=0).astype(BF16)
    kall[past + 2 * nr:, :] = jnp.zeros((all_keys - past - 2 * nr, LANES), BF16)
    vall[past + 2 * nr:, :] = jnp.zeros((all_keys - past - 2 * nr, LANES), BF16)
    kwin[0:win_buf, :] = wkc_ref[0].astype(BF16)
    vwin[0:win_buf, :] = wvc_ref[0].astype(BF16)
    kwin[win_buf:win_buf + 2 * nr, :] = jnp.concatenate([wkn_ref[...], zeros8], axis=0).astype(BF16)
    vwin[win_buf:win_buf + 2 * nr, :] = jnp.concatenate([wvn_ref[...], zeros8], axis=0).astype(BF16)
    kwin[win_buf + 2 * nr:, :] = jnp.zeros((win_keys - win_buf - 2 * nr, LANES), BF16)
    vwin[win_buf + 2 * nr:, :] = jnp.zeros((win_keys - win_buf - 2 * nr, LANES), BF16)

    ck = _compress(jnp.concatenate([r[0] for r in cmpk], axis=0), wk1_ref, wk2_ref, pek_ref)
    cv = _compress(jnp.concatenate([r[0] for r in cmpv], axis=0), wv1_ref, wv2_ref, pev_ref)

    q_raw = jnp.concatenate([q_ref[:, h * LANES:(h + 1) * LANES] for h in range(Q_HEADS)], axis=0)
    q_rot = jnp.concatenate([qr_ref[:, h * LANES:(h + 1) * LANES] for h in range(Q_HEADS)], axis=0)

    tok = lax.broadcasted_iota(jnp.int32, (nr, LANES), 0)
    col = lax.broadcasted_iota(jnp.int32, (nr, LANES), 1)
    cvalid = (CMP_STRIDE * col + (CMP_LEN - 1) <= past + tok) & (col < N_CMP)
    o_cmp, p = _cmp_attention(q_raw, ck, cv, cvalid, Q_HEADS)

    p_sum = []
    for kvh in range(KV_HEADS):
        base = kvh * GROUP * nr
        p_sum.append(sum(p[base + g * nr:base + (g + 1) * nr] for g in range(GROUP)))
    p_sum = jnp.concatenate(p_sum, axis=0)
    ns = (past + n_new + SEL_BLOCK - 1) // SEL_BLOCK
    ns_rows = -(-ns // SUBLANES) * SUBLANES
    nq = KV_HEADS * nr
    imp_t = _mm_exact_lhs(ovt_ref[...], p_sum, _NT)[0:ns_rows]
    n_idx = lax.broadcasted_iota(jnp.int32, (ns_rows, nq), 0)
    qb_t = (past + lax.broadcasted_iota(jnp.int32, (ns_rows, nq), 1) % nr) // SEL_BLOCK
    sel_t = _select_blocks(imp_t, n_idx, qb_t, ns)
    sel_t = jnp.concatenate([sel_t, jnp.zeros((LANES - ns_rows, nq), F32)], axis=0)
    sel = _mm(_eye(nq, BF16), sel_t, _NT)
    sel_rows = jnp.concatenate([_tile_rows(sel[kvh * nr:(kvh + 1) * nr], GROUP) for kvh in range(KV_HEADS)], axis=0)

    picked = _mm(sel_rows, e_ref[...])
    q_pos = past + lax.broadcasted_iota(jnp.int32, (rows, all_keys), 0) % nr
    k_pos = lax.broadcasted_iota(jnp.int32, (rows, all_keys), 1)
    ok = (picked > 0.5) & (k_pos <= q_pos)
    s = _mm(q_rot, kall[...], _NT) + jnp.where(ok, 0.0, MASK_VALUE)
    o_slc = _mm(_softmax_rows(s), vall[...])

    q_pos = past + lax.broadcasted_iota(jnp.int32, (rows, win_keys), 0) % nr
    k_pos = past - win_buf + lax.broadcasted_iota(jnp.int32, (rows, win_keys), 1)
    diff = q_pos - k_pos
    ok = (diff >= 0) & (diff <= WINDOW)
    s = _mm(q_rot, kwin[...], _NT) + jnp.where(ok, 0.0, MASK_VALUE)
    o_win = _mm(_softmax_rows(s), vwin[...])

    sig = _sigmoid(sm_ref[...])

    def gate(branch):
        cols = [sig[:, branch * Q_HEADS + h:branch * Q_HEADS + h + 1] for h in range(Q_HEADS)]
        return jnp.concatenate([jnp.broadcast_to(c, (nr, LANES)) for c in cols], axis=0)

    o_all = gate(0) * o_cmp + gate(1) * o_slc + gate(2) * o_win
    for kvh in range(KV_HEADS):
        o_kv = o_all[kvh * GROUP * nr:(kvh + 1) * GROUP * nr]
        for pair, slab in enumerate(_merge_heads(o_kv, kvh, nr)):
            lo = (kvh * (GROUP // 2) + pair) * LANES
            o_ref[:, lo:lo + LANES] = (slab * _silu(zn_ref[:, lo:lo + LANES])).astype(o_ref.dtype)

    for cache_ref, new_ref, out_ref in ((wkc_ref, wkn_ref, wko_ref), (wvc_ref, wvn_ref, wvo_ref)):
        shifted = pltpu.roll(cache_ref[0], win_buf - n_new, 0)
        out_ref[0] = shifted
        tail = jnp.where(tok < nr - n_new, shifted[win_buf - nr:], pltpu.roll(new_ref[...], nr - n_new, 0))
        out_ref[0, win_buf - nr:, :] = tail


def _nsa_sample(q, qr, sk, sv, wk, wv, sm, zn, win_k, win_v, cmp_k, cmp_v, slc_k, slc_v, page_table,
                cmp_k_w, cmp_v_w, n_new):
    batch, n_pages = page_table.shape
    n_pool, page = cmp_k.shape[0], cmp_k.shape[1]
    past = n_pages * page
    nr = SAMPLE_ROWS
    sub_w = CMP_STRIDE * KV_WIDTH
    win_buf = win_k.shape[1]
    all_keys = -(-(past + 2 * nr) // LANES) * LANES
    win_keys = -(-(win_buf + 2 * nr) // LANES) * LANES
    ns = (past + n_new + SEL_BLOCK - 1) // SEL_BLOCK
    e = jnp.asarray(np.arange(LANES)[:, None] == (np.arange(all_keys) // SEL_BLOCK)[None, :], BF16)
    cmp_k = cmp_k.reshape(n_pool, page // CMP_STRIDE, sub_w)
    cmp_v = cmp_v.reshape(n_pool, page // CMP_STRIDE, sub_w)
    slc_k = slc_k.reshape(n_pool, page, KV_WIDTH)
    slc_v = slc_v.reshape(n_pool, page, KV_WIDTH)
    win_k = win_k.reshape(batch, win_buf, KV_WIDTH)
    win_v = win_v.reshape(batch, win_buf, KV_WIDTH)

    row = lambda b, pt: (b, 0)
    per_b3 = lambda b, pt: (b, 0, 0)
    fix2 = lambda b, pt: (0, 0)
    page_map = lambda p: (lambda b, pt: (pt[b, p], 0, 0))
    wspecs = [pl.BlockSpec((sub_w, 4 * CMP_HIDDEN), fix2),
              pl.BlockSpec((2 * CMP_HIDDEN, KV_WIDTH), fix2),
              pl.BlockSpec((SUBLANES, sub_w), fix2)]
    in_specs = ([pl.BlockSpec((nr, QPAD_WIDTH), row)] * 2 + [pl.BlockSpec((nr, KV_WIDTH), row)] * 4
                + [pl.BlockSpec((nr, LANES), row), pl.BlockSpec((nr, NSA_WIDTH), row)]
                + [pl.BlockSpec((1, win_buf, KV_WIDTH), per_b3)] * 2
                + [pl.BlockSpec((1, page // CMP_STRIDE, sub_w), page_map(p)) for p in range(n_pages)] * 1
                + [pl.BlockSpec((1, page // CMP_STRIDE, sub_w), page_map(p)) for p in range(n_pages)]
                + [pl.BlockSpec((1, page, KV_WIDTH), page_map(p)) for p in range(n_pages)]
                + [pl.BlockSpec((1, page, KV_WIDTH), page_map(p)) for p in range(n_pages)]
                + wspecs + wspecs
                + [pl.BlockSpec((LANES, LANES), fix2), pl.BlockSpec((LANES, all_keys), fix2)])
    grid_spec = pltpu.PrefetchScalarGridSpec(
        num_scalar_prefetch=1,
        grid=(batch,),
        in_specs=in_specs,
        out_specs=[pl.BlockSpec((nr, NSA_WIDTH), row),
                   pl.BlockSpec((1, win_buf, KV_WIDTH), per_b3),
                   pl.BlockSpec((1, win_buf, KV_WIDTH), per_b3)],
        scratch_shapes=[pltpu.VMEM((all_keys, KV_WIDTH), BF16), pltpu.VMEM((all_keys, KV_WIDTH), BF16),
                        pltpu.VMEM((win_keys, KV_WIDTH), BF16), pltpu.VMEM((win_keys, KV_WIDTH), BF16)])
    return pl.pallas_call(
        functools.partial(_nsa_sample_kernel, n_pages=n_pages, page=page, past=past, n_new=n_new,
                          win_keys=win_keys),
        grid_spec=grid_spec,
        out_shape=[jax.ShapeDtypeStruct((batch * nr, NSA_WIDTH), F32),
                   jax.ShapeDtypeStruct((batch, win_buf, KV_WIDTH), F32),
                   jax.ShapeDtypeStruct((batch, win_buf, KV_WIDTH), F32)],
        compiler_params=pltpu.CompilerParams(dimension_semantics=("arbitrary",),
                                             vmem_limit_bytes=VMEM_LIMIT_BYTES),
        name="nsa_sample",
    )(page_table, q, qr, sk, sv, wk, wv, sm, zn, win_k, win_v,
      *([cmp_k] * n_pages), *([cmp_v] * n_pages), *([slc_k] * n_pages), *([slc_v] * n_pages),
      *cmp_k_w, *cmp_v_w, _overlap_t(ns), e)


def _unit_lower_inverse(a, n_valid, hi):
    c = a.shape[0]
    inv = _eye(c, F32) - a
    power = a
    span = 2
    while span < n_valid:
        power = _mm(power, power, hi=hi)
        inv = inv + _mm(inv, power, hi=hi)
        span *= 2
    return inv


def _gdn_kernel(qkv_ref, sm_ref, zg_ref, conv0_ref, s0_ref, wc_ref, vec_ref, wg_ref,
                go_ref, xp_out_ref, st_ref, xp_s, *, bblk, chunk, n_valid, hi):
    c_idx = pl.program_id(1)

    @pl.when(c_idx == 0)
    def _():
        xp_s[:, 0:SUBLANES, :] = conv0_ref[...]
        st_ref[...] = s0_ref[...]

    wc = wc_ref[...]
    vec = vec_ref[...]
    row1 = lax.broadcasted_iota(jnp.int32, (chunk, LANES), 0)
    rr = lax.broadcasted_iota(jnp.int32, (chunk, chunk), 0)
    cc = lax.broadcasted_iota(jnp.int32, (chunk, chunk), 1)
    tri = rr >= cc
    tri_bf = tri.astype(BF16)
    eye_bf = _eye(LANES, BF16)

    acts, betas, decays, decay_ts, e_decs = [], [], [], [], []
    for b in range(bblk):
        xp_s[b, SUBLANES:SUBLANES + chunk, :] = qkv_ref[b]
        y = xp_s[b, SUBLANES:SUBLANES + chunk, :] * wc[GDN_CONV - 1:GDN_CONV, :]
        for j in range(GDN_CONV - 1):
            lo = SUBLANES - (GDN_CONV - 1) + j
            y = y + xp_s[b, lo:lo + chunk, :] * wc[j:j + 1, :]
        xp_out_ref[b] = xp_s[b]
        xp_s[b, 0:SUBLANES, :] = xp_s[b, chunk:chunk + SUBLANES, :]
        act = _silu(y)
        small = sm_ref[b]
        z = small + vec[1:2, :]
        softplus = jnp.maximum(z, 0.0) + jnp.log1p(jnp.exp(-jnp.abs(z)))
        g_all = -jnp.exp(vec[0:1, :]) * softplus
        if n_valid < chunk:
            valid = row1 < n_valid
            act = act * _tile_lanes(valid.astype(F32), GDN_CONV_CH // LANES)
            g_all = jnp.where(valid, g_all, 0.0)
        acts.append(act)
        betas.append(_sigmoid(small))
        decays.append(_mm_exact_lhs(tri_bf, g_all))
    for b in range(bblk):
        decay_ts.append(_mm_exact_lhs(eye_bf, decays[b], _NT))
        e_decs.append(jnp.exp(decays[b]))

    chains = [(b, h) for b in range(bblk) for h in range(GDN_HEADS)]
    qs, ks, kbs, dmasks, rhs_u, rhs_w, qds, kds, gls = [], [], [], [], [], [], [], [], []
    for b, h in chains:
        act = acts[b]
        qh = act[:, h * GDN_DK:(h + 1) * GDN_DK]
        kh = act[:, GDN_WIDTH + h * GDN_DK:GDN_WIDTH + (h + 1) * GDN_DK]
        vh = act[:, 2 * GDN_WIDTH + h * GDN_DV:2 * GDN_WIDTH + (h + 1) * GDN_DV]
        qh = qh * lax.rsqrt(jnp.sum(qh * qh, axis=-1, keepdims=True) + NORM_EPS) * (GDN_DK ** -0.5)
        kh = kh * lax.rsqrt(jnp.sum(kh * kh, axis=-1, keepdims=True) + NORM_EPS)
        beta = betas[b][:, SM_B + h:SM_B + h + 1]
        dcol = decays[b][:, SM_A + h:SM_A + h + 1]
        drow = decay_ts[b][SM_A + h:SM_A + h + 1, :]
        ed = e_decs[b][:, SM_A + h:SM_A + h + 1]
        dlast = decays[b][chunk - 1:chunk, SM_A + h:SM_A + h + 1]
        kb = kh * beta
        qs.append(qh)
        ks.append(kh)
        kbs.append(kb)
        dmasks.append(jnp.where(tri, jnp.exp(jnp.where(tri, dcol - drow, 0.0)), 0.0))
        rhs_u.append(vh * beta)
        rhs_w.append(kb * ed)
        qds.append(qh * ed)
        kds.append(kh * jnp.exp(dlast - dcol))
        gls.append(jnp.exp(dlast))

    n = len(chains)
    eye_c = _eye(chunk, F32)
    kkts = [_mm(kbs[i], ks[i], _NT, hi=hi) for i in range(n)]
    qks = [_mm(qs[i], ks[i], _NT, hi=hi) for i in range(n)]
    powers = [jnp.where(rr > cc, kkts[i] * dmasks[i], 0.0) for i in range(n)]
    qks = [qks[i] * dmasks[i] for i in range(n)]
    invs = [eye_c - powers[i] for i in range(n)]
    span = 2
    while span < n_valid:
        powers = [_mm(powers[i], powers[i], hi=hi) for i in range(n)]
        invs = [invs[i] + _mm(invs[i], powers[i], hi=hi) for i in range(n)]
        span *= 2
    us = [_mm(invs[i], rhs_u[i], hi=hi) for i in range(n)]
    ws = [_mm(invs[i], rhs_w[i], hi=hi) for i in range(n)]
    sts = [st_ref[b, h] for b, h in chains]
    v_news = [us[i] - _mm(ws[i], sts[i], hi=hi) for i in range(n)]
    os_ = [_mm(qds[i], sts[i], hi=hi) for i in range(n)]
    os_ = [os_[i] + _mm(qks[i], v_news[i], hi=hi) for i in range(n)]
    upd = [_mm(kds[i], v_news[i], _TN, hi=hi) for i in range(n)]
    for i, (b, h) in enumerate(chains):
        st_ref[b, h] = sts[i] * gls[i] + upd[i]
        o = os_[i]
        o = o * lax.rsqrt(jnp.mean(o * o, axis=-1, keepdims=True) + NORM_EPS) * wg_ref[...]
        o = o * _silu(zg_ref[b, :, h * GDN_DV:(h + 1) * GDN_DV])
        go_ref[b, :, h * GDN_DV:(h + 1) * GDN_DV] = o.astype(go_ref.dtype)


def _gdn(qkv, sm, zg, conv0, s0, w_conv, a_log, dt_bias, w_gnorm, batch, rows, bblk, chunk, n_valid, hi,
         out_dtype):
    nc = rows // chunk
    tile = lambda b, c: (b, c, 0)
    per_b3 = lambda b, c: (b, 0, 0)
    per_b4 = lambda b, c: (b, 0, 0, 0)
    fix2 = lambda b, c: (0, 0)
    wc = jnp.concatenate([w_conv, jnp.zeros((SUBLANES - GDN_CONV, GDN_CONV_CH), w_conv.dtype)], axis=0)
    vec = jnp.zeros((SUBLANES, LANES), F32)
    vec = vec.at[0, SM_A:SM_A + GDN_HEADS].set(a_log).at[1, SM_A:SM_A + GDN_HEADS].set(dt_bias)
    go, xp, st = pl.pallas_call(
        functools.partial(_gdn_kernel, bblk=bblk, chunk=chunk, n_valid=n_valid, hi=hi),
        grid=(batch // bblk, nc),
        in_specs=[pl.BlockSpec((bblk, chunk, GDN_CONV_CH), tile),
                  pl.BlockSpec((bblk, chunk, LANES), tile),
                  pl.BlockSpec((bblk, chunk, GDN_WIDTH), tile),
                  pl.BlockSpec((bblk, SUBLANES, GDN_CONV_CH), per_b3),
                  pl.BlockSpec((bblk, GDN_HEADS, GDN_DK, GDN_DV), per_b4),
                  pl.BlockSpec((SUBLANES, GDN_CONV_CH), fix2),
                  pl.BlockSpec((SUBLANES, LANES), fix2),
                  pl.BlockSpec((1, GDN_DV), fix2)],
        out_specs=[pl.BlockSpec((bblk, chunk, GDN_WIDTH), tile),
                   pl.BlockSpec((bblk, SUBLANES + chunk, GDN_CONV_CH), per_b3),
                   pl.BlockSpec((bblk, GDN_HEADS, GDN_DK, GDN_DV), per_b4)],
        out_shape=[jax.ShapeDtypeStruct((batch, rows, GDN_WIDTH), out_dtype),
                   jax.ShapeDtypeStruct((batch, SUBLANES + chunk, GDN_CONV_CH), F32),
                   jax.ShapeDtypeStruct((batch, GDN_HEADS, GDN_DK, GDN_DV), F32)],
        scratch_shapes=[pltpu.VMEM((bblk, SUBLANES + chunk, GDN_CONV_CH), F32)],
        compiler_params=pltpu.CompilerParams(dimension_semantics=("arbitrary", "arbitrary"),
                                             vmem_limit_bytes=VMEM_LIMIT_BYTES),
        name="gdn",
    )(qkv.reshape(batch, rows, GDN_CONV_CH), sm.reshape(batch, rows, LANES), zg.reshape(batch, rows, GDN_WIDTH),
      conv0, s0, wc, vec, w_gnorm.reshape(1, GDN_DV))
    return go.reshape(batch * rows, GDN_WIDTH), xp, st


PROMPT_TM = 512
SAMPLE_TM = 128
GDN_PROMPT_BBLK = 4
GDN_SAMPLE_BBLK = 8


def _layer_prompt(h, lw, final, w_final, win_buf):
    w_norm, w_pack, cmp_k_w, cmp_v_w, w_conv, a_log, dt_bias, w_gnorm, w_out = lw
    batch, seq, d = h.shape
    x2d = h.reshape(batch * seq, d)
    tables = _rope_tables(jnp.arange(seq, dtype=jnp.int32))
    (q, qr, ck, cv, sk, sv, wk, wv, kvb, zn, qkv, zg, sm) = _project(
        x2d, w_norm, w_pack.astype(BF16), tables, seq, PROMPT_TM, False, BF16)
    nsa = _nsa_prompt(q, qr, kvb, ck, cv, sm, zn, cmp_k_w, cmp_v_w, batch, seq)
    conv0 = jnp.zeros((batch, SUBLANES, GDN_CONV_CH), F32)
    s0 = jnp.zeros((batch, GDN_HEADS, GDN_DK, GDN_DV), F32)
    go, xp, st = _gdn(qkv, sm, zg, conv0, s0, w_conv, a_log, dt_bias, w_gnorm,
                      batch, seq, GDN_PROMPT_BBLK, GDN_CHUNK, GDN_CHUNK, False, BF16)
    y = _out_project(x2d, nsa, go, w_out.astype(BF16), w_final, PROMPT_TM, False, final)
    kv4 = lambda t: t.reshape(batch, seq, KV_HEADS, HEAD_DIM)
    lead = ((0, 0), (max(win_buf - seq, 0), 0), (0, 0), (0, 0))
    win = lambda t: jnp.pad(kv4(t), lead)[:, -win_buf:]
    conv_new = xp[:, SUBLANES + GDN_CHUNK - (GDN_CONV - 1):SUBLANES + GDN_CHUNK]
    return y.reshape(batch, seq, d), (kv4(ck), kv4(cv), kv4(sk), kv4(sv), win(wk), win(wv), conv_new, st)


def _layer_sample(h8, n_new, caches, page_table, lw, final, w_final):
    w_norm, w_pack, cmp_k_w, cmp_v_w, w_conv, a_log, dt_bias, w_gnorm, w_out = lw
    c_cmp_k, c_cmp_v, c_slc_k, c_slc_v, c_win_k, c_win_v, s_conv, s_gdn = caches
    batch, nr, d = h8.shape
    past = page_table.shape[1] * c_cmp_k.shape[1]
    x2d = h8.reshape(batch * nr, d)
    tables = _rope_tables(past + jnp.arange(nr, dtype=jnp.int32))
    tables = tuple(jnp.tile(t, (SAMPLE_TM // nr, 1)) for t in tables)
    (q, qr, ck, cv, sk, sv, wk, wv, _, zn, qkv, zg, sm) = _project(
        x2d, w_norm, w_pack, tables, SAMPLE_TM, SAMPLE_TM, True, F32)
    nsa, win_k, win_v = _nsa_sample(q, qr, sk, sv, wk, wv, sm, zn, c_win_k, c_win_v,
                                    c_cmp_k, c_cmp_v, c_slc_k, c_slc_v, page_table, cmp_k_w, cmp_v_w, n_new)
    conv0 = jnp.pad(s_conv, ((0, 0), (SUBLANES - (GDN_CONV - 1), 0), (0, 0)))
    go, xp, st = _gdn(qkv, sm, zg, conv0, s_gdn, w_conv, a_log, dt_bias, w_gnorm,
                      batch, nr, GDN_SAMPLE_BBLK, nr, n_new, True, F32)
    y = _out_project(x2d, nsa, go, w_out, w_final, SAMPLE_TM, True, final)
    kv4 = lambda t: t.reshape(batch, nr, KV_HEADS, HEAD_DIM)[:, :n_new]
    win4 = lambda t: t.reshape(batch, -1, KV_HEADS, HEAD_DIM)
    conv_new = xp[:, SUBLANES + n_new - (GDN_CONV - 1):SUBLANES + n_new]
    return y.reshape(batch, nr, d), (kv4(ck), kv4(cv), kv4(sk), kv4(sv), win4(win_k), win4(win_v), conv_new, st)


def kernel(x_prompt, x_sample, cache_cmp_k, cache_cmp_v, cache_slc_k, cache_slc_v, cache_win_k, cache_win_v, state_conv, state_gdn, page_table, w_norm, w_in, pe_cmp_k, w_cmp_k1, w_cmp_k2, pe_cmp_v, w_cmp_v1, w_cmp_v2, w_conv, a_log, dt_bias, w_gdn_norm, w_out, w_final_norm):
    depth = w_in.shape[0]
    n_new = x_sample.shape[1]
    win_buf = cache_win_k.shape[2]
    h_p = x_prompt
    h_s = jnp.pad(x_sample, ((0, 0), (0, SAMPLE_ROWS - n_new), (0, 0)))
    st_p, st_s = [], []
    for layer in range(depth):
        lw = (w_norm[layer], _pack_w_in(w_in[layer]),
              _compress_weights(pe_cmp_k[layer], w_cmp_k1[layer], w_cmp_k2[layer]),
              _compress_weights(pe_cmp_v[layer], w_cmp_v1[layer], w_cmp_v2[layer]),
              w_conv[layer], a_log[layer], dt_bias[layer], w_gdn_norm[layer], w_out[layer])
        final = layer == depth - 1
        h_p, sp = _layer_prompt(h_p, lw, final, w_final_norm, win_buf)
        caches = (cache_cmp_k[layer], cache_cmp_v[layer], cache_slc_k[layer], cache_slc_v[layer],
                  cache_win_k[layer], cache_win_v[layer], state_conv[layer], state_gdn[layer])
        h_s, ss = _layer_sample(h_s, n_new, caches, page_table, lw, final, w_final_norm)
        st_p.append(sp)
        st_s.append(ss)
    outs = [h_p, h_s[:, :n_new]]
    for i in range(8):
        outs.append(jnp.stack([s[i] for s in st_p]))
        outs.append(jnp.stack([s[i] for s in st_s]))
    return tuple(outs)
```

```python
import functools

import numpy as np
import jax
import jax.numpy as jnp
from jax import lax
from jax.experimental import pallas as pl
from jax.experimental.pallas import tpu as pltpu

F32 = jnp.float32
BF16 = jnp.bfloat16

D_MODEL = 1024
HEAD_DIM = 64
Q_HEADS = 8
KV_HEADS = 2
GROUP = Q_HEADS // KV_HEADS
NSA_WIDTH = Q_HEADS * HEAD_DIM
KV_WIDTH = KV_HEADS * HEAD_DIM
CMP_LEN = 32
CMP_STRIDE = 16
CMP_HIDDEN = 128
SEL_BLOCK = 64
TOP_N = 8
WINDOW = 512
FORCE_BONUS = 1.0e4
ROT_DIM = HEAD_DIM // 4
ROPE_THETA = 500000.0
GDN_DK = 128
GDN_DV = 128
GDN_HEADS = 4
GDN_WIDTH = GDN_HEADS * GDN_DV
GDN_CONV = 4
GDN_CONV_CH = 3 * GDN_WIDTH
GDN_CHUNK = 64
NORM_EPS = 1e-6
MASK_VALUE = -1e30

LANES = 128
SUBLANES = 8
VMEM_LIMIT_BYTES = 56 * 1024 * 1024

QPAD_WIDTH = Q_HEADS * LANES
C_Q = 0
C_KV = C_Q + QPAD_WIDTH
C_ZN = C_KV + 6 * KV_WIDTH
C_QKV = C_ZN + NSA_WIDTH
C_ZG = C_QKV + GDN_CONV_CH
C_SM = C_ZG + GDN_WIDTH
N_PACK = C_SM + LANES
SM_B = 3 * Q_HEADS
SM_A = SM_B + GDN_HEADS

TQ = 128
CK_SLC = 256
CK_WIN = 128
N_SUB = 128
N_CMP = N_SUB - CMP_LEN // CMP_STRIDE + 1
SAMPLE_ROWS = 8
KVB_SK = 0
KVB_SV = KVB_SK + KV_WIDTH
KVB_WK = KVB_SV + 2 * KV_WIDTH
KVB_WV = KVB_WK + KV_WIDTH
KVB_WIDTH = KVB_WV + 2 * KV_WIDTH
KVB_COLS = (KVB_SK, KVB_SV, KVB_WK, KVB_WV)
WIN_SPAN = WINDOW + TQ


def _pack_w_in(w_in):
    o_gate = NSA_WIDTH + 6 * KV_WIDTH
    o_zn = o_gate + 3 * Q_HEADS
    o_qkv = o_zn + NSA_WIDTH
    o_b = o_qkv + GDN_CONV_CH
    o_a = o_b + GDN_HEADS
    o_zg = o_a + GDN_HEADS
    d = w_in.shape[0]
    z64 = jnp.zeros((d, HEAD_DIM), w_in.dtype)
    qcols = []
    for hq in range(Q_HEADS):
        wq = w_in[:, hq * HEAD_DIM:(hq + 1) * HEAD_DIM]
        qcols += [wq, z64] if hq // GROUP == 0 else [z64, wq]
    pad = jnp.zeros((d, LANES - SM_A - GDN_HEADS), w_in.dtype)
    return jnp.concatenate(
        qcols + [w_in[:, NSA_WIDTH:o_gate], w_in[:, o_zn:o_qkv], w_in[:, o_qkv:o_b], w_in[:, o_zg:],
                 w_in[:, o_gate:o_zn], w_in[:, o_b:o_a], w_in[:, o_a:o_zg], pad], axis=1)


def _rope_tables(pos):
    half = ROT_DIM // 2
    inv = ROPE_THETA ** (-(jnp.arange(half, dtype=F32) * 2.0 / ROT_DIM))
    ang = pos.astype(F32)[:, None] * inv[None, :]
    cos, sin = jnp.cos(ang), jnp.sin(ang)
    n = pos.shape[0]
    one = jnp.ones((n, HEAD_DIM - ROT_DIM), F32)
    zero = jnp.zeros((n, HEAD_DIM - ROT_DIM), F32)
    zh = jnp.zeros((n, half), F32)
    c64 = jnp.concatenate([cos, cos, one], axis=1)
    a64 = jnp.concatenate([zh, sin, zero], axis=1)
    b64 = jnp.concatenate([-sin, zh, zero], axis=1)
    tile = lambda t: jnp.concatenate([t, t], axis=1)
    return tile(c64), tile(a64), tile(b64)


def _rope128(x, c, a, b):
    half = ROT_DIM // 2
    return x * c + pltpu.roll(x, half, 1) * a + pltpu.roll(x, LANES - half, 1) * b


_NN = (((1,), (0,)), ((), ()))
_NT = (((1,), (1,)), ((), ()))
_TN = (((0,), (0,)), ((), ()))


def _mm(a, b, dims=_NN, hi=False):
    if hi:
        return lax.dot_general(a.astype(F32), b.astype(F32), dims, precision=lax.Precision.HIGHEST,
                               preferred_element_type=F32)
    return lax.dot_general(a.astype(BF16), b.astype(BF16), dims, preferred_element_type=F32)


def _mm_exact_lhs(a_bf16, x, dims=_NN):
    x1 = x.astype(BF16)
    r1 = x - x1.astype(F32)
    x2 = r1.astype(BF16)
    x3 = (r1 - x2.astype(F32)).astype(BF16)
    dot = lambda t: lax.dot_general(a_bf16, t, dims, preferred_element_type=F32)
    return dot(x1) + dot(x2) + dot(x3)


def _sigmoid(x):
    return 1.0 / (1.0 + jnp.exp(-x))


def _silu(x):
    return x * _sigmoid(x)


def _tile_rows(x, n):
    return jnp.concatenate([x] * n, axis=0)


def _tile_lanes(x, n):
    return x if n == 1 else jnp.concatenate([x] * n, axis=1)


def _eye(n, dtype):
    r = lax.broadcasted_iota(jnp.int32, (n, n), 0)
    c = lax.broadcasted_iota(jnp.int32, (n, n), 1)
    return (r == c).astype(dtype)


def _proj_kernel(x_ref, wn_ref, w_ref, c_ref, a_ref, b_ref,
                 q_ref, qr_ref, ck_ref, cv_ref, sk_ref, sv_ref, wk_ref, wv_ref, kvb_ref,
                 zn_ref, qkv_ref, zg_ref, sm_ref, *, hi):
    x = x_ref[...]
    ms = jnp.mean(x * x, axis=-1, keepdims=True)
    xn = x * lax.rsqrt(ms + NORM_EPS) * wn_ref[...]
    if not hi:
        xn = xn.astype(BF16)
    c, a, b = c_ref[...], a_ref[...], b_ref[...]
    scale = HEAD_DIM ** -0.5
    for j in range(Q_HEADS):
        qj = _mm(xn, w_ref[:, C_Q + j * LANES:C_Q + (j + 1) * LANES], hi=hi)
        q_ref[:, j * LANES:(j + 1) * LANES] = (qj * scale).astype(q_ref.dtype)
        qr_ref[:, j * LANES:(j + 1) * LANES] = (_rope128(qj, c, a, b) * scale).astype(qr_ref.dtype)
    kv_refs = (ck_ref, cv_ref, sk_ref, sv_ref, wk_ref, wv_ref)
    for j in range(6):
        kj = _mm(xn, w_ref[:, C_KV + j * LANES:C_KV + (j + 1) * LANES], hi=hi)
        if j in (2, 4):
            kj = _rope128(kj, c, a, b)
        kv_refs[j][...] = kj
        if j >= 2:
            lo = KVB_COLS[j - 2]
            kvb_ref[:, lo:lo + LANES] = kj.astype(BF16)
    ones = jnp.ones((x.shape[0], LANES), BF16)
    kvb_ref[:, KVB_SV + LANES:KVB_SV + 2 * LANES] = ones
    kvb_ref[:, KVB_WV + LANES:KVB_WV + 2 * LANES] = ones
    zn_ref[...] = _mm(xn, w_ref[:, C_ZN:C_QKV], hi=hi)
    for j in range(3):
        lo = C_QKV + j * GDN_WIDTH
        qkv_ref[:, j * GDN_WIDTH:(j + 1) * GDN_WIDTH] = _mm(xn, w_ref[:, lo:lo + GDN_WIDTH], hi=hi)
    zg_ref[...] = _mm(xn, w_ref[:, C_ZG:C_SM], hi=hi)
    sm_ref[...] = _mm(xn, w_ref[:, C_SM:N_PACK], hi=hi)


def _project(x2d, w_norm, w_pack, tables, rows_per_seq, tm, hi, q_dtype):
    n = x2d.shape[0]
    nt = rows_per_seq // tm
    row = lambda i: (i, 0)
    tab = lambda i: (i % nt, 0)
    fix = lambda i: (0, 0)
    widths = (QPAD_WIDTH, QPAD_WIDTH) + (KV_WIDTH,) * 6 + (KVB_WIDTH, NSA_WIDTH, GDN_CONV_CH, GDN_WIDTH, LANES)
    dtypes = (q_dtype, q_dtype) + (F32,) * 6 + (BF16, F32, F32, F32, F32)
    return pl.pallas_call(
        functools.partial(_proj_kernel, hi=hi),
        grid=(n // tm,),
        in_specs=[pl.BlockSpec((tm, D_MODEL), row),
                  pl.BlockSpec((1, D_MODEL), fix),
                  pl.BlockSpec((D_MODEL, N_PACK), fix),
                  pl.BlockSpec((tm, LANES), tab),
                  pl.BlockSpec((tm, LANES), tab),
                  pl.BlockSpec((tm, LANES), tab)],
        out_specs=[pl.BlockSpec((tm, w), row) for w in widths],
        out_shape=[jax.ShapeDtypeStruct((n, w), d) for w, d in zip(widths, dtypes)],
        compiler_params=pltpu.CompilerParams(dimension_semantics=("arbitrary",),
                                             vmem_limit_bytes=VMEM_LIMIT_BYTES),
        name="in_proj",
    )(x2d, w_norm.reshape(1, D_MODEL), w_pack, *tables)


def _out_kernel(x_ref, nsa_ref, gdn_ref, w_ref, wf_ref, y_ref, *, hi, final):
    acc = (_mm(nsa_ref[...], w_ref[:NSA_WIDTH, :], hi=hi)
           + _mm(gdn_ref[...], w_ref[NSA_WIDTH:, :], hi=hi))
    h = x_ref[...] + acc
    if final:
        ms = jnp.mean(h * h, axis=-1, keepdims=True)
        h = h * lax.rsqrt(ms + NORM_EPS) * wf_ref[...]
    y_ref[...] = h


def _out_project(x2d, nsa, gdn, w_out, w_final, tm, hi, final):
    n = x2d.shape[0]
    row = lambda i: (i, 0)
    fix = lambda i: (0, 0)
    return pl.pallas_call(
        functools.partial(_out_kernel, hi=hi, final=final),
        grid=(n // tm,),
        in_specs=[pl.BlockSpec((tm, D_MODEL), row),
                  pl.BlockSpec((tm, NSA_WIDTH), row),
                  pl.BlockSpec((tm, GDN_WIDTH), row),
                  pl.BlockSpec((D_MODEL, D_MODEL), fix),
                  pl.BlockSpec((1, D_MODEL), fix)],
        out_specs=pl.BlockSpec((tm, D_MODEL), row),
        out_shape=jax.ShapeDtypeStruct((n, D_MODEL), F32),
        compiler_params=pltpu.CompilerParams(dimension_semantics=("arbitrary",),
                                             vmem_limit_bytes=VMEM_LIMIT_BYTES),
        name="out_proj",
    )(x2d, nsa, gdn, w_out, w_final.reshape(1, D_MODEL))


def _compress_weights(pe, w1, w2):
    half = CMP_STRIDE * HEAD_DIM
    z = jnp.zeros((CMP_STRIDE, HEAD_DIM, CMP_HIDDEN), w1.dtype)

    def place(wpart, h):
        wp = wpart.reshape(CMP_STRIDE, HEAD_DIM, CMP_HIDDEN)
        parts = [wp, z] if h == 0 else [z, wp]
        return jnp.stack(parts, axis=1).reshape(CMP_STRIDE * KV_WIDTH, CMP_HIDDEN)

    w1big = jnp.concatenate([place(w1[:half], 0), place(w1[:half], 1),
                             place(w1[half:], 0), place(w1[half:], 1)], axis=1)
    zz = jnp.zeros_like(w2)
    w2big = jnp.concatenate([jnp.concatenate([w2, zz], axis=1), jnp.concatenate([zz, w2], axis=1)], axis=0)
    pe_a = jnp.tile(pe[:CMP_STRIDE], (1, KV_HEADS)).reshape(1, CMP_STRIDE * KV_WIDTH)
    pe_b = jnp.tile(pe[CMP_STRIDE:], (1, KV_HEADS)).reshape(1, CMP_STRIDE * KV_WIDTH)
    pe2 = jnp.concatenate([pe_a, pe_b, jnp.zeros((SUBLANES - 2, CMP_STRIDE * KV_WIDTH), pe.dtype)], axis=0)
    return w1big.astype(BF16), w2big.astype(BF16), pe2


def _compress(sub_rows, w1_ref, w2_ref, pe_ref):
    hid = 2 * CMP_HIDDEN
    w1 = w1_ref[...]
    ab = _mm(sub_rows, w1)
    pe = pe_ref[...]
    pe_hi = pe.astype(BF16)
    pe_lo = (pe - pe_hi.astype(F32)).astype(BF16)
    r = _mm(pe_hi, w1) + _mm(pe_lo, w1)
    bias = r[0:1, :hid] + r[1:2, hid:]
    h = ab[:, :hid] + pltpu.roll(ab[:, hid:], N_SUB - 1, 0) + bias
    return _mm(_silu(h), w2_ref[...])


def _overlap_t(ns):
    c0 = np.arange(N_CMP)[None, :] * CMP_STRIDE
    b0 = np.arange(ns)[:, None] * SEL_BLOCK
    ov = np.minimum(c0 + CMP_LEN, b0 + SEL_BLOCK) - np.maximum(c0, b0)
    out = np.zeros((LANES, LANES), np.float32)
    out[:ns, :N_CMP] = np.maximum(ov, 0) / CMP_LEN
    return jnp.asarray(out, BF16)


def _cmp_attention(q_stack, ck, cv, cvalid, groups):
    s = _mm(q_stack, ck, _NT)
    cm = _tile_rows(cvalid, groups)
    s = jnp.where(cm, s, MASK_VALUE)
    m = jnp.max(s, axis=1, keepdims=True)
    e = jnp.exp(s - m)
    p = e / jnp.sum(e, axis=1, keepdims=True) * cm.astype(F32)
    return _mm(p, cv), p


def _select_blocks(imp_t, n_idx, q_blk, n_rows):
    forced = (n_idx == 0) | (n_idx == q_blk) | (n_idx == q_blk - 1)
    allowed = n_idx <= q_blk
    v = jnp.where(allowed, imp_t + FORCE_BONUS * forced.astype(F32), MASK_VALUE)
    rank = jnp.zeros(v.shape, F32)
    for j in range(n_rows):
        vj = v[j:j + 1, :]
        ge = jnp.where(vj >= v, 1.0, 0.0)
        gt = jnp.where(vj > v, 1.0, 0.0)
        rank = rank + jnp.where(n_idx > j, ge, gt)
    return ((rank < TOP_N) & allowed).astype(F32)


def _flash_init(m_s, l_s, acc_s):
    m_s[...] = jnp.full(m_s.shape, MASK_VALUE, F32)
    l_s[...] = jnp.zeros(l_s.shape, F32)
    acc_s[...] = jnp.zeros(acc_s.shape, F32)


def _flash_step(s, v_c, m_s, l_s, acc_s):
    m_prev = m_s[...]
    m_next = jnp.maximum(m_prev, jnp.max(s, axis=1, keepdims=True))
    alpha = jnp.exp(m_prev - m_next)
    p = jnp.exp(s - _tile_lanes(m_next, s.shape[1] // LANES))
    l_s[...] = alpha * l_s[...] + jnp.sum(p, axis=1, keepdims=True)
    acc_s[...] = acc_s[...] * alpha + _mm(p, v_c)
    m_s[...] = m_next


def _merge_heads(o_sum, kvh, tq):
    lane = lax.broadcasted_iota(jnp.int32, (tq, LANES), 1)
    slabs = []
    for pair in range(GROUP // 2):
        halves = []
        for par in range(2):
            o = o_sum[(2 * pair + par) * tq:(2 * pair + par + 1) * tq]
            halves.append(o if par == kvh else pltpu.roll(o, HEAD_DIM, 1))
        slabs.append(jnp.where(lane < HEAD_DIM, halves[0], halves[1]))
    return slabs


def _nsa_prompt_kernel(q_ref, qr_ref, kvb_ref, ckr_ref, cvr_ref, sm_ref, zn_ref,
                       wk1_ref, wk2_ref, pek_ref, wv1_ref, wv2_ref, pev_ref, ovt_ref, e_ref,
                       o_ref, ck_s, cv_s, s_s, mrun_s, m_s, acc_s):
    i = pl.program_id(1)

    @pl.when(i == 0)
    def _():
        ck_s[...] = _compress(ckr_ref[0], wk1_ref, wk2_ref, pek_ref).astype(BF16)
        cv_s[...] = _compress(cvr_ref[0], wv1_ref, wv2_ref, pev_ref).astype(BF16)

    kvs = range(KV_HEADS)
    t0 = i * TQ
    rows = GROUP * TQ
    row = lax.broadcasted_iota(jnp.int32, (TQ, LANES), 0)
    col = lax.broadcasted_iota(jnp.int32, (TQ, LANES), 1)
    cvalid = (CMP_STRIDE * col + (CMP_LEN - 1) <= t0 + row) & (col < N_CMP)
    cm = _tile_rows(cvalid, GROUP)
    cm_f = cm.astype(F32)
    sig = _sigmoid(sm_ref[...])

    def gate(branch, kvh):
        cols = [sig[:, branch * Q_HEADS + h:branch * Q_HEADS + h + 1] for h in range(kvh * GROUP, (kvh + 1) * GROUP)]
        return jnp.concatenate([jnp.broadcast_to(c, (TQ, LANES)) for c in cols], axis=0)

    def stack(ref, kvh):
        return jnp.concatenate([ref[:, h * LANES:(h + 1) * LANES] for h in range(kvh * GROUP, (kvh + 1) * GROUP)], axis=0)

    q_raw = [stack(q_ref, k) for k in kvs]
    q_rot = [stack(qr_ref, k) for k in kvs]

    ck, cv = ck_s[...], cv_s[...]
    s = [jnp.where(cm, _mm(q_raw[k], ck, _NT), MASK_VALUE) for k in kvs]
    e = [jnp.exp(s[k] - jnp.max(s[k], axis=1, keepdims=True)) for k in kvs]
    p = [e[k] / jnp.sum(e[k], axis=1, keepdims=True) * cm_f for k in kvs]
    o_sum = [gate(0, k) * _mm(p[k], cv) for k in kvs]
    p_sum = jnp.concatenate([p[k][0:TQ] + p[k][TQ:2 * TQ] + p[k][2 * TQ:3 * TQ] + p[k][3 * TQ:4 * TQ] for k in kvs],
                            axis=0)
    ns_rows = 32
    nq = KV_HEADS * TQ
    imp_t = _mm_exact_lhs(ovt_ref[...], p_sum, _NT)[0:ns_rows]
    n_idx = lax.broadcasted_iota(jnp.int32, (ns_rows, nq), 0)
    qb_t = (t0 + lax.broadcasted_iota(jnp.int32, (ns_rows, nq), 1) % TQ) // SEL_BLOCK
    sel_t = _select_blocks(imp_t, n_idx, qb_t, ns_rows)
    sel_t = jnp.concatenate([sel_t, jnp.zeros((LANES - ns_rows, nq), F32)], axis=0).astype(BF16)
    eye = _eye(TQ, BF16)
    sel = [_mm(eye, sel_t[:, k * TQ:(k + 1) * TQ], _NT).astype(BF16) for k in kvs]

    col_s = lax.broadcasted_iota(jnp.int32, (TQ, CK_SLC), 1)
    row_s = lax.broadcasted_iota(jnp.int32, (TQ, CK_SLC), 0)
    n_chunks = (t0 + TQ + CK_SLC - 1) // CK_SLC
    mrun_s[...] = jnp.full(mrun_s.shape, MASK_VALUE, F32)

    def score_body(j, carry):
        k0 = pl.multiple_of(j * CK_SLC, CK_SLC)
        k_c = kvb_ref[pl.ds(k0, CK_SLC), KVB_SK:KVB_SK + LANES]
        causal = k0 + col_s <= t0 + row_s
        e_j = e_ref[j]
        sc = [_mm(q_rot[k], k_c, _NT) for k in kvs]
        picked = [_mm(sel[k], e_j) for k in kvs]
        for k in kvs:
            bias = jnp.where((picked[k] > 0.5) & causal, 0.0, MASK_VALUE)
            sk = sc[k] + _tile_rows(bias, GROUP)
            s_s[k, j] = sk
            mrun_s[k] = jnp.maximum(mrun_s[k], jnp.maximum(sk[:, :LANES], sk[:, LANES:]))
        return carry

    lax.fori_loop(0, n_chunks, score_body, 0)
    for k in kvs:
        m_s[k] = jnp.broadcast_to(jnp.max(mrun_s[k], axis=1, keepdims=True), (rows, LANES))
    acc_s[...] = jnp.zeros(acc_s.shape, F32)

    def value_body(j, carry):
        k0 = pl.multiple_of(j * CK_SLC, CK_SLC)
        v_c = kvb_ref[pl.ds(k0, CK_SLC), KVB_SV:KVB_SV + 2 * LANES]
        pr = [jnp.exp(s_s[k, j] - _tile_lanes(m_s[k], CK_SLC // LANES)) for k in kvs]
        for k in kvs:
            acc_s[k] += _mm(pr[k], v_c)
        return carry

    lax.fori_loop(0, n_chunks, value_body, 0)
    for k in kvs:
        acc = acc_s[k]
        o_sum[k] = o_sum[k] + gate(1, k) * (acc[:, :LANES] / acc[:, LANES:])

    ks0 = pl.multiple_of(jnp.maximum(t0 - WINDOW, 0), TQ)
    k_w = kvb_ref[pl.ds(ks0, WIN_SPAN), KVB_WK:KVB_WK + LANES]
    v_w = kvb_ref[pl.ds(ks0, WIN_SPAN), KVB_WV:KVB_WV + 2 * LANES]
    diff = (t0 + lax.broadcasted_iota(jnp.int32, (TQ, WIN_SPAN), 0)) - (ks0 + lax.broadcasted_iota(jnp.int32, (TQ, WIN_SPAN), 1))
    bias_w = _tile_rows(jnp.where((diff >= 0) & (diff <= WINDOW), 0.0, MASK_VALUE), GROUP)
    sw = [_mm(q_rot[k], k_w, _NT) + bias_w for k in kvs]
    pw = [jnp.exp(sw[k] - jnp.max(sw[k], axis=1, keepdims=True)) for k in kvs]
    rw = [_mm(pw[k], v_w) for k in kvs]
    for k in kvs:
        o_all = o_sum[k] + gate(2, k) * (rw[k][:, :LANES] / rw[k][:, LANES:])
        for pair, slab in enumerate(_merge_heads(o_all, k, TQ)):
            lo = (k * (GROUP // 2) + pair) * LANES
            o_ref[:, lo:lo + LANES] = (slab * _silu(zn_ref[:, lo:lo + LANES])).astype(o_ref.dtype)


def _nsa_prompt(q, qr, kvb, ck, cv, sm, zn, cmp_k_w, cmp_v_w, batch, seq):
    nt = seq // TQ
    ns = seq // SEL_BLOCK
    sub_w = CMP_STRIDE * KV_WIDTH
    keys = np.arange(seq)
    e = (np.arange(LANES)[None, :, None] == (keys // SEL_BLOCK).reshape(seq // CK_SLC, 1, CK_SLC))
    e = jnp.asarray(e, BF16)
    tile = lambda b, i: (b * nt + i, 0)
    per_b = lambda b, i: (b, 0)
    per_b3 = lambda b, i: (b, 0, 0)
    fix2 = lambda b, i: (0, 0)
    fix3 = lambda b, i: (0, 0, 0)
    wspecs = [pl.BlockSpec((sub_w, 4 * CMP_HIDDEN), fix2),
              pl.BlockSpec((2 * CMP_HIDDEN, KV_WIDTH), fix2),
              pl.BlockSpec((SUBLANES, sub_w), fix2)]
    rows = GROUP * TQ
    return pl.pallas_call(
        _nsa_prompt_kernel,
        grid=(batch, nt),
        in_specs=[pl.BlockSpec((TQ, QPAD_WIDTH), tile),
                  pl.BlockSpec((TQ, QPAD_WIDTH), tile),
                  pl.BlockSpec((seq, KVB_WIDTH), per_b),
                  pl.BlockSpec((1, seq // CMP_STRIDE, sub_w), per_b3),
                  pl.BlockSpec((1, seq // CMP_STRIDE, sub_w), per_b3),
                  pl.BlockSpec((TQ, LANES), tile),
                  pl.BlockSpec((TQ, NSA_WIDTH), tile)] + wspecs + wspecs + [
                  pl.BlockSpec((LANES, LANES), fix2),
                  pl.BlockSpec((seq // CK_SLC, LANES, CK_SLC), fix3)],
        out_specs=pl.BlockSpec((TQ, NSA_WIDTH), tile),
        out_shape=jax.ShapeDtypeStruct((batch * seq, NSA_WIDTH), BF16),
        scratch_shapes=[pltpu.VMEM((N_SUB, KV_WIDTH), BF16), pltpu.VMEM((N_SUB, KV_WIDTH), BF16),
                        pltpu.VMEM((KV_HEADS, seq // CK_SLC, rows, CK_SLC), F32),
                        pltpu.VMEM((KV_HEADS, rows, LANES), F32), pltpu.VMEM((KV_HEADS, rows, LANES), F32),
                        pltpu.VMEM((KV_HEADS, rows, 2 * LANES), F32)],
        compiler_params=pltpu.CompilerParams(dimension_semantics=("arbitrary", "arbitrary"),
                                             vmem_limit_bytes=VMEM_LIMIT_BYTES),
        name="nsa_prompt",
    )(q, qr, kvb, ck.reshape(batch, seq // CMP_STRIDE, sub_w), cv.reshape(batch, seq // CMP_STRIDE, sub_w),
      sm, zn, *cmp_k_w, *cmp_v_w, _overlap_t(ns), e)


def _softmax_rows(s):
    m = jnp.max(s, axis=1, keepdims=True)
    e = jnp.exp(s - m)
    return e / jnp.sum(e, axis=1, keepdims=True)


def _nsa_sample_kernel(pt_ref, q_ref, qr_ref, skn_ref, svn_ref, wkn_ref, wvn_ref, sm_ref, zn_ref,
                       wkc_ref, wvc_ref, *rest, n_pages, page, past, n_new, win_keys):
    del pt_ref
    cmpk, cmpv = rest[0:n_pages], rest[n_pages:2 * n_pages]
    slck, slcv = rest[2 * n_pages:3 * n_pages], rest[3 * n_pages:4 * n_pages]
    wk1_ref, wk2_ref, pek_ref, wv1_ref, wv2_ref, pev_ref, ovt_ref, e_ref = rest[4 * n_pages:4 * n_pages + 8]
    o_ref, wko_ref, wvo_ref = rest[4 * n_pages + 8:4 * n_pages + 11]
    kall, vall, kwin, vwin = rest[4 * n_pages + 11:]
    nr = SAMPLE_ROWS
    rows = Q_HEADS * nr
    all_keys = kall.shape[0]
    win_buf = wkc_ref.shape[1]
    zeros8 = jnp.zeros((nr, LANES), F32)

    for p in range(n_pages):
        kall[p * page:(p + 1) * page, :] = slck[p][0].astype(BF16)
        vall[p * page:(p + 1) * page, :] = slcv[p][0].astype(BF16)
    kall[past:past + 2 * nr, :] = jnp.concatenate([skn_ref[...], zeros8], axis=0).astype(BF16)
    vall[past:past + 2 * nr, :] = jnp.concatenate([svn_ref[...], zeros8], axis=0).astype(BF16)
    kall[past + 2 * nr:, :] = jnp.zeros((all_keys - past - 2 * nr, LANES), BF16)
    vall[past + 2 * nr:, :] = jnp.zeros((all_keys - past - 2 * nr, LANES), BF16)
    kwin[0:win_buf, :] = wkc_ref[0].astype(BF16)
    vwin[0:win_buf, :] = wvc_ref[0].astype(BF16)
    kwin[win_buf:win_buf + 2 * nr, :] = jnp.concatenate([wkn_ref[...], zeros8], axis=0).astype(BF16)
    vwin[win_buf:win_buf + 2 * nr, :] = jnp.concatenate([wvn_ref[...], zeros8], axis=0).astype(BF16)
    kwin[win_buf + 2 * nr:, :] = jnp.zeros((win_keys - win_buf - 2 * nr, LANES), BF16)
    vwin[win_buf + 2 * nr:, :] = jnp.zeros((win_keys - win_buf - 2 * nr, LANES), BF16)

    ck = _compress(jnp.concatenate([r[0] for r in cmpk], axis=0), wk1_ref, wk2_ref, pek_ref)
    cv = _compress(jnp.concatenate([r[0] for r in cmpv], axis=0), wv1_ref, wv2_ref, pev_ref)

    q_raw = jnp.concatenate([q_ref[:, h * LANES:(h + 1) * LANES] for h in range(Q_HEADS)], axis=0)
    q_rot = jnp.concatenate([qr_ref[:, h * LANES:(h + 1) * LANES] for h in range(Q_HEADS)], axis=0)

    tok = lax.broadcasted_iota(jnp.int32, (nr, LANES), 0)
    col = lax.broadcasted_iota(jnp.int32, (nr, LANES), 1)
    cvalid = (CMP_STRIDE * col + (CMP_LEN - 1) <= past + tok) & (col < N_CMP)
    o_cmp, p = _cmp_attention(q_raw, ck, cv, cvalid, Q_HEADS)

    p_sum = []
    for kvh in range(KV_HEADS):
        base = kvh * GROUP * nr
        p_sum.append(sum(p[base + g * nr:base + (g + 1) * nr] for g in range(GROUP)))
    p_sum = jnp.concatenate(p_sum, axis=0)
    ns = (past + n_new + SEL_BLOCK - 1) // SEL_BLOCK
    ns_rows = -(-ns // SUBLANES) * SUBLANES
    nq = KV_HEADS * nr
    imp_t = _mm_exact_lhs(ovt_ref[...], p_sum, _NT)[0:ns_rows]
    n_idx = lax.broadcasted_iota(jnp.int32, (ns_rows, nq), 0)
    qb_t = (past + lax.broadcasted_iota(jnp.int32, (ns_rows, nq), 1) % nr) // SEL_BLOCK
    sel_t = _select_blocks(imp_t, n_idx, qb_t, ns)
    sel_t = jnp.concatenate([sel_t, jnp.zeros((LANES - ns_rows, nq), F32)], axis=0)
    sel = _mm(_eye(nq, BF16), sel_t, _NT)
    sel_rows = jnp.concatenate([_tile_rows(sel[kvh * nr:(kvh + 1) * nr], GROUP) for kvh in range(KV_HEADS)], axis=0)

    picked = _mm(sel_rows, e_ref[...])
    q_pos = past + lax.broadcasted_iota(jnp.int32, (rows, all_keys), 0) % nr
    k_pos = lax.broadcasted_iota(jnp.int32, (rows, all_keys), 1)
    ok = (picked > 0.5) & (k_pos <= q_pos)
    s = _mm(q_rot, kall[...], _NT) + jnp.where(ok, 0.0, MASK_VALUE)
    o_slc = _mm(_softmax_rows(s), vall[...])

    q_pos = past + lax.broadcasted_iota(jnp.int32, (rows, win_keys), 0) % nr
    k_pos = past - win_buf + lax.broadcasted_iota(jnp.int32, (rows, win_keys), 1)
    diff = q_pos - k_pos
    ok = (diff >= 0) & (diff <= WINDOW)
    s = _mm(q_rot, kwin[...], _NT) + jnp.where(ok, 0.0, MASK_VALUE)
    o_win = _mm(_softmax_rows(s), vwin[...])

    sig = _sigmoid(sm_ref[...])

    def gate(branch):
        cols = [sig[:, branch * Q_HEADS + h:branch * Q_HEADS + h + 1] for h in range(Q_HEADS)]
        return jnp.concatenate([jnp.broadcast_to(c, (nr, LANES)) for c in cols], axis=0)

    o_all = gate(0) * o_cmp + gate(1) * o_slc + gate(2) * o_win
    for kvh in range(KV_HEADS):
        o_kv = o_all[kvh * GROUP * nr:(kvh + 1) * GROUP * nr]
        for pair, slab in enumerate(_merge_heads(o_kv, kvh, nr)):
            lo = (kvh * (GROUP // 2) + pair) * LANES
            o_ref[:, lo:lo + LANES] = (slab * _silu(zn_ref[:, lo:lo + LANES])).astype(o_ref.dtype)

    for cache_ref, new_ref, out_ref in ((wkc_ref, wkn_ref, wko_ref), (wvc_ref, wvn_ref, wvo_ref)):
        shifted = pltpu.roll(cache_ref[0], win_buf - n_new, 0)
        out_ref[0] = shifted
        tail = jnp.where(tok < nr - n_new, shifted[win_buf - nr:], pltpu.roll(new_ref[...], nr - n_new, 0))
        out_ref[0, win_buf - nr:, :] = tail


def _nsa_sample(q, qr, sk, sv, wk, wv, sm, zn, win_k, win_v, cmp_k, cmp_v, slc_k, slc_v, page_table,
                cmp_k_w, cmp_v_w, n_new):
    batch, n_pages = page_table.shape
    n_pool, page = cmp_k.shape[0], cmp_k.shape[1]
    past = n_pages * page
    nr = SAMPLE_ROWS
    sub_w = CMP_STRIDE * KV_WIDTH
    win_buf = win_k.shape[1]
    all_keys = -(-(past + 2 * nr) // LANES) * LANES
    win_keys = -(-(win_buf + 2 * nr) // LANES) * LANES
    ns = (past + n_new + SEL_BLOCK - 1) // SEL_BLOCK
    e = jnp.asarray(np.arange(LANES)[:, None] == (np.arange(all_keys) // SEL_BLOCK)[None, :], BF16)
    cmp_k = cmp_k.reshape(n_pool, page // CMP_STRIDE, sub_w)
    cmp_v = cmp_v.reshape(n_pool, page // CMP_STRIDE, sub_w)
    slc_k = slc_k.reshape(n_pool, page, KV_WIDTH)
    slc_v = slc_v.reshape(n_pool, page, KV_WIDTH)
    win_k = win_k.reshape(batch, win_buf, KV_WIDTH)
    win_v = win_v.reshape(batch, win_buf, KV_WIDTH)

    row = lambda b, pt: (b, 0)
    per_b3 = lambda b, pt: (b, 0, 0)
    fix2 = lambda b, pt: (0, 0)
    page_map = lambda p: (lambda b, pt: (pt[b, p], 0, 0))
    wspecs = [pl.BlockSpec((sub_w, 4 * CMP_HIDDEN), fix2),
              pl.BlockSpec((2 * CMP_HIDDEN, KV_WIDTH), fix2),
              pl.BlockSpec((SUBLANES, sub_w), fix2)]
    in_specs = ([pl.BlockSpec((nr, QPAD_WIDTH), row)] * 2 + [pl.BlockSpec((nr, KV_WIDTH), row)] * 4
                + [pl.BlockSpec((nr, LANES), row), pl.BlockSpec((nr, NSA_WIDTH), row)]
                + [pl.BlockSpec((1, win_buf, KV_WIDTH), per_b3)] * 2
                + [pl.BlockSpec((1, page // CMP_STRIDE, sub_w), page_map(p)) for p in range(n_pages)] * 1
                + [pl.BlockSpec((1, page // CMP_STRIDE, sub_w), page_map(p)) for p in range(n_pages)]
                + [pl.BlockSpec((1, page, KV_WIDTH), page_map(p)) for p in range(n_pages)]
                + [pl.BlockSpec((1, page, KV_WIDTH), page_map(p)) for p in range(n_pages)]
                + wspecs + wspecs
                + [pl.BlockSpec((LANES, LANES), fix2), pl.BlockSpec((LANES, all_keys), fix2)])
    grid_spec = pltpu.PrefetchScalarGridSpec(
        num_scalar_prefetch=1,
        grid=(batch,),
        in_specs=in_specs,
        out_specs=[pl.BlockSpec((nr, NSA_WIDTH), row),
                   pl.BlockSpec((1, win_buf, KV_WIDTH), per_b3),
                   pl.BlockSpec((1, win_buf, KV_WIDTH), per_b3)],
        scratch_shapes=[pltpu.VMEM((all_keys, KV_WIDTH), BF16), pltpu.VMEM((all_keys, KV_WIDTH), BF16),
                        pltpu.VMEM((win_keys, KV_WIDTH), BF16), pltpu.VMEM((win_keys, KV_WIDTH), BF16)])
    return pl.pallas_call(
        functools.partial(_nsa_sample_kernel, n_pages=n_pages, page=page, past=past, n_new=n_new,
                          win_keys=win_keys),
        grid_spec=grid_spec,
        out_shape=[jax.ShapeDtypeStruct((batch * nr, NSA_WIDTH), F32),
                   jax.ShapeDtypeStruct((batch, win_buf, KV_WIDTH), F32),
                   jax.ShapeDtypeStruct((batch, win_buf, KV_WIDTH), F32)],
        compiler_params=pltpu.CompilerParams(dimension_semantics=("arbitrary",),
                                             vmem_limit_bytes=VMEM_LIMIT_BYTES),
        name="nsa_sample",
    )(page_table, q, qr, sk, sv, wk, wv, sm, zn, win_k, win_v,
      *([cmp_k] * n_pages), *([cmp_v] * n_pages), *([slc_k] * n_pages), *([slc_v] * n_pages),
      *cmp_k_w, *cmp_v_w, _overlap_t(ns), e)


def _unit_lower_inverse(a, n_valid, hi):
    c = a.shape[0]
    inv = _eye(c, F32) - a
    power = a
    span = 2
    while span < n_valid:
        power = _mm(power, power, hi=hi)
        inv = inv + _mm(inv, power, hi=hi)
        span *= 2
    return inv


def _gdn_kernel(qkv_ref, sm_ref, zg_ref, conv0_ref, s0_ref, wc_ref, vec_ref, wg_ref,
                go_ref, xp_out_ref, st_ref, xp_s, *, bblk, chunk, n_valid, hi):
    c_idx = pl.program_id(1)

    @pl.when(c_idx == 0)
    def _():
        xp_s[:, 0:SUBLANES, :] = conv0_ref[...]
        st_ref[...] = s0_ref[...]

    wc = wc_ref[...]
    vec = vec_ref[...]
    row1 = lax.broadcasted_iota(jnp.int32, (chunk, LANES), 0)
    rr = lax.broadcasted_iota(jnp.int32, (chunk, chunk), 0)
    cc = lax.broadcasted_iota(jnp.int32, (chunk, chunk), 1)
    tri = rr >= cc
    tri_bf = tri.astype(BF16)
    eye_bf = _eye(LANES, BF16)

    acts, betas, decays, decay_ts, e_decs = [], [], [], [], []
    for b in range(bblk):
        xp_s[b, SUBLANES:SUBLANES + chunk, :] = qkv_ref[b]
        y = xp_s[b, SUBLANES:SUBLANES + chunk, :] * wc[GDN_CONV - 1:GDN_CONV, :]
        for j in range(GDN_CONV - 1):
            lo = SUBLANES - (GDN_CONV - 1) + j
            y = y + xp_s[b, lo:lo + chunk, :] * wc[j:j + 1, :]
        xp_out_ref[b] = xp_s[b]
        xp_s[b, 0:SUBLANES, :] = xp_s[b, chunk:chunk + SUBLANES, :]
        act = _silu(y)
        small = sm_ref[b]
        z = small + vec[1:2, :]
        softplus = jnp.maximum(z, 0.0) + jnp.log1p(jnp.exp(-jnp.abs(z)))
        g_all = -jnp.exp(vec[0:1, :]) * softplus
        if n_valid < chunk:
            valid = row1 < n_valid
            act = act * _tile_lanes(valid.astype(F32), GDN_CONV_CH // LANES)
            g_all = jnp.where(valid, g_all, 0.0)
        acts.append(act)
        betas.append(_sigmoid(small))
        decays.append(_mm_exact_lhs(tri_bf, g_all))
    for b in range(bblk):
        decay_ts.append(_mm_exact_lhs(eye_bf, decays[b], _NT))
        e_decs.append(jnp.exp(decays[b]))

    chains = [(b, h) for b in range(bblk) for h in range(GDN_HEADS)]
    qs, ks, kbs, dmasks, rhs_u, rhs_w, qds, kds, gls = [], [], [], [], [], [], [], [], []
    for b, h in chains:
        act = acts[b]
        qh = act[:, h * GDN_DK:(h + 1) * GDN_DK]
        kh = act[:, GDN_WIDTH + h * GDN_DK:GDN_WIDTH + (h + 1) * GDN_DK]
        vh = act[:, 2 * GDN_WIDTH + h * GDN_DV:2 * GDN_WIDTH + (h + 1) * GDN_DV]
        qh = qh * lax.rsqrt(jnp.sum(qh * qh, axis=-1, keepdims=True) + NORM_EPS) * (GDN_DK ** -0.5)
        kh = kh * lax.rsqrt(jnp.sum(kh * kh, axis=-1, keepdims=True) + NORM_EPS)
        beta = betas[b][:, SM_B + h:SM_B + h + 1]
        dcol = decays[b][:, SM_A + h:SM_A + h + 1]
        drow = decay_ts[b][SM_A + h:SM_A + h + 1, :]
        ed = e_decs[b][:, SM_A + h:SM_A + h + 1]
        dlast = decays[b][chunk - 1:chunk, SM_A + h:SM_A + h + 1]
        kb = kh * beta
        qs.append(qh)
        ks.append(kh)
        kbs.append(kb)
        dmasks.append(jnp.where(tri, jnp.exp(jnp.where(tri, dcol - drow, 0.0)), 0.0))
        rhs_u.append(vh * beta)
        rhs_w.append(kb * ed)
        qds.append(qh * ed)
        kds.append(kh * jnp.exp(dlast - dcol))
        gls.append(jnp.exp(dlast))

    n = len(chains)
    eye_c = _eye(chunk, F32)
    kkts = [_mm(kbs[i], ks[i], _NT, hi=hi) for i in range(n)]
    qks = [_mm(qs[i], ks[i], _NT, hi=hi) for i in range(n)]
    powers = [jnp.where(rr > cc, kkts[i] * dmasks[i], 0.0) for i in range(n)]
    qks = [qks[i] * dmasks[i] for i in range(n)]
    invs = [eye_c - powers[i] for i in range(n)]
    span = 2
    while span < n_valid:
        powers = [_mm(powers[i], powers[i], hi=hi) for i in range(n)]
        invs = [invs[i] + _mm(invs[i], powers[i], hi=hi) for i in range(n)]
        span *= 2
    us = [_mm(invs[i], rhs_u[i], hi=hi) for i in range(n)]
    ws = [_mm(invs[i], rhs_w[i], hi=hi) for i in range(n)]
    sts = [st_ref[b, h] for b, h in chains]
    v_news = [us[i] - _mm(ws[i], sts[i], hi=hi) for i in range(n)]
    os_ = [_mm(qds[i], sts[i], hi=hi) for i in range(n)]
    os_ = [os_[i] + _mm(qks[i], v_news[i], hi=hi) for i in range(n)]
    upd = [_mm(kds[i], v_news[i], _TN, hi=hi) for i in range(n)]
    for i, (b, h) in enumerate(chains):
        st_ref[b, h] = sts[i] * gls[i] + upd[i]
        o = os_[i]
        o = o * lax.rsqrt(jnp.mean(o * o, axis=-1, keepdims=True) + NORM_EPS) * wg_ref[...]
        o = o * _silu(zg_ref[b, :, h * GDN_DV:(h + 1) * GDN_DV])
        go_ref[b, :, h * GDN_DV:(h + 1) * GDN_DV] = o.astype(go_ref.dtype)


def _gdn(qkv, sm, zg, conv0, s0, w_conv, a_log, dt_bias, w_gnorm, batch, rows, bblk, chunk, n_valid, hi,
         out_dtype):
    nc = rows // chunk
    tile = lambda b, c: (b, c, 0)
    per_b3 = lambda b, c: (b, 0, 0)
    per_b4 = lambda b, c: (b, 0, 0, 0)
    fix2 = lambda b, c: (0, 0)
    wc = jnp.concatenate([w_conv, jnp.zeros((SUBLANES - GDN_CONV, GDN_CONV_CH), w_conv.dtype)], axis=0)
    vec = jnp.zeros((SUBLANES, LANES), F32)
    vec = vec.at[0, SM_A:SM_A + GDN_HEADS].set(a_log).at[1, SM_A:SM_A + GDN_HEADS].set(dt_bias)
    go, xp, st = pl.pallas_call(
        functools.partial(_gdn_kernel, bblk=bblk, chunk=chunk, n_valid=n_valid, hi=hi),
        grid=(batch // bblk, nc),
        in_specs=[pl.BlockSpec((bblk, chunk, GDN_CONV_CH), tile),
                  pl.BlockSpec((bblk, chunk, LANES), tile),
                  pl.BlockSpec((bblk, chunk, GDN_WIDTH), tile),
                  pl.BlockSpec((bblk, SUBLANES, GDN_CONV_CH), per_b3),
                  pl.BlockSpec((bblk, GDN_HEADS, GDN_DK, GDN_DV), per_b4),
                  pl.BlockSpec((SUBLANES, GDN_CONV_CH), fix2),
                  pl.BlockSpec((SUBLANES, LANES), fix2),
                  pl.BlockSpec((1, GDN_DV), fix2)],
        out_specs=[pl.BlockSpec((bblk, chunk, GDN_WIDTH), tile),
                   pl.BlockSpec((bblk, SUBLANES + chunk, GDN_CONV_CH), per_b3),
                   pl.BlockSpec((bblk, GDN_HEADS, GDN_DK, GDN_DV), per_b4)],
        out_shape=[jax.ShapeDtypeStruct((batch, rows, GDN_WIDTH), out_dtype),
                   jax.ShapeDtypeStruct((batch, SUBLANES + chunk, GDN_CONV_CH), F32),
                   jax.ShapeDtypeStruct((batch, GDN_HEADS, GDN_DK, GDN_DV), F32)],
        scratch_shapes=[pltpu.VMEM((bblk, SUBLANES + chunk, GDN_CONV_CH), F32)],
        compiler_params=pltpu.CompilerParams(dimension_semantics=("arbitrary", "arbitrary"),
                                             vmem_limit_bytes=VMEM_LIMIT_BYTES),
        name="gdn",
    )(qkv.reshape(batch, rows, GDN_CONV_CH), sm.reshape(batch, rows, LANES), zg.reshape(batch, rows, GDN_WIDTH),
      conv0, s0, wc, vec, w_gnorm.reshape(1, GDN_DV))
    return go.reshape(batch * rows, GDN_WIDTH), xp, st


PROMPT_TM = 512
SAMPLE_TM = 128
GDN_PROMPT_BBLK = 4
GDN_SAMPLE_BBLK = 8


def _layer_prompt(h, lw, final, w_final, win_buf):
    w_norm, w_pack, cmp_k_w, cmp_v_w, w_conv, a_log, dt_bias, w_gnorm, w_out = lw
    batch, seq, d = h.shape
    x2d = h.reshape(batch * seq, d)
    tables = _rope_tables(jnp.arange(seq, dtype=jnp.int32))
    (q, qr, ck, cv, sk, sv, wk, wv, kvb, zn, qkv, zg, sm) = _project(
        x2d, w_norm, w_pack.astype(BF16), tables, seq, PROMPT_TM, False, BF16)
    nsa = _nsa_prompt(q, qr, kvb, ck, cv, sm, zn, cmp_k_w, cmp_v_w, batch, seq)
    conv0 = jnp.zeros((batch, SUBLANES, GDN_CONV_CH), F32)
    s0 = jnp.zeros((batch, GDN_HEADS, GDN_DK, GDN_DV), F32)
    go, xp, st = _gdn(qkv, sm, zg, conv0, s0, w_conv, a_log, dt_bias, w_gnorm,
                      batch, seq, GDN_PROMPT_BBLK, GDN_CHUNK, GDN_CHUNK, False, BF16)
    y = _out_project(x2d, nsa, go, w_out.astype(BF16), w_final, PROMPT_TM, False, final)
    kv4 = lambda t: t.reshape(batch, seq, KV_HEADS, HEAD_DIM)
    lead = ((0, 0), (max(win_buf - seq, 0), 0), (0, 0), (0, 0))
    win = lambda t: jnp.pad(kv4(t), lead)[:, -win_buf:]
    conv_new = xp[:, SUBLANES + GDN_CHUNK - (GDN_CONV - 1):SUBLANES + GDN_CHUNK]
    return y.reshape(batch, seq, d), (kv4(ck), kv4(cv), kv4(sk), kv4(sv), win(wk), win(wv), conv_new, st)


def _layer_sample(h8, n_new, caches, page_table, lw, final, w_final):
    w_norm, w_pack, cmp_k_w, cmp_v_w, w_conv, a_log, dt_bias, w_gnorm, w_out = lw
    c_cmp_k, c_cmp_v, c_slc_k, c_slc_v, c_win_k, c_win_v, s_conv, s_gdn = caches
    batch, nr, d = h8.shape
    past = page_table.shape[1] * c_cmp_k.shape[1]
    x2d = h8.reshape(batch * nr, d)
    tables = _rope_tables(past + jnp.arange(nr, dtype=jnp.int32))
    tables = tuple(jnp.tile(t, (SAMPLE_TM // nr, 1)) for t in tables)
    (q, qr, ck, cv, sk, sv, wk, wv, _, zn, qkv, zg, sm) = _project(
        x2d, w_norm, w_pack, tables, SAMPLE_TM, SAMPLE_TM, True, F32)
    nsa, win_k, win_v = _nsa_sample(q, qr, sk, sv, wk, wv, sm, zn, c_win_k, c_win_v,
                                    c_cmp_k, c_cmp_v, c_slc_k, c_slc_v, page_table, cmp_k_w, cmp_v_w, n_new)
    conv0 = jnp.pad(s_conv, ((0, 0), (SUBLANES - (GDN_CONV - 1), 0), (0, 0)))
    go, xp, st = _gdn(qkv, sm, zg, conv0, s_gdn, w_conv, a_log, dt_bias, w_gnorm,
                      batch, nr, GDN_SAMPLE_BBLK, nr, n_new, True, F32)
    y = _out_project(x2d, nsa, go, w_out, w_final, SAMPLE_TM, True, final)
    kv4 = lambda t: t.reshape(batch, nr, KV_HEADS, HEAD_DIM)[:, :n_new]
    win4 = lambda t: t.reshape(batch, -1, KV_HEADS, HEAD_DIM)
    conv_new = xp[:, SUBLANES + n_new - (GDN_CONV - 1):SUBLANES + n_new]
    return y.reshape(batch, nr, d), (kv4(ck), kv4(cv), kv4(sk), kv4(sv), win4(win_k), win4(win_v), conv_new, st)


def kernel(x_prompt, x_sample, cache_cmp_k, cache_cmp_v, cache_slc_k, cache_slc_v, cache_win_k, cache_win_v, state_conv, state_gdn, page_table, w_norm, w_in, pe_cmp_k, w_cmp_k1, w_cmp_k2, pe_cmp_v, w_cmp_v1, w_cmp_v2, w_conv, a_log, dt_bias, w_gdn_norm, w_out, w_final_norm):
    depth = w_in.shape[0]
    n_new = x_sample.shape[1]
    win_buf = cache_win_k.shape[2]
    h_p = x_prompt
    h_s = jnp.pad(x_sample, ((0, 0), (0, SAMPLE_ROWS - n_new), (0, 0)))
    st_p, st_s = [], []
    for layer in range(depth):
        lw = (w_norm[layer], _pack_w_in(w_in[layer]),
              _compress_weights(pe_cmp_k[layer], w_cmp_k1[layer], w_cmp_k2[layer]),
              _compress_weights(pe_cmp_v[layer], w_cmp_v1[layer], w_cmp_v2[layer]),
              w_conv[layer], a_log[layer], dt_bias[layer], w_gdn_norm[layer], w_out[layer])
        final = layer == depth - 1
        h_p, sp = _layer_prompt(h_p, lw, final, w_final_norm, win_buf)
        caches = (cache_cmp_k[layer], cache_cmp_v[layer], cache_slc_k[layer], cache_slc_v[layer],
                  cache_win_k[layer], cache_win_v[layer], state_conv[layer], state_gdn[layer])
        h_s, ss = _layer_sample(h_s, n_new, caches, page_table, lw, final, w_final_norm)
        st_p.append(sp)
        st_s.append(ss)
    outs = [h_p, h_s[:, :n_new]]
    for i in range(8):
        outs.append(jnp.stack([s[i] for s in st_p]))
        outs.append(jnp.stack([s[i] for s in st_s]))
    return tuple(outs)
```

```python
import functools

import numpy as np
import jax
import jax.numpy as jnp
from jax import lax
from jax.experimental import pallas as pl
from jax.experimental.pallas import tpu as pltpu

F32 = jnp.float32
BF16 = jnp.bfloat16

D_MODEL = 1024
HEAD_DIM = 64
Q_HEADS = 8
KV_HEADS = 2
GROUP = Q_HEADS // KV_HEADS
NSA_WIDTH = Q_HEADS * HEAD_DIM
KV_WIDTH = KV_HEADS * HEAD_DIM
CMP_LEN = 32
CMP_STRIDE = 16
CMP_HIDDEN = 128
SEL_BLOCK = 64
TOP_N = 8
WINDOW = 512
FORCE_BONUS = 1.0e4
ROT_DIM = HEAD_DIM // 4
ROPE_THETA = 500000.0
GDN_DK = 128
GDN_DV = 128
GDN_HEADS = 4
GDN_WIDTH = GDN_HEADS * GDN_DV
GDN_CONV = 4
GDN_CONV_CH = 3 * GDN_WIDTH
GDN_CHUNK = 64
NORM_EPS = 1e-6
MASK_VALUE = -1e30

LANES = 128
SUBLANES = 8
VMEM_LIMIT_BYTES = 56 * 1024 * 1024

QPAD_WIDTH = Q_HEADS * LANES
C_Q = 0
C_KV = C_Q + QPAD_WIDTH
C_ZN = C_KV + 6 * KV_WIDTH
C_QKV = C_ZN + NSA_WIDTH
C_ZG = C_QKV + GDN_CONV_CH
C_SM = C_ZG + GDN_WIDTH
N_PACK = C_SM + LANES
SM_B = 3 * Q_HEADS
SM_A = SM_B + GDN_HEADS

TQ = 128
CK_SLC = 256
CK_WIN = 128
N_SUB = 128
N_CMP = N_SUB - CMP_LEN // CMP_STRIDE + 1
SAMPLE_ROWS = 8
KVB_SK = 0
KVB_SV = KVB_SK + KV_WIDTH
KVB_WK = KVB_SV + 2 * KV_WIDTH
KVB_WV = KVB_WK + KV_WIDTH
KVB_WIDTH = KVB_WV + 2 * KV_WIDTH
KVB_COLS = (KVB_SK, KVB_SV, KVB_WK, KVB_WV)
WIN_SPAN = WINDOW + TQ


def _pack_w_in(w_in):
    o_gate = NSA_WIDTH + 6 * KV_WIDTH
    o_zn = o_gate + 3 * Q_HEADS
    o_qkv = o_zn + NSA_WIDTH
    o_b = o_qkv + GDN_CONV_CH
    o_a = o_b + GDN_HEADS
    o_zg = o_a + GDN_HEADS
    d = w_in.shape[0]
    z64 = jnp.zeros((d, HEAD_DIM), w_in.dtype)
    qcols = []
    for hq in range(Q_HEADS):
        wq = w_in[:, hq * HEAD_DIM:(hq + 1) * HEAD_DIM]
        qcols += [wq, z64] if hq // GROUP == 0 else [z64, wq]
    pad = jnp.zeros((d, LANES - SM_A - GDN_HEADS), w_in.dtype)
    return jnp.concatenate(
        qcols + [w_in[:, NSA_WIDTH:o_gate], w_in[:, o_zn:o_qkv], w_in[:, o_qkv:o_b], w_in[:, o_zg:],
                 w_in[:, o_gate:o_zn], w_in[:, o_b:o_a], w_in[:, o_a:o_zg], pad], axis=1)


def _rope_tables(pos):
    half = ROT_DIM // 2
    inv = ROPE_THETA ** (-(jnp.arange(half, dtype=F32) * 2.0 / ROT_DIM))
    ang = pos.astype(F32)[:, None] * inv[None, :]
    cos, sin = jnp.cos(ang), jnp.sin(ang)
    n = pos.shape[0]
    one = jnp.ones((n, HEAD_DIM - ROT_DIM), F32)
    zero = jnp.zeros((n, HEAD_DIM - ROT_DIM), F32)
    zh = jnp.zeros((n, half), F32)
    c64 = jnp.concatenate([cos, cos, one], axis=1)
    a64 = jnp.concatenate([zh, sin, zero], axis=1)
    b64 = jnp.concatenate([-sin, zh, zero], axis=1)
    tile = lambda t: jnp.concatenate([t, t], axis=1)
    return tile(c64), tile(a64), tile(b64)


def _rope128(x, c, a, b):
    half = ROT_DIM // 2
    return x * c + pltpu.roll(x, half, 1) * a + pltpu.roll(x, LANES - half, 1) * b


_NN = (((1,), (0,)), ((), ()))
_NT = (((1,), (1,)), ((), ()))
_TN = (((0,), (0,)), ((), ()))


def _mm(a, b, dims=_NN, hi=False):
    if hi:
        return lax.dot_general(a.astype(F32), b.astype(F32), dims, precision=lax.Precision.HIGHEST,
                               preferred_element_type=F32)
    return lax.dot_general(a.astype(BF16), b.astype(BF16), dims, preferred_element_type=F32)


def _mm_exact_lhs(a_bf16, x, dims=_NN):
    x1 = x.astype(BF16)
    r1 = x - x1.astype(F32)
    x2 = r1.astype(BF16)
    x3 = (r1 - x2.astype(F32)).astype(BF16)
    dot = lambda t: lax.dot_general(a_bf16, t, dims, preferred_element_type=F32)
    return dot(x1) + dot(x2) + dot(x3)


def _sigmoid(x):
    return 1.0 / (1.0 + jnp.exp(-x))


def _silu(x):
    return x * _sigmoid(x)


def _tile_rows(x, n):
    return jnp.concatenate([x] * n, axis=0)


def _tile_lanes(x, n):
    return x if n == 1 else jnp.concatenate([x] * n, axis=1)


def _eye(n, dtype):
    r = lax.broadcasted_iota(jnp.int32, (n, n), 0)
    c = lax.broadcasted_iota(jnp.int32, (n, n), 1)
    return (r == c).astype(dtype)


def _proj_kernel(x_ref, wn_ref, w_ref, c_ref, a_ref, b_ref,
                 q_ref, qr_ref, ck_ref, cv_ref, sk_ref, sv_ref, wk_ref, wv_ref, kvb_ref,
                 zn_ref, qkv_ref, zg_ref, sm_ref, *, hi):
    x = x_ref[...]
    ms = jnp.mean(x * x, axis=-1, keepdims=True)
    xn = x * lax.rsqrt(ms + NORM_EPS) * wn_ref[...]
    if not hi:
        xn = xn.astype(BF16)
    c, a, b = c_ref[...], a_ref[...], b_ref[...]
    scale = HEAD_DIM ** -0.5
    for j in range(Q_HEADS):
        qj = _mm(xn, w_ref[:, C_Q + j * LANES:C_Q + (j + 1) * LANES], hi=hi)
        q_ref[:, j * LANES:(j + 1) * LANES] = (qj * scale).astype(q_ref.dtype)
        qr_ref[:, j * LANES:(j + 1) * LANES] = (_rope128(qj, c, a, b) * scale).astype(qr_ref.dtype)
    kv_refs = (ck_ref, cv_ref, sk_ref, sv_ref, wk_ref, wv_ref)
    for j in range(6):
        kj = _mm(xn, w_ref[:, C_KV + j * LANES:C_KV + (j + 1) * LANES], hi=hi)
        if j in (2, 4):
            kj = _rope128(kj, c, a, b)
        kv_refs[j][...] = kj
        if j >= 2:
            lo = KVB_COLS[j - 2]
            kvb_ref[:, lo:lo + LANES] = kj.astype(BF16)
    ones = jnp.ones((x.shape[0], LANES), BF16)
    kvb_ref[:, KVB_SV + LANES:KVB_SV + 2 * LANES] = ones
    kvb_ref[:, KVB_WV + LANES:KVB_WV + 2 * LANES] = ones
    zn_ref[...] = _mm(xn, w_ref[:, C_ZN:C_QKV], hi=hi)
    for j in range(3):
        lo = C_QKV + j * GDN_WIDTH
        qkv_ref[:, j * GDN_WIDTH:(j + 1) * GDN_WIDTH] = _mm(xn, w_ref[:, lo:lo + GDN_WIDTH], hi=hi)
    zg_ref[...] = _mm(xn, w_ref[:, C_ZG:C_SM], hi=hi)
    sm_ref[...] = _mm(xn, w_ref[:, C_SM:N_PACK], hi=hi)


def _project(x2d, w_norm, w_pack, tables, rows_per_seq, tm, hi, q_dtype):
    n = x2d.shape[0]
    nt = rows_per_seq // tm
    row = lambda i: (i, 0)
    tab = lambda i: (i % nt, 0)
    fix = lambda i: (0, 0)
    widths = (QPAD_WIDTH, QPAD_WIDTH) + (KV_WIDTH,) * 6 + (KVB_WIDTH, NSA_WIDTH, GDN_CONV_CH, GDN_WIDTH, LANES)
    dtypes = (q_dtype, q_dtype) + (F32,) * 6 + (BF16, F32, F32, F32, F32)
    return pl.pallas_call(
        functools.partial(_proj_kernel, hi=hi),
        grid=(n // tm,),
        in_specs=[pl.BlockSpec((tm, D_MODEL), row),
                  pl.BlockSpec((1, D_MODEL), fix),
                  pl.BlockSpec((D_MODEL, N_PACK), fix),
                  pl.BlockSpec((tm, LANES), tab),
                  pl.BlockSpec((tm, LANES), tab),
                  pl.BlockSpec((tm, LANES), tab)],
        out_specs=[pl.BlockSpec((tm, w), row) for w in widths],
        out_shape=[jax.ShapeDtypeStruct((n, w), d) for w, d in zip(widths, dtypes)],
        compiler_params=pltpu.CompilerParams(dimension_semantics=("arbitrary",),
                                             vmem_limit_bytes=VMEM_LIMIT_BYTES),
        name="in_proj",
    )(x2d, w_norm.reshape(1, D_MODEL), w_pack, *tables)


def _out_kernel(x_ref, nsa_ref, gdn_ref, w_ref, wf_ref, y_ref, *, hi, final):
    acc = (_mm(nsa_ref[...], w_ref[:NSA_WIDTH, :], hi=hi)
           + _mm(gdn_ref[...], w_ref[NSA_WIDTH:, :], hi=hi))
    h = x_ref[...] + acc
    if final:
        ms = jnp.mean(h * h, axis=-1, keepdims=True)
        h = h * lax.rsqrt(ms + NORM_EPS) * wf_ref[...]
    y_ref[...] = h


def _out_project(x2d, nsa, gdn, w_out, w_final, tm, hi, final):
    n = x2d.shape[0]
    row = lambda i: (i, 0)
    fix = lambda i: (0, 0)
    return pl.pallas_call(
        functools.partial(_out_kernel, hi=hi, final=final),
        grid=(n // tm,),
        in_specs=[pl.BlockSpec((tm, D_MODEL), row),
                  pl.BlockSpec((tm, NSA_WIDTH), row),
                  pl.BlockSpec((tm, GDN_WIDTH), row),
                  pl.BlockSpec((D_MODEL, D_MODEL), fix),
                  pl.BlockSpec((1, D_MODEL), fix)],
        out_specs=pl.BlockSpec((tm, D_MODEL), row),
        out_shape=jax.ShapeDtypeStruct((n, D_MODEL), F32),
        compiler_params=pltpu.CompilerParams(dimension_semantics=("arbitrary",),
                                             vmem_limit_bytes=VMEM_LIMIT_BYTES),
        name="out_proj",
    )(x2d, nsa, gdn, w_out, w_final.reshape(1, D_MODEL))


def _compress_weights(pe, w1, w2):
    half = CMP_STRIDE * HEAD_DIM
    z = jnp.zeros((CMP_STRIDE, HEAD_DIM, CMP_HIDDEN), w1.dtype)

    def place(wpart, h):
        wp = wpart.reshape(CMP_STRIDE, HEAD_DIM, CMP_HIDDEN)
        parts = [wp, z] if h == 0 else [z, wp]
        return jnp.stack(parts, axis=1).reshape(CMP_STRIDE * KV_WIDTH, CMP_HIDDEN)

    w1big = jnp.concatenate([place(w1[:half], 0), place(w1[:half], 1),
                             place(w1[half:], 0), place(w1[half:], 1)], axis=1)
    zz = jnp.zeros_like(w2)
    w2big = jnp.concatenate([jnp.concatenate([w2, zz], axis=1), jnp.concatenate([zz, w2], axis=1)], axis=0)
    pe_a = jnp.tile(pe[:CMP_STRIDE], (1, KV_HEADS)).reshape(1, CMP_STRIDE * KV_WIDTH)
    pe_b = jnp.tile(pe[CMP_STRIDE:], (1, KV_HEADS)).reshape(1, CMP_STRIDE * KV_WIDTH)
    pe2 = jnp.concatenate([pe_a, pe_b, jnp.zeros((SUBLANES - 2, CMP_STRIDE * KV_WIDTH), pe.dtype)], axis=0)
    return w1big.astype(BF16), w2big.astype(BF16), pe2


def _compress(sub_rows, w1_ref, w2_ref, pe_ref):
    hid = 2 * CMP_HIDDEN
    w1 = w1_ref[...]
    ab = _mm(sub_rows, w1)
    pe = pe_ref[...]
    pe_hi = pe.astype(BF16)
    pe_lo = (pe - pe_hi.astype(F32)).astype(BF16)
    r = _mm(pe_hi, w1) + _mm(pe_lo, w1)
    bias = r[0:1, :hid] + r[1:2, hid:]
    h = ab[:, :hid] + pltpu.roll(ab[:, hid:], N_SUB - 1, 0) + bias
    return _mm(_silu(h), w2_ref[...])


def _overlap_t(ns):
    c0 = np.arange(N_CMP)[None, :] * CMP_STRIDE
    b0 = np.arange(ns)[:, None] * SEL_BLOCK
    ov = np.minimum(c0 + CMP_LEN, b0 + SEL_BLOCK) - np.maximum(c0, b0)
    out = np.zeros((LANES, LANES), np.float32)
    out[:ns, :N_CMP] = np.maximum(ov, 0) / CMP_LEN
    return jnp.asarray(out, BF16)


def _cmp_attention(q_stack, ck, cv, cvalid, groups):
    s = _mm(q_stack, ck, _NT)
    cm = _tile_rows(cvalid, groups)
    s = jnp.where(cm, s, MASK_VALUE)
    m = jnp.max(s, axis=1, keepdims=True)
    e = jnp.exp(s - m)
    p = e / jnp.sum(e, axis=1, keepdims=True) * cm.astype(F32)
    return _mm(p, cv), p


def _select_blocks(imp_t, n_idx, q_blk, n_rows):
    forced = (n_idx == 0) | (n_idx == q_blk) | (n_idx == q_blk - 1)
    allowed = n_idx <= q_blk
    v = jnp.where(allowed, imp_t + FORCE_BONUS * forced.astype(F32), MASK_VALUE)
    rank = jnp.zeros(v.shape, F32)
    for j in range(n_rows):
        vj = v[j:j + 1, :]
        ge = jnp.where(vj >= v, 1.0, 0.0)
        gt = jnp.where(vj > v, 1.0, 0.0)
        rank = rank + jnp.where(n_idx > j, ge, gt)
    return ((rank < TOP_N) & allowed).astype(F32)


def _flash_init(m_s, l_s, acc_s):
    m_s[...] = jnp.full(m_s.shape, MASK_VALUE, F32)
    l_s[...] = jnp.zeros(l_s.shape, F32)
    acc_s[...] = jnp.zeros(acc_s.shape, F32)


def _flash_step(s, v_c, m_s, l_s, acc_s):
    m_prev = m_s[...]
    m_next = jnp.maximum(m_prev, jnp.max(s, axis=1, keepdims=True))
    alpha = jnp.exp(m_prev - m_next)
    p = jnp.exp(s - _tile_lanes(m_next, s.shape[1] // LANES))
    l_s[...] = alpha * l_s[...] + jnp.sum(p, axis=1, keepdims=True)
    acc_s[...] = acc_s[...] * alpha + _mm(p, v_c)
    m_s[...] = m_next


def _merge_heads(o_sum, kvh, tq):
    lane = lax.broadcasted_iota(jnp.int32, (tq, LANES), 1)
    slabs = []
    for pair in range(GROUP // 2):
        halves = []
        for par in range(2):
            o = o_sum[(2 * pair + par) * tq:(2 * pair + par + 1) * tq]
            halves.append(o if par == kvh else pltpu.roll(o, HEAD_DIM, 1))
        slabs.append(jnp.where(lane < HEAD_DIM, halves[0], halves[1]))
    return slabs


def _nsa_prompt_kernel(q_ref, qr_ref, kvb_ref, ckr_ref, cvr_ref, sm_ref, zn_ref,
                       wk1_ref, wk2_ref, pek_ref, wv1_ref, wv2_ref, pev_ref, ovt_ref, e_ref,
                       o_ref, ck_s, cv_s, s_s, mrun_s, m_s, acc_s):
    i = pl.program_id(1)

    @pl.when(i == 0)
    def _():
        ck_s[...] = _compress(ckr_ref[0], wk1_ref, wk2_ref, pek_ref).astype(BF16)
        cv_s[...] = _compress(cvr_ref[0], wv1_ref, wv2_ref, pev_ref).astype(BF16)

    kvs = range(KV_HEADS)
    t0 = i * TQ
    rows = GROUP * TQ
    row = lax.broadcasted_iota(jnp.int32, (TQ, LANES), 0)
    col = lax.broadcasted_iota(jnp.int32, (TQ, LANES), 1)
    cvalid = (CMP_STRIDE * col + (CMP_LEN - 1) <= t0 + row) & (col < N_CMP)
    cm = _tile_rows(cvalid, GROUP)
    cm_f = cm.astype(F32)
    sig = _sigmoid(sm_ref[...])

    def gate(branch, kvh):
        cols = [sig[:, branch * Q_HEADS + h:branch * Q_HEADS + h + 1] for h in range(kvh * GROUP, (kvh + 1) * GROUP)]
        return jnp.concatenate([jnp.broadcast_to(c, (TQ, LANES)) for c in cols], axis=0)

    def stack(ref, kvh):
        return jnp.concatenate([ref[:, h * LANES:(h + 1) * LANES] for h in range(kvh * GROUP, (kvh + 1) * GROUP)], axis=0)

    q_raw = [stack(q_ref, k) for k in kvs]
    q_rot = [stack(qr_ref, k) for k in kvs]

    ck, cv = ck_s[...], cv_s[...]
    s = [jnp.where(cm, _mm(q_raw[k], ck, _NT), MASK_VALUE) for k in kvs]
    e = [jnp.exp(s[k] - jnp.max(s[k], axis=1, keepdims=True)) for k in kvs]
    p = [e[k] / jnp.sum(e[k], axis=1, keepdims=True) * cm_f for k in kvs]
    o_sum = [gate(0, k) * _mm(p[k], cv) for k in kvs]
    p_sum = jnp.concatenate([p[k][0:TQ] + p[k][TQ:2 * TQ] + p[k][2 * TQ:3 * TQ] + p[k][3 * TQ:4 * TQ] for k in kvs],
                            axis=0)
    ns_rows = 32
    nq = KV_HEADS * TQ
    imp_t = _mm_exact_lhs(ovt_ref[...], p_sum, _NT)[0:ns_rows]
    n_idx = lax.broadcasted_iota(jnp.int32, (ns_rows, nq), 0)
    qb_t = (t0 + lax.broadcasted_iota(jnp.int32, (ns_rows, nq), 1) % TQ) // SEL_BLOCK
    sel_t = _select_blocks(imp_t, n_idx, qb_t, ns_rows)
    sel_t = jnp.concatenate([sel_t, jnp.zeros((LANES - ns_rows, nq), F32)], axis=0).astype(BF16)
    eye = _eye(TQ, BF16)
    sel = [_mm(eye, sel_t[:, k * TQ:(k + 1) * TQ], _NT).astype(BF16) for k in kvs]

    col_s = lax.broadcasted_iota(jnp.int32, (TQ, CK_SLC), 1)
    row_s = lax.broadcasted_iota(jnp.int32, (TQ, CK_SLC), 0)
    n_chunks = (t0 + TQ + CK_SLC - 1) // CK_SLC
    mrun_s[...] = jnp.full(mrun_s.shape, MASK_VALUE, F32)

    def score_body(j, carry):
        k0 = pl.multiple_of(j * CK_SLC, CK_SLC)
        k_c = kvb_ref[pl.ds(k0, CK_SLC), KVB_SK:KVB_SK + LANES]
        causal = k0 + col_s <= t0 + row_s
        e_j = e_ref[j]
        sc = [_mm(q_rot[k], k_c, _NT) for k in kvs]
        picked = [_mm(sel[k], e_j) for k in kvs]
        for k in kvs:
            bias = jnp.where((picked[k] > 0.5) & causal, 0.0, MASK_VALUE)
            sk = sc[k] + _tile_rows(bias, GROUP)
            s_s[k, j] = sk
            mrun_s[k] = jnp.maximum(mrun_s[k], jnp.maximum(sk[:, :LANES], sk[:, LANES:]))
        return carry

    lax.fori_loop(0, n_chunks, score_body, 0)
    for k in kvs:
        m_s[k] = jnp.broadcast_to(jnp.max(mrun_s[k], axis=1, keepdims=True), (rows, LANES))
    acc_s[...] = jnp.zeros(acc_s.shape, F32)

    def value_body(j, carry):
        k0 = pl.multiple_of(j * CK_SLC, CK_SLC)
        v_c = kvb_ref[pl.ds(k0, CK_SLC), KVB_SV:KVB_SV + 2 * LANES]
        pr = [jnp.exp(s_s[k, j] - _tile_lanes(m_s[k], CK_SLC // LANES)) for k in kvs]
        for k in kvs:
            acc_s[k] += _mm(pr[k], v_c)
        return carry

    lax.fori_loop(0, n_chunks, value_body, 0)
    for k in kvs:
        acc = acc_s[k]
        o_sum[k] = o_sum[k] + gate(1, k) * (acc[:, :LANES] / acc[:, LANES:])

    ks0 = pl.multiple_of(jnp.maximum(t0 - WINDOW, 0), TQ)
    k_w = kvb_ref[pl.ds(ks0, WIN_SPAN), KVB_WK:KVB_WK + LANES]
    v_w = kvb_ref[pl.ds(ks0, WIN_SPAN), KVB_WV:KVB_WV + 2 * LANES]
    diff = (t0 + lax.broadcasted_iota(jnp.int32, (TQ, WIN_SPAN), 0)) - (ks0 + lax.broadcasted_iota(jnp.int32, (TQ, WIN_SPAN), 1))
    bias_w = _tile_rows(jnp.where((diff >= 0) & (diff <= WINDOW), 0.0, MASK_VALUE), GROUP)
    sw = [_mm(q_rot[k], k_w, _NT) + bias_w for k in kvs]
    pw = [jnp.exp(sw[k] - jnp.max(sw[k], axis=1, keepdims=True)) for k in kvs]
    rw = [_mm(pw[k], v_w) for k in kvs]
    for k in kvs:
        o_all = o_sum[k] + gate(2, k) * (rw[k][:, :LANES] / rw[k][:, LANES:])
        for pair, slab in enumerate(_merge_heads(o_all, k, TQ)):
            lo = (k * (GROUP // 2) + pair) * LANES
            o_ref[:, lo:lo + LANES] = (slab * _silu(zn_ref[:, lo:lo + LANES])).astype(o_ref.dtype)


def _nsa_prompt(q, qr, kvb, ck, cv, sm, zn, cmp_k_w, cmp_v_w, batch, seq):
    nt = seq // TQ
    ns = seq // SEL_BLOCK
    sub_w = CMP_STRIDE * KV_WIDTH
    keys = np.arange(seq)
    e = (np.arange(LANES)[None, :, None] == (keys // SEL_BLOCK).reshape(seq // CK_SLC, 1, CK_SLC))
    e = jnp.asarray(e, BF16)
    tile = lambda b, i: (b * nt + i, 0)
    per_b = lambda b, i: (b, 0)
    per_b3 = lambda b, i: (b, 0, 0)
    fix2 = lambda b, i: (0, 0)
    fix3 = lambda b, i: (0, 0, 0)
    wspecs = [pl.BlockSpec((sub_w, 4 * CMP_HIDDEN), fix2),
              pl.BlockSpec((2 * CMP_HIDDEN, KV_WIDTH), fix2),
              pl.BlockSpec((SUBLANES, sub_w), fix2)]
    rows = GROUP * TQ
    return pl.pallas_call(
        _nsa_prompt_kernel,
        grid=(batch, nt),
        in_specs=[pl.BlockSpec((TQ, QPAD_WIDTH), tile),
                  pl.BlockSpec((TQ, QPAD_WIDTH), tile),
                  pl.BlockSpec((seq, KVB_WIDTH), per_b),
                  pl.BlockSpec((1, seq // CMP_STRIDE, sub_w), per_b3),
                  pl.BlockSpec((1, seq // CMP_STRIDE, sub_w), per_b3),
                  pl.BlockSpec((TQ, LANES), tile),
                  pl.BlockSpec((TQ, NSA_WIDTH), tile)] + wspecs + wspecs + [
                  pl.BlockSpec((LANES, LANES), fix2),
                  pl.BlockSpec((seq // CK_SLC, LANES, CK_SLC), fix3)],
        out_specs=pl.BlockSpec((TQ, NSA_WIDTH), tile),
        out_shape=jax.ShapeDtypeStruct((batch * seq, NSA_WIDTH), BF16),
        scratch_shapes=[pltpu.VMEM((N_SUB, KV_WIDTH), BF16), pltpu.VMEM((N_SUB, KV_WIDTH), BF16),
                        pltpu.VMEM((KV_HEADS, seq // CK_SLC, rows, CK_SLC), F32),
                        pltpu.VMEM((KV_HEADS, rows, LANES), F32), pltpu.VMEM((KV_HEADS, rows, LANES), F32),
                        pltpu.VMEM((KV_HEADS, rows, 2 * LANES), F32)],
        compiler_params=pltpu.CompilerParams(dimension_semantics=("arbitrary", "arbitrary"),
                                             vmem_limit_bytes=VMEM_LIMIT_BYTES),
        name="nsa_prompt",
    )(q, qr, kvb, ck.reshape(batch, seq // CMP_STRIDE, sub_w), cv.reshape(batch, seq // CMP_STRIDE, sub_w),
      sm, zn, *cmp_k_w, *cmp_v_w, _overlap_t(ns), e)


def _softmax_rows(s):
    m = jnp.max(s, axis=1, keepdims=True)
    e = jnp.exp(s - m)
    return e / jnp.sum(e, axis=1, keepdims=True)


def _nsa_sample_kernel(pt_ref, q_ref, qr_ref, skn_ref, svn_ref, wkn_ref, wvn_ref, sm_ref, zn_ref,
                       wkc_ref, wvc_ref, *rest, n_pages, page, past, n_new, win_keys):
    del pt_ref
    cmpk, cmpv = rest[0:n_pages], rest[n_pages:2 * n_pages]
    slck, slcv = rest[2 * n_pages:3 * n_pages], rest[3 * n_pages:4 * n_pages]
    wk1_ref, wk2_ref, pek_ref, wv1_ref, wv2_ref, pev_ref, ovt_ref, e_ref = rest[4 * n_pages:4 * n_pages + 8]
    o_ref, wko_ref, wvo_ref = rest[4 * n_pages + 8:4 * n_pages + 11]
    kall, vall, kwin, vwin = rest[4 * n_pages + 11:]
    nr = SAMPLE_ROWS
    rows = Q_HEADS * nr
    all_keys = kall.shape[0]
    win_buf = wkc_ref.shape[1]
    zeros8 = jnp.zeros((nr, LANES), F32)

    for p in range(n_pages):
        kall[p * page:(p + 1) * page, :] = slck[p][0].astype(BF16)
        vall[p * page:(p + 1) * page, :] = slcv[p][0].astype(BF16)
    kall[past:past + 2 * nr, :] = jnp.concatenate([skn_ref[...], zeros8], axis=0).astype(BF16)
    vall[past:past + 2 * nr, :] = jnp.concatenate([svn_ref[...], zeros8], axis=0).astype(BF16)
    kall[past + 2 * nr:, :] = jnp.zeros((all_keys - past - 2 * nr, LANES), BF16)
    vall[past + 2 * nr:, :] = jnp.zeros((all_keys - past - 2 * nr, LANES), BF16)
    kwin[0:win_buf, :] = wkc_ref[0].astype(BF16)
    vwin[0:win_buf, :] = wvc_ref[0].astype(BF16)
    kwin[win_buf:win_buf + 2 * nr, :] = jnp.concatenate([wkn_ref[...], zeros8], axis=0).astype(BF16)
    vwin[win_buf:win_buf + 2 * nr, :] = jnp.concatenate([wvn_ref[...], zeros8], axis=0).astype(BF16)
    kwin[win_buf + 2 * nr:, :] = jnp.zeros((win_keys - win_buf - 2 * nr, LANES), BF16)
    vwin[win_buf + 2 * nr:, :] = jnp.zeros((win_keys - win_buf - 2 * nr, LANES), BF16)

    ck = _compress(jnp.concatenate([r[0] for r in cmpk], axis=0), wk1_ref, wk2_ref, pek_ref)
    cv = _compress(jnp.concatenate([r[0] for r in cmpv], axis=0), wv1_ref, wv2_ref, pev_ref)

    q_raw = jnp.concatenate([q_ref[:, h * LANES:(h + 1) * LANES] for h in range(Q_HEADS)], axis=0)
    q_rot = jnp.concatenate([qr_ref[:, h * LANES:(h + 1) * LANES] for h in range(Q_HEADS)], axis=0)

    tok = lax.broadcasted_iota(jnp.int32, (nr, LANES), 0)
    col = lax.broadcasted_iota(jnp.int32, (nr, LANES), 1)
    cvalid = (CMP_STRIDE * col + (CMP_LEN - 1) <= past + tok) & (col < N_CMP)
    o_cmp, p = _cmp_attention(q_raw, ck, cv, cvalid, Q_HEADS)

    p_sum = []
    for kvh in range(KV_HEADS):
        base = kvh * GROUP * nr
        p_sum.append(sum(p[base + g * nr:base + (g + 1) * nr] for g in range(GROUP)))
    p_sum = jnp.concatenate(p_sum, axis=0)
    ns = (past + n_new + SEL_BLOCK - 1) // SEL_BLOCK
    ns_rows = -(-ns // SUBLANES) * SUBLANES
    nq = KV_HEADS * nr
    imp_t = _mm_exact_lhs(ovt_ref[...], p_sum, _NT)[0:ns_rows]
    n_idx = lax.broadcasted_iota(jnp.int32, (ns_rows, nq), 0)
    qb_t = (past + lax.broadcasted_iota(jnp.int32, (ns_rows, nq), 1) % nr) // SEL_BLOCK
    sel_t = _select_blocks(imp_t, n_idx, qb_t, ns)
    sel_t = jnp.concatenate([sel_t, jnp.zeros((LANES - ns_rows, nq), F32)], axis=0)
    sel = _mm(_eye(nq, BF16), sel_t, _NT)
    sel_rows = jnp.concatenate([_tile_rows(sel[kvh * nr:(kvh + 1) * nr], GROUP) for kvh in range(KV_HEADS)], axis=0)

    picked = _mm(sel_rows, e_ref[...])
    q_pos = past + lax.broadcasted_iota(jnp.int32, (rows, all_keys), 0) % nr
    k_pos = lax.broadcasted_iota(jnp.int32, (rows, all_keys), 1)
    ok = (picked > 0.5) & (k_pos <= q_pos)
    s = _mm(q_rot, kall[...], _NT) + jnp.where(ok, 0.0, MASK_VALUE)
    o_slc = _mm(_softmax_rows(s), vall[...])

    q_pos = past + lax.broadcasted_iota(jnp.int32, (rows, win_keys), 0) % nr
    k_pos = past - win_buf + lax.broadcasted_iota(jnp.int32, (rows, win_keys), 1)
    diff = q_pos - k_pos
    ok = (diff >= 0) & (diff <= WINDOW)
    s = _mm(q_rot, kwin[...], _NT) + jnp.where(ok, 0.0, MASK_VALUE)
    o_win = _mm(_softmax_rows(s), vwin[...])

    sig = _sigmoid(sm_ref[...])

    def gate(branch):
        cols = [sig[:, branch * Q_HEADS + h:branch * Q_HEADS + h + 1] for h in range(Q_HEADS)]
        return jnp.concatenate([jnp.broadcast_to(c, (nr, LANES)) for c in cols], axis=0)

    o_all = gate(0) * o_cmp + gate(1) * o_slc + gate(2) * o_win
    for kvh in range(KV_HEADS):
        o_kv = o_all[kvh * GROUP * nr:(kvh + 1) * GROUP * nr]
        for pair, slab in enumerate(_merge_heads(o_kv, kvh, nr)):
            lo = (kvh * (GROUP // 2) + pair) * LANES
            o_ref[:, lo:lo + LANES] = (slab * _silu(zn_ref[:, lo:lo + LANES])).astype(o_ref.dtype)

    for cache_ref, new_ref, out_ref in ((wkc_ref, wkn_ref, wko_ref), (wvc_ref, wvn_ref, wvo_ref)):
        shifted = pltpu.roll(cache_ref[0], win_buf - n_new, 0)
        out_ref[0] = shifted
        tail = jnp.where(tok < nr - n_new, shifted[win_buf - nr:], pltpu.roll(new_ref[...], nr - n_new, 0))
        out_ref[0, win_buf - nr:, :] = tail


def _nsa_sample(q, qr, sk, sv, wk, wv, sm, zn, win_k, win_v, cmp_k, cmp_v, slc_k, slc_v, page_table,
                cmp_k_w, cmp_v_w, n_new):
    batch, n_pages = page_table.shape
    n_pool, page = cmp_k.shape[0], cmp_k.shape[1]
    past = n_pages * page
    nr = SAMPLE_ROWS
    sub_w = CMP_STRIDE * KV_WIDTH
    win_buf = win_k.shape[1]
    all_keys = -(-(past + 2 * nr) // LANES) * LANES
    win_keys = -(-(win_buf + 2 * nr) // LANES) * LANES
    ns = (past + n_new + SEL_BLOCK - 1) // SEL_BLOCK
    e = jnp.asarray(np.arange(LANES)[:, None] == (np.arange(all_keys) // SEL_BLOCK)[None, :], BF16)
    cmp_k = cmp_k.reshape(n_pool, page // CMP_STRIDE, sub_w)
    cmp_v = cmp_v.reshape(n_pool, page // CMP_STRIDE, sub_w)
    slc_k = slc_k.reshape(n_pool, page, KV_WIDTH)
    slc_v = slc_v.reshape(n_pool, page, KV_WIDTH)
    win_k = win_k.reshape(batch, win_buf, KV_WIDTH)
    win_v = win_v.reshape(batch, win_buf, KV_WIDTH)

    row = lambda b, pt: (b, 0)
    per_b3 = lambda b, pt: (b, 0, 0)
    fix2 = lambda b, pt: (0, 0)
    page_map = lambda p: (lambda b, pt: (pt[b, p], 0, 0))
    wspecs = [pl.BlockSpec((sub_w, 4 * CMP_HIDDEN), fix2),
              pl.BlockSpec((2 * CMP_HIDDEN, KV_WIDTH), fix2),
              pl.BlockSpec((SUBLANES, sub_w), fix2)]
    in_specs = ([pl.BlockSpec((nr, QPAD_WIDTH), row)] * 2 + [pl.BlockSpec((nr, KV_WIDTH), row)] * 4
                + [pl.BlockSpec((nr, LANES), row), pl.BlockSpec((nr, NSA_WIDTH), row)]
                + [pl.BlockSpec((1, win_buf, KV_WIDTH), per_b3)] * 2
                + [pl.BlockSpec((1, page // CMP_STRIDE, sub_w), page_map(p)) for p in range(n_pages)] * 1
                + [pl.BlockSpec((1, page // CMP_STRIDE, sub_w), page_map(p)) for p in range(n_pages)]
                + [pl.BlockSpec((1, page, KV_WIDTH), page_map(p)) for p in range(n_pages)]
                + [pl.BlockSpec((1, page, KV_WIDTH), page_map(p)) for p in range(n_pages)]
                + wspecs + wspecs
                + [pl.BlockSpec((LANES, LANES), fix2), pl.BlockSpec((LANES, all_keys), fix2)])
    grid_spec = pltpu.PrefetchScalarGridSpec(
        num_scalar_prefetch=1,
        grid=(batch,),
        in_specs=in_specs,
        out_specs=[pl.BlockSpec((nr, NSA_WIDTH), row),
                   pl.BlockSpec((1, win_buf, KV_WIDTH), per_b3),
                   pl.BlockSpec((1, win_buf, KV_WIDTH), per_b3)],
        scratch_shapes=[pltpu.VMEM((all_keys, KV_WIDTH), BF16), pltpu.VMEM((all_keys, KV_WIDTH), BF16),
                        pltpu.VMEM((win_keys, KV_WIDTH), BF16), pltpu.VMEM((win_keys, KV_WIDTH), BF16)])
    return pl.pallas_call(
        functools.partial(_nsa_sample_kernel, n_pages=n_pages, page=page, past=past, n_new=n_new,
                          win_keys=win_keys),
        grid_spec=grid_spec,
        out_shape=[jax.ShapeDtypeStruct((batch * nr, NSA_WIDTH), F32),
                   jax.ShapeDtypeStruct((batch, win_buf, KV_WIDTH), F32),
                   jax.ShapeDtypeStruct((batch, win_buf, KV_WIDTH), F32)],
        compiler_params=pltpu.CompilerParams(dimension_semantics=("arbitrary",),
                                             vmem_limit_bytes=VMEM_LIMIT_BYTES),
        name="nsa_sample",
    )(page_table, q, qr, sk, sv, wk, wv, sm, zn, win_k, win_v,
      *([cmp_k] * n_pages), *([cmp_v] * n_pages), *([slc_k] * n_pages), *([slc_v] * n_pages),
      *cmp_k_w, *cmp_v_w, _overlap_t(ns), e)


def _compress_weights_t(pe, w1, w2):
    w1big, w2big, pe2 = _compress_weights(pe, w1, w2)
    return w1big.T, w2big.T, pe2


def _nsa_sample_t_kernel(pt_ref, q_ref, qr_ref, skn_ref, svn_ref, wkn_ref, wvn_ref, sm_ref, zn_ref,
                         wkc_ref, wvc_ref, *rest, nb, n_pages, page, past, n_new):
    del pt_ref
    npg = nb * n_pages
    cmpk, cmpv, slck, slcv = (rest[i * npg:(i + 1) * npg] for i in range(4))
    w1k_ref, w2k_ref, pek_ref, w1v_ref, w2v_ref, pev_ref, ovt_ref, e_ref = rest[4 * npg:4 * npg + 8]
    o_ref = rest[4 * npg + 8]
    kt_s, vt_s, kwt_s, vwt_s = rest[4 * npg + 9:]
    nr = SAMPLE_ROWS
    rows = Q_HEADS * nr
    win_buf = wkc_ref.shape[2]
    bs = range(nb)
    hid = 2 * CMP_HIDDEN
    zpad = jnp.zeros((LANES - nr, LANES), F32)

    def new_rows(ref, b):
        return jnp.concatenate([ref[b * nr:(b + 1) * nr, :], zpad], axis=0).astype(BF16)

    def stack(ref, b):
        return jnp.concatenate([ref[b * nr:(b + 1) * nr, h * LANES:(h + 1) * LANES] for h in range(Q_HEADS)],
                               axis=0).astype(BF16)

    for b in bs:
        for p in range(n_pages):
            kt_s[b, :, p * page:(p + 1) * page] = slck[b * n_pages + p][0].astype(BF16)
            vt_s[b, :, p * page:(p + 1) * page] = slcv[b * n_pages + p][0].astype(BF16)
        kwt_s[b] = wkc_ref[b].astype(BF16)
        vwt_s[b] = wvc_ref[b].astype(BF16)
    mi = lax.broadcasted_iota(jnp.int32, (page, page), 0)
    ki = lax.broadcasted_iota(jnp.int32, (page, page), 1)
    per_page = page // CMP_STRIDE
    perm = (ki == CMP_STRIDE * (mi % per_page) + mi // per_page).astype(BF16)
    rows_t = [[[_mm(perm, pages[b * n_pages + p][0], _NT) for p in range(n_pages)] for pages in (cmpk, cmpv)]
              for b in bs]

    q_raw = [stack(q_ref, b) for b in bs]
    q_rot = [stack(qr_ref, b) for b in bs]

    sw = [jnp.concatenate([_mm(q_rot[b], kwt_s[b]), _mm(q_rot[b], new_rows(wkn_ref, b), _NT)], axis=1) for b in bs]
    ss = [jnp.concatenate([_mm(q_rot[b], kt_s[b]), _mm(q_rot[b], new_rows(skn_ref, b), _NT)], axis=1) for b in bs]

    def sub_blocks(b, c):
        cols = [jnp.concatenate([rows_t[b][c][p][l * per_page:(l + 1) * per_page] for p in range(n_pages)], axis=0)
                for l in range(CMP_STRIDE)]
        return jnp.concatenate(cols, axis=1)

    ckt, cvt = [], []
    for c, (w1_ref, w2_ref, pe_ref, out) in enumerate(((w1k_ref, w2k_ref, pek_ref, ckt), (w1v_ref, w2v_ref, pev_ref, cvt))):
        pe = pe_ref[...]
        subs = [sub_blocks(b, c) for b in bs]
        h_a = [_mm(w1_ref[0:hid, :], subs[b] + pe[0:1, :], _NT) for b in bs]
        h_b = [_mm(w1_ref[hid:2 * hid, :], subs[b] + pe[1:2, :], _NT) for b in bs]
        h = [_silu(h_a[b] + pltpu.roll(h_b[b], N_SUB - 1, 1)) for b in bs]
        out.extend(_mm(w2_ref[...], h[b]) for b in bs)

    tok = lax.broadcasted_iota(jnp.int32, (nr, LANES), 0)
    col = lax.broadcasted_iota(jnp.int32, (nr, LANES), 1)
    cm = _tile_rows((CMP_STRIDE * col + (CMP_LEN - 1) <= past + tok) & (col < N_CMP), Q_HEADS)
    cm_f = cm.astype(F32)
    sc = [jnp.where(cm, _mm(q_raw[b], ckt[b]), MASK_VALUE) for b in bs]
    ec = [jnp.exp(sc[b] - jnp.max(sc[b], axis=1, keepdims=True)) for b in bs]
    pc = [ec[b] / jnp.sum(ec[b], axis=1, keepdims=True) * cm_f for b in bs]
    o_cmp = [_mm(pc[b], cvt[b], _NT) for b in bs]
    p_sum = jnp.concatenate(
        [sum(pc[b][(kvh * GROUP + g) * nr:(kvh * GROUP + g + 1) * nr] for g in range(GROUP))
         for b in bs for kvh in range(KV_HEADS)], axis=0)
    ns = (past + n_new + SEL_BLOCK - 1) // SEL_BLOCK
    ns_rows = -(-ns // SUBLANES) * SUBLANES
    nq = nb * KV_HEADS * nr
    imp_t = _mm_exact_lhs(ovt_ref[...], p_sum, _NT)[0:ns_rows]
    n_idx = lax.broadcasted_iota(jnp.int32, (ns_rows, nq), 0)
    qb_t = (past + lax.broadcasted_iota(jnp.int32, (ns_rows, nq), 1) % nr) // SEL_BLOCK
    sel_t = _select_blocks(imp_t, n_idx, qb_t, ns)
    sel_t = jnp.concatenate([sel_t, jnp.zeros((LANES - ns_rows, nq), F32)], axis=0)
    sel = _mm(_eye(nq, BF16), sel_t, _NT)

    all_keys = past + LANES
    q_pos = past + lax.broadcasted_iota(jnp.int32, (rows, all_keys), 0) % nr
    causal = lax.broadcasted_iota(jnp.int32, (rows, all_keys), 1) <= q_pos
    o_slc = []
    for b in bs:
        sel_rows = jnp.concatenate(
            [_tile_rows(sel[(b * KV_HEADS + kvh) * nr:(b * KV_HEADS + kvh + 1) * nr], GROUP) for kvh in range(KV_HEADS)],
            axis=0)
        ok = (_mm(sel_rows, e_ref[...]) > 0.5) & causal
        s = ss[b] + jnp.where(ok, 0.0, MASK_VALUE)
        pr = jnp.exp(s - jnp.max(s, axis=1, keepdims=True))
        pr = pr / jnp.sum(pr, axis=1, keepdims=True)
        o_slc.append(_mm(pr[:, :past], vt_s[b], _NT) + _mm(pr[:, past:], new_rows(svn_ref, b)))

    win_keys = win_buf + LANES
    q_pos = past + lax.broadcasted_iota(jnp.int32, (rows, win_keys), 0) % nr
    diff = q_pos - (past - win_buf + lax.broadcasted_iota(jnp.int32, (rows, win_keys), 1))
    bias_w = jnp.where((diff >= 0) & (diff <= WINDOW), 0.0, MASK_VALUE)
    o_win = []
    for b in bs:
        s = sw[b] + bias_w
        pr = jnp.exp(s - jnp.max(s, axis=1, keepdims=True))
        pr = pr / jnp.sum(pr, axis=1, keepdims=True)
        o_win.append(_mm(pr[:, :win_buf], vwt_s[b], _NT) + _mm(pr[:, win_buf:], new_rows(wvn_ref, b)))

    for b in bs:
        sig = _sigmoid(sm_ref[b * nr:(b + 1) * nr, :])

        def gate(branch):
            cols = [sig[:, branch * Q_HEADS + h:branch * Q_HEADS + h + 1] for h in range(Q_HEADS)]
            return jnp.concatenate([jnp.broadcast_to(c, (nr, LANES)) for c in cols], axis=0)

        o_all = gate(0) * o_cmp[b] + gate(1) * o_slc[b] + gate(2) * o_win[b]
        for kvh in range(KV_HEADS):
            o_kv = o_all[kvh * GROUP * nr:(kvh + 1) * GROUP * nr]
            for pair, slab in enumerate(_merge_heads(o_kv, kvh, nr)):
                lo = (kvh * (GROUP // 2) + pair) * LANES
                o_ref[b * nr:(b + 1) * nr, lo:lo + LANES] = slab * _silu(zn_ref[b * nr:(b + 1) * nr, lo:lo + LANES])


def _nsa_sample_t(q, qr, sk, sv, wk, wv, sm, zn, win_k, win_v, cmp_k, cmp_v, slc_k, slc_v, page_table,
                  cmp_k_w, cmp_v_w, n_new, nb):
    batch, n_pages = page_table.shape
    n_pool, page = cmp_k.shape[0], cmp_k.shape[1]
    past = n_pages * page
    nr = SAMPLE_ROWS
    win_buf = win_k.shape[1]
    ns = (past + n_new + SEL_BLOCK - 1) // SEL_BLOCK
    all_keys = past + LANES
    e = jnp.asarray(np.arange(LANES)[:, None] == (np.arange(all_keys) // SEL_BLOCK)[None, :], BF16)
    tview = lambda c: jnp.transpose(c, (0, 2, 3, 1)).reshape(c.shape[0], KV_WIDTH, c.shape[1])
    cmp_k, cmp_v, slc_k, slc_v, win_k, win_v = (tview(c) for c in (cmp_k, cmp_v, slc_k, slc_v, win_k, win_v))

    row = lambda b, pt: (b, 0)
    per_b3 = lambda b, pt: (b, 0, 0)
    fix2 = lambda b, pt: (0, 0)
    page_map = lambda j, p: (lambda b, pt: (pt[b * nb + j, p], 0, 0))
    page_specs = [pl.BlockSpec((1, KV_WIDTH, page), page_map(j, p)) for j in range(nb) for p in range(n_pages)]
    sub_w = CMP_STRIDE * KV_WIDTH
    wspecs = [pl.BlockSpec((4 * CMP_HIDDEN, sub_w), fix2),
              pl.BlockSpec((KV_WIDTH, 2 * CMP_HIDDEN), fix2),
              pl.BlockSpec((SUBLANES, sub_w), fix2)]
    in_specs = ([pl.BlockSpec((nb * nr, QPAD_WIDTH), row)] * 2 + [pl.BlockSpec((nb * nr, KV_WIDTH), row)] * 4
                + [pl.BlockSpec((nb * nr, LANES), row), pl.BlockSpec((nb * nr, NSA_WIDTH), row)]
                + [pl.BlockSpec((nb, KV_WIDTH, win_buf), per_b3)] * 2
                + page_specs * 4 + wspecs + wspecs
                + [pl.BlockSpec((LANES, LANES), fix2), pl.BlockSpec((LANES, all_keys), fix2)])
    grid_spec = pltpu.PrefetchScalarGridSpec(
        num_scalar_prefetch=1,
        grid=(batch // nb,),
        in_specs=in_specs,
        out_specs=pl.BlockSpec((nb * nr, NSA_WIDTH), row),
        scratch_shapes=[pltpu.VMEM((nb, KV_WIDTH, past), BF16), pltpu.VMEM((nb, KV_WIDTH, past), BF16),
                        pltpu.VMEM((nb, KV_WIDTH, win_buf), BF16), pltpu.VMEM((nb, KV_WIDTH, win_buf), BF16)])
    npg = nb * n_pages
    return pl.pallas_call(
        functools.partial(_nsa_sample_t_kernel, nb=nb, n_pages=n_pages, page=page, past=past, n_new=n_new),
        grid_spec=grid_spec,
        out_shape=jax.ShapeDtypeStruct((batch * nr, NSA_WIDTH), F32),
        compiler_params=pltpu.CompilerParams(dimension_semantics=("arbitrary",),
                                             vmem_limit_bytes=VMEM_LIMIT_BYTES),
        name="nsa_sample",
    )(page_table, q, qr, sk, sv, wk, wv, sm, zn, win_k, win_v,
      *([cmp_k] * npg), *([cmp_v] * npg), *([slc_k] * npg), *([slc_v] * npg),
      *cmp_k_w, *cmp_v_w, _overlap_t(ns), e)


def _unit_lower_inverse(a, n_valid, hi):
    c = a.shape[0]
    inv = _eye(c, F32) - a
    power = a
    span = 2
    while span < n_valid:
        power = _mm(power, power, hi=hi)
        inv = inv + _mm(inv, power, hi=hi)
        span *= 2
    return inv


def _gdn_kernel(qkv_ref, sm_ref, zg_ref, conv0_ref, s0_ref, wc_ref, vec_ref, wg_ref,
                go_ref, xp_out_ref, st_ref, xp_s, *, bblk, chunk, n_valid, hi):
    c_idx = pl.program_id(1)

    @pl.when(c_idx == 0)
    def _():
        xp_s[:, 0:SUBLANES, :] = conv0_ref[...]
        st_ref[...] = s0_ref[...]

    wc = wc_ref[...]
    vec = vec_ref[...]
    row1 = lax.broadcasted_iota(jnp.int32, (chunk, LANES), 0)
    rr = lax.broadcasted_iota(jnp.int32, (chunk, chunk), 0)
    cc = lax.broadcasted_iota(jnp.int32, (chunk, chunk), 1)
    tri = rr >= cc
    tri_bf = tri.astype(BF16)
    eye_bf = _eye(LANES, BF16)

    acts, betas, decays, decay_ts, e_decs = [], [], [], [], []
    for b in range(bblk):
        xp_s[b, SUBLANES:SUBLANES + chunk, :] = qkv_ref[b]
        y = xp_s[b, SUBLANES:SUBLANES + chunk, :] * wc[GDN_CONV - 1:GDN_CONV, :]
        for j in range(GDN_CONV - 1):
            lo = SUBLANES - (GDN_CONV - 1) + j
            y = y + xp_s[b, lo:lo + chunk, :] * wc[j:j + 1, :]
        xp_out_ref[b] = xp_s[b]
        xp_s[b, 0:SUBLANES, :] = xp_s[b, chunk:chunk + SUBLANES, :]
        act = _silu(y)
        small = sm_ref[b]
        z = small + vec[1:2, :]
        softplus = jnp.maximum(z, 0.0) + jnp.log1p(jnp.exp(-jnp.abs(z)))
        g_all = -jnp.exp(vec[0:1, :]) * softplus
        if n_valid < chunk:
            valid = row1 < n_valid
            act = act * _tile_lanes(valid.astype(F32), GDN_CONV_CH // LANES)
            g_all = jnp.where(valid, g_all, 0.0)
        acts.append(act)
        betas.append(_sigmoid(small))
        decays.append(_mm_exact_lhs(tri_bf, g_all))
    for b in range(bblk):
        decay_ts.append(_mm_exact_lhs(eye_bf, decays[b], _NT))
        e_decs.append(jnp.exp(decays[b]))

    chains = [(b, h) for b in range(bblk) for h in range(GDN_HEADS)]
    qs, ks, kbs, dmasks, rhs_u, rhs_w, qds, kds, gls = [], [], [], [], [], [], [], [], []
    for b, h in chains:
        act = acts[b]
        qh = act[:, h * GDN_DK:(h + 1) * GDN_DK]
        kh = act[:, GDN_WIDTH + h * GDN_DK:GDN_WIDTH + (h + 1) * GDN_DK]
        vh = act[:, 2 * GDN_WIDTH + h * GDN_DV:2 * GDN_WIDTH + (h + 1) * GDN_DV]
        qh = qh * lax.rsqrt(jnp.sum(qh * qh, axis=-1, keepdims=True) + NORM_EPS) * (GDN_DK ** -0.5)
        kh = kh * lax.rsqrt(jnp.sum(kh * kh, axis=-1, keepdims=True) + NORM_EPS)
        beta = betas[b][:, SM_B + h:SM_B + h + 1]
        dcol = decays[b][:, SM_A + h:SM_A + h + 1]
        drow = decay_ts[b][SM_A + h:SM_A + h + 1, :]
        ed = e_decs[b][:, SM_A + h:SM_A + h + 1]
        dlast = decays[b][chunk - 1:chunk, SM_A + h:SM_A + h + 1]
        kb = kh * beta
        qs.append(qh)
        ks.append(kh)
        kbs.append(kb)
        dmasks.append(jnp.where(tri, jnp.exp(jnp.where(tri, dcol - drow, 0.0)), 0.0))
        rhs_u.append(vh * beta)
        rhs_w.append(kb * ed)
        qds.append(qh * ed)
        kds.append(kh * jnp.exp(dlast - dcol))
        gls.append(jnp.exp(dlast))

    n = len(chains)
    eye_c = _eye(chunk, F32)
    kkts = [_mm(kbs[i], ks[i], _NT, hi=hi) for i in range(n)]
    qks = [_mm(qs[i], ks[i], _NT, hi=hi) for i in range(n)]
    powers = [jnp.where(rr > cc, kkts[i] * dmasks[i], 0.0) for i in range(n)]
    qks = [qks[i] * dmasks[i] for i in range(n)]
    invs = [eye_c - powers[i] for i in range(n)]
    span = 2
    while span < n_valid:
        powers = [_mm(powers[i], powers[i], hi=hi) for i in range(n)]
        invs = [invs[i] + _mm(invs[i], powers[i], hi=hi) for i in range(n)]
        span *= 2
    us = [_mm(invs[i], rhs_u[i], hi=hi) for i in range(n)]
    ws = [_mm(invs[i], rhs_w[i], hi=hi) for i in range(n)]
    sts = [st_ref[b, h] for b, h in chains]
    v_news = [us[i] - _mm(ws[i], sts[i], hi=hi) for i in range(n)]
    os_ = [_mm(qds[i], sts[i], hi=hi) for i in range(n)]
    os_ = [os_[i] + _mm(qks[i], v_news[i], hi=hi) for i in range(n)]
    upd = [_mm(kds[i], v_news[i], _TN, hi=hi) for i in range(n)]
    for i, (b, h) in enumerate(chains):
        st_ref[b, h] = sts[i] * gls[i] + upd[i]
        o = os_[i]
        o = o * lax.rsqrt(jnp.mean(o * o, axis=-1, keepdims=True) + NORM_EPS) * wg_ref[...]
        o = o * _silu(zg_ref[b, :, h * GDN_DV:(h + 1) * GDN_DV])
        go_ref[b, :, h * GDN_DV:(h + 1) * GDN_DV] = o.astype(go_ref.dtype)


def _gdn(qkv, sm, zg, conv0, s0, w_conv, a_log, dt_bias, w_gnorm, batch, rows, bblk, chunk, n_valid, hi,
         out_dtype):
    nc = rows // chunk
    tile = lambda b, c: (b, c, 0)
    per_b3 = lambda b, c: (b, 0, 0)
    per_b4 = lambda b, c: (b, 0, 0, 0)
    fix2 = lambda b, c: (0, 0)
    wc = jnp.concatenate([w_conv, jnp.zeros((SUBLANES - GDN_CONV, GDN_CONV_CH), w_conv.dtype)], axis=0)
    vec = jnp.zeros((SUBLANES, LANES), F32)
    vec = vec.at[0, SM_A:SM_A + GDN_HEADS].set(a_log).at[1, SM_A:SM_A + GDN_HEADS].set(dt_bias)
    go, xp, st = pl.pallas_call(
        functools.partial(_gdn_kernel, bblk=bblk, chunk=chunk, n_valid=n_valid, hi=hi),
        grid=(batch // bblk, nc),
        in_specs=[pl.BlockSpec((bblk, chunk, GDN_CONV_CH), tile),
                  pl.BlockSpec((bblk, chunk, LANES), tile),
                  pl.BlockSpec((bblk, chunk, GDN_WIDTH), tile),
                  pl.BlockSpec((bblk, SUBLANES, GDN_CONV_CH), per_b3),
                  pl.BlockSpec((bblk, GDN_HEADS, GDN_DK, GDN_DV), per_b4),
                  pl.BlockSpec((SUBLANES, GDN_CONV_CH), fix2),
                  pl.BlockSpec((SUBLANES, LANES), fix2),
                  pl.BlockSpec((1, GDN_DV), fix2)],
        out_specs=[pl.BlockSpec((bblk, chunk, GDN_WIDTH), tile),
                   pl.BlockSpec((bblk, SUBLANES + chunk, GDN_CONV_CH), per_b3),
                   pl.BlockSpec((bblk, GDN_HEADS, GDN_DK, GDN_DV), per_b4)],
        out_shape=[jax.ShapeDtypeStruct((batch, rows, GDN_WIDTH), out_dtype),
                   jax.ShapeDtypeStruct((batch, SUBLANES + chunk, GDN_CONV_CH), F32),
                   jax.ShapeDtypeStruct((batch, GDN_HEADS, GDN_DK, GDN_DV), F32)],
        scratch_shapes=[pltpu.VMEM((bblk, SUBLANES + chunk, GDN_CONV_CH), F32)],
        compiler_params=pltpu.CompilerParams(dimension_semantics=("arbitrary", "arbitrary"),
                                             vmem_limit_bytes=VMEM_LIMIT_BYTES),
        name="gdn",
    )(qkv.reshape(batch, rows, GDN_CONV_CH), sm.reshape(batch, rows, LANES), zg.reshape(batch, rows, GDN_WIDTH),
      conv0, s0, wc, vec, w_gnorm.reshape(1, GDN_DV))
    return go.reshape(batch * rows, GDN_WIDTH), xp, st


PROMPT_TM = 512
SAMPLE_TM = 128
GDN_PROMPT_BBLK = 4
GDN_SAMPLE_BBLK = 8
NSA_SAMPLE_NB = 2


def _layer_prompt(h, lw, final, w_final, win_buf):
    w_norm, w_pack, cmp_k_w, cmp_v_w, w_conv, a_log, dt_bias, w_gnorm, w_out = lw
    batch, seq, d = h.shape
    x2d = h.reshape(batch * seq, d)
    tables = _rope_tables(jnp.arange(seq, dtype=jnp.int32))
    (q, qr, ck, cv, sk, sv, wk, wv, kvb, zn, qkv, zg, sm) = _project(
        x2d, w_norm, w_pack.astype(BF16), tables, seq, PROMPT_TM, False, BF16)
    nsa = _nsa_prompt(q, qr, kvb, ck, cv, sm, zn, cmp_k_w, cmp_v_w, batch, seq)
    conv0 = jnp.zeros((batch, SUBLANES, GDN_CONV_CH), F32)
    s0 = jnp.zeros((batch, GDN_HEADS, GDN_DK, GDN_DV), F32)
    go, xp, st = _gdn(qkv, sm, zg, conv0, s0, w_conv, a_log, dt_bias, w_gnorm,
                      batch, seq, GDN_PROMPT_BBLK, GDN_CHUNK, GDN_CHUNK, False, BF16)
    y = _out_project(x2d, nsa, go, w_out.astype(BF16), w_final, PROMPT_TM, False, final)
    kv4 = lambda t: t.reshape(batch, seq, KV_HEADS, HEAD_DIM)
    lead = ((0, 0), (max(win_buf - seq, 0), 0), (0, 0), (0, 0))
    win = lambda t: jnp.pad(kv4(t), lead)[:, -win_buf:]
    conv_new = xp[:, SUBLANES + GDN_CHUNK - (GDN_CONV - 1):SUBLANES + GDN_CHUNK]
    return y.reshape(batch, seq, d), (kv4(ck), kv4(cv), kv4(sk), kv4(sv), win(wk), win(wv), conv_new, st)


def _layer_sample(h8, n_new, caches, page_table, lw, final, w_final):
    w_norm, w_pack, cmp_k_w, cmp_v_w, w_conv, a_log, dt_bias, w_gnorm, w_out = lw
    cmp_k_t = (cmp_k_w[0].T, cmp_k_w[1].T, cmp_k_w[2])
    cmp_v_t = (cmp_v_w[0].T, cmp_v_w[1].T, cmp_v_w[2])
    c_cmp_k, c_cmp_v, c_slc_k, c_slc_v, c_win_k, c_win_v, s_conv, s_gdn = caches
    batch, nr, d = h8.shape
    past = page_table.shape[1] * c_cmp_k.shape[1]
    x2d = h8.reshape(batch * nr, d)
    tables = _rope_tables(past + jnp.arange(nr, dtype=jnp.int32))
    tables = tuple(jnp.tile(t, (SAMPLE_TM // nr, 1)) for t in tables)
    (q, qr, ck, cv, sk, sv, wk, wv, _, zn, qkv, zg, sm) = _project(
        x2d, w_norm, w_pack, tables, SAMPLE_TM, SAMPLE_TM, True, F32)
    nsa = _nsa_sample_t(q, qr, sk, sv, wk, wv, sm, zn, c_win_k, c_win_v, c_cmp_k, c_cmp_v, c_slc_k, c_slc_v,
                        page_table, cmp_k_t, cmp_v_t, n_new, NSA_SAMPLE_NB)
    conv0 = jnp.pad(s_conv, ((0, 0), (SUBLANES - (GDN_CONV - 1), 0), (0, 0)))
    go, xp, st = _gdn(qkv, sm, zg, conv0, s_gdn, w_conv, a_log, dt_bias, w_gnorm,
                      batch, nr, GDN_SAMPLE_BBLK, nr, n_new, True, F32)
    y = _out_project(x2d, nsa, go, w_out, w_final, SAMPLE_TM, True, final)
    kv4 = lambda t: t.reshape(batch, nr, KV_HEADS, HEAD_DIM)[:, :n_new]
    win_k = jnp.concatenate([c_win_k[:, n_new:], kv4(wk)], axis=1)
    win_v = jnp.concatenate([c_win_v[:, n_new:], kv4(wv)], axis=1)
    conv_new = xp[:, SUBLANES + n_new - (GDN_CONV - 1):SUBLANES + n_new]
    return y.reshape(batch, nr, d), (kv4(ck), kv4(cv), kv4(sk), kv4(sv), win_k, win_v, conv_new, st)


def kernel(x_prompt, x_sample, cache_cmp_k, cache_cmp_v, cache_slc_k, cache_slc_v, cache_win_k, cache_win_v, state_conv, state_gdn, page_table, w_norm, w_in, pe_cmp_k, w_cmp_k1, w_cmp_k2, pe_cmp_v, w_cmp_v1, w_cmp_v2, w_conv, a_log, dt_bias, w_gdn_norm, w_out, w_final_norm):
    depth = w_in.shape[0]
    n_new = x_sample.shape[1]
    win_buf = cache_win_k.shape[2]
    h_p = x_prompt
    h_s = jnp.pad(x_sample, ((0, 0), (0, SAMPLE_ROWS - n_new), (0, 0)))
    st_p, st_s = [], []
    for layer in range(depth):
        lw = (w_norm[layer], _pack_w_in(w_in[layer]),
              _compress_weights(pe_cmp_k[layer], w_cmp_k1[layer], w_cmp_k2[layer]),
              _compress_weights(pe_cmp_v[layer], w_cmp_v1[layer], w_cmp_v2[layer]),
              w_conv[layer], a_log[layer], dt_bias[layer], w_gdn_norm[layer], w_out[layer])
        final = layer == depth - 1
        h_p, sp = _layer_prompt(h_p, lw, final, w_final_norm, win_buf)
        caches = (cache_cmp_k[layer], cache_cmp_v[layer], cache_slc_k[layer], cache_slc_v[layer],
                  cache_win_k[layer], cache_win_v[layer], state_conv[layer], state_gdn[layer])
        h_s, ss = _layer_sample(h_s, n_new, caches, page_table, lw, final, w_final_norm)
        st_p.append(sp)
        st_s.append(ss)
    outs = [h_p, h_s[:, :n_new]]
    for i in range(8):
        outs.append(jnp.stack([s[i] for s in st_p]))
        outs.append(jnp.stack([s[i] for s in st_s]))
    return tuple(outs)
```

```python
import functools

import numpy as np
import jax
import jax.numpy as jnp
from jax import lax
from jax.experimental import pallas as pl
from jax.experimental.pallas import tpu as pltpu

F32 = jnp.float32
BF16 = jnp.bfloat16

D_MODEL = 1024
HEAD_DIM = 64
Q_HEADS = 8
KV_HEADS = 2
GROUP = Q_HEADS // KV_HEADS
NSA_WIDTH = Q_HEADS * HEAD_DIM
KV_WIDTH = KV_HEADS * HEAD_DIM
CMP_LEN = 32
CMP_STRIDE = 16
CMP_HIDDEN = 128
SEL_BLOCK = 64
TOP_N = 8
WINDOW = 512
FORCE_BONUS = 1.0e4
ROT_DIM = HEAD_DIM // 4
ROPE_THETA = 500000.0
GDN_DK = 128
GDN_DV = 128
GDN_HEADS = 4
GDN_WIDTH = GDN_HEADS * GDN_DV
GDN_CONV = 4
GDN_CONV_CH = 3 * GDN_WIDTH
GDN_CHUNK = 64
NORM_EPS = 1e-6
MASK_VALUE = -1e30

LANES = 128
SUBLANES = 8
VMEM_LIMIT_BYTES = 56 * 1024 * 1024

QPAD_WIDTH = Q_HEADS * LANES
C_Q = 0
C_KV = C_Q + QPAD_WIDTH
C_ZN = C_KV + 6 * KV_WIDTH
C_QKV = C_ZN + NSA_WIDTH
C_ZG = C_QKV + GDN_CONV_CH
C_SM = C_ZG + GDN_WIDTH
N_PACK = C_SM + LANES
SM_B = 3 * Q_HEADS
SM_A = SM_B + GDN_HEADS

TQ = 128
CK_SLC = 256
CK_WIN = 128
N_SUB = 128
N_CMP = N_SUB - CMP_LEN // CMP_STRIDE + 1
SAMPLE_ROWS = 8
KVB_SK = 0
KVB_SV = KVB_SK + KV_WIDTH
KVB_WK = KVB_SV + 2 * KV_WIDTH
KVB_WV = KVB_WK + KV_WIDTH
KVB_WIDTH = KVB_WV + 2 * KV_WIDTH
KVB_COLS = (KVB_SK, KVB_SV, KVB_WK, KVB_WV)
WIN_SPAN = WINDOW + TQ


def _pack_w_in(w_in):
    o_gate = NSA_WIDTH + 6 * KV_WIDTH
    o_zn = o_gate + 3 * Q_HEADS
    o_qkv = o_zn + NSA_WIDTH
    o_b = o_qkv + GDN_CONV_CH
    o_a = o_b + GDN_HEADS
    o_zg = o_a + GDN_HEADS
    d = w_in.shape[0]
    z64 = jnp.zeros((d, HEAD_DIM), w_in.dtype)
    qcols = []
    for hq in range(Q_HEADS):
        wq = w_in[:, hq * HEAD_DIM:(hq + 1) * HEAD_DIM]
        qcols += [wq, z64] if hq // GROUP == 0 else [z64, wq]
    pad = jnp.zeros((d, LANES - SM_A - GDN_HEADS), w_in.dtype)
    return jnp.concatenate(
        qcols + [w_in[:, NSA_WIDTH:o_gate], w_in[:, o_zn:o_qkv], w_in[:, o_qkv:o_b], w_in[:, o_zg:],
                 w_in[:, o_gate:o_zn], w_in[:, o_b:o_a], w_in[:, o_a:o_zg], pad], axis=1)


def _rope_tables(pos):
    half = ROT_DIM // 2
    inv = ROPE_THETA ** (-(jnp.arange(half, dtype=F32) * 2.0 / ROT_DIM))
    ang = pos.astype(F32)[:, None] * inv[None, :]
    cos, sin = jnp.cos(ang), jnp.sin(ang)
    n = pos.shape[0]
    one = jnp.ones((n, HEAD_DIM - ROT_DIM), F32)
    zero = jnp.zeros((n, HEAD_DIM - ROT_DIM), F32)
    zh = jnp.zeros((n, half), F32)
    c64 = jnp.concatenate([cos, cos, one], axis=1)
    a64 = jnp.concatenate([zh, sin, zero], axis=1)
    b64 = jnp.concatenate([-sin, zh, zero], axis=1)
    tile = lambda t: jnp.concatenate([t, t], axis=1)
    return tile(c64), tile(a64), tile(b64)


def _rope128(x, c, a, b):
    half = ROT_DIM // 2
    return x * c + pltpu.roll(x, half, 1) * a + pltpu.roll(x, LANES - half, 1) * b


_NN = (((1,), (0,)), ((), ()))
_NT = (((1,), (1,)), ((), ()))
_TN = (((0,), (0,)), ((), ()))


def _split_bf16(x):
    hi = x.astype(BF16)
    return hi, (x - hi.astype(F32)).astype(BF16)


def _mm(a, b, dims=_NN, hi=False):
    dot = lambda x, y: lax.dot_general(x, y, dims, preferred_element_type=F32)
    if hi:
        a_hi, a_lo = _split_bf16(a.astype(F32))
        b_hi, b_lo = _split_bf16(b.astype(F32))
        return dot(a_hi, b_hi) + (dot(a_lo, b_hi) + dot(a_hi, b_lo))
    return dot(a.astype(BF16), b.astype(BF16))


def _mm_exact_lhs(a_bf16, x, dims=_NN):
    x1 = x.astype(BF16)
    r1 = x - x1.astype(F32)
    x2 = r1.astype(BF16)
    x3 = (r1 - x2.astype(F32)).astype(BF16)
    dot = lambda t: lax.dot_general(a_bf16, t, dims, preferred_element_type=F32)
    return dot(x1) + dot(x2) + dot(x3)


def _sigmoid(x):
    return 1.0 / (1.0 + jnp.exp(-x))


def _silu(x):
    return x * _sigmoid(x)


def _tile_rows(x, n):
    return jnp.concatenate([x] * n, axis=0)


def _tile_lanes(x, n):
    return x if n == 1 else jnp.concatenate([x] * n, axis=1)


def _eye(n, dtype):
    r = lax.broadcasted_iota(jnp.int32, (n, n), 0)
    c = lax.broadcasted_iota(jnp.int32, (n, n), 1)
    return (r == c).astype(dtype)


def _proj_kernel(x_ref, wn_ref, w_ref, c_ref, a_ref, b_ref,
                 q_ref, qr_ref, ck_ref, cv_ref, sk_ref, sv_ref, wk_ref, wv_ref, kvb_ref,
                 zn_ref, qkv_ref, zg_ref, sm_ref, *t_refs, hi_gdn):
    x = x_ref[...]
    ms = jnp.mean(x * x, axis=-1, keepdims=True)
    xn = x * lax.rsqrt(ms + NORM_EPS) * wn_ref[...]
    xb = xn.astype(BF16)
    c, a, b = c_ref[...], a_ref[...], b_ref[...]
    scale = HEAD_DIM ** -0.5
    for j in range(Q_HEADS):
        qj = _mm(xb, w_ref[:, C_Q + j * LANES:C_Q + (j + 1) * LANES])
        q_ref[:, j * LANES:(j + 1) * LANES] = (qj * scale).astype(q_ref.dtype)
        qr_ref[:, j * LANES:(j + 1) * LANES] = (_rope128(qj, c, a, b) * scale).astype(qr_ref.dtype)
    kv_refs = (ck_ref, cv_ref, sk_ref, sv_ref, wk_ref, wv_ref)
    for j in range(6):
        kj = _mm(xb, w_ref[:, C_KV + j * LANES:C_KV + (j + 1) * LANES])
        if j in (2, 4):
            kj = _rope128(kj, c, a, b)
        kv_refs[j][...] = kj
        if j >= 2:
            lo = KVB_COLS[j - 2]
            kvb_ref[:, lo:lo + LANES] = kj.astype(BF16)
        if t_refs:
            t_refs[j][0] = kj.T
    ones = jnp.ones((x.shape[0], LANES), BF16)
    kvb_ref[:, KVB_SV + LANES:KVB_SV + 2 * LANES] = ones
    kvb_ref[:, KVB_WV + LANES:KVB_WV + 2 * LANES] = ones
    zn_ref[...] = _mm(xb, w_ref[:, C_ZN:C_QKV])
    xg = xn if hi_gdn else xb
    for j in range(3):
        lo = C_QKV + j * GDN_WIDTH
        qkv_ref[:, j * GDN_WIDTH:(j + 1) * GDN_WIDTH] = _mm(xg, w_ref[:, lo:lo + GDN_WIDTH], hi=hi_gdn)
    zg_ref[...] = _mm(xb, w_ref[:, C_ZG:C_SM])
    sm_ref[...] = _mm(xg, w_ref[:, C_SM:N_PACK], hi=hi_gdn)


def _project(x2d, w_norm, w_pack, tables, rows_per_seq, tm, hi_gdn, q_dtype, emit_t):
    n = x2d.shape[0]
    nt = rows_per_seq // tm
    row = lambda i: (i, 0)
    tab = lambda i: (i % nt, 0)
    fix = lambda i: (0, 0)
    widths = (QPAD_WIDTH, QPAD_WIDTH) + (KV_WIDTH,) * 6 + (KVB_WIDTH, NSA_WIDTH, GDN_CONV_CH, GDN_WIDTH, LANES)
    dtypes = (q_dtype, q_dtype) + (F32,) * 6 + (BF16, F32, F32, F32, F32)
    out_specs = [pl.BlockSpec((tm, w), row) for w in widths]
    out_shape = [jax.ShapeDtypeStruct((n, w), d) for w, d in zip(widths, dtypes)]
    if emit_t:
        out_specs += [pl.BlockSpec((1, KV_WIDTH, tm), lambda i: (i // nt, 0, i % nt))] * 6
        out_shape += [jax.ShapeDtypeStruct((n // rows_per_seq, KV_WIDTH, rows_per_seq), F32)] * 6
    return pl.pallas_call(
        functools.partial(_proj_kernel, hi_gdn=hi_gdn),
        grid=(n // tm,),
        in_specs=[pl.BlockSpec((tm, D_MODEL), row),
                  pl.BlockSpec((1, D_MODEL), fix),
                  pl.BlockSpec((D_MODEL, N_PACK), fix),
                  pl.BlockSpec((tm, LANES), tab),
                  pl.BlockSpec((tm, LANES), tab),
                  pl.BlockSpec((tm, LANES), tab)],
        out_specs=out_specs,
        out_shape=out_shape,
        compiler_params=pltpu.CompilerParams(dimension_semantics=("arbitrary",),
                                             vmem_limit_bytes=VMEM_LIMIT_BYTES),
        name="in_proj",
    )(x2d, w_norm.reshape(1, D_MODEL), w_pack, *tables)


def _out_kernel(x_ref, nsa_ref, gdn_ref, w_ref, wf_ref, y_ref, *, hi, final):
    acc = (_mm(nsa_ref[...], w_ref[:NSA_WIDTH, :], hi=hi)
           + _mm(gdn_ref[...], w_ref[NSA_WIDTH:, :], hi=hi))
    h = x_ref[...] + acc
    if final:
        ms = jnp.mean(h * h, axis=-1, keepdims=True)
        h = h * lax.rsqrt(ms + NORM_EPS) * wf_ref[...]
    y_ref[...] = h


def _out_project(x2d, nsa, gdn, w_out, w_final, tm, hi, final):
    n = x2d.shape[0]
    row = lambda i: (i, 0)
    fix = lambda i: (0, 0)
    return pl.pallas_call(
        functools.partial(_out_kernel, hi=hi, final=final),
        grid=(n // tm,),
        in_specs=[pl.BlockSpec((tm, D_MODEL), row),
                  pl.BlockSpec((tm, NSA_WIDTH), row),
                  pl.BlockSpec((tm, GDN_WIDTH), row),
                  pl.BlockSpec((D_MODEL, D_MODEL), fix),
                  pl.BlockSpec((1, D_MODEL), fix)],
        out_specs=pl.BlockSpec((tm, D_MODEL), row),
        out_shape=jax.ShapeDtypeStruct((n, D_MODEL), F32),
        compiler_params=pltpu.CompilerParams(dimension_semantics=("arbitrary",),
                                             vmem_limit_bytes=VMEM_LIMIT_BYTES),
        name="out_proj",
    )(x2d, nsa, gdn, w_out, w_final.reshape(1, D_MODEL))


def _compress_weights(pe, w1, w2):
    half = CMP_STRIDE * HEAD_DIM
    z = jnp.zeros((CMP_STRIDE, HEAD_DIM, CMP_HIDDEN), w1.dtype)

    def place(wpart, h):
        wp = wpart.reshape(CMP_STRIDE, HEAD_DIM, CMP_HIDDEN)
        parts = [wp, z] if h == 0 else [z, wp]
        return jnp.stack(parts, axis=1).reshape(CMP_STRIDE * KV_WIDTH, CMP_HIDDEN)

    w1big = jnp.concatenate([place(w1[:half], 0), place(w1[:half], 1),
                             place(w1[half:], 0), place(w1[half:], 1)], axis=1)
    zz = jnp.zeros_like(w2)
    w2big = jnp.concatenate([jnp.concatenate([w2, zz], axis=1), jnp.concatenate([zz, w2], axis=1)], axis=0)
    pe_a = jnp.tile(pe[:CMP_STRIDE], (1, KV_HEADS)).reshape(1, CMP_STRIDE * KV_WIDTH)
    pe_b = jnp.tile(pe[CMP_STRIDE:], (1, KV_HEADS)).reshape(1, CMP_STRIDE * KV_WIDTH)
    pe2 = jnp.concatenate([pe_a, pe_b, jnp.zeros((SUBLANES - 2, CMP_STRIDE * KV_WIDTH), pe.dtype)], axis=0)
    return w1big.astype(BF16), w2big.astype(BF16), pe2


def _compress(sub_rows, w1_ref, w2_ref, pe_ref):
    hid = 2 * CMP_HIDDEN
    w1 = w1_ref[...]
    ab = _mm(sub_rows, w1)
    pe = pe_ref[...]
    pe_hi = pe.astype(BF16)
    pe_lo = (pe - pe_hi.astype(F32)).astype(BF16)
    r = _mm(pe_hi, w1) + _mm(pe_lo, w1)
    bias = r[0:1, :hid] + r[1:2, hid:]
    h = ab[:, :hid] + pltpu.roll(ab[:, hid:], N_SUB - 1, 0) + bias
    return _mm(_silu(h), w2_ref[...])


def _overlap_t(ns):
    c0 = np.arange(N_CMP)[None, :] * CMP_STRIDE
    b0 = np.arange(ns)[:, None] * SEL_BLOCK
    ov = np.minimum(c0 + CMP_LEN, b0 + SEL_BLOCK) - np.maximum(c0, b0)
    out = np.zeros((LANES, LANES), np.float32)
    out[:ns, :N_CMP] = np.maximum(ov, 0) / CMP_LEN
    return jnp.asarray(out, BF16)


def _cmp_attention(q_stack, ck, cv, cvalid, groups):
    s = _mm(q_stack, ck, _NT)
    cm = _tile_rows(cvalid, groups)
    s = jnp.where(cm, s, MASK_VALUE)
    m = jnp.max(s, axis=1, keepdims=True)
    e = jnp.exp(s - m)
    p = e / jnp.sum(e, axis=1, keepdims=True) * cm.astype(F32)
    return _mm(p, cv), p


def _select_blocks(imp_t, n_idx, q_blk, n_rows):
    forced = (n_idx == 0) | (n_idx == q_blk) | (n_idx == q_blk - 1)
    allowed = n_idx <= q_blk
    v = jnp.where(allowed, imp_t + FORCE_BONUS * forced.astype(F32), MASK_VALUE)
    rank = jnp.zeros(v.shape, F32)
    for j in range(n_rows):
        vj = v[j:j + 1, :]
        ge = jnp.where(vj >= v, 1.0, 0.0)
        gt = jnp.where(vj > v, 1.0, 0.0)
        rank = rank + jnp.where(n_idx > j, ge, gt)
    return ((rank < TOP_N) & allowed).astype(F32)


def _flash_init(m_s, l_s, acc_s):
    m_s[...] = jnp.full(m_s.shape, MASK_VALUE, F32)
    l_s[...] = jnp.zeros(l_s.shape, F32)
    acc_s[...] = jnp.zeros(acc_s.shape, F32)


def _flash_step(s, v_c, m_s, l_s, acc_s):
    m_prev = m_s[...]
    m_next = jnp.maximum(m_prev, jnp.max(s, axis=1, keepdims=True))
    alpha = jnp.exp(m_prev - m_next)
    p = jnp.exp(s - _tile_lanes(m_next, s.shape[1] // LANES))
    l_s[...] = alpha * l_s[...] + jnp.sum(p, axis=1, keepdims=True)
    acc_s[...] = acc_s[...] * alpha + _mm(p, v_c)
    m_s[...] = m_next


def _merge_heads(o_sum, kvh, tq):
    lane = lax.broadcasted_iota(jnp.int32, (tq, LANES), 1)
    slabs = []
    for pair in range(GROUP // 2):
        halves = []
        for par in range(2):
            o = o_sum[(2 * pair + par) * tq:(2 * pair + par + 1) * tq]
            halves.append(o if par == kvh else pltpu.roll(o, HEAD_DIM, 1))
        slabs.append(jnp.where(lane < HEAD_DIM, halves[0], halves[1]))
    return slabs


def _nsa_prompt_kernel(q_ref, qr_ref, kvb_ref, ckr_ref, cvr_ref, sm_ref, zn_ref,
                       wk1_ref, wk2_ref, pek_ref, wv1_ref, wv2_ref, pev_ref, ovt_ref, e_ref,
                       o_ref, ck_s, cv_s, s_s, mrun_s, m_s, acc_s):
    i = pl.program_id(1)

    @pl.when(i == 0)
    def _():
        ck_s[...] = _compress(ckr_ref[0], wk1_ref, wk2_ref, pek_ref).astype(BF16)
        cv_s[...] = _compress(cvr_ref[0], wv1_ref, wv2_ref, pev_ref).astype(BF16)

    kvs = range(KV_HEADS)
    t0 = i * TQ
    rows = GROUP * TQ
    row = lax.broadcasted_iota(jnp.int32, (TQ, LANES), 0)
    col = lax.broadcasted_iota(jnp.int32, (TQ, LANES), 1)
    cvalid = (CMP_STRIDE * col + (CMP_LEN - 1) <= t0 + row) & (col < N_CMP)
    cm = _tile_rows(cvalid, GROUP)
    cm_f = cm.astype(F32)
    sig = _sigmoid(sm_ref[...])

    def gate(branch, kvh):
        cols = [sig[:, branch * Q_HEADS + h:branch * Q_HEADS + h + 1] for h in range(kvh * GROUP, (kvh + 1) * GROUP)]
        return jnp.concatenate([jnp.broadcast_to(c, (TQ, LANES)) for c in cols], axis=0)

    def stack(ref, kvh):
        return jnp.concatenate([ref[:, h * LANES:(h + 1) * LANES] for h in range(kvh * GROUP, (kvh + 1) * GROUP)], axis=0)

    q_raw = [stack(q_ref, k) for k in kvs]
    q_rot = [stack(qr_ref, k) for k in kvs]

    ck, cv = ck_s[...], cv_s[...]
    s = [jnp.where(cm, _mm(q_raw[k], ck, _NT), MASK_VALUE) for k in kvs]
    e = [jnp.exp(s[k] - jnp.max(s[k], axis=1, keepdims=True)) for k in kvs]
    p = [e[k] / jnp.sum(e[k], axis=1, keepdims=True) * cm_f for k in kvs]
    o_sum = [gate(0, k) * _mm(p[k], cv) for k in kvs]
    p_sum = jnp.concatenate([p[k][0:TQ] + p[k][TQ:2 * TQ] + p[k][2 * TQ:3 * TQ] + p[k][3 * TQ:4 * TQ] for k in kvs],
                            axis=0)
    ns_rows = 32
    nq = KV_HEADS * TQ
    imp_t = _mm_exact_lhs(ovt_ref[...], p_sum, _NT)[0:ns_rows]
    n_idx = lax.broadcasted_iota(jnp.int32, (ns_rows, nq), 0)
    qb_t = (t0 + lax.broadcasted_iota(jnp.int32, (ns_rows, nq), 1) % TQ) // SEL_BLOCK
    sel_t = _select_blocks(imp_t, n_idx, qb_t, ns_rows)
    sel_t = jnp.concatenate([sel_t, jnp.zeros((LANES - ns_rows, nq), F32)], axis=0).astype(BF16)
    eye = _eye(TQ, BF16)
    sel = [_mm(eye, sel_t[:, k * TQ:(k + 1) * TQ], _NT).astype(BF16) for k in kvs]

    col_s = lax.broadcasted_iota(jnp.int32, (TQ, CK_SLC), 1)
    row_s = lax.broadcasted_iota(jnp.int32, (TQ, CK_SLC), 0)
    n_chunks = (t0 + TQ + CK_SLC - 1) // CK_SLC
    mrun_s[...] = jnp.full(mrun_s.shape, MASK_VALUE, F32)

    def score_body(j, carry):
        k0 = pl.multiple_of(j * CK_SLC, CK_SLC)
        k_c = kvb_ref[pl.ds(k0, CK_SLC), KVB_SK:KVB_SK + LANES]
        causal = k0 + col_s <= t0 + row_s
        e_j = e_ref[j]
        sc = [_mm(q_rot[k], k_c, _NT) for k in kvs]
        picked = [_mm(sel[k], e_j) for k in kvs]
        for k in kvs:
            bias = jnp.where((picked[k] > 0.5) & causal, 0.0, MASK_VALUE)
            sk = sc[k] + _tile_rows(bias, GROUP)
            s_s[k, j] = sk
            mrun_s[k] = jnp.maximum(mrun_s[k], jnp.maximum(sk[:, :LANES], sk[:, LANES:]))
        return carry

    lax.fori_loop(0, n_chunks, score_body, 0)
    for k in kvs:
        m_s[k] = jnp.broadcast_to(jnp.max(mrun_s[k], axis=1, keepdims=True), (rows, LANES))
    acc_s[...] = jnp.zeros(acc_s.shape, F32)

    def value_body(j, carry):
        k0 = pl.multiple_of(j * CK_SLC, CK_SLC)
        v_c = kvb_ref[pl.ds(k0, CK_SLC), KVB_SV:KVB_SV + 2 * LANES]
        pr = [jnp.exp(s_s[k, j] - _tile_lanes(m_s[k], CK_SLC // LANES)) for k in kvs]
        for k in kvs:
            acc_s[k] += _mm(pr[k], v_c)
        return carry

    lax.fori_loop(0, n_chunks, value_body, 0)
    for k in kvs:
        acc = acc_s[k]
        o_sum[k] = o_sum[k] + gate(1, k) * (acc[:, :LANES] / acc[:, LANES:])

    ks0 = pl.multiple_of(jnp.maximum(t0 - WINDOW, 0), TQ)
    k_w = kvb_ref[pl.ds(ks0, WIN_SPAN), KVB_WK:KVB_WK + LANES]
    v_w = kvb_ref[pl.ds(ks0, WIN_SPAN), KVB_WV:KVB_WV + 2 * LANES]
    diff = (t0 + lax.broadcasted_iota(jnp.int32, (TQ, WIN_SPAN), 0)) - (ks0 + lax.broadcasted_iota(jnp.int32, (TQ, WIN_SPAN), 1))
    bias_w = _tile_rows(jnp.where((diff >= 0) & (diff <= WINDOW), 0.0, MASK_VALUE), GROUP)
    sw = [_mm(q_rot[k], k_w, _NT) + bias_w for k in kvs]
    pw = [jnp.exp(sw[k] - jnp.max(sw[k], axis=1, keepdims=True)) for k in kvs]
    rw = [_mm(pw[k], v_w) for k in kvs]
    for k in kvs:
        o_all = o_sum[k] + gate(2, k) * (rw[k][:, :LANES] / rw[k][:, LANES:])
        for pair, slab in enumerate(_merge_heads(o_all, k, TQ)):
            lo = (k * (GROUP // 2) + pair) * LANES
            o_ref[:, lo:lo + LANES] = (slab * _silu(zn_ref[:, lo:lo + LANES])).astype(o_ref.dtype)


def _nsa_prompt(q, qr, kvb, ck, cv, sm, zn, cmp_k_w, cmp_v_w, batch, seq):
    nt = seq // TQ
    ns = seq // SEL_BLOCK
    sub_w = CMP_STRIDE * KV_WIDTH
    keys = np.arange(seq)
    e = (np.arange(LANES)[None, :, None] == (keys // SEL_BLOCK).reshape(seq // CK_SLC, 1, CK_SLC))
    e = jnp.asarray(e, BF16)
    tile = lambda b, i: (b * nt + i, 0)
    per_b = lambda b, i: (b, 0)
    per_b3 = lambda b, i: (b, 0, 0)
    fix2 = lambda b, i: (0, 0)
    fix3 = lambda b, i: (0, 0, 0)
    wspecs = [pl.BlockSpec((sub_w, 4 * CMP_HIDDEN), fix2),
              pl.BlockSpec((2 * CMP_HIDDEN, KV_WIDTH), fix2),
              pl.BlockSpec((SUBLANES, sub_w), fix2)]
    rows = GROUP * TQ
    return pl.pallas_call(
        _nsa_prompt_kernel,
        grid=(batch, nt),
        in_specs=[pl.BlockSpec((TQ, QPAD_WIDTH), tile),
                  pl.BlockSpec((TQ, QPAD_WIDTH), tile),
                  pl.BlockSpec((seq, KVB_WIDTH), per_b),
                  pl.BlockSpec((1, seq // CMP_STRIDE, sub_w), per_b3),
                  pl.BlockSpec((1, seq // CMP_STRIDE, sub_w), per_b3),
                  pl.BlockSpec((TQ, LANES), tile),
                  pl.BlockSpec((TQ, NSA_WIDTH), tile)] + wspecs + wspecs + [
                  pl.BlockSpec((LANES, LANES), fix2),
                  pl.BlockSpec((seq // CK_SLC, LANES, CK_SLC), fix3)],
        out_specs=pl.BlockSpec((TQ, NSA_WIDTH), tile),
        out_shape=jax.ShapeDtypeStruct((batch * seq, NSA_WIDTH), BF16),
        scratch_shapes=[pltpu.VMEM((N_SUB, KV_WIDTH), BF16), pltpu.VMEM((N_SUB, KV_WIDTH), BF16),
                        pltpu.VMEM((KV_HEADS, seq // CK_SLC, rows, CK_SLC), F32),
                        pltpu.VMEM((KV_HEADS, rows, LANES), F32), pltpu.VMEM((KV_HEADS, rows, LANES), F32),
                        pltpu.VMEM((KV_HEADS, rows, 2 * LANES), F32)],
        compiler_params=pltpu.CompilerParams(dimension_semantics=("arbitrary", "arbitrary"),
                                             vmem_limit_bytes=VMEM_LIMIT_BYTES),
        name="nsa_prompt",
    )(q, qr, kvb, ck.reshape(batch, seq // CMP_STRIDE, sub_w), cv.reshape(batch, seq // CMP_STRIDE, sub_w),
      sm, zn, *cmp_k_w, *cmp_v_w, _overlap_t(ns), e)


def _softmax_rows(s):
    m = jnp.max(s, axis=1, keepdims=True)
    e = jnp.exp(s - m)
    return e / jnp.sum(e, axis=1, keepdims=True)


def _nsa_sample_kernel(pt_ref, q_ref, qr_ref, skn_ref, svn_ref, wkn_ref, wvn_ref, sm_ref, zn_ref,
                       wkc_ref, wvc_ref, *rest, n_pages, page, past, n_new, win_keys):
    del pt_ref
    cmpk, cmpv = rest[0:n_pages], rest[n_pages:2 * n_pages]
    slck, slcv = rest[2 * n_pages:3 * n_pages], rest[3 * n_pages:4 * n_pages]
    wk1_ref, wk2_ref, pek_ref, wv1_ref, wv2_ref, pev_ref, ovt_ref, e_ref = rest[4 * n_pages:4 * n_pages + 8]
    o_ref, wko_ref, wvo_ref = rest[4 * n_pages + 8:4 * n_pages + 11]
    kall, vall, kwin, vwin = rest[4 * n_pages + 11:]
    nr = SAMPLE_ROWS
    rows = Q_HEADS * nr
    all_keys = kall.shape[0]
    win_buf = wkc_ref.shape[1]
    zeros8 = jnp.zeros((nr, LANES), F32)

    for p in range(n_pages):
        kall[p * page:(p + 1) * page, :] = slck[p][0].astype(BF16)
        vall[p * page:(p + 1) * page, :] = slcv[p][0].astype(BF16)
    kall[past:past + 2 * nr, :] = jnp.concatenate([skn_ref[...], zeros8], axis=0).astype(BF16)
    vall[past:past + 2 * nr, :] = jnp.concatenate([svn_ref[...], zeros8], axis=0).astype(BF16)
    kall[past + 2 * nr:, :] = jnp.zeros((all_keys - past - 2 * nr, LANES), BF16)
    vall[past + 2 * nr:, :] = jnp.zeros((all_keys - past - 2 * nr, LANES), BF16)
    kwin[0:win_buf, :] = wkc_ref[0].astype(BF16)
    vwin[0:win_buf, :] = wvc_ref[0].astype(BF16)
    kwin[win_buf:win_buf + 2 * nr, :] = jnp.concatenate([wkn_ref[...], zeros8], axis=0).astype(BF16)
    vwin[win_buf:win_buf + 2 * nr, :] = jnp.concatenate([wvn_ref[...], zeros8], axis=0).astype(BF16)
    kwin[win_buf + 2 * nr:, :] = jnp.zeros((win_keys - win_buf - 2 * nr, LANES), BF16)
    vwin[win_buf + 2 * nr:, :] = jnp.zeros((win_keys - win_buf - 2 * nr, LANES), BF16)

    ck = _compress(jnp.concatenate([r[0] for r in cmpk], axis=0), wk1_ref, wk2_ref, pek_ref)
    cv = _compress(jnp.concatenate([r[0] for r in cmpv], axis=0), wv1_ref, wv2_ref, pev_ref)

    q_raw = jnp.concatenate([q_ref[:, h * LANES:(h + 1) * LANES] for h in range(Q_HEADS)], axis=0)
    q_rot = jnp.concatenate([qr_ref[:, h * LANES:(h + 1) * LANES] for h in range(Q_HEADS)], axis=0)

    tok = lax.broadcasted_iota(jnp.int32, (nr, LANES), 0)
    col = lax.broadcasted_iota(jnp.int32, (nr, LANES), 1)
    cvalid = (CMP_STRIDE * col + (CMP_LEN - 1) <= past + tok) & (col < N_CMP)
    o_cmp, p = _cmp_attention(q_raw, ck, cv, cvalid, Q_HEADS)

    p_sum = []
    for kvh in range(KV_HEADS):
        base = kvh * GROUP * nr
        p_sum.append(sum(p[base + g * nr:base + (g + 1) * nr] for g in range(GROUP)))
    p_sum = jnp.concatenate(p_sum, axis=0)
    ns = (past + n_new + SEL_BLOCK - 1) // SEL_BLOCK
    ns_rows = -(-ns // SUBLANES) * SUBLANES
    nq = KV_HEADS * nr
    imp_t = _mm_exact_lhs(ovt_ref[...], p_sum, _NT)[0:ns_rows]
    n_idx = lax.broadcasted_iota(jnp.int32, (ns_rows, nq), 0)
    qb_t = (past + lax.broadcasted_iota(jnp.int32, (ns_rows, nq), 1) % nr) // SEL_BLOCK
    sel_t = _select_blocks(imp_t, n_idx, qb_t, ns)
    sel_t = jnp.concatenate([sel_t, jnp.zeros((LANES - ns_rows, nq), F32)], axis=0)
    sel = _mm(_eye(nq, BF16), sel_t, _NT)
    sel_rows = jnp.concatenate([_tile_rows(sel[kvh * nr:(kvh + 1) * nr], GROUP) for kvh in range(KV_HEADS)], axis=0)

    picked = _mm(sel_rows, e_ref[...])
    q_pos = past + lax.broadcasted_iota(jnp.int32, (rows, all_keys), 0) % nr
    k_pos = lax.broadcasted_iota(jnp.int32, (rows, all_keys), 1)
    ok = (picked > 0.5) & (k_pos <= q_pos)
    s = _mm(q_rot, kall[...], _NT) + jnp.where(ok, 0.0, MASK_VALUE)
    o_slc = _mm(_softmax_rows(s), vall[...])

    q_pos = past + lax.broadcasted_iota(jnp.int32, (rows, win_keys), 0) % nr
    k_pos = past - win_buf + lax.broadcasted_iota(jnp.int32, (rows, win_keys), 1)
    diff = q_pos - k_pos
    ok = (diff >= 0) & (diff <= WINDOW)
    s = _mm(q_rot, kwin[...], _NT) + jnp.where(ok, 0.0, MASK_VALUE)
    o_win = _mm(_softmax_rows(s), vwin[...])

    sig = _sigmoid(sm_ref[...])

    def gate(branch):
        cols = [sig[:, branch * Q_HEADS + h:branch * Q_HEADS + h + 1] for h in range(Q_HEADS)]
        return jnp.concatenate([jnp.broadcast_to(c, (nr, LANES)) for c in cols], axis=0)

    o_all = gate(0) * o_cmp + gate(1) * o_slc + gate(2) * o_win
    for kvh in range(KV_HEADS):
        o_kv = o_all[kvh * GROUP * nr:(kvh + 1) * GROUP * nr]
        for pair, slab in enumerate(_merge_heads(o_kv, kvh, nr)):
            lo = (kvh * (GROUP // 2) + pair) * LANES
            o_ref[:, lo:lo + LANES] = (slab * _silu(zn_ref[:, lo:lo + LANES])).astype(o_ref.dtype)

    for cache_ref, new_ref, out_ref in ((wkc_ref, wkn_ref, wko_ref), (wvc_ref, wvn_ref, wvo_ref)):
        shifted = pltpu.roll(cache_ref[0], win_buf - n_new, 0)
        out_ref[0] = shifted
        tail = jnp.where(tok < nr - n_new, shifted[win_buf - nr:], pltpu.roll(new_ref[...], nr - n_new, 0))
        out_ref[0, win_buf - nr:, :] = tail


def _nsa_sample(q, qr, sk, sv, wk, wv, sm, zn, win_k, win_v, cmp_k, cmp_v, slc_k, slc_v, page_table,
                cmp_k_w, cmp_v_w, n_new):
    batch, n_pages = page_table.shape
    n_pool, page = cmp_k.shape[0], cmp_k.shape[1]
    past = n_pages * page
    nr = SAMPLE_ROWS
    sub_w = CMP_STRIDE * KV_WIDTH
    win_buf = win_k.shape[1]
    all_keys = -(-(past + 2 * nr) // LANES) * LANES
    win_keys = -(-(win_buf + 2 * nr) // LANES) * LANES
    ns = (past + n_new + SEL_BLOCK - 1) // SEL_BLOCK
    e = jnp.asarray(np.arange(LANES)[:, None] == (np.arange(all_keys) // SEL_BLOCK)[None, :], BF16)
    cmp_k = cmp_k.reshape(n_pool, page // CMP_STRIDE, sub_w)
    cmp_v = cmp_v.reshape(n_pool, page // CMP_STRIDE, sub_w)
    slc_k = slc_k.reshape(n_pool, page, KV_WIDTH)
    slc_v = slc_v.reshape(n_pool, page, KV_WIDTH)
    win_k = win_k.reshape(batch, win_buf, KV_WIDTH)
    win_v = win_v.reshape(batch, win_buf, KV_WIDTH)

    row = lambda b, pt: (b, 0)
    per_b3 = lambda b, pt: (b, 0, 0)
    fix2 = lambda b, pt: (0, 0)
    page_map = lambda p: (lambda b, pt: (pt[b, p], 0, 0))
    wspecs = [pl.BlockSpec((sub_w, 4 * CMP_HIDDEN), fix2),
              pl.BlockSpec((2 * CMP_HIDDEN, KV_WIDTH), fix2),
              pl.BlockSpec((SUBLANES, sub_w), fix2)]
    in_specs = ([pl.BlockSpec((nr, QPAD_WIDTH), row)] * 2 + [pl.BlockSpec((nr, KV_WIDTH), row)] * 4
                + [pl.BlockSpec((nr, LANES), row), pl.BlockSpec((nr, NSA_WIDTH), row)]
                + [pl.BlockSpec((1, win_buf, KV_WIDTH), per_b3)] * 2
                + [pl.BlockSpec((1, page // CMP_STRIDE, sub_w), page_map(p)) for p in range(n_pages)] * 1
                + [pl.BlockSpec((1, page // CMP_STRIDE, sub_w), page_map(p)) for p in range(n_pages)]
                + [pl.BlockSpec((1, page, KV_WIDTH), page_map(p)) for p in range(n_pages)]
                + [pl.BlockSpec((1, page, KV_WIDTH), page_map(p)) for p in range(n_pages)]
                + wspecs + wspecs
                + [pl.BlockSpec((LANES, LANES), fix2), pl.BlockSpec((LANES, all_keys), fix2)])
    grid_spec = pltpu.PrefetchScalarGridSpec(
        num_scalar_prefetch=1,
        grid=(batch,),
        in_specs=in_specs,
        out_specs=[pl.BlockSpec((nr, NSA_WIDTH), row),
                   pl.BlockSpec((1, win_buf, KV_WIDTH), per_b3),
                   pl.BlockSpec((1, win_buf, KV_WIDTH), per_b3)],
        scratch_shapes=[pltpu.VMEM((all_keys, KV_WIDTH), BF16), pltpu.VMEM((all_keys, KV_WIDTH), BF16),
                        pltpu.VMEM((win_keys, KV_WIDTH), BF16), pltpu.VMEM((win_keys, KV_WIDTH), BF16)])
    return pl.pallas_call(
        functools.partial(_nsa_sample_kernel, n_pages=n_pages, page=page, past=past, n_new=n_new,
                          win_keys=win_keys),
        grid_spec=grid_spec,
        out_shape=[jax.ShapeDtypeStruct((batch * nr, NSA_WIDTH), F32),
                   jax.ShapeDtypeStruct((batch, win_buf, KV_WIDTH), F32),
                   jax.ShapeDtypeStruct((batch, win_buf, KV_WIDTH), F32)],
        compiler_params=pltpu.CompilerParams(dimension_semantics=("arbitrary",),
                                             vmem_limit_bytes=VMEM_LIMIT_BYTES),
        name="nsa_sample",
    )(page_table, q, qr, sk, sv, wk, wv, sm, zn, win_k, win_v,
      *([cmp_k] * n_pages), *([cmp_v] * n_pages), *([slc_k] * n_pages), *([slc_v] * n_pages),
      *cmp_k_w, *cmp_v_w, _overlap_t(ns), e)


def _compress_weights_t(pe, w1, w2):
    w1big, w2big, pe2 = _compress_weights(pe, w1, w2)
    return w1big.T, w2big.T, pe2


def _nsa_sample_t_kernel(pt_ref, q_ref, qr_ref, skn_ref, svn_ref, wkn_ref, wvn_ref, sm_ref, zn_ref,
                         wkc_ref, wvc_ref, *rest, nb, n_pages, page, past, n_new):
    del pt_ref
    npg = nb * n_pages
    cmpk, cmpv, slck, slcv = (rest[i * npg:(i + 1) * npg] for i in range(4))
    w1k_ref, w2k_ref, pek_ref, w1v_ref, w2v_ref, pev_ref, ovt_ref, e_ref = rest[4 * npg:4 * npg + 8]
    o_ref = rest[4 * npg + 8]
    kt_s, vt_s, kwt_s, vwt_s = rest[4 * npg + 9:]
    nr = SAMPLE_ROWS
    rows = Q_HEADS * nr
    win_buf = wkc_ref.shape[2]
    bs = range(nb)
    hid = 2 * CMP_HIDDEN
    zpad = jnp.zeros((LANES - nr, LANES), F32)

    def new_rows(ref, b):
        return jnp.concatenate([ref[b * nr:(b + 1) * nr, :], zpad], axis=0).astype(BF16)

    def stack(ref, b):
        return jnp.concatenate([ref[b * nr:(b + 1) * nr, h * LANES:(h + 1) * LANES] for h in range(Q_HEADS)],
                               axis=0).astype(BF16)

    for b in bs:
        for p in range(n_pages):
            kt_s[b, :, p * page:(p + 1) * page] = slck[b * n_pages + p][0].astype(BF16)
            vt_s[b, :, p * page:(p + 1) * page] = slcv[b * n_pages + p][0].astype(BF16)
        kwt_s[b] = wkc_ref[b].astype(BF16)
        vwt_s[b] = wvc_ref[b].astype(BF16)
    mi = lax.broadcasted_iota(jnp.int32, (page, page), 0)
    ki = lax.broadcasted_iota(jnp.int32, (page, page), 1)
    per_page = page // CMP_STRIDE
    perm = (ki == CMP_STRIDE * (mi % per_page) + mi // per_page).astype(BF16)
    rows_t = [[[_mm(perm, pages[b * n_pages + p][0], _NT) for p in range(n_pages)] for pages in (cmpk, cmpv)]
              for b in bs]

    q_raw = [stack(q_ref, b) for b in bs]
    q_rot = [stack(qr_ref, b) for b in bs]

    sw = [jnp.concatenate([_mm(q_rot[b], kwt_s[b]), _mm(q_rot[b], new_rows(wkn_ref, b), _NT)], axis=1) for b in bs]
    ss = [jnp.concatenate([_mm(q_rot[b], kt_s[b]), _mm(q_rot[b], new_rows(skn_ref, b), _NT)], axis=1) for b in bs]

    def sub_blocks(b, c):
        cols = [jnp.concatenate([rows_t[b][c][p][l * per_page:(l + 1) * per_page] for p in range(n_pages)], axis=0)
                for l in range(CMP_STRIDE)]
        return jnp.concatenate(cols, axis=1)

    ckt, cvt = [], []
    for c, (w1_ref, w2_ref, pe_ref, out) in enumerate(((w1k_ref, w2k_ref, pek_ref, ckt), (w1v_ref, w2v_ref, pev_ref, cvt))):
        pe = pe_ref[...]
        subs = [sub_blocks(b, c) for b in bs]
        h_a = [_mm(w1_ref[0:hid, :], subs[b] + pe[0:1, :], _NT) for b in bs]
        h_b = [_mm(w1_ref[hid:2 * hid, :], subs[b] + pe[1:2, :], _NT) for b in bs]
        h = [_silu(h_a[b] + pltpu.roll(h_b[b], N_SUB - 1, 1)) for b in bs]
        out.extend(_mm(w2_ref[...], h[b]) for b in bs)

    tok = lax.broadcasted_iota(jnp.int32, (nr, LANES), 0)
    col = lax.broadcasted_iota(jnp.int32, (nr, LANES), 1)
    cm = _tile_rows((CMP_STRIDE * col + (CMP_LEN - 1) <= past + tok) & (col < N_CMP), Q_HEADS)
    cm_f = cm.astype(F32)
    sc = [jnp.where(cm, _mm(q_raw[b], ckt[b]), MASK_VALUE) for b in bs]
    ec = [jnp.exp(sc[b] - jnp.max(sc[b], axis=1, keepdims=True)) for b in bs]
    pc = [ec[b] / jnp.sum(ec[b], axis=1, keepdims=True) * cm_f for b in bs]
    o_cmp = [_mm(pc[b], cvt[b], _NT) for b in bs]
    p_sum = jnp.concatenate(
        [sum(pc[b][(kvh * GROUP + g) * nr:(kvh * GROUP + g + 1) * nr] for g in range(GROUP))
         for b in bs for kvh in range(KV_HEADS)], axis=0)
    ns = (past + n_new + SEL_BLOCK - 1) // SEL_BLOCK
    ns_rows = -(-ns // SUBLANES) * SUBLANES
    nq = nb * KV_HEADS * nr
    imp_t = _mm_exact_lhs(ovt_ref[...], p_sum, _NT)[0:ns_rows]
    n_idx = lax.broadcasted_iota(jnp.int32, (ns_rows, nq), 0)
    qb_t = (past + lax.broadcasted_iota(jnp.int32, (ns_rows, nq), 1) % nr) // SEL_BLOCK
    sel_t = _select_blocks(imp_t, n_idx, qb_t, ns)
    sel_t = jnp.concatenate([sel_t, jnp.zeros((LANES - ns_rows, nq), F32)], axis=0)
    sel = _mm(_eye(nq, BF16), sel_t, _NT)

    all_keys = past + LANES
    q_pos = past + lax.broadcasted_iota(jnp.int32, (rows, all_keys), 0) % nr
    causal = lax.broadcasted_iota(jnp.int32, (rows, all_keys), 1) <= q_pos
    o_slc = []
    for b in bs:
        sel_rows = jnp.concatenate(
            [_tile_rows(sel[(b * KV_HEADS + kvh) * nr:(b * KV_HEADS + kvh + 1) * nr], GROUP) for kvh in range(KV_HEADS)],
            axis=0)
        ok = (_mm(sel_rows, e_ref[...]) > 0.5) & causal
        s = ss[b] + jnp.where(ok, 0.0, MASK_VALUE)
        pr = jnp.exp(s - jnp.max(s, axis=1, keepdims=True))
        pr = pr / jnp.sum(pr, axis=1, keepdims=True)
        o_slc.append(_mm(pr[:, :past], vt_s[b], _NT) + _mm(pr[:, past:], new_rows(svn_ref, b)))

    win_keys = win_buf + LANES
    q_pos = past + lax.broadcasted_iota(jnp.int32, (rows, win_keys), 0) % nr
    diff = q_pos - (past - win_buf + lax.broadcasted_iota(jnp.int32, (rows, win_keys), 1))
    bias_w = jnp.where((diff >= 0) & (diff <= WINDOW), 0.0, MASK_VALUE)
    o_win = []
    for b in bs:
        s = sw[b] + bias_w
        pr = jnp.exp(s - jnp.max(s, axis=1, keepdims=True))
        pr = pr / jnp.sum(pr, axis=1, keepdims=True)
        o_win.append(_mm(pr[:, :win_buf], vwt_s[b], _NT) + _mm(pr[:, win_buf:], new_rows(wvn_ref, b)))

    for b in bs:
        sig = _sigmoid(sm_ref[b * nr:(b + 1) * nr, :])

        def gate(branch):
            cols = [sig[:, branch * Q_HEADS + h:branch * Q_HEADS + h + 1] for h in range(Q_HEADS)]
            return jnp.concatenate([jnp.broadcast_to(c, (nr, LANES)) for c in cols], axis=0)

        o_all = gate(0) * o_cmp[b] + gate(1) * o_slc[b] + gate(2) * o_win[b]
        for kvh in range(KV_HEADS):
            o_kv = o_all[kvh * GROUP * nr:(kvh + 1) * GROUP * nr]
            for pair, slab in enumerate(_merge_heads(o_kv, kvh, nr)):
                lo = (kvh * (GROUP // 2) + pair) * LANES
                o_ref[b * nr:(b + 1) * nr, lo:lo + LANES] = slab * _silu(zn_ref[b * nr:(b + 1) * nr, lo:lo + LANES])


def _nsa_sample_t(q, qr, sk, sv, wk, wv, sm, zn, win_k, win_v, cmp_k, cmp_v, slc_k, slc_v, page_table,
                  cmp_k_w, cmp_v_w, n_new, nb):
    batch, n_pages = page_table.shape
    n_pool, page = cmp_k.shape[0], cmp_k.shape[1]
    past = n_pages * page
    nr = SAMPLE_ROWS
    win_buf = win_k.shape[1]
    ns = (past + n_new + SEL_BLOCK - 1) // SEL_BLOCK
    all_keys = past + LANES
    e = jnp.asarray(np.arange(LANES)[:, None] == (np.arange(all_keys) // SEL_BLOCK)[None, :], BF16)
    tview = lambda c: jnp.transpose(c, (0, 2, 3, 1)).reshape(c.shape[0], KV_WIDTH, c.shape[1])
    cmp_k, cmp_v, slc_k, slc_v, win_k, win_v = (tview(c) for c in (cmp_k, cmp_v, slc_k, slc_v, win_k, win_v))

    row = lambda b, pt: (b, 0)
    per_b3 = lambda b, pt: (b, 0, 0)
    fix2 = lambda b, pt: (0, 0)
    page_map = lambda j, p: (lambda b, pt: (pt[b * nb + j, p], 0, 0))
    page_specs = [pl.BlockSpec((1, KV_WIDTH, page), page_map(j, p)) for j in range(nb) for p in range(n_pages)]
    sub_w = CMP_STRIDE * KV_WIDTH
    wspecs = [pl.BlockSpec((4 * CMP_HIDDEN, sub_w), fix2),
              pl.BlockSpec((KV_WIDTH, 2 * CMP_HIDDEN), fix2),
              pl.BlockSpec((SUBLANES, sub_w), fix2)]
    in_specs = ([pl.BlockSpec((nb * nr, QPAD_WIDTH), row)] * 2 + [pl.BlockSpec((nb * nr, KV_WIDTH), row)] * 4
                + [pl.BlockSpec((nb * nr, LANES), row), pl.BlockSpec((nb * nr, NSA_WIDTH), row)]
                + [pl.BlockSpec((nb, KV_WIDTH, win_buf), per_b3)] * 2
                + page_specs * 4 + wspecs + wspecs
                + [pl.BlockSpec((LANES, LANES), fix2), pl.BlockSpec((LANES, all_keys), fix2)])
    grid_spec = pltpu.PrefetchScalarGridSpec(
        num_scalar_prefetch=1,
        grid=(batch // nb,),
        in_specs=in_specs,
        out_specs=pl.BlockSpec((nb * nr, NSA_WIDTH), row),
        scratch_shapes=[pltpu.VMEM((nb, KV_WIDTH, past), BF16), pltpu.VMEM((nb, KV_WIDTH, past), BF16),
                        pltpu.VMEM((nb, KV_WIDTH, win_buf), BF16), pltpu.VMEM((nb, KV_WIDTH, win_buf), BF16)])
    npg = nb * n_pages
    return pl.pallas_call(
        functools.partial(_nsa_sample_t_kernel, nb=nb, n_pages=n_pages, page=page, past=past, n_new=n_new),
        grid_spec=grid_spec,
        out_shape=jax.ShapeDtypeStruct((batch * nr, NSA_WIDTH), F32),
        compiler_params=pltpu.CompilerParams(dimension_semantics=("arbitrary",),
                                             vmem_limit_bytes=VMEM_LIMIT_BYTES),
        name="nsa_sample",
    )(page_table, q, qr, sk, sv, wk, wv, sm, zn, win_k, win_v,
      *([cmp_k] * npg), *([cmp_v] * npg), *([slc_k] * npg), *([slc_v] * npg),
      *cmp_k_w, *cmp_v_w, _overlap_t(ns), e)


def _unit_lower_inverse(a, n_valid, hi):
    c = a.shape[0]
    inv = _eye(c, F32) - a
    power = a
    span = 2
    while span < n_valid:
        power = _mm(power, power, hi=hi)
        inv = inv + _mm(inv, power, hi=hi)
        span *= 2
    return inv


def _gdn_kernel(qkv_ref, sm_ref, zg_ref, conv0_ref, s0_ref, wc_ref, vec_ref, wg_ref,
                go_ref, xp_out_ref, st_ref, xp_s, *, bblk, chunk, n_valid, hi):
    c_idx = pl.program_id(1)

    @pl.when(c_idx == 0)
    def _():
        xp_s[:, 0:SUBLANES, :] = conv0_ref[...]
        st_ref[...] = s0_ref[...]

    wc = wc_ref[...]
    vec = vec_ref[...]
    row1 = lax.broadcasted_iota(jnp.int32, (chunk, LANES), 0)
    rr = lax.broadcasted_iota(jnp.int32, (chunk, chunk), 0)
    cc = lax.broadcasted_iota(jnp.int32, (chunk, chunk), 1)
    tri = rr >= cc
    tri_bf = tri.astype(BF16)
    eye_bf = _eye(LANES, BF16)

    acts, betas, decays, decay_ts, e_decs = [], [], [], [], []
    for b in range(bblk):
        xp_s[b, SUBLANES:SUBLANES + chunk, :] = qkv_ref[b]
        y = xp_s[b, SUBLANES:SUBLANES + chunk, :] * wc[GDN_CONV - 1:GDN_CONV, :]
        for j in range(GDN_CONV - 1):
            lo = SUBLANES - (GDN_CONV - 1) + j
            y = y + xp_s[b, lo:lo + chunk, :] * wc[j:j + 1, :]
        xp_out_ref[b] = xp_s[b]
        xp_s[b, 0:SUBLANES, :] = xp_s[b, chunk:chunk + SUBLANES, :]
        act = _silu(y)
        small = sm_ref[b]
        z = small + vec[1:2, :]
        softplus = jnp.maximum(z, 0.0) + jnp.log1p(jnp.exp(-jnp.abs(z)))
        g_all = -jnp.exp(vec[0:1, :]) * softplus
        if n_valid < chunk:
            valid = row1 < n_valid
            act = act * _tile_lanes(valid.astype(F32), GDN_CONV_CH // LANES)
            g_all = jnp.where(valid, g_all, 0.0)
        acts.append(act)
        betas.append(_sigmoid(small))
        decays.append(_mm_exact_lhs(tri_bf, g_all))
    for b in range(bblk):
        decay_ts.append(_mm_exact_lhs(eye_bf, decays[b], _NT))
        e_decs.append(jnp.exp(decays[b]))

    chains = [(b, h) for b in range(bblk) for h in range(GDN_HEADS)]
    qs, ks, kbs, dmasks, rhs_u, rhs_w, qds, kds, gls = [], [], [], [], [], [], [], [], []
    for b, h in chains:
        act = acts[b]
        qh = act[:, h * GDN_DK:(h + 1) * GDN_DK]
        kh = act[:, GDN_WIDTH + h * GDN_DK:GDN_WIDTH + (h + 1) * GDN_DK]
        vh = act[:, 2 * GDN_WIDTH + h * GDN_DV:2 * GDN_WIDTH + (h + 1) * GDN_DV]
        qh = qh * lax.rsqrt(jnp.sum(qh * qh, axis=-1, keepdims=True) + NORM_EPS) * (GDN_DK ** -0.5)
        kh = kh * lax.rsqrt(jnp.sum(kh * kh, axis=-1, keepdims=True) + NORM_EPS)
        beta = betas[b][:, SM_B + h:SM_B + h + 1]
        dcol = decays[b][:, SM_A + h:SM_A + h + 1]
        drow = decay_ts[b][SM_A + h:SM_A + h + 1, :]
        ed = e_decs[b][:, SM_A + h:SM_A + h + 1]
        dlast = decays[b][chunk - 1:chunk, SM_A + h:SM_A + h + 1]
        kb = kh * beta
        qs.append(qh)
        ks.append(kh)
        kbs.append(kb)
        dmasks.append(jnp.where(tri, jnp.exp(jnp.where(tri, dcol - drow, 0.0)), 0.0))
        rhs_u.append(vh * beta)
        rhs_w.append(kb * ed)
        qds.append(qh * ed)
        kds.append(kh * jnp.exp(dlast - dcol))
        gls.append(jnp.exp(dlast))

    n = len(chains)
    eye_c = _eye(chunk, F32)
    kkts = [_mm(kbs[i], ks[i], _NT, hi=hi) for i in range(n)]
    qks = [_mm(qs[i], ks[i], _NT, hi=hi) for i in range(n)]
    powers = [jnp.where(rr > cc, kkts[i] * dmasks[i], 0.0) for i in range(n)]
    qks = [qks[i] * dmasks[i] for i in range(n)]
    invs = [eye_c - powers[i] for i in range(n)]
    span = 2
    while span < n_valid:
        powers = [_mm(powers[i], powers[i], hi=hi) for i in range(n)]
        invs = [invs[i] + _mm(invs[i], powers[i], hi=hi) for i in range(n)]
        span *= 2
    us = [_mm(invs[i], rhs_u[i], hi=hi) for i in range(n)]
    ws = [_mm(invs[i], rhs_w[i], hi=hi) for i in range(n)]
    sts = [st_ref[b, h] for b, h in chains]
    v_news = [us[i] - _mm(ws[i], sts[i], hi=hi) for i in range(n)]
    os_ = [_mm(qds[i], sts[i], hi=hi) for i in range(n)]
    os_ = [os_[i] + _mm(qks[i], v_news[i], hi=hi) for i in range(n)]
    upd = [_mm(kds[i], v_news[i], _TN, hi=hi) for i in range(n)]
    for i, (b, h) in enumerate(chains):
        st_ref[b, h] = sts[i] * gls[i] + upd[i]
        o = os_[i]
        o = o * lax.rsqrt(jnp.mean(o * o, axis=-1, keepdims=True) + NORM_EPS) * wg_ref[...]
        o = o * _silu(zg_ref[b, :, h * GDN_DV:(h + 1) * GDN_DV])
        go_ref[b, :, h * GDN_DV:(h + 1) * GDN_DV] = o.astype(go_ref.dtype)


def _gdn(qkv, sm, zg, conv0, s0, w_conv, a_log, dt_bias, w_gnorm, batch, rows, bblk, chunk, n_valid, hi,
         out_dtype):
    nc = rows // chunk
    tile = lambda b, c: (b, c, 0)
    per_b3 = lambda b, c: (b, 0, 0)
    per_b4 = lambda b, c: (b, 0, 0, 0)
    fix2 = lambda b, c: (0, 0)
    wc = jnp.concatenate([w_conv, jnp.zeros((SUBLANES - GDN_CONV, GDN_CONV_CH), w_conv.dtype)], axis=0)
    vec = jnp.zeros((SUBLANES, LANES), F32)
    vec = vec.at[0, SM_A:SM_A + GDN_HEADS].set(a_log).at[1, SM_A:SM_A + GDN_HEADS].set(dt_bias)
    go, xp, st = pl.pallas_call(
        functools.partial(_gdn_kernel, bblk=bblk, chunk=chunk, n_valid=n_valid, hi=hi),
        grid=(batch // bblk, nc),
        in_specs=[pl.BlockSpec((bblk, chunk, GDN_CONV_CH), tile),
                  pl.BlockSpec((bblk, chunk, LANES), tile),
                  pl.BlockSpec((bblk, chunk, GDN_WIDTH), tile),
                  pl.BlockSpec((bblk, SUBLANES, GDN_CONV_CH), per_b3),
                  pl.BlockSpec((bblk, GDN_HEADS, GDN_DK, GDN_DV), per_b4),
                  pl.BlockSpec((SUBLANES, GDN_CONV_CH), fix2),
                  pl.BlockSpec((SUBLANES, LANES), fix2),
                  pl.BlockSpec((1, GDN_DV), fix2)],
        out_specs=[pl.BlockSpec((bblk, chunk, GDN_WIDTH), tile),
                   pl.BlockSpec((bblk, SUBLANES + chunk, GDN_CONV_CH), per_b3),
                   pl.BlockSpec((bblk, GDN_HEADS, GDN_DK, GDN_DV), per_b4)],
        out_shape=[jax.ShapeDtypeStruct((batch, rows, GDN_WIDTH), out_dtype),
                   jax.ShapeDtypeStruct((batch, SUBLANES + chunk, GDN_CONV_CH), F32),
                   jax.ShapeDtypeStruct((batch, GDN_HEADS, GDN_DK, GDN_DV), F32)],
        scratch_shapes=[pltpu.VMEM((bblk, SUBLANES + chunk, GDN_CONV_CH), F32)],
        compiler_params=pltpu.CompilerParams(dimension_semantics=("arbitrary", "arbitrary"),
                                             vmem_limit_bytes=VMEM_LIMIT_BYTES),
        name="gdn",
    )(qkv.reshape(batch, rows, GDN_CONV_CH), sm.reshape(batch, rows, LANES), zg.reshape(batch, rows, GDN_WIDTH),
      conv0, s0, wc, vec, w_gnorm.reshape(1, GDN_DV))
    return go.reshape(batch * rows, GDN_WIDTH), xp, st


PROMPT_TM = 512
SAMPLE_TM = 256
GDN_PROMPT_BBLK = 4
GDN_SAMPLE_BBLK = 8
NSA_SAMPLE_NB = 2


def _layer_prompt(h, lw, final, w_final, win_buf):
    w_norm, w_pack, cmp_k_w, cmp_v_w, w_conv, a_log, dt_bias, w_gnorm, w_out = lw
    batch, seq, d = h.shape
    x2d = h.reshape(batch * seq, d)
    tables = _rope_tables(jnp.arange(seq, dtype=jnp.int32))
    (q, qr, ck, cv, sk, sv, wk, wv, kvb, zn, qkv, zg, sm, ck_t, cv_t, sk_t, sv_t, wk_t, wv_t) = _project(
        x2d, w_norm, w_pack.astype(BF16), tables, seq, PROMPT_TM, False, BF16, True)
    nsa = _nsa_prompt(q, qr, kvb, ck, cv, sm, zn, cmp_k_w, cmp_v_w, batch, seq)
    conv0 = jnp.zeros((batch, SUBLANES, GDN_CONV_CH), F32)
    s0 = jnp.zeros((batch, GDN_HEADS, GDN_DK, GDN_DV), F32)
    go, xp, st = _gdn(qkv, sm, zg, conv0, s0, w_conv, a_log, dt_bias, w_gnorm,
                      batch, seq, GDN_PROMPT_BBLK, GDN_CHUNK, GDN_CHUNK, False, BF16)
    y = _out_project(x2d, nsa, go, w_out.astype(BF16), w_final, PROMPT_TM, False, final)
    conv_new = xp[:, SUBLANES + GDN_CHUNK - (GDN_CONV - 1):SUBLANES + GDN_CHUNK]
    from_t = lambda t: jnp.transpose(t.reshape(batch, KV_HEADS, HEAD_DIM, t.shape[-1]), (0, 3, 1, 2))
    lead = ((0, 0), (0, 0), (max(win_buf - seq, 0), 0))
    win = lambda t: from_t(jnp.pad(t, lead)[:, :, -win_buf:])
    return y.reshape(batch, seq, d), (from_t(ck_t), from_t(cv_t), from_t(sk_t), from_t(sv_t), win(wk_t), win(wv_t),
                                      conv_new, st)


def _layer_sample(h8, n_new, caches, page_table, lw, final, w_final):
    w_norm, w_pack, cmp_k_w, cmp_v_w, w_conv, a_log, dt_bias, w_gnorm, w_out = lw
    cmp_k_t = (cmp_k_w[0].T, cmp_k_w[1].T, cmp_k_w[2])
    cmp_v_t = (cmp_v_w[0].T, cmp_v_w[1].T, cmp_v_w[2])
    c_cmp_k, c_cmp_v, c_slc_k, c_slc_v, c_win_k, c_win_v, s_conv, s_gdn = caches
    batch, nr, d = h8.shape
    past = page_table.shape[1] * c_cmp_k.shape[1]
    x2d = h8.reshape(batch * nr, d)
    tables = _rope_tables(past + jnp.arange(nr, dtype=jnp.int32))
    tables = tuple(jnp.tile(t, (SAMPLE_TM // nr, 1)) for t in tables)
    (q, qr, ck, cv, sk, sv, wk, wv, _, zn, qkv, zg, sm) = _project(
        x2d, w_norm, w_pack, tables, SAMPLE_TM, SAMPLE_TM, True, F32, False)
    nsa = _nsa_sample_t(q, qr, sk, sv, wk, wv, sm, zn, c_win_k, c_win_v, c_cmp_k, c_cmp_v, c_slc_k, c_slc_v,
                        page_table, cmp_k_t, cmp_v_t, n_new, NSA_SAMPLE_NB)
    conv0 = jnp.pad(s_conv, ((0, 0), (SUBLANES - (GDN_CONV - 1), 0), (0, 0)))
    go, xp, st = _gdn(qkv, sm, zg, conv0, s_gdn, w_conv, a_log, dt_bias, w_gnorm,
                      batch, nr, GDN_SAMPLE_BBLK, nr, n_new, True, F32)
    y = _out_project(x2d, nsa, go, w_out, w_final, SAMPLE_TM, False, final)
    kv4 = lambda t: t.reshape(batch, nr, KV_HEADS, HEAD_DIM)[:, :n_new]
    win_k = jnp.concatenate([c_win_k[:, n_new:], kv4(wk)], axis=1)
    win_v = jnp.concatenate([c_win_v[:, n_new:], kv4(wv)], axis=1)
    conv_new = xp[:, SUBLANES + n_new - (GDN_CONV - 1):SUBLANES + n_new]
    return y.reshape(batch, nr, d), (kv4(ck), kv4(cv), kv4(sk), kv4(sv), win_k, win_v, conv_new, st)


def kernel(x_prompt, x_sample, cache_cmp_k, cache_cmp_v, cache_slc_k, cache_slc_v, cache_win_k, cache_win_v, state_conv, state_gdn, page_table, w_norm, w_in, pe_cmp_k, w_cmp_k1, w_cmp_k2, pe_cmp_v, w_cmp_v1, w_cmp_v2, w_conv, a_log, dt_bias, w_gdn_norm, w_out, w_final_norm):
    depth = w_in.shape[0]
    n_new = x_sample.shape[1]
    win_buf = cache_win_k.shape[2]
    h_p = x_prompt
    h_s = jnp.pad(x_sample, ((0, 0), (0, SAMPLE_ROWS - n_new), (0, 0)))
    st_p, st_s = [], []
    for layer in range(depth):
        lw = (w_norm[layer], _pack_w_in(w_in[layer]),
              _compress_weights(pe_cmp_k[layer], w_cmp_k1[layer], w_cmp_k2[layer]),
              _compress_weights(pe_cmp_v[layer], w_cmp_v1[layer], w_cmp_v2[layer]),
              w_conv[layer], a_log[layer], dt_bias[layer], w_gdn_norm[layer], w_out[layer])
        final = layer == depth - 1
        h_p, sp = _layer_prompt(h_p, lw, final, w_final_norm, win_buf)
        caches = (cache_cmp_k[layer], cache_cmp_v[layer], cache_slc_k[layer], cache_slc_v[layer],
                  cache_win_k[layer], cache_win_v[layer], state_conv[layer], state_gdn[layer])
        h_s, ss = _layer_sample(h_s, n_new, caches, page_table, lw, final, w_final_norm)
        st_p.append(sp)
        st_s.append(ss)
    outs = [h_p, h_s[:, :n_new]]
    for i in range(8):
        outs.append(jnp.stack([s[i] for s in st_p]))
        outs.append(jnp.stack([s[i] for s in st_s]))
    return tuple(outs)
```

```python
import functools

import numpy as np
import jax
import jax.numpy as jnp
from jax import lax
from jax.experimental import pallas as pl
from jax.experimental.pallas import tpu as pltpu

F32 = jnp.float32
BF16 = jnp.bfloat16

D_MODEL = 1024
HEAD_DIM = 64
Q_HEADS = 8
KV_HEADS = 2
GROUP = Q_HEADS // KV_HEADS
NSA_WIDTH = Q_HEADS * HEAD_DIM
KV_WIDTH = KV_HEADS * HEAD_DIM
CMP_LEN = 32
CMP_STRIDE = 16
CMP_HIDDEN = 128
SEL_BLOCK = 64
TOP_N = 8
WINDOW = 512
FORCE_BONUS = 1.0e4
ROT_DIM = HEAD_DIM // 4
ROPE_THETA = 500000.0
GDN_DK = 128
GDN_DV = 128
GDN_HEADS = 4
GDN_WIDTH = GDN_HEADS * GDN_DV
GDN_CONV = 4
GDN_CONV_CH = 3 * GDN_WIDTH
GDN_CHUNK = 64
NORM_EPS = 1e-6
MASK_VALUE = -1e30

LANES = 128
SUBLANES = 8
VMEM_LIMIT_BYTES = 56 * 1024 * 1024

QPAD_WIDTH = Q_HEADS * LANES
C_Q = 0
C_KV = C_Q + QPAD_WIDTH
C_ZN = C_KV + 6 * KV_WIDTH
C_QKV = C_ZN + NSA_WIDTH
C_ZG = C_QKV + GDN_CONV_CH
C_SM = C_ZG + GDN_WIDTH
N_PACK = C_SM + LANES
SM_B = 3 * Q_HEADS
SM_A = SM_B + GDN_HEADS

TQ = 128
CK_SLC = 256
CK_WIN = 128
N_SUB = 128
N_CMP = N_SUB - CMP_LEN // CMP_STRIDE + 1
SAMPLE_ROWS = 8
KVB_SK = 0
KVB_SV = KVB_SK + KV_WIDTH
KVB_WK = KVB_SV + 2 * KV_WIDTH
KVB_WV = KVB_WK + KV_WIDTH
KVB_WIDTH = KVB_WV + 2 * KV_WIDTH
KVB_COLS = (KVB_SK, KVB_SV, KVB_WK, KVB_WV)
WIN_SPAN = WINDOW + TQ


def _pack_w_in(w_in):
    o_gate = NSA_WIDTH + 6 * KV_WIDTH
    o_zn = o_gate + 3 * Q_HEADS
    o_qkv = o_zn + NSA_WIDTH
    o_b = o_qkv + GDN_CONV_CH
    o_a = o_b + GDN_HEADS
    o_zg = o_a + GDN_HEADS
    d = w_in.shape[0]
    z64 = jnp.zeros((d, HEAD_DIM), w_in.dtype)
    qcols = []
    for hq in range(Q_HEADS):
        wq = w_in[:, hq * HEAD_DIM:(hq + 1) * HEAD_DIM]
        qcols += [wq, z64] if hq // GROUP == 0 else [z64, wq]
    pad = jnp.zeros((d, LANES - SM_A - GDN_HEADS), w_in.dtype)
    return jnp.concatenate(
        qcols + [w_in[:, NSA_WIDTH:o_gate], w_in[:, o_zn:o_qkv], w_in[:, o_qkv:o_b], w_in[:, o_zg:],
                 w_in[:, o_gate:o_zn], w_in[:, o_b:o_a], w_in[:, o_a:o_zg], pad], axis=1)


def _rope_tables(pos):
    half = ROT_DIM // 2
    inv = ROPE_THETA ** (-(jnp.arange(half, dtype=F32) * 2.0 / ROT_DIM))
    ang = pos.astype(F32)[:, None] * inv[None, :]
    cos, sin = jnp.cos(ang), jnp.sin(ang)
    n = pos.shape[0]
    one = jnp.ones((n, HEAD_DIM - ROT_DIM), F32)
    zero = jnp.zeros((n, HEAD_DIM - ROT_DIM), F32)
    zh = jnp.zeros((n, half), F32)
    c64 = jnp.concatenate([cos, cos, one], axis=1)
    a64 = jnp.concatenate([zh, sin, zero], axis=1)
    b64 = jnp.concatenate([-sin, zh, zero], axis=1)
    tile = lambda t: jnp.concatenate([t, t], axis=1)
    return tile(c64), tile(a64), tile(b64)


def _rope128(x, c, a, b):
    half = ROT_DIM // 2
    return x * c + pltpu.roll(x, half, 1) * a + pltpu.roll(x, LANES - half, 1) * b


_NN = (((1,), (0,)), ((), ()))
_NT = (((1,), (1,)), ((), ()))
_TN = (((0,), (0,)), ((), ()))


def _split_bf16(x):
    hi = x.astype(BF16)
    return hi, (x - hi.astype(F32)).astype(BF16)


def _mm(a, b, dims=_NN, hi=False):
    dot = lambda x, y: lax.dot_general(x, y, dims, preferred_element_type=F32)
    if hi:
        a_hi, a_lo = _split_bf16(a.astype(F32))
        b_hi, b_lo = _split_bf16(b.astype(F32))
        return dot(a_hi, b_hi) + (dot(a_lo, b_hi) + dot(a_hi, b_lo))
    return dot(a.astype(BF16), b.astype(BF16))


def _mm_exact_lhs(a_bf16, x, dims=_NN):
    x1 = x.astype(BF16)
    r1 = x - x1.astype(F32)
    x2 = r1.astype(BF16)
    x3 = (r1 - x2.astype(F32)).astype(BF16)
    dot = lambda t: lax.dot_general(a_bf16, t, dims, preferred_element_type=F32)
    return dot(x1) + dot(x2) + dot(x3)


def _sigmoid(x):
    return 1.0 / (1.0 + jnp.exp(-x))


def _silu(x):
    return x * _sigmoid(x)


def _tile_rows(x, n):
    return jnp.concatenate([x] * n, axis=0)


def _tile_lanes(x, n):
    return x if n == 1 else jnp.concatenate([x] * n, axis=1)


def _eye(n, dtype):
    r = lax.broadcasted_iota(jnp.int32, (n, n), 0)
    c = lax.broadcasted_iota(jnp.int32, (n, n), 1)
    return (r == c).astype(dtype)


def _proj_kernel(x_ref, wn_ref, w_ref, c_ref, a_ref, b_ref,
                 q_ref, qr_ref, ck_ref, cv_ref, sk_ref, sv_ref, wk_ref, wv_ref, kvb_ref,
                 zn_ref, qkv_ref, zg_ref, sm_ref, *t_refs, hi_gdn):
    x = x_ref[...]
    ms = jnp.mean(x * x, axis=-1, keepdims=True)
    xn = x * lax.rsqrt(ms + NORM_EPS) * wn_ref[...]
    xb = xn.astype(BF16)
    c, a, b = c_ref[...], a_ref[...], b_ref[...]
    scale = HEAD_DIM ** -0.5
    for j in range(Q_HEADS):
        qj = _mm(xb, w_ref[:, C_Q + j * LANES:C_Q + (j + 1) * LANES])
        q_ref[:, j * LANES:(j + 1) * LANES] = (qj * scale).astype(q_ref.dtype)
        qr_ref[:, j * LANES:(j + 1) * LANES] = (_rope128(qj, c, a, b) * scale).astype(qr_ref.dtype)
    kv_refs = (ck_ref, cv_ref, sk_ref, sv_ref, wk_ref, wv_ref)
    for j in range(6):
        kj = _mm(xb, w_ref[:, C_KV + j * LANES:C_KV + (j + 1) * LANES])
        if j in (2, 4):
            kj = _rope128(kj, c, a, b)
        kv_refs[j][...] = kj
        if j >= 2:
            lo = KVB_COLS[j - 2]
            kvb_ref[:, lo:lo + LANES] = kj.astype(BF16)
        if t_refs:
            t_refs[j][0] = kj.T
    ones = jnp.ones((x.shape[0], LANES), BF16)
    kvb_ref[:, KVB_SV + LANES:KVB_SV + 2 * LANES] = ones
    kvb_ref[:, KVB_WV + LANES:KVB_WV + 2 * LANES] = ones
    zn_ref[...] = _mm(xb, w_ref[:, C_ZN:C_QKV])
    xg = xn if hi_gdn else xb
    for j in range(3):
        lo = C_QKV + j * GDN_WIDTH
        qkv_ref[:, j * GDN_WIDTH:(j + 1) * GDN_WIDTH] = _mm(xg, w_ref[:, lo:lo + GDN_WIDTH], hi=hi_gdn)
    zg_ref[...] = _mm(xb, w_ref[:, C_ZG:C_SM])
    sm_ref[...] = _mm(xg, w_ref[:, C_SM:N_PACK], hi=hi_gdn)


def _project(x2d, w_norm, w_pack, tables, rows_per_seq, tm, hi_gdn, q_dtype, emit_t):
    n = x2d.shape[0]
    nt = rows_per_seq // tm
    row = lambda i: (i, 0)
    tab = lambda i: (i % nt, 0)
    fix = lambda i: (0, 0)
    widths = (QPAD_WIDTH, QPAD_WIDTH) + (KV_WIDTH,) * 6 + (KVB_WIDTH, NSA_WIDTH, GDN_CONV_CH, GDN_WIDTH, LANES)
    dtypes = (q_dtype, q_dtype) + (F32,) * 6 + (BF16, F32, F32, F32, F32)
    out_specs = [pl.BlockSpec((tm, w), row) for w in widths]
    out_shape = [jax.ShapeDtypeStruct((n, w), d) for w, d in zip(widths, dtypes)]
    if emit_t:
        out_specs += [pl.BlockSpec((1, KV_WIDTH, tm), lambda i: (i // nt, 0, i % nt))] * 6
        out_shape += [jax.ShapeDtypeStruct((n // rows_per_seq, KV_WIDTH, rows_per_seq), F32)] * 6
    return pl.pallas_call(
        functools.partial(_proj_kernel, hi_gdn=hi_gdn),
        grid=(n // tm,),
        in_specs=[pl.BlockSpec((tm, D_MODEL), row),
                  pl.BlockSpec((1, D_MODEL), fix),
                  pl.BlockSpec((D_MODEL, N_PACK), fix),
                  pl.BlockSpec((tm, LANES), tab),
                  pl.BlockSpec((tm, LANES), tab),
                  pl.BlockSpec((tm, LANES), tab)],
        out_specs=out_specs,
        out_shape=out_shape,
        compiler_params=pltpu.CompilerParams(dimension_semantics=("arbitrary",),
                                             vmem_limit_bytes=VMEM_LIMIT_BYTES),
        name="in_proj",
    )(x2d, w_norm.reshape(1, D_MODEL), w_pack, *tables)


def _out_kernel(x_ref, nsa_ref, gdn_ref, w_ref, wf_ref, y_ref, *, hi, final):
    acc = (_mm(nsa_ref[...], w_ref[:NSA_WIDTH, :], hi=hi)
           + _mm(gdn_ref[...], w_ref[NSA_WIDTH:, :], hi=hi))
    h = x_ref[...] + acc
    if final:
        ms = jnp.mean(h * h, axis=-1, keepdims=True)
        h = h * lax.rsqrt(ms + NORM_EPS) * wf_ref[...]
    y_ref[...] = h


def _out_project(x2d, nsa, gdn, w_out, w_final, tm, hi, final):
    n = x2d.shape[0]
    row = lambda i: (i, 0)
    fix = lambda i: (0, 0)
    return pl.pallas_call(
        functools.partial(_out_kernel, hi=hi, final=final),
        grid=(n // tm,),
        in_specs=[pl.BlockSpec((tm, D_MODEL), row),
                  pl.BlockSpec((tm, NSA_WIDTH), row),
                  pl.BlockSpec((tm, GDN_WIDTH), row),
                  pl.BlockSpec((D_MODEL, D_MODEL), fix),
                  pl.BlockSpec((1, D_MODEL), fix)],
        out_specs=pl.BlockSpec((tm, D_MODEL), row),
        out_shape=jax.ShapeDtypeStruct((n, D_MODEL), F32),
        compiler_params=pltpu.CompilerParams(dimension_semantics=("arbitrary",),
                                             vmem_limit_bytes=VMEM_LIMIT_BYTES),
        name="out_proj",
    )(x2d, nsa, gdn, w_out, w_final.reshape(1, D_MODEL))


def _compress_weights(pe, w1, w2):
    half = CMP_STRIDE * HEAD_DIM
    z = jnp.zeros((CMP_STRIDE, HEAD_DIM, CMP_HIDDEN), w1.dtype)

    def place(wpart, h):
        wp = wpart.reshape(CMP_STRIDE, HEAD_DIM, CMP_HIDDEN)
        parts = [wp, z] if h == 0 else [z, wp]
        return jnp.stack(parts, axis=1).reshape(CMP_STRIDE * KV_WIDTH, CMP_HIDDEN)

    w1big = jnp.concatenate([place(w1[:half], 0), place(w1[:half], 1),
                             place(w1[half:], 0), place(w1[half:], 1)], axis=1)
    zz = jnp.zeros_like(w2)
    w2big = jnp.concatenate([jnp.concatenate([w2, zz], axis=1), jnp.concatenate([zz, w2], axis=1)], axis=0)
    pe_a = jnp.tile(pe[:CMP_STRIDE], (1, KV_HEADS)).reshape(1, CMP_STRIDE * KV_WIDTH)
    pe_b = jnp.tile(pe[CMP_STRIDE:], (1, KV_HEADS)).reshape(1, CMP_STRIDE * KV_WIDTH)
    pe2 = jnp.concatenate([pe_a, pe_b, jnp.zeros((SUBLANES - 2, CMP_STRIDE * KV_WIDTH), pe.dtype)], axis=0)
    return w1big.astype(BF16), w2big.astype(BF16), pe2


def _compress(sub_rows, w1_ref, w2_ref, pe_ref):
    hid = 2 * CMP_HIDDEN
    w1 = w1_ref[...]
    ab = _mm(sub_rows, w1)
    pe = pe_ref[...]
    pe_hi = pe.astype(BF16)
    pe_lo = (pe - pe_hi.astype(F32)).astype(BF16)
    r = _mm(pe_hi, w1) + _mm(pe_lo, w1)
    bias = r[0:1, :hid] + r[1:2, hid:]
    h = ab[:, :hid] + pltpu.roll(ab[:, hid:], N_SUB - 1, 0) + bias
    return _mm(_silu(h), w2_ref[...])


def _overlap_t(ns):
    c0 = np.arange(N_CMP)[None, :] * CMP_STRIDE
    b0 = np.arange(ns)[:, None] * SEL_BLOCK
    ov = np.minimum(c0 + CMP_LEN, b0 + SEL_BLOCK) - np.maximum(c0, b0)
    out = np.zeros((LANES, LANES), np.float32)
    out[:ns, :N_CMP] = np.maximum(ov, 0) / CMP_LEN
    return jnp.asarray(out, BF16)


def _cmp_attention(q_stack, ck, cv, cvalid, groups):
    s = _mm(q_stack, ck, _NT)
    cm = _tile_rows(cvalid, groups)
    s = jnp.where(cm, s, MASK_VALUE)
    m = jnp.max(s, axis=1, keepdims=True)
    e = jnp.exp(s - m)
    p = e / jnp.sum(e, axis=1, keepdims=True) * cm.astype(F32)
    return _mm(p, cv), p


def _select_blocks(imp_t, n_idx, q_blk, n_rows):
    forced = (n_idx == 0) | (n_idx == q_blk) | (n_idx == q_blk - 1)
    allowed = n_idx <= q_blk
    v = jnp.where(allowed, imp_t + FORCE_BONUS * forced.astype(F32), MASK_VALUE)
    rank = jnp.zeros(v.shape, F32)
    for j in range(n_rows):
        vj = v[j:j + 1, :]
        ge = jnp.where(vj >= v, 1.0, 0.0)
        gt = jnp.where(vj > v, 1.0, 0.0)
        rank = rank + jnp.where(n_idx > j, ge, gt)
    return ((rank < TOP_N) & allowed).astype(F32)


def _flash_init(m_s, l_s, acc_s):
    m_s[...] = jnp.full(m_s.shape, MASK_VALUE, F32)
    l_s[...] = jnp.zeros(l_s.shape, F32)
    acc_s[...] = jnp.zeros(acc_s.shape, F32)


def _flash_step(s, v_c, m_s, l_s, acc_s):
    m_prev = m_s[...]
    m_next = jnp.maximum(m_prev, jnp.max(s, axis=1, keepdims=True))
    alpha = jnp.exp(m_prev - m_next)
    p = jnp.exp(s - _tile_lanes(m_next, s.shape[1] // LANES))
    l_s[...] = alpha * l_s[...] + jnp.sum(p, axis=1, keepdims=True)
    acc_s[...] = acc_s[...] * alpha + _mm(p, v_c)
    m_s[...] = m_next


def _merge_heads(o_sum, kvh, tq):
    lane = lax.broadcasted_iota(jnp.int32, (tq, LANES), 1)
    slabs = []
    for pair in range(GROUP // 2):
        halves = []
        for par in range(2):
            o = o_sum[(2 * pair + par) * tq:(2 * pair + par + 1) * tq]
            halves.append(o if par == kvh else pltpu.roll(o, HEAD_DIM, 1))
        slabs.append(jnp.where(lane < HEAD_DIM, halves[0], halves[1]))
    return slabs


def _nsa_prompt_kernel(q_ref, qr_ref, kvb_ref, ckr_ref, cvr_ref, sm_ref, zn_ref,
                       wk1_ref, wk2_ref, pek_ref, wv1_ref, wv2_ref, pev_ref, ovt_ref, e_ref,
                       o_ref, ck_s, cv_s, s_s, mrun_s, m_s, acc_s):
    i = pl.program_id(1)

    @pl.when(i == 0)
    def _():
        ck_s[...] = _compress(ckr_ref[0], wk1_ref, wk2_ref, pek_ref).astype(BF16)
        cv_s[...] = _compress(cvr_ref[0], wv1_ref, wv2_ref, pev_ref).astype(BF16)

    kvs = range(KV_HEADS)
    t0 = i * TQ
    rows = GROUP * TQ
    row = lax.broadcasted_iota(jnp.int32, (TQ, LANES), 0)
    col = lax.broadcasted_iota(jnp.int32, (TQ, LANES), 1)
    cvalid = (CMP_STRIDE * col + (CMP_LEN - 1) <= t0 + row) & (col < N_CMP)
    cm = _tile_rows(cvalid, GROUP)
    cm_f = cm.astype(F32)
    sig = _sigmoid(sm_ref[...])

    def gate(branch, kvh):
        cols = [sig[:, branch * Q_HEADS + h:branch * Q_HEADS + h + 1] for h in range(kvh * GROUP, (kvh + 1) * GROUP)]
        return jnp.concatenate([jnp.broadcast_to(c, (TQ, LANES)) for c in cols], axis=0)

    def stack(ref, kvh):
        return jnp.concatenate([ref[:, h * LANES:(h + 1) * LANES] for h in range(kvh * GROUP, (kvh + 1) * GROUP)], axis=0)

    q_raw = [stack(q_ref, k) for k in kvs]
    q_rot = [stack(qr_ref, k) for k in kvs]

    ck, cv = ck_s[...], cv_s[...]
    s = [jnp.where(cm, _mm(q_raw[k], ck, _NT), MASK_VALUE) for k in kvs]
    e = [jnp.exp(s[k] - jnp.max(s[k], axis=1, keepdims=True)) for k in kvs]
    p = [e[k] / jnp.sum(e[k], axis=1, keepdims=True) * cm_f for k in kvs]
    o_sum = [gate(0, k) * _mm(p[k], cv) for k in kvs]
    p_sum = jnp.concatenate([p[k][0:TQ] + p[k][TQ:2 * TQ] + p[k][2 * TQ:3 * TQ] + p[k][3 * TQ:4 * TQ] for k in kvs],
                            axis=0)
    ns_rows = 32
    nq = KV_HEADS * TQ
    imp_t = _mm_exact_lhs(ovt_ref[...], p_sum, _NT)[0:ns_rows]
    n_idx = lax.broadcasted_iota(jnp.int32, (ns_rows, nq), 0)
    qb_t = (t0 + lax.broadcasted_iota(jnp.int32, (ns_rows, nq), 1) % TQ) // SEL_BLOCK
    sel_t = _select_blocks(imp_t, n_idx, qb_t, ns_rows)
    sel_t = jnp.concatenate([sel_t, jnp.zeros((LANES - ns_rows, nq), F32)], axis=0).astype(BF16)
    eye = _eye(TQ, BF16)
    sel = [_mm(eye, sel_t[:, k * TQ:(k + 1) * TQ], _NT).astype(BF16) for k in kvs]

    col_s = lax.broadcasted_iota(jnp.int32, (TQ, CK_SLC), 1)
    row_s = lax.broadcasted_iota(jnp.int32, (TQ, CK_SLC), 0)
    n_chunks = (t0 + TQ + CK_SLC - 1) // CK_SLC
    mrun_s[...] = jnp.full(mrun_s.shape, MASK_VALUE, F32)

    def score_body(j, carry):
        k0 = pl.multiple_of(j * CK_SLC, CK_SLC)
        k_c = kvb_ref[pl.ds(k0, CK_SLC), KVB_SK:KVB_SK + LANES]
        causal = k0 + col_s <= t0 + row_s
        e_j = e_ref[j]
        sc = [_mm(q_rot[k], k_c, _NT) for k in kvs]
        picked = [_mm(sel[k], e_j) for k in kvs]
        for k in kvs:
            bias = jnp.where((picked[k] > 0.5) & causal, 0.0, MASK_VALUE)
            sk = sc[k] + _tile_rows(bias, GROUP)
            s_s[k, j] = sk
            mrun_s[k] = jnp.maximum(mrun_s[k], jnp.maximum(sk[:, :LANES], sk[:, LANES:]))
        return carry

    lax.fori_loop(0, n_chunks, score_body, 0)
    for k in kvs:
        m_s[k] = jnp.broadcast_to(jnp.max(mrun_s[k], axis=1, keepdims=True), (rows, LANES))
    acc_s[...] = jnp.zeros(acc_s.shape, F32)

    def value_body(j, carry):
        k0 = pl.multiple_of(j * CK_SLC, CK_SLC)
        v_c = kvb_ref[pl.ds(k0, CK_SLC), KVB_SV:KVB_SV + 2 * LANES]
        pr = [jnp.exp(s_s[k, j] - _tile_lanes(m_s[k], CK_SLC // LANES)) for k in kvs]
        for k in kvs:
            acc_s[k] += _mm(pr[k], v_c)
        return carry

    lax.fori_loop(0, n_chunks, value_body, 0)
    for k in kvs:
        acc = acc_s[k]
        o_sum[k] = o_sum[k] + gate(1, k) * (acc[:, :LANES] / acc[:, LANES:])

    ks0 = pl.multiple_of(jnp.maximum(t0 - WINDOW, 0), TQ)
    k_w = kvb_ref[pl.ds(ks0, WIN_SPAN), KVB_WK:KVB_WK + LANES]
    v_w = kvb_ref[pl.ds(ks0, WIN_SPAN), KVB_WV:KVB_WV + 2 * LANES]
    diff = (t0 + lax.broadcasted_iota(jnp.int32, (TQ, WIN_SPAN), 0)) - (ks0 + lax.broadcasted_iota(jnp.int32, (TQ, WIN_SPAN), 1))
    bias_w = _tile_rows(jnp.where((diff >= 0) & (diff <= WINDOW), 0.0, MASK_VALUE), GROUP)
    sw = [_mm(q_rot[k], k_w, _NT) + bias_w for k in kvs]
    pw = [jnp.exp(sw[k] - jnp.max(sw[k], axis=1, keepdims=True)) for k in kvs]
    rw = [_mm(pw[k], v_w) for k in kvs]
    for k in kvs:
        o_all = o_sum[k] + gate(2, k) * (rw[k][:, :LANES] / rw[k][:, LANES:])
        for pair, slab in enumerate(_merge_heads(o_all, k, TQ)):
            lo = (k * (GROUP // 2) + pair) * LANES
            o_ref[:, lo:lo + LANES] = (slab * _silu(zn_ref[:, lo:lo + LANES])).astype(o_ref.dtype)


def _nsa_prompt(q, qr, kvb, ck, cv, sm, zn, cmp_k_w, cmp_v_w, batch, seq):
    nt = seq // TQ
    ns = seq // SEL_BLOCK
    sub_w = CMP_STRIDE * KV_WIDTH
    keys = np.arange(seq)
    e = (np.arange(LANES)[None, :, None] == (keys // SEL_BLOCK).reshape(seq // CK_SLC, 1, CK_SLC))
    e = jnp.asarray(e, BF16)
    tile = lambda b, i: (b * nt + i, 0)
    per_b = lambda b, i: (b, 0)
    per_b3 = lambda b, i: (b, 0, 0)
    fix2 = lambda b, i: (0, 0)
    fix3 = lambda b, i: (0, 0, 0)
    wspecs = [pl.BlockSpec((sub_w, 4 * CMP_HIDDEN), fix2),
              pl.BlockSpec((2 * CMP_HIDDEN, KV_WIDTH), fix2),
              pl.BlockSpec((SUBLANES, sub_w), fix2)]
    rows = GROUP * TQ
    return pl.pallas_call(
        _nsa_prompt_kernel,
        grid=(batch, nt),
        in_specs=[pl.BlockSpec((TQ, QPAD_WIDTH), tile),
                  pl.BlockSpec((TQ, QPAD_WIDTH), tile),
                  pl.BlockSpec((seq, KVB_WIDTH), per_b),
                  pl.BlockSpec((1, seq // CMP_STRIDE, sub_w), per_b3),
                  pl.BlockSpec((1, seq // CMP_STRIDE, sub_w), per_b3),
                  pl.BlockSpec((TQ, LANES), tile),
                  pl.BlockSpec((TQ, NSA_WIDTH), tile)] + wspecs + wspecs + [
                  pl.BlockSpec((LANES, LANES), fix2),
                  pl.BlockSpec((seq // CK_SLC, LANES, CK_SLC), fix3)],
        out_specs=pl.BlockSpec((TQ, NSA_WIDTH), tile),
        out_shape=jax.ShapeDtypeStruct((batch * seq, NSA_WIDTH), BF16),
        scratch_shapes=[pltpu.VMEM((N_SUB, KV_WIDTH), BF16), pltpu.VMEM((N_SUB, KV_WIDTH), BF16),
                        pltpu.VMEM((KV_HEADS, seq // CK_SLC, rows, CK_SLC), F32),
                        pltpu.VMEM((KV_HEADS, rows, LANES), F32), pltpu.VMEM((KV_HEADS, rows, LANES), F32),
                        pltpu.VMEM((KV_HEADS, rows, 2 * LANES), F32)],
        compiler_params=pltpu.CompilerParams(dimension_semantics=("arbitrary", "arbitrary"),
                                             vmem_limit_bytes=VMEM_LIMIT_BYTES),
        name="nsa_prompt",
    )(q, qr, kvb, ck.reshape(batch, seq // CMP_STRIDE, sub_w), cv.reshape(batch, seq // CMP_STRIDE, sub_w),
      sm, zn, *cmp_k_w, *cmp_v_w, _overlap_t(ns), e)


def _softmax_rows(s):
    m = jnp.max(s, axis=1, keepdims=True)
    e = jnp.exp(s - m)
    return e / jnp.sum(e, axis=1, keepdims=True)


def _nsa_sample_kernel(pt_ref, q_ref, qr_ref, skn_ref, svn_ref, wkn_ref, wvn_ref, sm_ref, zn_ref,
                       wkc_ref, wvc_ref, *rest, n_pages, page, past, n_new, win_keys):
    del pt_ref
    cmpk, cmpv = rest[0:n_pages], rest[n_pages:2 * n_pages]
    slck, slcv = rest[2 * n_pages:3 * n_pages], rest[3 * n_pages:4 * n_pages]
    wk1_ref, wk2_ref, pek_ref, wv1_ref, wv2_ref, pev_ref, ovt_ref, e_ref = rest[4 * n_pages:4 * n_pages + 8]
    o_ref, wko_ref, wvo_ref = rest[4 * n_pages + 8:4 * n_pages + 11]
    kall, vall, kwin, vwin = rest[4 * n_pages + 11:]
    nr = SAMPLE_ROWS
    rows = Q_HEADS * nr
    all_keys = kall.shape[0]
    win_buf = wkc_ref.shape[1]
    zeros8 = jnp.zeros((nr, LANES), F32)

    for p in range(n_pages):
        kall[p * page:(p + 1) * page, :] = slck[p][0].astype(BF16)
        vall[p * page:(p + 1) * page, :] = slcv[p][0].astype(BF16)
    kall[past:past + 2 * nr, :] = jnp.concatenate([skn_ref[...], zeros8], axis=0).astype(BF16)
    vall[past:past + 2 * nr, :] = jnp.concatenate([svn_ref[...], zeros8], axis=0).astype(BF16)
    kall[past + 2 * nr:, :] = jnp.zeros((all_keys - past - 2 * nr, LANES), BF16)
    vall[past + 2 * nr:, :] = jnp.zeros((all_keys - past - 2 * nr, LANES), BF16)
    kwin[0:win_buf, :] = wkc_ref[0].astype(BF16)
    vwin[0:win_buf, :] = wvc_ref[0].astype(BF16)
    kwin[win_buf:win_buf + 2 * nr, :] = jnp.concatenate([wkn_ref[...], zeros8], axis=0).astype(BF16)
    vwin[win_buf:win_buf + 2 * nr, :] = jnp.concatenate([wvn_ref[...], zeros8], axis=0).astype(BF16)
    kwin[win_buf + 2 * nr:, :] = jnp.zeros((win_keys - win_buf - 2 * nr, LANES), BF16)
    vwin[win_buf + 2 * nr:, :] = jnp.zeros((win_keys - win_buf - 2 * nr, LANES), BF16)

    ck = _compress(jnp.concatenate([r[0] for r in cmpk], axis=0), wk1_ref, wk2_ref, pek_ref)
    cv = _compress(jnp.concatenate([r[0] for r in cmpv], axis=0), wv1_ref, wv2_ref, pev_ref)

    q_raw = jnp.concatenate([q_ref[:, h * LANES:(h + 1) * LANES] for h in range(Q_HEADS)], axis=0)
    q_rot = jnp.concatenate([qr_ref[:, h * LANES:(h + 1) * LANES] for h in range(Q_HEADS)], axis=0)

    tok = lax.broadcasted_iota(jnp.int32, (nr, LANES), 0)
    col = lax.broadcasted_iota(jnp.int32, (nr, LANES), 1)
    cvalid = (CMP_STRIDE * col + (CMP_LEN - 1) <= past + tok) & (col < N_CMP)
    o_cmp, p = _cmp_attention(q_raw, ck, cv, cvalid, Q_HEADS)

    p_sum = []
    for kvh in range(KV_HEADS):
        base = kvh * GROUP * nr
        p_sum.append(sum(p[base + g * nr:base + (g + 1) * nr] for g in range(GROUP)))
    p_sum = jnp.concatenate(p_sum, axis=0)
    ns = (past + n_new + SEL_BLOCK - 1) // SEL_BLOCK
    ns_rows = -(-ns // SUBLANES) * SUBLANES
    nq = KV_HEADS * nr
    imp_t = _mm_exact_lhs(ovt_ref[...], p_sum, _NT)[0:ns_rows]
    n_idx = lax.broadcasted_iota(jnp.int32, (ns_rows, nq), 0)
    qb_t = (past + lax.broadcasted_iota(jnp.int32, (ns_rows, nq), 1) % nr) // SEL_BLOCK
    sel_t = _select_blocks(imp_t, n_idx, qb_t, ns)
    sel_t = jnp.concatenate([sel_t, jnp.zeros((LANES - ns_rows, nq), F32)], axis=0)
    sel = _mm(_eye(nq, BF16), sel_t, _NT)
    sel_rows = jnp.concatenate([_tile_rows(sel[kvh * nr:(kvh + 1) * nr], GROUP) for kvh in range(KV_HEADS)], axis=0)

    picked = _mm(sel_rows, e_ref[...])
    q_pos = past + lax.broadcasted_iota(jnp.int32, (rows, all_keys), 0) % nr
    k_pos = lax.broadcasted_iota(jnp.int32, (rows, all_keys), 1)
    ok = (picked > 0.5) & (k_pos <= q_pos)
    s = _mm(q_rot, kall[...], _NT) + jnp.where(ok, 0.0, MASK_VALUE)
    o_slc = _mm(_softmax_rows(s), vall[...])

    q_pos = past + lax.broadcasted_iota(jnp.int32, (rows, win_keys), 0) % nr
    k_pos = past - win_buf + lax.broadcasted_iota(jnp.int32, (rows, win_keys), 1)
    diff = q_pos - k_pos
    ok = (diff >= 0) & (diff <= WINDOW)
    s = _mm(q_rot, kwin[...], _NT) + jnp.where(ok, 0.0, MASK_VALUE)
    o_win = _mm(_softmax_rows(s), vwin[...])

    sig = _sigmoid(sm_ref[...])

    def gate(branch):
        cols = [sig[:, branch * Q_HEADS + h:branch * Q_HEADS + h + 1] for h in range(Q_HEADS)]
        return jnp.concatenate([jnp.broadcast_to(c, (nr, LANES)) for c in cols], axis=0)

    o_all = gate(0) * o_cmp + gate(1) * o_slc + gate(2) * o_win
    for kvh in range(KV_HEADS):
        o_kv = o_all[kvh * GROUP * nr:(kvh + 1) * GROUP * nr]
        for pair, slab in enumerate(_merge_heads(o_kv, kvh, nr)):
            lo = (kvh * (GROUP // 2) + pair) * LANES
            o_ref[:, lo:lo + LANES] = (slab * _silu(zn_ref[:, lo:lo + LANES])).astype(o_ref.dtype)

    for cache_ref, new_ref, out_ref in ((wkc_ref, wkn_ref, wko_ref), (wvc_ref, wvn_ref, wvo_ref)):
        shifted = pltpu.roll(cache_ref[0], win_buf - n_new, 0)
        out_ref[0] = shifted
        tail = jnp.where(tok < nr - n_new, shifted[win_buf - nr:], pltpu.roll(new_ref[...], nr - n_new, 0))
        out_ref[0, win_buf - nr:, :] = tail


def _nsa_sample(q, qr, sk, sv, wk, wv, sm, zn, win_k, win_v, cmp_k, cmp_v, slc_k, slc_v, page_table,
                cmp_k_w, cmp_v_w, n_new):
    batch, n_pages = page_table.shape
    n_pool, page = cmp_k.shape[0], cmp_k.shape[1]
    past = n_pages * page
    nr = SAMPLE_ROWS
    sub_w = CMP_STRIDE * KV_WIDTH
    win_buf = win_k.shape[1]
    all_keys = -(-(past + 2 * nr) // LANES) * LANES
    win_keys = -(-(win_buf + 2 * nr) // LANES) * LANES
    ns = (past + n_new + SEL_BLOCK - 1) // SEL_BLOCK
    e = jnp.asarray(np.arange(LANES)[:, None] == (np.arange(all_keys) // SEL_BLOCK)[None, :], BF16)
    cmp_k = cmp_k.reshape(n_pool, page // CMP_STRIDE, sub_w)
    cmp_v = cmp_v.reshape(n_pool, page // CMP_STRIDE, sub_w)
    slc_k = slc_k.reshape(n_pool, page, KV_WIDTH)
    slc_v = slc_v.reshape(n_pool, page, KV_WIDTH)
    win_k = win_k.reshape(batch, win_buf, KV_WIDTH)
    win_v = win_v.reshape(batch, win_buf, KV_WIDTH)

    row = lambda b, pt: (b, 0)
    per_b3 = lambda b, pt: (b, 0, 0)
    fix2 = lambda b, pt: (0, 0)
    page_map = lambda p: (lambda b, pt: (pt[b, p], 0, 0))
    wspecs = [pl.BlockSpec((sub_w, 4 * CMP_HIDDEN), fix2),
              pl.BlockSpec((2 * CMP_HIDDEN, KV_WIDTH), fix2),
              pl.BlockSpec((SUBLANES, sub_w), fix2)]
    in_specs = ([pl.BlockSpec((nr, QPAD_WIDTH), row)] * 2 + [pl.BlockSpec((nr, KV_WIDTH), row)] * 4
                + [pl.BlockSpec((nr, LANES), row), pl.BlockSpec((nr, NSA_WIDTH), row)]
                + [pl.BlockSpec((1, win_buf, KV_WIDTH), per_b3)] * 2
                + [pl.BlockSpec((1, page // CMP_STRIDE, sub_w), page_map(p)) for p in range(n_pages)] * 1
                + [pl.BlockSpec((1, page // CMP_STRIDE, sub_w), page_map(p)) for p in range(n_pages)]
                + [pl.BlockSpec((1, page, KV_WIDTH), page_map(p)) for p in range(n_pages)]
                + [pl.BlockSpec((1, page, KV_WIDTH), page_map(p)) for p in range(n_pages)]
                + wspecs + wspecs
                + [pl.BlockSpec((LANES, LANES), fix2), pl.BlockSpec((LANES, all_keys), fix2)])
    grid_spec = pltpu.PrefetchScalarGridSpec(
        num_scalar_prefetch=1,
        grid=(batch,),
        in_specs=in_specs,
        out_specs=[pl.BlockSpec((nr, NSA_WIDTH), row),
                   pl.BlockSpec((1, win_buf, KV_WIDTH), per_b3),
                   pl.BlockSpec((1, win_buf, KV_WIDTH), per_b3)],
        scratch_shapes=[pltpu.VMEM((all_keys, KV_WIDTH), BF16), pltpu.VMEM((all_keys, KV_WIDTH), BF16),
                        pltpu.VMEM((win_keys, KV_WIDTH), BF16), pltpu.VMEM((win_keys, KV_WIDTH), BF16)])
    return pl.pallas_call(
        functools.partial(_nsa_sample_kernel, n_pages=n_pages, page=page, past=past, n_new=n_new,
                          win_keys=win_keys),
        grid_spec=grid_spec,
        out_shape=[jax.ShapeDtypeStruct((batch * nr, NSA_WIDTH), F32),
                   jax.ShapeDtypeStruct((batch, win_buf, KV_WIDTH), F32),
                   jax.ShapeDtypeStruct((batch, win_buf, KV_WIDTH), F32)],
        compiler_params=pltpu.CompilerParams(dimension_semantics=("arbitrary",),
                                             vmem_limit_bytes=VMEM_LIMIT_BYTES),
        name="nsa_sample",
    )(page_table, q, qr, sk, sv, wk, wv, sm, zn, win_k, win_v,
      *([cmp_k] * n_pages), *([cmp_v] * n_pages), *([slc_k] * n_pages), *([slc_v] * n_pages),
      *cmp_k_w, *cmp_v_w, _overlap_t(ns), e)


def _compress_weights_t(pe, w1, w2):
    w1big, w2big, pe2 = _compress_weights(pe, w1, w2)
    return w1big.T, w2big.T, pe2


def _nsa_sample_t_kernel(pt_ref, q_ref, qr_ref, skn_ref, svn_ref, wkn_ref, wvn_ref, sm_ref, zn_ref,
                         wkc_ref, wvc_ref, cmpk_hbm, cmpv_hbm, slck_hbm, slcv_hbm,
                         w1k_ref, w2k_ref, pek_ref, w1v_ref, w2v_ref, pev_ref, ovt_ref, e_ref,
                         o_ref, kt_s, vt_s, kwt_s, vwt_s, pages_s, sems, *, nb, n_pages, page, past, n_new):
    step = pl.program_id(0)
    last = pl.num_programs(0) - 1
    slot = step % 2
    caches = (cmpk_hbm, cmpv_hbm, slck_hbm, slcv_hbm)
    page_ids = [(j, c, p) for j in range(nb) for c in range(len(caches)) for p in range(n_pages)]

    def page_copy(at_step, at_slot, j, c, p):
        src = caches[c].at[pt_ref[at_step * nb + j, p]]
        return pltpu.make_async_copy(src, pages_s.at[at_slot, j, c, p], sems.at[at_slot, c])

    @pl.when(step == 0)
    def _():
        for ids in page_ids:
            page_copy(0, 0, *ids).start()

    nxt = jnp.minimum(step + 1, last)
    for ids in page_ids:
        page_copy(nxt, 1 - slot, *ids).start()
    for ids in page_ids:
        page_copy(step, slot, *ids).wait()

    cmpk, cmpv, slck, slcv = ([pages_s.at[slot, j, c, p] for j in range(nb) for p in range(n_pages)]
                              for c in range(len(caches)))
    nr = SAMPLE_ROWS
    rows = Q_HEADS * nr
    win_buf = wkc_ref.shape[2]
    bs = range(nb)
    hid = 2 * CMP_HIDDEN
    zpad = jnp.zeros((LANES - nr, LANES), F32)

    def new_rows(ref, b):
        return jnp.concatenate([ref[b * nr:(b + 1) * nr, :], zpad], axis=0).astype(BF16)

    def stack(ref, b):
        return jnp.concatenate([ref[b * nr:(b + 1) * nr, h * LANES:(h + 1) * LANES] for h in range(Q_HEADS)],
                               axis=0).astype(BF16)

    for b in bs:
        for p in range(n_pages):
            kt_s[b, :, p * page:(p + 1) * page] = slck[b * n_pages + p][...].astype(BF16)
            vt_s[b, :, p * page:(p + 1) * page] = slcv[b * n_pages + p][...].astype(BF16)
        kwt_s[b] = wkc_ref[b].astype(BF16)
        vwt_s[b] = wvc_ref[b].astype(BF16)
    mi = lax.broadcasted_iota(jnp.int32, (page, page), 0)
    ki = lax.broadcasted_iota(jnp.int32, (page, page), 1)
    per_page = page // CMP_STRIDE
    perm = (ki == CMP_STRIDE * (mi % per_page) + mi // per_page).astype(BF16)
    rows_t = [[[_mm(perm, pages[b * n_pages + p][...], _NT) for p in range(n_pages)] for pages in (cmpk, cmpv)]
              for b in bs]

    q_raw = [stack(q_ref, b) for b in bs]
    q_rot = [stack(qr_ref, b) for b in bs]

    sw = [jnp.concatenate([_mm(q_rot[b], kwt_s[b]), _mm(q_rot[b], new_rows(wkn_ref, b), _NT)], axis=1) for b in bs]
    ss = [jnp.concatenate([_mm(q_rot[b], kt_s[b]), _mm(q_rot[b], new_rows(skn_ref, b), _NT)], axis=1) for b in bs]

    def sub_blocks(b, c):
        cols = [jnp.concatenate([rows_t[b][c][p][l * per_page:(l + 1) * per_page] for p in range(n_pages)], axis=0)
                for l in range(CMP_STRIDE)]
        return jnp.concatenate(cols, axis=1)

    ckt, cvt = [], []
    for c, (w1_ref, w2_ref, pe_ref, out) in enumerate(((w1k_ref, w2k_ref, pek_ref, ckt), (w1v_ref, w2v_ref, pev_ref, cvt))):
        pe = pe_ref[...]
        subs = [sub_blocks(b, c) for b in bs]
        h_a = [_mm(w1_ref[0:hid, :], subs[b] + pe[0:1, :], _NT) for b in bs]
        h_b = [_mm(w1_ref[hid:2 * hid, :], subs[b] + pe[1:2, :], _NT) for b in bs]
        h = [_silu(h_a[b] + pltpu.roll(h_b[b], N_SUB - 1, 1)) for b in bs]
        out.extend(_mm(w2_ref[...], h[b]) for b in bs)

    tok = lax.broadcasted_iota(jnp.int32, (nr, LANES), 0)
    col = lax.broadcasted_iota(jnp.int32, (nr, LANES), 1)
    cm = _tile_rows((CMP_STRIDE * col + (CMP_LEN - 1) <= past + tok) & (col < N_CMP), Q_HEADS)
    cm_f = cm.astype(F32)
    sc = [jnp.where(cm, _mm(q_raw[b], ckt[b]), MASK_VALUE) for b in bs]
    ec = [jnp.exp(sc[b] - jnp.max(sc[b], axis=1, keepdims=True)) for b in bs]
    pc = [ec[b] / jnp.sum(ec[b], axis=1, keepdims=True) * cm_f for b in bs]
    o_cmp = [_mm(pc[b], cvt[b], _NT) for b in bs]
    p_sum = jnp.concatenate(
        [sum(pc[b][(kvh * GROUP + g) * nr:(kvh * GROUP + g + 1) * nr] for g in range(GROUP))
         for b in bs for kvh in range(KV_HEADS)], axis=0)
    ns = (past + n_new + SEL_BLOCK - 1) // SEL_BLOCK
    ns_rows = -(-ns // SUBLANES) * SUBLANES
    nq = nb * KV_HEADS * nr
    imp_t = _mm_exact_lhs(ovt_ref[...], p_sum, _NT)[0:ns_rows]
    n_idx = lax.broadcasted_iota(jnp.int32, (ns_rows, nq), 0)
    qb_t = (past + lax.broadcasted_iota(jnp.int32, (ns_rows, nq), 1) % nr) // SEL_BLOCK
    sel_t = _select_blocks(imp_t, n_idx, qb_t, ns)
    sel_t = jnp.concatenate([sel_t, jnp.zeros((LANES - ns_rows, nq), F32)], axis=0)
    sel = _mm(_eye(nq, BF16), sel_t, _NT)

    all_keys = past + LANES
    q_pos = past + lax.broadcasted_iota(jnp.int32, (rows, all_keys), 0) % nr
    causal = lax.broadcasted_iota(jnp.int32, (rows, all_keys), 1) <= q_pos
    o_slc = []
    for b in bs:
        sel_rows = jnp.concatenate(
            [_tile_rows(sel[(b * KV_HEADS + kvh) * nr:(b * KV_HEADS + kvh + 1) * nr], GROUP) for kvh in range(KV_HEADS)],
            axis=0)
        ok = (_mm(sel_rows, e_ref[...]) > 0.5) & causal
        s = ss[b] + jnp.where(ok, 0.0, MASK_VALUE)
        pr = jnp.exp(s - jnp.max(s, axis=1, keepdims=True))
        pr = pr / jnp.sum(pr, axis=1, keepdims=True)
        o_slc.append(_mm(pr[:, :past], vt_s[b], _NT) + _mm(pr[:, past:], new_rows(svn_ref, b)))

    win_keys = win_buf + LANES
    q_pos = past + lax.broadcasted_iota(jnp.int32, (rows, win_keys), 0) % nr
    diff = q_pos - (past - win_buf + lax.broadcasted_iota(jnp.int32, (rows, win_keys), 1))
    bias_w = jnp.where((diff >= 0) & (diff <= WINDOW), 0.0, MASK_VALUE)
    o_win = []
    for b in bs:
        s = sw[b] + bias_w
        pr = jnp.exp(s - jnp.max(s, axis=1, keepdims=True))
        pr = pr / jnp.sum(pr, axis=1, keepdims=True)
        o_win.append(_mm(pr[:, :win_buf], vwt_s[b], _NT) + _mm(pr[:, win_buf:], new_rows(wvn_ref, b)))

    for b in bs:
        sig = _sigmoid(sm_ref[b * nr:(b + 1) * nr, :])

        def gate(branch):
            cols = [sig[:, branch * Q_HEADS + h:branch * Q_HEADS + h + 1] for h in range(Q_HEADS)]
            return jnp.concatenate([jnp.broadcast_to(c, (nr, LANES)) for c in cols], axis=0)

        o_all = gate(0) * o_cmp[b] + gate(1) * o_slc[b] + gate(2) * o_win[b]
        for kvh in range(KV_HEADS):
            o_kv = o_all[kvh * GROUP * nr:(kvh + 1) * GROUP * nr]
            for pair, slab in enumerate(_merge_heads(o_kv, kvh, nr)):
                lo = (kvh * (GROUP // 2) + pair) * LANES
                o_ref[b * nr:(b + 1) * nr, lo:lo + LANES] = slab * _silu(zn_ref[b * nr:(b + 1) * nr, lo:lo + LANES])

    @pl.when(step == last)
    def _():
        for ids in page_ids:
            page_copy(nxt, 1 - slot, *ids).wait()


def _nsa_sample_t(q, qr, sk, sv, wk, wv, sm, zn, win_k, win_v, cmp_k, cmp_v, slc_k, slc_v, page_table,
                  cmp_k_w, cmp_v_w, n_new, nb):
    batch, n_pages = page_table.shape
    n_pool, page = cmp_k.shape[0], cmp_k.shape[1]
    past = n_pages * page
    nr = SAMPLE_ROWS
    win_buf = win_k.shape[1]
    ns = (past + n_new + SEL_BLOCK - 1) // SEL_BLOCK
    all_keys = past + LANES
    e = jnp.asarray(np.arange(LANES)[:, None] == (np.arange(all_keys) // SEL_BLOCK)[None, :], BF16)
    tview = lambda c: jnp.transpose(c, (0, 2, 3, 1)).reshape(c.shape[0], KV_WIDTH, c.shape[1])
    cmp_k, cmp_v, slc_k, slc_v, win_k, win_v = (tview(c) for c in (cmp_k, cmp_v, slc_k, slc_v, win_k, win_v))

    row = lambda b, pt: (b, 0)
    per_b3 = lambda b, pt: (b, 0, 0)
    fix2 = lambda b, pt: (0, 0)
    sub_w = CMP_STRIDE * KV_WIDTH
    wspecs = [pl.BlockSpec((4 * CMP_HIDDEN, sub_w), fix2),
              pl.BlockSpec((KV_WIDTH, 2 * CMP_HIDDEN), fix2),
              pl.BlockSpec((SUBLANES, sub_w), fix2)]
    in_specs = ([pl.BlockSpec((nb * nr, QPAD_WIDTH), row)] * 2 + [pl.BlockSpec((nb * nr, KV_WIDTH), row)] * 4
                + [pl.BlockSpec((nb * nr, LANES), row), pl.BlockSpec((nb * nr, NSA_WIDTH), row)]
                + [pl.BlockSpec((nb, KV_WIDTH, win_buf), per_b3)] * 2
                + [pl.BlockSpec(memory_space=pl.ANY)] * 4 + wspecs + wspecs
                + [pl.BlockSpec((LANES, LANES), fix2), pl.BlockSpec((LANES, all_keys), fix2)])
    grid_spec = pltpu.PrefetchScalarGridSpec(
        num_scalar_prefetch=1,
        grid=(batch // nb,),
        in_specs=in_specs,
        out_specs=pl.BlockSpec((nb * nr, NSA_WIDTH), row),
        scratch_shapes=[pltpu.VMEM((nb, KV_WIDTH, past), BF16), pltpu.VMEM((nb, KV_WIDTH, past), BF16),
                        pltpu.VMEM((nb, KV_WIDTH, win_buf), BF16), pltpu.VMEM((nb, KV_WIDTH, win_buf), BF16),
                        pltpu.VMEM((2, nb, 4, n_pages, KV_WIDTH, page), F32),
                        pltpu.SemaphoreType.DMA((2, 4))])
    return pl.pallas_call(
        functools.partial(_nsa_sample_t_kernel, nb=nb, n_pages=n_pages, page=page, past=past, n_new=n_new),
        grid_spec=grid_spec,
        out_shape=jax.ShapeDtypeStruct((batch * nr, NSA_WIDTH), F32),
        compiler_params=pltpu.CompilerParams(dimension_semantics=("arbitrary",),
                                             vmem_limit_bytes=VMEM_LIMIT_BYTES),
        name="nsa_sample",
    )(page_table, q, qr, sk, sv, wk, wv, sm, zn, win_k, win_v, cmp_k, cmp_v, slc_k, slc_v,
      *cmp_k_w, *cmp_v_w, _overlap_t(ns), e)


def _unit_lower_inverse(a, n_valid, hi):
    c = a.shape[0]
    inv = _eye(c, F32) - a
    power = a
    span = 2
    while span < n_valid:
        power = _mm(power, power, hi=hi)
        inv = inv + _mm(inv, power, hi=hi)
        span *= 2
    return inv


def _gdn_kernel(qkv_ref, sm_ref, zg_ref, conv0_ref, s0_ref, wc_ref, vec_ref, wg_ref,
                go_ref, xp_out_ref, st_ref, xp_s, *, bblk, chunk, n_valid, hi):
    c_idx = pl.program_id(1)

    @pl.when(c_idx == 0)
    def _():
        xp_s[:, 0:SUBLANES, :] = conv0_ref[...]
        st_ref[...] = s0_ref[...]

    wc = wc_ref[...]
    vec = vec_ref[...]
    row1 = lax.broadcasted_iota(jnp.int32, (chunk, LANES), 0)
    rr = lax.broadcasted_iota(jnp.int32, (chunk, chunk), 0)
    cc = lax.broadcasted_iota(jnp.int32, (chunk, chunk), 1)
    tri = rr >= cc
    tri_bf = tri.astype(BF16)
    eye_bf = _eye(LANES, BF16)

    acts, betas, decays, decay_ts, e_decs = [], [], [], [], []
    for b in range(bblk):
        xp_s[b, SUBLANES:SUBLANES + chunk, :] = qkv_ref[b]
        y = xp_s[b, SUBLANES:SUBLANES + chunk, :] * wc[GDN_CONV - 1:GDN_CONV, :]
        for j in range(GDN_CONV - 1):
            lo = SUBLANES - (GDN_CONV - 1) + j
            y = y + xp_s[b, lo:lo + chunk, :] * wc[j:j + 1, :]
        xp_out_ref[b] = xp_s[b]
        xp_s[b, 0:SUBLANES, :] = xp_s[b, chunk:chunk + SUBLANES, :]
        act = _silu(y)
        small = sm_ref[b]
        z = small + vec[1:2, :]
        softplus = jnp.maximum(z, 0.0) + jnp.log1p(jnp.exp(-jnp.abs(z)))
        g_all = -jnp.exp(vec[0:1, :]) * softplus
        if n_valid < chunk:
            valid = row1 < n_valid
            act = act * _tile_lanes(valid.astype(F32), GDN_CONV_CH // LANES)
            g_all = jnp.where(valid, g_all, 0.0)
        acts.append(act)
        betas.append(_sigmoid(small))
        decays.append(_mm_exact_lhs(tri_bf, g_all))
    for b in range(bblk):
        decay_ts.append(_mm_exact_lhs(eye_bf, decays[b], _NT))
        e_decs.append(jnp.exp(decays[b]))

    chains = [(b, h) for b in range(bblk) for h in range(GDN_HEADS)]
    qs, ks, kbs, dmasks, rhs_u, rhs_w, qds, kds, gls = [], [], [], [], [], [], [], [], []
    for b, h in chains:
        act = acts[b]
        qh = act[:, h * GDN_DK:(h + 1) * GDN_DK]
        kh = act[:, GDN_WIDTH + h * GDN_DK:GDN_WIDTH + (h + 1) * GDN_DK]
        vh = act[:, 2 * GDN_WIDTH + h * GDN_DV:2 * GDN_WIDTH + (h + 1) * GDN_DV]
        qh = qh * lax.rsqrt(jnp.sum(qh * qh, axis=-1, keepdims=True) + NORM_EPS) * (GDN_DK ** -0.5)
        kh = kh * lax.rsqrt(jnp.sum(kh * kh, axis=-1, keepdims=True) + NORM_EPS)
        beta = betas[b][:, SM_B + h:SM_B + h + 1]
        dcol = decays[b][:, SM_A + h:SM_A + h + 1]
        drow = decay_ts[b][SM_A + h:SM_A + h + 1, :]
        ed = e_decs[b][:, SM_A + h:SM_A + h + 1]
        dlast = decays[b][chunk - 1:chunk, SM_A + h:SM_A + h + 1]
        kb = kh * beta
        qs.append(qh)
        ks.append(kh)
        kbs.append(kb)
        dmasks.append(jnp.where(tri, jnp.exp(jnp.where(tri, dcol - drow, 0.0)), 0.0))
        rhs_u.append(vh * beta)
        rhs_w.append(kb * ed)
        qds.append(qh * ed)
        kds.append(kh * jnp.exp(dlast - dcol))
        gls.append(jnp.exp(dlast))

    n = len(chains)
    eye_c = _eye(chunk, F32)
    kkts = [_mm(kbs[i], ks[i], _NT, hi=hi) for i in range(n)]
    qks = [_mm(qs[i], ks[i], _NT, hi=hi) for i in range(n)]
    powers = [jnp.where(rr > cc, kkts[i] * dmasks[i], 0.0) for i in range(n)]
    qks = [qks[i] * dmasks[i] for i in range(n)]
    invs = [eye_c - powers[i] for i in range(n)]
    span = 2
    while span < n_valid:
        powers = [_mm(powers[i], powers[i], hi=hi) for i in range(n)]
        invs = [invs[i] + _mm(invs[i], powers[i], hi=hi) for i in range(n)]
        span *= 2
    us = [_mm(invs[i], rhs_u[i], hi=hi) for i in range(n)]
    ws = [_mm(invs[i], rhs_w[i], hi=hi) for i in range(n)]
    sts = [st_ref[b, h] for b, h in chains]
    v_news = [us[i] - _mm(ws[i], sts[i], hi=hi) for i in range(n)]
    os_ = [_mm(qds[i], sts[i], hi=hi) for i in range(n)]
    os_ = [os_[i] + _mm(qks[i], v_news[i], hi=hi) for i in range(n)]
    upd = [_mm(kds[i], v_news[i], _TN, hi=hi) for i in range(n)]
    for i, (b, h) in enumerate(chains):
        st_ref[b, h] = sts[i] * gls[i] + upd[i]
        o = os_[i]
        o = o * lax.rsqrt(jnp.mean(o * o, axis=-1, keepdims=True) + NORM_EPS) * wg_ref[...]
        o = o * _silu(zg_ref[b, :, h * GDN_DV:(h + 1) * GDN_DV])
        go_ref[b, :, h * GDN_DV:(h + 1) * GDN_DV] = o.astype(go_ref.dtype)


def _gdn(qkv, sm, zg, conv0, s0, w_conv, a_log, dt_bias, w_gnorm, batch, rows, bblk, chunk, n_valid, hi,
         out_dtype):
    nc = rows // chunk
    tile = lambda b, c: (b, c, 0)
    per_b3 = lambda b, c: (b, 0, 0)
    per_b4 = lambda b, c: (b, 0, 0, 0)
    fix2 = lambda b, c: (0, 0)
    wc = jnp.concatenate([w_conv, jnp.zeros((SUBLANES - GDN_CONV, GDN_CONV_CH), w_conv.dtype)], axis=0)
    vec = jnp.zeros((SUBLANES, LANES), F32)
    vec = vec.at[0, SM_A:SM_A + GDN_HEADS].set(a_log).at[1, SM_A:SM_A + GDN_HEADS].set(dt_bias)
    go, xp, st = pl.pallas_call(
        functools.partial(_gdn_kernel, bblk=bblk, chunk=chunk, n_valid=n_valid, hi=hi),
        grid=(batch // bblk, nc),
        in_specs=[pl.BlockSpec((bblk, chunk, GDN_CONV_CH), tile),
                  pl.BlockSpec((bblk, chunk, LANES), tile),
                  pl.BlockSpec((bblk, chunk, GDN_WIDTH), tile),
                  pl.BlockSpec((bblk, SUBLANES, GDN_CONV_CH), per_b3),
                  pl.BlockSpec((bblk, GDN_HEADS, GDN_DK, GDN_DV), per_b4),
                  pl.BlockSpec((SUBLANES, GDN_CONV_CH), fix2),
                  pl.BlockSpec((SUBLANES, LANES), fix2),
                  pl.BlockSpec((1, GDN_DV), fix2)],
        out_specs=[pl.BlockSpec((bblk, chunk, GDN_WIDTH), tile),
                   pl.BlockSpec((bblk, SUBLANES + chunk, GDN_CONV_CH), per_b3),
                   pl.BlockSpec((bblk, GDN_HEADS, GDN_DK, GDN_DV), per_b4)],
        out_shape=[jax.ShapeDtypeStruct((batch, rows, GDN_WIDTH), out_dtype),
                   jax.ShapeDtypeStruct((batch, SUBLANES + chunk, GDN_CONV_CH), F32),
                   jax.ShapeDtypeStruct((batch, GDN_HEADS, GDN_DK, GDN_DV), F32)],
        scratch_shapes=[pltpu.VMEM((bblk, SUBLANES + chunk, GDN_CONV_CH), F32)],
        compiler_params=pltpu.CompilerParams(dimension_semantics=("arbitrary", "arbitrary"),
                                             vmem_limit_bytes=VMEM_LIMIT_BYTES),
        name="gdn",
    )(qkv.reshape(batch, rows, GDN_CONV_CH), sm.reshape(batch, rows, LANES), zg.reshape(batch, rows, GDN_WIDTH),
      conv0, s0, wc, vec, w_gnorm.reshape(1, GDN_DV))
    return go.reshape(batch * rows, GDN_WIDTH), xp, st


PROMPT_TM = 512
SAMPLE_TM = 256
GDN_PROMPT_BBLK = 4
GDN_SAMPLE_BBLK = 8
NSA_SAMPLE_NB = 2


def _layer_prompt(h, lw, final, w_final, win_buf):
    w_norm, w_pack, cmp_k_w, cmp_v_w, w_conv, a_log, dt_bias, w_gnorm, w_out = lw
    batch, seq, d = h.shape
    x2d = h.reshape(batch * seq, d)
    tables = _rope_tables(jnp.arange(seq, dtype=jnp.int32))
    (q, qr, ck, cv, sk, sv, wk, wv, kvb, zn, qkv, zg, sm, ck_t, cv_t, sk_t, sv_t, wk_t, wv_t) = _project(
        x2d, w_norm, w_pack.astype(BF16), tables, seq, PROMPT_TM, False, BF16, True)
    nsa = _nsa_prompt(q, qr, kvb, ck, cv, sm, zn, cmp_k_w, cmp_v_w, batch, seq)
    conv0 = jnp.zeros((batch, SUBLANES, GDN_CONV_CH), F32)
    s0 = jnp.zeros((batch, GDN_HEADS, GDN_DK, GDN_DV), F32)
    go, xp, st = _gdn(qkv, sm, zg, conv0, s0, w_conv, a_log, dt_bias, w_gnorm,
                      batch, seq, GDN_PROMPT_BBLK, GDN_CHUNK, GDN_CHUNK, False, BF16)
    y = _out_project(x2d, nsa, go, w_out.astype(BF16), w_final, PROMPT_TM, False, final)
    conv_new = xp[:, SUBLANES + GDN_CHUNK - (GDN_CONV - 1):SUBLANES + GDN_CHUNK]
    from_t = lambda t: jnp.transpose(t.reshape(batch, KV_HEADS, HEAD_DIM, t.shape[-1]), (0, 3, 1, 2))
    lead = ((0, 0), (0, 0), (max(win_buf - seq, 0), 0))
    win = lambda t: from_t(jnp.pad(t, lead)[:, :, -win_buf:])
    return y.reshape(batch, seq, d), (from_t(ck_t), from_t(cv_t), from_t(sk_t), from_t(sv_t), win(wk_t), win(wv_t),
                                      conv_new, st)


def _layer_sample(h8, n_new, caches, page_table, lw, final, w_final):
    w_norm, w_pack, cmp_k_w, cmp_v_w, w_conv, a_log, dt_bias, w_gnorm, w_out = lw
    cmp_k_t = (cmp_k_w[0].T, cmp_k_w[1].T, cmp_k_w[2])
    cmp_v_t = (cmp_v_w[0].T, cmp_v_w[1].T, cmp_v_w[2])
    c_cmp_k, c_cmp_v, c_slc_k, c_slc_v, c_win_k, c_win_v, s_conv, s_gdn = caches
    batch, nr, d = h8.shape
    past = page_table.shape[1] * c_cmp_k.shape[1]
    x2d = h8.reshape(batch * nr, d)
    tables = _rope_tables(past + jnp.arange(nr, dtype=jnp.int32))
    tables = tuple(jnp.tile(t, (SAMPLE_TM // nr, 1)) for t in tables)
    (q, qr, ck, cv, sk, sv, wk, wv, _, zn, qkv, zg, sm) = _project(
        x2d, w_norm, w_pack, tables, SAMPLE_TM, SAMPLE_TM, True, F32, False)
    nsa = _nsa_sample_t(q, qr, sk, sv, wk, wv, sm, zn, c_win_k, c_win_v, c_cmp_k, c_cmp_v, c_slc_k, c_slc_v,
                        page_table, cmp_k_t, cmp_v_t, n_new, NSA_SAMPLE_NB)
    conv0 = jnp.pad(s_conv, ((0, 0), (SUBLANES - (GDN_CONV - 1), 0), (0, 0)))
    go, xp, st = _gdn(qkv, sm, zg, conv0, s_gdn, w_conv, a_log, dt_bias, w_gnorm,
                      batch, nr, GDN_SAMPLE_BBLK, nr, n_new, True, F32)
    y = _out_project(x2d, nsa, go, w_out, w_final, SAMPLE_TM, False, final)
    kv4 = lambda t: t.reshape(batch, nr, KV_HEADS, HEAD_DIM)[:, :n_new]
    win_k = jnp.concatenate([c_win_k[:, n_new:], kv4(wk)], axis=1)
    win_v = jnp.concatenate([c_win_v[:, n_new:], kv4(wv)], axis=1)
    conv_new = xp[:, SUBLANES + n_new - (GDN_CONV - 1):SUBLANES + n_new]
    return y.reshape(batch, nr, d), (kv4(ck), kv4(cv), kv4(sk), kv4(sv), win_k, win_v, conv_new, st)


def kernel(x_prompt, x_sample, cache_cmp_k, cache_cmp_v, cache_slc_k, cache_slc_v, cache_win_k, cache_win_v, state_conv, state_gdn, page_table, w_norm, w_in, pe_cmp_k, w_cmp_k1, w_cmp_k2, pe_cmp_v, w_cmp_v1, w_cmp_v2, w_conv, a_log, dt_bias, w_gdn_norm, w_out, w_final_norm):
    depth = w_in.shape[0]
    n_new = x_sample.shape[1]
    win_buf = cache_win_k.shape[2]
    h_p = x_prompt
    h_s = jnp.pad(x_sample, ((0, 0), (0, SAMPLE_ROWS - n_new), (0, 0)))
    st_p, st_s = [], []
    for layer in range(depth):
        lw = (w_norm[layer], _pack_w_in(w_in[layer]),
              _compress_weights(pe_cmp_k[layer], w_cmp_k1[layer], w_cmp_k2[layer]),
              _compress_weights(pe_cmp_v[layer], w_cmp_v1[layer], w_cmp_v2[layer]),
              w_conv[layer], a_log[layer], dt_bias[layer], w_gdn_norm[layer], w_out[layer])
        final = layer == depth - 1
        h_p, sp = _layer_prompt(h_p, lw, final, w_final_norm, win_buf)
        caches = (cache_cmp_k[layer], cache_cmp_v[layer], cache_slc_k[layer], cache_slc_v[layer],
                  cache_win_k[layer], cache_win_v[layer], state_conv[layer], state_gdn[layer])
        h_s, ss = _layer_sample(h_s, n_new, caches, page_table, lw, final, w_final_norm)
        st_p.append(sp)
        st_s.append(ss)
    outs = [h_p, h_s[:, :n_new]]
    for i in range(8):
        outs.append(jnp.stack([s[i] for s in st_p]))
        outs.append(jnp.stack([s[i] for s in st_s]))
    return tuple(outs)
```

```python
import functools

import numpy as np
import jax
import jax.numpy as jnp
from jax import lax
from jax.experimental import pallas as pl
from jax.experimental.pallas import tpu as pltpu

F32 = jnp.float32
BF16 = jnp.bfloat16

D_MODEL = 1024
HEAD_DIM = 64
Q_HEADS = 8
KV_HEADS = 2
GROUP = Q_HEADS // KV_HEADS
NSA_WIDTH = Q_HEADS * HEAD_DIM
KV_WIDTH = KV_HEADS * HEAD_DIM
CMP_LEN = 32
CMP_STRIDE = 16
CMP_HIDDEN = 128
SEL_BLOCK = 64
TOP_N = 8
WINDOW = 512
FORCE_BONUS = 1.0e4
ROT_DIM = HEAD_DIM // 4
ROPE_THETA = 500000.0
GDN_DK = 128
GDN_DV = 128
GDN_HEADS = 4
GDN_WIDTH = GDN_HEADS * GDN_DV
GDN_CONV = 4
GDN_CONV_CH = 3 * GDN_WIDTH
GDN_CHUNK = 64
NORM_EPS = 1e-6
MASK_VALUE = -1e30

LANES = 128
SUBLANES = 8
VMEM_LIMIT_BYTES = 56 * 1024 * 1024

QPAD_WIDTH = Q_HEADS * LANES
C_Q = 0
C_KV = C_Q + QPAD_WIDTH
C_ZN = C_KV + 6 * KV_WIDTH
C_QKV = C_ZN + NSA_WIDTH
C_ZG = C_QKV + GDN_CONV_CH
C_SM = C_ZG + GDN_WIDTH
N_PACK = C_SM + LANES
SM_B = 3 * Q_HEADS
SM_A = SM_B + GDN_HEADS

TQ = 128
CK_SLC = 256
CK_WIN = 128
N_SUB = 128
N_CMP = N_SUB - CMP_LEN // CMP_STRIDE + 1
SAMPLE_ROWS = 8
KVB_SK = 0
KVB_SV = KVB_SK + KV_WIDTH
KVB_WK = KVB_SV + 2 * KV_WIDTH
KVB_WV = KVB_WK + KV_WIDTH
KVB_WIDTH = KVB_WV + 2 * KV_WIDTH
KVB_COLS = (KVB_SK, KVB_SV, KVB_WK, KVB_WV)
WIN_SPAN = WINDOW + TQ


def _pack_w_in(w_in):
    o_gate = NSA_WIDTH + 6 * KV_WIDTH
    o_zn = o_gate + 3 * Q_HEADS
    o_qkv = o_zn + NSA_WIDTH
    o_b = o_qkv + GDN_CONV_CH
    o_a = o_b + GDN_HEADS
    o_zg = o_a + GDN_HEADS
    d = w_in.shape[0]
    z64 = jnp.zeros((d, HEAD_DIM), w_in.dtype)
    qcols = []
    for hq in range(Q_HEADS):
        wq = w_in[:, hq * HEAD_DIM:(hq + 1) * HEAD_DIM]
        qcols += [wq, z64] if hq // GROUP == 0 else [z64, wq]
    pad = jnp.zeros((d, LANES - SM_A - GDN_HEADS), w_in.dtype)
    return jnp.concatenate(
        qcols + [w_in[:, NSA_WIDTH:o_gate], w_in[:, o_zn:o_qkv], w_in[:, o_qkv:o_b], w_in[:, o_zg:],
                 w_in[:, o_gate:o_zn], w_in[:, o_b:o_a], w_in[:, o_a:o_zg], pad], axis=1)


def _rope_tables(pos):
    half = ROT_DIM // 2
    inv = ROPE_THETA ** (-(jnp.arange(half, dtype=F32) * 2.0 / ROT_DIM))
    ang = pos.astype(F32)[:, None] * inv[None, :]
    cos, sin = jnp.cos(ang), jnp.sin(ang)
    n = pos.shape[0]
    one = jnp.ones((n, HEAD_DIM - ROT_DIM), F32)
    zero = jnp.zeros((n, HEAD_DIM - ROT_DIM), F32)
    zh = jnp.zeros((n, half), F32)
    c64 = jnp.concatenate([cos, cos, one], axis=1)
    a64 = jnp.concatenate([zh, sin, zero], axis=1)
    b64 = jnp.concatenate([-sin, zh, zero], axis=1)
    tile = lambda t: jnp.concatenate([t, t], axis=1)
    return tile(c64), tile(a64), tile(b64)


def _rope128(x, c, a, b):
    half = ROT_DIM // 2
    return x * c + pltpu.roll(x, half, 1) * a + pltpu.roll(x, LANES - half, 1) * b


_NN = (((1,), (0,)), ((), ()))
_NT = (((1,), (1,)), ((), ()))
_TN = (((0,), (0,)), ((), ()))


def _split_bf16(x):
    hi = x.astype(BF16)
    return hi, (x - hi.astype(F32)).astype(BF16)


def _mm(a, b, dims=_NN, hi=False):
    dot = lambda x, y: lax.dot_general(x, y, dims, preferred_element_type=F32)
    if hi:
        a_hi, a_lo = _split_bf16(a.astype(F32))
        b_hi, b_lo = _split_bf16(b.astype(F32))
        return dot(a_hi, b_hi) + (dot(a_lo, b_hi) + dot(a_hi, b_lo))
    return dot(a.astype(BF16), b.astype(BF16))


def _mm_exact_lhs(a_bf16, x, dims=_NN):
    x1 = x.astype(BF16)
    r1 = x - x1.astype(F32)
    x2 = r1.astype(BF16)
    x3 = (r1 - x2.astype(F32)).astype(BF16)
    dot = lambda t: lax.dot_general(a_bf16, t, dims, preferred_element_type=F32)
    return dot(x1) + dot(x2) + dot(x3)


def _sigmoid(x):
    return 1.0 / (1.0 + jnp.exp(-x))


def _silu(x):
    return x * _sigmoid(x)


def _tile_rows(x, n):
    return jnp.concatenate([x] * n, axis=0)


def _tile_lanes(x, n):
    return x if n == 1 else jnp.concatenate([x] * n, axis=1)


def _eye(n, dtype):
    r = lax.broadcasted_iota(jnp.int32, (n, n), 0)
    c = lax.broadcasted_iota(jnp.int32, (n, n), 1)
    return (r == c).astype(dtype)


def _proj_kernel(x_ref, wn_ref, w_ref, c_ref, a_ref, b_ref,
                 q_ref, qr_ref, ck_ref, cv_ref, sk_ref, sv_ref, wk_ref, wv_ref, kvb_ref,
                 zn_ref, qkv_ref, zg_ref, sm_ref, *t_refs, hi_gdn):
    x = x_ref[...]
    ms = jnp.mean(x * x, axis=-1, keepdims=True)
    xn = x * lax.rsqrt(ms + NORM_EPS) * wn_ref[...]
    xb = xn.astype(BF16)
    c, a, b = c_ref[...], a_ref[...], b_ref[...]
    scale = HEAD_DIM ** -0.5
    for j in range(0, Q_HEADS, 2):
        q2 = _mm(xb, w_ref[:, C_Q + j * LANES:C_Q + (j + 2) * LANES])
        for jj in range(2):
            qj = q2[:, jj * LANES:(jj + 1) * LANES]
            lo = (j + jj) * LANES
            q_ref[:, lo:lo + LANES] = (qj * scale).astype(q_ref.dtype)
            qr_ref[:, lo:lo + LANES] = (_rope128(qj, c, a, b) * scale).astype(qr_ref.dtype)
    kv_refs = (ck_ref, cv_ref, sk_ref, sv_ref, wk_ref, wv_ref)
    kv2 = [_mm(xb, w_ref[:, C_KV + j * LANES:C_KV + (j + 2) * LANES]) for j in range(0, 6, 2)]
    for j in range(6):
        kj = kv2[j // 2][:, (j % 2) * LANES:(j % 2 + 1) * LANES]
        if j in (2, 4):
            kj = _rope128(kj, c, a, b)
        kv_refs[j][...] = kj
        if j >= 2:
            lo = KVB_COLS[j - 2]
            kvb_ref[:, lo:lo + LANES] = kj.astype(BF16)
        if t_refs:
            t_refs[j][0] = kj.T
    ones = jnp.ones((x.shape[0], LANES), BF16)
    kvb_ref[:, KVB_SV + LANES:KVB_SV + 2 * LANES] = ones
    kvb_ref[:, KVB_WV + LANES:KVB_WV + 2 * LANES] = ones
    zn_ref[...] = _mm(xb, w_ref[:, C_ZN:C_QKV])
    xg = xn if hi_gdn else xb
    for j in range(3):
        lo = C_QKV + j * GDN_WIDTH
        qkv_ref[:, j * GDN_WIDTH:(j + 1) * GDN_WIDTH] = _mm(xg, w_ref[:, lo:lo + GDN_WIDTH], hi=hi_gdn)
    zg_ref[...] = _mm(xb, w_ref[:, C_ZG:C_SM])
    sm_ref[...] = _mm(xg, w_ref[:, C_SM:N_PACK], hi=hi_gdn)


def _project(x2d, w_norm, w_pack, tables, rows_per_seq, tm, hi_gdn, q_dtype, emit_t):
    n = x2d.shape[0]
    nt = rows_per_seq // tm
    row = lambda i: (i, 0)
    tab = lambda i: (i % nt, 0)
    fix = lambda i: (0, 0)
    widths = (QPAD_WIDTH, QPAD_WIDTH) + (KV_WIDTH,) * 6 + (KVB_WIDTH, NSA_WIDTH, GDN_CONV_CH, GDN_WIDTH, LANES)
    dtypes = (q_dtype, q_dtype) + (F32,) * 6 + (BF16, F32, F32, F32, F32)
    out_specs = [pl.BlockSpec((tm, w), row) for w in widths]
    out_shape = [jax.ShapeDtypeStruct((n, w), d) for w, d in zip(widths, dtypes)]
    if emit_t:
        out_specs += [pl.BlockSpec((1, KV_WIDTH, tm), lambda i: (i // nt, 0, i % nt))] * 6
        out_shape += [jax.ShapeDtypeStruct((n // rows_per_seq, KV_WIDTH, rows_per_seq), F32)] * 6
    return pl.pallas_call(
        functools.partial(_proj_kernel, hi_gdn=hi_gdn),
        grid=(n // tm,),
        in_specs=[pl.BlockSpec((tm, D_MODEL), row),
                  pl.BlockSpec((1, D_MODEL), fix),
                  pl.BlockSpec((D_MODEL, N_PACK), fix),
                  pl.BlockSpec((tm, LANES), tab),
                  pl.BlockSpec((tm, LANES), tab),
                  pl.BlockSpec((tm, LANES), tab)],
        out_specs=out_specs,
        out_shape=out_shape,
        compiler_params=pltpu.CompilerParams(dimension_semantics=("arbitrary",),
                                             vmem_limit_bytes=VMEM_LIMIT_BYTES),
        name="in_proj",
    )(x2d, w_norm.reshape(1, D_MODEL), w_pack, *tables)


def _out_kernel(x_ref, nsa_ref, gdn_ref, w_ref, wf_ref, y_ref, *, hi, final):
    acc = (_mm(nsa_ref[...], w_ref[:NSA_WIDTH, :], hi=hi)
           + _mm(gdn_ref[...], w_ref[NSA_WIDTH:, :], hi=hi))
    h = x_ref[...] + acc
    if final:
        ms = jnp.mean(h * h, axis=-1, keepdims=True)
        h = h * lax.rsqrt(ms + NORM_EPS) * wf_ref[...]
    y_ref[...] = h


def _out_project(x2d, nsa, gdn, w_out, w_final, tm, hi, final):
    n = x2d.shape[0]
    row = lambda i: (i, 0)
    fix = lambda i: (0, 0)
    return pl.pallas_call(
        functools.partial(_out_kernel, hi=hi, final=final),
        grid=(n // tm,),
        in_specs=[pl.BlockSpec((tm, D_MODEL), row),
                  pl.BlockSpec((tm, NSA_WIDTH), row),
                  pl.BlockSpec((tm, GDN_WIDTH), row),
                  pl.BlockSpec((D_MODEL, D_MODEL), fix),
                  pl.BlockSpec((1, D_MODEL), fix)],
        out_specs=pl.BlockSpec((tm, D_MODEL), row),
        out_shape=jax.ShapeDtypeStruct((n, D_MODEL), F32),
        compiler_params=pltpu.CompilerParams(dimension_semantics=("arbitrary",),
                                             vmem_limit_bytes=VMEM_LIMIT_BYTES),
        name="out_proj",
    )(x2d, nsa, gdn, w_out, w_final.reshape(1, D_MODEL))


def _compress_weights(pe, w1, w2):
    half = CMP_STRIDE * HEAD_DIM
    z = jnp.zeros((CMP_STRIDE, HEAD_DIM, CMP_HIDDEN), w1.dtype)

    def place(wpart, h):
        wp = wpart.reshape(CMP_STRIDE, HEAD_DIM, CMP_HIDDEN)
        parts = [wp, z] if h == 0 else [z, wp]
        return jnp.stack(parts, axis=1).reshape(CMP_STRIDE * KV_WIDTH, CMP_HIDDEN)

    w1big = jnp.concatenate([place(w1[:half], 0), place(w1[:half], 1),
                             place(w1[half:], 0), place(w1[half:], 1)], axis=1)
    zz = jnp.zeros_like(w2)
    w2big = jnp.concatenate([jnp.concatenate([w2, zz], axis=1), jnp.concatenate([zz, w2], axis=1)], axis=0)
    pe_a = jnp.tile(pe[:CMP_STRIDE], (1, KV_HEADS)).reshape(1, CMP_STRIDE * KV_WIDTH)
    pe_b = jnp.tile(pe[CMP_STRIDE:], (1, KV_HEADS)).reshape(1, CMP_STRIDE * KV_WIDTH)
    pe2 = jnp.concatenate([pe_a, pe_b, jnp.zeros((SUBLANES - 2, CMP_STRIDE * KV_WIDTH), pe.dtype)], axis=0)
    return w1big.astype(BF16), w2big.astype(BF16), pe2


def _compress(sub_rows, w1_ref, w2_ref, pe_ref):
    hid = 2 * CMP_HIDDEN
    w1 = w1_ref[...]
    ab = _mm(sub_rows, w1)
    pe = pe_ref[...]
    pe_hi = pe.astype(BF16)
    pe_lo = (pe - pe_hi.astype(F32)).astype(BF16)
    r = _mm(pe_hi, w1) + _mm(pe_lo, w1)
    bias = r[0:1, :hid] + r[1:2, hid:]
    h = ab[:, :hid] + pltpu.roll(ab[:, hid:], N_SUB - 1, 0) + bias
    return _mm(_silu(h), w2_ref[...])


def _overlap_t(ns):
    c0 = np.arange(N_CMP)[None, :] * CMP_STRIDE
    b0 = np.arange(ns)[:, None] * SEL_BLOCK
    ov = np.minimum(c0 + CMP_LEN, b0 + SEL_BLOCK) - np.maximum(c0, b0)
    out = np.zeros((LANES, LANES), np.float32)
    out[:ns, :N_CMP] = np.maximum(ov, 0) / CMP_LEN
    return jnp.asarray(out, BF16)


def _cmp_attention(q_stack, ck, cv, cvalid, groups):
    s = _mm(q_stack, ck, _NT)
    cm = _tile_rows(cvalid, groups)
    s = jnp.where(cm, s, MASK_VALUE)
    m = jnp.max(s, axis=1, keepdims=True)
    e = jnp.exp(s - m)
    p = e / jnp.sum(e, axis=1, keepdims=True) * cm.astype(F32)
    return _mm(p, cv), p


def _select_blocks(imp_t, n_idx, q_blk, n_rows):
    forced = (n_idx == 0) | (n_idx == q_blk) | (n_idx == q_blk - 1)
    allowed = n_idx <= q_blk
    v = jnp.where(allowed, imp_t + FORCE_BONUS * forced.astype(F32), MASK_VALUE)
    rank = jnp.zeros(v.shape, F32)
    for j in range(n_rows):
        vj = v[j:j + 1, :]
        ge = jnp.where(vj >= v, 1.0, 0.0)
        gt = jnp.where(vj > v, 1.0, 0.0)
        rank = rank + jnp.where(n_idx > j, ge, gt)
    return ((rank < TOP_N) & allowed).astype(F32)


def _flash_init(m_s, l_s, acc_s):
    m_s[...] = jnp.full(m_s.shape, MASK_VALUE, F32)
    l_s[...] = jnp.zeros(l_s.shape, F32)
    acc_s[...] = jnp.zeros(acc_s.shape, F32)


def _flash_step(s, v_c, m_s, l_s, acc_s):
    m_prev = m_s[...]
    m_next = jnp.maximum(m_prev, jnp.max(s, axis=1, keepdims=True))
    alpha = jnp.exp(m_prev - m_next)
    p = jnp.exp(s - _tile_lanes(m_next, s.shape[1] // LANES))
    l_s[...] = alpha * l_s[...] + jnp.sum(p, axis=1, keepdims=True)
    acc_s[...] = acc_s[...] * alpha + _mm(p, v_c)
    m_s[...] = m_next


def _merge_heads(o_sum, kvh, tq):
    lane = lax.broadcasted_iota(jnp.int32, (tq, LANES), 1)
    slabs = []
    for pair in range(GROUP // 2):
        halves = []
        for par in range(2):
            o = o_sum[(2 * pair + par) * tq:(2 * pair + par + 1) * tq]
            halves.append(o if par == kvh else pltpu.roll(o, HEAD_DIM, 1))
        slabs.append(jnp.where(lane < HEAD_DIM, halves[0], halves[1]))
    return slabs


def _nsa_prompt_kernel(q_ref, qr_ref, kvb_ref, ckr_ref, cvr_ref, sm_ref, zn_ref,
                       wk1_ref, wk2_ref, pek_ref, wv1_ref, wv2_ref, pev_ref, ovt_ref, e_ref,
                       o_ref, ck_s, cv_s, s_s, mrun_s, m_s, acc_s):
    i = pl.program_id(1)

    @pl.when(i == 0)
    def _():
        ck_s[...] = _compress(ckr_ref[0], wk1_ref, wk2_ref, pek_ref).astype(BF16)
        cv_s[...] = _compress(cvr_ref[0], wv1_ref, wv2_ref, pev_ref).astype(BF16)

    kvs = range(KV_HEADS)
    t0 = i * TQ
    rows = GROUP * TQ
    row = lax.broadcasted_iota(jnp.int32, (TQ, LANES), 0)
    col = lax.broadcasted_iota(jnp.int32, (TQ, LANES), 1)
    cvalid = (CMP_STRIDE * col + (CMP_LEN - 1) <= t0 + row) & (col < N_CMP)
    cm = _tile_rows(cvalid, GROUP)
    cm_f = cm.astype(F32)
    sig = _sigmoid(sm_ref[...])

    def gate(branch, kvh):
        cols = [sig[:, branch * Q_HEADS + h:branch * Q_HEADS + h + 1] for h in range(kvh * GROUP, (kvh + 1) * GROUP)]
        return jnp.concatenate([jnp.broadcast_to(c, (TQ, LANES)) for c in cols], axis=0)

    def stack(ref, kvh):
        return jnp.concatenate([ref[:, h * LANES:(h + 1) * LANES] for h in range(kvh * GROUP, (kvh + 1) * GROUP)], axis=0)

    q_raw = [stack(q_ref, k) for k in kvs]
    q_rot = [stack(qr_ref, k) for k in kvs]

    ck, cv = ck_s[...], cv_s[...]
    s = [jnp.where(cm, _mm(q_raw[k], ck, _NT), MASK_VALUE) for k in kvs]
    e = [jnp.exp(s[k] - jnp.max(s[k], axis=1, keepdims=True)) for k in kvs]
    p = [e[k] / jnp.sum(e[k], axis=1, keepdims=True) * cm_f for k in kvs]
    o_sum = [gate(0, k) * _mm(p[k], cv) for k in kvs]
    p_sum = jnp.concatenate([p[k][0:TQ] + p[k][TQ:2 * TQ] + p[k][2 * TQ:3 * TQ] + p[k][3 * TQ:4 * TQ] for k in kvs],
                            axis=0)
    ns_rows = 32
    nq = KV_HEADS * TQ
    imp_t = _mm_exact_lhs(ovt_ref[...], p_sum, _NT)[0:ns_rows]
    n_idx = lax.broadcasted_iota(jnp.int32, (ns_rows, nq), 0)
    qb_t = (t0 + lax.broadcasted_iota(jnp.int32, (ns_rows, nq), 1) % TQ) // SEL_BLOCK
    sel_t = _select_blocks(imp_t, n_idx, qb_t, ns_rows)
    sel_t = jnp.concatenate([sel_t, jnp.zeros((LANES - ns_rows, nq), F32)], axis=0).astype(BF16)
    eye = _eye(TQ, BF16)
    sel = [_mm(eye, sel_t[:, k * TQ:(k + 1) * TQ], _NT).astype(BF16) for k in kvs]

    col_s = lax.broadcasted_iota(jnp.int32, (TQ, CK_SLC), 1)
    row_s = lax.broadcasted_iota(jnp.int32, (TQ, CK_SLC), 0)
    n_chunks = (t0 + TQ + CK_SLC - 1) // CK_SLC
    mrun_s[...] = jnp.full(mrun_s.shape, MASK_VALUE, F32)

    def score_chunks(js):
        k0 = [pl.multiple_of(j * CK_SLC, CK_SLC) for j in js]
        k_c = [kvb_ref[pl.ds(k0[n], CK_SLC), KVB_SK:KVB_SK + LANES] for n in range(len(js))]
        sc = [[_mm(q_rot[k], k_c[n], _NT) for k in kvs] for n in range(len(js))]
        picked = [[_mm(sel[k], e_ref[js[n]]) for k in kvs] for n in range(len(js))]
        for k in kvs:
            run = mrun_s[k]
            for n, j in enumerate(js):
                causal = k0[n] + col_s <= t0 + row_s
                bias = jnp.where((picked[n][k] > 0.5) & causal, 0.0, MASK_VALUE)
                sk = sc[n][k] + _tile_rows(bias, GROUP)
                s_s[k, j] = sk
                run = jnp.maximum(run, jnp.maximum(sk[:, :LANES], sk[:, LANES:]))
            mrun_s[k] = run

    def value_chunks(js):
        k0 = [pl.multiple_of(j * CK_SLC, CK_SLC) for j in js]
        v_c = [kvb_ref[pl.ds(k0[n], CK_SLC), KVB_SV:KVB_SV + 2 * LANES] for n in range(len(js))]
        pr = [[jnp.exp(s_s[k, js[n]] - _tile_lanes(m_s[k], CK_SLC // LANES)) for k in kvs] for n in range(len(js))]
        pv = [[_mm(pr[n][k], v_c[n]) for k in kvs] for n in range(len(js))]
        for k in kvs:
            acc_s[k] += sum(pv[n][k] for n in range(len(js)))

    def paired(fn):
        def pair_body(jj, carry):
            fn([2 * jj, 2 * jj + 1])
            return carry

        lax.fori_loop(0, n_chunks // 2, pair_body, 0)

        @pl.when(n_chunks % 2 == 1)
        def _():
            fn([n_chunks - 1])

    paired(score_chunks)
    for k in kvs:
        m_s[k] = jnp.broadcast_to(jnp.max(mrun_s[k], axis=1, keepdims=True), (rows, LANES))
    acc_s[...] = jnp.zeros(acc_s.shape, F32)

    paired(value_chunks)
    for k in kvs:
        acc = acc_s[k]
        o_sum[k] = o_sum[k] + gate(1, k) * (acc[:, :LANES] / acc[:, LANES:])

    ks0 = pl.multiple_of(jnp.maximum(t0 - WINDOW, 0), TQ)
    k_w = kvb_ref[pl.ds(ks0, WIN_SPAN), KVB_WK:KVB_WK + LANES]
    v_w = kvb_ref[pl.ds(ks0, WIN_SPAN), KVB_WV:KVB_WV + 2 * LANES]
    diff = (t0 + lax.broadcasted_iota(jnp.int32, (TQ, WIN_SPAN), 0)) - (ks0 + lax.broadcasted_iota(jnp.int32, (TQ, WIN_SPAN), 1))
    bias_w = _tile_rows(jnp.where((diff >= 0) & (diff <= WINDOW), 0.0, MASK_VALUE), GROUP)
    sw = [_mm(q_rot[k], k_w, _NT) + bias_w for k in kvs]
    pw = [jnp.exp(sw[k] - jnp.max(sw[k], axis=1, keepdims=True)) for k in kvs]
    rw = [_mm(pw[k], v_w) for k in kvs]
    for k in kvs:
        o_all = o_sum[k] + gate(2, k) * (rw[k][:, :LANES] / rw[k][:, LANES:])
        for pair, slab in enumerate(_merge_heads(o_all, k, TQ)):
            lo = (k * (GROUP // 2) + pair) * LANES
            o_ref[:, lo:lo + LANES] = (slab * _silu(zn_ref[:, lo:lo + LANES])).astype(o_ref.dtype)


def _nsa_prompt(q, qr, kvb, ck, cv, sm, zn, cmp_k_w, cmp_v_w, batch, seq):
    nt = seq // TQ
    ns = seq // SEL_BLOCK
    sub_w = CMP_STRIDE * KV_WIDTH
    keys = np.arange(seq)
    e = (np.arange(LANES)[None, :, None] == (keys // SEL_BLOCK).reshape(seq // CK_SLC, 1, CK_SLC))
    e = jnp.asarray(e, BF16)
    tile = lambda b, i: (b * nt + i, 0)
    per_b = lambda b, i: (b, 0)
    per_b3 = lambda b, i: (b, 0, 0)
    fix2 = lambda b, i: (0, 0)
    fix3 = lambda b, i: (0, 0, 0)
    wspecs = [pl.BlockSpec((sub_w, 4 * CMP_HIDDEN), fix2),
              pl.BlockSpec((2 * CMP_HIDDEN, KV_WIDTH), fix2),
              pl.BlockSpec((SUBLANES, sub_w), fix2)]
    rows = GROUP * TQ
    return pl.pallas_call(
        _nsa_prompt_kernel,
        grid=(batch, nt),
        in_specs=[pl.BlockSpec((TQ, QPAD_WIDTH), tile),
                  pl.BlockSpec((TQ, QPAD_WIDTH), tile),
                  pl.BlockSpec((seq, KVB_WIDTH), per_b),
                  pl.BlockSpec((1, seq // CMP_STRIDE, sub_w), per_b3),
                  pl.BlockSpec((1, seq // CMP_STRIDE, sub_w), per_b3),
                  pl.BlockSpec((TQ, LANES), tile),
                  pl.BlockSpec((TQ, NSA_WIDTH), tile)] + wspecs + wspecs + [
                  pl.BlockSpec((LANES, LANES), fix2),
                  pl.BlockSpec((seq // CK_SLC, LANES, CK_SLC), fix3)],
        out_specs=pl.BlockSpec((TQ, NSA_WIDTH), tile),
        out_shape=jax.ShapeDtypeStruct((batch * seq, NSA_WIDTH), BF16),
        scratch_shapes=[pltpu.VMEM((N_SUB, KV_WIDTH), BF16), pltpu.VMEM((N_SUB, KV_WIDTH), BF16),
                        pltpu.VMEM((KV_HEADS, seq // CK_SLC, rows, CK_SLC), F32),
                        pltpu.VMEM((KV_HEADS, rows, LANES), F32), pltpu.VMEM((KV_HEADS, rows, LANES), F32),
                        pltpu.VMEM((KV_HEADS, rows, 2 * LANES), F32)],
        compiler_params=pltpu.CompilerParams(dimension_semantics=("arbitrary", "arbitrary"),
                                             vmem_limit_bytes=VMEM_LIMIT_BYTES),
        name="nsa_prompt",
    )(q, qr, kvb, ck.reshape(batch, seq // CMP_STRIDE, sub_w), cv.reshape(batch, seq // CMP_STRIDE, sub_w),
      sm, zn, *cmp_k_w, *cmp_v_w, _overlap_t(ns), e)


def _softmax_rows(s):
    m = jnp.max(s, axis=1, keepdims=True)
    e = jnp.exp(s - m)
    return e / jnp.sum(e, axis=1, keepdims=True)


def _nsa_sample_kernel(pt_ref, q_ref, qr_ref, skn_ref, svn_ref, wkn_ref, wvn_ref, sm_ref, zn_ref,
                       wkc_ref, wvc_ref, *rest, n_pages, page, past, n_new, win_keys):
    del pt_ref
    cmpk, cmpv = rest[0:n_pages], rest[n_pages:2 * n_pages]
    slck, slcv = rest[2 * n_pages:3 * n_pages], rest[3 * n_pages:4 * n_pages]
    wk1_ref, wk2_ref, pek_ref, wv1_ref, wv2_ref, pev_ref, ovt_ref, e_ref = rest[4 * n_pages:4 * n_pages + 8]
    o_ref, wko_ref, wvo_ref = rest[4 * n_pages + 8:4 * n_pages + 11]
    kall, vall, kwin, vwin = rest[4 * n_pages + 11:]
    nr = SAMPLE_ROWS
    rows = Q_HEADS * nr
    all_keys = kall.shape[0]
    win_buf = wkc_ref.shape[1]
    zeros8 = jnp.zeros((nr, LANES), F32)

    for p in range(n_pages):
        kall[p * page:(p + 1) * page, :] = slck[p][0].astype(BF16)
        vall[p * page:(p + 1) * page, :] = slcv[p][0].astype(BF16)
    kall[past:past + 2 * nr, :] = jnp.concatenate([skn_ref[...], zeros8], axis=0).astype(BF16)
    vall[past:past + 2 * nr, :] = jnp.concatenate([svn_ref[...], zeros8], axis=0).astype(BF16)
    kall[past + 2 * nr:, :] = jnp.zeros((all_keys - past - 2 * nr, LANES), BF16)
    vall[past + 2 * nr:, :] = jnp.zeros((all_keys - past - 2 * nr, LANES), BF16)
    kwin[0:win_buf, :] = wkc_ref[0].astype(BF16)
    vwin[0:win_buf, :] = wvc_ref[0].astype(BF16)
    kwin[win_buf:win_buf + 2 * nr, :] = jnp.concatenate([wkn_ref[...], zeros8], axis=0).astype(BF16)
    vwin[win_buf:win_buf + 2 * nr, :] = jnp.concatenate([wvn_ref[...], zeros8], axis=0).astype(BF16)
    kwin[win_buf + 2 * nr:, :] = jnp.zeros((win_keys - win_buf - 2 * nr, LANES), BF16)
    vwin[win_buf + 2 * nr:, :] = jnp.zeros((win_keys - win_buf - 2 * nr, LANES), BF16)

    ck = _compress(jnp.concatenate([r[0] for r in cmpk], axis=0), wk1_ref, wk2_ref, pek_ref)
    cv = _compress(jnp.concatenate([r[0] for r in cmpv], axis=0), wv1_ref, wv2_ref, pev_ref)

    q_raw = jnp.concatenate([q_ref[:, h * LANES:(h + 1) * LANES] for h in range(Q_HEADS)], axis=0)
    q_rot = jnp.concatenate([qr_ref[:, h * LANES:(h + 1) * LANES] for h in range(Q_HEADS)], axis=0)

    tok = lax.broadcasted_iota(jnp.int32, (nr, LANES), 0)
    col = lax.broadcasted_iota(jnp.int32, (nr, LANES), 1)
    cvalid = (CMP_STRIDE * col + (CMP_LEN - 1) <= past + tok) & (col < N_CMP)
    o_cmp, p = _cmp_attention(q_raw, ck, cv, cvalid, Q_HEADS)

    p_sum = []
    for kvh in range(KV_HEADS):
        base = kvh * GROUP * nr
        p_sum.append(sum(p[base + g * nr:base + (g + 1) * nr] for g in range(GROUP)))
    p_sum = jnp.concatenate(p_sum, axis=0)
    ns = (past + n_new + SEL_BLOCK - 1) // SEL_BLOCK
    ns_rows = -(-ns // SUBLANES) * SUBLANES
    nq = KV_HEADS * nr
    imp_t = _mm_exact_lhs(ovt_ref[...], p_sum, _NT)[0:ns_rows]
    n_idx = lax.broadcasted_iota(jnp.int32, (ns_rows, nq), 0)
    qb_t = (past + lax.broadcasted_iota(jnp.int32, (ns_rows, nq), 1) % nr) // SEL_BLOCK
    sel_t = _select_blocks(imp_t, n_idx, qb_t, ns)
    sel_t = jnp.concatenate([sel_t, jnp.zeros((LANES - ns_rows, nq), F32)], axis=0)
    sel = _mm(_eye(nq, BF16), sel_t, _NT)
    sel_rows = jnp.concatenate([_tile_rows(sel[kvh * nr:(kvh + 1) * nr], GROUP) for kvh in range(KV_HEADS)], axis=0)

    picked = _mm(sel_rows, e_ref[...])
    q_pos = past + lax.broadcasted_iota(jnp.int32, (rows, all_keys), 0) % nr
    k_pos = lax.broadcasted_iota(jnp.int32, (rows, all_keys), 1)
    ok = (picked > 0.5) & (k_pos <= q_pos)
    s = _mm(q_rot, kall[...], _NT) + jnp.where(ok, 0.0, MASK_VALUE)
    o_slc = _mm(_softmax_rows(s), vall[...])

    q_pos = past + lax.broadcasted_iota(jnp.int32, (rows, win_keys), 0) % nr
    k_pos = past - win_buf + lax.broadcasted_iota(jnp.int32, (rows, win_keys), 1)
    diff = q_pos - k_pos
    ok = (diff >= 0) & (diff <= WINDOW)
    s = _mm(q_rot, kwin[...], _NT) + jnp.where(ok, 0.0, MASK_VALUE)
    o_win = _mm(_softmax_rows(s), vwin[...])

    sig = _sigmoid(sm_ref[...])

    def gate(branch):
        cols = [sig[:, branch * Q_HEADS + h:branch * Q_HEADS + h + 1] for h in range(Q_HEADS)]
        return jnp.concatenate([jnp.broadcast_to(c, (nr, LANES)) for c in cols], axis=0)

    o_all = gate(0) * o_cmp + gate(1) * o_slc + gate(2) * o_win
    for kvh in range(KV_HEADS):
        o_kv = o_all[kvh * GROUP * nr:(kvh + 1) * GROUP * nr]
        for pair, slab in enumerate(_merge_heads(o_kv, kvh, nr)):
            lo = (kvh * (GROUP // 2) + pair) * LANES
            o_ref[:, lo:lo + LANES] = (slab * _silu(zn_ref[:, lo:lo + LANES])).astype(o_ref.dtype)

    for cache_ref, new_ref, out_ref in ((wkc_ref, wkn_ref, wko_ref), (wvc_ref, wvn_ref, wvo_ref)):
        shifted = pltpu.roll(cache_ref[0], win_buf - n_new, 0)
        out_ref[0] = shifted
        tail = jnp.where(tok < nr - n_new, shifted[win_buf - nr:], pltpu.roll(new_ref[...], nr - n_new, 0))
        out_ref[0, win_buf - nr:, :] = tail


def _nsa_sample(q, qr, sk, sv, wk, wv, sm, zn, win_k, win_v, cmp_k, cmp_v, slc_k, slc_v, page_table,
                cmp_k_w, cmp_v_w, n_new):
    batch, n_pages = page_table.shape
    n_pool, page = cmp_k.shape[0], cmp_k.shape[1]
    past = n_pages * page
    nr = SAMPLE_ROWS
    sub_w = CMP_STRIDE * KV_WIDTH
    win_buf = win_k.shape[1]
    all_keys = -(-(past + 2 * nr) // LANES) * LANES
    win_keys = -(-(win_buf + 2 * nr) // LANES) * LANES
    ns = (past + n_new + SEL_BLOCK - 1) // SEL_BLOCK
    e = jnp.asarray(np.arange(LANES)[:, None] == (np.arange(all_keys) // SEL_BLOCK)[None, :], BF16)
    cmp_k = cmp_k.reshape(n_pool, page // CMP_STRIDE, sub_w)
    cmp_v = cmp_v.reshape(n_pool, page // CMP_STRIDE, sub_w)
    slc_k = slc_k.reshape(n_pool, page, KV_WIDTH)
    slc_v = slc_v.reshape(n_pool, page, KV_WIDTH)
    win_k = win_k.reshape(batch, win_buf, KV_WIDTH)
    win_v = win_v.reshape(batch, win_buf, KV_WIDTH)

    row = lambda b, pt: (b, 0)
    per_b3 = lambda b, pt: (b, 0, 0)
    fix2 = lambda b, pt: (0, 0)
    page_map = lambda p: (lambda b, pt: (pt[b, p], 0, 0))
    wspecs = [pl.BlockSpec((sub_w, 4 * CMP_HIDDEN), fix2),
              pl.BlockSpec((2 * CMP_HIDDEN, KV_WIDTH), fix2),
              pl.BlockSpec((SUBLANES, sub_w), fix2)]
    in_specs = ([pl.BlockSpec((nr, QPAD_WIDTH), row)] * 2 + [pl.BlockSpec((nr, KV_WIDTH), row)] * 4
                + [pl.BlockSpec((nr, LANES), row), pl.BlockSpec((nr, NSA_WIDTH), row)]
                + [pl.BlockSpec((1, win_buf, KV_WIDTH), per_b3)] * 2
                + [pl.BlockSpec((1, page // CMP_STRIDE, sub_w), page_map(p)) for p in range(n_pages)] * 1
                + [pl.BlockSpec((1, page // CMP_STRIDE, sub_w), page_map(p)) for p in range(n_pages)]
                + [pl.BlockSpec((1, page, KV_WIDTH), page_map(p)) for p in range(n_pages)]
                + [pl.BlockSpec((1, page, KV_WIDTH), page_map(p)) for p in range(n_pages)]
                + wspecs + wspecs
                + [pl.BlockSpec((LANES, LANES), fix2), pl.BlockSpec((LANES, all_keys), fix2)])
    grid_spec = pltpu.PrefetchScalarGridSpec(
        num_scalar_prefetch=1,
        grid=(batch,),
        in_specs=in_specs,
        out_specs=[pl.BlockSpec((nr, NSA_WIDTH), row),
                   pl.BlockSpec((1, win_buf, KV_WIDTH), per_b3),
                   pl.BlockSpec((1, win_buf, KV_WIDTH), per_b3)],
        scratch_shapes=[pltpu.VMEM((all_keys, KV_WIDTH), BF16), pltpu.VMEM((all_keys, KV_WIDTH), BF16),
                        pltpu.VMEM((win_keys, KV_WIDTH), BF16), pltpu.VMEM((win_keys, KV_WIDTH), BF16)])
    return pl.pallas_call(
        functools.partial(_nsa_sample_kernel, n_pages=n_pages, page=page, past=past, n_new=n_new,
                          win_keys=win_keys),
        grid_spec=grid_spec,
        out_shape=[jax.ShapeDtypeStruct((batch * nr, NSA_WIDTH), F32),
                   jax.ShapeDtypeStruct((batch, win_buf, KV_WIDTH), F32),
                   jax.ShapeDtypeStruct((batch, win_buf, KV_WIDTH), F32)],
        compiler_params=pltpu.CompilerParams(dimension_semantics=("arbitrary",),
                                             vmem_limit_bytes=VMEM_LIMIT_BYTES),
        name="nsa_sample",
    )(page_table, q, qr, sk, sv, wk, wv, sm, zn, win_k, win_v,
      *([cmp_k] * n_pages), *([cmp_v] * n_pages), *([slc_k] * n_pages), *([slc_v] * n_pages),
      *cmp_k_w, *cmp_v_w, _overlap_t(ns), e)


def _compress_weights_t(pe, w1, w2):
    w1big, w2big, pe2 = _compress_weights(pe, w1, w2)
    return w1big.T, w2big.T, pe2


def _nsa_sample_t_kernel(pt_ref, q_ref, qr_ref, skn_ref, svn_ref, wkn_ref, wvn_ref, sm_ref, zn_ref,
                         wkc_ref, wvc_ref, cmpk_hbm, cmpv_hbm, slck_hbm, slcv_hbm,
                         w1k_ref, w2k_ref, pek_ref, w1v_ref, w2v_ref, pev_ref, ovt_ref, e_ref,
                         o_ref, kt_s, vt_s, kwt_s, vwt_s, pages_s, sems, *, nb, n_pages, page, past, n_new):
    step = pl.program_id(0)
    last = pl.num_programs(0) - 1
    slot = step % 2
    caches = (cmpk_hbm, cmpv_hbm, slck_hbm, slcv_hbm)
    page_ids = [(j, c, p) for j in range(nb) for c in range(len(caches)) for p in range(n_pages)]

    def page_copy(at_step, at_slot, j, c, p):
        src = caches[c].at[pt_ref[at_step * nb + j, p]]
        return pltpu.make_async_copy(src, pages_s.at[at_slot, j, c, p], sems.at[at_slot, c])

    @pl.when(step == 0)
    def _():
        for ids in page_ids:
            page_copy(0, 0, *ids).start()

    nxt = jnp.minimum(step + 1, last)
    for ids in page_ids:
        page_copy(nxt, 1 - slot, *ids).start()
    for ids in page_ids:
        page_copy(step, slot, *ids).wait()

    cmpk, cmpv, slck, slcv = ([pages_s.at[slot, j, c, p] for j in range(nb) for p in range(n_pages)]
                              for c in range(len(caches)))
    nr = SAMPLE_ROWS
    rows = Q_HEADS * nr
    win_buf = wkc_ref.shape[2]
    bs = range(nb)
    hid = 2 * CMP_HIDDEN
    zpad = jnp.zeros((LANES - nr, LANES), F32)

    def new_rows(ref, b):
        return jnp.concatenate([ref[b * nr:(b + 1) * nr, :], zpad], axis=0).astype(BF16)

    def stack(ref, b):
        return jnp.concatenate([ref[b * nr:(b + 1) * nr, h * LANES:(h + 1) * LANES] for h in range(Q_HEADS)],
                               axis=0).astype(BF16)

    for b in bs:
        for p in range(n_pages):
            kt_s[b, :, p * page:(p + 1) * page] = slck[b * n_pages + p][...].astype(BF16)
            vt_s[b, :, p * page:(p + 1) * page] = slcv[b * n_pages + p][...].astype(BF16)
        kwt_s[b] = wkc_ref[b].astype(BF16)
        vwt_s[b] = wvc_ref[b].astype(BF16)
    mi = lax.broadcasted_iota(jnp.int32, (page, page), 0)
    ki = lax.broadcasted_iota(jnp.int32, (page, page), 1)
    per_page = page // CMP_STRIDE
    perm = (ki == CMP_STRIDE * (mi % per_page) + mi // per_page).astype(BF16)
    rows_t = [[[_mm(perm, pages[b * n_pages + p][...], _NT) for p in range(n_pages)] for pages in (cmpk, cmpv)]
              for b in bs]

    q_raw = [stack(q_ref, b) for b in bs]
    q_rot = [stack(qr_ref, b) for b in bs]

    sw = [jnp.concatenate([_mm(q_rot[b], kwt_s[b]), _mm(q_rot[b], new_rows(wkn_ref, b), _NT)], axis=1) for b in bs]
    ss = [jnp.concatenate([_mm(q_rot[b], kt_s[b]), _mm(q_rot[b], new_rows(skn_ref, b), _NT)], axis=1) for b in bs]

    def sub_blocks(b, c):
        cols = [jnp.concatenate([rows_t[b][c][p][l * per_page:(l + 1) * per_page] for p in range(n_pages)], axis=0)
                for l in range(CMP_STRIDE)]
        return jnp.concatenate(cols, axis=1)

    ckt, cvt = [], []
    for c, (w1_ref, w2_ref, pe_ref, out) in enumerate(((w1k_ref, w2k_ref, pek_ref, ckt), (w1v_ref, w2v_ref, pev_ref, cvt))):
        pe = pe_ref[...]
        subs = [sub_blocks(b, c) for b in bs]
        h_a = [_mm(w1_ref[0:hid, :], subs[b] + pe[0:1, :], _NT) for b in bs]
        h_b = [_mm(w1_ref[hid:2 * hid, :], subs[b] + pe[1:2, :], _NT) for b in bs]
        h = [_silu(h_a[b] + pltpu.roll(h_b[b], N_SUB - 1, 1)) for b in bs]
        out.extend(_mm(w2_ref[...], h[b]) for b in bs)

    tok = lax.broadcasted_iota(jnp.int32, (nr, LANES), 0)
    col = lax.broadcasted_iota(jnp.int32, (nr, LANES), 1)
    cm = _tile_rows((CMP_STRIDE * col + (CMP_LEN - 1) <= past + tok) & (col < N_CMP), Q_HEADS)
    cm_f = cm.astype(F32)
    sc = [jnp.where(cm, _mm(q_raw[b], ckt[b]), MASK_VALUE) for b in bs]
    ec = [jnp.exp(sc[b] - jnp.max(sc[b], axis=1, keepdims=True)) for b in bs]
    pc = [ec[b] / jnp.sum(ec[b], axis=1, keepdims=True) * cm_f for b in bs]
    o_cmp = [_mm(pc[b], cvt[b], _NT) for b in bs]
    p_sum = jnp.concatenate(
        [sum(pc[b][(kvh * GROUP + g) * nr:(kvh * GROUP + g + 1) * nr] for g in range(GROUP))
         for b in bs for kvh in range(KV_HEADS)], axis=0)
    ns = (past + n_new + SEL_BLOCK - 1) // SEL_BLOCK
    ns_rows = -(-ns // SUBLANES) * SUBLANES
    nq = nb * KV_HEADS * nr
    imp_t = _mm_exact_lhs(ovt_ref[...], p_sum, _NT)[0:ns_rows]
    n_idx = lax.broadcasted_iota(jnp.int32, (ns_rows, nq), 0)
    qb_t = (past + lax.broadcasted_iota(jnp.int32, (ns_rows, nq), 1) % nr) // SEL_BLOCK
    sel_t = _select_blocks(imp_t, n_idx, qb_t, ns)
    sel_t = jnp.concatenate([sel_t, jnp.zeros((LANES - ns_rows, nq), F32)], axis=0)
    sel = _mm(_eye(nq, BF16), sel_t, _NT)

    all_keys = past + LANES
    q_pos = past + lax.broadcasted_iota(jnp.int32, (rows, all_keys), 0) % nr
    causal = lax.broadcasted_iota(jnp.int32, (rows, all_keys), 1) <= q_pos
    o_slc = []
    for b in bs:
        sel_rows = jnp.concatenate(
            [_tile_rows(sel[(b * KV_HEADS + kvh) * nr:(b * KV_HEADS + kvh + 1) * nr], GROUP) for kvh in range(KV_HEADS)],
            axis=0)
        ok = (_mm(sel_rows, e_ref[...]) > 0.5) & causal
        s = ss[b] + jnp.where(ok, 0.0, MASK_VALUE)
        pr = jnp.exp(s - jnp.max(s, axis=1, keepdims=True))
        pr = pr / jnp.sum(pr, axis=1, keepdims=True)
        o_slc.append(_mm(pr[:, :past], vt_s[b], _NT) + _mm(pr[:, past:], new_rows(svn_ref, b)))

    win_keys = win_buf + LANES
    q_pos = past + lax.broadcasted_iota(jnp.int32, (rows, win_keys), 0) % nr
    diff = q_pos - (past - win_buf + lax.broadcasted_iota(jnp.int32, (rows, win_keys), 1))
    bias_w = jnp.where((diff >= 0) & (diff <= WINDOW), 0.0, MASK_VALUE)
    o_win = []
    for b in bs:
        s = sw[b] + bias_w
        pr = jnp.exp(s - jnp.max(s, axis=1, keepdims=True))
        pr = pr / jnp.sum(pr, axis=1, keepdims=True)
        o_win.append(_mm(pr[:, :win_buf], vwt_s[b], _NT) + _mm(pr[:, win_buf:], new_rows(wvn_ref, b)))

    for b in bs:
        sig = _sigmoid(sm_ref[b * nr:(b + 1) * nr, :])

        def gate(branch):
            cols = [sig[:, branch * Q_HEADS + h:branch * Q_HEADS + h + 1] for h in range(Q_HEADS)]
            return jnp.concatenate([jnp.broadcast_to(c, (nr, LANES)) for c in cols], axis=0)

        o_all = gate(0) * o_cmp[b] + gate(1) * o_slc[b] + gate(2) * o_win[b]
        for kvh in range(KV_HEADS):
            o_kv = o_all[kvh * GROUP * nr:(kvh + 1) * GROUP * nr]
            for pair, slab in enumerate(_merge_heads(o_kv, kvh, nr)):
                lo = (kvh * (GROUP // 2) + pair) * LANES
                o_ref[b * nr:(b + 1) * nr, lo:lo + LANES] = slab * _silu(zn_ref[b * nr:(b + 1) * nr, lo:lo + LANES])

    @pl.when(step == last)
    def _():
        for ids in page_ids:
            page_copy(nxt, 1 - slot, *ids).wait()


def _nsa_sample_t(q, qr, sk, sv, wk, wv, sm, zn, win_k, win_v, cmp_k, cmp_v, slc_k, slc_v, page_table,
                  cmp_k_w, cmp_v_w, n_new, nb):
    batch, n_pages = page_table.shape
    n_pool, page = cmp_k.shape[0], cmp_k.shape[1]
    past = n_pages * page
    nr = SAMPLE_ROWS
    win_buf = win_k.shape[1]
    ns = (past + n_new + SEL_BLOCK - 1) // SEL_BLOCK
    all_keys = past + LANES
    e = jnp.asarray(np.arange(LANES)[:, None] == (np.arange(all_keys) // SEL_BLOCK)[None, :], BF16)
    tview = lambda c: jnp.transpose(c, (0, 2, 3, 1)).reshape(c.shape[0], KV_WIDTH, c.shape[1])
    cmp_k, cmp_v, slc_k, slc_v, win_k, win_v = (tview(c) for c in (cmp_k, cmp_v, slc_k, slc_v, win_k, win_v))

    row = lambda b, pt: (b, 0)
    per_b3 = lambda b, pt: (b, 0, 0)
    fix2 = lambda b, pt: (0, 0)
    sub_w = CMP_STRIDE * KV_WIDTH
    wspecs = [pl.BlockSpec((4 * CMP_HIDDEN, sub_w), fix2),
              pl.BlockSpec((KV_WIDTH, 2 * CMP_HIDDEN), fix2),
              pl.BlockSpec((SUBLANES, sub_w), fix2)]
    in_specs = ([pl.BlockSpec((nb * nr, QPAD_WIDTH), row)] * 2 + [pl.BlockSpec((nb * nr, KV_WIDTH), row)] * 4
                + [pl.BlockSpec((nb * nr, LANES), row), pl.BlockSpec((nb * nr, NSA_WIDTH), row)]
                + [pl.BlockSpec((nb, KV_WIDTH, win_buf), per_b3)] * 2
                + [pl.BlockSpec(memory_space=pl.ANY)] * 4 + wspecs + wspecs
                + [pl.BlockSpec((LANES, LANES), fix2), pl.BlockSpec((LANES, all_keys), fix2)])
    grid_spec = pltpu.PrefetchScalarGridSpec(
        num_scalar_prefetch=1,
        grid=(batch // nb,),
        in_specs=in_specs,
        out_specs=pl.BlockSpec((nb * nr, NSA_WIDTH), row),
        scratch_shapes=[pltpu.VMEM((nb, KV_WIDTH, past), BF16), pltpu.VMEM((nb, KV_WIDTH, past), BF16),
                        pltpu.VMEM((nb, KV_WIDTH, win_buf), BF16), pltpu.VMEM((nb, KV_WIDTH, win_buf), BF16),
                        pltpu.VMEM((2, nb, 4, n_pages, KV_WIDTH, page), F32),
                        pltpu.SemaphoreType.DMA((2, 4))])
    return pl.pallas_call(
        functools.partial(_nsa_sample_t_kernel, nb=nb, n_pages=n_pages, page=page, past=past, n_new=n_new),
        grid_spec=grid_spec,
        out_shape=jax.ShapeDtypeStruct((batch * nr, NSA_WIDTH), F32),
        compiler_params=pltpu.CompilerParams(dimension_semantics=("arbitrary",),
                                             vmem_limit_bytes=VMEM_LIMIT_BYTES),
        name="nsa_sample",
    )(page_table, q, qr, sk, sv, wk, wv, sm, zn, win_k, win_v, cmp_k, cmp_v, slc_k, slc_v,
      *cmp_k_w, *cmp_v_w, _overlap_t(ns), e)


def _unit_lower_inverse(a, n_valid, hi):
    c = a.shape[0]
    inv = _eye(c, F32) - a
    power = a
    span = 2
    while span < n_valid:
        power = _mm(power, power, hi=hi)
        inv = inv + _mm(inv, power, hi=hi)
        span *= 2
    return inv


def _gdn_kernel(qkv_ref, sm_ref, zg_ref, conv0_ref, s0_ref, wc_ref, vec_ref, wg_ref,
                go_ref, xp_out_ref, st_ref, xp_s, *, bblk, chunk, n_valid, hi):
    c_idx = pl.program_id(1)

    @pl.when(c_idx == 0)
    def _():
        xp_s[:, 0:SUBLANES, :] = conv0_ref[...]
        st_ref[...] = s0_ref[...]

    wc = wc_ref[...]
    vec = vec_ref[...]
    row1 = lax.broadcasted_iota(jnp.int32, (chunk, LANES), 0)
    rr = lax.broadcasted_iota(jnp.int32, (chunk, chunk), 0)
    cc = lax.broadcasted_iota(jnp.int32, (chunk, chunk), 1)
    tri = rr >= cc
    tri_bf = tri.astype(BF16)
    eye_bf = _eye(LANES, BF16)

    acts, betas, decays, decay_ts, e_decs = [], [], [], [], []
    for b in range(bblk):
        xp_s[b, SUBLANES:SUBLANES + chunk, :] = qkv_ref[b]
        y = xp_s[b, SUBLANES:SUBLANES + chunk, :] * wc[GDN_CONV - 1:GDN_CONV, :]
        for j in range(GDN_CONV - 1):
            lo = SUBLANES - (GDN_CONV - 1) + j
            y = y + xp_s[b, lo:lo + chunk, :] * wc[j:j + 1, :]
        xp_out_ref[b] = xp_s[b]
        xp_s[b, 0:SUBLANES, :] = xp_s[b, chunk:chunk + SUBLANES, :]
        act = _silu(y)
        small = sm_ref[b]
        z = small + vec[1:2, :]
        softplus = jnp.maximum(z, 0.0) + jnp.log1p(jnp.exp(-jnp.abs(z)))
        g_all = -jnp.exp(vec[0:1, :]) * softplus
        if n_valid < chunk:
            valid = row1 < n_valid
            act = act * _tile_lanes(valid.astype(F32), GDN_CONV_CH // LANES)
            g_all = jnp.where(valid, g_all, 0.0)
        acts.append(act)
        betas.append(_sigmoid(small))
        decays.append(_mm_exact_lhs(tri_bf, g_all))
    for b in range(bblk):
        decay_ts.append(_mm_exact_lhs(eye_bf, decays[b], _NT))
        e_decs.append(jnp.exp(decays[b]))

    chains = [(b, h) for b in range(bblk) for h in range(GDN_HEADS)]
    qs, ks, kbs, dmasks, rhs_u, rhs_w, qds, kds, gls = [], [], [], [], [], [], [], [], []
    for b, h in chains:
        act = acts[b]
        qh = act[:, h * GDN_DK:(h + 1) * GDN_DK]
        kh = act[:, GDN_WIDTH + h * GDN_DK:GDN_WIDTH + (h + 1) * GDN_DK]
        vh = act[:, 2 * GDN_WIDTH + h * GDN_DV:2 * GDN_WIDTH + (h + 1) * GDN_DV]
        qh = qh * lax.rsqrt(jnp.sum(qh * qh, axis=-1, keepdims=True) + NORM_EPS) * (GDN_DK ** -0.5)
        kh = kh * lax.rsqrt(jnp.sum(kh * kh, axis=-1, keepdims=True) + NORM_EPS)
        beta = betas[b][:, SM_B + h:SM_B + h + 1]
        dcol = decays[b][:, SM_A + h:SM_A + h + 1]
        drow = decay_ts[b][SM_A + h:SM_A + h + 1, :]
        ed = e_decs[b][:, SM_A + h:SM_A + h + 1]
        dlast = decays[b][chunk - 1:chunk, SM_A + h:SM_A + h + 1]
        kb = kh * beta
        qs.append(qh)
        ks.append(kh)
        kbs.append(kb)
        dmasks.append(jnp.where(tri, jnp.exp(jnp.where(tri, dcol - drow, 0.0)), 0.0))
        rhs_u.append(vh * beta)
        rhs_w.append(kb * ed)
        qds.append(qh * ed)
        kds.append(kh * jnp.exp(dlast - dcol))
        gls.append(jnp.exp(dlast))

    n = len(chains)
    eye_c = _eye(chunk, F32)
    kkts = [_mm(kbs[i], ks[i], _NT, hi=hi) for i in range(n)]
    qks = [_mm(qs[i], ks[i], _NT, hi=hi) for i in range(n)]
    powers = [jnp.where(rr > cc, kkts[i] * dmasks[i], 0.0) for i in range(n)]
    qks = [qks[i] * dmasks[i] for i in range(n)]
    invs = [eye_c - powers[i] for i in range(n)]
    span = 2
    while span < n_valid:
        powers = [_mm(powers[i], powers[i], hi=hi) for i in range(n)]
        invs = [invs[i] + _mm(invs[i], powers[i], hi=hi) for i in range(n)]
        span *= 2
    us = [_mm(invs[i], rhs_u[i], hi=hi) for i in range(n)]
    ws = [_mm(invs[i], rhs_w[i], hi=hi) for i in range(n)]
    sts = [st_ref[b, h] for b, h in chains]
    v_news = [us[i] - _mm(ws[i], sts[i], hi=hi) for i in range(n)]
    os_ = [_mm(qds[i], sts[i], hi=hi) for i in range(n)]
    os_ = [os_[i] + _mm(qks[i], v_news[i], hi=hi) for i in range(n)]
    upd = [_mm(kds[i], v_news[i], _TN, hi=hi) for i in range(n)]
    for i, (b, h) in enumerate(chains):
        st_ref[b, h] = sts[i] * gls[i] + upd[i]
        o = os_[i]
        o = o * lax.rsqrt(jnp.mean(o * o, axis=-1, keepdims=True) + NORM_EPS) * wg_ref[...]
        o = o * _silu(zg_ref[b, :, h * GDN_DV:(h + 1) * GDN_DV])
        go_ref[b, :, h * GDN_DV:(h + 1) * GDN_DV] = o.astype(go_ref.dtype)


def _gdn(qkv, sm, zg, conv0, s0, w_conv, a_log, dt_bias, w_gnorm, batch, rows, bblk, chunk, n_valid, hi,
         out_dtype):
    nc = rows // chunk
    tile = lambda b, c: (b, c, 0)
    per_b3 = lambda b, c: (b, 0, 0)
    per_b4 = lambda b, c: (b, 0, 0, 0)
    fix2 = lambda b, c: (0, 0)
    wc = jnp.concatenate([w_conv, jnp.zeros((SUBLANES - GDN_CONV, GDN_CONV_CH), w_conv.dtype)], axis=0)
    vec = jnp.zeros((SUBLANES, LANES), F32)
    vec = vec.at[0, SM_A:SM_A + GDN_HEADS].set(a_log).at[1, SM_A:SM_A + GDN_HEADS].set(dt_bias)
    go, xp, st = pl.pallas_call(
        functools.partial(_gdn_kernel, bblk=bblk, chunk=chunk, n_valid=n_valid, hi=hi),
        grid=(batch // bblk, nc),
        in_specs=[pl.BlockSpec((bblk, chunk, GDN_CONV_CH), tile),
                  pl.BlockSpec((bblk, chunk, LANES), tile),
                  pl.BlockSpec((bblk, chunk, GDN_WIDTH), tile),
                  pl.BlockSpec((bblk, SUBLANES, GDN_CONV_CH), per_b3),
                  pl.BlockSpec((bblk, GDN_HEADS, GDN_DK, GDN_DV), per_b4),
                  pl.BlockSpec((SUBLANES, GDN_CONV_CH), fix2),
                  pl.BlockSpec((SUBLANES, LANES), fix2),
                  pl.BlockSpec((1, GDN_DV), fix2)],
        out_specs=[pl.BlockSpec((bblk, chunk, GDN_WIDTH), tile),
                   pl.BlockSpec((bblk, SUBLANES + chunk, GDN_CONV_CH), per_b3),
                   pl.BlockSpec((bblk, GDN_HEADS, GDN_DK, GDN_DV), per_b4)],
        out_shape=[jax.ShapeDtypeStruct((batch, rows, GDN_WIDTH), out_dtype),
                   jax.ShapeDtypeStruct((batch, SUBLANES + chunk, GDN_CONV_CH), F32),
                   jax.ShapeDtypeStruct((batch, GDN_HEADS, GDN_DK, GDN_DV), F32)],
        scratch_shapes=[pltpu.VMEM((bblk, SUBLANES + chunk, GDN_CONV_CH), F32)],
        compiler_params=pltpu.CompilerParams(dimension_semantics=("arbitrary", "arbitrary"),
                                             vmem_limit_bytes=VMEM_LIMIT_BYTES),
        name="gdn",
    )(qkv.reshape(batch, rows, GDN_CONV_CH), sm.reshape(batch, rows, LANES), zg.reshape(batch, rows, GDN_WIDTH),
      conv0, s0, wc, vec, w_gnorm.reshape(1, GDN_DV))
    return go.reshape(batch * rows, GDN_WIDTH), xp, st


PROMPT_TM = 512
SAMPLE_TM = 256
GDN_PROMPT_BBLK = 8
GDN_SAMPLE_BBLK = 8
NSA_SAMPLE_NB = 2


def _layer_prompt(h, lw, final, w_final, win_buf):
    w_norm, w_pack, cmp_k_w, cmp_v_w, w_conv, a_log, dt_bias, w_gnorm, w_out = lw
    batch, seq, d = h.shape
    x2d = h.reshape(batch * seq, d)
    tables = _rope_tables(jnp.arange(seq, dtype=jnp.int32))
    (q, qr, ck, cv, sk, sv, wk, wv, kvb, zn, qkv, zg, sm, ck_t, cv_t, sk_t, sv_t, wk_t, wv_t) = _project(
        x2d, w_norm, w_pack.astype(BF16), tables, seq, PROMPT_TM, False, BF16, True)
    nsa = _nsa_prompt(q, qr, kvb, ck, cv, sm, zn, cmp_k_w, cmp_v_w, batch, seq)
    conv0 = jnp.zeros((batch, SUBLANES, GDN_CONV_CH), F32)
    s0 = jnp.zeros((batch, GDN_HEADS, GDN_DK, GDN_DV), F32)
    go, xp, st = _gdn(qkv, sm, zg, conv0, s0, w_conv, a_log, dt_bias, w_gnorm,
                      batch, seq, GDN_PROMPT_BBLK, GDN_CHUNK, GDN_CHUNK, False, BF16)
    y = _out_project(x2d, nsa, go, w_out.astype(BF16), w_final, PROMPT_TM, False, final)
    conv_new = xp[:, SUBLANES + GDN_CHUNK - (GDN_CONV - 1):SUBLANES + GDN_CHUNK]
    from_t = lambda t: jnp.transpose(t.reshape(batch, KV_HEADS, HEAD_DIM, t.shape[-1]), (0, 3, 1, 2))
    lead = ((0, 0), (0, 0), (max(win_buf - seq, 0), 0))
    win = lambda t: from_t(jnp.pad(t, lead)[:, :, -win_buf:])
    return y.reshape(batch, seq, d), (from_t(ck_t), from_t(cv_t), from_t(sk_t), from_t(sv_t), win(wk_t), win(wv_t),
                                      conv_new, st)


def _layer_sample(h8, n_new, caches, page_table, lw, final, w_final):
    w_norm, w_pack, cmp_k_w, cmp_v_w, w_conv, a_log, dt_bias, w_gnorm, w_out = lw
    cmp_k_t = (cmp_k_w[0].T, cmp_k_w[1].T, cmp_k_w[2])
    cmp_v_t = (cmp_v_w[0].T, cmp_v_w[1].T, cmp_v_w[2])
    c_cmp_k, c_cmp_v, c_slc_k, c_slc_v, c_win_k, c_win_v, s_conv, s_gdn = caches
    batch, nr, d = h8.shape
    past = page_table.shape[1] * c_cmp_k.shape[1]
    x2d = h8.reshape(batch * nr, d)
    tables = _rope_tables(past + jnp.arange(nr, dtype=jnp.int32))
    tables = tuple(jnp.tile(t, (SAMPLE_TM // nr, 1)) for t in tables)
    (q, qr, ck, cv, sk, sv, wk, wv, _, zn, qkv, zg, sm) = _project(
        x2d, w_norm, w_pack, tables, SAMPLE_TM, SAMPLE_TM, True, F32, False)
    nsa = _nsa_sample_t(q, qr, sk, sv, wk, wv, sm, zn, c_win_k, c_win_v, c_cmp_k, c_cmp_v, c_slc_k, c_slc_v,
                        page_table, cmp_k_t, cmp_v_t, n_new, NSA_SAMPLE_NB)
    conv0 = jnp.pad(s_conv, ((0, 0), (SUBLANES - (GDN_CONV - 1), 0), (0, 0)))
    go, xp, st = _gdn(qkv, sm, zg, conv0, s_gdn, w_conv, a_log, dt_bias, w_gnorm,
                      batch, nr, GDN_SAMPLE_BBLK, nr, n_new, True, F32)
    y = _out_project(x2d, nsa, go, w_out, w_final, SAMPLE_TM, False, final)
    kv4 = lambda t: t.reshape(batch, nr, KV_HEADS, HEAD_DIM)[:, :n_new]
    win_k = jnp.concatenate([c_win_k[:, n_new:], kv4(wk)], axis=1)
    win_v = jnp.concatenate([c_win_v[:, n_new:], kv4(wv)], axis=1)
    conv_new = xp[:, SUBLANES + n_new - (GDN_CONV - 1):SUBLANES + n_new]
    return y.reshape(batch, nr, d), (kv4(ck), kv4(cv), kv4(sk), kv4(sv), win_k, win_v, conv_new, st)


def kernel(x_prompt, x_sample, cache_cmp_k, cache_cmp_v, cache_slc_k, cache_slc_v, cache_win_k, cache_win_v, state_conv, state_gdn, page_table, w_norm, w_in, pe_cmp_k, w_cmp_k1, w_cmp_k2, pe_cmp_v, w_cmp_v1, w_cmp_v2, w_conv, a_log, dt_bias, w_gdn_norm, w_out, w_final_norm):
    depth = w_in.shape[0]
    n_new = x_sample.shape[1]
    win_buf = cache_win_k.shape[2]
    h_p = x_prompt
    h_s = jnp.pad(x_sample, ((0, 0), (0, SAMPLE_ROWS - n_new), (0, 0)))
    st_p, st_s = [], []
    for layer in range(depth):
        lw = (w_norm[layer], _pack_w_in(w_in[layer]),
              _compress_weights(pe_cmp_k[layer], w_cmp_k1[layer], w_cmp_k2[layer]),
              _compress_weights(pe_cmp_v[layer], w_cmp_v1[layer], w_cmp_v2[layer]),
              w_conv[layer], a_log[layer], dt_bias[layer], w_gdn_norm[layer], w_out[layer])
        final = layer == depth - 1
        h_p, sp = _layer_prompt(h_p, lw, final, w_final_norm, win_buf)
        caches = (cache_cmp_k[layer], cache_cmp_v[layer], cache_slc_k[layer], cache_slc_v[layer],
                  cache_win_k[layer], cache_win_v[layer], state_conv[layer], state_gdn[layer])
        h_s, ss = _layer_sample(h_s, n_new, caches, page_table, lw, final, w_final_norm)
        st_p.append(sp)
        st_s.append(ss)
    outs = [h_p, h_s[:, :n_new]]
    for i in range(8):
        outs.append(jnp.stack([s[i] for s in st_p]))
        outs.append(jnp.stack([s[i] for s in st_s]))
    return tuple(outs)
```

```python
import functools

import numpy as np
import jax
import jax.numpy as jnp
from jax import lax
from jax.experimental import pallas as pl
from jax.experimental.pallas import tpu as pltpu

F32 = jnp.float32
BF16 = jnp.bfloat16

D_MODEL = 1024
HEAD_DIM = 64
Q_HEADS = 8
KV_HEADS = 2
GROUP = Q_HEADS // KV_HEADS
NSA_WIDTH = Q_HEADS * HEAD_DIM
KV_WIDTH = KV_HEADS * HEAD_DIM
CMP_LEN = 32
CMP_STRIDE = 16
CMP_HIDDEN = 128
SEL_BLOCK = 64
TOP_N = 8
WINDOW = 512
FORCE_BONUS = 1.0e4
ROT_DIM = HEAD_DIM // 4
ROPE_THETA = 500000.0
GDN_DK = 128
GDN_DV = 128
GDN_HEADS = 4
GDN_WIDTH = GDN_HEADS * GDN_DV
GDN_CONV = 4
GDN_CONV_CH = 3 * GDN_WIDTH
GDN_CHUNK = 64
NORM_EPS = 1e-6
MASK_VALUE = -1e30

LANES = 128
SUBLANES = 8
VMEM_LIMIT_BYTES = 56 * 1024 * 1024

QPAD_WIDTH = Q_HEADS * LANES
C_Q = 0
C_KV = C_Q + QPAD_WIDTH
C_ZN = C_KV + 6 * KV_WIDTH
C_QKV = C_ZN + NSA_WIDTH
C_ZG = C_QKV + GDN_CONV_CH
C_SM = C_ZG + GDN_WIDTH
N_PACK = C_SM + LANES
SM_B = 3 * Q_HEADS
SM_A = SM_B + GDN_HEADS

TQ = 128
CK_SLC = 256
CK_WIN = 128
N_SUB = 128
N_CMP = N_SUB - CMP_LEN // CMP_STRIDE + 1
SAMPLE_ROWS = 8
KVB_SK = 0
KVB_SV = KVB_SK + KV_WIDTH
KVB_WK = KVB_SV + 2 * KV_WIDTH
KVB_WV = KVB_WK + KV_WIDTH
KVB_WIDTH = KVB_WV + 2 * KV_WIDTH
KVB_COLS = (KVB_SK, KVB_SV, KVB_WK, KVB_WV)
WIN_SPAN = WINDOW + TQ


def _pack_w_in(w_in):
    o_gate = NSA_WIDTH + 6 * KV_WIDTH
    o_zn = o_gate + 3 * Q_HEADS
    o_qkv = o_zn + NSA_WIDTH
    o_b = o_qkv + GDN_CONV_CH
    o_a = o_b + GDN_HEADS
    o_zg = o_a + GDN_HEADS
    d = w_in.shape[0]
    z64 = jnp.zeros((d, HEAD_DIM), w_in.dtype)
    qcols = []
    for hq in range(Q_HEADS):
        wq = w_in[:, hq * HEAD_DIM:(hq + 1) * HEAD_DIM]
        qcols += [wq, z64] if hq // GROUP == 0 else [z64, wq]
    pad = jnp.zeros((d, LANES - SM_A - GDN_HEADS), w_in.dtype)
    return jnp.concatenate(
        qcols + [w_in[:, NSA_WIDTH:o_gate], w_in[:, o_zn:o_qkv], w_in[:, o_qkv:o_b], w_in[:, o_zg:],
                 w_in[:, o_gate:o_zn], w_in[:, o_b:o_a], w_in[:, o_a:o_zg], pad], axis=1)


def _rope_tables(pos):
    half = ROT_DIM // 2
    inv = ROPE_THETA ** (-(jnp.arange(half, dtype=F32) * 2.0 / ROT_DIM))
    ang = pos.astype(F32)[:, None] * inv[None, :]
    cos, sin = jnp.cos(ang), jnp.sin(ang)
    n = pos.shape[0]
    one = jnp.ones((n, HEAD_DIM - ROT_DIM), F32)
    zero = jnp.zeros((n, HEAD_DIM - ROT_DIM), F32)
    zh = jnp.zeros((n, half), F32)
    c64 = jnp.concatenate([cos, cos, one], axis=1)
    a64 = jnp.concatenate([zh, sin, zero], axis=1)
    b64 = jnp.concatenate([-sin, zh, zero], axis=1)
    tile = lambda t: jnp.concatenate([t, t], axis=1)
    return tile(c64), tile(a64), tile(b64)


def _rope128(x, c, a, b):
    half = ROT_DIM // 2
    return x * c + pltpu.roll(x, half, 1) * a + pltpu.roll(x, LANES - half, 1) * b


_NN = (((1,), (0,)), ((), ()))
_NT = (((1,), (1,)), ((), ()))
_TN = (((0,), (0,)), ((), ()))


def _split_bf16(x):
    hi = x.astype(BF16)
    return hi, (x - hi.astype(F32)).astype(BF16)


def _mm(a, b, dims=_NN, hi=False):
    dot = lambda x, y: lax.dot_general(x, y, dims, preferred_element_type=F32)
    if hi:
        a_hi, a_lo = _split_bf16(a.astype(F32))
        b_hi, b_lo = _split_bf16(b.astype(F32))
        return dot(a_hi, b_hi) + (dot(a_lo, b_hi) + dot(a_hi, b_lo))
    return dot(a.astype(BF16), b.astype(BF16))


def _mm_exact_lhs(a_bf16, x, dims=_NN):
    x1 = x.astype(BF16)
    r1 = x - x1.astype(F32)
    x2 = r1.astype(BF16)
    x3 = (r1 - x2.astype(F32)).astype(BF16)
    dot = lambda t: lax.dot_general(a_bf16, t, dims, preferred_element_type=F32)
    return dot(x1) + dot(x2) + dot(x3)


def _sigmoid(x):
    return 1.0 / (1.0 + jnp.exp(-x))


def _silu(x):
    return x * _sigmoid(x)


def _tile_rows(x, n):
    return jnp.concatenate([x] * n, axis=0)


def _tile_lanes(x, n):
    return x if n == 1 else jnp.concatenate([x] * n, axis=1)


def _eye(n, dtype):
    r = lax.broadcasted_iota(jnp.int32, (n, n), 0)
    c = lax.broadcasted_iota(jnp.int32, (n, n), 1)
    return (r == c).astype(dtype)


def _proj_kernel(x_ref, wn_ref, w_ref, c_ref, a_ref, b_ref,
                 q_ref, qr_ref, ck_ref, cv_ref, sk_ref, sv_ref, wk_ref, wv_ref, kvb_ref,
                 zn_ref, qkv_ref, zg_ref, sm_ref, *t_refs, hi_gdn):
    x = x_ref[...]
    ms = jnp.mean(x * x, axis=-1, keepdims=True)
    xn = x * lax.rsqrt(ms + NORM_EPS) * wn_ref[...]
    xb = xn.astype(BF16)
    c, a, b = c_ref[...], a_ref[...], b_ref[...]
    scale = HEAD_DIM ** -0.5
    for j in range(0, Q_HEADS, 2):
        q2 = _mm(xb, w_ref[:, C_Q + j * LANES:C_Q + (j + 2) * LANES])
        for jj in range(2):
            qj = q2[:, jj * LANES:(jj + 1) * LANES]
            lo = (j + jj) * LANES
            q_ref[:, lo:lo + LANES] = (qj * scale).astype(q_ref.dtype)
            qr_ref[:, lo:lo + LANES] = (_rope128(qj, c, a, b) * scale).astype(qr_ref.dtype)
    kv_refs = (ck_ref, cv_ref, sk_ref, sv_ref, wk_ref, wv_ref)
    kv2 = [_mm(xb, w_ref[:, C_KV + j * LANES:C_KV + (j + 2) * LANES]) for j in range(0, 6, 2)]
    for j in range(6):
        kj = kv2[j // 2][:, (j % 2) * LANES:(j % 2 + 1) * LANES]
        if j in (2, 4):
            kj = _rope128(kj, c, a, b)
        kv_refs[j][...] = kj
        if j >= 2:
            lo = KVB_COLS[j - 2]
            kvb_ref[:, lo:lo + LANES] = kj.astype(BF16)
        if t_refs:
            t_refs[j][0] = kj.T
    ones = jnp.ones((x.shape[0], LANES), BF16)
    kvb_ref[:, KVB_SV + LANES:KVB_SV + 2 * LANES] = ones
    kvb_ref[:, KVB_WV + LANES:KVB_WV + 2 * LANES] = ones
    zn_ref[...] = _mm(xb, w_ref[:, C_ZN:C_QKV])
    xg = xn if hi_gdn else xb
    for j in range(3):
        lo = C_QKV + j * GDN_WIDTH
        qkv_ref[:, j * GDN_WIDTH:(j + 1) * GDN_WIDTH] = _mm(xg, w_ref[:, lo:lo + GDN_WIDTH], hi=hi_gdn)
    zg_ref[...] = _mm(xb, w_ref[:, C_ZG:C_SM])
    sm_ref[...] = _mm(xg, w_ref[:, C_SM:N_PACK], hi=hi_gdn)


def _project(x2d, w_norm, w_pack, tables, rows_per_seq, tm, hi_gdn, q_dtype, emit_t):
    n = x2d.shape[0]
    nt = rows_per_seq // tm
    row = lambda i: (i, 0)
    tab = lambda i: (i % nt, 0)
    fix = lambda i: (0, 0)
    widths = (QPAD_WIDTH, QPAD_WIDTH) + (KV_WIDTH,) * 6 + (KVB_WIDTH, NSA_WIDTH, GDN_CONV_CH, GDN_WIDTH, LANES)
    dtypes = (q_dtype, q_dtype) + (F32,) * 6 + (BF16, F32, F32, F32, F32)
    out_specs = [pl.BlockSpec((tm, w), row) for w in widths]
    out_shape = [jax.ShapeDtypeStruct((n, w), d) for w, d in zip(widths, dtypes)]
    if emit_t:
        out_specs += [pl.BlockSpec((1, KV_WIDTH, tm), lambda i: (i // nt, 0, i % nt))] * 6
        out_shape += [jax.ShapeDtypeStruct((n // rows_per_seq, KV_WIDTH, rows_per_seq), F32)] * 6
    return pl.pallas_call(
        functools.partial(_proj_kernel, hi_gdn=hi_gdn),
        grid=(n // tm,),
        in_specs=[pl.BlockSpec((tm, D_MODEL), row),
                  pl.BlockSpec((1, D_MODEL), fix),
                  pl.BlockSpec((D_MODEL, N_PACK), fix),
                  pl.BlockSpec((tm, LANES), tab),
                  pl.BlockSpec((tm, LANES), tab),
                  pl.BlockSpec((tm, LANES), tab)],
        out_specs=out_specs,
        out_shape=out_shape,
        compiler_params=pltpu.CompilerParams(dimension_semantics=("arbitrary",),
                                             vmem_limit_bytes=VMEM_LIMIT_BYTES),
        name="in_proj",
    )(x2d, w_norm.reshape(1, D_MODEL), w_pack, *tables)


def _out_kernel(x_ref, nsa_ref, gdn_ref, w_ref, wf_ref, y_ref, *, final):
    mix = jnp.concatenate([nsa_ref[...].astype(BF16), gdn_ref[...].astype(BF16)], axis=1)
    h = x_ref[...] + _mm(mix, w_ref[...])
    if final:
        ms = jnp.mean(h * h, axis=-1, keepdims=True)
        h = h * lax.rsqrt(ms + NORM_EPS) * wf_ref[...]
    y_ref[...] = h


def _out_project(x2d, nsa, gdn, w_out, w_final, tm, final):
    n = x2d.shape[0]
    row = lambda i: (i, 0)
    fix = lambda i: (0, 0)
    return pl.pallas_call(
        functools.partial(_out_kernel, final=final),
        grid=(n // tm,),
        in_specs=[pl.BlockSpec((tm, D_MODEL), row),
                  pl.BlockSpec((tm, NSA_WIDTH), row),
                  pl.BlockSpec((tm, GDN_WIDTH), row),
                  pl.BlockSpec((D_MODEL, D_MODEL), fix),
                  pl.BlockSpec((1, D_MODEL), fix)],
        out_specs=pl.BlockSpec((tm, D_MODEL), row),
        out_shape=jax.ShapeDtypeStruct((n, D_MODEL), F32),
        compiler_params=pltpu.CompilerParams(dimension_semantics=("arbitrary",),
                                             vmem_limit_bytes=VMEM_LIMIT_BYTES),
        name="out_proj",
    )(x2d, nsa, gdn, w_out, w_final.reshape(1, D_MODEL))


def _compress_weights(pe, w1, w2):
    half = CMP_STRIDE * HEAD_DIM
    z = jnp.zeros((CMP_STRIDE, HEAD_DIM, CMP_HIDDEN), w1.dtype)

    def place(wpart, h):
        wp = wpart.reshape(CMP_STRIDE, HEAD_DIM, CMP_HIDDEN)
        parts = [wp, z] if h == 0 else [z, wp]
        return jnp.stack(parts, axis=1).reshape(CMP_STRIDE * KV_WIDTH, CMP_HIDDEN)

    w1big = jnp.concatenate([place(w1[:half], 0), place(w1[:half], 1),
                             place(w1[half:], 0), place(w1[half:], 1)], axis=1)
    zz = jnp.zeros_like(w2)
    w2big = jnp.concatenate([jnp.concatenate([w2, zz], axis=1), jnp.concatenate([zz, w2], axis=1)], axis=0)
    pe_a = jnp.tile(pe[:CMP_STRIDE], (1, KV_HEADS)).reshape(1, CMP_STRIDE * KV_WIDTH)
    pe_b = jnp.tile(pe[CMP_STRIDE:], (1, KV_HEADS)).reshape(1, CMP_STRIDE * KV_WIDTH)
    pe2 = jnp.concatenate([pe_a, pe_b, jnp.zeros((SUBLANES - 2, CMP_STRIDE * KV_WIDTH), pe.dtype)], axis=0)
    return w1big.astype(BF16), w2big.astype(BF16), pe2


def _compress(sub_rows, w1_ref, w2_ref, pe_ref):
    hid = 2 * CMP_HIDDEN
    w1 = w1_ref[...]
    ab = _mm(sub_rows, w1)
    pe = pe_ref[...]
    pe_hi = pe.astype(BF16)
    pe_lo = (pe - pe_hi.astype(F32)).astype(BF16)
    r = _mm(pe_hi, w1) + _mm(pe_lo, w1)
    bias = r[0:1, :hid] + r[1:2, hid:]
    h = ab[:, :hid] + pltpu.roll(ab[:, hid:], N_SUB - 1, 0) + bias
    return _mm(_silu(h), w2_ref[...])


def _overlap_t(ns):
    c0 = np.arange(N_CMP)[None, :] * CMP_STRIDE
    b0 = np.arange(ns)[:, None] * SEL_BLOCK
    ov = np.minimum(c0 + CMP_LEN, b0 + SEL_BLOCK) - np.maximum(c0, b0)
    out = np.zeros((LANES, LANES), np.float32)
    out[:ns, :N_CMP] = np.maximum(ov, 0) / CMP_LEN
    return jnp.asarray(out, BF16)


def _cmp_attention(q_stack, ck, cv, cvalid, groups):
    s = _mm(q_stack, ck, _NT)
    cm = _tile_rows(cvalid, groups)
    s = jnp.where(cm, s, MASK_VALUE)
    m = jnp.max(s, axis=1, keepdims=True)
    e = jnp.exp(s - m)
    p = e / jnp.sum(e, axis=1, keepdims=True) * cm.astype(F32)
    return _mm(p, cv), p


def _select_blocks(imp_t, n_idx, q_blk, n_rows):
    forced = (n_idx == 0) | (n_idx == q_blk) | (n_idx == q_blk - 1)
    allowed = n_idx <= q_blk
    v = jnp.where(allowed, imp_t + FORCE_BONUS * forced.astype(F32), MASK_VALUE)
    rank = jnp.zeros(v.shape, F32)
    for j in range(n_rows):
        vj = v[j:j + 1, :]
        ge = jnp.where(vj >= v, 1.0, 0.0)
        gt = jnp.where(vj > v, 1.0, 0.0)
        rank = rank + jnp.where(n_idx > j, ge, gt)
    return ((rank < TOP_N) & allowed).astype(F32)


def _flash_init(m_s, l_s, acc_s):
    m_s[...] = jnp.full(m_s.shape, MASK_VALUE, F32)
    l_s[...] = jnp.zeros(l_s.shape, F32)
    acc_s[...] = jnp.zeros(acc_s.shape, F32)


def _flash_step(s, v_c, m_s, l_s, acc_s):
    m_prev = m_s[...]
    m_next = jnp.maximum(m_prev, jnp.max(s, axis=1, keepdims=True))
    alpha = jnp.exp(m_prev - m_next)
    p = jnp.exp(s - _tile_lanes(m_next, s.shape[1] // LANES))
    l_s[...] = alpha * l_s[...] + jnp.sum(p, axis=1, keepdims=True)
    acc_s[...] = acc_s[...] * alpha + _mm(p, v_c)
    m_s[...] = m_next


def _merge_heads(o_sum, kvh, tq):
    lane = lax.broadcasted_iota(jnp.int32, (tq, LANES), 1)
    slabs = []
    for pair in range(GROUP // 2):
        halves = []
        for par in range(2):
            o = o_sum[(2 * pair + par) * tq:(2 * pair + par + 1) * tq]
            halves.append(o if par == kvh else pltpu.roll(o, HEAD_DIM, 1))
        slabs.append(jnp.where(lane < HEAD_DIM, halves[0], halves[1]))
    return slabs


def _nsa_prompt_kernel(q_ref, qr_ref, kvb_ref, ckr_ref, cvr_ref, sm_ref, zn_ref,
                       wk1_ref, wk2_ref, pek_ref, wv1_ref, wv2_ref, pev_ref, ovt_ref, e_ref,
                       o_ref, ck_s, cv_s, s_s, mrun_s, m_s, acc_s):
    i = pl.program_id(1)

    @pl.when(i == 0)
    def _():
        ck_s[...] = _compress(ckr_ref[0], wk1_ref, wk2_ref, pek_ref).astype(BF16)
        cv_s[...] = _compress(cvr_ref[0], wv1_ref, wv2_ref, pev_ref).astype(BF16)

    kvs = range(KV_HEADS)
    t0 = i * TQ
    rows = GROUP * TQ
    row = lax.broadcasted_iota(jnp.int32, (TQ, LANES), 0)
    col = lax.broadcasted_iota(jnp.int32, (TQ, LANES), 1)
    cvalid = (CMP_STRIDE * col + (CMP_LEN - 1) <= t0 + row) & (col < N_CMP)
    cm = _tile_rows(cvalid, GROUP)
    cm_f = cm.astype(F32)
    sig = _sigmoid(sm_ref[...])

    def gate(branch, kvh):
        cols = [sig[:, branch * Q_HEADS + h:branch * Q_HEADS + h + 1] for h in range(kvh * GROUP, (kvh + 1) * GROUP)]
        return jnp.concatenate([jnp.broadcast_to(c, (TQ, LANES)) for c in cols], axis=0)

    def stack(ref, kvh):
        return jnp.concatenate([ref[:, h * LANES:(h + 1) * LANES] for h in range(kvh * GROUP, (kvh + 1) * GROUP)], axis=0)

    q_raw = [stack(q_ref, k) for k in kvs]
    q_rot = [stack(qr_ref, k) for k in kvs]

    ck, cv = ck_s[...], cv_s[...]
    s = [jnp.where(cm, _mm(q_raw[k], ck, _NT), MASK_VALUE) for k in kvs]
    e = [jnp.exp(s[k] - jnp.max(s[k], axis=1, keepdims=True)) for k in kvs]
    p = [e[k] / jnp.sum(e[k], axis=1, keepdims=True) * cm_f for k in kvs]
    o_sum = [gate(0, k) * _mm(p[k], cv) for k in kvs]
    p_sum = jnp.concatenate([p[k][0:TQ] + p[k][TQ:2 * TQ] + p[k][2 * TQ:3 * TQ] + p[k][3 * TQ:4 * TQ] for k in kvs],
                            axis=0)
    ns_rows = 32
    nq = KV_HEADS * TQ
    imp_t = _mm_exact_lhs(ovt_ref[...], p_sum, _NT)[0:ns_rows]
    n_idx = lax.broadcasted_iota(jnp.int32, (ns_rows, nq), 0)
    qb_t = (t0 + lax.broadcasted_iota(jnp.int32, (ns_rows, nq), 1) % TQ) // SEL_BLOCK
    sel_t = _select_blocks(imp_t, n_idx, qb_t, ns_rows)
    sel_t = jnp.concatenate([sel_t, jnp.zeros((LANES - ns_rows, nq), F32)], axis=0).astype(BF16)
    eye = _eye(TQ, BF16)
    sel = [_mm(eye, sel_t[:, k * TQ:(k + 1) * TQ], _NT).astype(BF16) for k in kvs]

    col_s = lax.broadcasted_iota(jnp.int32, (TQ, CK_SLC), 1)
    row_s = lax.broadcasted_iota(jnp.int32, (TQ, CK_SLC), 0)
    n_chunks = (t0 + TQ + CK_SLC - 1) // CK_SLC
    mrun_s[...] = jnp.full(mrun_s.shape, MASK_VALUE, F32)

    def score_chunks(js):
        k0 = [pl.multiple_of(j * CK_SLC, CK_SLC) for j in js]
        k_c = [kvb_ref[pl.ds(k0[n], CK_SLC), KVB_SK:KVB_SK + LANES] for n in range(len(js))]
        sc = [[_mm(q_rot[k], k_c[n], _NT) for k in kvs] for n in range(len(js))]
        picked = [[_mm(sel[k], e_ref[js[n]]) for k in kvs] for n in range(len(js))]
        for k in kvs:
            run = mrun_s[k]
            for n, j in enumerate(js):
                causal = k0[n] + col_s <= t0 + row_s
                bias = jnp.where((picked[n][k] > 0.5) & causal, 0.0, MASK_VALUE)
                sk = sc[n][k] + _tile_rows(bias, GROUP)
                s_s[k, j] = sk
                run = jnp.maximum(run, jnp.maximum(sk[:, :LANES], sk[:, LANES:]))
            mrun_s[k] = run

    def value_chunks(js):
        k0 = [pl.multiple_of(j * CK_SLC, CK_SLC) for j in js]
        v_c = [kvb_ref[pl.ds(k0[n], CK_SLC), KVB_SV:KVB_SV + 2 * LANES] for n in range(len(js))]
        pr = [[jnp.exp(s_s[k, js[n]] - _tile_lanes(m_s[k], CK_SLC // LANES)) for k in kvs] for n in range(len(js))]
        pv = [[_mm(pr[n][k], v_c[n]) for k in kvs] for n in range(len(js))]
        for k in kvs:
            acc_s[k] += sum(pv[n][k] for n in range(len(js)))

    def paired(fn):
        def pair_body(jj, carry):
            fn([2 * jj, 2 * jj + 1])
            return carry

        lax.fori_loop(0, n_chunks // 2, pair_body, 0)

        @pl.when(n_chunks % 2 == 1)
        def _():
            fn([n_chunks - 1])

    paired(score_chunks)
    for k in kvs:
        m_s[k] = jnp.broadcast_to(jnp.max(mrun_s[k], axis=1, keepdims=True), (rows, LANES))
    acc_s[...] = jnp.zeros(acc_s.shape, F32)

    paired(value_chunks)
    for k in kvs:
        acc = acc_s[k]
        o_sum[k] = o_sum[k] + gate(1, k) * (acc[:, :LANES] / acc[:, LANES:])

    ks0 = pl.multiple_of(jnp.maximum(t0 - WINDOW, 0), TQ)
    k_w = kvb_ref[pl.ds(ks0, WIN_SPAN), KVB_WK:KVB_WK + LANES]
    v_w = kvb_ref[pl.ds(ks0, WIN_SPAN), KVB_WV:KVB_WV + 2 * LANES]
    diff = (t0 + lax.broadcasted_iota(jnp.int32, (TQ, WIN_SPAN), 0)) - (ks0 + lax.broadcasted_iota(jnp.int32, (TQ, WIN_SPAN), 1))
    bias_w = _tile_rows(jnp.where((diff >= 0) & (diff <= WINDOW), 0.0, MASK_VALUE), GROUP)
    sw = [_mm(q_rot[k], k_w, _NT) + bias_w for k in kvs]
    pw = [jnp.exp(sw[k] - jnp.max(sw[k], axis=1, keepdims=True)) for k in kvs]
    rw = [_mm(pw[k], v_w) for k in kvs]
    for k in kvs:
        o_all = o_sum[k] + gate(2, k) * (rw[k][:, :LANES] / rw[k][:, LANES:])
        for pair, slab in enumerate(_merge_heads(o_all, k, TQ)):
            lo = (k * (GROUP // 2) + pair) * LANES
            o_ref[:, lo:lo + LANES] = (slab * _silu(zn_ref[:, lo:lo + LANES])).astype(o_ref.dtype)


def _nsa_prompt(q, qr, kvb, ck, cv, sm, zn, cmp_k_w, cmp_v_w, batch, seq):
    nt = seq // TQ
    ns = seq // SEL_BLOCK
    sub_w = CMP_STRIDE * KV_WIDTH
    keys = np.arange(seq)
    e = (np.arange(LANES)[None, :, None] == (keys // SEL_BLOCK).reshape(seq // CK_SLC, 1, CK_SLC))
    e = jnp.asarray(e, BF16)
    tile = lambda b, i: (b * nt + i, 0)
    per_b = lambda b, i: (b, 0)
    per_b3 = lambda b, i: (b, 0, 0)
    fix2 = lambda b, i: (0, 0)
    fix3 = lambda b, i: (0, 0, 0)
    wspecs = [pl.BlockSpec((sub_w, 4 * CMP_HIDDEN), fix2),
              pl.BlockSpec((2 * CMP_HIDDEN, KV_WIDTH), fix2),
              pl.BlockSpec((SUBLANES, sub_w), fix2)]
    rows = GROUP * TQ
    return pl.pallas_call(
        _nsa_prompt_kernel,
        grid=(batch, nt),
        in_specs=[pl.BlockSpec((TQ, QPAD_WIDTH), tile),
                  pl.BlockSpec((TQ, QPAD_WIDTH), tile),
                  pl.BlockSpec((seq, KVB_WIDTH), per_b),
                  pl.BlockSpec((1, seq // CMP_STRIDE, sub_w), per_b3),
                  pl.BlockSpec((1, seq // CMP_STRIDE, sub_w), per_b3),
                  pl.BlockSpec((TQ, LANES), tile),
                  pl.BlockSpec((TQ, NSA_WIDTH), tile)] + wspecs + wspecs + [
                  pl.BlockSpec((LANES, LANES), fix2),
                  pl.BlockSpec((seq // CK_SLC, LANES, CK_SLC), fix3)],
        out_specs=pl.BlockSpec((TQ, NSA_WIDTH), tile),
        out_shape=jax.ShapeDtypeStruct((batch * seq, NSA_WIDTH), BF16),
        scratch_shapes=[pltpu.VMEM((N_SUB, KV_WIDTH), BF16), pltpu.VMEM((N_SUB, KV_WIDTH), BF16),
                        pltpu.VMEM((KV_HEADS, seq // CK_SLC, rows, CK_SLC), F32),
                        pltpu.VMEM((KV_HEADS, rows, LANES), F32), pltpu.VMEM((KV_HEADS, rows, LANES), F32),
                        pltpu.VMEM((KV_HEADS, rows, 2 * LANES), F32)],
        compiler_params=pltpu.CompilerParams(dimension_semantics=("arbitrary", "arbitrary"),
                                             vmem_limit_bytes=VMEM_LIMIT_BYTES),
        name="nsa_prompt",
    )(q, qr, kvb, ck.reshape(batch, seq // CMP_STRIDE, sub_w), cv.reshape(batch, seq // CMP_STRIDE, sub_w),
      sm, zn, *cmp_k_w, *cmp_v_w, _overlap_t(ns), e)


def _softmax_rows(s):
    m = jnp.max(s, axis=1, keepdims=True)
    e = jnp.exp(s - m)
    return e / jnp.sum(e, axis=1, keepdims=True)


def _nsa_sample_kernel(pt_ref, q_ref, qr_ref, skn_ref, svn_ref, wkn_ref, wvn_ref, sm_ref, zn_ref,
                       wkc_ref, wvc_ref, *rest, n_pages, page, past, n_new, win_keys):
    del pt_ref
    cmpk, cmpv = rest[0:n_pages], rest[n_pages:2 * n_pages]
    slck, slcv = rest[2 * n_pages:3 * n_pages], rest[3 * n_pages:4 * n_pages]
    wk1_ref, wk2_ref, pek_ref, wv1_ref, wv2_ref, pev_ref, ovt_ref, e_ref = rest[4 * n_pages:4 * n_pages + 8]
    o_ref, wko_ref, wvo_ref = rest[4 * n_pages + 8:4 * n_pages + 11]
    kall, vall, kwin, vwin = rest[4 * n_pages + 11:]
    nr = SAMPLE_ROWS
    rows = Q_HEADS * nr
    all_keys = kall.shape[0]
    win_buf = wkc_ref.shape[1]
    zeros8 = jnp.zeros((nr, LANES), F32)

    for p in range(n_pages):
        kall[p * page:(p + 1) * page, :] = slck[p][0].astype(BF16)
        vall[p * page:(p + 1) * page, :] = slcv[p][0].astype(BF16)
    kall[past:past + 2 * nr, :] = jnp.concatenate([skn_ref[...], zeros8], axis=0).astype(BF16)
    vall[past:past + 2 * nr, :] = jnp.concatenate([svn_ref[...], zeros8], axis=0).astype(BF16)
    kall[past + 2 * nr:, :] = jnp.zeros((all_keys - past - 2 * nr, LANES), BF16)
    vall[past + 2 * nr:, :] = jnp.zeros((all_keys - past - 2 * nr, LANES), BF16)
    kwin[0:win_buf, :] = wkc_ref[0].astype(BF16)
    vwin[0:win_buf, :] = wvc_ref[0].astype(BF16)
    kwin[win_buf:win_buf + 2 * nr, :] = jnp.concatenate([wkn_ref[...], zeros8], axis=0).astype(BF16)
    vwin[win_buf:win_buf + 2 * nr, :] = jnp.concatenate([wvn_ref[...], zeros8], axis=0).astype(BF16)
    kwin[win_buf + 2 * nr:, :] = jnp.zeros((win_keys - win_buf - 2 * nr, LANES), BF16)
    vwin[win_buf + 2 * nr:, :] = jnp.zeros((win_keys - win_buf - 2 * nr, LANES), BF16)

    ck = _compress(jnp.concatenate([r[0] for r in cmpk], axis=0), wk1_ref, wk2_ref, pek_ref)
    cv = _compress(jnp.concatenate([r[0] for r in cmpv], axis=0), wv1_ref, wv2_ref, pev_ref)

    q_raw = jnp.concatenate([q_ref[:, h * LANES:(h + 1) * LANES] for h in range(Q_HEADS)], axis=0)
    q_rot = jnp.concatenate([qr_ref[:, h * LANES:(h + 1) * LANES] for h in range(Q_HEADS)], axis=0)

    tok = lax.broadcasted_iota(jnp.int32, (nr, LANES), 0)
    col = lax.broadcasted_iota(jnp.int32, (nr, LANES), 1)
    cvalid = (CMP_STRIDE * col + (CMP_LEN - 1) <= past + tok) & (col < N_CMP)
    o_cmp, p = _cmp_attention(q_raw, ck, cv, cvalid, Q_HEADS)

    p_sum = []
    for kvh in range(KV_HEADS):
        base = kvh * GROUP * nr
        p_sum.append(sum(p[base + g * nr:base + (g + 1) * nr] for g in range(GROUP)))
    p_sum = jnp.concatenate(p_sum, axis=0)
    ns = (past + n_new + SEL_BLOCK - 1) // SEL_BLOCK
    ns_rows = -(-ns // SUBLANES) * SUBLANES
    nq = KV_HEADS * nr
    imp_t = _mm_exact_lhs(ovt_ref[...], p_sum, _NT)[0:ns_rows]
    n_idx = lax.broadcasted_iota(jnp.int32, (ns_rows, nq), 0)
    qb_t = (past + lax.broadcasted_iota(jnp.int32, (ns_rows, nq), 1) % nr) // SEL_BLOCK
    sel_t = _select_blocks(imp_t, n_idx, qb_t, ns)
    sel_t = jnp.concatenate([sel_t, jnp.zeros((LANES - ns_rows, nq), F32)], axis=0)
    sel = _mm(_eye(nq, BF16), sel_t, _NT)
    sel_rows = jnp.concatenate([_tile_rows(sel[kvh * nr:(kvh + 1) * nr], GROUP) for kvh in range(KV_HEADS)], axis=0)

    picked = _mm(sel_rows, e_ref[...])
    q_pos = past + lax.broadcasted_iota(jnp.int32, (rows, all_keys), 0) % nr
    k_pos = lax.broadcasted_iota(jnp.int32, (rows, all_keys), 1)
    ok = (picked > 0.5) & (k_pos <= q_pos)
    s = _mm(q_rot, kall[...], _NT) + jnp.where(ok, 0.0, MASK_VALUE)
    o_slc = _mm(_softmax_rows(s), vall[...])

    q_pos = past + lax.broadcasted_iota(jnp.int32, (rows, win_keys), 0) % nr
    k_pos = past - win_buf + lax.broadcasted_iota(jnp.int32, (rows, win_keys), 1)
    diff = q_pos - k_pos
    ok = (diff >= 0) & (diff <= WINDOW)
    s = _mm(q_rot, kwin[...], _NT) + jnp.where(ok, 0.0, MASK_VALUE)
    o_win = _mm(_softmax_rows(s), vwin[...])

    sig = _sigmoid(sm_ref[...])

    def gate(branch):
        cols = [sig[:, branch * Q_HEADS + h:branch * Q_HEADS + h + 1] for h in range(Q_HEADS)]
        return jnp.concatenate([jnp.broadcast_to(c, (nr, LANES)) for c in cols], axis=0)

    o_all = gate(0) * o_cmp + gate(1) * o_slc + gate(2) * o_win
    for kvh in range(KV_HEADS):
        o_kv = o_all[kvh * GROUP * nr:(kvh + 1) * GROUP * nr]
        for pair, slab in enumerate(_merge_heads(o_kv, kvh, nr)):
            lo = (kvh * (GROUP // 2) + pair) * LANES
            o_ref[:, lo:lo + LANES] = (slab * _silu(zn_ref[:, lo:lo + LANES])).astype(o_ref.dtype)

    for cache_ref, new_ref, out_ref in ((wkc_ref, wkn_ref, wko_ref), (wvc_ref, wvn_ref, wvo_ref)):
        shifted = pltpu.roll(cache_ref[0], win_buf - n_new, 0)
        out_ref[0] = shifted
        tail = jnp.where(tok < nr - n_new, shifted[win_buf - nr:], pltpu.roll(new_ref[...], nr - n_new, 0))
        out_ref[0, win_buf - nr:, :] = tail


def _nsa_sample(q, qr, sk, sv, wk, wv, sm, zn, win_k, win_v, cmp_k, cmp_v, slc_k, slc_v, page_table,
                cmp_k_w, cmp_v_w, n_new):
    batch, n_pages = page_table.shape
    n_pool, page = cmp_k.shape[0], cmp_k.shape[1]
    past = n_pages * page
    nr = SAMPLE_ROWS
    sub_w = CMP_STRIDE * KV_WIDTH
    win_buf = win_k.shape[1]
    all_keys = -(-(past + 2 * nr) // LANES) * LANES
    win_keys = -(-(win_buf + 2 * nr) // LANES) * LANES
    ns = (past + n_new + SEL_BLOCK - 1) // SEL_BLOCK
    e = jnp.asarray(np.arange(LANES)[:, None] == (np.arange(all_keys) // SEL_BLOCK)[None, :], BF16)
    cmp_k = cmp_k.reshape(n_pool, page // CMP_STRIDE, sub_w)
    cmp_v = cmp_v.reshape(n_pool, page // CMP_STRIDE, sub_w)
    slc_k = slc_k.reshape(n_pool, page, KV_WIDTH)
    slc_v = slc_v.reshape(n_pool, page, KV_WIDTH)
    win_k = win_k.reshape(batch, win_buf, KV_WIDTH)
    win_v = win_v.reshape(batch, win_buf, KV_WIDTH)

    row = lambda b, pt: (b, 0)
    per_b3 = lambda b, pt: (b, 0, 0)
    fix2 = lambda b, pt: (0, 0)
    page_map = lambda p: (lambda b, pt: (pt[b, p], 0, 0))
    wspecs = [pl.BlockSpec((sub_w, 4 * CMP_HIDDEN), fix2),
              pl.BlockSpec((2 * CMP_HIDDEN, KV_WIDTH), fix2),
              pl.BlockSpec((SUBLANES, sub_w), fix2)]
    in_specs = ([pl.BlockSpec((nr, QPAD_WIDTH), row)] * 2 + [pl.BlockSpec((nr, KV_WIDTH), row)] * 4
                + [pl.BlockSpec((nr, LANES), row), pl.BlockSpec((nr, NSA_WIDTH), row)]
                + [pl.BlockSpec((1, win_buf, KV_WIDTH), per_b3)] * 2
                + [pl.BlockSpec((1, page // CMP_STRIDE, sub_w), page_map(p)) for p in range(n_pages)] * 1
                + [pl.BlockSpec((1, page // CMP_STRIDE, sub_w), page_map(p)) for p in range(n_pages)]
                + [pl.BlockSpec((1, page, KV_WIDTH), page_map(p)) for p in range(n_pages)]
                + [pl.BlockSpec((1, page, KV_WIDTH), page_map(p)) for p in range(n_pages)]
                + wspecs + wspecs
                + [pl.BlockSpec((LANES, LANES), fix2), pl.BlockSpec((LANES, all_keys), fix2)])
    grid_spec = pltpu.PrefetchScalarGridSpec(
        num_scalar_prefetch=1,
        grid=(batch,),
        in_specs=in_specs,
        out_specs=[pl.BlockSpec((nr, NSA_WIDTH), row),
                   pl.BlockSpec((1, win_buf, KV_WIDTH), per_b3),
                   pl.BlockSpec((1, win_buf, KV_WIDTH), per_b3)],
        scratch_shapes=[pltpu.VMEM((all_keys, KV_WIDTH), BF16), pltpu.VMEM((all_keys, KV_WIDTH), BF16),
                        pltpu.VMEM((win_keys, KV_WIDTH), BF16), pltpu.VMEM((win_keys, KV_WIDTH), BF16)])
    return pl.pallas_call(
        functools.partial(_nsa_sample_kernel, n_pages=n_pages, page=page, past=past, n_new=n_new,
                          win_keys=win_keys),
        grid_spec=grid_spec,
        out_shape=[jax.ShapeDtypeStruct((batch * nr, NSA_WIDTH), F32),
                   jax.ShapeDtypeStruct((batch, win_buf, KV_WIDTH), F32),
                   jax.ShapeDtypeStruct((batch, win_buf, KV_WIDTH), F32)],
        compiler_params=pltpu.CompilerParams(dimension_semantics=("arbitrary",),
                                             vmem_limit_bytes=VMEM_LIMIT_BYTES),
        name="nsa_sample",
    )(page_table, q, qr, sk, sv, wk, wv, sm, zn, win_k, win_v,
      *([cmp_k] * n_pages), *([cmp_v] * n_pages), *([slc_k] * n_pages), *([slc_v] * n_pages),
      *cmp_k_w, *cmp_v_w, _overlap_t(ns), e)


def _compress_weights_t(pe, w1, w2):
    w1big, w2big, pe2 = _compress_weights(pe, w1, w2)
    return w1big.T, w2big.T, pe2


def _nsa_sample_t_kernel(pt_ref, q_ref, qr_ref, skn_ref, svn_ref, wkn_ref, wvn_ref, sm_ref, zn_ref,
                         wkc_ref, wvc_ref, cmpk_hbm, cmpv_hbm, slck_hbm, slcv_hbm,
                         w1k_ref, w2k_ref, pek_ref, w1v_ref, w2v_ref, pev_ref, ovt_ref, e_ref,
                         o_ref, wko_ref, wvo_ref, kt_s, vt_s, kwt_s, vwt_s, pages_s, sems,
                         *, nb, n_pages, page, past, n_new):
    step = pl.program_id(0)
    last = pl.num_programs(0) - 1
    slot = step % 2
    caches = (cmpk_hbm, cmpv_hbm, slck_hbm, slcv_hbm)
    page_ids = [(j, c, p) for j in range(nb) for c in range(len(caches)) for p in range(n_pages)]

    def page_copy(at_step, at_slot, j, c, p):
        src = caches[c].at[pt_ref[at_step * nb + j, p]]
        return pltpu.make_async_copy(src, pages_s.at[at_slot, j, c, p], sems.at[at_slot, c])

    @pl.when(step == 0)
    def _():
        for ids in page_ids:
            page_copy(0, 0, *ids).start()

    nxt = jnp.minimum(step + 1, last)
    for ids in page_ids:
        page_copy(nxt, 1 - slot, *ids).start()
    for ids in page_ids:
        page_copy(step, slot, *ids).wait()

    cmpk, cmpv, slck, slcv = ([pages_s.at[slot, j, c, p] for j in range(nb) for p in range(n_pages)]
                              for c in range(len(caches)))
    nr = SAMPLE_ROWS
    rows = Q_HEADS * nr
    win_buf = wkc_ref.shape[2]
    bs = range(nb)
    hid = 2 * CMP_HIDDEN
    zpad = jnp.zeros((LANES - nr, LANES), F32)

    def new_rows(ref, b):
        return jnp.concatenate([ref[b * nr:(b + 1) * nr, :], zpad], axis=0).astype(BF16)

    def stack(ref, b):
        return jnp.concatenate([ref[b * nr:(b + 1) * nr, h * LANES:(h + 1) * LANES] for h in range(Q_HEADS)],
                               axis=0).astype(BF16)

    for b in bs:
        for p in range(n_pages):
            kt_s[b, :, p * page:(p + 1) * page] = slck[b * n_pages + p][...].astype(BF16)
            vt_s[b, :, p * page:(p + 1) * page] = slcv[b * n_pages + p][...].astype(BF16)
        kwt_s[b] = wkc_ref[b].astype(BF16)
        vwt_s[b] = wvc_ref[b].astype(BF16)
    mi = lax.broadcasted_iota(jnp.int32, (page, page), 0)
    ki = lax.broadcasted_iota(jnp.int32, (page, page), 1)
    per_page = page // CMP_STRIDE
    perm = (ki == CMP_STRIDE * (mi % per_page) + mi // per_page).astype(BF16)
    rows_t = [[[_mm(perm, pages[b * n_pages + p][...], _NT) for p in range(n_pages)] for pages in (cmpk, cmpv)]
              for b in bs]

    q_raw = [stack(q_ref, b) for b in bs]
    q_rot = [stack(qr_ref, b) for b in bs]

    sw = [jnp.concatenate([_mm(q_rot[b], kwt_s[b]), _mm(q_rot[b], new_rows(wkn_ref, b), _NT)], axis=1) for b in bs]
    ss = [jnp.concatenate([_mm(q_rot[b], kt_s[b]), _mm(q_rot[b], new_rows(skn_ref, b), _NT)], axis=1) for b in bs]

    def sub_blocks(b, c):
        cols = [jnp.concatenate([rows_t[b][c][p][l * per_page:(l + 1) * per_page] for p in range(n_pages)], axis=0)
                for l in range(CMP_STRIDE)]
        return jnp.concatenate(cols, axis=1)

    ckt, cvt = [], []
    def activate(t):
        bias = t[0:hid, LANES:LANES + 1] + t[hid:2 * hid, LANES + 1:LANES + 2]
        return _silu(t[0:hid, 0:LANES] + pltpu.roll(t[hid:2 * hid, 0:LANES], N_SUB - 1, 1) + bias)

    for c, (w1_ref, w2_ref, pe_ref, out) in enumerate(((w1k_ref, w2k_ref, pek_ref, ckt), (w1v_ref, w2v_ref, pev_ref, cvt))):
        pe_rows = pe_ref[...]
        ht = [_mm(w1_ref[...], jnp.concatenate([sub_blocks(b, c).astype(BF16), pe_rows], axis=0), _NT) for b in bs]
        out.extend(_mm(w2_ref[...], activate(ht[b])) for b in bs)

    tok = lax.broadcasted_iota(jnp.int32, (nr, LANES), 0)
    col = lax.broadcasted_iota(jnp.int32, (nr, LANES), 1)
    cm = _tile_rows((CMP_STRIDE * col + (CMP_LEN - 1) <= past + tok) & (col < N_CMP), Q_HEADS)
    cm_f = cm.astype(F32)
    sc = [jnp.where(cm, _mm(q_raw[b], ckt[b]), MASK_VALUE) for b in bs]
    ec = [jnp.exp(sc[b] - jnp.max(sc[b], axis=1, keepdims=True)) for b in bs]
    pc = [ec[b] / jnp.sum(ec[b], axis=1, keepdims=True) * cm_f for b in bs]
    o_cmp = [_mm(pc[b], cvt[b], _NT) for b in bs]
    p_sum = jnp.concatenate(
        [sum(pc[b][(kvh * GROUP + g) * nr:(kvh * GROUP + g + 1) * nr] for g in range(GROUP))
         for b in bs for kvh in range(KV_HEADS)], axis=0)
    ns = (past + n_new + SEL_BLOCK - 1) // SEL_BLOCK
    ns_rows = -(-ns // SUBLANES) * SUBLANES
    nq = nb * KV_HEADS * nr
    imp_t = _mm_exact_lhs(ovt_ref[...], p_sum, _NT)[0:ns_rows]
    n_idx = lax.broadcasted_iota(jnp.int32, (ns_rows, nq), 0)
    qb_t = (past + lax.broadcasted_iota(jnp.int32, (ns_rows, nq), 1) % nr) // SEL_BLOCK
    sel_t = _select_blocks(imp_t, n_idx, qb_t, ns)
    sel_t = jnp.concatenate([sel_t, jnp.zeros((LANES - ns_rows, nq), F32)], axis=0)
    sel = _mm(_eye(nq, BF16), sel_t, _NT)

    all_keys = past + LANES
    q_pos = past + lax.broadcasted_iota(jnp.int32, (rows, all_keys), 0) % nr
    causal = lax.broadcasted_iota(jnp.int32, (rows, all_keys), 1) <= q_pos
    o_slc = []
    for b in bs:
        sel_rows = jnp.concatenate(
            [_tile_rows(sel[(b * KV_HEADS + kvh) * nr:(b * KV_HEADS + kvh + 1) * nr], GROUP) for kvh in range(KV_HEADS)],
            axis=0)
        ok = (_mm(sel_rows, e_ref[...]) > 0.5) & causal
        s = ss[b] + jnp.where(ok, 0.0, MASK_VALUE)
        pr = jnp.exp(s - jnp.max(s, axis=1, keepdims=True))
        o = _mm(pr[:, :past], vt_s[b], _NT) + _mm(pr[:, past:], new_rows(svn_ref, b))
        o_slc.append(o / jnp.sum(pr, axis=1, keepdims=True))

    win_keys = win_buf + LANES
    q_pos = past + lax.broadcasted_iota(jnp.int32, (rows, win_keys), 0) % nr
    diff = q_pos - (past - win_buf + lax.broadcasted_iota(jnp.int32, (rows, win_keys), 1))
    bias_w = jnp.where((diff >= 0) & (diff <= WINDOW), 0.0, MASK_VALUE)
    o_win = []
    for b in bs:
        s = sw[b] + bias_w
        pr = jnp.exp(s - jnp.max(s, axis=1, keepdims=True))
        o = _mm(pr[:, :win_buf], vwt_s[b], _NT) + _mm(pr[:, win_buf:], new_rows(wvn_ref, b))
        o_win.append(o / jnp.sum(pr, axis=1, keepdims=True))

    for b in bs:
        sig = _sigmoid(sm_ref[b * nr:(b + 1) * nr, :])

        def gate(branch):
            cols = [sig[:, branch * Q_HEADS + h:branch * Q_HEADS + h + 1] for h in range(Q_HEADS)]
            return jnp.concatenate([jnp.broadcast_to(c, (nr, LANES)) for c in cols], axis=0)

        o_all = gate(0) * o_cmp[b] + gate(1) * o_slc[b] + gate(2) * o_win[b]
        for kvh in range(KV_HEADS):
            o_kv = o_all[kvh * GROUP * nr:(kvh + 1) * GROUP * nr]
            for pair, slab in enumerate(_merge_heads(o_kv, kvh, nr)):
                lo = (kvh * (GROUP // 2) + pair) * LANES
                o_ref[b * nr:(b + 1) * nr, lo:lo + LANES] = slab * _silu(zn_ref[b * nr:(b + 1) * nr, lo:lo + LANES])

    eye_bf = _eye(LANES, BF16)
    lane = lax.broadcasted_iota(jnp.int32, (KV_WIDTH, LANES), 1)
    for cache_ref, new_ref, out_ref in ((wkc_ref, wkn_ref, wko_ref), (wvc_ref, wvn_ref, wvo_ref)):
        for b in bs:
            shifted = pltpu.roll(cache_ref[b], win_buf - n_new, 1)
            new_pad = jnp.concatenate([new_ref[b * nr:(b + 1) * nr, :], zpad], axis=0)
            new_t = _mm_exact_lhs(eye_bf, new_pad, _NT)
            tail = jnp.where(lane >= LANES - n_new, pltpu.roll(new_t, LANES - n_new, 1), shifted[:, win_buf - LANES:])
            out_ref[b] = jnp.concatenate([shifted[:, :win_buf - LANES], tail], axis=1)

    @pl.when(step == last)
    def _():
        for ids in page_ids:
            page_copy(nxt, 1 - slot, *ids).wait()


def _nsa_sample_t(q, qr, sk, sv, wk, wv, sm, zn, win_k, win_v, cmp_k, cmp_v, slc_k, slc_v, page_table,
                  cmp_k_w, cmp_v_w, n_new, nb):
    batch, n_pages = page_table.shape
    n_pool, page = cmp_k.shape[0], cmp_k.shape[1]
    past = n_pages * page
    nr = SAMPLE_ROWS
    win_buf = win_k.shape[1]
    ns = (past + n_new + SEL_BLOCK - 1) // SEL_BLOCK
    all_keys = past + LANES
    e = jnp.asarray(np.arange(LANES)[:, None] == (np.arange(all_keys) // SEL_BLOCK)[None, :], BF16)
    tview = lambda c: jnp.transpose(c, (0, 2, 3, 1)).reshape(c.shape[0], KV_WIDTH, c.shape[1])
    cmp_k, cmp_v, slc_k, slc_v, win_k, win_v = (tview(c) for c in (cmp_k, cmp_v, slc_k, slc_v, win_k, win_v))

    row = lambda b, pt: (b, 0)
    per_b3 = lambda b, pt: (b, 0, 0)
    fix2 = lambda b, pt: (0, 0)
    sub_w = CMP_STRIDE * KV_WIDTH
    wspecs = [pl.BlockSpec((4 * CMP_HIDDEN, sub_w), fix2),
              pl.BlockSpec((KV_WIDTH, 2 * CMP_HIDDEN), fix2),
              pl.BlockSpec((LANES, sub_w), fix2)]
    in_specs = ([pl.BlockSpec((nb * nr, QPAD_WIDTH), row)] * 2 + [pl.BlockSpec((nb * nr, KV_WIDTH), row)] * 4
                + [pl.BlockSpec((nb * nr, LANES), row), pl.BlockSpec((nb * nr, NSA_WIDTH), row)]
                + [pl.BlockSpec((nb, KV_WIDTH, win_buf), per_b3)] * 2
                + [pl.BlockSpec(memory_space=pl.ANY)] * 4 + wspecs + wspecs
                + [pl.BlockSpec((LANES, LANES), fix2), pl.BlockSpec((LANES, all_keys), fix2)])
    grid_spec = pltpu.PrefetchScalarGridSpec(
        num_scalar_prefetch=1,
        grid=(batch // nb,),
        in_specs=in_specs,
        out_specs=[pl.BlockSpec((nb * nr, NSA_WIDTH), row),
                   pl.BlockSpec((nb, KV_WIDTH, win_buf), per_b3), pl.BlockSpec((nb, KV_WIDTH, win_buf), per_b3)],
        scratch_shapes=[pltpu.VMEM((nb, KV_WIDTH, past), BF16), pltpu.VMEM((nb, KV_WIDTH, past), BF16),
                        pltpu.VMEM((nb, KV_WIDTH, win_buf), BF16), pltpu.VMEM((nb, KV_WIDTH, win_buf), BF16),
                        pltpu.VMEM((2, nb, 4, n_pages, KV_WIDTH, page), F32),
                        pltpu.SemaphoreType.DMA((2, 4))])
    return pl.pallas_call(
        functools.partial(_nsa_sample_t_kernel, nb=nb, n_pages=n_pages, page=page, past=past, n_new=n_new),
        grid_spec=grid_spec,
        out_shape=[jax.ShapeDtypeStruct((batch * nr, NSA_WIDTH), F32),
                   jax.ShapeDtypeStruct((batch, KV_WIDTH, win_buf), F32),
                   jax.ShapeDtypeStruct((batch, KV_WIDTH, win_buf), F32)],
        compiler_params=pltpu.CompilerParams(dimension_semantics=("arbitrary",),
                                             vmem_limit_bytes=VMEM_LIMIT_BYTES),
        name="nsa_sample",
    )(page_table, q, qr, sk, sv, wk, wv, sm, zn, win_k, win_v, cmp_k, cmp_v, slc_k, slc_v,
      *cmp_k_w, *cmp_v_w, _overlap_t(ns), e)


def _unit_lower_inverse(a, n_valid, hi):
    c = a.shape[0]
    inv = _eye(c, F32) - a
    power = a
    span = 2
    while span < n_valid:
        power = _mm(power, power, hi=hi)
        inv = inv + _mm(inv, power, hi=hi)
        span *= 2
    return inv


def _gdn_kernel(qkv_ref, sm_ref, zg_ref, conv0_ref, s0_ref, wc_ref, vec_ref, wg_ref,
                go_ref, xp_out_ref, st_ref, xp_s, *, bblk, chunk, n_valid, hi):
    c_idx = pl.program_id(1)

    @pl.when(c_idx == 0)
    def _():
        xp_s[:, 0:SUBLANES, :] = conv0_ref[...]
        st_ref[...] = s0_ref[...]

    wc = wc_ref[...]
    vec = vec_ref[...]
    row1 = lax.broadcasted_iota(jnp.int32, (chunk, LANES), 0)
    rr = lax.broadcasted_iota(jnp.int32, (chunk, chunk), 0)
    cc = lax.broadcasted_iota(jnp.int32, (chunk, chunk), 1)
    tri = rr >= cc
    tri_bf = tri.astype(BF16)
    eye_bf = _eye(LANES, BF16)

    acts, betas, decays, decay_ts, e_decs = [], [], [], [], []
    for b in range(bblk):
        xp_s[b, SUBLANES:SUBLANES + chunk, :] = qkv_ref[b]
        y = xp_s[b, SUBLANES:SUBLANES + chunk, :] * wc[GDN_CONV - 1:GDN_CONV, :]
        for j in range(GDN_CONV - 1):
            lo = SUBLANES - (GDN_CONV - 1) + j
            y = y + xp_s[b, lo:lo + chunk, :] * wc[j:j + 1, :]
        xp_out_ref[b] = xp_s[b]
        xp_s[b, 0:SUBLANES, :] = xp_s[b, chunk:chunk + SUBLANES, :]
        act = _silu(y)
        small = sm_ref[b]
        z = small + vec[1:2, :]
        softplus = jnp.maximum(z, 0.0) + jnp.log1p(jnp.exp(-jnp.abs(z)))
        g_all = -jnp.exp(vec[0:1, :]) * softplus
        if n_valid < chunk:
            valid = row1 < n_valid
            act = act * _tile_lanes(valid.astype(F32), GDN_CONV_CH // LANES)
            g_all = jnp.where(valid, g_all, 0.0)
        acts.append(act)
        betas.append(_sigmoid(small))
        decays.append(_mm_exact_lhs(tri_bf, g_all))
    for b in range(bblk):
        decay_ts.append(_mm_exact_lhs(eye_bf, decays[b], _NT))
        e_decs.append(jnp.exp(decays[b]))

    chains = [(b, h) for b in range(bblk) for h in range(GDN_HEADS)]
    qs, ks, kbs, dmasks, rhs_u, rhs_w, qds, kds, gls = [], [], [], [], [], [], [], [], []
    for b, h in chains:
        act = acts[b]
        qh = act[:, h * GDN_DK:(h + 1) * GDN_DK]
        kh = act[:, GDN_WIDTH + h * GDN_DK:GDN_WIDTH + (h + 1) * GDN_DK]
        vh = act[:, 2 * GDN_WIDTH + h * GDN_DV:2 * GDN_WIDTH + (h + 1) * GDN_DV]
        qh = qh * lax.rsqrt(jnp.sum(qh * qh, axis=-1, keepdims=True) + NORM_EPS) * (GDN_DK ** -0.5)
        kh = kh * lax.rsqrt(jnp.sum(kh * kh, axis=-1, keepdims=True) + NORM_EPS)
        beta = betas[b][:, SM_B + h:SM_B + h + 1]
        dcol = decays[b][:, SM_A + h:SM_A + h + 1]
        drow = decay_ts[b][SM_A + h:SM_A + h + 1, :]
        ed = e_decs[b][:, SM_A + h:SM_A + h + 1]
        dlast = decays[b][chunk - 1:chunk, SM_A + h:SM_A + h + 1]
        kb = kh * beta
        qs.append(qh)
        ks.append(kh)
        kbs.append(kb)
        dmasks.append(jnp.where(tri, jnp.exp(jnp.where(tri, dcol - drow, 0.0)), 0.0))
        rhs_u.append(vh * beta)
        rhs_w.append(kb * ed)
        qds.append(qh * ed)
        kds.append(kh * jnp.exp(dlast - dcol))
        gls.append(jnp.exp(dlast))

    n = len(chains)
    eye_c = _eye(chunk, F32)
    kkts = [_mm(kbs[i], ks[i], _NT, hi=hi) for i in range(n)]
    qks = [_mm(qs[i], ks[i], _NT, hi=hi) for i in range(n)]
    powers = [jnp.where(rr > cc, kkts[i] * dmasks[i], 0.0) for i in range(n)]
    qks = [qks[i] * dmasks[i] for i in range(n)]
    invs = [eye_c - powers[i] for i in range(n)]
    span = 2
    while span < n_valid:
        powers = [_mm(powers[i], powers[i], hi=hi) for i in range(n)]
        invs = [invs[i] + _mm(invs[i], powers[i], hi=hi) for i in range(n)]
        span *= 2
    us = [_mm(invs[i], rhs_u[i], hi=hi) for i in range(n)]
    ws = [_mm(invs[i], rhs_w[i], hi=hi) for i in range(n)]
    sts = [st_ref[b, h] for b, h in chains]
    v_news = [us[i] - _mm(ws[i], sts[i], hi=hi) for i in range(n)]
    os_ = [_mm(qds[i], sts[i], hi=hi) for i in range(n)]
    os_ = [os_[i] + _mm(qks[i], v_news[i], hi=hi) for i in range(n)]
    upd = [_mm(kds[i], v_news[i], _TN, hi=hi) for i in range(n)]
    for i, (b, h) in enumerate(chains):
        st_ref[b, h] = sts[i] * gls[i] + upd[i]
        o = os_[i]
        o = o * lax.rsqrt(jnp.mean(o * o, axis=-1, keepdims=True) + NORM_EPS) * wg_ref[...]
        o = o * _silu(zg_ref[b, :, h * GDN_DV:(h + 1) * GDN_DV])
        go_ref[b, :, h * GDN_DV:(h + 1) * GDN_DV] = o.astype(go_ref.dtype)


def _gdn(qkv, sm, zg, conv0, s0, w_conv, a_log, dt_bias, w_gnorm, batch, rows, bblk, chunk, n_valid, hi,
         out_dtype):
    nc = rows // chunk
    tile = lambda b, c: (b, c, 0)
    per_b3 = lambda b, c: (b, 0, 0)
    per_b4 = lambda b, c: (b, 0, 0, 0)
    fix2 = lambda b, c: (0, 0)
    wc = jnp.concatenate([w_conv, jnp.zeros((SUBLANES - GDN_CONV, GDN_CONV_CH), w_conv.dtype)], axis=0)
    vec = jnp.zeros((SUBLANES, LANES), F32)
    vec = vec.at[0, SM_A:SM_A + GDN_HEADS].set(a_log).at[1, SM_A:SM_A + GDN_HEADS].set(dt_bias)
    go, xp, st = pl.pallas_call(
        functools.partial(_gdn_kernel, bblk=bblk, chunk=chunk, n_valid=n_valid, hi=hi),
        grid=(batch // bblk, nc),
        in_specs=[pl.BlockSpec((bblk, chunk, GDN_CONV_CH), tile),
                  pl.BlockSpec((bblk, chunk, LANES), tile),
                  pl.BlockSpec((bblk, chunk, GDN_WIDTH), tile),
                  pl.BlockSpec((bblk, SUBLANES, GDN_CONV_CH), per_b3),
                  pl.BlockSpec((bblk, GDN_HEADS, GDN_DK, GDN_DV), per_b4),
                  pl.BlockSpec((SUBLANES, GDN_CONV_CH), fix2),
                  pl.BlockSpec((SUBLANES, LANES), fix2),
                  pl.BlockSpec((1, GDN_DV), fix2)],
        out_specs=[pl.BlockSpec((bblk, chunk, GDN_WIDTH), tile),
                   pl.BlockSpec((bblk, SUBLANES + chunk, GDN_CONV_CH), per_b3),
                   pl.BlockSpec((bblk, GDN_HEADS, GDN_DK, GDN_DV), per_b4)],
        out_shape=[jax.ShapeDtypeStruct((batch, rows, GDN_WIDTH), out_dtype),
                   jax.ShapeDtypeStruct((batch, SUBLANES + chunk, GDN_CONV_CH), F32),
                   jax.ShapeDtypeStruct((batch, GDN_HEADS, GDN_DK, GDN_DV), F32)],
        scratch_shapes=[pltpu.VMEM((bblk, SUBLANES + chunk, GDN_CONV_CH), F32)],
        compiler_params=pltpu.CompilerParams(dimension_semantics=("arbitrary", "arbitrary"),
                                             vmem_limit_bytes=VMEM_LIMIT_BYTES),
        name="gdn",
    )(qkv.reshape(batch, rows, GDN_CONV_CH), sm.reshape(batch, rows, LANES), zg.reshape(batch, rows, GDN_WIDTH),
      conv0, s0, wc, vec, w_gnorm.reshape(1, GDN_DV))
    return go.reshape(batch * rows, GDN_WIDTH), xp, st


PROMPT_TM = 512
SAMPLE_TM = 256
GDN_PROMPT_BBLK = 8
GDN_SAMPLE_BBLK = 8
NSA_SAMPLE_NB = 2


def _layer_prompt(h, lw, final, w_final, win_buf):
    w_norm, w_pack, cmp_k_w, cmp_v_w, w_conv, a_log, dt_bias, w_gnorm, w_out = lw
    batch, seq, d = h.shape
    x2d = h.reshape(batch * seq, d)
    tables = _rope_tables(jnp.arange(seq, dtype=jnp.int32))
    (q, qr, ck, cv, sk, sv, wk, wv, kvb, zn, qkv, zg, sm, ck_t, cv_t, sk_t, sv_t, wk_t, wv_t) = _project(
        x2d, w_norm, w_pack.astype(BF16), tables, seq, PROMPT_TM, False, BF16, True)
    nsa = _nsa_prompt(q, qr, kvb, ck, cv, sm, zn, cmp_k_w, cmp_v_w, batch, seq)
    conv0 = jnp.zeros((batch, SUBLANES, GDN_CONV_CH), F32)
    s0 = jnp.zeros((batch, GDN_HEADS, GDN_DK, GDN_DV), F32)
    go, xp, st = _gdn(qkv, sm, zg, conv0, s0, w_conv, a_log, dt_bias, w_gnorm,
                      batch, seq, GDN_PROMPT_BBLK, GDN_CHUNK, GDN_CHUNK, False, BF16)
    y = _out_project(x2d, nsa, go, w_out.astype(BF16), w_final, PROMPT_TM, final)
    conv_new = xp[:, SUBLANES + GDN_CHUNK - (GDN_CONV - 1):SUBLANES + GDN_CHUNK]
    from_t = lambda t: jnp.transpose(t.reshape(batch, KV_HEADS, HEAD_DIM, t.shape[-1]), (0, 3, 1, 2))
    lead = ((0, 0), (0, 0), (max(win_buf - seq, 0), 0))
    win = lambda t: from_t(jnp.pad(t, lead)[:, :, -win_buf:])
    return y.reshape(batch, seq, d), (from_t(ck_t), from_t(cv_t), from_t(sk_t), from_t(sv_t), win(wk_t), win(wv_t),
                                      conv_new, st)


def _layer_sample(h8, n_new, caches, page_table, lw, final, w_final):
    w_norm, w_pack, cmp_k_w, cmp_v_w, w_conv, a_log, dt_bias, w_gnorm, w_out = lw
    pe_rows = lambda pe2: jnp.pad(pe2, ((0, LANES - pe2.shape[0]), (0, 0))).astype(BF16)
    cmp_k_t = (cmp_k_w[0].T, cmp_k_w[1].T, pe_rows(cmp_k_w[2]))
    cmp_v_t = (cmp_v_w[0].T, cmp_v_w[1].T, pe_rows(cmp_v_w[2]))
    c_cmp_k, c_cmp_v, c_slc_k, c_slc_v, c_win_k, c_win_v, s_conv, s_gdn = caches
    batch, nr, d = h8.shape
    past = page_table.shape[1] * c_cmp_k.shape[1]
    x2d = h8.reshape(batch * nr, d)
    tables = _rope_tables(past + jnp.arange(nr, dtype=jnp.int32))
    tables = tuple(jnp.tile(t, (SAMPLE_TM // nr, 1)) for t in tables)
    (q, qr, ck, cv, sk, sv, wk, wv, _, zn, qkv, zg, sm) = _project(
        x2d, w_norm, w_pack, tables, SAMPLE_TM, SAMPLE_TM, True, F32, False)
    nsa, win_k_t, win_v_t = _nsa_sample_t(q, qr, sk, sv, wk, wv, sm, zn, c_win_k, c_win_v, c_cmp_k, c_cmp_v,
                                          c_slc_k, c_slc_v, page_table, cmp_k_t, cmp_v_t, n_new, NSA_SAMPLE_NB)
    conv0 = jnp.pad(s_conv, ((0, 0), (SUBLANES - (GDN_CONV - 1), 0), (0, 0)))
    go, xp, st = _gdn(qkv, sm, zg, conv0, s_gdn, w_conv, a_log, dt_bias, w_gnorm,
                      batch, nr, GDN_SAMPLE_BBLK, nr, n_new, True, F32)
    y = _out_project(x2d, nsa, go, w_out.astype(BF16), w_final, SAMPLE_TM, final)
    kv4 = lambda t: t.reshape(batch, nr, KV_HEADS, HEAD_DIM)[:, :n_new]
    from_t = lambda t: jnp.transpose(t.reshape(batch, KV_HEADS, HEAD_DIM, t.shape[-1]), (0, 3, 1, 2))
    conv_new = xp[:, SUBLANES + n_new - (GDN_CONV - 1):SUBLANES + n_new]
    return y.reshape(batch, nr, d), (kv4(ck), kv4(cv), kv4(sk), kv4(sv), from_t(win_k_t), from_t(win_v_t),
                                     conv_new, st)


def kernel(x_prompt, x_sample, cache_cmp_k, cache_cmp_v, cache_slc_k, cache_slc_v, cache_win_k, cache_win_v, state_conv, state_gdn, page_table, w_norm, w_in, pe_cmp_k, w_cmp_k1, w_cmp_k2, pe_cmp_v, w_cmp_v1, w_cmp_v2, w_conv, a_log, dt_bias, w_gdn_norm, w_out, w_final_norm):
    depth = w_in.shape[0]
    n_new = x_sample.shape[1]
    win_buf = cache_win_k.shape[2]
    h_p = x_prompt
    h_s = jnp.pad(x_sample, ((0, 0), (0, SAMPLE_ROWS - n_new), (0, 0)))
    st_p, st_s = [], []
    for layer in range(depth):
        lw = (w_norm[layer], _pack_w_in(w_in[layer]),
              _compress_weights(pe_cmp_k[layer], w_cmp_k1[layer], w_cmp_k2[layer]),
              _compress_weights(pe_cmp_v[layer], w_cmp_v1[layer], w_cmp_v2[layer]),
              w_conv[layer], a_log[layer], dt_bias[layer], w_gdn_norm[layer], w_out[layer])
        final = layer == depth - 1
        h_p, sp = _layer_prompt(h_p, lw, final, w_final_norm, win_buf)
        caches = (cache_cmp_k[layer], cache_cmp_v[layer], cache_slc_k[layer], cache_slc_v[layer],
                  cache_win_k[layer], cache_win_v[layer], state_conv[layer], state_gdn[layer])
        h_s, ss = _layer_sample(h_s, n_new, caches, page_table, lw, final, w_final_norm)
        st_p.append(sp)
        st_s.append(ss)
    outs = [h_p, h_s[:, :n_new]]
    for i in range(8):
        outs.append(jnp.stack([s[i] for s in st_p]))
        outs.append(jnp.stack([s[i] for s in st_s]))
    return tuple(outs)
```

```python
import functools

import numpy as np
import jax
import jax.numpy as jnp
from jax import lax
from jax.experimental import pallas as pl
from jax.experimental.pallas import tpu as pltpu

F32 = jnp.float32
BF16 = jnp.bfloat16

D_MODEL = 1024
HEAD_DIM = 64
Q_HEADS = 8
KV_HEADS = 2
GROUP = Q_HEADS // KV_HEADS
NSA_WIDTH = Q_HEADS * HEAD_DIM
KV_WIDTH = KV_HEADS * HEAD_DIM
CMP_LEN = 32
CMP_STRIDE = 16
CMP_HIDDEN = 128
SEL_BLOCK = 64
TOP_N = 8
WINDOW = 512
FORCE_BONUS = 1.0e4
ROT_DIM = HEAD_DIM // 4
ROPE_THETA = 500000.0
GDN_DK = 128
GDN_DV = 128
GDN_HEADS = 4
GDN_WIDTH = GDN_HEADS * GDN_DV
GDN_CONV = 4
GDN_CONV_CH = 3 * GDN_WIDTH
GDN_CHUNK = 64
NORM_EPS = 1e-6
MASK_VALUE = -1e30

LANES = 128
SUBLANES = 8
VMEM_LIMIT_BYTES = 56 * 1024 * 1024

QPAD_WIDTH = Q_HEADS * LANES
C_Q = 0
C_KV = C_Q + QPAD_WIDTH
C_ZN = C_KV + 6 * KV_WIDTH
C_QKV = C_ZN + NSA_WIDTH
C_ZG = C_QKV + GDN_CONV_CH
C_SM = C_ZG + GDN_WIDTH
N_PACK = C_SM + LANES
SM_B = 3 * Q_HEADS
SM_A = SM_B + GDN_HEADS

TQ = 128
CK_SLC = 256
CK_WIN = 128
N_SUB = 128
N_CMP = N_SUB - CMP_LEN // CMP_STRIDE + 1
SAMPLE_ROWS = 8
KVB_SK = 0
KVB_SV = KVB_SK + KV_WIDTH
KVB_WK = KVB_SV + 2 * KV_WIDTH
KVB_WV = KVB_WK + KV_WIDTH
KVB_WIDTH = KVB_WV + 2 * KV_WIDTH
KVB_COLS = (KVB_SK, KVB_SV, KVB_WK, KVB_WV)
WIN_SPAN = WINDOW + TQ


def _pack_w_in(w_in):
    o_gate = NSA_WIDTH + 6 * KV_WIDTH
    o_zn = o_gate + 3 * Q_HEADS
    o_qkv = o_zn + NSA_WIDTH
    o_b = o_qkv + GDN_CONV_CH
    o_a = o_b + GDN_HEADS
    o_zg = o_a + GDN_HEADS
    d = w_in.shape[0]
    z64 = jnp.zeros((d, HEAD_DIM), w_in.dtype)
    qcols = []
    for hq in range(Q_HEADS):
        wq = w_in[:, hq * HEAD_DIM:(hq + 1) * HEAD_DIM]
        qcols += [wq, z64] if hq // GROUP == 0 else [z64, wq]
    pad = jnp.zeros((d, LANES - SM_A - GDN_HEADS), w_in.dtype)
    return jnp.concatenate(
        qcols + [w_in[:, NSA_WIDTH:o_gate], w_in[:, o_zn:o_qkv], w_in[:, o_qkv:o_b], w_in[:, o_zg:],
                 w_in[:, o_gate:o_zn], w_in[:, o_b:o_a], w_in[:, o_a:o_zg], pad], axis=1)


def _rope_tables(pos):
    half = ROT_DIM // 2
    inv = ROPE_THETA ** (-(jnp.arange(half, dtype=F32) * 2.0 / ROT_DIM))
    ang = pos.astype(F32)[:, None] * inv[None, :]
    cos, sin = jnp.cos(ang), jnp.sin(ang)
    n = pos.shape[0]
    one = jnp.ones((n, HEAD_DIM - ROT_DIM), F32)
    zero = jnp.zeros((n, HEAD_DIM - ROT_DIM), F32)
    zh = jnp.zeros((n, half), F32)
    c64 = jnp.concatenate([cos, cos, one], axis=1)
    a64 = jnp.concatenate([zh, sin, zero], axis=1)
    b64 = jnp.concatenate([-sin, zh, zero], axis=1)
    tile = lambda t: jnp.concatenate([t, t], axis=1)
    return tile(c64), tile(a64), tile(b64)


def _rope128(x, c, a, b):
    half = ROT_DIM // 2
    return x * c + pltpu.roll(x, half, 1) * a + pltpu.roll(x, LANES - half, 1) * b


_NN = (((1,), (0,)), ((), ()))
_NT = (((1,), (1,)), ((), ()))
_TN = (((0,), (0,)), ((), ()))


def _split_bf16(x):
    hi = x.astype(BF16)
    return hi, (x - hi.astype(F32)).astype(BF16)


def _mm(a, b, dims=_NN, hi=False):
    dot = lambda x, y: lax.dot_general(x, y, dims, preferred_element_type=F32)
    if hi:
        a_hi, a_lo = _split_bf16(a.astype(F32))
        b_hi, b_lo = _split_bf16(b.astype(F32))
        return dot(a_hi, b_hi) + (dot(a_lo, b_hi) + dot(a_hi, b_lo))
    return dot(a.astype(BF16), b.astype(BF16))


def _mm_exact_lhs(a_bf16, x, dims=_NN):
    x1 = x.astype(BF16)
    r1 = x - x1.astype(F32)
    x2 = r1.astype(BF16)
    x3 = (r1 - x2.astype(F32)).astype(BF16)
    dot = lambda t: lax.dot_general(a_bf16, t, dims, preferred_element_type=F32)
    return dot(x1) + dot(x2) + dot(x3)


def _sigmoid(x):
    return 1.0 / (1.0 + jnp.exp(-x))


def _silu(x):
    return x * _sigmoid(x)


def _tile_rows(x, n):
    return jnp.concatenate([x] * n, axis=0)


def _tile_lanes(x, n):
    return x if n == 1 else jnp.concatenate([x] * n, axis=1)


def _eye(n, dtype):
    r = lax.broadcasted_iota(jnp.int32, (n, n), 0)
    c = lax.broadcasted_iota(jnp.int32, (n, n), 1)
    return (r == c).astype(dtype)


def _proj_kernel(x_ref, wn_ref, w_ref, c_ref, a_ref, b_ref,
                 q_ref, qr_ref, ck_ref, cv_ref, sk_ref, sv_ref, wk_ref, wv_ref, kvb_ref,
                 zn_ref, qkv_ref, zg_ref, sm_ref, *t_refs, hi_gdn):
    x = x_ref[...]
    ms = jnp.mean(x * x, axis=-1, keepdims=True)
    xn = x * lax.rsqrt(ms + NORM_EPS) * wn_ref[...]
    xb = xn.astype(BF16)
    c, a, b = c_ref[...], a_ref[...], b_ref[...]
    scale = HEAD_DIM ** -0.5
    for j in range(0, Q_HEADS, 2):
        q2 = _mm(xb, w_ref[:, C_Q + j * LANES:C_Q + (j + 2) * LANES])
        for jj in range(2):
            qj = q2[:, jj * LANES:(jj + 1) * LANES]
            lo = (j + jj) * LANES
            q_ref[:, lo:lo + LANES] = (qj * scale).astype(q_ref.dtype)
            qr_ref[:, lo:lo + LANES] = (_rope128(qj, c, a, b) * scale).astype(qr_ref.dtype)
    kv_refs = (ck_ref, cv_ref, sk_ref, sv_ref, wk_ref, wv_ref)
    kv2 = [_mm(xb, w_ref[:, C_KV + j * LANES:C_KV + (j + 2) * LANES]) for j in range(0, 6, 2)]
    for j in range(6):
        kj = kv2[j // 2][:, (j % 2) * LANES:(j % 2 + 1) * LANES]
        if j in (2, 4):
            kj = _rope128(kj, c, a, b)
        kv_refs[j][...] = kj.astype(kv_refs[j].dtype)
        if j >= 2:
            lo = KVB_COLS[j - 2]
            kvb_ref[:, lo:lo + LANES] = kj.astype(BF16)
        if t_refs:
            t_refs[j][0] = kj.T
    ones = jnp.ones((x.shape[0], LANES), BF16)
    kvb_ref[:, KVB_SV + LANES:KVB_SV + 2 * LANES] = ones
    kvb_ref[:, KVB_WV + LANES:KVB_WV + 2 * LANES] = ones
    zn_ref[...] = _mm(xb, w_ref[:, C_ZN:C_QKV]).astype(zn_ref.dtype)
    xg = xn if hi_gdn else xb
    for j in range(3):
        lo = C_QKV + j * GDN_WIDTH
        qkv = _mm(xg, w_ref[:, lo:lo + GDN_WIDTH], hi=hi_gdn)
        qkv_ref[:, j * GDN_WIDTH:(j + 1) * GDN_WIDTH] = qkv.astype(qkv_ref.dtype)
    zg_ref[...] = _mm(xb, w_ref[:, C_ZG:C_SM]).astype(zg_ref.dtype)
    sm_ref[...] = _mm(xg, w_ref[:, C_SM:N_PACK], hi=hi_gdn)


def _project(x2d, w_norm, w_pack, tables, rows_per_seq, tm, hi_gdn, act_dtype, emit_t):
    n = x2d.shape[0]
    nt = rows_per_seq // tm
    row = lambda i: (i, 0)
    tab = lambda i: (i % nt, 0)
    fix = lambda i: (0, 0)
    widths = (QPAD_WIDTH, QPAD_WIDTH) + (KV_WIDTH,) * 6 + (KVB_WIDTH, NSA_WIDTH, GDN_CONV_CH, GDN_WIDTH, LANES)
    dtypes = (act_dtype,) * 8 + (BF16, act_dtype, act_dtype, act_dtype, F32)
    out_specs = [pl.BlockSpec((tm, w), row) for w in widths]
    out_shape = [jax.ShapeDtypeStruct((n, w), d) for w, d in zip(widths, dtypes)]
    if emit_t:
        out_specs += [pl.BlockSpec((1, KV_WIDTH, tm), lambda i: (i // nt, 0, i % nt))] * 6
        out_shape += [jax.ShapeDtypeStruct((n // rows_per_seq, KV_WIDTH, rows_per_seq), F32)] * 6
    return pl.pallas_call(
        functools.partial(_proj_kernel, hi_gdn=hi_gdn),
        grid=(n // tm,),
        in_specs=[pl.BlockSpec((tm, D_MODEL), row),
                  pl.BlockSpec((1, D_MODEL), fix),
                  pl.BlockSpec((D_MODEL, N_PACK), fix),
                  pl.BlockSpec((tm, LANES), tab),
                  pl.BlockSpec((tm, LANES), tab),
                  pl.BlockSpec((tm, LANES), tab)],
        out_specs=out_specs,
        out_shape=out_shape,
        compiler_params=pltpu.CompilerParams(dimension_semantics=("arbitrary",),
                                             vmem_limit_bytes=VMEM_LIMIT_BYTES),
        name="in_proj",
    )(x2d, w_norm.reshape(1, D_MODEL), w_pack, *tables)


def _out_kernel(x_ref, nsa_ref, gdn_ref, w_ref, wf_ref, y_ref, *, final):
    mix = jnp.concatenate([nsa_ref[...].astype(BF16), gdn_ref[...].astype(BF16)], axis=1)
    h = x_ref[...] + _mm(mix, w_ref[...])
    if final:
        ms = jnp.mean(h * h, axis=-1, keepdims=True)
        h = h * lax.rsqrt(ms + NORM_EPS) * wf_ref[...]
    y_ref[...] = h


def _out_project(x2d, nsa, gdn, w_out, w_final, tm, final):
    n = x2d.shape[0]
    row = lambda i: (i, 0)
    fix = lambda i: (0, 0)
    return pl.pallas_call(
        functools.partial(_out_kernel, final=final),
        grid=(n // tm,),
        in_specs=[pl.BlockSpec((tm, D_MODEL), row),
                  pl.BlockSpec((tm, NSA_WIDTH), row),
                  pl.BlockSpec((tm, GDN_WIDTH), row),
                  pl.BlockSpec((D_MODEL, D_MODEL), fix),
                  pl.BlockSpec((1, D_MODEL), fix)],
        out_specs=pl.BlockSpec((tm, D_MODEL), row),
        out_shape=jax.ShapeDtypeStruct((n, D_MODEL), F32),
        compiler_params=pltpu.CompilerParams(dimension_semantics=("arbitrary",),
                                             vmem_limit_bytes=VMEM_LIMIT_BYTES),
        name="out_proj",
    )(x2d, nsa, gdn, w_out, w_final.reshape(1, D_MODEL))


def _compress_weights(pe, w1, w2):
    half = CMP_STRIDE * HEAD_DIM
    z = jnp.zeros((CMP_STRIDE, HEAD_DIM, CMP_HIDDEN), w1.dtype)

    def place(wpart, h):
        wp = wpart.reshape(CMP_STRIDE, HEAD_DIM, CMP_HIDDEN)
        parts = [wp, z] if h == 0 else [z, wp]
        return jnp.stack(parts, axis=1).reshape(CMP_STRIDE * KV_WIDTH, CMP_HIDDEN)

    w1big = jnp.concatenate([place(w1[:half], 0), place(w1[:half], 1),
                             place(w1[half:], 0), place(w1[half:], 1)], axis=1)
    zz = jnp.zeros_like(w2)
    w2big = jnp.concatenate([jnp.concatenate([w2, zz], axis=1), jnp.concatenate([zz, w2], axis=1)], axis=0)
    pe_a = jnp.tile(pe[:CMP_STRIDE], (1, KV_HEADS)).reshape(1, CMP_STRIDE * KV_WIDTH)
    pe_b = jnp.tile(pe[CMP_STRIDE:], (1, KV_HEADS)).reshape(1, CMP_STRIDE * KV_WIDTH)
    pe2 = jnp.concatenate([pe_a, pe_b, jnp.zeros((SUBLANES - 2, CMP_STRIDE * KV_WIDTH), pe.dtype)], axis=0)
    return w1big.astype(BF16), w2big.astype(BF16), pe2


def _compress(sub_rows, w1_ref, w2_ref, pe_ref):
    hid = 2 * CMP_HIDDEN
    w1 = w1_ref[...]
    ab = _mm(sub_rows, w1)
    pe = pe_ref[...]
    pe_hi = pe.astype(BF16)
    pe_lo = (pe - pe_hi.astype(F32)).astype(BF16)
    r = _mm(pe_hi, w1) + _mm(pe_lo, w1)
    bias = r[0:1, :hid] + r[1:2, hid:]
    h = ab[:, :hid] + pltpu.roll(ab[:, hid:], N_SUB - 1, 0) + bias
    return _mm(_silu(h), w2_ref[...])


def _overlap_t(ns):
    c0 = np.arange(N_CMP)[None, :] * CMP_STRIDE
    b0 = np.arange(ns)[:, None] * SEL_BLOCK
    ov = np.minimum(c0 + CMP_LEN, b0 + SEL_BLOCK) - np.maximum(c0, b0)
    out = np.zeros((LANES, LANES), np.float32)
    out[:ns, :N_CMP] = np.maximum(ov, 0) / CMP_LEN
    return jnp.asarray(out, BF16)


def _cmp_attention(q_stack, ck, cv, cvalid, groups):
    s = _mm(q_stack, ck, _NT)
    cm = _tile_rows(cvalid, groups)
    s = jnp.where(cm, s, MASK_VALUE)
    m = jnp.max(s, axis=1, keepdims=True)
    e = jnp.exp(s - m)
    p = e / jnp.sum(e, axis=1, keepdims=True) * cm.astype(F32)
    return _mm(p, cv), p


def _select_blocks(imp_t, n_idx, q_blk, n_rows):
    forced = (n_idx == 0) | (n_idx == q_blk) | (n_idx == q_blk - 1)
    allowed = n_idx <= q_blk
    v = jnp.where(allowed, imp_t + FORCE_BONUS * forced.astype(F32), MASK_VALUE)
    rank = jnp.zeros(v.shape, F32)
    for j in range(n_rows):
        vj = v[j:j + 1, :]
        ge = jnp.where(vj >= v, 1.0, 0.0)
        gt = jnp.where(vj > v, 1.0, 0.0)
        rank = rank + jnp.where(n_idx > j, ge, gt)
    return ((rank < TOP_N) & allowed).astype(F32)


def _flash_init(m_s, l_s, acc_s):
    m_s[...] = jnp.full(m_s.shape, MASK_VALUE, F32)
    l_s[...] = jnp.zeros(l_s.shape, F32)
    acc_s[...] = jnp.zeros(acc_s.shape, F32)


def _flash_step(s, v_c, m_s, l_s, acc_s):
    m_prev = m_s[...]
    m_next = jnp.maximum(m_prev, jnp.max(s, axis=1, keepdims=True))
    alpha = jnp.exp(m_prev - m_next)
    p = jnp.exp(s - _tile_lanes(m_next, s.shape[1] // LANES))
    l_s[...] = alpha * l_s[...] + jnp.sum(p, axis=1, keepdims=True)
    acc_s[...] = acc_s[...] * alpha + _mm(p, v_c)
    m_s[...] = m_next


def _merge_heads(o_sum, kvh, tq):
    lane = lax.broadcasted_iota(jnp.int32, (tq, LANES), 1)
    slabs = []
    for pair in range(GROUP // 2):
        halves = []
        for par in range(2):
            o = o_sum[(2 * pair + par) * tq:(2 * pair + par + 1) * tq]
            halves.append(o if par == kvh else pltpu.roll(o, HEAD_DIM, 1))
        slabs.append(jnp.where(lane < HEAD_DIM, halves[0], halves[1]))
    return slabs


def _nsa_prompt_kernel(q_ref, qr_ref, kvb_ref, ckr_ref, cvr_ref, sm_ref, zn_ref,
                       wk1_ref, wk2_ref, pek_ref, wv1_ref, wv2_ref, pev_ref, ovt_ref, e_ref,
                       o_ref, ck_s, cv_s, s_s, mrun_s, m_s, acc_s):
    i = pl.program_id(1)

    @pl.when(i == 0)
    def _():
        ck_s[...] = _compress(ckr_ref[0], wk1_ref, wk2_ref, pek_ref).astype(BF16)
        cv_s[...] = _compress(cvr_ref[0], wv1_ref, wv2_ref, pev_ref).astype(BF16)

    kvs = range(KV_HEADS)
    t0 = i * TQ
    rows = GROUP * TQ
    row = lax.broadcasted_iota(jnp.int32, (TQ, LANES), 0)
    col = lax.broadcasted_iota(jnp.int32, (TQ, LANES), 1)
    cvalid = (CMP_STRIDE * col + (CMP_LEN - 1) <= t0 + row) & (col < N_CMP)
    cm = _tile_rows(cvalid, GROUP)
    cm_f = cm.astype(F32)
    sig = _sigmoid(sm_ref[...])

    def gate(branch, kvh):
        cols = [sig[:, branch * Q_HEADS + h:branch * Q_HEADS + h + 1] for h in range(kvh * GROUP, (kvh + 1) * GROUP)]
        return jnp.concatenate([jnp.broadcast_to(c, (TQ, LANES)) for c in cols], axis=0)

    def stack(ref, kvh):
        return jnp.concatenate([ref[:, h * LANES:(h + 1) * LANES] for h in range(kvh * GROUP, (kvh + 1) * GROUP)], axis=0)

    q_raw = [stack(q_ref, k) for k in kvs]
    q_rot = [stack(qr_ref, k) for k in kvs]

    ck, cv = ck_s[...], cv_s[...]
    s = [jnp.where(cm, _mm(q_raw[k], ck, _NT), MASK_VALUE) for k in kvs]
    e = [jnp.exp(s[k] - jnp.max(s[k], axis=1, keepdims=True)) for k in kvs]
    p = [e[k] / jnp.sum(e[k], axis=1, keepdims=True) * cm_f for k in kvs]
    o_sum = [gate(0, k) * _mm(p[k], cv) for k in kvs]
    p_sum = jnp.concatenate([p[k][0:TQ] + p[k][TQ:2 * TQ] + p[k][2 * TQ:3 * TQ] + p[k][3 * TQ:4 * TQ] for k in kvs],
                            axis=0)
    ns_rows = 32
    nq = KV_HEADS * TQ
    imp_t = _mm_exact_lhs(ovt_ref[...], p_sum, _NT)[0:ns_rows]
    n_idx = lax.broadcasted_iota(jnp.int32, (ns_rows, nq), 0)
    qb_t = (t0 + lax.broadcasted_iota(jnp.int32, (ns_rows, nq), 1) % TQ) // SEL_BLOCK
    sel_t = _select_blocks(imp_t, n_idx, qb_t, ns_rows)
    sel_t = jnp.concatenate([sel_t, jnp.zeros((LANES - ns_rows, nq), F32)], axis=0).astype(BF16)
    eye = _eye(TQ, BF16)
    sel = [_mm(eye, sel_t[:, k * TQ:(k + 1) * TQ], _NT).astype(BF16) for k in kvs]

    col_s = lax.broadcasted_iota(jnp.int32, (TQ, CK_SLC), 1)
    row_s = lax.broadcasted_iota(jnp.int32, (TQ, CK_SLC), 0)
    n_chunks = (t0 + TQ + CK_SLC - 1) // CK_SLC
    mrun_s[...] = jnp.full(mrun_s.shape, MASK_VALUE, F32)

    def score_chunks(js):
        k0 = [pl.multiple_of(j * CK_SLC, CK_SLC) for j in js]
        k_c = [kvb_ref[pl.ds(k0[n], CK_SLC), KVB_SK:KVB_SK + LANES] for n in range(len(js))]
        sc = [[_mm(q_rot[k], k_c[n], _NT) for k in kvs] for n in range(len(js))]
        picked = [[_mm(sel[k], e_ref[js[n]]) for k in kvs] for n in range(len(js))]
        for k in kvs:
            run = mrun_s[k]
            for n, j in enumerate(js):
                causal = k0[n] + col_s <= t0 + row_s
                bias = jnp.where((picked[n][k] > 0.5) & causal, 0.0, MASK_VALUE)
                sk = sc[n][k] + _tile_rows(bias, GROUP)
                s_s[k, j] = sk
                run = jnp.maximum(run, jnp.maximum(sk[:, :LANES], sk[:, LANES:]))
            mrun_s[k] = run

    def value_chunks(js):
        k0 = [pl.multiple_of(j * CK_SLC, CK_SLC) for j in js]
        v_c = [kvb_ref[pl.ds(k0[n], CK_SLC), KVB_SV:KVB_SV + 2 * LANES] for n in range(len(js))]
        pr = [[jnp.exp(s_s[k, js[n]] - _tile_lanes(m_s[k], CK_SLC // LANES)) for k in kvs] for n in range(len(js))]
        pv = [[_mm(pr[n][k], v_c[n]) for k in kvs] for n in range(len(js))]
        for k in kvs:
            acc_s[k] += sum(pv[n][k] for n in range(len(js)))

    def paired(fn):
        def pair_body(jj, carry):
            fn([2 * jj, 2 * jj + 1])
            return carry

        lax.fori_loop(0, n_chunks // 2, pair_body, 0)

        @pl.when(n_chunks % 2 == 1)
        def _():
            fn([n_chunks - 1])

    paired(score_chunks)
    for k in kvs:
        m_s[k] = jnp.broadcast_to(jnp.max(mrun_s[k], axis=1, keepdims=True), (rows, LANES))
    acc_s[...] = jnp.zeros(acc_s.shape, F32)

    paired(value_chunks)
    for k in kvs:
        acc = acc_s[k]
        o_sum[k] = o_sum[k] + gate(1, k) * (acc[:, :LANES] / acc[:, LANES:])

    ks0 = pl.multiple_of(jnp.maximum(t0 - WINDOW, 0), TQ)
    k_w = kvb_ref[pl.ds(ks0, WIN_SPAN), KVB_WK:KVB_WK + LANES]
    v_w = kvb_ref[pl.ds(ks0, WIN_SPAN), KVB_WV:KVB_WV + 2 * LANES]
    diff = (t0 + lax.broadcasted_iota(jnp.int32, (TQ, WIN_SPAN), 0)) - (ks0 + lax.broadcasted_iota(jnp.int32, (TQ, WIN_SPAN), 1))
    bias_w = _tile_rows(jnp.where((diff >= 0) & (diff <= WINDOW), 0.0, MASK_VALUE), GROUP)
    sw = [_mm(q_rot[k], k_w, _NT) + bias_w for k in kvs]
    pw = [jnp.exp(sw[k] - jnp.max(sw[k], axis=1, keepdims=True)) for k in kvs]
    rw = [_mm(pw[k], v_w) for k in kvs]
    for k in kvs:
        o_all = o_sum[k] + gate(2, k) * (rw[k][:, :LANES] / rw[k][:, LANES:])
        for pair, slab in enumerate(_merge_heads(o_all, k, TQ)):
            lo = (k * (GROUP // 2) + pair) * LANES
            o_ref[:, lo:lo + LANES] = (slab * _silu(zn_ref[:, lo:lo + LANES].astype(F32))).astype(o_ref.dtype)


def _nsa_prompt(q, qr, kvb, ck, cv, sm, zn, cmp_k_w, cmp_v_w, batch, seq):
    nt = seq // TQ
    ns = seq // SEL_BLOCK
    sub_w = CMP_STRIDE * KV_WIDTH
    keys = np.arange(seq)
    e = (np.arange(LANES)[None, :, None] == (keys // SEL_BLOCK).reshape(seq // CK_SLC, 1, CK_SLC))
    e = jnp.asarray(e, BF16)
    tile = lambda b, i: (b * nt + i, 0)
    per_b = lambda b, i: (b, 0)
    per_b3 = lambda b, i: (b, 0, 0)
    fix2 = lambda b, i: (0, 0)
    fix3 = lambda b, i: (0, 0, 0)
    wspecs = [pl.BlockSpec((sub_w, 4 * CMP_HIDDEN), fix2),
              pl.BlockSpec((2 * CMP_HIDDEN, KV_WIDTH), fix2),
              pl.BlockSpec((SUBLANES, sub_w), fix2)]
    rows = GROUP * TQ
    return pl.pallas_call(
        _nsa_prompt_kernel,
        grid=(batch, nt),
        in_specs=[pl.BlockSpec((TQ, QPAD_WIDTH), tile),
                  pl.BlockSpec((TQ, QPAD_WIDTH), tile),
                  pl.BlockSpec((seq, KVB_WIDTH), per_b),
                  pl.BlockSpec((1, seq // CMP_STRIDE, sub_w), per_b3),
                  pl.BlockSpec((1, seq // CMP_STRIDE, sub_w), per_b3),
                  pl.BlockSpec((TQ, LANES), tile),
                  pl.BlockSpec((TQ, NSA_WIDTH), tile)] + wspecs + wspecs + [
                  pl.BlockSpec((LANES, LANES), fix2),
                  pl.BlockSpec((seq // CK_SLC, LANES, CK_SLC), fix3)],
        out_specs=pl.BlockSpec((TQ, NSA_WIDTH), tile),
        out_shape=jax.ShapeDtypeStruct((batch * seq, NSA_WIDTH), BF16),
        scratch_shapes=[pltpu.VMEM((N_SUB, KV_WIDTH), BF16), pltpu.VMEM((N_SUB, KV_WIDTH), BF16),
                        pltpu.VMEM((KV_HEADS, seq // CK_SLC, rows, CK_SLC), F32),
                        pltpu.VMEM((KV_HEADS, rows, LANES), F32), pltpu.VMEM((KV_HEADS, rows, LANES), F32),
                        pltpu.VMEM((KV_HEADS, rows, 2 * LANES), F32)],
        compiler_params=pltpu.CompilerParams(dimension_semantics=("arbitrary", "arbitrary"),
                                             vmem_limit_bytes=VMEM_LIMIT_BYTES),
        name="nsa_prompt",
    )(q, qr, kvb, ck.reshape(batch, seq // CMP_STRIDE, sub_w), cv.reshape(batch, seq // CMP_STRIDE, sub_w),
      sm, zn, *cmp_k_w, *cmp_v_w, _overlap_t(ns), e)


def _softmax_rows(s):
    m = jnp.max(s, axis=1, keepdims=True)
    e = jnp.exp(s - m)
    return e / jnp.sum(e, axis=1, keepdims=True)


def _nsa_sample_kernel(pt_ref, q_ref, qr_ref, skn_ref, svn_ref, wkn_ref, wvn_ref, sm_ref, zn_ref,
                       wkc_ref, wvc_ref, *rest, n_pages, page, past, n_new, win_keys):
    del pt_ref
    cmpk, cmpv = rest[0:n_pages], rest[n_pages:2 * n_pages]
    slck, slcv = rest[2 * n_pages:3 * n_pages], rest[3 * n_pages:4 * n_pages]
    wk1_ref, wk2_ref, pek_ref, wv1_ref, wv2_ref, pev_ref, ovt_ref, e_ref = rest[4 * n_pages:4 * n_pages + 8]
    o_ref, wko_ref, wvo_ref = rest[4 * n_pages + 8:4 * n_pages + 11]
    kall, vall, kwin, vwin = rest[4 * n_pages + 11:]
    nr = SAMPLE_ROWS
    rows = Q_HEADS * nr
    all_keys = kall.shape[0]
    win_buf = wkc_ref.shape[1]
    zeros8 = jnp.zeros((nr, LANES), F32)

    for p in range(n_pages):
        kall[p * page:(p + 1) * page, :] = slck[p][0].astype(BF16)
        vall[p * page:(p + 1) * page, :] = slcv[p][0].astype(BF16)
    kall[past:past + 2 * nr, :] = jnp.concatenate([skn_ref[...], zeros8], axis=0).astype(BF16)
    vall[past:past + 2 * nr, :] = jnp.concatenate([svn_ref[...], zeros8], axis=0).astype(BF16)
    kall[past + 2 * nr:, :] = jnp.zeros((all_keys - past - 2 * nr, LANES), BF16)
    vall[past + 2 * nr:, :] = jnp.zeros((all_keys - past - 2 * nr, LANES), BF16)
    kwin[0:win_buf, :] = wkc_ref[0].astype(BF16)
    vwin[0:win_buf, :] = wvc_ref[0].astype(BF16)
    kwin[win_buf:win_buf + 2 * nr, :] = jnp.concatenate([wkn_ref[...], zeros8], axis=0).astype(BF16)
    vwin[win_buf:win_buf + 2 * nr, :] = jnp.concatenate([wvn_ref[...], zeros8], axis=0).astype(BF16)
    kwin[win_buf + 2 * nr:, :] = jnp.zeros((win_keys - win_buf - 2 * nr, LANES), BF16)
    vwin[win_buf + 2 * nr:, :] = jnp.zeros((win_keys - win_buf - 2 * nr, LANES), BF16)

    ck = _compress(jnp.concatenate([r[0] for r in cmpk], axis=0), wk1_ref, wk2_ref, pek_ref)
    cv = _compress(jnp.concatenate([r[0] for r in cmpv], axis=0), wv1_ref, wv2_ref, pev_ref)

    q_raw = jnp.concatenate([q_ref[:, h * LANES:(h + 1) * LANES] for h in range(Q_HEADS)], axis=0)
    q_rot = jnp.concatenate([qr_ref[:, h * LANES:(h + 1) * LANES] for h in range(Q_HEADS)], axis=0)

    tok = lax.broadcasted_iota(jnp.int32, (nr, LANES), 0)
    col = lax.broadcasted_iota(jnp.int32, (nr, LANES), 1)
    cvalid = (CMP_STRIDE * col + (CMP_LEN - 1) <= past + tok) & (col < N_CMP)
    o_cmp, p = _cmp_attention(q_raw, ck, cv, cvalid, Q_HEADS)

    p_sum = []
    for kvh in range(KV_HEADS):
        base = kvh * GROUP * nr
        p_sum.append(sum(p[base + g * nr:base + (g + 1) * nr] for g in range(GROUP)))
    p_sum = jnp.concatenate(p_sum, axis=0)
    ns = (past + n_new + SEL_BLOCK - 1) // SEL_BLOCK
    ns_rows = -(-ns // SUBLANES) * SUBLANES
    nq = KV_HEADS * nr
    imp_t = _mm_exact_lhs(ovt_ref[...], p_sum, _NT)[0:ns_rows]
    n_idx = lax.broadcasted_iota(jnp.int32, (ns_rows, nq), 0)
    qb_t = (past + lax.broadcasted_iota(jnp.int32, (ns_rows, nq), 1) % nr) // SEL_BLOCK
    sel_t = _select_blocks(imp_t, n_idx, qb_t, ns)
    sel_t = jnp.concatenate([sel_t, jnp.zeros((LANES - ns_rows, nq), F32)], axis=0)
    sel = _mm(_eye(nq, BF16), sel_t, _NT)
    sel_rows = jnp.concatenate([_tile_rows(sel[kvh * nr:(kvh + 1) * nr], GROUP) for kvh in range(KV_HEADS)], axis=0)

    picked = _mm(sel_rows, e_ref[...])
    q_pos = past + lax.broadcasted_iota(jnp.int32, (rows, all_keys), 0) % nr
    k_pos = lax.broadcasted_iota(jnp.int32, (rows, all_keys), 1)
    ok = (picked > 0.5) & (k_pos <= q_pos)
    s = _mm(q_rot, kall[...], _NT) + jnp.where(ok, 0.0, MASK_VALUE)
    o_slc = _mm(_softmax_rows(s), vall[...])

    q_pos = past + lax.broadcasted_iota(jnp.int32, (rows, win_keys), 0) % nr
    k_pos = past - win_buf + lax.broadcasted_iota(jnp.int32, (rows, win_keys), 1)
    diff = q_pos - k_pos
    ok = (diff >= 0) & (diff <= WINDOW)
    s = _mm(q_rot, kwin[...], _NT) + jnp.where(ok, 0.0, MASK_VALUE)
    o_win = _mm(_softmax_rows(s), vwin[...])

    sig = _sigmoid(sm_ref[...])

    def gate(branch):
        cols = [sig[:, branch * Q_HEADS + h:branch * Q_HEADS + h + 1] for h in range(Q_HEADS)]
        return jnp.concatenate([jnp.broadcast_to(c, (nr, LANES)) for c in cols], axis=0)

    o_all = gate(0) * o_cmp + gate(1) * o_slc + gate(2) * o_win
    for kvh in range(KV_HEADS):
        o_kv = o_all[kvh * GROUP * nr:(kvh + 1) * GROUP * nr]
        for pair, slab in enumerate(_merge_heads(o_kv, kvh, nr)):
            lo = (kvh * (GROUP // 2) + pair) * LANES
            o_ref[:, lo:lo + LANES] = (slab * _silu(zn_ref[:, lo:lo + LANES].astype(F32))).astype(o_ref.dtype)

    for cache_ref, new_ref, out_ref in ((wkc_ref, wkn_ref, wko_ref), (wvc_ref, wvn_ref, wvo_ref)):
        shifted = pltpu.roll(cache_ref[0], win_buf - n_new, 0)
        out_ref[0] = shifted
        tail = jnp.where(tok < nr - n_new, shifted[win_buf - nr:], pltpu.roll(new_ref[...], nr - n_new, 0))
        out_ref[0, win_buf - nr:, :] = tail


def _nsa_sample(q, qr, sk, sv, wk, wv, sm, zn, win_k, win_v, cmp_k, cmp_v, slc_k, slc_v, page_table,
                cmp_k_w, cmp_v_w, n_new):
    batch, n_pages = page_table.shape
    n_pool, page = cmp_k.shape[0], cmp_k.shape[1]
    past = n_pages * page
    nr = SAMPLE_ROWS
    sub_w = CMP_STRIDE * KV_WIDTH
    win_buf = win_k.shape[1]
    all_keys = -(-(past + 2 * nr) // LANES) * LANES
    win_keys = -(-(win_buf + 2 * nr) // LANES) * LANES
    ns = (past + n_new + SEL_BLOCK - 1) // SEL_BLOCK
    e = jnp.asarray(np.arange(LANES)[:, None] == (np.arange(all_keys) // SEL_BLOCK)[None, :], BF16)
    cmp_k = cmp_k.reshape(n_pool, page // CMP_STRIDE, sub_w)
    cmp_v = cmp_v.reshape(n_pool, page // CMP_STRIDE, sub_w)
    slc_k = slc_k.reshape(n_pool, page, KV_WIDTH)
    slc_v = slc_v.reshape(n_pool, page, KV_WIDTH)
    win_k = win_k.reshape(batch, win_buf, KV_WIDTH)
    win_v = win_v.reshape(batch, win_buf, KV_WIDTH)

    row = lambda b, pt: (b, 0)
    per_b3 = lambda b, pt: (b, 0, 0)
    fix2 = lambda b, pt: (0, 0)
    page_map = lambda p: (lambda b, pt: (pt[b, p], 0, 0))
    wspecs = [pl.BlockSpec((sub_w, 4 * CMP_HIDDEN), fix2),
              pl.BlockSpec((2 * CMP_HIDDEN, KV_WIDTH), fix2),
              pl.BlockSpec((SUBLANES, sub_w), fix2)]
    in_specs = ([pl.BlockSpec((nr, QPAD_WIDTH), row)] * 2 + [pl.BlockSpec((nr, KV_WIDTH), row)] * 4
                + [pl.BlockSpec((nr, LANES), row), pl.BlockSpec((nr, NSA_WIDTH), row)]
                + [pl.BlockSpec((1, win_buf, KV_WIDTH), per_b3)] * 2
                + [pl.BlockSpec((1, page // CMP_STRIDE, sub_w), page_map(p)) for p in range(n_pages)] * 1
                + [pl.BlockSpec((1, page // CMP_STRIDE, sub_w), page_map(p)) for p in range(n_pages)]
                + [pl.BlockSpec((1, page, KV_WIDTH), page_map(p)) for p in range(n_pages)]
                + [pl.BlockSpec((1, page, KV_WIDTH), page_map(p)) for p in range(n_pages)]
                + wspecs + wspecs
                + [pl.BlockSpec((LANES, LANES), fix2), pl.BlockSpec((LANES, all_keys), fix2)])
    grid_spec = pltpu.PrefetchScalarGridSpec(
        num_scalar_prefetch=1,
        grid=(batch,),
        in_specs=in_specs,
        out_specs=[pl.BlockSpec((nr, NSA_WIDTH), row),
                   pl.BlockSpec((1, win_buf, KV_WIDTH), per_b3),
                   pl.BlockSpec((1, win_buf, KV_WIDTH), per_b3)],
        scratch_shapes=[pltpu.VMEM((all_keys, KV_WIDTH), BF16), pltpu.VMEM((all_keys, KV_WIDTH), BF16),
                        pltpu.VMEM((win_keys, KV_WIDTH), BF16), pltpu.VMEM((win_keys, KV_WIDTH), BF16)])
    return pl.pallas_call(
        functools.partial(_nsa_sample_kernel, n_pages=n_pages, page=page, past=past, n_new=n_new,
                          win_keys=win_keys),
        grid_spec=grid_spec,
        out_shape=[jax.ShapeDtypeStruct((batch * nr, NSA_WIDTH), F32),
                   jax.ShapeDtypeStruct((batch, win_buf, KV_WIDTH), F32),
                   jax.ShapeDtypeStruct((batch, win_buf, KV_WIDTH), F32)],
        compiler_params=pltpu.CompilerParams(dimension_semantics=("arbitrary",),
                                             vmem_limit_bytes=VMEM_LIMIT_BYTES),
        name="nsa_sample",
    )(page_table, q, qr, sk, sv, wk, wv, sm, zn, win_k, win_v,
      *([cmp_k] * n_pages), *([cmp_v] * n_pages), *([slc_k] * n_pages), *([slc_v] * n_pages),
      *cmp_k_w, *cmp_v_w, _overlap_t(ns), e)


def _compress_weights_t(pe, w1, w2):
    w1big, w2big, pe2 = _compress_weights(pe, w1, w2)
    return w1big.T, w2big.T, pe2


def _nsa_sample_t_kernel(pt_ref, q_ref, qr_ref, skn_ref, svn_ref, wkn_ref, wvn_ref, sm_ref, zn_ref,
                         wkc_ref, wvc_ref, cmpk_hbm, cmpv_hbm, slck_hbm, slcv_hbm,
                         w1k_ref, w2k_ref, pek_ref, w1v_ref, w2v_ref, pev_ref, ovt_ref, e_ref,
                         o_ref, wko_ref, wvo_ref, kt_s, vt_s, kwt_s, vwt_s, pages_s, sems,
                         *, nb, n_pages, page, past, n_new):
    step = pl.program_id(0)
    last = pl.num_programs(0) - 1
    slot = step % 2
    caches = (cmpk_hbm, cmpv_hbm, slck_hbm, slcv_hbm)
    page_ids = [(j, c, p) for j in range(nb) for c in range(len(caches)) for p in range(n_pages)]

    def page_copy(at_step, at_slot, j, c, p):
        src = caches[c].at[pt_ref[at_step * nb + j, p]]
        return pltpu.make_async_copy(src, pages_s.at[at_slot, j, c, p], sems.at[at_slot, c])

    @pl.when(step == 0)
    def _():
        for ids in page_ids:
            page_copy(0, 0, *ids).start()

    nxt = jnp.minimum(step + 1, last)
    for ids in page_ids:
        page_copy(nxt, 1 - slot, *ids).start()
    for ids in page_ids:
        page_copy(step, slot, *ids).wait()

    cmpk, cmpv, slck, slcv = ([pages_s.at[slot, j, c, p] for j in range(nb) for p in range(n_pages)]
                              for c in range(len(caches)))
    nr = SAMPLE_ROWS
    rows = Q_HEADS * nr
    win_buf = wkc_ref.shape[2]
    bs = range(nb)
    hid = 2 * CMP_HIDDEN
    zpad = jnp.zeros((LANES - nr, LANES), F32)

    def new_rows(ref, b):
        return jnp.concatenate([ref[b * nr:(b + 1) * nr, :], zpad], axis=0).astype(BF16)

    def stack(ref, b):
        return jnp.concatenate([ref[b * nr:(b + 1) * nr, h * LANES:(h + 1) * LANES] for h in range(Q_HEADS)],
                               axis=0).astype(BF16)

    for b in bs:
        for p in range(n_pages):
            kt_s[b, :, p * page:(p + 1) * page] = slck[b * n_pages + p][...].astype(BF16)
            vt_s[b, :, p * page:(p + 1) * page] = slcv[b * n_pages + p][...].astype(BF16)
        kwt_s[b] = wkc_ref[b].astype(BF16)
        vwt_s[b] = wvc_ref[b].astype(BF16)
    mi = lax.broadcasted_iota(jnp.int32, (page, page), 0)
    ki = lax.broadcasted_iota(jnp.int32, (page, page), 1)
    per_page = page // CMP_STRIDE
    perm = (ki == CMP_STRIDE * (mi % per_page) + mi // per_page).astype(BF16)
    rows_t = [[[_mm(perm, pages[b * n_pages + p][...], _NT) for p in range(n_pages)] for pages in (cmpk, cmpv)]
              for b in bs]

    q_raw = [stack(q_ref, b) for b in bs]
    q_rot = [stack(qr_ref, b) for b in bs]

    sw = [jnp.concatenate([_mm(q_rot[b], kwt_s[b]), _mm(q_rot[b], new_rows(wkn_ref, b), _NT)], axis=1) for b in bs]
    ss = [jnp.concatenate([_mm(q_rot[b], kt_s[b]), _mm(q_rot[b], new_rows(skn_ref, b), _NT)], axis=1) for b in bs]

    def sub_blocks(b, c):
        cols = [jnp.concatenate([rows_t[b][c][p][l * per_page:(l + 1) * per_page] for p in range(n_pages)], axis=0)
                for l in range(CMP_STRIDE)]
        return jnp.concatenate(cols, axis=1)

    ckt, cvt = [], []
    def activate(t):
        bias = t[0:hid, LANES:LANES + 1] + t[hid:2 * hid, LANES + 1:LANES + 2]
        return _silu(t[0:hid, 0:LANES] + pltpu.roll(t[hid:2 * hid, 0:LANES], N_SUB - 1, 1) + bias)

    for c, (w1_ref, w2_ref, pe_ref, out) in enumerate(((w1k_ref, w2k_ref, pek_ref, ckt), (w1v_ref, w2v_ref, pev_ref, cvt))):
        pe_rows = pe_ref[...]
        ht = [_mm(w1_ref[...], jnp.concatenate([sub_blocks(b, c).astype(BF16), pe_rows], axis=0), _NT) for b in bs]
        out.extend(_mm(w2_ref[...], activate(ht[b])) for b in bs)

    tok = lax.broadcasted_iota(jnp.int32, (nr, LANES), 0)
    col = lax.broadcasted_iota(jnp.int32, (nr, LANES), 1)
    cm = _tile_rows((CMP_STRIDE * col + (CMP_LEN - 1) <= past + tok) & (col < N_CMP), Q_HEADS)
    cm_f = cm.astype(F32)
    sc = [jnp.where(cm, _mm(q_raw[b], ckt[b]), MASK_VALUE) for b in bs]
    ec = [jnp.exp(sc[b] - jnp.max(sc[b], axis=1, keepdims=True)) for b in bs]
    pc = [ec[b] / jnp.sum(ec[b], axis=1, keepdims=True) * cm_f for b in bs]
    o_cmp = [_mm(pc[b], cvt[b], _NT) for b in bs]
    p_sum = jnp.concatenate(
        [sum(pc[b][(kvh * GROUP + g) * nr:(kvh * GROUP + g + 1) * nr] for g in range(GROUP))
         for b in bs for kvh in range(KV_HEADS)], axis=0)
    ns = (past + n_new + SEL_BLOCK - 1) // SEL_BLOCK
    ns_rows = -(-ns // SUBLANES) * SUBLANES
    nq = nb * KV_HEADS * nr
    imp_t = _mm_exact_lhs(ovt_ref[...], p_sum, _NT)[0:ns_rows]
    n_idx = lax.broadcasted_iota(jnp.int32, (ns_rows, nq), 0)
    qb_t = (past + lax.broadcasted_iota(jnp.int32, (ns_rows, nq), 1) % nr) // SEL_BLOCK
    sel_t = _select_blocks(imp_t, n_idx, qb_t, ns)
    sel_t = jnp.concatenate([sel_t, jnp.zeros((LANES - ns_rows, nq), F32)], axis=0)
    sel = _mm(_eye(nq, BF16), sel_t, _NT)

    all_keys = past + LANES
    q_pos = past + lax.broadcasted_iota(jnp.int32, (rows, all_keys), 0) % nr
    causal = lax.broadcasted_iota(jnp.int32, (rows, all_keys), 1) <= q_pos
    o_slc = []
    for b in bs:
        sel_rows = jnp.concatenate(
            [_tile_rows(sel[(b * KV_HEADS + kvh) * nr:(b * KV_HEADS + kvh + 1) * nr], GROUP) for kvh in range(KV_HEADS)],
            axis=0)
        ok = (_mm(sel_rows, e_ref[...]) > 0.5) & causal
        s = ss[b] + jnp.where(ok, 0.0, MASK_VALUE)
        pr = jnp.exp(s - jnp.max(s, axis=1, keepdims=True))
        o = _mm(pr[:, :past], vt_s[b], _NT) + _mm(pr[:, past:], new_rows(svn_ref, b))
        o_slc.append(o / jnp.sum(pr, axis=1, keepdims=True))

    win_keys = win_buf + LANES
    q_pos = past + lax.broadcasted_iota(jnp.int32, (rows, win_keys), 0) % nr
    diff = q_pos - (past - win_buf + lax.broadcasted_iota(jnp.int32, (rows, win_keys), 1))
    bias_w = jnp.where((diff >= 0) & (diff <= WINDOW), 0.0, MASK_VALUE)
    o_win = []
    for b in bs:
        s = sw[b] + bias_w
        pr = jnp.exp(s - jnp.max(s, axis=1, keepdims=True))
        o = _mm(pr[:, :win_buf], vwt_s[b], _NT) + _mm(pr[:, win_buf:], new_rows(wvn_ref, b))
        o_win.append(o / jnp.sum(pr, axis=1, keepdims=True))

    for b in bs:
        sig = _sigmoid(sm_ref[b * nr:(b + 1) * nr, :])

        def gate(branch):
            cols = [sig[:, branch * Q_HEADS + h:branch * Q_HEADS + h + 1] for h in range(Q_HEADS)]
            return jnp.concatenate([jnp.broadcast_to(c, (nr, LANES)) for c in cols], axis=0)

        o_all = gate(0) * o_cmp[b] + gate(1) * o_slc[b] + gate(2) * o_win[b]
        for kvh in range(KV_HEADS):
            o_kv = o_all[kvh * GROUP * nr:(kvh + 1) * GROUP * nr]
            for pair, slab in enumerate(_merge_heads(o_kv, kvh, nr)):
                lo = (kvh * (GROUP // 2) + pair) * LANES
                o_ref[b * nr:(b + 1) * nr, lo:lo + LANES] = slab * _silu(zn_ref[b * nr:(b + 1) * nr, lo:lo + LANES])

    eye_bf = _eye(LANES, BF16)
    lane = lax.broadcasted_iota(jnp.int32, (KV_WIDTH, LANES), 1)
    for cache_ref, new_ref, out_ref in ((wkc_ref, wkn_ref, wko_ref), (wvc_ref, wvn_ref, wvo_ref)):
        for b in bs:
            shifted = pltpu.roll(cache_ref[b], win_buf - n_new, 1)
            new_pad = jnp.concatenate([new_ref[b * nr:(b + 1) * nr, :], zpad], axis=0)
            new_t = _mm_exact_lhs(eye_bf, new_pad, _NT)
            tail = jnp.where(lane >= LANES - n_new, pltpu.roll(new_t, LANES - n_new, 1), shifted[:, win_buf - LANES:])
            out_ref[b] = jnp.concatenate([shifted[:, :win_buf - LANES], tail], axis=1)

    @pl.when(step == last)
    def _():
        for ids in page_ids:
            page_copy(nxt, 1 - slot, *ids).wait()


def _nsa_sample_t(q, qr, sk, sv, wk, wv, sm, zn, win_k, win_v, cmp_k, cmp_v, slc_k, slc_v, page_table,
                  cmp_k_w, cmp_v_w, n_new, nb):
    batch, n_pages = page_table.shape
    n_pool, page = cmp_k.shape[0], cmp_k.shape[1]
    past = n_pages * page
    nr = SAMPLE_ROWS
    win_buf = win_k.shape[1]
    ns = (past + n_new + SEL_BLOCK - 1) // SEL_BLOCK
    all_keys = past + LANES
    e = jnp.asarray(np.arange(LANES)[:, None] == (np.arange(all_keys) // SEL_BLOCK)[None, :], BF16)
    tview = lambda c: jnp.transpose(c, (0, 2, 3, 1)).reshape(c.shape[0], KV_WIDTH, c.shape[1])
    cmp_k, cmp_v, slc_k, slc_v, win_k, win_v = (tview(c) for c in (cmp_k, cmp_v, slc_k, slc_v, win_k, win_v))

    row = lambda b, pt: (b, 0)
    per_b3 = lambda b, pt: (b, 0, 0)
    fix2 = lambda b, pt: (0, 0)
    sub_w = CMP_STRIDE * KV_WIDTH
    wspecs = [pl.BlockSpec((4 * CMP_HIDDEN, sub_w), fix2),
              pl.BlockSpec((KV_WIDTH, 2 * CMP_HIDDEN), fix2),
              pl.BlockSpec((LANES, sub_w), fix2)]
    in_specs = ([pl.BlockSpec((nb * nr, QPAD_WIDTH), row)] * 2 + [pl.BlockSpec((nb * nr, KV_WIDTH), row)] * 4
                + [pl.BlockSpec((nb * nr, LANES), row), pl.BlockSpec((nb * nr, NSA_WIDTH), row)]
                + [pl.BlockSpec((nb, KV_WIDTH, win_buf), per_b3)] * 2
                + [pl.BlockSpec(memory_space=pl.ANY)] * 4 + wspecs + wspecs
                + [pl.BlockSpec((LANES, LANES), fix2), pl.BlockSpec((LANES, all_keys), fix2)])
    grid_spec = pltpu.PrefetchScalarGridSpec(
        num_scalar_prefetch=1,
        grid=(batch // nb,),
        in_specs=in_specs,
        out_specs=[pl.BlockSpec((nb * nr, NSA_WIDTH), row),
                   pl.BlockSpec((nb, KV_WIDTH, win_buf), per_b3), pl.BlockSpec((nb, KV_WIDTH, win_buf), per_b3)],
        scratch_shapes=[pltpu.VMEM((nb, KV_WIDTH, past), BF16), pltpu.VMEM((nb, KV_WIDTH, past), BF16),
                        pltpu.VMEM((nb, KV_WIDTH, win_buf), BF16), pltpu.VMEM((nb, KV_WIDTH, win_buf), BF16),
                        pltpu.VMEM((2, nb, 4, n_pages, KV_WIDTH, page), F32),
                        pltpu.SemaphoreType.DMA((2, 4))])
    return pl.pallas_call(
        functools.partial(_nsa_sample_t_kernel, nb=nb, n_pages=n_pages, page=page, past=past, n_new=n_new),
        grid_spec=grid_spec,
        out_shape=[jax.ShapeDtypeStruct((batch * nr, NSA_WIDTH), F32),
                   jax.ShapeDtypeStruct((batch, KV_WIDTH, win_buf), F32),
                   jax.ShapeDtypeStruct((batch, KV_WIDTH, win_buf), F32)],
        compiler_params=pltpu.CompilerParams(dimension_semantics=("arbitrary",),
                                             vmem_limit_bytes=VMEM_LIMIT_BYTES),
        name="nsa_sample",
    )(page_table, q, qr, sk, sv, wk, wv, sm, zn, win_k, win_v, cmp_k, cmp_v, slc_k, slc_v,
      *cmp_k_w, *cmp_v_w, _overlap_t(ns), e)


def _unit_lower_inverse(a, n_valid, hi):
    c = a.shape[0]
    inv = _eye(c, F32) - a
    power = a
    span = 2
    while span < n_valid:
        power = _mm(power, power, hi=hi)
        inv = inv + _mm(inv, power, hi=hi)
        span *= 2
    return inv


def _gdn_kernel(qkv_ref, sm_ref, zg_ref, conv0_ref, s0_ref, wc_ref, vec_ref, wg_ref,
                go_ref, xp_out_ref, st_ref, xp_s, *, bblk, chunk, n_valid, hi):
    c_idx = pl.program_id(1)

    @pl.when(c_idx == 0)
    def _():
        xp_s[:, 0:SUBLANES, :] = conv0_ref[...]
        st_ref[...] = s0_ref[...]

    wc = wc_ref[...]
    vec = vec_ref[...]
    row1 = lax.broadcasted_iota(jnp.int32, (chunk, LANES), 0)
    rr = lax.broadcasted_iota(jnp.int32, (chunk, chunk), 0)
    cc = lax.broadcasted_iota(jnp.int32, (chunk, chunk), 1)
    tri = rr >= cc
    tri_bf = tri.astype(BF16)
    eye_bf = _eye(LANES, BF16)

    acts, betas, decays, decay_ts, e_decs = [], [], [], [], []
    for b in range(bblk):
        xp_s[b, SUBLANES:SUBLANES + chunk, :] = qkv_ref[b].astype(F32)
        y = xp_s[b, SUBLANES:SUBLANES + chunk, :] * wc[GDN_CONV - 1:GDN_CONV, :]
        for j in range(GDN_CONV - 1):
            lo = SUBLANES - (GDN_CONV - 1) + j
            y = y + xp_s[b, lo:lo + chunk, :] * wc[j:j + 1, :]
        xp_out_ref[b] = xp_s[b]
        xp_s[b, 0:SUBLANES, :] = xp_s[b, chunk:chunk + SUBLANES, :]
        act = _silu(y)
        small = sm_ref[b]
        z = small + vec[1:2, :]
        softplus = jnp.maximum(z, 0.0) + jnp.log1p(jnp.exp(-jnp.abs(z)))
        g_all = -jnp.exp(vec[0:1, :]) * softplus
        if n_valid < chunk:
            valid = row1 < n_valid
            act = act * _tile_lanes(valid.astype(F32), GDN_CONV_CH // LANES)
            g_all = jnp.where(valid, g_all, 0.0)
        acts.append(act)
        betas.append(_sigmoid(small))
        decays.append(_mm_exact_lhs(tri_bf, g_all))
    for b in range(bblk):
        decay_ts.append(_mm_exact_lhs(eye_bf, decays[b], _NT))
        e_decs.append(jnp.exp(decays[b]))

    chains = [(b, h) for b in range(bblk) for h in range(GDN_HEADS)]
    qs, ks, kbs, dmasks, rhs_u, rhs_w, qds, kds, gls = [], [], [], [], [], [], [], [], []
    for b, h in chains:
        act = acts[b]
        qh = act[:, h * GDN_DK:(h + 1) * GDN_DK]
        kh = act[:, GDN_WIDTH + h * GDN_DK:GDN_WIDTH + (h + 1) * GDN_DK]
        vh = act[:, 2 * GDN_WIDTH + h * GDN_DV:2 * GDN_WIDTH + (h + 1) * GDN_DV]
        qh = qh * lax.rsqrt(jnp.sum(qh * qh, axis=-1, keepdims=True) + NORM_EPS) * (GDN_DK ** -0.5)
        kh = kh * lax.rsqrt(jnp.sum(kh * kh, axis=-1, keepdims=True) + NORM_EPS)
        beta = betas[b][:, SM_B + h:SM_B + h + 1]
        dcol = decays[b][:, SM_A + h:SM_A + h + 1]
        drow = decay_ts[b][SM_A + h:SM_A + h + 1, :]
        ed = e_decs[b][:, SM_A + h:SM_A + h + 1]
        dlast = decays[b][chunk - 1:chunk, SM_A + h:SM_A + h + 1]
        kb = kh * beta
        qs.append(qh)
        ks.append(kh)
        kbs.append(kb)
        dmasks.append(jnp.where(tri, jnp.exp(jnp.where(tri, dcol - drow, 0.0)), 0.0))
        rhs_u.append(vh * beta)
        rhs_w.append(kb * ed)
        qds.append(qh * ed)
        kds.append(kh * jnp.exp(dlast - dcol))
        gls.append(jnp.exp(dlast))

    n = len(chains)
    eye_c = _eye(chunk, F32)
    kkts = [_mm(kbs[i], ks[i], _NT, hi=hi) for i in range(n)]
    qks = [_mm(qs[i], ks[i], _NT, hi=hi) for i in range(n)]
    powers = [jnp.where(rr > cc, kkts[i] * dmasks[i], 0.0) for i in range(n)]
    qks = [qks[i] * dmasks[i] for i in range(n)]
    invs = [eye_c - powers[i] for i in range(n)]
    span = 2
    while span < n_valid:
        powers = [_mm(powers[i], powers[i], hi=hi) for i in range(n)]
        invs = [invs[i] + _mm(invs[i], powers[i], hi=hi) for i in range(n)]
        span *= 2
    us = [_mm(invs[i], rhs_u[i], hi=hi) for i in range(n)]
    ws = [_mm(invs[i], rhs_w[i], hi=hi) for i in range(n)]
    sts = [st_ref[b, h] for b, h in chains]
    v_news = [us[i] - _mm(ws[i], sts[i], hi=hi) for i in range(n)]
    os_ = [_mm(qds[i], sts[i], hi=hi) for i in range(n)]
    os_ = [os_[i] + _mm(qks[i], v_news[i], hi=hi) for i in range(n)]
    upd = [_mm(kds[i], v_news[i], _TN, hi=hi) for i in range(n)]
    for i, (b, h) in enumerate(chains):
        st_ref[b, h] = sts[i] * gls[i] + upd[i]
        o = os_[i]
        o = o * lax.rsqrt(jnp.mean(o * o, axis=-1, keepdims=True) + NORM_EPS) * wg_ref[...]
        o = o * _silu(zg_ref[b, :, h * GDN_DV:(h + 1) * GDN_DV].astype(F32))
        go_ref[b, :, h * GDN_DV:(h + 1) * GDN_DV] = o.astype(go_ref.dtype)


def _gdn(qkv, sm, zg, conv0, s0, w_conv, a_log, dt_bias, w_gnorm, batch, rows, bblk, chunk, n_valid, hi,
         out_dtype):
    nc = rows // chunk
    tile = lambda b, c: (b, c, 0)
    per_b3 = lambda b, c: (b, 0, 0)
    per_b4 = lambda b, c: (b, 0, 0, 0)
    fix2 = lambda b, c: (0, 0)
    wc = jnp.concatenate([w_conv, jnp.zeros((SUBLANES - GDN_CONV, GDN_CONV_CH), w_conv.dtype)], axis=0)
    vec = jnp.zeros((SUBLANES, LANES), F32)
    vec = vec.at[0, SM_A:SM_A + GDN_HEADS].set(a_log).at[1, SM_A:SM_A + GDN_HEADS].set(dt_bias)
    go, xp, st = pl.pallas_call(
        functools.partial(_gdn_kernel, bblk=bblk, chunk=chunk, n_valid=n_valid, hi=hi),
        grid=(batch // bblk, nc),
        in_specs=[pl.BlockSpec((bblk, chunk, GDN_CONV_CH), tile),
                  pl.BlockSpec((bblk, chunk, LANES), tile),
                  pl.BlockSpec((bblk, chunk, GDN_WIDTH), tile),
                  pl.BlockSpec((bblk, SUBLANES, GDN_CONV_CH), per_b3),
                  pl.BlockSpec((bblk, GDN_HEADS, GDN_DK, GDN_DV), per_b4),
                  pl.BlockSpec((SUBLANES, GDN_CONV_CH), fix2),
                  pl.BlockSpec((SUBLANES, LANES), fix2),
                  pl.BlockSpec((1, GDN_DV), fix2)],
        out_specs=[pl.BlockSpec((bblk, chunk, GDN_WIDTH), tile),
                   pl.BlockSpec((bblk, SUBLANES + chunk, GDN_CONV_CH), per_b3),
                   pl.BlockSpec((bblk, GDN_HEADS, GDN_DK, GDN_DV), per_b4)],
        out_shape=[jax.ShapeDtypeStruct((batch, rows, GDN_WIDTH), out_dtype),
                   jax.ShapeDtypeStruct((batch, SUBLANES + chunk, GDN_CONV_CH), F32),
                   jax.ShapeDtypeStruct((batch, GDN_HEADS, GDN_DK, GDN_DV), F32)],
        scratch_shapes=[pltpu.VMEM((bblk, SUBLANES + chunk, GDN_CONV_CH), F32)],
        compiler_params=pltpu.CompilerParams(dimension_semantics=("arbitrary", "arbitrary"),
                                             vmem_limit_bytes=VMEM_LIMIT_BYTES),
        name="gdn",
    )(qkv.reshape(batch, rows, GDN_CONV_CH), sm.reshape(batch, rows, LANES), zg.reshape(batch, rows, GDN_WIDTH),
      conv0, s0, wc, vec, w_gnorm.reshape(1, GDN_DV))
    return go.reshape(batch * rows, GDN_WIDTH), xp, st


PROMPT_TM = 512
SAMPLE_TM = 256
GDN_PROMPT_BBLK = 8
GDN_SAMPLE_BBLK = 8
NSA_SAMPLE_NB = 2


def _layer_prompt(h, lw, final, w_final, win_buf):
    w_norm, w_pack, cmp_k_w, cmp_v_w, w_conv, a_log, dt_bias, w_gnorm, w_out = lw
    batch, seq, d = h.shape
    x2d = h.reshape(batch * seq, d)
    tables = _rope_tables(jnp.arange(seq, dtype=jnp.int32))
    (q, qr, ck, cv, sk, sv, wk, wv, kvb, zn, qkv, zg, sm, ck_t, cv_t, sk_t, sv_t, wk_t, wv_t) = _project(
        x2d, w_norm, w_pack.astype(BF16), tables, seq, PROMPT_TM, False, BF16, True)
    nsa = _nsa_prompt(q, qr, kvb, ck, cv, sm, zn, cmp_k_w, cmp_v_w, batch, seq)
    conv0 = jnp.zeros((batch, SUBLANES, GDN_CONV_CH), F32)
    s0 = jnp.zeros((batch, GDN_HEADS, GDN_DK, GDN_DV), F32)
    go, xp, st = _gdn(qkv, sm, zg, conv0, s0, w_conv, a_log, dt_bias, w_gnorm,
                      batch, seq, GDN_PROMPT_BBLK, GDN_CHUNK, GDN_CHUNK, False, BF16)
    y = _out_project(x2d, nsa, go, w_out.astype(BF16), w_final, PROMPT_TM, final)
    conv_new = xp[:, SUBLANES + GDN_CHUNK - (GDN_CONV - 1):SUBLANES + GDN_CHUNK]
    from_t = lambda t: jnp.transpose(t.reshape(batch, KV_HEADS, HEAD_DIM, t.shape[-1]), (0, 3, 1, 2))
    lead = ((0, 0), (0, 0), (max(win_buf - seq, 0), 0))
    win = lambda t: from_t(jnp.pad(t, lead)[:, :, -win_buf:])
    return y.reshape(batch, seq, d), (from_t(ck_t), from_t(cv_t), from_t(sk_t), from_t(sv_t), win(wk_t), win(wv_t),
                                      conv_new, st)


def _layer_sample(h8, n_new, caches, page_table, lw, final, w_final):
    w_norm, w_pack, cmp_k_w, cmp_v_w, w_conv, a_log, dt_bias, w_gnorm, w_out = lw
    pe_rows = lambda pe2: jnp.pad(pe2, ((0, LANES - pe2.shape[0]), (0, 0))).astype(BF16)
    cmp_k_t = (cmp_k_w[0].T, cmp_k_w[1].T, pe_rows(cmp_k_w[2]))
    cmp_v_t = (cmp_v_w[0].T, cmp_v_w[1].T, pe_rows(cmp_v_w[2]))
    c_cmp_k, c_cmp_v, c_slc_k, c_slc_v, c_win_k, c_win_v, s_conv, s_gdn = caches
    batch, nr, d = h8.shape
    past = page_table.shape[1] * c_cmp_k.shape[1]
    x2d = h8.reshape(batch * nr, d)
    tables = _rope_tables(past + jnp.arange(nr, dtype=jnp.int32))
    tables = tuple(jnp.tile(t, (SAMPLE_TM // nr, 1)) for t in tables)
    (q, qr, ck, cv, sk, sv, wk, wv, _, zn, qkv, zg, sm) = _project(
        x2d, w_norm, w_pack, tables, SAMPLE_TM, SAMPLE_TM, True, F32, False)
    nsa, win_k_t, win_v_t = _nsa_sample_t(q, qr, sk, sv, wk, wv, sm, zn, c_win_k, c_win_v, c_cmp_k, c_cmp_v,
                                          c_slc_k, c_slc_v, page_table, cmp_k_t, cmp_v_t, n_new, NSA_SAMPLE_NB)
    conv0 = jnp.pad(s_conv, ((0, 0), (SUBLANES - (GDN_CONV - 1), 0), (0, 0)))
    go, xp, st = _gdn(qkv, sm, zg, conv0, s_gdn, w_conv, a_log, dt_bias, w_gnorm,
                      batch, nr, GDN_SAMPLE_BBLK, nr, n_new, True, F32)
    y = _out_project(x2d, nsa, go, w_out.astype(BF16), w_final, SAMPLE_TM, final)
    kv4 = lambda t: t.reshape(batch, nr, KV_HEADS, HEAD_DIM)[:, :n_new]
    from_t = lambda t: jnp.transpose(t.reshape(batch, KV_HEADS, HEAD_DIM, t.shape[-1]), (0, 3, 1, 2))
    conv_new = xp[:, SUBLANES + n_new - (GDN_CONV - 1):SUBLANES + n_new]
    return y.reshape(batch, nr, d), (kv4(ck), kv4(cv), kv4(sk), kv4(sv), from_t(win_k_t), from_t(win_v_t),
                                     conv_new, st)


def kernel(x_prompt, x_sample, cache_cmp_k, cache_cmp_v, cache_slc_k, cache_slc_v, cache_win_k, cache_win_v, state_conv, state_gdn, page_table, w_norm, w_in, pe_cmp_k, w_cmp_k1, w_cmp_k2, pe_cmp_v, w_cmp_v1, w_cmp_v2, w_conv, a_log, dt_bias, w_gdn_norm, w_out, w_final_norm):
    depth = w_in.shape[0]
    n_new = x_sample.shape[1]
    win_buf = cache_win_k.shape[2]
    h_p = x_prompt
    h_s = jnp.pad(x_sample, ((0, 0), (0, SAMPLE_ROWS - n_new), (0, 0)))
    st_p, st_s = [], []
    for layer in range(depth):
        lw = (w_norm[layer], _pack_w_in(w_in[layer]),
              _compress_weights(pe_cmp_k[layer], w_cmp_k1[layer], w_cmp_k2[layer]),
              _compress_weights(pe_cmp_v[layer], w_cmp_v1[layer], w_cmp_v2[layer]),
              w_conv[layer], a_log[layer], dt_bias[layer], w_gdn_norm[layer], w_out[layer])
        final = layer == depth - 1
        h_p, sp = _layer_prompt(h_p, lw, final, w_final_norm, win_buf)
        caches = (cache_cmp_k[layer], cache_cmp_v[layer], cache_slc_k[layer], cache_slc_v[layer],
                  cache_win_k[layer], cache_win_v[layer], state_conv[layer], state_gdn[layer])
        h_s, ss = _layer_sample(h_s, n_new, caches, page_table, lw, final, w_final_norm)
        st_p.append(sp)
        st_s.append(ss)
    outs = [h_p, h_s[:, :n_new]]
    for i in range(8):
        outs.append(jnp.stack([s[i] for s in st_p]))
        outs.append(jnp.stack([s[i] for s in st_s]))
    return tuple(outs)
```

```python
import functools

import numpy as np
import jax
import jax.numpy as jnp
from jax import lax
from jax.experimental import pallas as pl
from jax.experimental.pallas import tpu as pltpu

F32 = jnp.float32
BF16 = jnp.bfloat16

D_MODEL = 1024
HEAD_DIM = 64
Q_HEADS = 8
KV_HEADS = 2
GROUP = Q_HEADS // KV_HEADS
NSA_WIDTH = Q_HEADS * HEAD_DIM
KV_WIDTH = KV_HEADS * HEAD_DIM
CMP_LEN = 32
CMP_STRIDE = 16
CMP_HIDDEN = 128
SEL_BLOCK = 64
TOP_N = 8
WINDOW = 512
FORCE_BONUS = 1.0e4
ROT_DIM = HEAD_DIM // 4
ROPE_THETA = 500000.0
GDN_DK = 128
GDN_DV = 128
GDN_HEADS = 4
GDN_WIDTH = GDN_HEADS * GDN_DV
GDN_CONV = 4
GDN_CONV_CH = 3 * GDN_WIDTH
GDN_CHUNK = 64
NORM_EPS = 1e-6
MASK_VALUE = -1e30

LANES = 128
SUBLANES = 8
VMEM_LIMIT_BYTES = 56 * 1024 * 1024

QPAD_WIDTH = Q_HEADS * LANES
C_Q = 0
C_KV = C_Q + NSA_WIDTH
C_ZN = C_KV + 6 * KV_WIDTH
C_QKV = C_ZN + NSA_WIDTH
C_ZG = C_QKV + GDN_CONV_CH
C_SM = C_ZG + GDN_WIDTH
N_PACK = C_SM + LANES
SM_B = 3 * Q_HEADS
SM_A = SM_B + GDN_HEADS

TQ = 128
CK_SLC = 256
CK_WIN = 128
N_SUB = 128
N_CMP = N_SUB - CMP_LEN // CMP_STRIDE + 1
SAMPLE_ROWS = 8
KVB_SK = 0
KVB_SV = KVB_SK + KV_WIDTH
KVB_WK = KVB_SV + 2 * KV_WIDTH
KVB_WV = KVB_WK + KV_WIDTH
KVB_WIDTH = KVB_WV + 2 * KV_WIDTH
KVB_COLS = (KVB_SK, KVB_SV, KVB_WK, KVB_WV)
WIN_SPAN = WINDOW + TQ


def _pack_w_in(w_in):
    o_gate = NSA_WIDTH + 6 * KV_WIDTH
    o_zn = o_gate + 3 * Q_HEADS
    o_qkv = o_zn + NSA_WIDTH
    o_b = o_qkv + GDN_CONV_CH
    o_a = o_b + GDN_HEADS
    o_zg = o_a + GDN_HEADS
    pad = jnp.zeros((w_in.shape[0], LANES - SM_A - GDN_HEADS), w_in.dtype)
    return jnp.concatenate(
        [w_in[:, :o_gate], w_in[:, o_zn:o_qkv], w_in[:, o_qkv:o_b], w_in[:, o_zg:],
         w_in[:, o_gate:o_zn], w_in[:, o_b:o_a], w_in[:, o_a:o_zg], pad], axis=1)


def _rope_tables(pos):
    half = ROT_DIM // 2
    inv = ROPE_THETA ** (-(jnp.arange(half, dtype=F32) * 2.0 / ROT_DIM))
    ang = pos.astype(F32)[:, None] * inv[None, :]
    cos, sin = jnp.cos(ang), jnp.sin(ang)
    n = pos.shape[0]
    one = jnp.ones((n, HEAD_DIM - ROT_DIM), F32)
    zero = jnp.zeros((n, HEAD_DIM - ROT_DIM), F32)
    zh = jnp.zeros((n, half), F32)
    c64 = jnp.concatenate([cos, cos, one], axis=1)
    a64 = jnp.concatenate([zh, sin, zero], axis=1)
    b64 = jnp.concatenate([-sin, zh, zero], axis=1)
    tile = lambda t: jnp.concatenate([t, t], axis=1)
    return tile(c64), tile(a64), tile(b64)


def _rope128(x, c, a, b):
    half = ROT_DIM // 2
    return x * c + pltpu.roll(x, half, 1) * a + pltpu.roll(x, LANES - half, 1) * b


_NN = (((1,), (0,)), ((), ()))
_NT = (((1,), (1,)), ((), ()))
_TN = (((0,), (0,)), ((), ()))


def _split_bf16(x):
    hi = x.astype(BF16)
    return hi, (x - hi.astype(F32)).astype(BF16)


def _mm(a, b, dims=_NN, hi=False):
    dot = lambda x, y: lax.dot_general(x, y, dims, preferred_element_type=F32)
    if hi:
        a_hi, a_lo = _split_bf16(a.astype(F32))
        b_hi, b_lo = _split_bf16(b.astype(F32))
        return dot(a_hi, b_hi) + (dot(a_lo, b_hi) + dot(a_hi, b_lo))
    return dot(a.astype(BF16), b.astype(BF16))


def _mm_exact_lhs(a_bf16, x, dims=_NN):
    x1 = x.astype(BF16)
    r1 = x - x1.astype(F32)
    x2 = r1.astype(BF16)
    x3 = (r1 - x2.astype(F32)).astype(BF16)
    dot = lambda t: lax.dot_general(a_bf16, t, dims, preferred_element_type=F32)
    return dot(x1) + dot(x2) + dot(x3)


def _sigmoid(x):
    return 1.0 / (1.0 + jnp.exp(-x))


def _silu(x):
    return x * _sigmoid(x)


def _tile_rows(x, n):
    return jnp.concatenate([x] * n, axis=0)


def _tile_lanes(x, n):
    return x if n == 1 else jnp.concatenate([x] * n, axis=1)


def _eye(n, dtype):
    r = lax.broadcasted_iota(jnp.int32, (n, n), 0)
    c = lax.broadcasted_iota(jnp.int32, (n, n), 1)
    return (r == c).astype(dtype)


def _proj_kernel(x_ref, wn_ref, w_ref, c_ref, a_ref, b_ref,
                 q_ref, qr_ref, ck_ref, cv_ref, sk_ref, sv_ref, wk_ref, wv_ref, kvb_ref,
                 zn_ref, qkv_ref, zg_ref, sm_ref, *t_refs, hi_gdn):
    x = x_ref[...]
    ms = jnp.mean(x * x, axis=-1, keepdims=True)
    xn = x * lax.rsqrt(ms + NORM_EPS) * wn_ref[...]
    xb = xn.astype(BF16)
    c, a, b = c_ref[...], a_ref[...], b_ref[...]
    scale = HEAD_DIM ** -0.5
    lane = lax.broadcasted_iota(jnp.int32, (x.shape[0], LANES), 1)
    half_mask = (lane < HEAD_DIM, lane >= HEAD_DIM)
    for j in range(NSA_WIDTH // (2 * LANES)):
        q2 = _mm(xb, w_ref[:, C_Q + 2 * j * LANES:C_Q + 2 * (j + 1) * LANES])
        for jj in range(2):
            raw = q2[:, jj * LANES:(jj + 1) * LANES]
            for src, dst in ((raw, q_ref), (_rope128(raw, c, a, b), qr_ref)):
                swapped = pltpu.roll(src, HEAD_DIM, 1)
                for par in range(2):
                    head = 2 * (2 * j + jj) + par
                    kvh = head // GROUP
                    val = jnp.where(half_mask[kvh], src if par == kvh else swapped, 0.0)
                    dst[:, head * LANES:(head + 1) * LANES] = (val * scale).astype(dst.dtype)
    kv_refs = (ck_ref, cv_ref, sk_ref, sv_ref, wk_ref, wv_ref)
    kv2 = [_mm(xb, w_ref[:, C_KV + j * LANES:C_KV + (j + 2) * LANES]) for j in range(0, 6, 2)]
    for j in range(6):
        kj = kv2[j // 2][:, (j % 2) * LANES:(j % 2 + 1) * LANES]
        if j in (2, 4):
            kj = _rope128(kj, c, a, b)
        kv_refs[j][...] = kj
        if j >= 2:
            lo = KVB_COLS[j - 2]
            kvb_ref[:, lo:lo + LANES] = kj.astype(BF16)
        if t_refs:
            t_refs[j][0] = kj.T
    ones = jnp.ones((x.shape[0], LANES), BF16)
    kvb_ref[:, KVB_SV + LANES:KVB_SV + 2 * LANES] = ones
    kvb_ref[:, KVB_WV + LANES:KVB_WV + 2 * LANES] = ones
    zn_ref[...] = _mm(xb, w_ref[:, C_ZN:C_QKV])
    xg = xn if hi_gdn else xb
    for j in range(3):
        lo = C_QKV + j * GDN_WIDTH
        qkv_ref[:, j * GDN_WIDTH:(j + 1) * GDN_WIDTH] = _mm(xg, w_ref[:, lo:lo + GDN_WIDTH], hi=hi_gdn)
    zg_ref[...] = _mm(xb, w_ref[:, C_ZG:C_SM])
    sm_ref[...] = _mm(xg, w_ref[:, C_SM:N_PACK], hi=hi_gdn)


def _project(x2d, w_norm, w_pack, tables, rows_per_seq, tm, hi_gdn, q_dtype, emit_t):
    n = x2d.shape[0]
    nt = rows_per_seq // tm
    row = lambda i: (i, 0)
    tab = lambda i: (i % nt, 0)
    fix = lambda i: (0, 0)
    widths = (QPAD_WIDTH, QPAD_WIDTH) + (KV_WIDTH,) * 6 + (KVB_WIDTH, NSA_WIDTH, GDN_CONV_CH, GDN_WIDTH, LANES)
    dtypes = (q_dtype, q_dtype) + (F32,) * 6 + (BF16, F32, F32, F32, F32)
    out_specs = [pl.BlockSpec((tm, w), row) for w in widths]
    out_shape = [jax.ShapeDtypeStruct((n, w), d) for w, d in zip(widths, dtypes)]
    if emit_t:
        out_specs += [pl.BlockSpec((1, KV_WIDTH, tm), lambda i: (i // nt, 0, i % nt))] * 6
        out_shape += [jax.ShapeDtypeStruct((n // rows_per_seq, KV_WIDTH, rows_per_seq), F32)] * 6
    return pl.pallas_call(
        functools.partial(_proj_kernel, hi_gdn=hi_gdn),
        grid=(n // tm,),
        in_specs=[pl.BlockSpec((tm, D_MODEL), row),
                  pl.BlockSpec((1, D_MODEL), fix),
                  pl.BlockSpec((D_MODEL, N_PACK), fix),
                  pl.BlockSpec((tm, LANES), tab),
                  pl.BlockSpec((tm, LANES), tab),
                  pl.BlockSpec((tm, LANES), tab)],
        out_specs=out_specs,
        out_shape=out_shape,
        compiler_params=pltpu.CompilerParams(dimension_semantics=("arbitrary",),
                                             vmem_limit_bytes=VMEM_LIMIT_BYTES),
        name="in_proj",
    )(x2d, w_norm.reshape(1, D_MODEL), w_pack, *tables)


def _out_kernel(x_ref, nsa_ref, gdn_ref, w_ref, wf_ref, y_ref, *, final):
    mix = jnp.concatenate([nsa_ref[...].astype(BF16), gdn_ref[...].astype(BF16)], axis=1)
    h = x_ref[...] + _mm(mix, w_ref[...])
    if final:
        ms = jnp.mean(h * h, axis=-1, keepdims=True)
        h = h * lax.rsqrt(ms + NORM_EPS) * wf_ref[...]
    y_ref[...] = h


def _out_project(x2d, nsa, gdn, w_out, w_final, tm, final):
    n = x2d.shape[0]
    row = lambda i: (i, 0)
    fix = lambda i: (0, 0)
    return pl.pallas_call(
        functools.partial(_out_kernel, final=final),
        grid=(n // tm,),
        in_specs=[pl.BlockSpec((tm, D_MODEL), row),
                  pl.BlockSpec((tm, NSA_WIDTH), row),
                  pl.BlockSpec((tm, GDN_WIDTH), row),
                  pl.BlockSpec((D_MODEL, D_MODEL), fix),
                  pl.BlockSpec((1, D_MODEL), fix)],
        out_specs=pl.BlockSpec((tm, D_MODEL), row),
        out_shape=jax.ShapeDtypeStruct((n, D_MODEL), F32),
        compiler_params=pltpu.CompilerParams(dimension_semantics=("arbitrary",),
                                             vmem_limit_bytes=VMEM_LIMIT_BYTES),
        name="out_proj",
    )(x2d, nsa, gdn, w_out, w_final.reshape(1, D_MODEL))


def _compress_weights(pe, w1, w2):
    half = CMP_STRIDE * HEAD_DIM
    z = jnp.zeros((CMP_STRIDE, HEAD_DIM, CMP_HIDDEN), w1.dtype)

    def place(wpart, h):
        wp = wpart.reshape(CMP_STRIDE, HEAD_DIM, CMP_HIDDEN)
        parts = [wp, z] if h == 0 else [z, wp]
        return jnp.stack(parts, axis=1).reshape(CMP_STRIDE * KV_WIDTH, CMP_HIDDEN)

    w1big = jnp.concatenate([place(w1[:half], 0), place(w1[:half], 1),
                             place(w1[half:], 0), place(w1[half:], 1)], axis=1)
    zz = jnp.zeros_like(w2)
    w2big = jnp.concatenate([jnp.concatenate([w2, zz], axis=1), jnp.concatenate([zz, w2], axis=1)], axis=0)
    pe_a = jnp.tile(pe[:CMP_STRIDE], (1, KV_HEADS)).reshape(1, CMP_STRIDE * KV_WIDTH)
    pe_b = jnp.tile(pe[CMP_STRIDE:], (1, KV_HEADS)).reshape(1, CMP_STRIDE * KV_WIDTH)
    pe2 = jnp.concatenate([pe_a, pe_b, jnp.zeros((SUBLANES - 2, CMP_STRIDE * KV_WIDTH), pe.dtype)], axis=0)
    return w1big.astype(BF16), w2big.astype(BF16), pe2


def _compress(sub_rows, w1_ref, w2_ref, pe_ref):
    hid = 2 * CMP_HIDDEN
    w1 = w1_ref[...]
    ab = _mm(sub_rows, w1)
    pe = pe_ref[...]
    pe_hi = pe.astype(BF16)
    pe_lo = (pe - pe_hi.astype(F32)).astype(BF16)
    r = _mm(pe_hi, w1) + _mm(pe_lo, w1)
    bias = r[0:1, :hid] + r[1:2, hid:]
    h = ab[:, :hid] + pltpu.roll(ab[:, hid:], N_SUB - 1, 0) + bias
    return _mm(_silu(h), w2_ref[...])


def _overlap_t(ns):
    c0 = np.arange(N_CMP)[None, :] * CMP_STRIDE
    b0 = np.arange(ns)[:, None] * SEL_BLOCK
    ov = np.minimum(c0 + CMP_LEN, b0 + SEL_BLOCK) - np.maximum(c0, b0)
    out = np.zeros((LANES, LANES), np.float32)
    out[:ns, :N_CMP] = np.maximum(ov, 0) / CMP_LEN
    return jnp.asarray(out, BF16)


def _cmp_attention(q_stack, ck, cv, cvalid, groups):
    s = _mm(q_stack, ck, _NT)
    cm = _tile_rows(cvalid, groups)
    s = jnp.where(cm, s, MASK_VALUE)
    m = jnp.max(s, axis=1, keepdims=True)
    e = jnp.exp(s - m)
    p = e / jnp.sum(e, axis=1, keepdims=True) * cm.astype(F32)
    return _mm(p, cv), p


def _select_blocks(imp_t, n_idx, q_blk, n_rows):
    forced = (n_idx == 0) | (n_idx == q_blk) | (n_idx == q_blk - 1)
    allowed = n_idx <= q_blk
    v = jnp.where(allowed, imp_t + FORCE_BONUS * forced.astype(F32), MASK_VALUE)
    rank = jnp.zeros(v.shape, F32)
    for j in range(n_rows):
        vj = v[j:j + 1, :]
        ge = jnp.where(vj >= v, 1.0, 0.0)
        gt = jnp.where(vj > v, 1.0, 0.0)
        rank = rank + jnp.where(n_idx > j, ge, gt)
    return ((rank < TOP_N) & allowed).astype(F32)


def _flash_init(m_s, l_s, acc_s):
    m_s[...] = jnp.full(m_s.shape, MASK_VALUE, F32)
    l_s[...] = jnp.zeros(l_s.shape, F32)
    acc_s[...] = jnp.zeros(acc_s.shape, F32)


def _flash_step(s, v_c, m_s, l_s, acc_s):
    m_prev = m_s[...]
    m_next = jnp.maximum(m_prev, jnp.max(s, axis=1, keepdims=True))
    alpha = jnp.exp(m_prev - m_next)
    p = jnp.exp(s - _tile_lanes(m_next, s.shape[1] // LANES))
    l_s[...] = alpha * l_s[...] + jnp.sum(p, axis=1, keepdims=True)
    acc_s[...] = acc_s[...] * alpha + _mm(p, v_c)
    m_s[...] = m_next


def _merge_heads(o_sum, kvh, tq):
    lane = lax.broadcasted_iota(jnp.int32, (tq, LANES), 1)
    slabs = []
    for pair in range(GROUP // 2):
        halves = []
        for par in range(2):
            o = o_sum[(2 * pair + par) * tq:(2 * pair + par + 1) * tq]
            halves.append(o if par == kvh else pltpu.roll(o, HEAD_DIM, 1))
        slabs.append(jnp.where(lane < HEAD_DIM, halves[0], halves[1]))
    return slabs


def _nsa_prompt_kernel(q_ref, qr_ref, kvb_ref, ckr_ref, cvr_ref, sm_ref, zn_ref,
                       wk1_ref, wk2_ref, pek_ref, wv1_ref, wv2_ref, pev_ref, ovt_ref, e_ref,
                       o_ref, ck_s, cv_s, s_s, mrun_s, m_s, acc_s):
    i = pl.program_id(1)

    @pl.when(i == 0)
    def _():
        ck_s[...] = _compress(ckr_ref[0], wk1_ref, wk2_ref, pek_ref).astype(BF16)
        cv_s[...] = _compress(cvr_ref[0], wv1_ref, wv2_ref, pev_ref).astype(BF16)

    kvs = range(KV_HEADS)
    t0 = i * TQ
    rows = GROUP * TQ
    row = lax.broadcasted_iota(jnp.int32, (TQ, LANES), 0)
    col = lax.broadcasted_iota(jnp.int32, (TQ, LANES), 1)
    cvalid = (CMP_STRIDE * col + (CMP_LEN - 1) <= t0 + row) & (col < N_CMP)
    cm = _tile_rows(cvalid, GROUP)
    cm_f = cm.astype(F32)
    sig = _sigmoid(sm_ref[...])

    def gate(branch, kvh):
        cols = [sig[:, branch * Q_HEADS + h:branch * Q_HEADS + h + 1] for h in range(kvh * GROUP, (kvh + 1) * GROUP)]
        return jnp.concatenate([jnp.broadcast_to(c, (TQ, LANES)) for c in cols], axis=0)

    def stack(ref, kvh):
        return jnp.concatenate([ref[:, h * LANES:(h + 1) * LANES] for h in range(kvh * GROUP, (kvh + 1) * GROUP)], axis=0)

    q_raw = [stack(q_ref, k) for k in kvs]
    q_rot = [stack(qr_ref, k) for k in kvs]

    ck, cv = ck_s[...], cv_s[...]
    s = [jnp.where(cm, _mm(q_raw[k], ck, _NT), MASK_VALUE) for k in kvs]
    e = [jnp.exp(s[k] - jnp.max(s[k], axis=1, keepdims=True)) for k in kvs]
    p = [e[k] / jnp.sum(e[k], axis=1, keepdims=True) * cm_f for k in kvs]
    o_sum = [gate(0, k) * _mm(p[k], cv) for k in kvs]
    p_sum = jnp.concatenate([p[k][0:TQ] + p[k][TQ:2 * TQ] + p[k][2 * TQ:3 * TQ] + p[k][3 * TQ:4 * TQ] for k in kvs],
                            axis=0)
    ns_rows = 32
    nq = KV_HEADS * TQ
    imp_t = _mm_exact_lhs(ovt_ref[...], p_sum, _NT)[0:ns_rows]
    n_idx = lax.broadcasted_iota(jnp.int32, (ns_rows, nq), 0)
    qb_t = (t0 + lax.broadcasted_iota(jnp.int32, (ns_rows, nq), 1) % TQ) // SEL_BLOCK
    sel_t = _select_blocks(imp_t, n_idx, qb_t, ns_rows)
    sel_t = jnp.concatenate([sel_t, jnp.zeros((LANES - ns_rows, nq), F32)], axis=0).astype(BF16)
    eye = _eye(TQ, BF16)
    sel = [_mm(eye, sel_t[:, k * TQ:(k + 1) * TQ], _NT).astype(BF16) for k in kvs]

    col_s = lax.broadcasted_iota(jnp.int32, (TQ, CK_SLC), 1)
    row_s = lax.broadcasted_iota(jnp.int32, (TQ, CK_SLC), 0)
    n_chunks = (t0 + TQ + CK_SLC - 1) // CK_SLC
    mrun_s[...] = jnp.full(mrun_s.shape, MASK_VALUE, F32)

    def score_chunks(js):
        k0 = [pl.multiple_of(j * CK_SLC, CK_SLC) for j in js]
        k_c = [kvb_ref[pl.ds(k0[n], CK_SLC), KVB_SK:KVB_SK + LANES] for n in range(len(js))]
        sc = [[_mm(q_rot[k], k_c[n], _NT) for k in kvs] for n in range(len(js))]
        picked = [[_mm(sel[k], e_ref[js[n]]) for k in kvs] for n in range(len(js))]
        for k in kvs:
            run = mrun_s[k]
            for n, j in enumerate(js):
                causal = k0[n] + col_s <= t0 + row_s
                bias = jnp.where((picked[n][k] > 0.5) & causal, 0.0, MASK_VALUE)
                sk = sc[n][k] + _tile_rows(bias, GROUP)
                s_s[k, j] = sk
                run = jnp.maximum(run, jnp.maximum(sk[:, :LANES], sk[:, LANES:]))
            mrun_s[k] = run

    def value_chunks(js):
        k0 = [pl.multiple_of(j * CK_SLC, CK_SLC) for j in js]
        v_c = [kvb_ref[pl.ds(k0[n], CK_SLC), KVB_SV:KVB_SV + 2 * LANES] for n in range(len(js))]
        pr = [[jnp.exp(s_s[k, js[n]] - _tile_lanes(m_s[k], CK_SLC // LANES)) for k in kvs] for n in range(len(js))]
        pv = [[_mm(pr[n][k], v_c[n]) for k in kvs] for n in range(len(js))]
        for k in kvs:
            acc_s[k] += sum(pv[n][k] for n in range(len(js)))

    def paired(fn):
        def pair_body(jj, carry):
            fn([2 * jj, 2 * jj + 1])
            return carry

        lax.fori_loop(0, n_chunks // 2, pair_body, 0)

        @pl.when(n_chunks % 2 == 1)
        def _():
            fn([n_chunks - 1])

    paired(score_chunks)
    for k in kvs:
        m_s[k] = jnp.broadcast_to(jnp.max(mrun_s[k], axis=1, keepdims=True), (rows, LANES))
    acc_s[...] = jnp.zeros(acc_s.shape, F32)

    paired(value_chunks)
    for k in kvs:
        acc = acc_s[k]
        o_sum[k] = o_sum[k] + gate(1, k) * (acc[:, :LANES] / acc[:, LANES:])

    ks0 = pl.multiple_of(jnp.maximum(t0 - WINDOW, 0), TQ)
    k_w = kvb_ref[pl.ds(ks0, WIN_SPAN), KVB_WK:KVB_WK + LANES]
    v_w = kvb_ref[pl.ds(ks0, WIN_SPAN), KVB_WV:KVB_WV + 2 * LANES]
    diff = (t0 + lax.broadcasted_iota(jnp.int32, (TQ, WIN_SPAN), 0)) - (ks0 + lax.broadcasted_iota(jnp.int32, (TQ, WIN_SPAN), 1))
    bias_w = _tile_rows(jnp.where((diff >= 0) & (diff <= WINDOW), 0.0, MASK_VALUE), GROUP)
    sw = [_mm(q_rot[k], k_w, _NT) + bias_w for k in kvs]
    pw = [jnp.exp(sw[k] - jnp.max(sw[k], axis=1, keepdims=True)) for k in kvs]
    rw = [_mm(pw[k], v_w) for k in kvs]
    for k in kvs:
        o_all = o_sum[k] + gate(2, k) * (rw[k][:, :LANES] / rw[k][:, LANES:])
        for pair, slab in enumerate(_merge_heads(o_all, k, TQ)):
            lo = (k * (GROUP // 2) + pair) * LANES
            o_ref[:, lo:lo + LANES] = (slab * _silu(zn_ref[:, lo:lo + LANES])).astype(o_ref.dtype)


def _nsa_prompt(q, qr, kvb, ck, cv, sm, zn, cmp_k_w, cmp_v_w, batch, seq):
    nt = seq // TQ
    ns = seq // SEL_BLOCK
    sub_w = CMP_STRIDE * KV_WIDTH
    keys = np.arange(seq)
    e = (np.arange(LANES)[None, :, None] == (keys // SEL_BLOCK).reshape(seq // CK_SLC, 1, CK_SLC))
    e = jnp.asarray(e, BF16)
    tile = lambda b, i: (b * nt + i, 0)
    per_b = lambda b, i: (b, 0)
    per_b3 = lambda b, i: (b, 0, 0)
    fix2 = lambda b, i: (0, 0)
    fix3 = lambda b, i: (0, 0, 0)
    wspecs = [pl.BlockSpec((sub_w, 4 * CMP_HIDDEN), fix2),
              pl.BlockSpec((2 * CMP_HIDDEN, KV_WIDTH), fix2),
              pl.BlockSpec((SUBLANES, sub_w), fix2)]
    rows = GROUP * TQ
    return pl.pallas_call(
        _nsa_prompt_kernel,
        grid=(batch, nt),
        in_specs=[pl.BlockSpec((TQ, QPAD_WIDTH), tile),
                  pl.BlockSpec((TQ, QPAD_WIDTH), tile),
                  pl.BlockSpec((seq, KVB_WIDTH), per_b),
                  pl.BlockSpec((1, seq // CMP_STRIDE, sub_w), per_b3),
                  pl.BlockSpec((1, seq // CMP_STRIDE, sub_w), per_b3),
                  pl.BlockSpec((TQ, LANES), tile),
                  pl.BlockSpec((TQ, NSA_WIDTH), tile)] + wspecs + wspecs + [
                  pl.BlockSpec((LANES, LANES), fix2),
                  pl.BlockSpec((seq // CK_SLC, LANES, CK_SLC), fix3)],
        out_specs=pl.BlockSpec((TQ, NSA_WIDTH), tile),
        out_shape=jax.ShapeDtypeStruct((batch * seq, NSA_WIDTH), BF16),
        scratch_shapes=[pltpu.VMEM((N_SUB, KV_WIDTH), BF16), pltpu.VMEM((N_SUB, KV_WIDTH), BF16),
                        pltpu.VMEM((KV_HEADS, seq // CK_SLC, rows, CK_SLC), F32),
                        pltpu.VMEM((KV_HEADS, rows, LANES), F32), pltpu.VMEM((KV_HEADS, rows, LANES), F32),
                        pltpu.VMEM((KV_HEADS, rows, 2 * LANES), F32)],
        compiler_params=pltpu.CompilerParams(dimension_semantics=("arbitrary", "arbitrary"),
                                             vmem_limit_bytes=VMEM_LIMIT_BYTES),
        name="nsa_prompt",
    )(q, qr, kvb, ck.reshape(batch, seq // CMP_STRIDE, sub_w), cv.reshape(batch, seq // CMP_STRIDE, sub_w),
      sm, zn, *cmp_k_w, *cmp_v_w, _overlap_t(ns), e)


def _softmax_rows(s):
    m = jnp.max(s, axis=1, keepdims=True)
    e = jnp.exp(s - m)
    return e / jnp.sum(e, axis=1, keepdims=True)


def _nsa_sample_kernel(pt_ref, q_ref, qr_ref, skn_ref, svn_ref, wkn_ref, wvn_ref, sm_ref, zn_ref,
                       wkc_ref, wvc_ref, *rest, n_pages, page, past, n_new, win_keys):
    del pt_ref
    cmpk, cmpv = rest[0:n_pages], rest[n_pages:2 * n_pages]
    slck, slcv = rest[2 * n_pages:3 * n_pages], rest[3 * n_pages:4 * n_pages]
    wk1_ref, wk2_ref, pek_ref, wv1_ref, wv2_ref, pev_ref, ovt_ref, e_ref = rest[4 * n_pages:4 * n_pages + 8]
    o_ref, wko_ref, wvo_ref = rest[4 * n_pages + 8:4 * n_pages + 11]
    kall, vall, kwin, vwin = rest[4 * n_pages + 11:]
    nr = SAMPLE_ROWS
    rows = Q_HEADS * nr
    all_keys = kall.shape[0]
    win_buf = wkc_ref.shape[1]
    zeros8 = jnp.zeros((nr, LANES), F32)

    for p in range(n_pages):
        kall[p * page:(p + 1) * page, :] = slck[p][0].astype(BF16)
        vall[p * page:(p + 1) * page, :] = slcv[p][0].astype(BF16)
    kall[past:past + 2 * nr, :] = jnp.concatenate([skn_ref[...], zeros8], axis=0).astype(BF16)
    vall[past:past + 2 * nr, :] = jnp.concatenate([svn_ref[...], zeros8], axis=0).astype(BF16)
    kall[past + 2 * nr:, :] = jnp.zeros((all_keys - past - 2 * nr, LANES), BF16)
    vall[past + 2 * nr:, :] = jnp.zeros((all_keys - past - 2 * nr, LANES), BF16)
    kwin[0:win_buf, :] = wkc_ref[0].astype(BF16)
    vwin[0:win_buf, :] = wvc_ref[0].astype(BF16)
    kwin[win_buf:win_buf + 2 * nr, :] = jnp.concatenate([wkn_ref[...], zeros8], axis=0).astype(BF16)
    vwin[win_buf:win_buf + 2 * nr, :] = jnp.concatenate([wvn_ref[...], zeros8], axis=0).astype(BF16)
    kwin[win_buf + 2 * nr:, :] = jnp.zeros((win_keys - win_buf - 2 * nr, LANES), BF16)
    vwin[win_buf + 2 * nr:, :] = jnp.zeros((win_keys - win_buf - 2 * nr, LANES), BF16)

    ck = _compress(jnp.concatenate([r[0] for r in cmpk], axis=0), wk1_ref, wk2_ref, pek_ref)
    cv = _compress(jnp.concatenate([r[0] for r in cmpv], axis=0), wv1_ref, wv2_ref, pev_ref)

    q_raw = jnp.concatenate([q_ref[:, h * LANES:(h + 1) * LANES] for h in range(Q_HEADS)], axis=0)
    q_rot = jnp.concatenate([qr_ref[:, h * LANES:(h + 1) * LANES] for h in range(Q_HEADS)], axis=0)

    tok = lax.broadcasted_iota(jnp.int32, (nr, LANES), 0)
    col = lax.broadcasted_iota(jnp.int32, (nr, LANES), 1)
    cvalid = (CMP_STRIDE * col + (CMP_LEN - 1) <= past + tok) & (col < N_CMP)
    o_cmp, p = _cmp_attention(q_raw, ck, cv, cvalid, Q_HEADS)

    p_sum = []
    for kvh in range(KV_HEADS):
        base = kvh * GROUP * nr
        p_sum.append(sum(p[base + g * nr:base + (g + 1) * nr] for g in range(GROUP)))
    p_sum = jnp.concatenate(p_sum, axis=0)
    ns = (past + n_new + SEL_BLOCK - 1) // SEL_BLOCK
    ns_rows = -(-ns // SUBLANES) * SUBLANES
    nq = KV_HEADS * nr
    imp_t = _mm_exact_lhs(ovt_ref[...], p_sum, _NT)[0:ns_rows]
    n_idx = lax.broadcasted_iota(jnp.int32, (ns_rows, nq), 0)
    qb_t = (past + lax.broadcasted_iota(jnp.int32, (ns_rows, nq), 1) % nr) // SEL_BLOCK
    sel_t = _select_blocks(imp_t, n_idx, qb_t, ns)
    sel_t = jnp.concatenate([sel_t, jnp.zeros((LANES - ns_rows, nq), F32)], axis=0)
    sel = _mm(_eye(nq, BF16), sel_t, _NT)
    sel_rows = jnp.concatenate([_tile_rows(sel[kvh * nr:(kvh + 1) * nr], GROUP) for kvh in range(KV_HEADS)], axis=0)

    picked = _mm(sel_rows, e_ref[...])
    q_pos = past + lax.broadcasted_iota(jnp.int32, (rows, all_keys), 0) % nr
    k_pos = lax.broadcasted_iota(jnp.int32, (rows, all_keys), 1)
    ok = (picked > 0.5) & (k_pos <= q_pos)
    s = _mm(q_rot, kall[...], _NT) + jnp.where(ok, 0.0, MASK_VALUE)
    o_slc = _mm(_softmax_rows(s), vall[...])

    q_pos = past + lax.broadcasted_iota(jnp.int32, (rows, win_keys), 0) % nr
    k_pos = past - win_buf + lax.broadcasted_iota(jnp.int32, (rows, win_keys), 1)
    diff = q_pos - k_pos
    ok = (diff >= 0) & (diff <= WINDOW)
    s = _mm(q_rot, kwin[...], _NT) + jnp.where(ok, 0.0, MASK_VALUE)
    o_win = _mm(_softmax_rows(s), vwin[...])

    sig = _sigmoid(sm_ref[...])

    def gate(branch):
        cols = [sig[:, branch * Q_HEADS + h:branch * Q_HEADS + h + 1] for h in range(Q_HEADS)]
        return jnp.concatenate([jnp.broadcast_to(c, (nr, LANES)) for c in cols], axis=0)

    o_all = gate(0) * o_cmp + gate(1) * o_slc + gate(2) * o_win
    for kvh in range(KV_HEADS):
        o_kv = o_all[kvh * GROUP * nr:(kvh + 1) * GROUP * nr]
        for pair, slab in enumerate(_merge_heads(o_kv, kvh, nr)):
            lo = (kvh * (GROUP // 2) + pair) * LANES
            o_ref[:, lo:lo + LANES] = (slab * _silu(zn_ref[:, lo:lo + LANES])).astype(o_ref.dtype)

    for cache_ref, new_ref, out_ref in ((wkc_ref, wkn_ref, wko_ref), (wvc_ref, wvn_ref, wvo_ref)):
        shifted = pltpu.roll(cache_ref[0], win_buf - n_new, 0)
        out_ref[0] = shifted
        tail = jnp.where(tok < nr - n_new, shifted[win_buf - nr:], pltpu.roll(new_ref[...], nr - n_new, 0))
        out_ref[0, win_buf - nr:, :] = tail


def _nsa_sample(q, qr, sk, sv, wk, wv, sm, zn, win_k, win_v, cmp_k, cmp_v, slc_k, slc_v, page_table,
                cmp_k_w, cmp_v_w, n_new):
    batch, n_pages = page_table.shape
    n_pool, page = cmp_k.shape[0], cmp_k.shape[1]
    past = n_pages * page
    nr = SAMPLE_ROWS
    sub_w = CMP_STRIDE * KV_WIDTH
    win_buf = win_k.shape[1]
    all_keys = -(-(past + 2 * nr) // LANES) * LANES
    win_keys = -(-(win_buf + 2 * nr) // LANES) * LANES
    ns = (past + n_new + SEL_BLOCK - 1) // SEL_BLOCK
    e = jnp.asarray(np.arange(LANES)[:, None] == (np.arange(all_keys) // SEL_BLOCK)[None, :], BF16)
    cmp_k = cmp_k.reshape(n_pool, page // CMP_STRIDE, sub_w)
    cmp_v = cmp_v.reshape(n_pool, page // CMP_STRIDE, sub_w)
    slc_k = slc_k.reshape(n_pool, page, KV_WIDTH)
    slc_v = slc_v.reshape(n_pool, page, KV_WIDTH)
    win_k = win_k.reshape(batch, win_buf, KV_WIDTH)
    win_v = win_v.reshape(batch, win_buf, KV_WIDTH)

    row = lambda b, pt: (b, 0)
    per_b3 = lambda b, pt: (b, 0, 0)
    fix2 = lambda b, pt: (0, 0)
    page_map = lambda p: (lambda b, pt: (pt[b, p], 0, 0))
    wspecs = [pl.BlockSpec((sub_w, 4 * CMP_HIDDEN), fix2),
              pl.BlockSpec((2 * CMP_HIDDEN, KV_WIDTH), fix2),
              pl.BlockSpec((SUBLANES, sub_w), fix2)]
    in_specs = ([pl.BlockSpec((nr, QPAD_WIDTH), row)] * 2 + [pl.BlockSpec((nr, KV_WIDTH), row)] * 4
                + [pl.BlockSpec((nr, LANES), row), pl.BlockSpec((nr, NSA_WIDTH), row)]
                + [pl.BlockSpec((1, win_buf, KV_WIDTH), per_b3)] * 2
                + [pl.BlockSpec((1, page // CMP_STRIDE, sub_w), page_map(p)) for p in range(n_pages)] * 1
                + [pl.BlockSpec((1, page // CMP_STRIDE, sub_w), page_map(p)) for p in range(n_pages)]
                + [pl.BlockSpec((1, page, KV_WIDTH), page_map(p)) for p in range(n_pages)]
                + [pl.BlockSpec((1, page, KV_WIDTH), page_map(p)) for p in range(n_pages)]
                + wspecs + wspecs
                + [pl.BlockSpec((LANES, LANES), fix2), pl.BlockSpec((LANES, all_keys), fix2)])
    grid_spec = pltpu.PrefetchScalarGridSpec(
        num_scalar_prefetch=1,
        grid=(batch,),
        in_specs=in_specs,
        out_specs=[pl.BlockSpec((nr, NSA_WIDTH), row),
                   pl.BlockSpec((1, win_buf, KV_WIDTH), per_b3),
                   pl.BlockSpec((1, win_buf, KV_WIDTH), per_b3)],
        scratch_shapes=[pltpu.VMEM((all_keys, KV_WIDTH), BF16), pltpu.VMEM((all_keys, KV_WIDTH), BF16),
                        pltpu.VMEM((win_keys, KV_WIDTH), BF16), pltpu.VMEM((win_keys, KV_WIDTH), BF16)])
    return pl.pallas_call(
        functools.partial(_nsa_sample_kernel, n_pages=n_pages, page=page, past=past, n_new=n_new,
                          win_keys=win_keys),
        grid_spec=grid_spec,
        out_shape=[jax.ShapeDtypeStruct((batch * nr, NSA_WIDTH), F32),
                   jax.ShapeDtypeStruct((batch, win_buf, KV_WIDTH), F32),
                   jax.ShapeDtypeStruct((batch, win_buf, KV_WIDTH), F32)],
        compiler_params=pltpu.CompilerParams(dimension_semantics=("arbitrary",),
                                             vmem_limit_bytes=VMEM_LIMIT_BYTES),
        name="nsa_sample",
    )(page_table, q, qr, sk, sv, wk, wv, sm, zn, win_k, win_v,
      *([cmp_k] * n_pages), *([cmp_v] * n_pages), *([slc_k] * n_pages), *([slc_v] * n_pages),
      *cmp_k_w, *cmp_v_w, _overlap_t(ns), e)


def _compress_weights_t(pe, w1, w2):
    w1big, w2big, pe2 = _compress_weights(pe, w1, w2)
    return w1big.T, w2big.T, pe2


def _nsa_sample_t_kernel(pt_ref, q_ref, qr_ref, skn_ref, svn_ref, wkn_ref, wvn_ref, sm_ref, zn_ref,
                         wkc_ref, wvc_ref, cmpk_hbm, cmpv_hbm, slck_hbm, slcv_hbm,
                         w1k_ref, w2k_ref, pek_ref, w1v_ref, w2v_ref, pev_ref, ovt_ref, e_ref,
                         o_ref, wko_ref, wvo_ref, kt_s, vt_s, kwt_s, vwt_s, pages_s, sems,
                         *, nb, n_pages, page, past, n_new):
    step = pl.program_id(0)
    last = pl.num_programs(0) - 1
    slot = step % 2
    caches = (cmpk_hbm, cmpv_hbm, slck_hbm, slcv_hbm)
    page_ids = [(j, c, p) for j in range(nb) for c in range(len(caches)) for p in range(n_pages)]

    def page_copy(at_step, at_slot, j, c, p):
        src = caches[c].at[pt_ref[at_step * nb + j, p]]
        return pltpu.make_async_copy(src, pages_s.at[at_slot, j, c, p], sems.at[at_slot, c])

    @pl.when(step == 0)
    def _():
        for ids in page_ids:
            page_copy(0, 0, *ids).start()

    nxt = jnp.minimum(step + 1, last)
    for ids in page_ids:
        page_copy(nxt, 1 - slot, *ids).start()
    for ids in page_ids:
        page_copy(step, slot, *ids).wait()

    cmpk, cmpv, slck, slcv = ([pages_s.at[slot, j, c, p] for j in range(nb) for p in range(n_pages)]
                              for c in range(len(caches)))
    nr = SAMPLE_ROWS
    rows = Q_HEADS * nr
    win_buf = wkc_ref.shape[2]
    bs = range(nb)
    hid = 2 * CMP_HIDDEN
    zpad = jnp.zeros((LANES - nr, LANES), F32)

    def new_rows(ref, b):
        return jnp.concatenate([ref[b * nr:(b + 1) * nr, :], zpad], axis=0).astype(BF16)

    def stack(ref, b):
        return jnp.concatenate([ref[b * nr:(b + 1) * nr, h * LANES:(h + 1) * LANES] for h in range(Q_HEADS)],
                               axis=0).astype(BF16)

    for b in bs:
        for p in range(n_pages):
            kt_s[b, :, p * page:(p + 1) * page] = slck[b * n_pages + p][...].astype(BF16)
            vt_s[b, :, p * page:(p + 1) * page] = slcv[b * n_pages + p][...].astype(BF16)
        kwt_s[b] = wkc_ref[b].astype(BF16)
        vwt_s[b] = wvc_ref[b].astype(BF16)
    mi = lax.broadcasted_iota(jnp.int32, (page, page), 0)
    ki = lax.broadcasted_iota(jnp.int32, (page, page), 1)
    per_page = page // CMP_STRIDE
    perm = (ki == CMP_STRIDE * (mi % per_page) + mi // per_page).astype(BF16)
    def transposed_pages(pages, b):
        out = []
        for p in range(0, n_pages, 2):
            pair = jnp.concatenate([pages[b * n_pages + p][...], pages[b * n_pages + p + 1][...]], axis=0)
            both = _mm(perm, pair, _NT)
            out += [both[:, :page], both[:, page:]]
        return out

    rows_t = [[transposed_pages(pages, b) for pages in (cmpk, cmpv)] for b in bs]

    q_raw = [stack(q_ref, b) for b in bs]
    q_rot = [stack(qr_ref, b) for b in bs]

    sw = [jnp.concatenate([_mm(q_rot[b], kwt_s[b]), _mm(q_rot[b], new_rows(wkn_ref, b), _NT)], axis=1) for b in bs]
    ss = [jnp.concatenate([_mm(q_rot[b], kt_s[b]), _mm(q_rot[b], new_rows(skn_ref, b), _NT)], axis=1) for b in bs]

    def sub_blocks(b, c):
        cols = [jnp.concatenate([rows_t[b][c][p][l * per_page:(l + 1) * per_page] for p in range(n_pages)], axis=0)
                for l in range(CMP_STRIDE)]
        return jnp.concatenate(cols, axis=1)

    ckt, cvt = [], []
    def activate(t):
        bias = t[0:hid, LANES:LANES + 1] + t[hid:2 * hid, LANES + 1:LANES + 2]
        return _silu(t[0:hid, 0:LANES] + pltpu.roll(t[hid:2 * hid, 0:LANES], N_SUB - 1, 1) + bias)

    for c, (w1_ref, w2_ref, pe_ref, out) in enumerate(((w1k_ref, w2k_ref, pek_ref, ckt), (w1v_ref, w2v_ref, pev_ref, cvt))):
        pe_rows = pe_ref[...]
        ht = [_mm(w1_ref[...], jnp.concatenate([sub_blocks(b, c).astype(BF16), pe_rows], axis=0), _NT) for b in bs]
        out.extend(_mm(w2_ref[...], activate(ht[b])) for b in bs)

    tok = lax.broadcasted_iota(jnp.int32, (nr, LANES), 0)
    col = lax.broadcasted_iota(jnp.int32, (nr, LANES), 1)
    cm = _tile_rows((CMP_STRIDE * col + (CMP_LEN - 1) <= past + tok) & (col < N_CMP), Q_HEADS)
    cm_f = cm.astype(F32)
    sc = [jnp.where(cm, _mm(q_raw[b], ckt[b]), MASK_VALUE) for b in bs]
    ec = [jnp.exp(sc[b] - jnp.max(sc[b], axis=1, keepdims=True)) for b in bs]
    pc = [ec[b] / jnp.sum(ec[b], axis=1, keepdims=True) * cm_f for b in bs]
    o_cmp = [_mm(pc[b], cvt[b], _NT) for b in bs]
    p_sum = jnp.concatenate(
        [sum(pc[b][(kvh * GROUP + g) * nr:(kvh * GROUP + g + 1) * nr] for g in range(GROUP))
         for b in bs for kvh in range(KV_HEADS)], axis=0)
    ns = (past + n_new + SEL_BLOCK - 1) // SEL_BLOCK
    ns_rows = -(-ns // SUBLANES) * SUBLANES
    nq = nb * KV_HEADS * nr
    imp_t = _mm_exact_lhs(ovt_ref[...], p_sum, _NT)[0:ns_rows]
    n_idx = lax.broadcasted_iota(jnp.int32, (ns_rows, nq), 0)
    qb_t = (past + lax.broadcasted_iota(jnp.int32, (ns_rows, nq), 1) % nr) // SEL_BLOCK
    sel_t = _select_blocks(imp_t, n_idx, qb_t, ns)
    sel_t = jnp.concatenate([sel_t, jnp.zeros((LANES - ns_rows, nq), F32)], axis=0)
    sel = _mm(_eye(nq, BF16), sel_t, _NT)

    all_keys = past + LANES
    q_pos = past + lax.broadcasted_iota(jnp.int32, (rows, all_keys), 0) % nr
    causal = lax.broadcasted_iota(jnp.int32, (rows, all_keys), 1) <= q_pos
    o_slc = []
    for b in bs:
        sel_rows = jnp.concatenate(
            [_tile_rows(sel[(b * KV_HEADS + kvh) * nr:(b * KV_HEADS + kvh + 1) * nr], GROUP) for kvh in range(KV_HEADS)],
            axis=0)
        ok = (_mm(sel_rows, e_ref[...]) > 0.5) & causal
        s = ss[b] + jnp.where(ok, 0.0, MASK_VALUE)
        pr = jnp.exp(s - jnp.max(s, axis=1, keepdims=True))
        o = _mm(pr[:, :past], vt_s[b], _NT) + _mm(pr[:, past:], new_rows(svn_ref, b))
        o_slc.append(o / jnp.sum(pr, axis=1, keepdims=True))

    win_keys = win_buf + LANES
    q_pos = past + lax.broadcasted_iota(jnp.int32, (rows, win_keys), 0) % nr
    diff = q_pos - (past - win_buf + lax.broadcasted_iota(jnp.int32, (rows, win_keys), 1))
    bias_w = jnp.where((diff >= 0) & (diff <= WINDOW), 0.0, MASK_VALUE)
    o_win = []
    for b in bs:
        s = sw[b] + bias_w
        pr = jnp.exp(s - jnp.max(s, axis=1, keepdims=True))
        o = _mm(pr[:, :win_buf], vwt_s[b], _NT) + _mm(pr[:, win_buf:], new_rows(wvn_ref, b))
        o_win.append(o / jnp.sum(pr, axis=1, keepdims=True))

    for b in bs:
        sig = _sigmoid(sm_ref[b * nr:(b + 1) * nr, :])

        def gate(branch):
            cols = [sig[:, branch * Q_HEADS + h:branch * Q_HEADS + h + 1] for h in range(Q_HEADS)]
            return jnp.concatenate([jnp.broadcast_to(c, (nr, LANES)) for c in cols], axis=0)

        o_all = gate(0) * o_cmp[b] + gate(1) * o_slc[b] + gate(2) * o_win[b]
        for kvh in range(KV_HEADS):
            o_kv = o_all[kvh * GROUP * nr:(kvh + 1) * GROUP * nr]
            for pair, slab in enumerate(_merge_heads(o_kv, kvh, nr)):
                lo = (kvh * (GROUP // 2) + pair) * LANES
                o_ref[b * nr:(b + 1) * nr, lo:lo + LANES] = slab * _silu(zn_ref[b * nr:(b + 1) * nr, lo:lo + LANES])

    eye_bf = _eye(LANES, BF16)
    lane = lax.broadcasted_iota(jnp.int32, (KV_WIDTH, LANES), 1)
    for cache_ref, new_ref, out_ref in ((wkc_ref, wkn_ref, wko_ref), (wvc_ref, wvn_ref, wvo_ref)):
        for b in bs:
            shifted = pltpu.roll(cache_ref[b], win_buf - n_new, 1)
            new_pad = jnp.concatenate([new_ref[b * nr:(b + 1) * nr, :], zpad], axis=0)
            new_t = _mm_exact_lhs(eye_bf, new_pad, _NT)
            tail = jnp.where(lane >= LANES - n_new, pltpu.roll(new_t, LANES - n_new, 1), shifted[:, win_buf - LANES:])
            out_ref[b] = jnp.concatenate([shifted[:, :win_buf - LANES], tail], axis=1)

    @pl.when(step == last)
    def _():
        for ids in page_ids:
            page_copy(nxt, 1 - slot, *ids).wait()


def _nsa_sample_t(q, qr, sk, sv, wk, wv, sm, zn, win_k, win_v, cmp_k, cmp_v, slc_k, slc_v, page_table,
                  cmp_k_w, cmp_v_w, n_new, nb):
    batch, n_pages = page_table.shape
    n_pool, page = cmp_k.shape[0], cmp_k.shape[1]
    past = n_pages * page
    nr = SAMPLE_ROWS
    win_buf = win_k.shape[1]
    ns = (past + n_new + SEL_BLOCK - 1) // SEL_BLOCK
    all_keys = past + LANES
    e = jnp.asarray(np.arange(LANES)[:, None] == (np.arange(all_keys) // SEL_BLOCK)[None, :], BF16)
    tview = lambda c: jnp.transpose(c, (0, 2, 3, 1)).reshape(c.shape[0], KV_WIDTH, c.shape[1])
    cmp_k, cmp_v, slc_k, slc_v, win_k, win_v = (tview(c) for c in (cmp_k, cmp_v, slc_k, slc_v, win_k, win_v))

    row = lambda b, pt: (b, 0)
    per_b3 = lambda b, pt: (b, 0, 0)
    fix2 = lambda b, pt: (0, 0)
    sub_w = CMP_STRIDE * KV_WIDTH
    wspecs = [pl.BlockSpec((4 * CMP_HIDDEN, sub_w), fix2),
              pl.BlockSpec((KV_WIDTH, 2 * CMP_HIDDEN), fix2),
              pl.BlockSpec((LANES, sub_w), fix2)]
    in_specs = ([pl.BlockSpec((nb * nr, QPAD_WIDTH), row)] * 2 + [pl.BlockSpec((nb * nr, KV_WIDTH), row)] * 4
                + [pl.BlockSpec((nb * nr, LANES), row), pl.BlockSpec((nb * nr, NSA_WIDTH), row)]
                + [pl.BlockSpec((nb, KV_WIDTH, win_buf), per_b3)] * 2
                + [pl.BlockSpec(memory_space=pl.ANY)] * 4 + wspecs + wspecs
                + [pl.BlockSpec((LANES, LANES), fix2), pl.BlockSpec((LANES, all_keys), fix2)])
    grid_spec = pltpu.PrefetchScalarGridSpec(
        num_scalar_prefetch=1,
        grid=(batch // nb,),
        in_specs=in_specs,
        out_specs=[pl.BlockSpec((nb * nr, NSA_WIDTH), row),
                   pl.BlockSpec((nb, KV_WIDTH, win_buf), per_b3), pl.BlockSpec((nb, KV_WIDTH, win_buf), per_b3)],
        scratch_shapes=[pltpu.VMEM((nb, KV_WIDTH, past), BF16), pltpu.VMEM((nb, KV_WIDTH, past), BF16),
                        pltpu.VMEM((nb, KV_WIDTH, win_buf), BF16), pltpu.VMEM((nb, KV_WIDTH, win_buf), BF16),
                        pltpu.VMEM((2, nb, 4, n_pages, KV_WIDTH, page), F32),
                        pltpu.SemaphoreType.DMA((2, 4))])
    return pl.pallas_call(
        functools.partial(_nsa_sample_t_kernel, nb=nb, n_pages=n_pages, page=page, past=past, n_new=n_new),
        grid_spec=grid_spec,
        out_shape=[jax.ShapeDtypeStruct((batch * nr, NSA_WIDTH), F32),
                   jax.ShapeDtypeStruct((batch, KV_WIDTH, win_buf), F32),
                   jax.ShapeDtypeStruct((batch, KV_WIDTH, win_buf), F32)],
        compiler_params=pltpu.CompilerParams(dimension_semantics=("arbitrary",),
                                             vmem_limit_bytes=VMEM_LIMIT_BYTES),
        name="nsa_sample",
    )(page_table, q, qr, sk, sv, wk, wv, sm, zn, win_k, win_v, cmp_k, cmp_v, slc_k, slc_v,
      *cmp_k_w, *cmp_v_w, _overlap_t(ns), e)


def _unit_lower_inverse(a, n_valid, hi):
    c = a.shape[0]
    inv = _eye(c, F32) - a
    power = a
    span = 2
    while span < n_valid:
        power = _mm(power, power, hi=hi)
        inv = inv + _mm(inv, power, hi=hi)
        span *= 2
    return inv


def _gdn_kernel(qkv_ref, sm_ref, zg_ref, conv0_ref, s0_ref, wc_ref, vec_ref, wg_ref,
                go_ref, xp_out_ref, st_ref, xp_s, *, bblk, chunk, n_valid, hi):
    c_idx = pl.program_id(1)

    @pl.when(c_idx == 0)
    def _():
        xp_s[:, 0:SUBLANES, :] = conv0_ref[...]
        st_ref[...] = s0_ref[...]

    wc = wc_ref[...]
    vec = vec_ref[...]
    row1 = lax.broadcasted_iota(jnp.int32, (chunk, LANES), 0)
    rr = lax.broadcasted_iota(jnp.int32, (chunk, chunk), 0)
    cc = lax.broadcasted_iota(jnp.int32, (chunk, chunk), 1)
    tri = rr >= cc
    tri_bf = tri.astype(BF16)
    eye_bf = _eye(LANES, BF16)

    acts, betas, decays, decay_ts, e_decs = [], [], [], [], []
    for b in range(bblk):
        xp_s[b, SUBLANES:SUBLANES + chunk, :] = qkv_ref[b]
        y = xp_s[b, SUBLANES:SUBLANES + chunk, :] * wc[GDN_CONV - 1:GDN_CONV, :]
        for j in range(GDN_CONV - 1):
            lo = SUBLANES - (GDN_CONV - 1) + j
            y = y + xp_s[b, lo:lo + chunk, :] * wc[j:j + 1, :]
        xp_out_ref[b] = xp_s[b]
        xp_s[b, 0:SUBLANES, :] = xp_s[b, chunk:chunk + SUBLANES, :]
        act = _silu(y)
        small = sm_ref[b]
        z = small + vec[1:2, :]
        softplus = jnp.maximum(z, 0.0) + jnp.log1p(jnp.exp(-jnp.abs(z)))
        g_all = -jnp.exp(vec[0:1, :]) * softplus
        if n_valid < chunk:
            valid = row1 < n_valid
            act = act * _tile_lanes(valid.astype(F32), GDN_CONV_CH // LANES)
            g_all = jnp.where(valid, g_all, 0.0)
        acts.append(act)
        betas.append(_sigmoid(small))
        decays.append(_mm_exact_lhs(tri_bf, g_all))
    for b in range(bblk):
        decay_ts.append(_mm_exact_lhs(eye_bf, decays[b], _NT))
        e_decs.append(jnp.exp(decays[b]))

    chains = [(b, h) for b in range(bblk) for h in range(GDN_HEADS)]
    qs, ks, kbs, dmasks, rhs_u, rhs_w, qds, kds, gls = [], [], [], [], [], [], [], [], []
    for b, h in chains:
        act = acts[b]
        qh = act[:, h * GDN_DK:(h + 1) * GDN_DK]
        kh = act[:, GDN_WIDTH + h * GDN_DK:GDN_WIDTH + (h + 1) * GDN_DK]
        vh = act[:, 2 * GDN_WIDTH + h * GDN_DV:2 * GDN_WIDTH + (h + 1) * GDN_DV]
        qh = qh * lax.rsqrt(jnp.sum(qh * qh, axis=-1, keepdims=True) + NORM_EPS) * (GDN_DK ** -0.5)
        kh = kh * lax.rsqrt(jnp.sum(kh * kh, axis=-1, keepdims=True) + NORM_EPS)
        beta = betas[b][:, SM_B + h:SM_B + h + 1]
        dcol = decays[b][:, SM_A + h:SM_A + h + 1]
        drow = decay_ts[b][SM_A + h:SM_A + h + 1, :]
        ed = e_decs[b][:, SM_A + h:SM_A + h + 1]
        dlast = decays[b][chunk - 1:chunk, SM_A + h:SM_A + h + 1]
        kb = kh * beta
        qs.append(qh)
        ks.append(kh)
        kbs.append(kb)
        dmasks.append(jnp.where(tri, jnp.exp(jnp.where(tri, dcol - drow, 0.0)), 0.0))
        rhs_u.append(vh * beta)
        rhs_w.append(kb * ed)
        qds.append(qh * ed)
        kds.append(kh * jnp.exp(dlast - dcol))
        gls.append(jnp.exp(dlast))

    n = len(chains)
    eye_c = _eye(chunk, F32)
    kkts = [_mm(kbs[i], ks[i], _NT, hi=hi) for i in range(n)]
    qks = [_mm(qs[i], ks[i], _NT, hi=hi) for i in range(n)]
    powers = [jnp.where(rr > cc, kkts[i] * dmasks[i], 0.0) for i in range(n)]
    qks = [qks[i] * dmasks[i] for i in range(n)]
    invs = [eye_c - powers[i] for i in range(n)]
    span = 2
    while span < n_valid:
        powers = [_mm(powers[i], powers[i], hi=hi) for i in range(n)]
        invs = [invs[i] + _mm(invs[i], powers[i], hi=hi) for i in range(n)]
        span *= 2
    us = [_mm(invs[i], rhs_u[i], hi=hi) for i in range(n)]
    ws = [_mm(invs[i], rhs_w[i], hi=hi) for i in range(n)]
    sts = [st_ref[b, h] for b, h in chains]
    v_news = [us[i] - _mm(ws[i], sts[i], hi=hi) for i in range(n)]
    os_ = [_mm(qds[i], sts[i], hi=hi) for i in range(n)]
    os_ = [os_[i] + _mm(qks[i], v_news[i], hi=hi) for i in range(n)]
    upd = [_mm(kds[i], v_news[i], _TN, hi=hi) for i in range(n)]
    for i, (b, h) in enumerate(chains):
        st_ref[b, h] = sts[i] * gls[i] + upd[i]
        o = os_[i]
        o = o * lax.rsqrt(jnp.mean(o * o, axis=-1, keepdims=True) + NORM_EPS) * wg_ref[...]
        o = o * _silu(zg_ref[b, :, h * GDN_DV:(h + 1) * GDN_DV])
        go_ref[b, :, h * GDN_DV:(h + 1) * GDN_DV] = o.astype(go_ref.dtype)


def _gdn(qkv, sm, zg, conv0, s0, w_conv, a_log, dt_bias, w_gnorm, batch, rows, bblk, chunk, n_valid, hi,
         out_dtype):
    nc = rows // chunk
    tile = lambda b, c: (b, c, 0)
    per_b3 = lambda b, c: (b, 0, 0)
    per_b4 = lambda b, c: (b, 0, 0, 0)
    fix2 = lambda b, c: (0, 0)
    wc = jnp.concatenate([w_conv, jnp.zeros((SUBLANES - GDN_CONV, GDN_CONV_CH), w_conv.dtype)], axis=0)
    vec = jnp.zeros((SUBLANES, LANES), F32)
    vec = vec.at[0, SM_A:SM_A + GDN_HEADS].set(a_log).at[1, SM_A:SM_A + GDN_HEADS].set(dt_bias)
    go, xp, st = pl.pallas_call(
        functools.partial(_gdn_kernel, bblk=bblk, chunk=chunk, n_valid=n_valid, hi=hi),
        grid=(batch // bblk, nc),
        in_specs=[pl.BlockSpec((bblk, chunk, GDN_CONV_CH), tile),
                  pl.BlockSpec((bblk, chunk, LANES), tile),
                  pl.BlockSpec((bblk, chunk, GDN_WIDTH), tile),
                  pl.BlockSpec((bblk, SUBLANES, GDN_CONV_CH), per_b3),
                  pl.BlockSpec((bblk, GDN_HEADS, GDN_DK, GDN_DV), per_b4),
                  pl.BlockSpec((SUBLANES, GDN_CONV_CH), fix2),
                  pl.BlockSpec((SUBLANES, LANES), fix2),
                  pl.BlockSpec((1, GDN_DV), fix2)],
        out_specs=[pl.BlockSpec((bblk, chunk, GDN_WIDTH), tile),
                   pl.BlockSpec((bblk, SUBLANES + chunk, GDN_CONV_CH), per_b3),
                   pl.BlockSpec((bblk, GDN_HEADS, GDN_DK, GDN_DV), per_b4)],
        out_shape=[jax.ShapeDtypeStruct((batch, rows, GDN_WIDTH), out_dtype),
                   jax.ShapeDtypeStruct((batch, SUBLANES + chunk, GDN_CONV_CH), F32),
                   jax.ShapeDtypeStruct((batch, GDN_HEADS, GDN_DK, GDN_DV), F32)],
        scratch_shapes=[pltpu.VMEM((bblk, SUBLANES + chunk, GDN_CONV_CH), F32)],
        compiler_params=pltpu.CompilerParams(dimension_semantics=("arbitrary", "arbitrary"),
                                             vmem_limit_bytes=VMEM_LIMIT_BYTES),
        name="gdn",
    )(qkv.reshape(batch, rows, GDN_CONV_CH), sm.reshape(batch, rows, LANES), zg.reshape(batch, rows, GDN_WIDTH),
      conv0, s0, wc, vec, w_gnorm.reshape(1, GDN_DV))
    return go.reshape(batch * rows, GDN_WIDTH), xp, st


PROMPT_TM = 512
SAMPLE_TM = 256
GDN_PROMPT_BBLK = 8
GDN_SAMPLE_BBLK = 8
NSA_SAMPLE_NB = 2


def _layer_prompt(h, lw, final, w_final, win_buf):
    w_norm, w_pack, cmp_k_w, cmp_v_w, w_conv, a_log, dt_bias, w_gnorm, w_out = lw
    batch, seq, d = h.shape
    x2d = h.reshape(batch * seq, d)
    tables = _rope_tables(jnp.arange(seq, dtype=jnp.int32))
    (q, qr, ck, cv, sk, sv, wk, wv, kvb, zn, qkv, zg, sm, ck_t, cv_t, sk_t, sv_t, wk_t, wv_t) = _project(
        x2d, w_norm, w_pack.astype(BF16), tables, seq, PROMPT_TM, False, BF16, True)
    nsa = _nsa_prompt(q, qr, kvb, ck, cv, sm, zn, cmp_k_w, cmp_v_w, batch, seq)
    conv0 = jnp.zeros((batch, SUBLANES, GDN_CONV_CH), F32)
    s0 = jnp.zeros((batch, GDN_HEADS, GDN_DK, GDN_DV), F32)
    go, xp, st = _gdn(qkv, sm, zg, conv0, s0, w_conv, a_log, dt_bias, w_gnorm,
                      batch, seq, GDN_PROMPT_BBLK, GDN_CHUNK, GDN_CHUNK, False, BF16)
    y = _out_project(x2d, nsa, go, w_out.astype(BF16), w_final, PROMPT_TM, final)
    conv_new = xp[:, SUBLANES + GDN_CHUNK - (GDN_CONV - 1):SUBLANES + GDN_CHUNK]
    from_t = lambda t: jnp.transpose(t.reshape(batch, KV_HEADS, HEAD_DIM, t.shape[-1]), (0, 3, 1, 2))
    lead = ((0, 0), (0, 0), (max(win_buf - seq, 0), 0))
    win = lambda t: from_t(jnp.pad(t, lead)[:, :, -win_buf:])
    return y.reshape(batch, seq, d), (from_t(ck_t), from_t(cv_t), from_t(sk_t), from_t(sv_t), win(wk_t), win(wv_t),
                                      conv_new, st)


def _layer_sample(h8, n_new, caches, page_table, lw, final, w_final):
    w_norm, w_pack, cmp_k_w, cmp_v_w, w_conv, a_log, dt_bias, w_gnorm, w_out = lw
    pe_rows = lambda pe2: jnp.pad(pe2, ((0, LANES - pe2.shape[0]), (0, 0))).astype(BF16)
    cmp_k_t = (cmp_k_w[0].T, cmp_k_w[1].T, pe_rows(cmp_k_w[2]))
    cmp_v_t = (cmp_v_w[0].T, cmp_v_w[1].T, pe_rows(cmp_v_w[2]))
    c_cmp_k, c_cmp_v, c_slc_k, c_slc_v, c_win_k, c_win_v, s_conv, s_gdn = caches
    batch, nr, d = h8.shape
    past = page_table.shape[1] * c_cmp_k.shape[1]
    x2d = h8.reshape(batch * nr, d)
    tables = _rope_tables(past + jnp.arange(nr, dtype=jnp.int32))
    tables = tuple(jnp.tile(t, (SAMPLE_TM // nr, 1)) for t in tables)
    (q, qr, ck, cv, sk, sv, wk, wv, _, zn, qkv, zg, sm) = _project(
        x2d, w_norm, w_pack, tables, SAMPLE_TM, SAMPLE_TM, True, F32, False)
    nsa, win_k_t, win_v_t = _nsa_sample_t(q, qr, sk, sv, wk, wv, sm, zn, c_win_k, c_win_v, c_cmp_k, c_cmp_v,
                                          c_slc_k, c_slc_v, page_table, cmp_k_t, cmp_v_t, n_new, NSA_SAMPLE_NB)
    conv0 = jnp.pad(s_conv, ((0, 0), (SUBLANES - (GDN_CONV - 1), 0), (0, 0)))
    go, xp, st = _gdn(qkv, sm, zg, conv0, s_gdn, w_conv, a_log, dt_bias, w_gnorm,
                      batch, nr, GDN_SAMPLE_BBLK, nr, n_new, True, F32)
    y = _out_project(x2d, nsa, go, w_out.astype(BF16), w_final, SAMPLE_TM, final)
    kv4 = lambda t: t.reshape(batch, nr, KV_HEADS, HEAD_DIM)[:, :n_new]
    from_t = lambda t: jnp.transpose(t.reshape(batch, KV_HEADS, HEAD_DIM, t.shape[-1]), (0, 3, 1, 2))
    conv_new = xp[:, SUBLANES + n_new - (GDN_CONV - 1):SUBLANES + n_new]
    return y.reshape(batch, nr, d), (kv4(ck), kv4(cv), kv4(sk), kv4(sv), from_t(win_k_t), from_t(win_v_t),
                                     conv_new, st)


def kernel(x_prompt, x_sample, cache_cmp_k, cache_cmp_v, cache_slc_k, cache_slc_v, cache_win_k, cache_win_v, state_conv, state_gdn, page_table, w_norm, w_in, pe_cmp_k, w_cmp_k1, w_cmp_k2, pe_cmp_v, w_cmp_v1, w_cmp_v2, w_conv, a_log, dt_bias, w_gdn_norm, w_out, w_final_norm):
    depth = w_in.shape[0]
    n_new = x_sample.shape[1]
    win_buf = cache_win_k.shape[2]
    h_p = x_prompt
    h_s = jnp.pad(x_sample, ((0, 0), (0, SAMPLE_ROWS - n_new), (0, 0)))
    st_p, st_s = [], []
    for layer in range(depth):
        lw = (w_norm[layer], _pack_w_in(w_in[layer]),
              _compress_weights(pe_cmp_k[layer], w_cmp_k1[layer], w_cmp_k2[layer]),
              _compress_weights(pe_cmp_v[layer], w_cmp_v1[layer], w_cmp_v2[layer]),
              w_conv[layer], a_log[layer], dt_bias[layer], w_gdn_norm[layer], w_out[layer])
        final = layer == depth - 1
        h_p, sp = _layer_prompt(h_p, lw, final, w_final_norm, win_buf)
        caches = (cache_cmp_k[layer], cache_cmp_v[layer], cache_slc_k[layer], cache_slc_v[layer],
                  cache_win_k[layer], cache_win_v[layer], state_conv[layer], state_gdn[layer])
        h_s, ss = _layer_sample(h_s, n_new, caches, page_table, lw, final, w_final_norm)
        st_p.append(sp)
        st_s.append(ss)
    outs = [h_p, h_s[:, :n_new]]
    for i in range(8):
        outs.append(jnp.stack([s[i] for s in st_p]))
        outs.append(jnp.stack([s[i] for s in st_s]))
    return tuple(outs)
```

```python
import functools

import numpy as np
import jax
import jax.numpy as jnp
from jax import lax
from jax.experimental import pallas as pl
from jax.experimental.pallas import tpu as pltpu

F32 = jnp.float32
BF16 = jnp.bfloat16

D_MODEL = 1024
HEAD_DIM = 64
Q_HEADS = 8
KV_HEADS = 2
GROUP = Q_HEADS // KV_HEADS
NSA_WIDTH = Q_HEADS * HEAD_DIM
KV_WIDTH = KV_HEADS * HEAD_DIM
CMP_LEN = 32
CMP_STRIDE = 16
CMP_HIDDEN = 128
SEL_BLOCK = 64
TOP_N = 8
WINDOW = 512
FORCE_BONUS = 1.0e4
ROT_DIM = HEAD_DIM // 4
ROPE_THETA = 500000.0
GDN_DK = 128
GDN_DV = 128
GDN_HEADS = 4
GDN_WIDTH = GDN_HEADS * GDN_DV
GDN_CONV = 4
GDN_CONV_CH = 3 * GDN_WIDTH
GDN_CHUNK = 64
NORM_EPS = 1e-6
MASK_VALUE = -1e30

LANES = 128
SUBLANES = 8
VMEM_LIMIT_BYTES = 56 * 1024 * 1024

QPAD_WIDTH = Q_HEADS * LANES
C_Q = 0
C_KV = C_Q + NSA_WIDTH
C_ZN = C_KV + 6 * KV_WIDTH
C_QKV = C_ZN + NSA_WIDTH
C_ZG = C_QKV + GDN_CONV_CH
C_SM = C_ZG + GDN_WIDTH
N_PACK = C_SM + LANES
SM_B = 3 * Q_HEADS
SM_A = SM_B + GDN_HEADS

TQ = 128
CK_SLC = 256
N_SUB = 128
N_CMP = N_SUB - CMP_LEN // CMP_STRIDE + 1
SAMPLE_ROWS = 8
KVB_SK = 0
KVB_SV = KVB_SK + KV_WIDTH
KVB_WK = KVB_SV + 2 * KV_WIDTH
KVB_WV = KVB_WK + KV_WIDTH
KVB_WIDTH = KVB_WV + 2 * KV_WIDTH
KVB_COLS = (KVB_SK, KVB_SV, KVB_WK, KVB_WV)
WIN_SPAN = WINDOW + TQ


def _pack_w_in(w_in):
    o_gate = NSA_WIDTH + 6 * KV_WIDTH
    o_zn = o_gate + 3 * Q_HEADS
    o_qkv = o_zn + NSA_WIDTH
    o_b = o_qkv + GDN_CONV_CH
    o_a = o_b + GDN_HEADS
    o_zg = o_a + GDN_HEADS
    pad = jnp.zeros((w_in.shape[0], LANES - SM_A - GDN_HEADS), w_in.dtype)
    return jnp.concatenate(
        [w_in[:, :o_gate], w_in[:, o_zn:o_qkv], w_in[:, o_qkv:o_b], w_in[:, o_zg:],
         w_in[:, o_gate:o_zn], w_in[:, o_b:o_a], w_in[:, o_a:o_zg], pad], axis=1)


def _rope_tables(pos):
    half = ROT_DIM // 2
    inv = ROPE_THETA ** (-(jnp.arange(half, dtype=F32) * 2.0 / ROT_DIM))
    ang = pos.astype(F32)[:, None] * inv[None, :]
    cos, sin = jnp.cos(ang), jnp.sin(ang)
    n = pos.shape[0]
    one = jnp.ones((n, HEAD_DIM - ROT_DIM), F32)
    zero = jnp.zeros((n, HEAD_DIM - ROT_DIM), F32)
    zh = jnp.zeros((n, half), F32)
    c64 = jnp.concatenate([cos, cos, one], axis=1)
    a64 = jnp.concatenate([zh, sin, zero], axis=1)
    b64 = jnp.concatenate([-sin, zh, zero], axis=1)
    tile = lambda t: jnp.concatenate([t, t], axis=1)
    return tile(c64), tile(a64), tile(b64)


def _rope128(x, c, a, b):
    half = ROT_DIM // 2
    return x * c + pltpu.roll(x, half, 1) * a + pltpu.roll(x, LANES - half, 1) * b


_NN = (((1,), (0,)), ((), ()))
_NT = (((1,), (1,)), ((), ()))
_TN = (((0,), (0,)), ((), ()))


def _split_bf16(x):
    hi = x.astype(BF16)
    return hi, (x - hi.astype(F32)).astype(BF16)


def _mm(a, b, dims=_NN, hi=False):
    dot = lambda x, y: lax.dot_general(x, y, dims, preferred_element_type=F32)
    if hi:
        a_hi, a_lo = _split_bf16(a.astype(F32))
        b_hi, b_lo = _split_bf16(b.astype(F32))
        return dot(a_hi, b_hi) + (dot(a_lo, b_hi) + dot(a_hi, b_lo))
    return dot(a.astype(BF16), b.astype(BF16))


def _mm_exact_lhs(a_bf16, x, dims=_NN):
    x1 = x.astype(BF16)
    r1 = x - x1.astype(F32)
    x2 = r1.astype(BF16)
    x3 = (r1 - x2.astype(F32)).astype(BF16)
    dot = lambda t: lax.dot_general(a_bf16, t, dims, preferred_element_type=F32)
    return dot(x1) + dot(x2) + dot(x3)


def _sigmoid(x):
    return 1.0 / (1.0 + jnp.exp(-x))


def _silu(x):
    return x * _sigmoid(x)


def _tile_rows(x, n):
    return jnp.concatenate([x] * n, axis=0)


def _tile_lanes(x, n):
    return x if n == 1 else jnp.concatenate([x] * n, axis=1)


def _eye(n, dtype):
    r = lax.broadcasted_iota(jnp.int32, (n, n), 0)
    c = lax.broadcasted_iota(jnp.int32, (n, n), 1)
    return (r == c).astype(dtype)


def _proj_kernel(x_ref, wn_ref, w_ref, c_ref, a_ref, b_ref,
                 q_ref, qr_ref, ck_ref, cv_ref, sk_ref, sv_ref, wk_ref, wv_ref, kvb_ref,
                 zn_ref, qkv_ref, zg_ref, sm_ref, *t_refs, hi_gdn):
    x = x_ref[...]
    ms = jnp.mean(x * x, axis=-1, keepdims=True)
    xn = x * lax.rsqrt(ms + NORM_EPS) * wn_ref[...]
    xb = xn.astype(BF16)
    c, a, b = c_ref[...], a_ref[...], b_ref[...]
    scale = HEAD_DIM ** -0.5
    lane = lax.broadcasted_iota(jnp.int32, (x.shape[0], LANES), 1)
    half_mask = (lane < HEAD_DIM, lane >= HEAD_DIM)
    for j in range(NSA_WIDTH // (2 * LANES)):
        q2 = _mm(xb, w_ref[:, C_Q + 2 * j * LANES:C_Q + 2 * (j + 1) * LANES])
        for jj in range(2):
            raw = q2[:, jj * LANES:(jj + 1) * LANES]
            for src, dst in ((raw, q_ref), (_rope128(raw, c, a, b), qr_ref)):
                swapped = pltpu.roll(src, HEAD_DIM, 1)
                for par in range(2):
                    head = 2 * (2 * j + jj) + par
                    kvh = head // GROUP
                    val = jnp.where(half_mask[kvh], src if par == kvh else swapped, 0.0)
                    dst[:, head * LANES:(head + 1) * LANES] = (val * scale).astype(dst.dtype)
    kv_refs = (ck_ref, cv_ref, sk_ref, sv_ref, wk_ref, wv_ref)
    kv2 = [_mm(xb, w_ref[:, C_KV + j * LANES:C_KV + (j + 2) * LANES]) for j in range(0, 6, 2)]
    for j in range(6):
        kj = kv2[j // 2][:, (j % 2) * LANES:(j % 2 + 1) * LANES]
        if j in (2, 4):
            kj = _rope128(kj, c, a, b)
        kv_refs[j][...] = kj
        if j >= 2:
            lo = KVB_COLS[j - 2]
            kvb_ref[:, lo:lo + LANES] = kj.astype(BF16)
        if t_refs:
            t_refs[j][0] = kj.T
    ones = jnp.ones((x.shape[0], LANES), BF16)
    kvb_ref[:, KVB_SV + LANES:KVB_SV + 2 * LANES] = ones
    kvb_ref[:, KVB_WV + LANES:KVB_WV + 2 * LANES] = ones
    zn_ref[...] = _mm(xb, w_ref[:, C_ZN:C_QKV])
    xg = xn if hi_gdn else xb
    for j in range(3):
        lo = C_QKV + j * GDN_WIDTH
        qkv_ref[:, j * GDN_WIDTH:(j + 1) * GDN_WIDTH] = _mm(xg, w_ref[:, lo:lo + GDN_WIDTH], hi=hi_gdn)
    zg_ref[...] = _mm(xb, w_ref[:, C_ZG:C_SM])
    sm_ref[...] = _mm(xg, w_ref[:, C_SM:N_PACK], hi=hi_gdn)


def _project(x2d, w_norm, w_pack, tables, rows_per_seq, tm, hi_gdn, q_dtype, emit_t):
    n = x2d.shape[0]
    nt = rows_per_seq // tm
    row = lambda i: (i, 0)
    tab = lambda i: (i % nt, 0)
    fix = lambda i: (0, 0)
    widths = (QPAD_WIDTH, QPAD_WIDTH) + (KV_WIDTH,) * 6 + (KVB_WIDTH, NSA_WIDTH, GDN_CONV_CH, GDN_WIDTH, LANES)
    dtypes = (q_dtype, q_dtype) + (F32,) * 6 + (BF16, F32, F32, F32, F32)
    out_specs = [pl.BlockSpec((tm, w), row) for w in widths]
    out_shape = [jax.ShapeDtypeStruct((n, w), d) for w, d in zip(widths, dtypes)]
    if emit_t:
        out_specs += [pl.BlockSpec((1, KV_WIDTH, tm), lambda i: (i // nt, 0, i % nt))] * 6
        out_shape += [jax.ShapeDtypeStruct((n // rows_per_seq, KV_WIDTH, rows_per_seq), F32)] * 6
    return pl.pallas_call(
        functools.partial(_proj_kernel, hi_gdn=hi_gdn),
        grid=(n // tm,),
        in_specs=[pl.BlockSpec((tm, D_MODEL), row),
                  pl.BlockSpec((1, D_MODEL), fix),
                  pl.BlockSpec((D_MODEL, N_PACK), fix),
                  pl.BlockSpec((tm, LANES), tab),
                  pl.BlockSpec((tm, LANES), tab),
                  pl.BlockSpec((tm, LANES), tab)],
        out_specs=out_specs,
        out_shape=out_shape,
        compiler_params=pltpu.CompilerParams(dimension_semantics=("arbitrary",),
                                             vmem_limit_bytes=VMEM_LIMIT_BYTES),
        name="in_proj",
    )(x2d, w_norm.reshape(1, D_MODEL), w_pack, *tables)


def _out_kernel(x_ref, nsa_ref, gdn_ref, w_ref, wf_ref, y_ref, *, final):
    mix = jnp.concatenate([nsa_ref[...].astype(BF16), gdn_ref[...].astype(BF16)], axis=1)
    h = x_ref[...] + _mm(mix, w_ref[...])
    if final:
        ms = jnp.mean(h * h, axis=-1, keepdims=True)
        h = h * lax.rsqrt(ms + NORM_EPS) * wf_ref[...]
    y_ref[...] = h


def _out_project(x2d, nsa, gdn, w_out, w_final, tm, final):
    n = x2d.shape[0]
    row = lambda i: (i, 0)
    fix = lambda i: (0, 0)
    return pl.pallas_call(
        functools.partial(_out_kernel, final=final),
        grid=(n // tm,),
        in_specs=[pl.BlockSpec((tm, D_MODEL), row),
                  pl.BlockSpec((tm, NSA_WIDTH), row),
                  pl.BlockSpec((tm, GDN_WIDTH), row),
                  pl.BlockSpec((D_MODEL, D_MODEL), fix),
                  pl.BlockSpec((1, D_MODEL), fix)],
        out_specs=pl.BlockSpec((tm, D_MODEL), row),
        out_shape=jax.ShapeDtypeStruct((n, D_MODEL), F32),
        compiler_params=pltpu.CompilerParams(dimension_semantics=("arbitrary",),
                                             vmem_limit_bytes=VMEM_LIMIT_BYTES),
        name="out_proj",
    )(x2d, nsa, gdn, w_out, w_final.reshape(1, D_MODEL))


def _compress_weights(pe, w1, w2):
    half = CMP_STRIDE * HEAD_DIM
    z = jnp.zeros((CMP_STRIDE, HEAD_DIM, CMP_HIDDEN), w1.dtype)

    def place(wpart, h):
        wp = wpart.reshape(CMP_STRIDE, HEAD_DIM, CMP_HIDDEN)
        parts = [wp, z] if h == 0 else [z, wp]
        return jnp.stack(parts, axis=1).reshape(CMP_STRIDE * KV_WIDTH, CMP_HIDDEN)

    w1big = jnp.concatenate([place(w1[:half], 0), place(w1[:half], 1),
                             place(w1[half:], 0), place(w1[half:], 1)], axis=1)
    zz = jnp.zeros_like(w2)
    w2big = jnp.concatenate([jnp.concatenate([w2, zz], axis=1), jnp.concatenate([zz, w2], axis=1)], axis=0)
    pe_a = jnp.tile(pe[:CMP_STRIDE], (1, KV_HEADS)).reshape(1, CMP_STRIDE * KV_WIDTH)
    pe_b = jnp.tile(pe[CMP_STRIDE:], (1, KV_HEADS)).reshape(1, CMP_STRIDE * KV_WIDTH)
    pe2 = jnp.concatenate([pe_a, pe_b, jnp.zeros((SUBLANES - 2, CMP_STRIDE * KV_WIDTH), pe.dtype)], axis=0)
    return w1big.astype(BF16), w2big.astype(BF16), pe2


def _compress(sub_rows, w1_ref, w2_ref, pe_ref):
    hid = 2 * CMP_HIDDEN
    w1 = w1_ref[...]
    ab = _mm(sub_rows, w1)
    pe = pe_ref[...]
    pe_hi = pe.astype(BF16)
    pe_lo = (pe - pe_hi.astype(F32)).astype(BF16)
    r = _mm(pe_hi, w1) + _mm(pe_lo, w1)
    bias = r[0:1, :hid] + r[1:2, hid:]
    h = ab[:, :hid] + pltpu.roll(ab[:, hid:], N_SUB - 1, 0) + bias
    return _mm(_silu(h), w2_ref[...])


def _overlap_t(ns):
    c0 = np.arange(N_CMP)[None, :] * CMP_STRIDE
    b0 = np.arange(ns)[:, None] * SEL_BLOCK
    ov = np.minimum(c0 + CMP_LEN, b0 + SEL_BLOCK) - np.maximum(c0, b0)
    out = np.zeros((LANES, LANES), np.float32)
    out[:ns, :N_CMP] = np.maximum(ov, 0) / CMP_LEN
    return jnp.asarray(out, BF16)


def _select_blocks(imp_t, n_idx, q_blk, n_rows):
    forced = (n_idx == 0) | (n_idx == q_blk) | (n_idx == q_blk - 1)
    allowed = n_idx <= q_blk
    v = jnp.where(allowed, imp_t + FORCE_BONUS * forced.astype(F32), MASK_VALUE)
    rank = jnp.zeros(v.shape, F32)
    for j in range(n_rows):
        vj = v[j:j + 1, :]
        ge = jnp.where(vj >= v, 1.0, 0.0)
        gt = jnp.where(vj > v, 1.0, 0.0)
        rank = rank + jnp.where(n_idx > j, ge, gt)
    return ((rank < TOP_N) & allowed).astype(F32)


def _merge_heads(o_sum, kvh, tq):
    lane = lax.broadcasted_iota(jnp.int32, (tq, LANES), 1)
    slabs = []
    for pair in range(GROUP // 2):
        halves = []
        for par in range(2):
            o = o_sum[(2 * pair + par) * tq:(2 * pair + par + 1) * tq]
            halves.append(o if par == kvh else pltpu.roll(o, HEAD_DIM, 1))
        slabs.append(jnp.where(lane < HEAD_DIM, halves[0], halves[1]))
    return slabs


def _nsa_prompt_kernel(q_ref, qr_ref, kvb_ref, ckr_ref, cvr_ref, sm_ref, zn_ref,
                       wk1_ref, wk2_ref, pek_ref, wv1_ref, wv2_ref, pev_ref, ovt_ref, e_ref,
                       o_ref, ck_s, cv_s, s_s, mrun_s, m_s, acc_s):
    i = pl.program_id(1)

    @pl.when(i == 0)
    def _():
        ck_s[...] = _compress(ckr_ref[0], wk1_ref, wk2_ref, pek_ref).astype(BF16)
        cv_s[...] = _compress(cvr_ref[0], wv1_ref, wv2_ref, pev_ref).astype(BF16)

    kvs = range(KV_HEADS)
    t0 = i * TQ
    rows = GROUP * TQ
    row = lax.broadcasted_iota(jnp.int32, (TQ, LANES), 0)
    col = lax.broadcasted_iota(jnp.int32, (TQ, LANES), 1)
    cvalid = (CMP_STRIDE * col + (CMP_LEN - 1) <= t0 + row) & (col < N_CMP)
    cm = _tile_rows(cvalid, GROUP)
    cm_f = cm.astype(F32)
    sig = _sigmoid(sm_ref[...])

    def gate(branch, kvh):
        cols = [sig[:, branch * Q_HEADS + h:branch * Q_HEADS + h + 1] for h in range(kvh * GROUP, (kvh + 1) * GROUP)]
        return jnp.concatenate([jnp.broadcast_to(c, (TQ, LANES)) for c in cols], axis=0)

    def stack(ref, kvh):
        return jnp.concatenate([ref[:, h * LANES:(h + 1) * LANES] for h in range(kvh * GROUP, (kvh + 1) * GROUP)], axis=0)

    q_raw = [stack(q_ref, k) for k in kvs]
    q_rot = [stack(qr_ref, k) for k in kvs]

    ck, cv = ck_s[...], cv_s[...]
    s = [jnp.where(cm, _mm(q_raw[k], ck, _NT), MASK_VALUE) for k in kvs]
    e = [jnp.exp(s[k] - jnp.max(s[k], axis=1, keepdims=True)) for k in kvs]
    p = [e[k] / jnp.sum(e[k], axis=1, keepdims=True) * cm_f for k in kvs]
    o_sum = [gate(0, k) * _mm(p[k], cv) for k in kvs]
    p_sum = jnp.concatenate([p[k][0:TQ] + p[k][TQ:2 * TQ] + p[k][2 * TQ:3 * TQ] + p[k][3 * TQ:4 * TQ] for k in kvs],
                            axis=0)
    ns_rows = 32
    nq = KV_HEADS * TQ
    imp_t = _mm_exact_lhs(ovt_ref[...], p_sum, _NT)[0:ns_rows]
    n_idx = lax.broadcasted_iota(jnp.int32, (ns_rows, nq), 0)
    qb_t = (t0 + lax.broadcasted_iota(jnp.int32, (ns_rows, nq), 1) % TQ) // SEL_BLOCK
    sel_t = _select_blocks(imp_t, n_idx, qb_t, ns_rows)
    sel_t = jnp.concatenate([sel_t, jnp.zeros((LANES - ns_rows, nq), F32)], axis=0).astype(BF16)
    eye = _eye(TQ, BF16)
    sel = [_mm(eye, sel_t[:, k * TQ:(k + 1) * TQ], _NT).astype(BF16) for k in kvs]

    col_s = lax.broadcasted_iota(jnp.int32, (TQ, CK_SLC), 1)
    row_s = lax.broadcasted_iota(jnp.int32, (TQ, CK_SLC), 0)
    n_chunks = (t0 + TQ + CK_SLC - 1) // CK_SLC
    mrun_s[...] = jnp.full(mrun_s.shape, MASK_VALUE, F32)

    def score_chunks(js):
        k0 = [pl.multiple_of(j * CK_SLC, CK_SLC) for j in js]
        k_c = [kvb_ref[pl.ds(k0[n], CK_SLC), KVB_SK:KVB_SK + LANES] for n in range(len(js))]
        sc = [[_mm(q_rot[k], k_c[n], _NT) for k in kvs] for n in range(len(js))]
        picked = [[_mm(sel[k], e_ref[js[n]]) for k in kvs] for n in range(len(js))]
        for k in kvs:
            run = mrun_s[k]
            for n, j in enumerate(js):
                causal = k0[n] + col_s <= t0 + row_s
                bias = jnp.where((picked[n][k] > 0.5) & causal, 0.0, MASK_VALUE)
                sk = sc[n][k] + _tile_rows(bias, GROUP)
                s_s[k, j] = sk
                run = jnp.maximum(run, jnp.maximum(sk[:, :LANES], sk[:, LANES:]))
            mrun_s[k] = run

    def value_chunks(js):
        k0 = [pl.multiple_of(j * CK_SLC, CK_SLC) for j in js]
        v_c = [kvb_ref[pl.ds(k0[n], CK_SLC), KVB_SV:KVB_SV + 2 * LANES] for n in range(len(js))]
        pr = [[jnp.exp(s_s[k, js[n]] - _tile_lanes(m_s[k], CK_SLC // LANES)) for k in kvs] for n in range(len(js))]
        pv = [[_mm(pr[n][k], v_c[n]) for k in kvs] for n in range(len(js))]
        for k in kvs:
            acc_s[k] += sum(pv[n][k] for n in range(len(js)))

    def paired(fn):
        def pair_body(jj, carry):
            fn([2 * jj, 2 * jj + 1])
            return carry

        lax.fori_loop(0, n_chunks // 2, pair_body, 0)

        @pl.when(n_chunks % 2 == 1)
        def _():
            fn([n_chunks - 1])

    paired(score_chunks)
    for k in kvs:
        m_s[k] = jnp.broadcast_to(jnp.max(mrun_s[k], axis=1, keepdims=True), (rows, LANES))
    acc_s[...] = jnp.zeros(acc_s.shape, F32)

    paired(value_chunks)
    for k in kvs:
        acc = acc_s[k]
        o_sum[k] = o_sum[k] + gate(1, k) * (acc[:, :LANES] / acc[:, LANES:])

    ks0 = pl.multiple_of(jnp.maximum(t0 - WINDOW, 0), TQ)
    k_w = kvb_ref[pl.ds(ks0, WIN_SPAN), KVB_WK:KVB_WK + LANES]
    v_w = kvb_ref[pl.ds(ks0, WIN_SPAN), KVB_WV:KVB_WV + 2 * LANES]
    diff = (t0 + lax.broadcasted_iota(jnp.int32, (TQ, WIN_SPAN), 0)) - (ks0 + lax.broadcasted_iota(jnp.int32, (TQ, WIN_SPAN), 1))
    bias_w = _tile_rows(jnp.where((diff >= 0) & (diff <= WINDOW), 0.0, MASK_VALUE), GROUP)
    sw = [_mm(q_rot[k], k_w, _NT) + bias_w for k in kvs]
    pw = [jnp.exp(sw[k] - jnp.max(sw[k], axis=1, keepdims=True)) for k in kvs]
    rw = [_mm(pw[k], v_w) for k in kvs]
    for k in kvs:
        o_all = o_sum[k] + gate(2, k) * (rw[k][:, :LANES] / rw[k][:, LANES:])
        for pair, slab in enumerate(_merge_heads(o_all, k, TQ)):
            lo = (k * (GROUP // 2) + pair) * LANES
            o_ref[:, lo:lo + LANES] = (slab * _silu(zn_ref[:, lo:lo + LANES])).astype(o_ref.dtype)


def _nsa_prompt(q, qr, kvb, ck, cv, sm, zn, cmp_k_w, cmp_v_w, batch, seq):
    nt = seq // TQ
    ns = seq // SEL_BLOCK
    sub_w = CMP_STRIDE * KV_WIDTH
    keys = np.arange(seq)
    e = (np.arange(LANES)[None, :, None] == (keys // SEL_BLOCK).reshape(seq // CK_SLC, 1, CK_SLC))
    e = jnp.asarray(e, BF16)
    tile = lambda b, i: (b * nt + i, 0)
    per_b = lambda b, i: (b, 0)
    per_b3 = lambda b, i: (b, 0, 0)
    fix2 = lambda b, i: (0, 0)
    fix3 = lambda b, i: (0, 0, 0)
    wspecs = [pl.BlockSpec((sub_w, 4 * CMP_HIDDEN), fix2),
              pl.BlockSpec((2 * CMP_HIDDEN, KV_WIDTH), fix2),
              pl.BlockSpec((SUBLANES, sub_w), fix2)]
    rows = GROUP * TQ
    return pl.pallas_call(
        _nsa_prompt_kernel,
        grid=(batch, nt),
        in_specs=[pl.BlockSpec((TQ, QPAD_WIDTH), tile),
                  pl.BlockSpec((TQ, QPAD_WIDTH), tile),
                  pl.BlockSpec((seq, KVB_WIDTH), per_b),
                  pl.BlockSpec((1, seq // CMP_STRIDE, sub_w), per_b3),
                  pl.BlockSpec((1, seq // CMP_STRIDE, sub_w), per_b3),
                  pl.BlockSpec((TQ, LANES), tile),
                  pl.BlockSpec((TQ, NSA_WIDTH), tile)] + wspecs + wspecs + [
                  pl.BlockSpec((LANES, LANES), fix2),
                  pl.BlockSpec((seq // CK_SLC, LANES, CK_SLC), fix3)],
        out_specs=pl.BlockSpec((TQ, NSA_WIDTH), tile),
        out_shape=jax.ShapeDtypeStruct((batch * seq, NSA_WIDTH), BF16),
        scratch_shapes=[pltpu.VMEM((N_SUB, KV_WIDTH), BF16), pltpu.VMEM((N_SUB, KV_WIDTH), BF16),
                        pltpu.VMEM((KV_HEADS, seq // CK_SLC, rows, CK_SLC), F32),
                        pltpu.VMEM((KV_HEADS, rows, LANES), F32), pltpu.VMEM((KV_HEADS, rows, LANES), F32),
                        pltpu.VMEM((KV_HEADS, rows, 2 * LANES), F32)],
        compiler_params=pltpu.CompilerParams(dimension_semantics=("arbitrary", "arbitrary"),
                                             vmem_limit_bytes=VMEM_LIMIT_BYTES),
        name="nsa_prompt",
    )(q, qr, kvb, ck.reshape(batch, seq // CMP_STRIDE, sub_w), cv.reshape(batch, seq // CMP_STRIDE, sub_w),
      sm, zn, *cmp_k_w, *cmp_v_w, _overlap_t(ns), e)


def _nsa_sample_t_kernel(pt_ref, q_ref, qr_ref, skn_ref, svn_ref, wkn_ref, wvn_ref, sm_ref, zn_ref,
                         wkc_ref, wvc_ref, cmpk_hbm, cmpv_hbm, slck_hbm, slcv_hbm,
                         w1k_ref, w2k_ref, pek_ref, w1v_ref, w2v_ref, pev_ref, ovt_ref, e_ref,
                         o_ref, wko_ref, wvo_ref, kt_s, vt_s, kwt_s, vwt_s, pages_s, sems,
                         *, nb, n_pages, page, past, n_new):
    step = pl.program_id(0)
    last = pl.num_programs(0) - 1
    slot = step % 2
    caches = (cmpk_hbm, cmpv_hbm, slck_hbm, slcv_hbm)
    page_ids = [(j, c, p) for j in range(nb) for c in range(len(caches)) for p in range(n_pages)]

    def page_copy(at_step, at_slot, j, c, p):
        src = caches[c].at[pt_ref[at_step * nb + j, p]]
        return pltpu.make_async_copy(src, pages_s.at[at_slot, j, c, p], sems.at[at_slot, c])

    @pl.when(step == 0)
    def _():
        for ids in page_ids:
            page_copy(0, 0, *ids).start()

    nxt = jnp.minimum(step + 1, last)
    for ids in page_ids:
        page_copy(nxt, 1 - slot, *ids).start()
    for ids in page_ids:
        page_copy(step, slot, *ids).wait()

    cmpk, cmpv, slck, slcv = ([pages_s.at[slot, j, c, p] for j in range(nb) for p in range(n_pages)]
                              for c in range(len(caches)))
    nr = SAMPLE_ROWS
    rows = Q_HEADS * nr
    win_buf = wkc_ref.shape[2]
    bs = range(nb)
    hid = 2 * CMP_HIDDEN
    zpad = jnp.zeros((LANES - nr, LANES), F32)

    def new_rows(ref, b):
        return jnp.concatenate([ref[b * nr:(b + 1) * nr, :], zpad], axis=0).astype(BF16)

    def stack(ref, b):
        return jnp.concatenate([ref[b * nr:(b + 1) * nr, h * LANES:(h + 1) * LANES] for h in range(Q_HEADS)],
                               axis=0).astype(BF16)

    for b in bs:
        for p in range(n_pages):
            kt_s[b, :, p * page:(p + 1) * page] = slck[b * n_pages + p][...].astype(BF16)
            vt_s[b, :, p * page:(p + 1) * page] = slcv[b * n_pages + p][...].astype(BF16)
        kwt_s[b] = wkc_ref[b].astype(BF16)
        vwt_s[b] = wvc_ref[b].astype(BF16)
    mi = lax.broadcasted_iota(jnp.int32, (page, page), 0)
    ki = lax.broadcasted_iota(jnp.int32, (page, page), 1)
    per_page = page // CMP_STRIDE
    perm = (ki == CMP_STRIDE * (mi % per_page) + mi // per_page).astype(BF16)

    def transposed_pages(pages, b):
        out = []
        for p in range(0, n_pages, 2):
            pair = jnp.concatenate([pages[b * n_pages + p][...], pages[b * n_pages + p + 1][...]], axis=0)
            both = _mm(perm, pair, _NT)
            out += [both[:, :page], both[:, page:]]
        return out

    rows_t = [[transposed_pages(pages, b) for pages in (cmpk, cmpv)] for b in bs]

    q_raw = [stack(q_ref, b) for b in bs]
    q_rot = [stack(qr_ref, b) for b in bs]

    sw = [jnp.concatenate([_mm(q_rot[b], kwt_s[b]), _mm(q_rot[b], new_rows(wkn_ref, b), _NT)], axis=1) for b in bs]
    ss = [jnp.concatenate([_mm(q_rot[b], kt_s[b]), _mm(q_rot[b], new_rows(skn_ref, b), _NT)], axis=1) for b in bs]

    def sub_blocks(b, c):
        cols = [jnp.concatenate([rows_t[b][c][p][l * per_page:(l + 1) * per_page] for p in range(n_pages)], axis=0)
                for l in range(CMP_STRIDE)]
        return jnp.concatenate(cols, axis=1)

    def activate(t):
        bias = t[0:hid, LANES:LANES + 1] + t[hid:2 * hid, LANES + 1:LANES + 2]
        return _silu(t[0:hid, 0:LANES] + pltpu.roll(t[hid:2 * hid, 0:LANES], N_SUB - 1, 1) + bias)

    ckt, cvt = [], []
    for c, (w1_ref, w2_ref, pe_ref, out) in enumerate(((w1k_ref, w2k_ref, pek_ref, ckt), (w1v_ref, w2v_ref, pev_ref, cvt))):
        pe_rows = pe_ref[...]
        ht = [_mm(w1_ref[...], jnp.concatenate([sub_blocks(b, c).astype(BF16), pe_rows], axis=0), _NT) for b in bs]
        out.extend(_mm(w2_ref[...], activate(ht[b])) for b in bs)

    tok = lax.broadcasted_iota(jnp.int32, (nr, LANES), 0)
    col = lax.broadcasted_iota(jnp.int32, (nr, LANES), 1)
    cm = _tile_rows((CMP_STRIDE * col + (CMP_LEN - 1) <= past + tok) & (col < N_CMP), Q_HEADS)
    cm_f = cm.astype(F32)
    sc = [jnp.where(cm, _mm(q_raw[b], ckt[b]), MASK_VALUE) for b in bs]
    ec = [jnp.exp(sc[b] - jnp.max(sc[b], axis=1, keepdims=True)) for b in bs]
    pc = [ec[b] / jnp.sum(ec[b], axis=1, keepdims=True) * cm_f for b in bs]
    o_cmp = [_mm(pc[b], cvt[b], _NT) for b in bs]
    p_sum = jnp.concatenate(
        [sum(pc[b][(kvh * GROUP + g) * nr:(kvh * GROUP + g + 1) * nr] for g in range(GROUP))
         for b in bs for kvh in range(KV_HEADS)], axis=0)
    ns = (past + n_new + SEL_BLOCK - 1) // SEL_BLOCK
    ns_rows = -(-ns // SUBLANES) * SUBLANES
    nq = nb * KV_HEADS * nr
    imp_t = _mm_exact_lhs(ovt_ref[...], p_sum, _NT)[0:ns_rows]
    n_idx = lax.broadcasted_iota(jnp.int32, (ns_rows, nq), 0)
    qb_t = (past + lax.broadcasted_iota(jnp.int32, (ns_rows, nq), 1) % nr) // SEL_BLOCK
    sel_t = _select_blocks(imp_t, n_idx, qb_t, ns)
    sel_t = jnp.concatenate([sel_t, jnp.zeros((LANES - ns_rows, nq), F32)], axis=0)
    sel = _mm(_eye(nq, BF16), sel_t, _NT)

    all_keys = past + LANES
    q_pos = past + lax.broadcasted_iota(jnp.int32, (rows, all_keys), 0) % nr
    causal = lax.broadcasted_iota(jnp.int32, (rows, all_keys), 1) <= q_pos
    o_slc = []
    for b in bs:
        sel_rows = jnp.concatenate(
            [_tile_rows(sel[(b * KV_HEADS + kvh) * nr:(b * KV_HEADS + kvh + 1) * nr], GROUP) for kvh in range(KV_HEADS)],
            axis=0)
        ok = (_mm(sel_rows, e_ref[...]) > 0.5) & causal
        s = ss[b] + jnp.where(ok, 0.0, MASK_VALUE)
        pr = jnp.exp(s - jnp.max(s, axis=1, keepdims=True))
        o = _mm(pr[:, :past], vt_s[b], _NT) + _mm(pr[:, past:], new_rows(svn_ref, b))
        o_slc.append(o / jnp.sum(pr, axis=1, keepdims=True))

    win_keys = win_buf + LANES
    q_pos = past + lax.broadcasted_iota(jnp.int32, (rows, win_keys), 0) % nr
    diff = q_pos - (past - win_buf + lax.broadcasted_iota(jnp.int32, (rows, win_keys), 1))
    bias_w = jnp.where((diff >= 0) & (diff <= WINDOW), 0.0, MASK_VALUE)
    o_win = []
    for b in bs:
        s = sw[b] + bias_w
        pr = jnp.exp(s - jnp.max(s, axis=1, keepdims=True))
        o = _mm(pr[:, :win_buf], vwt_s[b], _NT) + _mm(pr[:, win_buf:], new_rows(wvn_ref, b))
        o_win.append(o / jnp.sum(pr, axis=1, keepdims=True))

    for b in bs:
        sig = _sigmoid(sm_ref[b * nr:(b + 1) * nr, :])

        def gate(branch):
            cols = [sig[:, branch * Q_HEADS + h:branch * Q_HEADS + h + 1] for h in range(Q_HEADS)]
            return jnp.concatenate([jnp.broadcast_to(c, (nr, LANES)) for c in cols], axis=0)

        o_all = gate(0) * o_cmp[b] + gate(1) * o_slc[b] + gate(2) * o_win[b]
        for kvh in range(KV_HEADS):
            o_kv = o_all[kvh * GROUP * nr:(kvh + 1) * GROUP * nr]
            for pair, slab in enumerate(_merge_heads(o_kv, kvh, nr)):
                lo = (kvh * (GROUP // 2) + pair) * LANES
                o_ref[b * nr:(b + 1) * nr, lo:lo + LANES] = slab * _silu(zn_ref[b * nr:(b + 1) * nr, lo:lo + LANES])

    eye_bf = _eye(LANES, BF16)
    lane = lax.broadcasted_iota(jnp.int32, (KV_WIDTH, LANES), 1)
    for cache_ref, new_ref, out_ref in ((wkc_ref, wkn_ref, wko_ref), (wvc_ref, wvn_ref, wvo_ref)):
        for b in bs:
            shifted = pltpu.roll(cache_ref[b], win_buf - n_new, 1)
            new_pad = jnp.concatenate([new_ref[b * nr:(b + 1) * nr, :], zpad], axis=0)
            new_t = _mm_exact_lhs(eye_bf, new_pad, _NT)
            tail = jnp.where(lane >= LANES - n_new, pltpu.roll(new_t, LANES - n_new, 1), shifted[:, win_buf - LANES:])
            out_ref[b] = jnp.concatenate([shifted[:, :win_buf - LANES], tail], axis=1)

    @pl.when(step == last)
    def _():
        for ids in page_ids:
            page_copy(nxt, 1 - slot, *ids).wait()


def _nsa_sample_t(q, qr, sk, sv, wk, wv, sm, zn, win_k, win_v, cmp_k, cmp_v, slc_k, slc_v, page_table,
                  cmp_k_w, cmp_v_w, n_new, nb):
    batch, n_pages = page_table.shape
    n_pool, page = cmp_k.shape[0], cmp_k.shape[1]
    past = n_pages * page
    nr = SAMPLE_ROWS
    win_buf = win_k.shape[1]
    ns = (past + n_new + SEL_BLOCK - 1) // SEL_BLOCK
    all_keys = past + LANES
    e = jnp.asarray(np.arange(LANES)[:, None] == (np.arange(all_keys) // SEL_BLOCK)[None, :], BF16)
    tview = lambda c: jnp.transpose(c, (0, 2, 3, 1)).reshape(c.shape[0], KV_WIDTH, c.shape[1])
    cmp_k, cmp_v, slc_k, slc_v, win_k, win_v = (tview(c) for c in (cmp_k, cmp_v, slc_k, slc_v, win_k, win_v))

    row = lambda b, pt: (b, 0)
    per_b3 = lambda b, pt: (b, 0, 0)
    fix2 = lambda b, pt: (0, 0)
    sub_w = CMP_STRIDE * KV_WIDTH
    wspecs = [pl.BlockSpec((4 * CMP_HIDDEN, sub_w), fix2),
              pl.BlockSpec((KV_WIDTH, 2 * CMP_HIDDEN), fix2),
              pl.BlockSpec((LANES, sub_w), fix2)]
    in_specs = ([pl.BlockSpec((nb * nr, QPAD_WIDTH), row)] * 2 + [pl.BlockSpec((nb * nr, KV_WIDTH), row)] * 4
                + [pl.BlockSpec((nb * nr, LANES), row), pl.BlockSpec((nb * nr, NSA_WIDTH), row)]
                + [pl.BlockSpec((nb, KV_WIDTH, win_buf), per_b3)] * 2
                + [pl.BlockSpec(memory_space=pl.ANY)] * 4 + wspecs + wspecs
                + [pl.BlockSpec((LANES, LANES), fix2), pl.BlockSpec((LANES, all_keys), fix2)])
    grid_spec = pltpu.PrefetchScalarGridSpec(
        num_scalar_prefetch=1,
        grid=(batch // nb,),
        in_specs=in_specs,
        out_specs=[pl.BlockSpec((nb * nr, NSA_WIDTH), row),
                   pl.BlockSpec((nb, KV_WIDTH, win_buf), per_b3), pl.BlockSpec((nb, KV_WIDTH, win_buf), per_b3)],
        scratch_shapes=[pltpu.VMEM((nb, KV_WIDTH, past), BF16), pltpu.VMEM((nb, KV_WIDTH, past), BF16),
                        pltpu.VMEM((nb, KV_WIDTH, win_buf), BF16), pltpu.VMEM((nb, KV_WIDTH, win_buf), BF16),
                        pltpu.VMEM((2, nb, 4, n_pages, KV_WIDTH, page), F32),
                        pltpu.SemaphoreType.DMA((2, 4))])
    return pl.pallas_call(
        functools.partial(_nsa_sample_t_kernel, nb=nb, n_pages=n_pages, page=page, past=past, n_new=n_new),
        grid_spec=grid_spec,
        out_shape=[jax.ShapeDtypeStruct((batch * nr, NSA_WIDTH), F32),
                   jax.ShapeDtypeStruct((batch, KV_WIDTH, win_buf), F32),
                   jax.ShapeDtypeStruct((batch, KV_WIDTH, win_buf), F32)],
        compiler_params=pltpu.CompilerParams(dimension_semantics=("arbitrary",),
                                             vmem_limit_bytes=VMEM_LIMIT_BYTES),
        name="nsa_sample",
    )(page_table, q, qr, sk, sv, wk, wv, sm, zn, win_k, win_v, cmp_k, cmp_v, slc_k, slc_v,
      *cmp_k_w, *cmp_v_w, _overlap_t(ns), e)


def _gdn_kernel(qkv_ref, sm_ref, zg_ref, conv0_ref, s0_ref, wc_ref, vec_ref, wg_ref,
                go_ref, xp_out_ref, st_ref, xp_s, *, bblk, chunk, n_valid, hi):
    c_idx = pl.program_id(1)

    @pl.when(c_idx == 0)
    def _():
        xp_s[:, 0:SUBLANES, :] = conv0_ref[...]
        st_ref[...] = s0_ref[...]

    wc = wc_ref[...]
    vec = vec_ref[...]
    row1 = lax.broadcasted_iota(jnp.int32, (chunk, LANES), 0)
    rr = lax.broadcasted_iota(jnp.int32, (chunk, chunk), 0)
    cc = lax.broadcasted_iota(jnp.int32, (chunk, chunk), 1)
    tri = rr >= cc
    tri_bf = tri.astype(BF16)
    eye_bf = _eye(LANES, BF16)

    acts, betas, decays, decay_ts, e_decs = [], [], [], [], []
    for b in range(bblk):
        xp_s[b, SUBLANES:SUBLANES + chunk, :] = qkv_ref[b]
        y = xp_s[b, SUBLANES:SUBLANES + chunk, :] * wc[GDN_CONV - 1:GDN_CONV, :]
        for j in range(GDN_CONV - 1):
            lo = SUBLANES - (GDN_CONV - 1) + j
            y = y + xp_s[b, lo:lo + chunk, :] * wc[j:j + 1, :]
        xp_out_ref[b] = xp_s[b]
        xp_s[b, 0:SUBLANES, :] = xp_s[b, chunk:chunk + SUBLANES, :]
        act = _silu(y)
        small = sm_ref[b]
        z = small + vec[1:2, :]
        softplus = jnp.maximum(z, 0.0) + jnp.log1p(jnp.exp(-jnp.abs(z)))
        g_all = -jnp.exp(vec[0:1, :]) * softplus
        if n_valid < chunk:
            valid = row1 < n_valid
            act = act * _tile_lanes(valid.astype(F32), GDN_CONV_CH // LANES)
            g_all = jnp.where(valid, g_all, 0.0)
        acts.append(act)
        betas.append(_sigmoid(small))
        decays.append(_mm_exact_lhs(tri_bf, g_all))
    for b in range(bblk):
        decay_ts.append(_mm_exact_lhs(eye_bf, decays[b], _NT))
        e_decs.append(jnp.exp(decays[b]))

    chains = [(b, h) for b in range(bblk) for h in range(GDN_HEADS)]
    qs, ks, kbs, dmasks, rhs_u, rhs_w, qds, kds, gls = [], [], [], [], [], [], [], [], []
    for b, h in chains:
        act = acts[b]
        qh = act[:, h * GDN_DK:(h + 1) * GDN_DK]
        kh = act[:, GDN_WIDTH + h * GDN_DK:GDN_WIDTH + (h + 1) * GDN_DK]
        vh = act[:, 2 * GDN_WIDTH + h * GDN_DV:2 * GDN_WIDTH + (h + 1) * GDN_DV]
        qh = qh * lax.rsqrt(jnp.sum(qh * qh, axis=-1, keepdims=True) + NORM_EPS) * (GDN_DK ** -0.5)
        kh = kh * lax.rsqrt(jnp.sum(kh * kh, axis=-1, keepdims=True) + NORM_EPS)
        beta = betas[b][:, SM_B + h:SM_B + h + 1]
        dcol = decays[b][:, SM_A + h:SM_A + h + 1]
        drow = decay_ts[b][SM_A + h:SM_A + h + 1, :]
        ed = e_decs[b][:, SM_A + h:SM_A + h + 1]
        dlast = decays[b][chunk - 1:chunk, SM_A + h:SM_A + h + 1]
        kb = kh * beta
        qs.append(qh)
        ks.append(kh)
        kbs.append(kb)
        dmasks.append(jnp.where(tri, jnp.exp(jnp.where(tri, dcol - drow, 0.0)), 0.0))
        rhs_u.append(vh * beta)
        rhs_w.append(kb * ed)
        qds.append(qh * ed)
        kds.append(kh * jnp.exp(dlast - dcol))
        gls.append(jnp.exp(dlast))

    n = len(chains)
    eye_c = _eye(chunk, F32)
    kkts = [_mm(kbs[i], ks[i], _NT, hi=hi) for i in range(n)]
    qks = [_mm(qs[i], ks[i], _NT, hi=hi) for i in range(n)]
    powers = [jnp.where(rr > cc, kkts[i] * dmasks[i], 0.0) for i in range(n)]
    qks = [qks[i] * dmasks[i] for i in range(n)]
    invs = [eye_c - powers[i] for i in range(n)]
    span = 2
    while span < n_valid:
        powers = [_mm(powers[i], powers[i], hi=hi) for i in range(n)]
        invs = [invs[i] + _mm(invs[i], powers[i], hi=hi) for i in range(n)]
        span *= 2
    us = [_mm(invs[i], rhs_u[i], hi=hi) for i in range(n)]
    ws = [_mm(invs[i], rhs_w[i], hi=hi) for i in range(n)]
    sts = [st_ref[b, h] for b, h in chains]
    v_news = [us[i] - _mm(ws[i], sts[i], hi=hi) for i in range(n)]
    os_ = [_mm(qds[i], sts[i], hi=hi) for i in range(n)]
    os_ = [os_[i] + _mm(qks[i], v_news[i], hi=hi) for i in range(n)]
    upd = [_mm(kds[i], v_news[i], _TN, hi=hi) for i in range(n)]
    for i, (b, h) in enumerate(chains):
        st_ref[b, h] = sts[i] * gls[i] + upd[i]
        o = os_[i]
        o = o * lax.rsqrt(jnp.mean(o * o, axis=-1, keepdims=True) + NORM_EPS) * wg_ref[...]
        o = o * _silu(zg_ref[b, :, h * GDN_DV:(h + 1) * GDN_DV])
        go_ref[b, :, h * GDN_DV:(h + 1) * GDN_DV] = o.astype(go_ref.dtype)


def _gdn(qkv, sm, zg, conv0, s0, w_conv, a_log, dt_bias, w_gnorm, batch, rows, bblk, chunk, n_valid, hi,
         out_dtype):
    nc = rows // chunk
    tile = lambda b, c: (b, c, 0)
    per_b3 = lambda b, c: (b, 0, 0)
    per_b4 = lambda b, c: (b, 0, 0, 0)
    fix2 = lambda b, c: (0, 0)
    wc = jnp.concatenate([w_conv, jnp.zeros((SUBLANES - GDN_CONV, GDN_CONV_CH), w_conv.dtype)], axis=0)
    vec = jnp.zeros((SUBLANES, LANES), F32)
    vec = vec.at[0, SM_A:SM_A + GDN_HEADS].set(a_log).at[1, SM_A:SM_A + GDN_HEADS].set(dt_bias)
    go, xp, st = pl.pallas_call(
        functools.partial(_gdn_kernel, bblk=bblk, chunk=chunk, n_valid=n_valid, hi=hi),
        grid=(batch // bblk, nc),
        in_specs=[pl.BlockSpec((bblk, chunk, GDN_CONV_CH), tile),
                  pl.BlockSpec((bblk, chunk, LANES), tile),
                  pl.BlockSpec((bblk, chunk, GDN_WIDTH), tile),
                  pl.BlockSpec((bblk, SUBLANES, GDN_CONV_CH), per_b3),
                  pl.BlockSpec((bblk, GDN_HEADS, GDN_DK, GDN_DV), per_b4),
                  pl.BlockSpec((SUBLANES, GDN_CONV_CH), fix2),
                  pl.BlockSpec((SUBLANES, LANES), fix2),
                  pl.BlockSpec((1, GDN_DV), fix2)],
        out_specs=[pl.BlockSpec((bblk, chunk, GDN_WIDTH), tile),
                   pl.BlockSpec((bblk, SUBLANES + chunk, GDN_CONV_CH), per_b3),
                   pl.BlockSpec((bblk, GDN_HEADS, GDN_DK, GDN_DV), per_b4)],
        out_shape=[jax.ShapeDtypeStruct((batch, rows, GDN_WIDTH), out_dtype),
                   jax.ShapeDtypeStruct((batch, SUBLANES + chunk, GDN_CONV_CH), F32),
                   jax.ShapeDtypeStruct((batch, GDN_HEADS, GDN_DK, GDN_DV), F32)],
        scratch_shapes=[pltpu.VMEM((bblk, SUBLANES + chunk, GDN_CONV_CH), F32)],
        compiler_params=pltpu.CompilerParams(dimension_semantics=("arbitrary", "arbitrary"),
                                             vmem_limit_bytes=VMEM_LIMIT_BYTES),
        name="gdn",
    )(qkv.reshape(batch, rows, GDN_CONV_CH), sm.reshape(batch, rows, LANES), zg.reshape(batch, rows, GDN_WIDTH),
      conv0, s0, wc, vec, w_gnorm.reshape(1, GDN_DV))
    return go.reshape(batch * rows, GDN_WIDTH), xp, st


PROMPT_TM = 512
SAMPLE_TM = 256
GDN_PROMPT_BBLK = 8
GDN_SAMPLE_BBLK = 8
NSA_SAMPLE_NB = 2


def _layer_prompt(h, lw, final, w_final, win_buf):
    w_norm, w_pack, cmp_k_w, cmp_v_w, w_conv, a_log, dt_bias, w_gnorm, w_out = lw
    batch, seq, d = h.shape
    x2d = h.reshape(batch * seq, d)
    tables = _rope_tables(jnp.arange(seq, dtype=jnp.int32))
    (q, qr, ck, cv, sk, sv, wk, wv, kvb, zn, qkv, zg, sm, ck_t, cv_t, sk_t, sv_t, wk_t, wv_t) = _project(
        x2d, w_norm, w_pack.astype(BF16), tables, seq, PROMPT_TM, False, BF16, True)
    nsa = _nsa_prompt(q, qr, kvb, ck, cv, sm, zn, cmp_k_w, cmp_v_w, batch, seq)
    conv0 = jnp.zeros((batch, SUBLANES, GDN_CONV_CH), F32)
    s0 = jnp.zeros((batch, GDN_HEADS, GDN_DK, GDN_DV), F32)
    go, xp, st = _gdn(qkv, sm, zg, conv0, s0, w_conv, a_log, dt_bias, w_gnorm,
                      batch, seq, GDN_PROMPT_BBLK, GDN_CHUNK, GDN_CHUNK, False, BF16)
    y = _out_project(x2d, nsa, go, w_out.astype(BF16), w_final, PROMPT_TM, final)
    conv_new = xp[:, SUBLANES + GDN_CHUNK - (GDN_CONV - 1):SUBLANES + GDN_CHUNK]
    from_t = lambda t: jnp.transpose(t.reshape(batch, KV_HEADS, HEAD_DIM, t.shape[-1]), (0, 3, 1, 2))
    lead = ((0, 0), (0, 0), (max(win_buf - seq, 0), 0))
    win = lambda t: from_t(jnp.pad(t, lead)[:, :, -win_buf:])
    return y.reshape(batch, seq, d), (from_t(ck_t), from_t(cv_t), from_t(sk_t), from_t(sv_t), win(wk_t), win(wv_t),
                                      conv_new, st)


def _layer_sample(h8, n_new, caches, page_table, lw, final, w_final):
    w_norm, w_pack, cmp_k_w, cmp_v_w, w_conv, a_log, dt_bias, w_gnorm, w_out = lw
    pe_rows = lambda pe2: jnp.pad(pe2, ((0, LANES - pe2.shape[0]), (0, 0))).astype(BF16)
    cmp_k_t = (cmp_k_w[0].T, cmp_k_w[1].T, pe_rows(cmp_k_w[2]))
    cmp_v_t = (cmp_v_w[0].T, cmp_v_w[1].T, pe_rows(cmp_v_w[2]))
    c_cmp_k, c_cmp_v, c_slc_k, c_slc_v, c_win_k, c_win_v, s_conv, s_gdn = caches
    batch, nr, d = h8.shape
    past = page_table.shape[1] * c_cmp_k.shape[1]
    x2d = h8.reshape(batch * nr, d)
    tables = _rope_tables(past + jnp.arange(nr, dtype=jnp.int32))
    tables = tuple(jnp.tile(t, (SAMPLE_TM // nr, 1)) for t in tables)
    (q, qr, ck, cv, sk, sv, wk, wv, _, zn, qkv, zg, sm) = _project(
        x2d, w_norm, w_pack, tables, SAMPLE_TM, SAMPLE_TM, True, F32, False)
    nsa, win_k_t, win_v_t = _nsa_sample_t(q, qr, sk, sv, wk, wv, sm, zn, c_win_k, c_win_v, c_cmp_k, c_cmp_v,
                                          c_slc_k, c_slc_v, page_table, cmp_k_t, cmp_v_t, n_new, NSA_SAMPLE_NB)
    conv0 = jnp.pad(s_conv, ((0, 0), (SUBLANES - (GDN_CONV - 1), 0), (0, 0)))
    go, xp, st = _gdn(qkv, sm, zg, conv0, s_gdn, w_conv, a_log, dt_bias, w_gnorm,
                      batch, nr, GDN_SAMPLE_BBLK, nr, n_new, True, F32)
    y = _out_project(x2d, nsa, go, w_out.astype(BF16), w_final, SAMPLE_TM, final)
    kv4 = lambda t: t.reshape(batch, nr, KV_HEADS, HEAD_DIM)[:, :n_new]
    from_t = lambda t: jnp.transpose(t.reshape(batch, KV_HEADS, HEAD_DIM, t.shape[-1]), (0, 3, 1, 2))
    conv_new = xp[:, SUBLANES + n_new - (GDN_CONV - 1):SUBLANES + n_new]
    return y.reshape(batch, nr, d), (kv4(ck), kv4(cv), kv4(sk), kv4(sv), from_t(win_k_t), from_t(win_v_t),
                                     conv_new, st)


def kernel(x_prompt, x_sample, cache_cmp_k, cache_cmp_v, cache_slc_k, cache_slc_v, cache_win_k, cache_win_v, state_conv, state_gdn, page_table, w_norm, w_in, pe_cmp_k, w_cmp_k1, w_cmp_k2, pe_cmp_v, w_cmp_v1, w_cmp_v2, w_conv, a_log, dt_bias, w_gdn_norm, w_out, w_final_norm):
    depth = w_in.shape[0]
    n_new = x_sample.shape[1]
    win_buf = cache_win_k.shape[2]
    h_p = x_prompt
    h_s = jnp.pad(x_sample, ((0, 0), (0, SAMPLE_ROWS - n_new), (0, 0)))
    st_p, st_s = [], []
    for layer in range(depth):
        lw = (w_norm[layer], _pack_w_in(w_in[layer]),
              _compress_weights(pe_cmp_k[layer], w_cmp_k1[layer], w_cmp_k2[layer]),
              _compress_weights(pe_cmp_v[layer], w_cmp_v1[layer], w_cmp_v2[layer]),
              w_conv[layer], a_log[layer], dt_bias[layer], w_gdn_norm[layer], w_out[layer])
        final = layer == depth - 1
        h_p, sp = _layer_prompt(h_p, lw, final, w_final_norm, win_buf)
        caches = (cache_cmp_k[layer], cache_cmp_v[layer], cache_slc_k[layer], cache_slc_v[layer],
                  cache_win_k[layer], cache_win_v[layer], state_conv[layer], state_gdn[layer])
        h_s, ss = _layer_sample(h_s, n_new, caches, page_table, lw, final, w_final_norm)
        st_p.append(sp)
        st_s.append(ss)
    outs = [h_p, h_s[:, :n_new]]
    for i in range(8):
        outs.append(jnp.stack([s[i] for s in st_p]))
        outs.append(jnp.stack([s[i] for s in st_s]))
    return tuple(outs)
```

```python
import functools

import numpy as np
import jax
import jax.numpy as jnp
from jax import lax
from jax.experimental import pallas as pl
from jax.experimental.pallas import tpu as pltpu

F32 = jnp.float32
BF16 = jnp.bfloat16

D_MODEL = 1024
HEAD_DIM = 64
Q_HEADS = 8
KV_HEADS = 2
GROUP = Q_HEADS // KV_HEADS
NSA_WIDTH = Q_HEADS * HEAD_DIM
KV_WIDTH = KV_HEADS * HEAD_DIM
CMP_LEN = 32
CMP_STRIDE = 16
CMP_HIDDEN = 128
SEL_BLOCK = 64
TOP_N = 8
WINDOW = 512
FORCE_BONUS = 1.0e4
ROT_DIM = HEAD_DIM // 4
ROPE_THETA = 500000.0
GDN_DK = 128
GDN_DV = 128
GDN_HEADS = 4
GDN_WIDTH = GDN_HEADS * GDN_DV
GDN_CONV = 4
GDN_CONV_CH = 3 * GDN_WIDTH
GDN_CHUNK = 64
NORM_EPS = 1e-6
MASK_VALUE = -1e30

LANES = 128
SUBLANES = 8
VMEM_LIMIT_BYTES = 56 * 1024 * 1024

QPAD_WIDTH = Q_HEADS * LANES
C_Q = 0
C_KV = C_Q + NSA_WIDTH
C_ZN = C_KV + 6 * KV_WIDTH
C_QKV = C_ZN + NSA_WIDTH
C_ZG = C_QKV + GDN_CONV_CH
C_SM = C_ZG + GDN_WIDTH
N_PACK = C_SM + LANES
SM_B = 3 * Q_HEADS
SM_A = SM_B + GDN_HEADS

TQ = 128
CK_SLC = 256
N_SUB = 128
N_CMP = N_SUB - CMP_LEN // CMP_STRIDE + 1
SAMPLE_ROWS = 8
KVB_SK = 0
KVB_SV = KVB_SK + KV_WIDTH
KVB_WK = KVB_SV + 2 * KV_WIDTH
KVB_WV = KVB_WK + KV_WIDTH
KVB_WIDTH = KVB_WV + 2 * KV_WIDTH
KVB_COLS = (KVB_SK, KVB_SV, KVB_WK, KVB_WV)
WIN_SPAN = WINDOW + TQ


def _pack_w_in(w_in):
    o_gate = NSA_WIDTH + 6 * KV_WIDTH
    o_zn = o_gate + 3 * Q_HEADS
    o_qkv = o_zn + NSA_WIDTH
    o_b = o_qkv + GDN_CONV_CH
    o_a = o_b + GDN_HEADS
    o_zg = o_a + GDN_HEADS
    wt = w_in.T
    pad = jnp.zeros((LANES - SM_A - GDN_HEADS, wt.shape[1]), wt.dtype)
    return jnp.concatenate(
        [wt[:o_gate], wt[o_zn:o_qkv], wt[o_qkv:o_b], wt[o_zg:], wt[o_gate:o_zn], wt[o_b:o_a], wt[o_a:o_zg], pad],
        axis=0)


def _rope_tables(pos):
    half = ROT_DIM // 2
    inv = ROPE_THETA ** (-(jnp.arange(half, dtype=F32) * 2.0 / ROT_DIM))
    ang = pos.astype(F32)[:, None] * inv[None, :]
    cos, sin = jnp.cos(ang), jnp.sin(ang)
    n = pos.shape[0]
    one = jnp.ones((n, HEAD_DIM - ROT_DIM), F32)
    zero = jnp.zeros((n, HEAD_DIM - ROT_DIM), F32)
    zh = jnp.zeros((n, half), F32)
    c64 = jnp.concatenate([cos, cos, one], axis=1)
    a64 = jnp.concatenate([zh, sin, zero], axis=1)
    b64 = jnp.concatenate([-sin, zh, zero], axis=1)
    tile = lambda t: jnp.concatenate([t, t], axis=1)
    return tile(c64), tile(a64), tile(b64)


def _rope128(x, c, a, b):
    half = ROT_DIM // 2
    return x * c + pltpu.roll(x, half, 1) * a + pltpu.roll(x, LANES - half, 1) * b


_NN = (((1,), (0,)), ((), ()))
_NT = (((1,), (1,)), ((), ()))
_TN = (((0,), (0,)), ((), ()))


def _split_bf16(x):
    hi = x.astype(BF16)
    return hi, (x - hi.astype(F32)).astype(BF16)


def _mm(a, b, dims=_NN, hi=False):
    dot = lambda x, y: lax.dot_general(x, y, dims, preferred_element_type=F32)
    if hi:
        a_hi, a_lo = _split_bf16(a.astype(F32))
        b_hi, b_lo = _split_bf16(b.astype(F32))
        return dot(a_hi, b_hi) + (dot(a_lo, b_hi) + dot(a_hi, b_lo))
    return dot(a.astype(BF16), b.astype(BF16))


def _mm_exact_lhs(a_bf16, x, dims=_NN):
    x1 = x.astype(BF16)
    r1 = x - x1.astype(F32)
    x2 = r1.astype(BF16)
    x3 = (r1 - x2.astype(F32)).astype(BF16)
    dot = lambda t: lax.dot_general(a_bf16, t, dims, preferred_element_type=F32)
    return dot(x1) + dot(x2) + dot(x3)


def _sigmoid(x):
    return 1.0 / (1.0 + jnp.exp(-x))


def _silu(x):
    return x * _sigmoid(x)


def _tile_rows(x, n):
    return jnp.concatenate([x] * n, axis=0)


def _tile_lanes(x, n):
    return x if n == 1 else jnp.concatenate([x] * n, axis=1)


def _eye(n, dtype):
    r = lax.broadcasted_iota(jnp.int32, (n, n), 0)
    c = lax.broadcasted_iota(jnp.int32, (n, n), 1)
    return (r == c).astype(dtype)


def _proj_kernel(x_ref, wn_ref, w_ref, c_ref, a_ref, b_ref,
                 q_ref, qr_ref, ck_ref, cv_ref, sk_ref, sv_ref, wk_ref, wv_ref, kvb_ref,
                 zn_ref, qkv_ref, zg_ref, sm_ref, *t_refs, hi_gdn):
    x = x_ref[...]
    ms = jnp.mean(x * x, axis=-1, keepdims=True)
    xn = x * lax.rsqrt(ms + NORM_EPS) * wn_ref[...]
    xb = xn.astype(BF16)
    c, a, b = c_ref[...], a_ref[...], b_ref[...]
    scale = HEAD_DIM ** -0.5
    lane = lax.broadcasted_iota(jnp.int32, (x.shape[0], LANES), 1)
    half_mask = (lane < HEAD_DIM, lane >= HEAD_DIM)
    for j in range(NSA_WIDTH // (2 * LANES)):
        q2 = _mm(xb, w_ref[C_Q + 2 * j * LANES:C_Q + 2 * (j + 1) * LANES, :], _NT)
        for jj in range(2):
            raw = q2[:, jj * LANES:(jj + 1) * LANES]
            for src, dst in ((raw, q_ref), (_rope128(raw, c, a, b), qr_ref)):
                swapped = pltpu.roll(src, HEAD_DIM, 1)
                for par in range(2):
                    head = 2 * (2 * j + jj) + par
                    kvh = head // GROUP
                    val = jnp.where(half_mask[kvh], src if par == kvh else swapped, 0.0)
                    dst[:, head * LANES:(head + 1) * LANES] = (val * scale).astype(dst.dtype)
    kv_refs = (ck_ref, cv_ref, sk_ref, sv_ref, wk_ref, wv_ref)
    kv2 = [_mm(xb, w_ref[C_KV + j * LANES:C_KV + (j + 2) * LANES, :], _NT) for j in range(0, 6, 2)]
    for j in range(6):
        kj = kv2[j // 2][:, (j % 2) * LANES:(j % 2 + 1) * LANES]
        if j in (2, 4):
            kj = _rope128(kj, c, a, b)
        kv_refs[j][...] = kj
        if j >= 2:
            lo = KVB_COLS[j - 2]
            kvb_ref[:, lo:lo + LANES] = kj.astype(BF16)
        if t_refs:
            t_refs[j][0] = kj.T
    ones = jnp.ones((x.shape[0], LANES), BF16)
    kvb_ref[:, KVB_SV + LANES:KVB_SV + 2 * LANES] = ones
    kvb_ref[:, KVB_WV + LANES:KVB_WV + 2 * LANES] = ones
    zn_ref[...] = _mm(xb, w_ref[C_ZN:C_QKV, :], _NT)
    xg = xn if hi_gdn else xb
    for j in range(3):
        lo = C_QKV + j * GDN_WIDTH
        qkv_ref[:, j * GDN_WIDTH:(j + 1) * GDN_WIDTH] = _mm(xg, w_ref[lo:lo + GDN_WIDTH, :], _NT, hi=hi_gdn)
    zg_ref[...] = _mm(xb, w_ref[C_ZG:C_SM, :], _NT)
    sm_ref[...] = _mm(xg, w_ref[C_SM:N_PACK, :], _NT, hi=hi_gdn)


def _project(x2d, w_norm, w_pack, tables, rows_per_seq, tm, hi_gdn, q_dtype, emit_t):
    n = x2d.shape[0]
    nt = rows_per_seq // tm
    row = lambda i: (i, 0)
    tab = lambda i: (i % nt, 0)
    fix = lambda i: (0, 0)
    widths = (QPAD_WIDTH, QPAD_WIDTH) + (KV_WIDTH,) * 6 + (KVB_WIDTH, NSA_WIDTH, GDN_CONV_CH, GDN_WIDTH, LANES)
    dtypes = (q_dtype, q_dtype) + (F32,) * 6 + (BF16, F32, F32, F32, F32)
    out_specs = [pl.BlockSpec((tm, w), row) for w in widths]
    out_shape = [jax.ShapeDtypeStruct((n, w), d) for w, d in zip(widths, dtypes)]
    if emit_t:
        out_specs += [pl.BlockSpec((1, KV_WIDTH, tm), lambda i: (i // nt, 0, i % nt))] * 6
        out_shape += [jax.ShapeDtypeStruct((n // rows_per_seq, KV_WIDTH, rows_per_seq), F32)] * 6
    return pl.pallas_call(
        functools.partial(_proj_kernel, hi_gdn=hi_gdn),
        grid=(n // tm,),
        in_specs=[pl.BlockSpec((tm, D_MODEL), row),
                  pl.BlockSpec((1, D_MODEL), fix),
                  pl.BlockSpec((N_PACK, D_MODEL), fix),
                  pl.BlockSpec((tm, LANES), tab),
                  pl.BlockSpec((tm, LANES), tab),
                  pl.BlockSpec((tm, LANES), tab)],
        out_specs=out_specs,
        out_shape=out_shape,
        compiler_params=pltpu.CompilerParams(dimension_semantics=("arbitrary",),
                                             vmem_limit_bytes=VMEM_LIMIT_BYTES),
        name="in_proj",
    )(x2d, w_norm.reshape(1, D_MODEL), w_pack, *tables)


def _out_kernel(x_ref, nsa_ref, gdn_ref, w_ref, wf_ref, y_ref, *, final):
    mix = jnp.concatenate([nsa_ref[...].astype(BF16), gdn_ref[...].astype(BF16)], axis=1)
    h = x_ref[...] + _mm(mix, w_ref[...])
    if final:
        ms = jnp.mean(h * h, axis=-1, keepdims=True)
        h = h * lax.rsqrt(ms + NORM_EPS) * wf_ref[...]
    y_ref[...] = h


def _out_project(x2d, nsa, gdn, w_out, w_final, tm, final):
    n = x2d.shape[0]
    row = lambda i: (i, 0)
    fix = lambda i: (0, 0)
    return pl.pallas_call(
        functools.partial(_out_kernel, final=final),
        grid=(n // tm,),
        in_specs=[pl.BlockSpec((tm, D_MODEL), row),
                  pl.BlockSpec((tm, NSA_WIDTH), row),
                  pl.BlockSpec((tm, GDN_WIDTH), row),
                  pl.BlockSpec((D_MODEL, D_MODEL), fix),
                  pl.BlockSpec((1, D_MODEL), fix)],
        out_specs=pl.BlockSpec((tm, D_MODEL), row),
        out_shape=jax.ShapeDtypeStruct((n, D_MODEL), F32),
        compiler_params=pltpu.CompilerParams(dimension_semantics=("arbitrary",),
                                             vmem_limit_bytes=VMEM_LIMIT_BYTES),
        name="out_proj",
    )(x2d, nsa, gdn, w_out, w_final.reshape(1, D_MODEL))


def _compress_weights(pe, w1, w2):
    half = CMP_STRIDE * HEAD_DIM
    z = jnp.zeros((CMP_STRIDE, HEAD_DIM, CMP_HIDDEN), w1.dtype)

    def place(wpart, h):
        wp = wpart.reshape(CMP_STRIDE, HEAD_DIM, CMP_HIDDEN)
        parts = [wp, z] if h == 0 else [z, wp]
        return jnp.stack(parts, axis=1).reshape(CMP_STRIDE * KV_WIDTH, CMP_HIDDEN)

    w1big = jnp.concatenate([place(w1[:half], 0), place(w1[:half], 1),
                             place(w1[half:], 0), place(w1[half:], 1)], axis=1)
    zz = jnp.zeros_like(w2)
    w2big = jnp.concatenate([jnp.concatenate([w2, zz], axis=1), jnp.concatenate([zz, w2], axis=1)], axis=0)
    pe_a = jnp.tile(pe[:CMP_STRIDE], (1, KV_HEADS)).reshape(1, CMP_STRIDE * KV_WIDTH)
    pe_b = jnp.tile(pe[CMP_STRIDE:], (1, KV_HEADS)).reshape(1, CMP_STRIDE * KV_WIDTH)
    pe2 = jnp.concatenate([pe_a, pe_b, jnp.zeros((SUBLANES - 2, CMP_STRIDE * KV_WIDTH), pe.dtype)], axis=0)
    return w1big.astype(BF16), w2big.astype(BF16), pe2


def _compress(sub_rows, w1_ref, w2_ref, pe_ref):
    hid = 2 * CMP_HIDDEN
    w1 = w1_ref[...]
    ab = _mm(sub_rows, w1)
    pe = pe_ref[...]
    pe_hi = pe.astype(BF16)
    pe_lo = (pe - pe_hi.astype(F32)).astype(BF16)
    r = _mm(pe_hi, w1) + _mm(pe_lo, w1)
    bias = r[0:1, :hid] + r[1:2, hid:]
    h = ab[:, :hid] + pltpu.roll(ab[:, hid:], N_SUB - 1, 0) + bias
    return _mm(_silu(h), w2_ref[...])


def _overlap_t(ns):
    c0 = np.arange(N_CMP)[None, :] * CMP_STRIDE
    b0 = np.arange(ns)[:, None] * SEL_BLOCK
    ov = np.minimum(c0 + CMP_LEN, b0 + SEL_BLOCK) - np.maximum(c0, b0)
    out = np.zeros((LANES, LANES), np.float32)
    out[:ns, :N_CMP] = np.maximum(ov, 0) / CMP_LEN
    return jnp.asarray(out, BF16)


def _select_blocks(imp_t, n_idx, q_blk, n_rows):
    forced = (n_idx == 0) | (n_idx == q_blk) | (n_idx == q_blk - 1)
    allowed = n_idx <= q_blk
    v = jnp.where(allowed, imp_t + FORCE_BONUS * forced.astype(F32), MASK_VALUE)
    rank = jnp.zeros(v.shape, F32)
    for j in range(n_rows):
        vj = v[j:j + 1, :]
        ge = jnp.where(vj >= v, 1.0, 0.0)
        gt = jnp.where(vj > v, 1.0, 0.0)
        rank = rank + jnp.where(n_idx > j, ge, gt)
    return ((rank < TOP_N) & allowed).astype(F32)


def _merge_heads(o_sum, kvh, tq):
    lane = lax.broadcasted_iota(jnp.int32, (tq, LANES), 1)
    slabs = []
    for pair in range(GROUP // 2):
        halves = []
        for par in range(2):
            o = o_sum[(2 * pair + par) * tq:(2 * pair + par + 1) * tq]
            halves.append(o if par == kvh else pltpu.roll(o, HEAD_DIM, 1))
        slabs.append(jnp.where(lane < HEAD_DIM, halves[0], halves[1]))
    return slabs


def _nsa_prompt_kernel(q_ref, qr_ref, kvb_ref, ckr_ref, cvr_ref, sm_ref, zn_ref,
                       wk1_ref, wk2_ref, pek_ref, wv1_ref, wv2_ref, pev_ref, ovt_ref, e_ref,
                       o_ref, ck_s, cv_s, s_s, mrun_s, m_s, acc_s):
    i = pl.program_id(1)

    @pl.when(i == 0)
    def _():
        ck_s[...] = _compress(ckr_ref[0], wk1_ref, wk2_ref, pek_ref).astype(BF16)
        cv_s[...] = _compress(cvr_ref[0], wv1_ref, wv2_ref, pev_ref).astype(BF16)

    kvs = range(KV_HEADS)
    t0 = i * TQ
    rows = GROUP * TQ
    row = lax.broadcasted_iota(jnp.int32, (TQ, LANES), 0)
    col = lax.broadcasted_iota(jnp.int32, (TQ, LANES), 1)
    cvalid = (CMP_STRIDE * col + (CMP_LEN - 1) <= t0 + row) & (col < N_CMP)
    cm = _tile_rows(cvalid, GROUP)
    cm_f = cm.astype(F32)
    sig = _sigmoid(sm_ref[...])

    def gate(branch, kvh):
        cols = [sig[:, branch * Q_HEADS + h:branch * Q_HEADS + h + 1] for h in range(kvh * GROUP, (kvh + 1) * GROUP)]
        return jnp.concatenate([jnp.broadcast_to(c, (TQ, LANES)) for c in cols], axis=0)

    def stack(ref, kvh):
        return jnp.concatenate([ref[:, h * LANES:(h + 1) * LANES] for h in range(kvh * GROUP, (kvh + 1) * GROUP)], axis=0)

    q_raw = [stack(q_ref, k) for k in kvs]
    q_rot = [stack(qr_ref, k) for k in kvs]

    ck, cv = ck_s[...], cv_s[...]
    s = [jnp.where(cm, _mm(q_raw[k], ck, _NT), MASK_VALUE) for k in kvs]
    e = [jnp.exp(s[k] - jnp.max(s[k], axis=1, keepdims=True)) for k in kvs]
    p = [e[k] / jnp.sum(e[k], axis=1, keepdims=True) * cm_f for k in kvs]
    o_sum = [gate(0, k) * _mm(p[k], cv) for k in kvs]
    p_sum = jnp.concatenate([p[k][0:TQ] + p[k][TQ:2 * TQ] + p[k][2 * TQ:3 * TQ] + p[k][3 * TQ:4 * TQ] for k in kvs],
                            axis=0)
    ns_rows = 32
    nq = KV_HEADS * TQ
    imp_t = _mm_exact_lhs(ovt_ref[...], p_sum, _NT)[0:ns_rows]
    n_idx = lax.broadcasted_iota(jnp.int32, (ns_rows, nq), 0)
    qb_t = (t0 + lax.broadcasted_iota(jnp.int32, (ns_rows, nq), 1) % TQ) // SEL_BLOCK
    sel_t = _select_blocks(imp_t, n_idx, qb_t, ns_rows)
    sel_t = jnp.concatenate([sel_t, jnp.zeros((LANES - ns_rows, nq), F32)], axis=0).astype(BF16)
    eye = _eye(TQ, BF16)
    sel = [_mm(eye, sel_t[:, k * TQ:(k + 1) * TQ], _NT).astype(BF16) for k in kvs]

    col_s = lax.broadcasted_iota(jnp.int32, (TQ, CK_SLC), 1)
    row_s = lax.broadcasted_iota(jnp.int32, (TQ, CK_SLC), 0)
    n_chunks = (t0 + TQ + CK_SLC - 1) // CK_SLC
    mrun_s[...] = jnp.full(mrun_s.shape, MASK_VALUE, F32)

    def score_chunks(js):
        k0 = [pl.multiple_of(j * CK_SLC, CK_SLC) for j in js]
        k_c = [kvb_ref[pl.ds(k0[n], CK_SLC), KVB_SK:KVB_SK + LANES] for n in range(len(js))]
        sc = [[_mm(q_rot[k], k_c[n], _NT) for k in kvs] for n in range(len(js))]
        picked = [[_mm(sel[k], e_ref[js[n]]) for k in kvs] for n in range(len(js))]
        for k in kvs:
            run = mrun_s[k]
            for n, j in enumerate(js):
                causal = k0[n] + col_s <= t0 + row_s
                bias = jnp.where((picked[n][k] > 0.5) & causal, 0.0, MASK_VALUE)
                sk = sc[n][k] + _tile_rows(bias, GROUP)
                s_s[k, j] = sk
                run = jnp.maximum(run, jnp.maximum(sk[:, :LANES], sk[:, LANES:]))
            mrun_s[k] = run

    def value_chunks(js):
        k0 = [pl.multiple_of(j * CK_SLC, CK_SLC) for j in js]
        v_c = [kvb_ref[pl.ds(k0[n], CK_SLC), KVB_SV:KVB_SV + 2 * LANES] for n in range(len(js))]
        pr = [[jnp.exp(s_s[k, js[n]] - _tile_lanes(m_s[k], CK_SLC // LANES)) for k in kvs] for n in range(len(js))]
        pv = [[_mm(pr[n][k], v_c[n]) for k in kvs] for n in range(len(js))]
        for k in kvs:
            acc_s[k] += sum(pv[n][k] for n in range(len(js)))

    def paired(fn):
        def pair_body(jj, carry):
            fn([2 * jj, 2 * jj + 1])
            return carry

        lax.fori_loop(0, n_chunks // 2, pair_body, 0)

        @pl.when(n_chunks % 2 == 1)
        def _():
            fn([n_chunks - 1])

    paired(score_chunks)
    for k in kvs:
        m_s[k] = jnp.broadcast_to(jnp.max(mrun_s[k], axis=1, keepdims=True), (rows, LANES))
    acc_s[...] = jnp.zeros(acc_s.shape, F32)

    paired(value_chunks)
    for k in kvs:
        acc = acc_s[k]
        o_sum[k] = o_sum[k] + gate(1, k) * (acc[:, :LANES] / acc[:, LANES:])

    ks0 = pl.multiple_of(jnp.maximum(t0 - WINDOW, 0), TQ)
    k_w = kvb_ref[pl.ds(ks0, WIN_SPAN), KVB_WK:KVB_WK + LANES]
    v_w = kvb_ref[pl.ds(ks0, WIN_SPAN), KVB_WV:KVB_WV + 2 * LANES]
    diff = (t0 + lax.broadcasted_iota(jnp.int32, (TQ, WIN_SPAN), 0)) - (ks0 + lax.broadcasted_iota(jnp.int32, (TQ, WIN_SPAN), 1))
    bias_w = _tile_rows(jnp.where((diff >= 0) & (diff <= WINDOW), 0.0, MASK_VALUE), GROUP)
    sw = [_mm(q_rot[k], k_w, _NT) + bias_w for k in kvs]
    pw = [jnp.exp(sw[k] - jnp.max(sw[k], axis=1, keepdims=True)) for k in kvs]
    rw = [_mm(pw[k], v_w) for k in kvs]
    for k in kvs:
        o_all = o_sum[k] + gate(2, k) * (rw[k][:, :LANES] / rw[k][:, LANES:])
        for pair, slab in enumerate(_merge_heads(o_all, k, TQ)):
            lo = (k * (GROUP // 2) + pair) * LANES
            o_ref[:, lo:lo + LANES] = (slab * _silu(zn_ref[:, lo:lo + LANES])).astype(o_ref.dtype)


def _nsa_prompt(q, qr, kvb, ck, cv, sm, zn, cmp_k_w, cmp_v_w, batch, seq):
    nt = seq // TQ
    ns = seq // SEL_BLOCK
    sub_w = CMP_STRIDE * KV_WIDTH
    keys = np.arange(seq)
    e = (np.arange(LANES)[None, :, None] == (keys // SEL_BLOCK).reshape(seq // CK_SLC, 1, CK_SLC))
    e = jnp.asarray(e, BF16)
    tile = lambda b, i: (b * nt + i, 0)
    per_b = lambda b, i: (b, 0)
    per_b3 = lambda b, i: (b, 0, 0)
    fix2 = lambda b, i: (0, 0)
    fix3 = lambda b, i: (0, 0, 0)
    wspecs = [pl.BlockSpec((sub_w, 4 * CMP_HIDDEN), fix2),
              pl.BlockSpec((2 * CMP_HIDDEN, KV_WIDTH), fix2),
              pl.BlockSpec((SUBLANES, sub_w), fix2)]
    rows = GROUP * TQ
    return pl.pallas_call(
        _nsa_prompt_kernel,
        grid=(batch, nt),
        in_specs=[pl.BlockSpec((TQ, QPAD_WIDTH), tile),
                  pl.BlockSpec((TQ, QPAD_WIDTH), tile),
                  pl.BlockSpec((seq, KVB_WIDTH), per_b),
                  pl.BlockSpec((1, seq // CMP_STRIDE, sub_w), per_b3),
                  pl.BlockSpec((1, seq // CMP_STRIDE, sub_w), per_b3),
                  pl.BlockSpec((TQ, LANES), tile),
                  pl.BlockSpec((TQ, NSA_WIDTH), tile)] + wspecs + wspecs + [
                  pl.BlockSpec((LANES, LANES), fix2),
                  pl.BlockSpec((seq // CK_SLC, LANES, CK_SLC), fix3)],
        out_specs=pl.BlockSpec((TQ, NSA_WIDTH), tile),
        out_shape=jax.ShapeDtypeStruct((batch * seq, NSA_WIDTH), BF16),
        scratch_shapes=[pltpu.VMEM((N_SUB, KV_WIDTH), BF16), pltpu.VMEM((N_SUB, KV_WIDTH), BF16),
                        pltpu.VMEM((KV_HEADS, seq // CK_SLC, rows, CK_SLC), F32),
                        pltpu.VMEM((KV_HEADS, rows, LANES), F32), pltpu.VMEM((KV_HEADS, rows, LANES), F32),
                        pltpu.VMEM((KV_HEADS, rows, 2 * LANES), F32)],
        compiler_params=pltpu.CompilerParams(dimension_semantics=("arbitrary", "arbitrary"),
                                             vmem_limit_bytes=VMEM_LIMIT_BYTES),
        name="nsa_prompt",
    )(q, qr, kvb, ck.reshape(batch, seq // CMP_STRIDE, sub_w), cv.reshape(batch, seq // CMP_STRIDE, sub_w),
      sm, zn, *cmp_k_w, *cmp_v_w, _overlap_t(ns), e)


def _nsa_sample_t_kernel(pt_ref, q_ref, qr_ref, skn_ref, svn_ref, wkn_ref, wvn_ref, sm_ref, zn_ref,
                         wkc_ref, wvc_ref, cmpk_hbm, cmpv_hbm, slck_hbm, slcv_hbm,
                         w1k_ref, w2k_ref, pek_ref, w1v_ref, w2v_ref, pev_ref, ovt_ref, e_ref,
                         o_ref, wko_ref, wvo_ref, kt_s, vt_s, kwt_s, vwt_s, pages_s, sems,
                         *, nb, n_pages, page, past, n_new):
    step = pl.program_id(0)
    last = pl.num_programs(0) - 1
    slot = step % 2
    caches = (cmpk_hbm, cmpv_hbm, slck_hbm, slcv_hbm)
    page_ids = [(j, c, p) for j in range(nb) for c in range(len(caches)) for p in range(n_pages)]

    def page_copy(at_step, at_slot, j, c, p):
        src = caches[c].at[pt_ref[at_step * nb + j, p]]
        return pltpu.make_async_copy(src, pages_s.at[at_slot, j, c, p], sems.at[at_slot, c])

    @pl.when(step == 0)
    def _():
        for ids in page_ids:
            page_copy(0, 0, *ids).start()

    nxt = jnp.minimum(step + 1, last)
    for ids in page_ids:
        page_copy(nxt, 1 - slot, *ids).start()
    for ids in page_ids:
        page_copy(step, slot, *ids).wait()

    cmpk, cmpv, slck, slcv = ([pages_s.at[slot, j, c, p] for j in range(nb) for p in range(n_pages)]
                              for c in range(len(caches)))
    nr = SAMPLE_ROWS
    rows = Q_HEADS * nr
    win_buf = wkc_ref.shape[2]
    bs = range(nb)
    hid = 2 * CMP_HIDDEN
    zpad = jnp.zeros((LANES - nr, LANES), F32)

    def new_rows(ref, b):
        return jnp.concatenate([ref[b * nr:(b + 1) * nr, :], zpad], axis=0).astype(BF16)

    def stack(ref, b):
        return jnp.concatenate([ref[b * nr:(b + 1) * nr, h * LANES:(h + 1) * LANES] for h in range(Q_HEADS)],
                               axis=0).astype(BF16)

    for b in bs:
        for p in range(n_pages):
            kt_s[b, :, p * page:(p + 1) * page] = slck[b * n_pages + p][...].astype(BF16)
            vt_s[b, :, p * page:(p + 1) * page] = slcv[b * n_pages + p][...].astype(BF16)
        kwt_s[b] = wkc_ref[b].astype(BF16)
        vwt_s[b] = wvc_ref[b].astype(BF16)
    mi = lax.broadcasted_iota(jnp.int32, (page, page), 0)
    ki = lax.broadcasted_iota(jnp.int32, (page, page), 1)
    per_page = page // CMP_STRIDE
    perm = (ki == CMP_STRIDE * (mi % per_page) + mi // per_page).astype(BF16)

    def transposed_pages(pages, b):
        out = []
        for p in range(0, n_pages, 2):
            pair = jnp.concatenate([pages[b * n_pages + p][...], pages[b * n_pages + p + 1][...]], axis=0)
            both = _mm(perm, pair, _NT)
            out += [both[:, :page], both[:, page:]]
        return out

    rows_t = [[transposed_pages(pages, b) for pages in (cmpk, cmpv)] for b in bs]

    q_raw = [stack(q_ref, b) for b in bs]
    q_rot = [stack(qr_ref, b) for b in bs]

    sw = [jnp.concatenate([_mm(q_rot[b], kwt_s[b]), _mm(q_rot[b], new_rows(wkn_ref, b), _NT)], axis=1) for b in bs]
    ss = [jnp.concatenate([_mm(q_rot[b], kt_s[b]), _mm(q_rot[b], new_rows(skn_ref, b), _NT)], axis=1) for b in bs]

    def sub_blocks(b, c):
        cols = [jnp.concatenate([rows_t[b][c][p][l * per_page:(l + 1) * per_page] for p in range(n_pages)], axis=0)
                for l in range(CMP_STRIDE)]
        return jnp.concatenate(cols, axis=1)

    def activate(t):
        bias = t[0:hid, LANES:LANES + 1] + t[hid:2 * hid, LANES + 1:LANES + 2]
        return _silu(t[0:hid, 0:LANES] + pltpu.roll(t[hid:2 * hid, 0:LANES], N_SUB - 1, 1) + bias)

    ckt, cvt = [], []
    for c, (w1_ref, w2_ref, pe_ref, out) in enumerate(((w1k_ref, w2k_ref, pek_ref, ckt), (w1v_ref, w2v_ref, pev_ref, cvt))):
        pe_rows = pe_ref[...]
        ht = [_mm(w1_ref[...], jnp.concatenate([sub_blocks(b, c).astype(BF16), pe_rows], axis=0), _NT) for b in bs]
        out.extend(_mm(w2_ref[...], activate(ht[b])) for b in bs)

    tok = lax.broadcasted_iota(jnp.int32, (nr, LANES), 0)
    col = lax.broadcasted_iota(jnp.int32, (nr, LANES), 1)
    cm = _tile_rows((CMP_STRIDE * col + (CMP_LEN - 1) <= past + tok) & (col < N_CMP), Q_HEADS)
    cm_f = cm.astype(F32)
    sc = [jnp.where(cm, _mm(q_raw[b], ckt[b]), MASK_VALUE) for b in bs]
    ec = [jnp.exp(sc[b] - jnp.max(sc[b], axis=1, keepdims=True)) for b in bs]
    pc = [ec[b] / jnp.sum(ec[b], axis=1, keepdims=True) * cm_f for b in bs]
    o_cmp = [_mm(pc[b], cvt[b], _NT) for b in bs]
    p_sum = jnp.concatenate(
        [sum(pc[b][(kvh * GROUP + g) * nr:(kvh * GROUP + g + 1) * nr] for g in range(GROUP))
         for b in bs for kvh in range(KV_HEADS)], axis=0)
    ns = (past + n_new + SEL_BLOCK - 1) // SEL_BLOCK
    ns_rows = -(-ns // SUBLANES) * SUBLANES
    nq = nb * KV_HEADS * nr
    imp_t = _mm_exact_lhs(ovt_ref[...], p_sum, _NT)[0:ns_rows]
    n_idx = lax.broadcasted_iota(jnp.int32, (ns_rows, nq), 0)
    qb_t = (past + lax.broadcasted_iota(jnp.int32, (ns_rows, nq), 1) % nr) // SEL_BLOCK
    sel_t = _select_blocks(imp_t, n_idx, qb_t, ns)
    sel_t = jnp.concatenate([sel_t, jnp.zeros((LANES - ns_rows, nq), F32)], axis=0)
    sel = _mm(_eye(nq, BF16), sel_t, _NT)

    all_keys = past + LANES
    q_pos = past + lax.broadcasted_iota(jnp.int32, (rows, all_keys), 0) % nr
    causal = lax.broadcasted_iota(jnp.int32, (rows, all_keys), 1) <= q_pos
    o_slc = []
    for b in bs:
        sel_rows = jnp.concatenate(
            [_tile_rows(sel[(b * KV_HEADS + kvh) * nr:(b * KV_HEADS + kvh + 1) * nr], GROUP) for kvh in range(KV_HEADS)],
            axis=0)
        ok = (_mm(sel_rows, e_ref[...]) > 0.5) & causal
        s = ss[b] + jnp.where(ok, 0.0, MASK_VALUE)
        pr = jnp.exp(s - jnp.max(s, axis=1, keepdims=True))
        o = _mm(pr[:, :past], vt_s[b], _NT) + _mm(pr[:, past:], new_rows(svn_ref, b))
        o_slc.append(o / jnp.sum(pr, axis=1, keepdims=True))

    win_keys = win_buf + LANES
    q_pos = past + lax.broadcasted_iota(jnp.int32, (rows, win_keys), 0) % nr
    diff = q_pos - (past - win_buf + lax.broadcasted_iota(jnp.int32, (rows, win_keys), 1))
    bias_w = jnp.where((diff >= 0) & (diff <= WINDOW), 0.0, MASK_VALUE)
    o_win = []
    for b in bs:
        s = sw[b] + bias_w
        pr = jnp.exp(s - jnp.max(s, axis=1, keepdims=True))
        o = _mm(pr[:, :win_buf], vwt_s[b], _NT) + _mm(pr[:, win_buf:], new_rows(wvn_ref, b))
        o_win.append(o / jnp.sum(pr, axis=1, keepdims=True))

    for b in bs:
        sig = _sigmoid(sm_ref[b * nr:(b + 1) * nr, :])

        def gate(branch):
            cols = [sig[:, branch * Q_HEADS + h:branch * Q_HEADS + h + 1] for h in range(Q_HEADS)]
            return jnp.concatenate([jnp.broadcast_to(c, (nr, LANES)) for c in cols], axis=0)

        o_all = gate(0) * o_cmp[b] + gate(1) * o_slc[b] + gate(2) * o_win[b]
        for kvh in range(KV_HEADS):
            o_kv = o_all[kvh * GROUP * nr:(kvh + 1) * GROUP * nr]
            for pair, slab in enumerate(_merge_heads(o_kv, kvh, nr)):
                lo = (kvh * (GROUP // 2) + pair) * LANES
                o_ref[b * nr:(b + 1) * nr, lo:lo + LANES] = slab * _silu(zn_ref[b * nr:(b + 1) * nr, lo:lo + LANES])

    eye_bf = _eye(LANES, BF16)
    lane = lax.broadcasted_iota(jnp.int32, (KV_WIDTH, LANES), 1)
    for cache_ref, new_ref, out_ref in ((wkc_ref, wkn_ref, wko_ref), (wvc_ref, wvn_ref, wvo_ref)):
        for b in bs:
            shifted = pltpu.roll(cache_ref[b], win_buf - n_new, 1)
            new_pad = jnp.concatenate([new_ref[b * nr:(b + 1) * nr, :], zpad], axis=0)
            new_t = _mm_exact_lhs(eye_bf, new_pad, _NT)
            tail = jnp.where(lane >= LANES - n_new, pltpu.roll(new_t, LANES - n_new, 1), shifted[:, win_buf - LANES:])
            out_ref[b] = jnp.concatenate([shifted[:, :win_buf - LANES], tail], axis=1)

    @pl.when(step == last)
    def _():
        for ids in page_ids:
            page_copy(nxt, 1 - slot, *ids).wait()


def _nsa_sample_t(q, qr, sk, sv, wk, wv, sm, zn, win_k, win_v, cmp_k, cmp_v, slc_k, slc_v, page_table,
                  cmp_k_w, cmp_v_w, n_new, nb):
    batch, n_pages = page_table.shape
    n_pool, page = cmp_k.shape[0], cmp_k.shape[1]
    past = n_pages * page
    nr = SAMPLE_ROWS
    win_buf = win_k.shape[1]
    ns = (past + n_new + SEL_BLOCK - 1) // SEL_BLOCK
    all_keys = past + LANES
    e = jnp.asarray(np.arange(LANES)[:, None] == (np.arange(all_keys) // SEL_BLOCK)[None, :], BF16)
    tview = lambda c: jnp.transpose(c, (0, 2, 3, 1)).reshape(c.shape[0], KV_WIDTH, c.shape[1])
    cmp_k, cmp_v, slc_k, slc_v, win_k, win_v = (tview(c) for c in (cmp_k, cmp_v, slc_k, slc_v, win_k, win_v))

    row = lambda b, pt: (b, 0)
    per_b3 = lambda b, pt: (b, 0, 0)
    fix2 = lambda b, pt: (0, 0)
    sub_w = CMP_STRIDE * KV_WIDTH
    wspecs = [pl.BlockSpec((4 * CMP_HIDDEN, sub_w), fix2),
              pl.BlockSpec((KV_WIDTH, 2 * CMP_HIDDEN), fix2),
              pl.BlockSpec((LANES, sub_w), fix2)]
    in_specs = ([pl.BlockSpec((nb * nr, QPAD_WIDTH), row)] * 2 + [pl.BlockSpec((nb * nr, KV_WIDTH), row)] * 4
                + [pl.BlockSpec((nb * nr, LANES), row), pl.BlockSpec((nb * nr, NSA_WIDTH), row)]
                + [pl.BlockSpec((nb, KV_WIDTH, win_buf), per_b3)] * 2
                + [pl.BlockSpec(memory_space=pl.ANY)] * 4 + wspecs + wspecs
                + [pl.BlockSpec((LANES, LANES), fix2), pl.BlockSpec((LANES, all_keys), fix2)])
    grid_spec = pltpu.PrefetchScalarGridSpec(
        num_scalar_prefetch=1,
        grid=(batch // nb,),
        in_specs=in_specs,
        out_specs=[pl.BlockSpec((nb * nr, NSA_WIDTH), row),
                   pl.BlockSpec((nb, KV_WIDTH, win_buf), per_b3), pl.BlockSpec((nb, KV_WIDTH, win_buf), per_b3)],
        scratch_shapes=[pltpu.VMEM((nb, KV_WIDTH, past), BF16), pltpu.VMEM((nb, KV_WIDTH, past), BF16),
                        pltpu.VMEM((nb, KV_WIDTH, win_buf), BF16), pltpu.VMEM((nb, KV_WIDTH, win_buf), BF16),
                        pltpu.VMEM((2, nb, 4, n_pages, KV_WIDTH, page), F32),
                        pltpu.SemaphoreType.DMA((2, 4))])
    return pl.pallas_call(
        functools.partial(_nsa_sample_t_kernel, nb=nb, n_pages=n_pages, page=page, past=past, n_new=n_new),
        grid_spec=grid_spec,
        out_shape=[jax.ShapeDtypeStruct((batch * nr, NSA_WIDTH), F32),
                   jax.ShapeDtypeStruct((batch, KV_WIDTH, win_buf), F32),
                   jax.ShapeDtypeStruct((batch, KV_WIDTH, win_buf), F32)],
        compiler_params=pltpu.CompilerParams(dimension_semantics=("arbitrary",),
                                             vmem_limit_bytes=VMEM_LIMIT_BYTES),
        name="nsa_sample",
    )(page_table, q, qr, sk, sv, wk, wv, sm, zn, win_k, win_v, cmp_k, cmp_v, slc_k, slc_v,
      *cmp_k_w, *cmp_v_w, _overlap_t(ns), e)


def _gdn_kernel(qkv_ref, sm_ref, zg_ref, conv0_ref, s0_ref, wc_ref, vec_ref, wg_ref,
                go_ref, xp_out_ref, st_ref, xp_s, *, bblk, chunk, n_valid, hi):
    c_idx = pl.program_id(1)

    @pl.when(c_idx == 0)
    def _():
        xp_s[:, 0:SUBLANES, :] = conv0_ref[...]
        st_ref[...] = s0_ref[...]

    wc = wc_ref[...]
    vec = vec_ref[...]
    row1 = lax.broadcasted_iota(jnp.int32, (chunk, LANES), 0)
    rr = lax.broadcasted_iota(jnp.int32, (chunk, chunk), 0)
    cc = lax.broadcasted_iota(jnp.int32, (chunk, chunk), 1)
    tri = rr >= cc
    tri_bf = tri.astype(BF16)
    eye_bf = _eye(LANES, BF16)

    acts, betas, decays, decay_ts, e_decs = [], [], [], [], []
    for b in range(bblk):
        xp_s[b, SUBLANES:SUBLANES + chunk, :] = qkv_ref[b]
        y = xp_s[b, SUBLANES:SUBLANES + chunk, :] * wc[GDN_CONV - 1:GDN_CONV, :]
        for j in range(GDN_CONV - 1):
            lo = SUBLANES - (GDN_CONV - 1) + j
            y = y + xp_s[b, lo:lo + chunk, :] * wc[j:j + 1, :]
        xp_out_ref[b] = xp_s[b]
        xp_s[b, 0:SUBLANES, :] = xp_s[b, chunk:chunk + SUBLANES, :]
        act = _silu(y)
        small = sm_ref[b]
        z = small + vec[1:2, :]
        softplus = jnp.maximum(z, 0.0) + jnp.log1p(jnp.exp(-jnp.abs(z)))
        g_all = -jnp.exp(vec[0:1, :]) * softplus
        if n_valid < chunk:
            valid = row1 < n_valid
            act = act * _tile_lanes(valid.astype(F32), GDN_CONV_CH // LANES)
            g_all = jnp.where(valid, g_all, 0.0)
        acts.append(act)
        betas.append(_sigmoid(small))
        decays.append(_mm_exact_lhs(tri_bf, g_all))
    for b in range(bblk):
        decay_ts.append(_mm_exact_lhs(eye_bf, decays[b], _NT))
        e_decs.append(jnp.exp(decays[b]))

    chains = [(b, h) for b in range(bblk) for h in range(GDN_HEADS)]
    qs, ks, kbs, dmasks, rhs_u, rhs_w, qds, kds, gls = [], [], [], [], [], [], [], [], []
    for b, h in chains:
        act = acts[b]
        qh = act[:, h * GDN_DK:(h + 1) * GDN_DK]
        kh = act[:, GDN_WIDTH + h * GDN_DK:GDN_WIDTH + (h + 1) * GDN_DK]
        vh = act[:, 2 * GDN_WIDTH + h * GDN_DV:2 * GDN_WIDTH + (h + 1) * GDN_DV]
        qh = qh * lax.rsqrt(jnp.sum(qh * qh, axis=-1, keepdims=True) + NORM_EPS) * (GDN_DK ** -0.5)
        kh = kh * lax.rsqrt(jnp.sum(kh * kh, axis=-1, keepdims=True) + NORM_EPS)
        beta = betas[b][:, SM_B + h:SM_B + h + 1]
        dcol = decays[b][:, SM_A + h:SM_A + h + 1]
        drow = decay_ts[b][SM_A + h:SM_A + h + 1, :]
        ed = e_decs[b][:, SM_A + h:SM_A + h + 1]
        dlast = decays[b][chunk - 1:chunk, SM_A + h:SM_A + h + 1]
        kb = kh * beta
        qs.append(qh)
        ks.append(kh)
        kbs.append(kb)
        dmasks.append(jnp.where(tri, jnp.exp(jnp.where(tri, dcol - drow, 0.0)), 0.0))
        rhs_u.append(vh * beta)
        rhs_w.append(kb * ed)
        qds.append(qh * ed)
        kds.append(kh * jnp.exp(dlast - dcol))
        gls.append(jnp.exp(dlast))

    n = len(chains)
    eye_c = _eye(chunk, F32)
    kkts = [_mm(kbs[i], ks[i], _NT, hi=hi) for i in range(n)]
    qks = [_mm(qs[i], ks[i], _NT, hi=hi) for i in range(n)]
    powers = [jnp.where(rr > cc, kkts[i] * dmasks[i], 0.0) for i in range(n)]
    qks = [qks[i] * dmasks[i] for i in range(n)]
    invs = [eye_c - powers[i] for i in range(n)]
    span = 2
    while span < n_valid:
        powers = [_mm(powers[i], powers[i], hi=hi) for i in range(n)]
        invs = [invs[i] + _mm(invs[i], powers[i], hi=hi) for i in range(n)]
        span *= 2
    us = [_mm(invs[i], rhs_u[i], hi=hi) for i in range(n)]
    ws = [_mm(invs[i], rhs_w[i], hi=hi) for i in range(n)]
    sts = [st_ref[b, h] for b, h in chains]
    v_news = [us[i] - _mm(ws[i], sts[i], hi=hi) for i in range(n)]
    os_ = [_mm(qds[i], sts[i], hi=hi) for i in range(n)]
    os_ = [os_[i] + _mm(qks[i], v_news[i], hi=hi) for i in range(n)]
    upd = [_mm(kds[i], v_news[i], _TN, hi=hi) for i in range(n)]
    for i, (b, h) in enumerate(chains):
        st_ref[b, h] = sts[i] * gls[i] + upd[i]
        o = os_[i]
        o = o * lax.rsqrt(jnp.mean(o * o, axis=-1, keepdims=True) + NORM_EPS) * wg_ref[...]
        o = o * _silu(zg_ref[b, :, h * GDN_DV:(h + 1) * GDN_DV])
        go_ref[b, :, h * GDN_DV:(h + 1) * GDN_DV] = o.astype(go_ref.dtype)


def _gdn(qkv, sm, zg, conv0, s0, w_conv, a_log, dt_bias, w_gnorm, batch, rows, bblk, chunk, n_valid, hi,
         out_dtype):
    nc = rows // chunk
    tile = lambda b, c: (b, c, 0)
    per_b3 = lambda b, c: (b, 0, 0)
    per_b4 = lambda b, c: (b, 0, 0, 0)
    fix2 = lambda b, c: (0, 0)
    wc = jnp.concatenate([w_conv, jnp.zeros((SUBLANES - GDN_CONV, GDN_CONV_CH), w_conv.dtype)], axis=0)
    vec = jnp.zeros((SUBLANES, LANES), F32)
    vec = vec.at[0, SM_A:SM_A + GDN_HEADS].set(a_log).at[1, SM_A:SM_A + GDN_HEADS].set(dt_bias)
    go, xp, st = pl.pallas_call(
        functools.partial(_gdn_kernel, bblk=bblk, chunk=chunk, n_valid=n_valid, hi=hi),
        grid=(batch // bblk, nc),
        in_specs=[pl.BlockSpec((bblk, chunk, GDN_CONV_CH), tile),
                  pl.BlockSpec((bblk, chunk, LANES), tile),
                  pl.BlockSpec((bblk, chunk, GDN_WIDTH), tile),
                  pl.BlockSpec((bblk, SUBLANES, GDN_CONV_CH), per_b3),
                  pl.BlockSpec((bblk, GDN_HEADS, GDN_DK, GDN_DV), per_b4),
                  pl.BlockSpec((SUBLANES, GDN_CONV_CH), fix2),
                  pl.BlockSpec((SUBLANES, LANES), fix2),
                  pl.BlockSpec((1, GDN_DV), fix2)],
        out_specs=[pl.BlockSpec((bblk, chunk, GDN_WIDTH), tile),
                   pl.BlockSpec((bblk, SUBLANES + chunk, GDN_CONV_CH), per_b3),
                   pl.BlockSpec((bblk, GDN_HEADS, GDN_DK, GDN_DV), per_b4)],
        out_shape=[jax.ShapeDtypeStruct((batch, rows, GDN_WIDTH), out_dtype),
                   jax.ShapeDtypeStruct((batch, SUBLANES + chunk, GDN_CONV_CH), F32),
                   jax.ShapeDtypeStruct((batch, GDN_HEADS, GDN_DK, GDN_DV), F32)],
        scratch_shapes=[pltpu.VMEM((bblk, SUBLANES + chunk, GDN_CONV_CH), F32)],
        compiler_params=pltpu.CompilerParams(dimension_semantics=("arbitrary", "arbitrary"),
                                             vmem_limit_bytes=VMEM_LIMIT_BYTES),
        name="gdn",
    )(qkv.reshape(batch, rows, GDN_CONV_CH), sm.reshape(batch, rows, LANES), zg.reshape(batch, rows, GDN_WIDTH),
      conv0, s0, wc, vec, w_gnorm.reshape(1, GDN_DV))
    return go.reshape(batch * rows, GDN_WIDTH), xp, st


PROMPT_TM = 512
SAMPLE_TM = 256
GDN_PROMPT_BBLK = 8
GDN_SAMPLE_BBLK = 8
NSA_SAMPLE_NB = 2


def _layer_prompt(h, lw, final, w_final, win_buf):
    w_norm, w_pack, cmp_k_w, cmp_v_w, w_conv, a_log, dt_bias, w_gnorm, w_out = lw
    batch, seq, d = h.shape
    x2d = h.reshape(batch * seq, d)
    tables = _rope_tables(jnp.arange(seq, dtype=jnp.int32))
    (q, qr, ck, cv, sk, sv, wk, wv, kvb, zn, qkv, zg, sm, ck_t, cv_t, sk_t, sv_t, wk_t, wv_t) = _project(
        x2d, w_norm, w_pack.astype(BF16), tables, seq, PROMPT_TM, False, BF16, True)
    nsa = _nsa_prompt(q, qr, kvb, ck, cv, sm, zn, cmp_k_w, cmp_v_w, batch, seq)
    conv0 = jnp.zeros((batch, SUBLANES, GDN_CONV_CH), F32)
    s0 = jnp.zeros((batch, GDN_HEADS, GDN_DK, GDN_DV), F32)
    go, xp, st = _gdn(qkv, sm, zg, conv0, s0, w_conv, a_log, dt_bias, w_gnorm,
                      batch, seq, GDN_PROMPT_BBLK, GDN_CHUNK, GDN_CHUNK, False, BF16)
    y = _out_project(x2d, nsa, go, w_out.astype(BF16), w_final, PROMPT_TM, final)
    conv_new = xp[:, SUBLANES + GDN_CHUNK - (GDN_CONV - 1):SUBLANES + GDN_CHUNK]
    from_t = lambda t: jnp.transpose(t.reshape(batch, KV_HEADS, HEAD_DIM, t.shape[-1]), (0, 3, 1, 2))
    lead = ((0, 0), (0, 0), (max(win_buf - seq, 0), 0))
    win = lambda t: from_t(jnp.pad(t, lead)[:, :, -win_buf:])
    return y.reshape(batch, seq, d), (from_t(ck_t), from_t(cv_t), from_t(sk_t), from_t(sv_t), win(wk_t), win(wv_t),
                                      conv_new, st)


def _layer_sample(h8, n_new, caches, page_table, lw, final, w_final):
    w_norm, w_pack, cmp_k_w, cmp_v_w, w_conv, a_log, dt_bias, w_gnorm, w_out = lw
    pe_rows = lambda pe2: jnp.pad(pe2, ((0, LANES - pe2.shape[0]), (0, 0))).astype(BF16)
    cmp_k_t = (cmp_k_w[0].T, cmp_k_w[1].T, pe_rows(cmp_k_w[2]))
    cmp_v_t = (cmp_v_w[0].T, cmp_v_w[1].T, pe_rows(cmp_v_w[2]))
    c_cmp_k, c_cmp_v, c_slc_k, c_slc_v, c_win_k, c_win_v, s_conv, s_gdn = caches
    batch, nr, d = h8.shape
    past = page_table.shape[1] * c_cmp_k.shape[1]
    x2d = h8.reshape(batch * nr, d)
    tables = _rope_tables(past + jnp.arange(nr, dtype=jnp.int32))
    tables = tuple(jnp.tile(t, (SAMPLE_TM // nr, 1)) for t in tables)
    (q, qr, ck, cv, sk, sv, wk, wv, _, zn, qkv, zg, sm) = _project(
        x2d, w_norm, w_pack, tables, SAMPLE_TM, SAMPLE_TM, True, F32, False)
    nsa, win_k_t, win_v_t = _nsa_sample_t(q, qr, sk, sv, wk, wv, sm, zn, c_win_k, c_win_v, c_cmp_k, c_cmp_v,
                                          c_slc_k, c_slc_v, page_table, cmp_k_t, cmp_v_t, n_new, NSA_SAMPLE_NB)
    conv0 = jnp.pad(s_conv, ((0, 0), (SUBLANES - (GDN_CONV - 1), 0), (0, 0)))
    go, xp, st = _gdn(qkv, sm, zg, conv0, s_gdn, w_conv, a_log, dt_bias, w_gnorm,
                      batch, nr, GDN_SAMPLE_BBLK, nr, n_new, True, F32)
    y = _out_project(x2d, nsa, go, w_out.astype(BF16), w_final, SAMPLE_TM, final)
    kv4 = lambda t: t.reshape(batch, nr, KV_HEADS, HEAD_DIM)[:, :n_new]
    from_t = lambda t: jnp.transpose(t.reshape(batch, KV_HEADS, HEAD_DIM, t.shape[-1]), (0, 3, 1, 2))
    conv_new = xp[:, SUBLANES + n_new - (GDN_CONV - 1):SUBLANES + n_new]
    return y.reshape(batch, nr, d), (kv4(ck), kv4(cv), kv4(sk), kv4(sv), from_t(win_k_t), from_t(win_v_t),
                                     conv_new, st)


def kernel(x_prompt, x_sample, cache_cmp_k, cache_cmp_v, cache_slc_k, cache_slc_v, cache_win_k, cache_win_v, state_conv, state_gdn, page_table, w_norm, w_in, pe_cmp_k, w_cmp_k1, w_cmp_k2, pe_cmp_v, w_cmp_v1, w_cmp_v2, w_conv, a_log, dt_bias, w_gdn_norm, w_out, w_final_norm):
    depth = w_in.shape[0]
    n_new = x_sample.shape[1]
    win_buf = cache_win_k.shape[2]
    h_p = x_prompt
    h_s = jnp.pad(x_sample, ((0, 0), (0, SAMPLE_ROWS - n_new), (0, 0)))
    st_p, st_s = [], []
    for layer in range(depth):
        lw = (w_norm[layer], _pack_w_in(w_in[layer]),
              _compress_weights(pe_cmp_k[layer], w_cmp_k1[layer], w_cmp_k2[layer]),
              _compress_weights(pe_cmp_v[layer], w_cmp_v1[layer], w_cmp_v2[layer]),
              w_conv[layer], a_log[layer], dt_bias[layer], w_gdn_norm[layer], w_out[layer])
        final = layer == depth - 1
        h_p, sp = _layer_prompt(h_p, lw, final, w_final_norm, win_buf)
        caches = (cache_cmp_k[layer], cache_cmp_v[layer], cache_slc_k[layer], cache_slc_v[layer],
                  cache_win_k[layer], cache_win_v[layer], state_conv[layer], state_gdn[layer])
        h_s, ss = _layer_sample(h_s, n_new, caches, page_table, lw, final, w_final_norm)
        st_p.append(sp)
        st_s.append(ss)
    outs = [h_p, h_s[:, :n_new]]
    for i in range(8):
        outs.append(jnp.stack([s[i] for s in st_p]))
        outs.append(jnp.stack([s[i] for s in st_s]))
    return tuple(outs)
```

```python
import functools

import numpy as np
import jax
import jax.numpy as jnp
from jax import lax
from jax.experimental import pallas as pl
from jax.experimental.pallas import tpu as pltpu

F32 = jnp.float32
BF16 = jnp.bfloat16

D_MODEL = 1024
HEAD_DIM = 64
Q_HEADS = 8
KV_HEADS = 2
GROUP = Q_HEADS // KV_HEADS
NSA_WIDTH = Q_HEADS * HEAD_DIM
KV_WIDTH = KV_HEADS * HEAD_DIM
CMP_LEN = 32
CMP_STRIDE = 16
CMP_HIDDEN = 128
SEL_BLOCK = 64
TOP_N = 8
WINDOW = 512
FORCE_BONUS = 1.0e4
ROT_DIM = HEAD_DIM // 4
ROPE_THETA = 500000.0
GDN_DK = 128
GDN_DV = 128
GDN_HEADS = 4
GDN_WIDTH = GDN_HEADS * GDN_DV
GDN_CONV = 4
GDN_CONV_CH = 3 * GDN_WIDTH
GDN_CHUNK = 64
NORM_EPS = 1e-6
MASK_VALUE = -1e30

LANES = 128
SUBLANES = 8
VMEM_LIMIT_BYTES = 56 * 1024 * 1024

QPAD_WIDTH = Q_HEADS * LANES
C_Q = 0
C_KV = C_Q + NSA_WIDTH
C_ZN = C_KV + 6 * KV_WIDTH
C_QKV = C_ZN + NSA_WIDTH
C_ZG = C_QKV + GDN_CONV_CH
C_SM = C_ZG + GDN_WIDTH
N_PACK = C_SM + LANES
SM_B = 3 * Q_HEADS
SM_A = SM_B + GDN_HEADS

TQ = 128
CK_SLC = 256
N_SUB = 128
N_CMP = N_SUB - CMP_LEN // CMP_STRIDE + 1
SAMPLE_ROWS = 8
KVB_SK = 0
KVB_SV = KVB_SK + KV_WIDTH
KVB_WK = KVB_SV + 2 * KV_WIDTH
KVB_WV = KVB_WK + KV_WIDTH
KVB_WIDTH = KVB_WV + 2 * KV_WIDTH
KVB_COLS = (KVB_SK, KVB_SV, KVB_WK, KVB_WV)
WIN_SPAN = WINDOW + TQ


def _pack_w_in(w_in):
    o_gate = NSA_WIDTH + 6 * KV_WIDTH
    o_zn = o_gate + 3 * Q_HEADS
    o_qkv = o_zn + NSA_WIDTH
    o_b = o_qkv + GDN_CONV_CH
    o_a = o_b + GDN_HEADS
    o_zg = o_a + GDN_HEADS
    wt = w_in.T
    pad = jnp.zeros((LANES - SM_A - GDN_HEADS, wt.shape[1]), wt.dtype)
    return jnp.concatenate(
        [wt[:o_gate], wt[o_zn:o_qkv], wt[o_qkv:o_b], wt[o_zg:], wt[o_gate:o_zn], wt[o_b:o_a], wt[o_a:o_zg], pad],
        axis=0)


def _rope_tables(pos):
    half = ROT_DIM // 2
    inv = ROPE_THETA ** (-(jnp.arange(half, dtype=F32) * 2.0 / ROT_DIM))
    ang = pos.astype(F32)[:, None] * inv[None, :]
    cos, sin = jnp.cos(ang), jnp.sin(ang)
    n = pos.shape[0]
    one = jnp.ones((n, HEAD_DIM - ROT_DIM), F32)
    zero = jnp.zeros((n, HEAD_DIM - ROT_DIM), F32)
    zh = jnp.zeros((n, half), F32)
    c64 = jnp.concatenate([cos, cos, one], axis=1)
    a64 = jnp.concatenate([zh, sin, zero], axis=1)
    b64 = jnp.concatenate([-sin, zh, zero], axis=1)
    tile = lambda t: jnp.concatenate([t, t], axis=1)
    return tile(c64), tile(a64), tile(b64)


def _rope128(x, c, a, b):
    half = ROT_DIM // 2
    return x * c + pltpu.roll(x, half, 1) * a + pltpu.roll(x, LANES - half, 1) * b


_NN = (((1,), (0,)), ((), ()))
_NT = (((1,), (1,)), ((), ()))
_TN = (((0,), (0,)), ((), ()))


def _split_bf16(x):
    hi = x.astype(BF16)
    return hi, (x - hi.astype(F32)).astype(BF16)


def _mm(a, b, dims=_NN, hi=False):
    dot = lambda x, y: lax.dot_general(x, y, dims, preferred_element_type=F32)
    if hi:
        a_hi, a_lo = _split_bf16(a.astype(F32))
        b_hi, b_lo = _split_bf16(b.astype(F32))
        return dot(a_hi, b_hi) + (dot(a_lo, b_hi) + dot(a_hi, b_lo))
    return dot(a.astype(BF16), b.astype(BF16))


def _mm_exact_lhs(a_bf16, x, dims=_NN):
    x1 = x.astype(BF16)
    r1 = x - x1.astype(F32)
    x2 = r1.astype(BF16)
    x3 = (r1 - x2.astype(F32)).astype(BF16)
    dot = lambda t: lax.dot_general(a_bf16, t, dims, preferred_element_type=F32)
    return dot(x1) + dot(x2) + dot(x3)


def _sigmoid(x):
    return 1.0 / (1.0 + jnp.exp(-x))


def _silu(x):
    return x * _sigmoid(x)


def _tile_rows(x, n):
    return jnp.concatenate([x] * n, axis=0)


def _tile_lanes(x, n):
    return x if n == 1 else jnp.concatenate([x] * n, axis=1)


def _eye(n, dtype):
    r = lax.broadcasted_iota(jnp.int32, (n, n), 0)
    c = lax.broadcasted_iota(jnp.int32, (n, n), 1)
    return (r == c).astype(dtype)


def _proj_kernel(x_ref, wn_ref, w_ref, c_ref, a_ref, b_ref,
                 q_ref, qr_ref, ck_ref, cv_ref, sk_ref, sv_ref, wk_ref, wv_ref, kvb_ref,
                 zn_ref, qkv_ref, zg_ref, sm_ref, *t_refs, hi_gdn):
    x = x_ref[...]
    ms = jnp.mean(x * x, axis=-1, keepdims=True)
    xn = x * lax.rsqrt(ms + NORM_EPS) * wn_ref[...]
    xb = xn.astype(BF16)
    c, a, b = c_ref[...], a_ref[...], b_ref[...]
    scale = HEAD_DIM ** -0.5
    lane = lax.broadcasted_iota(jnp.int32, (x.shape[0], LANES), 1)
    half_mask = (lane < HEAD_DIM, lane >= HEAD_DIM)
    for j in range(NSA_WIDTH // (2 * LANES)):
        q2 = _mm(xb, w_ref[C_Q + 2 * j * LANES:C_Q + 2 * (j + 1) * LANES, :], _NT)
        for jj in range(2):
            raw = q2[:, jj * LANES:(jj + 1) * LANES]
            for src, dst in ((raw, q_ref), (_rope128(raw, c, a, b), qr_ref)):
                swapped = pltpu.roll(src, HEAD_DIM, 1)
                for par in range(2):
                    head = 2 * (2 * j + jj) + par
                    kvh = head // GROUP
                    val = jnp.where(half_mask[kvh], src if par == kvh else swapped, 0.0)
                    dst[:, head * LANES:(head + 1) * LANES] = (val * scale).astype(dst.dtype)
    kv_refs = (ck_ref, cv_ref, sk_ref, sv_ref, wk_ref, wv_ref)
    kv2 = [_mm(xb, w_ref[C_KV + j * LANES:C_KV + (j + 2) * LANES, :], _NT) for j in range(0, 6, 2)]
    for j in range(6):
        kj = kv2[j // 2][:, (j % 2) * LANES:(j % 2 + 1) * LANES]
        if j in (2, 4):
            kj = _rope128(kj, c, a, b)
        kv_refs[j][...] = kj
        if j >= 2:
            lo = KVB_COLS[j - 2]
            kvb_ref[:, lo:lo + LANES] = kj.astype(BF16)
        if t_refs:
            t_refs[j][0] = kj.T
    ones = jnp.ones((x.shape[0], LANES), BF16)
    kvb_ref[:, KVB_SV + LANES:KVB_SV + 2 * LANES] = ones
    kvb_ref[:, KVB_WV + LANES:KVB_WV + 2 * LANES] = ones
    zn_ref[...] = _mm(xb, w_ref[C_ZN:C_QKV, :], _NT)
    xg = xn if hi_gdn else xb
    for j in range(3):
        lo = C_QKV + j * GDN_WIDTH
        qkv_ref[:, j * GDN_WIDTH:(j + 1) * GDN_WIDTH] = _mm(xg, w_ref[lo:lo + GDN_WIDTH, :], _NT, hi=hi_gdn)
    zg_ref[...] = _mm(xb, w_ref[C_ZG:C_SM, :], _NT)
    sm_ref[...] = _mm(xg, w_ref[C_SM:N_PACK, :], _NT, hi=hi_gdn)


def _project(x2d, w_norm, w_pack, tables, rows_per_seq, tm, hi_gdn, q_dtype, emit_t):
    n = x2d.shape[0]
    nt = rows_per_seq // tm
    row = lambda i: (i, 0)
    tab = lambda i: (i % nt, 0)
    fix = lambda i: (0, 0)
    widths = (QPAD_WIDTH, QPAD_WIDTH) + (KV_WIDTH,) * 6 + (KVB_WIDTH, NSA_WIDTH, GDN_CONV_CH, GDN_WIDTH, LANES)
    dtypes = (q_dtype, q_dtype) + (F32,) * 6 + (BF16, F32, F32, F32, F32)
    out_specs = [pl.BlockSpec((tm, w), row) for w in widths]
    out_shape = [jax.ShapeDtypeStruct((n, w), d) for w, d in zip(widths, dtypes)]
    if emit_t:
        out_specs += [pl.BlockSpec((1, KV_WIDTH, tm), lambda i: (i // nt, 0, i % nt))] * 6
        out_shape += [jax.ShapeDtypeStruct((n // rows_per_seq, KV_WIDTH, rows_per_seq), F32)] * 6
    return pl.pallas_call(
        functools.partial(_proj_kernel, hi_gdn=hi_gdn),
        grid=(n // tm,),
        in_specs=[pl.BlockSpec((tm, D_MODEL), row),
                  pl.BlockSpec((1, D_MODEL), fix),
                  pl.BlockSpec((N_PACK, D_MODEL), fix),
                  pl.BlockSpec((tm, LANES), tab),
                  pl.BlockSpec((tm, LANES), tab),
                  pl.BlockSpec((tm, LANES), tab)],
        out_specs=out_specs,
        out_shape=out_shape,
        compiler_params=pltpu.CompilerParams(dimension_semantics=("arbitrary",),
                                             vmem_limit_bytes=VMEM_LIMIT_BYTES),
        name="in_proj",
    )(x2d, w_norm.reshape(1, D_MODEL), w_pack, *tables)


def _out_kernel(x_ref, nsa_ref, gdn_ref, w_ref, wf_ref, y_ref, *, final):
    mix = jnp.concatenate([nsa_ref[...].astype(BF16), gdn_ref[...].astype(BF16)], axis=1)
    h = x_ref[...] + _mm(mix, w_ref[...])
    if final:
        ms = jnp.mean(h * h, axis=-1, keepdims=True)
        h = h * lax.rsqrt(ms + NORM_EPS) * wf_ref[...]
    y_ref[...] = h


def _out_project(x2d, nsa, gdn, w_out, w_final, tm, final):
    n = x2d.shape[0]
    row = lambda i: (i, 0)
    fix = lambda i: (0, 0)
    return pl.pallas_call(
        functools.partial(_out_kernel, final=final),
        grid=(n // tm,),
        in_specs=[pl.BlockSpec((tm, D_MODEL), row),
                  pl.BlockSpec((tm, NSA_WIDTH), row),
                  pl.BlockSpec((tm, GDN_WIDTH), row),
                  pl.BlockSpec((D_MODEL, D_MODEL), fix),
                  pl.BlockSpec((1, D_MODEL), fix)],
        out_specs=pl.BlockSpec((tm, D_MODEL), row),
        out_shape=jax.ShapeDtypeStruct((n, D_MODEL), F32),
        compiler_params=pltpu.CompilerParams(dimension_semantics=("arbitrary",),
                                             vmem_limit_bytes=VMEM_LIMIT_BYTES),
        name="out_proj",
    )(x2d, nsa, gdn, w_out, w_final.reshape(1, D_MODEL))


def _compress_weights(pe, w1, w2):
    half = CMP_STRIDE * HEAD_DIM
    z = jnp.zeros((CMP_STRIDE, HEAD_DIM, CMP_HIDDEN), w1.dtype)

    def place(wpart, h):
        wp = wpart.reshape(CMP_STRIDE, HEAD_DIM, CMP_HIDDEN)
        parts = [wp, z] if h == 0 else [z, wp]
        return jnp.stack(parts, axis=1).reshape(CMP_STRIDE * KV_WIDTH, CMP_HIDDEN)

    w1big = jnp.concatenate([place(w1[:half], 0), place(w1[:half], 1),
                             place(w1[half:], 0), place(w1[half:], 1)], axis=1)
    zz = jnp.zeros_like(w2)
    w2big = jnp.concatenate([jnp.concatenate([w2, zz], axis=1), jnp.concatenate([zz, w2], axis=1)], axis=0)
    pe_a = jnp.tile(pe[:CMP_STRIDE], (1, KV_HEADS)).reshape(1, CMP_STRIDE * KV_WIDTH)
    pe_b = jnp.tile(pe[CMP_STRIDE:], (1, KV_HEADS)).reshape(1, CMP_STRIDE * KV_WIDTH)
    pe2 = jnp.concatenate([pe_a, pe_b, jnp.zeros((SUBLANES - 2, CMP_STRIDE * KV_WIDTH), pe.dtype)], axis=0)
    return w1big.astype(BF16), w2big.astype(BF16), pe2


def _compress(sub_rows, w1_ref, w2_ref, pe_ref):
    hid = 2 * CMP_HIDDEN
    w1 = w1_ref[...]
    ab = _mm(sub_rows, w1)
    pe = pe_ref[...]
    pe_hi = pe.astype(BF16)
    pe_lo = (pe - pe_hi.astype(F32)).astype(BF16)
    r = _mm(pe_hi, w1) + _mm(pe_lo, w1)
    bias = r[0:1, :hid] + r[1:2, hid:]
    h = ab[:, :hid] + pltpu.roll(ab[:, hid:], N_SUB - 1, 0) + bias
    return _mm(_silu(h), w2_ref[...])


def _overlap_t(ns):
    c0 = np.arange(N_CMP)[None, :] * CMP_STRIDE
    b0 = np.arange(ns)[:, None] * SEL_BLOCK
    ov = np.minimum(c0 + CMP_LEN, b0 + SEL_BLOCK) - np.maximum(c0, b0)
    out = np.zeros((LANES, LANES), np.float32)
    out[:ns, :N_CMP] = np.maximum(ov, 0) / CMP_LEN
    return jnp.asarray(out, BF16)


def _select_blocks(imp_t, n_idx, q_blk, n_rows):
    forced = (n_idx == 0) | (n_idx == q_blk) | (n_idx == q_blk - 1)
    allowed = n_idx <= q_blk
    v = jnp.where(allowed, imp_t + FORCE_BONUS * forced.astype(F32), MASK_VALUE)
    rank = jnp.zeros(v.shape, F32)
    for j in range(n_rows):
        vj = v[j:j + 1, :]
        ge = jnp.where(vj >= v, 1.0, 0.0)
        gt = jnp.where(vj > v, 1.0, 0.0)
        rank = rank + jnp.where(n_idx > j, ge, gt)
    return ((rank < TOP_N) & allowed).astype(F32)


def _merge_heads(o_sum, kvh, tq):
    lane = lax.broadcasted_iota(jnp.int32, (tq, LANES), 1)
    slabs = []
    for pair in range(GROUP // 2):
        halves = []
        for par in range(2):
            o = o_sum[(2 * pair + par) * tq:(2 * pair + par + 1) * tq]
            halves.append(o if par == kvh else pltpu.roll(o, HEAD_DIM, 1))
        slabs.append(jnp.where(lane < HEAD_DIM, halves[0], halves[1]))
    return slabs


def _nsa_prompt_kernel(q_ref, qr_ref, kvb_ref, ckr_ref, cvr_ref, sm_ref, zn_ref,
                       wk1_ref, wk2_ref, pek_ref, wv1_ref, wv2_ref, pev_ref, ovt_ref, e_ref,
                       o_ref, ck_s, cv_s, s_s, mrun_s, m_s, acc_s):
    i = pl.program_id(1)

    @pl.when(i == 0)
    def _():
        ck_s[...] = _compress(ckr_ref[0], wk1_ref, wk2_ref, pek_ref).astype(BF16)
        cv_s[...] = _compress(cvr_ref[0], wv1_ref, wv2_ref, pev_ref).astype(BF16)

    kvs = range(KV_HEADS)
    t0 = i * TQ
    rows = GROUP * TQ
    row = lax.broadcasted_iota(jnp.int32, (TQ, LANES), 0)
    col = lax.broadcasted_iota(jnp.int32, (TQ, LANES), 1)
    cvalid = (CMP_STRIDE * col + (CMP_LEN - 1) <= t0 + row) & (col < N_CMP)
    cm = _tile_rows(cvalid, GROUP)
    cm_f = cm.astype(F32)
    sig = _sigmoid(sm_ref[...])

    def gate(branch, kvh):
        cols = [sig[:, branch * Q_HEADS + h:branch * Q_HEADS + h + 1] for h in range(kvh * GROUP, (kvh + 1) * GROUP)]
        return jnp.concatenate([jnp.broadcast_to(c, (TQ, LANES)) for c in cols], axis=0)

    def stack(ref, kvh):
        return jnp.concatenate([ref[:, h * LANES:(h + 1) * LANES] for h in range(kvh * GROUP, (kvh + 1) * GROUP)], axis=0)

    q_raw = [stack(q_ref, k) for k in kvs]
    q_rot = [stack(qr_ref, k) for k in kvs]

    ck, cv = ck_s[...], cv_s[...]
    s = [jnp.where(cm, _mm(q_raw[k], ck, _NT), MASK_VALUE) for k in kvs]
    e = [jnp.exp(s[k] - jnp.max(s[k], axis=1, keepdims=True)) for k in kvs]
    p = [e[k] / jnp.sum(e[k], axis=1, keepdims=True) * cm_f for k in kvs]
    o_sum = [gate(0, k) * _mm(p[k], cv) for k in kvs]
    p_sum = jnp.concatenate([p[k][0:TQ] + p[k][TQ:2 * TQ] + p[k][2 * TQ:3 * TQ] + p[k][3 * TQ:4 * TQ] for k in kvs],
                            axis=0)
    ns_rows = 32
    nq = KV_HEADS * TQ
    imp_t = _mm_exact_lhs(ovt_ref[...], p_sum, _NT)[0:ns_rows]
    n_idx = lax.broadcasted_iota(jnp.int32, (ns_rows, nq), 0)
    qb_t = (t0 + lax.broadcasted_iota(jnp.int32, (ns_rows, nq), 1) % TQ) // SEL_BLOCK
    sel_t = _select_blocks(imp_t, n_idx, qb_t, ns_rows)
    sel_t = jnp.concatenate([sel_t, jnp.zeros((LANES - ns_rows, nq), F32)], axis=0).astype(BF16)
    eye = _eye(TQ, BF16)
    sel = [_mm(eye, sel_t[:, k * TQ:(k + 1) * TQ], _NT).astype(BF16) for k in kvs]

    col_s = lax.broadcasted_iota(jnp.int32, (TQ, CK_SLC), 1)
    row_s = lax.broadcasted_iota(jnp.int32, (TQ, CK_SLC), 0)
    n_chunks = (t0 + TQ + CK_SLC - 1) // CK_SLC
    mrun_s[...] = jnp.full(mrun_s.shape, MASK_VALUE, F32)

    def score_chunks(js):
        k0 = [pl.multiple_of(j * CK_SLC, CK_SLC) for j in js]
        k_c = [kvb_ref[pl.ds(k0[n], CK_SLC), KVB_SK:KVB_SK + LANES] for n in range(len(js))]
        sc = [[_mm(q_rot[k], k_c[n], _NT) for k in kvs] for n in range(len(js))]
        picked = [[_mm(sel[k], e_ref[js[n]]) for k in kvs] for n in range(len(js))]
        for k in kvs:
            run = mrun_s[k]
            for n, j in enumerate(js):
                causal = k0[n] + col_s <= t0 + row_s
                bias = jnp.where((picked[n][k] > 0.5) & causal, 0.0, MASK_VALUE)
                sk = sc[n][k] + _tile_rows(bias, GROUP)
                s_s[k, j] = sk
                run = jnp.maximum(run, jnp.maximum(sk[:, :LANES], sk[:, LANES:]))
            mrun_s[k] = run

    def value_chunks(js):
        k0 = [pl.multiple_of(j * CK_SLC, CK_SLC) for j in js]
        v_c = [kvb_ref[pl.ds(k0[n], CK_SLC), KVB_SV:KVB_SV + 2 * LANES] for n in range(len(js))]
        pr = [[jnp.exp(s_s[k, js[n]] - _tile_lanes(m_s[k], CK_SLC // LANES)) for k in kvs] for n in range(len(js))]
        pv = [[_mm(pr[n][k], v_c[n]) for k in kvs] for n in range(len(js))]
        for k in kvs:
            acc_s[k] += sum(pv[n][k] for n in range(len(js)))

    def paired(fn):
        def pair_body(jj, carry):
            fn([2 * jj, 2 * jj + 1])
            return carry

        lax.fori_loop(0, n_chunks // 2, pair_body, 0)

        @pl.when(n_chunks % 2 == 1)
        def _():
            fn([n_chunks - 1])

    paired(score_chunks)
    for k in kvs:
        m_s[k] = jnp.broadcast_to(jnp.max(mrun_s[k], axis=1, keepdims=True), (rows, LANES))
    acc_s[...] = jnp.zeros(acc_s.shape, F32)

    paired(value_chunks)
    for k in kvs:
        acc = acc_s[k]
        o_sum[k] = o_sum[k] + gate(1, k) * (acc[:, :LANES] / acc[:, LANES:])

    ks0 = pl.multiple_of(jnp.maximum(t0 - WINDOW, 0), TQ)
    k_w = kvb_ref[pl.ds(ks0, WIN_SPAN), KVB_WK:KVB_WK + LANES]
    v_w = kvb_ref[pl.ds(ks0, WIN_SPAN), KVB_WV:KVB_WV + 2 * LANES]
    diff = (t0 + lax.broadcasted_iota(jnp.int32, (TQ, WIN_SPAN), 0)) - (ks0 + lax.broadcasted_iota(jnp.int32, (TQ, WIN_SPAN), 1))
    bias_w = _tile_rows(jnp.where((diff >= 0) & (diff <= WINDOW), 0.0, MASK_VALUE), GROUP)
    sw = [_mm(q_rot[k], k_w, _NT) + bias_w for k in kvs]
    pw = [jnp.exp(sw[k] - jnp.max(sw[k], axis=1, keepdims=True)) for k in kvs]
    rw = [_mm(pw[k], v_w) for k in kvs]
    for k in kvs:
        o_all = o_sum[k] + gate(2, k) * (rw[k][:, :LANES] / rw[k][:, LANES:])
        for pair, slab in enumerate(_merge_heads(o_all, k, TQ)):
            lo = (k * (GROUP // 2) + pair) * LANES
            o_ref[:, lo:lo + LANES] = (slab * _silu(zn_ref[:, lo:lo + LANES])).astype(o_ref.dtype)


def _nsa_prompt(q, qr, kvb, ck, cv, sm, zn, cmp_k_w, cmp_v_w, batch, seq):
    nt = seq // TQ
    ns = seq // SEL_BLOCK
    sub_w = CMP_STRIDE * KV_WIDTH
    keys = np.arange(seq)
    e = (np.arange(LANES)[None, :, None] == (keys // SEL_BLOCK).reshape(seq // CK_SLC, 1, CK_SLC))
    e = jnp.asarray(e, BF16)
    tile = lambda b, i: (b * nt + i, 0)
    per_b = lambda b, i: (b, 0)
    per_b3 = lambda b, i: (b, 0, 0)
    fix2 = lambda b, i: (0, 0)
    fix3 = lambda b, i: (0, 0, 0)
    wspecs = [pl.BlockSpec((sub_w, 4 * CMP_HIDDEN), fix2),
              pl.BlockSpec((2 * CMP_HIDDEN, KV_WIDTH), fix2),
              pl.BlockSpec((SUBLANES, sub_w), fix2)]
    rows = GROUP * TQ
    return pl.pallas_call(
        _nsa_prompt_kernel,
        grid=(batch, nt),
        in_specs=[pl.BlockSpec((TQ, QPAD_WIDTH), tile),
                  pl.BlockSpec((TQ, QPAD_WIDTH), tile),
                  pl.BlockSpec((seq, KVB_WIDTH), per_b),
                  pl.BlockSpec((1, seq // CMP_STRIDE, sub_w), per_b3),
                  pl.BlockSpec((1, seq // CMP_STRIDE, sub_w), per_b3),
                  pl.BlockSpec((TQ, LANES), tile),
                  pl.BlockSpec((TQ, NSA_WIDTH), tile)] + wspecs + wspecs + [
                  pl.BlockSpec((LANES, LANES), fix2),
                  pl.BlockSpec((seq // CK_SLC, LANES, CK_SLC), fix3)],
        out_specs=pl.BlockSpec((TQ, NSA_WIDTH), tile),
        out_shape=jax.ShapeDtypeStruct((batch * seq, NSA_WIDTH), BF16),
        scratch_shapes=[pltpu.VMEM((N_SUB, KV_WIDTH), BF16), pltpu.VMEM((N_SUB, KV_WIDTH), BF16),
                        pltpu.VMEM((KV_HEADS, seq // CK_SLC, rows, CK_SLC), F32),
                        pltpu.VMEM((KV_HEADS, rows, LANES), F32), pltpu.VMEM((KV_HEADS, rows, LANES), F32),
                        pltpu.VMEM((KV_HEADS, rows, 2 * LANES), F32)],
        compiler_params=pltpu.CompilerParams(dimension_semantics=("arbitrary", "arbitrary"),
                                             vmem_limit_bytes=VMEM_LIMIT_BYTES),
        name="nsa_prompt",
    )(q, qr, kvb, ck.reshape(batch, seq // CMP_STRIDE, sub_w), cv.reshape(batch, seq // CMP_STRIDE, sub_w),
      sm, zn, *cmp_k_w, *cmp_v_w, _overlap_t(ns), e)


def _nsa_sample_t_kernel(pt_ref, q_ref, qr_ref, skn_ref, svn_ref, wkn_ref, wvn_ref, sm_ref, zn_ref,
                         wkc_ref, wvc_ref, cmpk_hbm, cmpv_hbm, slck_hbm, slcv_hbm,
                         w1k_ref, w2k_ref, pek_ref, w1v_ref, w2v_ref, pev_ref, ovt_ref, e_ref,
                         o_ref, wko_ref, wvo_ref, kt_s, vt_s, kwt_s, vwt_s, pages_s, sems,
                         *, nb, n_pages, page, past, n_new):
    step = pl.program_id(0)
    last = pl.num_programs(0) - 1
    slot = step % 2
    caches = (cmpk_hbm, cmpv_hbm, slck_hbm, slcv_hbm)
    page_ids = [(j, c, p) for j in range(nb) for c in range(len(caches)) for p in range(n_pages)]

    def page_copy(at_step, at_slot, j, c, p):
        src = caches[c].at[pt_ref[at_step * nb + j, p]]
        return pltpu.make_async_copy(src, pages_s.at[at_slot, j, c, p], sems.at[at_slot, c])

    @pl.when(step == 0)
    def _():
        for ids in page_ids:
            page_copy(0, 0, *ids).start()

    nxt = jnp.minimum(step + 1, last)
    for ids in page_ids:
        page_copy(nxt, 1 - slot, *ids).start()
    for ids in page_ids:
        page_copy(step, slot, *ids).wait()

    cmpk, cmpv, slck, slcv = ([pages_s.at[slot, j, c, p] for j in range(nb) for p in range(n_pages)]
                              for c in range(len(caches)))
    nr = SAMPLE_ROWS
    rows = Q_HEADS * nr
    win_buf = wkc_ref.shape[2]
    bs = range(nb)
    hid = 2 * CMP_HIDDEN
    zpad = jnp.zeros((LANES - nr, LANES), F32)

    def new_rows(ref, b):
        return jnp.concatenate([ref[b * nr:(b + 1) * nr, :], zpad], axis=0).astype(BF16)

    def stack(ref, b):
        return jnp.concatenate([ref[b * nr:(b + 1) * nr, h * LANES:(h + 1) * LANES] for h in range(Q_HEADS)],
                               axis=0).astype(BF16)

    for b in bs:
        for p in range(n_pages):
            kt_s[b, :, p * page:(p + 1) * page] = slck[b * n_pages + p][...].astype(BF16)
            vt_s[b, :, p * page:(p + 1) * page] = slcv[b * n_pages + p][...].astype(BF16)
        kwt_s[b] = wkc_ref[b].astype(BF16)
        vwt_s[b] = wvc_ref[b].astype(BF16)
    mi = lax.broadcasted_iota(jnp.int32, (page, page), 0)
    ki = lax.broadcasted_iota(jnp.int32, (page, page), 1)
    per_page = page // CMP_STRIDE
    perm = (ki == CMP_STRIDE * (mi % per_page) + mi // per_page).astype(BF16)

    def transposed_pages(pages, b):
        out = []
        for p in range(0, n_pages, 2):
            pair = jnp.concatenate([pages[b * n_pages + p][...], pages[b * n_pages + p + 1][...]], axis=0)
            both = _mm(perm, pair, _NT)
            out += [both[:, :page], both[:, page:]]
        return out

    rows_t = [[transposed_pages(pages, b) for pages in (cmpk, cmpv)] for b in bs]

    q_raw = [stack(q_ref, b) for b in bs]
    q_rot = [stack(qr_ref, b) for b in bs]

    sw = [jnp.concatenate([_mm(q_rot[b], kwt_s[b]), _mm(q_rot[b], new_rows(wkn_ref, b), _NT)], axis=1) for b in bs]
    ss = [jnp.concatenate([_mm(q_rot[b], kt_s[b]), _mm(q_rot[b], new_rows(skn_ref, b), _NT)], axis=1) for b in bs]

    def sub_blocks(b, c):
        cols = [jnp.concatenate([rows_t[b][c][p][l * per_page:(l + 1) * per_page] for p in range(n_pages)], axis=0)
                for l in range(CMP_STRIDE)]
        return jnp.concatenate(cols, axis=1)

    def activate(t):
        bias = t[0:hid, LANES:LANES + 1] + t[hid:2 * hid, LANES + 1:LANES + 2]
        return _silu(t[0:hid, 0:LANES] + pltpu.roll(t[hid:2 * hid, 0:LANES], N_SUB - 1, 1) + bias)

    ckt, cvt = [], []
    for c, (w1_ref, w2_ref, pe_ref, out) in enumerate(((w1k_ref, w2k_ref, pek_ref, ckt), (w1v_ref, w2v_ref, pev_ref, cvt))):
        pe_rows = pe_ref[...]
        ht = [_mm(w1_ref[...], jnp.concatenate([sub_blocks(b, c).astype(BF16), pe_rows], axis=0), _NT) for b in bs]
        out.extend(_mm(w2_ref[...], activate(ht[b])) for b in bs)

    tok = lax.broadcasted_iota(jnp.int32, (nr, LANES), 0)
    col = lax.broadcasted_iota(jnp.int32, (nr, LANES), 1)
    cm = _tile_rows((CMP_STRIDE * col + (CMP_LEN - 1) <= past + tok) & (col < N_CMP), Q_HEADS)
    cm_f = cm.astype(F32)
    sc = [jnp.where(cm, _mm(q_raw[b], ckt[b]), MASK_VALUE) for b in bs]
    ec = [jnp.exp(sc[b] - jnp.max(sc[b], axis=1, keepdims=True)) for b in bs]
    pc = [ec[b] / jnp.sum(ec[b], axis=1, keepdims=True) * cm_f for b in bs]
    o_cmp = [_mm(pc[b], cvt[b], _NT) for b in bs]
    p_sum = jnp.concatenate(
        [sum(pc[b][(kvh * GROUP + g) * nr:(kvh * GROUP + g + 1) * nr] for g in range(GROUP))
         for b in bs for kvh in range(KV_HEADS)], axis=0)
    ns = (past + n_new + SEL_BLOCK - 1) // SEL_BLOCK
    ns_rows = -(-ns // SUBLANES) * SUBLANES
    nq = nb * KV_HEADS * nr
    imp_t = _mm_exact_lhs(ovt_ref[...], p_sum, _NT)[0:ns_rows]
    n_idx = lax.broadcasted_iota(jnp.int32, (ns_rows, nq), 0)
    qb_t = (past + lax.broadcasted_iota(jnp.int32, (ns_rows, nq), 1) % nr) // SEL_BLOCK
    sel_t = _select_blocks(imp_t, n_idx, qb_t, ns)
    sel_t = jnp.concatenate([sel_t, jnp.zeros((LANES - ns_rows, nq), F32)], axis=0)
    sel = _mm(_eye(nq, BF16), sel_t, _NT)

    all_keys = past + LANES
    q_pos = past + lax.broadcasted_iota(jnp.int32, (rows, all_keys), 0) % nr
    causal = lax.broadcasted_iota(jnp.int32, (rows, all_keys), 1) <= q_pos
    o_slc = []
    for b in bs:
        sel_rows = jnp.concatenate(
            [_tile_rows(sel[(b * KV_HEADS + kvh) * nr:(b * KV_HEADS + kvh + 1) * nr], GROUP) for kvh in range(KV_HEADS)],
            axis=0)
        ok = (_mm(sel_rows, e_ref[...]) > 0.5) & causal
        s = ss[b] + jnp.where(ok, 0.0, MASK_VALUE)
        pr = jnp.exp(s - jnp.max(s, axis=1, keepdims=True))
        o = _mm(pr[:, :past], vt_s[b], _NT) + _mm(pr[:, past:], new_rows(svn_ref, b))
        o_slc.append(o / jnp.sum(pr, axis=1, keepdims=True))

    win_keys = win_buf + LANES
    q_pos = past + lax.broadcasted_iota(jnp.int32, (rows, win_keys), 0) % nr
    diff = q_pos - (past - win_buf + lax.broadcasted_iota(jnp.int32, (rows, win_keys), 1))
    bias_w = jnp.where((diff >= 0) & (diff <= WINDOW), 0.0, MASK_VALUE)
    o_win = []
    for b in bs:
        s = sw[b] + bias_w
        pr = jnp.exp(s - jnp.max(s, axis=1, keepdims=True))
        o = _mm(pr[:, :win_buf], vwt_s[b], _NT) + _mm(pr[:, win_buf:], new_rows(wvn_ref, b))
        o_win.append(o / jnp.sum(pr, axis=1, keepdims=True))

    for b in bs:
        sig = _sigmoid(sm_ref[b * nr:(b + 1) * nr, :])

        def gate(branch):
            cols = [sig[:, branch * Q_HEADS + h:branch * Q_HEADS + h + 1] for h in range(Q_HEADS)]
            return jnp.concatenate([jnp.broadcast_to(c, (nr, LANES)) for c in cols], axis=0)

        o_all = gate(0) * o_cmp[b] + gate(1) * o_slc[b] + gate(2) * o_win[b]
        for kvh in range(KV_HEADS):
            o_kv = o_all[kvh * GROUP * nr:(kvh + 1) * GROUP * nr]
            for pair, slab in enumerate(_merge_heads(o_kv, kvh, nr)):
                lo = (kvh * (GROUP // 2) + pair) * LANES
                o_ref[b * nr:(b + 1) * nr, lo:lo + LANES] = slab * _silu(zn_ref[b * nr:(b + 1) * nr, lo:lo + LANES])

    eye_bf = _eye(LANES, BF16)
    lane = lax.broadcasted_iota(jnp.int32, (KV_WIDTH, LANES), 1)
    for cache_ref, new_ref, out_ref in ((wkc_ref, wkn_ref, wko_ref), (wvc_ref, wvn_ref, wvo_ref)):
        for b in bs:
            shifted = pltpu.roll(cache_ref[b], win_buf - n_new, 1)
            new_pad = jnp.concatenate([new_ref[b * nr:(b + 1) * nr, :], zpad], axis=0)
            new_t = _mm_exact_lhs(eye_bf, new_pad, _NT)
            tail = jnp.where(lane >= LANES - n_new, pltpu.roll(new_t, LANES - n_new, 1), shifted[:, win_buf - LANES:])
            out_ref[b] = jnp.concatenate([shifted[:, :win_buf - LANES], tail], axis=1)

    @pl.when(step == last)
    def _():
        for ids in page_ids:
            page_copy(nxt, 1 - slot, *ids).wait()


def _nsa_sample_t(q, qr, sk, sv, wk, wv, sm, zn, win_k, win_v, cmp_k, cmp_v, slc_k, slc_v, page_table,
                  cmp_k_w, cmp_v_w, n_new, nb):
    batch, n_pages = page_table.shape
    n_pool, page = cmp_k.shape[0], cmp_k.shape[1]
    past = n_pages * page
    nr = SAMPLE_ROWS
    win_buf = win_k.shape[1]
    ns = (past + n_new + SEL_BLOCK - 1) // SEL_BLOCK
    all_keys = past + LANES
    e = jnp.asarray(np.arange(LANES)[:, None] == (np.arange(all_keys) // SEL_BLOCK)[None, :], BF16)
    tview = lambda c: jnp.transpose(c, (0, 2, 3, 1)).reshape(c.shape[0], KV_WIDTH, c.shape[1])
    cmp_k, cmp_v, slc_k, slc_v, win_k, win_v = (tview(c) for c in (cmp_k, cmp_v, slc_k, slc_v, win_k, win_v))

    row = lambda b, pt: (b, 0)
    per_b3 = lambda b, pt: (b, 0, 0)
    fix2 = lambda b, pt: (0, 0)
    sub_w = CMP_STRIDE * KV_WIDTH
    wspecs = [pl.BlockSpec((4 * CMP_HIDDEN, sub_w), fix2),
              pl.BlockSpec((KV_WIDTH, 2 * CMP_HIDDEN), fix2),
              pl.BlockSpec((LANES, sub_w), fix2)]
    in_specs = ([pl.BlockSpec((nb * nr, QPAD_WIDTH), row)] * 2 + [pl.BlockSpec((nb * nr, KV_WIDTH), row)] * 4
                + [pl.BlockSpec((nb * nr, LANES), row), pl.BlockSpec((nb * nr, NSA_WIDTH), row)]
                + [pl.BlockSpec((nb, KV_WIDTH, win_buf), per_b3)] * 2
                + [pl.BlockSpec(memory_space=pl.ANY)] * 4 + wspecs + wspecs
                + [pl.BlockSpec((LANES, LANES), fix2), pl.BlockSpec((LANES, all_keys), fix2)])
    grid_spec = pltpu.PrefetchScalarGridSpec(
        num_scalar_prefetch=1,
        grid=(batch // nb,),
        in_specs=in_specs,
        out_specs=[pl.BlockSpec((nb * nr, NSA_WIDTH), row),
                   pl.BlockSpec((nb, KV_WIDTH, win_buf), per_b3), pl.BlockSpec((nb, KV_WIDTH, win_buf), per_b3)],
        scratch_shapes=[pltpu.VMEM((nb, KV_WIDTH, past), BF16), pltpu.VMEM((nb, KV_WIDTH, past), BF16),
                        pltpu.VMEM((nb, KV_WIDTH, win_buf), BF16), pltpu.VMEM((nb, KV_WIDTH, win_buf), BF16),
                        pltpu.VMEM((2, nb, 4, n_pages, KV_WIDTH, page), F32),
                        pltpu.SemaphoreType.DMA((2, 4))])
    return pl.pallas_call(
        functools.partial(_nsa_sample_t_kernel, nb=nb, n_pages=n_pages, page=page, past=past, n_new=n_new),
        grid_spec=grid_spec,
        out_shape=[jax.ShapeDtypeStruct((batch * nr, NSA_WIDTH), F32),
                   jax.ShapeDtypeStruct((batch, KV_WIDTH, win_buf), F32),
                   jax.ShapeDtypeStruct((batch, KV_WIDTH, win_buf), F32)],
        compiler_params=pltpu.CompilerParams(dimension_semantics=("arbitrary",),
                                             vmem_limit_bytes=VMEM_LIMIT_BYTES),
        name="nsa_sample",
    )(page_table, q, qr, sk, sv, wk, wv, sm, zn, win_k, win_v, cmp_k, cmp_v, slc_k, slc_v,
      *cmp_k_w, *cmp_v_w, _overlap_t(ns), e)


def _gdn_kernel(qkv_ref, sm_ref, zg_ref, conv0_ref, s0_ref, wc_ref, vec_ref, wg_ref,
                go_ref, xp_out_ref, st_ref, xp_s, *, bblk, chunk, n_valid, hi):
    c_idx = pl.program_id(1)

    @pl.when(c_idx == 0)
    def _():
        xp_s[:, 0:SUBLANES, :] = conv0_ref[...]
        st_ref[...] = s0_ref[...]

    wc = wc_ref[...]
    vec = vec_ref[...]
    row1 = lax.broadcasted_iota(jnp.int32, (chunk, LANES), 0)
    rr = lax.broadcasted_iota(jnp.int32, (chunk, chunk), 0)
    cc = lax.broadcasted_iota(jnp.int32, (chunk, chunk), 1)
    tri = rr >= cc
    tri_bf = tri.astype(BF16)
    eye_bf = _eye(LANES, BF16)

    acts, betas, decays, decay_ts, e_decs = [], [], [], [], []
    for b in range(bblk):
        xp_s[b, SUBLANES:SUBLANES + chunk, :] = qkv_ref[b]
        y = xp_s[b, SUBLANES:SUBLANES + chunk, :] * wc[GDN_CONV - 1:GDN_CONV, :]
        for j in range(GDN_CONV - 1):
            lo = SUBLANES - (GDN_CONV - 1) + j
            y = y + xp_s[b, lo:lo + chunk, :] * wc[j:j + 1, :]
        xp_out_ref[b] = xp_s[b]
        xp_s[b, 0:SUBLANES, :] = xp_s[b, chunk:chunk + SUBLANES, :]
        act = _silu(y)
        small = sm_ref[b]
        z = small + vec[1:2, :]
        softplus = jnp.maximum(z, 0.0) + jnp.log1p(jnp.exp(-jnp.abs(z)))
        g_all = -jnp.exp(vec[0:1, :]) * softplus
        if n_valid < chunk:
            valid = row1 < n_valid
            act = act * _tile_lanes(valid.astype(F32), GDN_CONV_CH // LANES)
            g_all = jnp.where(valid, g_all, 0.0)
        acts.append(act)
        betas.append(_sigmoid(small))
        decays.append(_mm_exact_lhs(tri_bf, g_all))
    for b in range(bblk):
        decay_ts.append(_mm_exact_lhs(eye_bf, decays[b], _NT))
        e_decs.append(jnp.exp(decays[b]))

    chains = [(b, h) for b in range(bblk) for h in range(GDN_HEADS)]
    qs, ks, kbs, dmasks, rhs_u, rhs_w, qds, kds, gls = [], [], [], [], [], [], [], [], []
    for b, h in chains:
        act = acts[b]
        qh = act[:, h * GDN_DK:(h + 1) * GDN_DK]
        kh = act[:, GDN_WIDTH + h * GDN_DK:GDN_WIDTH + (h + 1) * GDN_DK]
        vh = act[:, 2 * GDN_WIDTH + h * GDN_DV:2 * GDN_WIDTH + (h + 1) * GDN_DV]
        qh = qh * lax.rsqrt(jnp.sum(qh * qh, axis=-1, keepdims=True) + NORM_EPS) * (GDN_DK ** -0.5)
        kh = kh * lax.rsqrt(jnp.sum(kh * kh, axis=-1, keepdims=True) + NORM_EPS)
        beta = betas[b][:, SM_B + h:SM_B + h + 1]
        dcol = decays[b][:, SM_A + h:SM_A + h + 1]
        drow = decay_ts[b][SM_A + h:SM_A + h + 1, :]
        ed = e_decs[b][:, SM_A + h:SM_A + h + 1]
        dlast = decays[b][chunk - 1:chunk, SM_A + h:SM_A + h + 1]
        kb = kh * beta
        qs.append(qh)
        ks.append(kh)
        kbs.append(kb)
        dmasks.append(jnp.where(tri, jnp.exp(jnp.where(tri, dcol - drow, 0.0)), 0.0))
        rhs_u.append(vh * beta)
        rhs_w.append(kb * ed)
        qds.append(qh * ed)
        kds.append(kh * jnp.exp(dlast - dcol))
        gls.append(jnp.exp(dlast))

    n = len(chains)
    eye_c = _eye(chunk, F32)
    kkts = [_mm(kbs[i], ks[i], _NT, hi=hi) for i in range(n)]
    qks = [_mm(qs[i], ks[i], _NT) for i in range(n)]
    powers = [jnp.where(rr > cc, kkts[i] * dmasks[i], 0.0) for i in range(n)]
    qks = [qks[i] * dmasks[i] for i in range(n)]
    invs = [eye_c - powers[i] for i in range(n)]
    span = 2
    while span < n_valid:
        powers = [_mm(powers[i], powers[i], hi=hi) for i in range(n)]
        invs = [invs[i] + _mm(invs[i], powers[i], hi=hi) for i in range(n)]
        span *= 2
    us = [_mm(invs[i], rhs_u[i], hi=hi) for i in range(n)]
    ws = [_mm(invs[i], rhs_w[i], hi=hi) for i in range(n)]
    sts = [st_ref[b, h] for b, h in chains]
    v_news = [us[i] - _mm(ws[i], sts[i], hi=hi) for i in range(n)]
    os_ = [_mm(qds[i], sts[i]) for i in range(n)]
    os_ = [os_[i] + _mm(qks[i], v_news[i]) for i in range(n)]
    upd = [_mm(kds[i], v_news[i], _TN, hi=hi) for i in range(n)]
    for i, (b, h) in enumerate(chains):
        st_ref[b, h] = sts[i] * gls[i] + upd[i]
        o = os_[i]
        o = o * lax.rsqrt(jnp.mean(o * o, axis=-1, keepdims=True) + NORM_EPS) * wg_ref[...]
        o = o * _silu(zg_ref[b, :, h * GDN_DV:(h + 1) * GDN_DV])
        go_ref[b, :, h * GDN_DV:(h + 1) * GDN_DV] = o.astype(go_ref.dtype)


def _gdn(qkv, sm, zg, conv0, s0, w_conv, a_log, dt_bias, w_gnorm, batch, rows, bblk, chunk, n_valid, hi,
         out_dtype):
    nc = rows // chunk
    tile = lambda b, c: (b, c, 0)
    per_b3 = lambda b, c: (b, 0, 0)
    per_b4 = lambda b, c: (b, 0, 0, 0)
    fix2 = lambda b, c: (0, 0)
    wc = jnp.concatenate([w_conv, jnp.zeros((SUBLANES - GDN_CONV, GDN_CONV_CH), w_conv.dtype)], axis=0)
    vec = jnp.zeros((SUBLANES, LANES), F32)
    vec = vec.at[0, SM_A:SM_A + GDN_HEADS].set(a_log).at[1, SM_A:SM_A + GDN_HEADS].set(dt_bias)
    go, xp, st = pl.pallas_call(
        functools.partial(_gdn_kernel, bblk=bblk, chunk=chunk, n_valid=n_valid, hi=hi),
        grid=(batch // bblk, nc),
        in_specs=[pl.BlockSpec((bblk, chunk, GDN_CONV_CH), tile),
                  pl.BlockSpec((bblk, chunk, LANES), tile),
                  pl.BlockSpec((bblk, chunk, GDN_WIDTH), tile),
                  pl.BlockSpec((bblk, SUBLANES, GDN_CONV_CH), per_b3),
                  pl.BlockSpec((bblk, GDN_HEADS, GDN_DK, GDN_DV), per_b4),
                  pl.BlockSpec((SUBLANES, GDN_CONV_CH), fix2),
                  pl.BlockSpec((SUBLANES, LANES), fix2),
                  pl.BlockSpec((1, GDN_DV), fix2)],
        out_specs=[pl.BlockSpec((bblk, chunk, GDN_WIDTH), tile),
                   pl.BlockSpec((bblk, SUBLANES + chunk, GDN_CONV_CH), per_b3),
                   pl.BlockSpec((bblk, GDN_HEADS, GDN_DK, GDN_DV), per_b4)],
        out_shape=[jax.ShapeDtypeStruct((batch, rows, GDN_WIDTH), out_dtype),
                   jax.ShapeDtypeStruct((batch, SUBLANES + chunk, GDN_CONV_CH), F32),
                   jax.ShapeDtypeStruct((batch, GDN_HEADS, GDN_DK, GDN_DV), F32)],
        scratch_shapes=[pltpu.VMEM((bblk, SUBLANES + chunk, GDN_CONV_CH), F32)],
        compiler_params=pltpu.CompilerParams(dimension_semantics=("arbitrary", "arbitrary"),
                                             vmem_limit_bytes=VMEM_LIMIT_BYTES),
        name="gdn",
    )(qkv.reshape(batch, rows, GDN_CONV_CH), sm.reshape(batch, rows, LANES), zg.reshape(batch, rows, GDN_WIDTH),
      conv0, s0, wc, vec, w_gnorm.reshape(1, GDN_DV))
    return go.reshape(batch * rows, GDN_WIDTH), xp, st


PROMPT_TM = 512
SAMPLE_TM = 256
GDN_PROMPT_BBLK = 8
GDN_SAMPLE_BBLK = 8
NSA_SAMPLE_NB = 2


def _layer_prompt(h, lw, final, w_final, win_buf):
    w_norm, w_pack, cmp_k_w, cmp_v_w, w_conv, a_log, dt_bias, w_gnorm, w_out = lw
    batch, seq, d = h.shape
    x2d = h.reshape(batch * seq, d)
    tables = _rope_tables(jnp.arange(seq, dtype=jnp.int32))
    (q, qr, ck, cv, sk, sv, wk, wv, kvb, zn, qkv, zg, sm, ck_t, cv_t, sk_t, sv_t, wk_t, wv_t) = _project(
        x2d, w_norm, w_pack.astype(BF16), tables, seq, PROMPT_TM, False, BF16, True)
    nsa = _nsa_prompt(q, qr, kvb, ck, cv, sm, zn, cmp_k_w, cmp_v_w, batch, seq)
    conv0 = jnp.zeros((batch, SUBLANES, GDN_CONV_CH), F32)
    s0 = jnp.zeros((batch, GDN_HEADS, GDN_DK, GDN_DV), F32)
    go, xp, st = _gdn(qkv, sm, zg, conv0, s0, w_conv, a_log, dt_bias, w_gnorm,
                      batch, seq, GDN_PROMPT_BBLK, GDN_CHUNK, GDN_CHUNK, False, BF16)
    y = _out_project(x2d, nsa, go, w_out.astype(BF16), w_final, PROMPT_TM, final)
    conv_new = xp[:, SUBLANES + GDN_CHUNK - (GDN_CONV - 1):SUBLANES + GDN_CHUNK]
    from_t = lambda t: jnp.transpose(t.reshape(batch, KV_HEADS, HEAD_DIM, t.shape[-1]), (0, 3, 1, 2))
    lead = ((0, 0), (0, 0), (max(win_buf - seq, 0), 0))
    win = lambda t: from_t(jnp.pad(t, lead)[:, :, -win_buf:])
    return y.reshape(batch, seq, d), (from_t(ck_t), from_t(cv_t), from_t(sk_t), from_t(sv_t), win(wk_t), win(wv_t),
                                      conv_new, st)


def _layer_sample(h8, n_new, caches, page_table, lw, final, w_final):
    w_norm, w_pack, cmp_k_w, cmp_v_w, w_conv, a_log, dt_bias, w_gnorm, w_out = lw
    pe_rows = lambda pe2: jnp.pad(pe2, ((0, LANES - pe2.shape[0]), (0, 0))).astype(BF16)
    cmp_k_t = (cmp_k_w[0].T, cmp_k_w[1].T, pe_rows(cmp_k_w[2]))
    cmp_v_t = (cmp_v_w[0].T, cmp_v_w[1].T, pe_rows(cmp_v_w[2]))
    c_cmp_k, c_cmp_v, c_slc_k, c_slc_v, c_win_k, c_win_v, s_conv, s_gdn = caches
    batch, nr, d = h8.shape
    past = page_table.shape[1] * c_cmp_k.shape[1]
    x2d = h8.reshape(batch * nr, d)
    tables = _rope_tables(past + jnp.arange(nr, dtype=jnp.int32))
    tables = tuple(jnp.tile(t, (SAMPLE_TM // nr, 1)) for t in tables)
    (q, qr, ck, cv, sk, sv, wk, wv, _, zn, qkv, zg, sm) = _project(
        x2d, w_norm, w_pack, tables, SAMPLE_TM, SAMPLE_TM, True, F32, False)
    nsa, win_k_t, win_v_t = _nsa_sample_t(q, qr, sk, sv, wk, wv, sm, zn, c_win_k, c_win_v, c_cmp_k, c_cmp_v,
                                          c_slc_k, c_slc_v, page_table, cmp_k_t, cmp_v_t, n_new, NSA_SAMPLE_NB)
    conv0 = jnp.pad(s_conv, ((0, 0), (SUBLANES - (GDN_CONV - 1), 0), (0, 0)))
    go, xp, st = _gdn(qkv, sm, zg, conv0, s_gdn, w_conv, a_log, dt_bias, w_gnorm,
                      batch, nr, GDN_SAMPLE_BBLK, nr, n_new, True, F32)
    y = _out_project(x2d, nsa, go, w_out.astype(BF16), w_final, SAMPLE_TM, final)
    kv4 = lambda t: t.reshape(batch, nr, KV_HEADS, HEAD_DIM)[:, :n_new]
    from_t = lambda t: jnp.transpose(t.reshape(batch, KV_HEADS, HEAD_DIM, t.shape[-1]), (0, 3, 1, 2))
    conv_new = xp[:, SUBLANES + n_new - (GDN_CONV - 1):SUBLANES + n_new]
    return y.reshape(batch, nr, d), (kv4(ck), kv4(cv), kv4(sk), kv4(sv), from_t(win_k_t), from_t(win_v_t),
                                     conv_new, st)


def kernel(x_prompt, x_sample, cache_cmp_k, cache_cmp_v, cache_slc_k, cache_slc_v, cache_win_k, cache_win_v, state_conv, state_gdn, page_table, w_norm, w_in, pe_cmp_k, w_cmp_k1, w_cmp_k2, pe_cmp_v, w_cmp_v1, w_cmp_v2, w_conv, a_log, dt_bias, w_gdn_norm, w_out, w_final_norm):
    depth = w_in.shape[0]
    n_new = x_sample.shape[1]
    win_buf = cache_win_k.shape[2]
    h_p = x_prompt
    h_s = jnp.pad(x_sample, ((0, 0), (0, SAMPLE_ROWS - n_new), (0, 0)))
    st_p, st_s = [], []
    for layer in range(depth):
        lw = (w_norm[layer], _pack_w_in(w_in[layer]),
              _compress_weights(pe_cmp_k[layer], w_cmp_k1[layer], w_cmp_k2[layer]),
              _compress_weights(pe_cmp_v[layer], w_cmp_v1[layer], w_cmp_v2[layer]),
              w_conv[layer], a_log[layer], dt_bias[layer], w_gdn_norm[layer], w_out[layer])
        final = layer == depth - 1
        h_p, sp = _layer_prompt(h_p, lw, final, w_final_norm, win_buf)
        caches = (cache_cmp_k[layer], cache_cmp_v[layer], cache_slc_k[layer], cache_slc_v[layer],
                  cache_win_k[layer], cache_win_v[layer], state_conv[layer], state_gdn[layer])
        h_s, ss = _layer_sample(h_s, n_new, caches, page_table, lw, final, w_final_norm)
        st_p.append(sp)
        st_s.append(ss)
    outs = [h_p, h_s[:, :n_new]]
    for i in range(8):
        outs.append(jnp.stack([s[i] for s in st_p]))
        outs.append(jnp.stack([s[i] for s in st_s]))
    return tuple(outs)
```

```python
import functools

import numpy as np
import jax
import jax.numpy as jnp
from jax import lax
from jax.experimental import pallas as pl
from jax.experimental.pallas import tpu as pltpu

F32 = jnp.float32
BF16 = jnp.bfloat16

D_MODEL = 1024
HEAD_DIM = 64
Q_HEADS = 8
KV_HEADS = 2
GROUP = Q_HEADS // KV_HEADS
NSA_WIDTH = Q_HEADS * HEAD_DIM
KV_WIDTH = KV_HEADS * HEAD_DIM
CMP_LEN = 32
CMP_STRIDE = 16
CMP_HIDDEN = 128
SEL_BLOCK = 64
TOP_N = 8
WINDOW = 512
FORCE_BONUS = 1.0e4
ROT_DIM = HEAD_DIM // 4
ROPE_THETA = 500000.0
GDN_DK = 128
GDN_DV = 128
GDN_HEADS = 4
GDN_WIDTH = GDN_HEADS * GDN_DV
GDN_CONV = 4
GDN_CONV_CH = 3 * GDN_WIDTH
GDN_CHUNK = 64
NORM_EPS = 1e-6
MASK_VALUE = -1e30

LANES = 128
SUBLANES = 8
VMEM_LIMIT_BYTES = 56 * 1024 * 1024

QPAD_WIDTH = Q_HEADS * LANES
C_Q = 0
C_KV = C_Q + NSA_WIDTH
C_ZN = C_KV + 6 * KV_WIDTH
C_QKV = C_ZN + NSA_WIDTH
C_ZG = C_QKV + GDN_CONV_CH
C_SM = C_ZG + GDN_WIDTH
N_PACK = C_SM + LANES
SM_B = 3 * Q_HEADS
SM_A = SM_B + GDN_HEADS

TQ = 128
CK_SLC = 256
N_SUB = 128
N_CMP = N_SUB - CMP_LEN // CMP_STRIDE + 1
SAMPLE_ROWS = 8
KVB_SK = 0
KVB_SV = KVB_SK + KV_WIDTH
KVB_WK = KVB_SV + 2 * KV_WIDTH
KVB_WV = KVB_WK + KV_WIDTH
KVB_WIDTH = KVB_WV + 2 * KV_WIDTH
KVB_COLS = (KVB_SK, KVB_SV, KVB_WK, KVB_WV)
WIN_SPAN = WINDOW + TQ


def _pack_w_in(w_in):
    o_gate = NSA_WIDTH + 6 * KV_WIDTH
    o_zn = o_gate + 3 * Q_HEADS
    o_qkv = o_zn + NSA_WIDTH
    o_b = o_qkv + GDN_CONV_CH
    o_a = o_b + GDN_HEADS
    o_zg = o_a + GDN_HEADS
    wt = w_in.T
    pad = jnp.zeros((LANES - SM_A - GDN_HEADS, wt.shape[1]), wt.dtype)
    return jnp.concatenate(
        [wt[:o_gate], wt[o_zn:o_qkv], wt[o_qkv:o_b], wt[o_zg:], wt[o_gate:o_zn], wt[o_b:o_a], wt[o_a:o_zg], pad],
        axis=0)


def _rope_tables(pos):
    half = ROT_DIM // 2
    inv = ROPE_THETA ** (-(jnp.arange(half, dtype=F32) * 2.0 / ROT_DIM))
    ang = pos.astype(F32)[:, None] * inv[None, :]
    cos, sin = jnp.cos(ang), jnp.sin(ang)
    n = pos.shape[0]
    one = jnp.ones((n, HEAD_DIM - ROT_DIM), F32)
    zero = jnp.zeros((n, HEAD_DIM - ROT_DIM), F32)
    zh = jnp.zeros((n, half), F32)
    c64 = jnp.concatenate([cos, cos, one], axis=1)
    a64 = jnp.concatenate([zh, sin, zero], axis=1)
    b64 = jnp.concatenate([-sin, zh, zero], axis=1)
    tile = lambda t: jnp.concatenate([t, t], axis=1)
    return tile(c64), tile(a64), tile(b64)


def _rope128(x, c, a, b):
    half = ROT_DIM // 2
    return x * c + pltpu.roll(x, half, 1) * a + pltpu.roll(x, LANES - half, 1) * b


_NN = (((1,), (0,)), ((), ()))
_NT = (((1,), (1,)), ((), ()))
_TN = (((0,), (0,)), ((), ()))


def _split_bf16(x):
    hi = x.astype(BF16)
    return hi, (x - hi.astype(F32)).astype(BF16)


def _mm(a, b, dims=_NN, hi=False):
    dot = lambda x, y: lax.dot_general(x, y, dims, preferred_element_type=F32)
    if hi:
        a_hi, a_lo = _split_bf16(a.astype(F32))
        b_hi, b_lo = _split_bf16(b.astype(F32))
        return dot(a_hi, b_hi) + (dot(a_lo, b_hi) + dot(a_hi, b_lo))
    return dot(a.astype(BF16), b.astype(BF16))


def _mm_exact_lhs(a_bf16, x, dims=_NN):
    x1 = x.astype(BF16)
    r1 = x - x1.astype(F32)
    x2 = r1.astype(BF16)
    x3 = (r1 - x2.astype(F32)).astype(BF16)
    dot = lambda t: lax.dot_general(a_bf16, t, dims, preferred_element_type=F32)
    return dot(x1) + dot(x2) + dot(x3)


def _sigmoid(x):
    return 1.0 / (1.0 + jnp.exp(-x))


def _silu(x):
    return x * _sigmoid(x)


def _tile_rows(x, n):
    return jnp.concatenate([x] * n, axis=0)


def _tile_lanes(x, n):
    return x if n == 1 else jnp.concatenate([x] * n, axis=1)


def _eye(n, dtype):
    r = lax.broadcasted_iota(jnp.int32, (n, n), 0)
    c = lax.broadcasted_iota(jnp.int32, (n, n), 1)
    return (r == c).astype(dtype)


def _proj_kernel(x_ref, wn_ref, w_ref, c_ref, a_ref, b_ref,
                 q_ref, qr_ref, ck_ref, cv_ref, sk_ref, sv_ref, wk_ref, wv_ref, kvb_ref,
                 zn_ref, qkv_ref, zg_ref, sm_ref, *t_refs, hi_gdn):
    x = x_ref[...]
    ms = jnp.mean(x * x, axis=-1, keepdims=True)
    xn = x * lax.rsqrt(ms + NORM_EPS) * wn_ref[...]
    xb = xn.astype(BF16)
    c, a, b = c_ref[...], a_ref[...], b_ref[...]
    scale = HEAD_DIM ** -0.5
    lane = lax.broadcasted_iota(jnp.int32, (x.shape[0], LANES), 1)
    half_mask = (lane < HEAD_DIM, lane >= HEAD_DIM)
    for j in range(NSA_WIDTH // (2 * LANES)):
        q2 = _mm(xb, w_ref[C_Q + 2 * j * LANES:C_Q + 2 * (j + 1) * LANES, :], _NT)
        for jj in range(2):
            raw = q2[:, jj * LANES:(jj + 1) * LANES]
            for src, dst in ((raw, q_ref), (_rope128(raw, c, a, b), qr_ref)):
                swapped = pltpu.roll(src, HEAD_DIM, 1)
                for par in range(2):
                    head = 2 * (2 * j + jj) + par
                    kvh = head // GROUP
                    val = jnp.where(half_mask[kvh], src if par == kvh else swapped, 0.0)
                    dst[:, head * LANES:(head + 1) * LANES] = (val * scale).astype(dst.dtype)
    kv_refs = (ck_ref, cv_ref, sk_ref, sv_ref, wk_ref, wv_ref)
    kv2 = [_mm(xb, w_ref[C_KV + j * LANES:C_KV + (j + 2) * LANES, :], _NT) for j in range(0, 6, 2)]
    for j in range(6):
        kj = kv2[j // 2][:, (j % 2) * LANES:(j % 2 + 1) * LANES]
        if j in (2, 4):
            kj = _rope128(kj, c, a, b)
        kv_refs[j][...] = kj
        if j >= 2:
            lo = KVB_COLS[j - 2]
            kvb_ref[:, lo:lo + LANES] = kj.astype(BF16)
        if t_refs:
            t_refs[j][0] = kj.T
    ones = jnp.ones((x.shape[0], LANES), BF16)
    kvb_ref[:, KVB_SV + LANES:KVB_SV + 2 * LANES] = ones
    kvb_ref[:, KVB_WV + LANES:KVB_WV + 2 * LANES] = ones
    zn_ref[...] = _mm(xb, w_ref[C_ZN:C_QKV, :], _NT)
    xg = xn if hi_gdn else xb
    for j in range(3):
        lo = C_QKV + j * GDN_WIDTH
        qkv_ref[:, j * GDN_WIDTH:(j + 1) * GDN_WIDTH] = _mm(xg, w_ref[lo:lo + GDN_WIDTH, :], _NT, hi=hi_gdn)
    zg_ref[...] = _mm(xb, w_ref[C_ZG:C_SM, :], _NT)
    sm_ref[...] = _mm(xg, w_ref[C_SM:N_PACK, :], _NT, hi=hi_gdn)


def _project(x2d, w_norm, w_pack, tables, rows_per_seq, tm, hi_gdn, q_dtype, emit_t):
    n = x2d.shape[0]
    nt = rows_per_seq // tm
    row = lambda i: (i, 0)
    tab = lambda i: (i % nt, 0)
    fix = lambda i: (0, 0)
    widths = (QPAD_WIDTH, QPAD_WIDTH) + (KV_WIDTH,) * 6 + (KVB_WIDTH, NSA_WIDTH, GDN_CONV_CH, GDN_WIDTH, LANES)
    dtypes = (q_dtype, q_dtype) + (F32,) * 6 + (BF16, F32, F32, F32, F32)
    out_specs = [pl.BlockSpec((tm, w), row) for w in widths]
    out_shape = [jax.ShapeDtypeStruct((n, w), d) for w, d in zip(widths, dtypes)]
    if emit_t:
        out_specs += [pl.BlockSpec((1, KV_WIDTH, tm), lambda i: (i // nt, 0, i % nt))] * 6
        out_shape += [jax.ShapeDtypeStruct((n // rows_per_seq, KV_WIDTH, rows_per_seq), F32)] * 6
    return pl.pallas_call(
        functools.partial(_proj_kernel, hi_gdn=hi_gdn),
        grid=(n // tm,),
        in_specs=[pl.BlockSpec((tm, D_MODEL), row),
                  pl.BlockSpec((1, D_MODEL), fix),
                  pl.BlockSpec((N_PACK, D_MODEL), fix),
                  pl.BlockSpec((tm, LANES), tab),
                  pl.BlockSpec((tm, LANES), tab),
                  pl.BlockSpec((tm, LANES), tab)],
        out_specs=out_specs,
        out_shape=out_shape,
        compiler_params=pltpu.CompilerParams(dimension_semantics=("arbitrary",),
                                             vmem_limit_bytes=VMEM_LIMIT_BYTES),
        name="in_proj",
    )(x2d, w_norm.reshape(1, D_MODEL), w_pack, *tables)


def _out_kernel(x_ref, nsa_ref, gdn_ref, w_ref, wf_ref, y_ref, *, final):
    mix = jnp.concatenate([nsa_ref[...].astype(BF16), gdn_ref[...].astype(BF16)], axis=1)
    h = x_ref[...] + _mm(mix, w_ref[...])
    if final:
        ms = jnp.mean(h * h, axis=-1, keepdims=True)
        h = h * lax.rsqrt(ms + NORM_EPS) * wf_ref[...]
    y_ref[...] = h


def _out_project(x2d, nsa, gdn, w_out, w_final, tm, final):
    n = x2d.shape[0]
    row = lambda i: (i, 0)
    fix = lambda i: (0, 0)
    return pl.pallas_call(
        functools.partial(_out_kernel, final=final),
        grid=(n // tm,),
        in_specs=[pl.BlockSpec((tm, D_MODEL), row),
                  pl.BlockSpec((tm, NSA_WIDTH), row),
                  pl.BlockSpec((tm, GDN_WIDTH), row),
                  pl.BlockSpec((D_MODEL, D_MODEL), fix),
                  pl.BlockSpec((1, D_MODEL), fix)],
        out_specs=pl.BlockSpec((tm, D_MODEL), row),
        out_shape=jax.ShapeDtypeStruct((n, D_MODEL), F32),
        compiler_params=pltpu.CompilerParams(dimension_semantics=("arbitrary",),
                                             vmem_limit_bytes=VMEM_LIMIT_BYTES),
        name="out_proj",
    )(x2d, nsa, gdn, w_out, w_final.reshape(1, D_MODEL))


def _compress_weights(pe, w1, w2):
    half = CMP_STRIDE * HEAD_DIM
    z = jnp.zeros((CMP_STRIDE, HEAD_DIM, CMP_HIDDEN), w1.dtype)

    def place(wpart, h):
        wp = wpart.reshape(CMP_STRIDE, HEAD_DIM, CMP_HIDDEN)
        parts = [wp, z] if h == 0 else [z, wp]
        return jnp.stack(parts, axis=1).reshape(CMP_STRIDE * KV_WIDTH, CMP_HIDDEN)

    w1big = jnp.concatenate([place(w1[:half], 0), place(w1[:half], 1),
                             place(w1[half:], 0), place(w1[half:], 1)], axis=1)
    zz = jnp.zeros_like(w2)
    w2big = jnp.concatenate([jnp.concatenate([w2, zz], axis=1), jnp.concatenate([zz, w2], axis=1)], axis=0)
    pe_a = jnp.tile(pe[:CMP_STRIDE], (1, KV_HEADS)).reshape(1, CMP_STRIDE * KV_WIDTH)
    pe_b = jnp.tile(pe[CMP_STRIDE:], (1, KV_HEADS)).reshape(1, CMP_STRIDE * KV_WIDTH)
    pe2 = jnp.concatenate([pe_a, pe_b, jnp.zeros((SUBLANES - 2, CMP_STRIDE * KV_WIDTH), pe.dtype)], axis=0)
    return w1big.astype(BF16), w2big.astype(BF16), pe2


def _compress(sub_rows, w1_ref, w2_ref, pe_ref):
    hid = 2 * CMP_HIDDEN
    w1 = w1_ref[...]
    ab = _mm(sub_rows, w1)
    pe = pe_ref[...]
    pe_hi = pe.astype(BF16)
    pe_lo = (pe - pe_hi.astype(F32)).astype(BF16)
    r = _mm(pe_hi, w1) + _mm(pe_lo, w1)
    bias = r[0:1, :hid] + r[1:2, hid:]
    h = ab[:, :hid] + pltpu.roll(ab[:, hid:], N_SUB - 1, 0) + bias
    return _mm(_silu(h), w2_ref[...])


def _overlap_t(ns):
    c0 = np.arange(N_CMP)[None, :] * CMP_STRIDE
    b0 = np.arange(ns)[:, None] * SEL_BLOCK
    ov = np.minimum(c0 + CMP_LEN, b0 + SEL_BLOCK) - np.maximum(c0, b0)
    out = np.zeros((LANES, LANES), np.float32)
    out[:ns, :N_CMP] = np.maximum(ov, 0) / CMP_LEN
    return jnp.asarray(out, BF16)


def _select_blocks(imp_t, n_idx, q_blk, n_rows):
    forced = (n_idx == 0) | (n_idx == q_blk) | (n_idx == q_blk - 1)
    allowed = n_idx <= q_blk
    v = jnp.where(allowed, imp_t + FORCE_BONUS * forced.astype(F32), MASK_VALUE)
    rank = jnp.zeros(v.shape, F32)
    for j in range(n_rows):
        vj = v[j:j + 1, :]
        ge = jnp.where(vj >= v, 1.0, 0.0)
        gt = jnp.where(vj > v, 1.0, 0.0)
        rank = rank + jnp.where(n_idx > j, ge, gt)
    return ((rank < TOP_N) & allowed).astype(F32)


def _merge_heads(o_sum, kvh, tq):
    lane = lax.broadcasted_iota(jnp.int32, (tq, LANES), 1)
    slabs = []
    for pair in range(GROUP // 2):
        halves = []
        for par in range(2):
            o = o_sum[(2 * pair + par) * tq:(2 * pair + par + 1) * tq]
            halves.append(o if par == kvh else pltpu.roll(o, HEAD_DIM, 1))
        slabs.append(jnp.where(lane < HEAD_DIM, halves[0], halves[1]))
    return slabs


def _nsa_prompt_kernel(q_ref, qr_ref, kvb_ref, ckr_ref, cvr_ref, sm_ref, zn_ref,
                       wk1_ref, wk2_ref, pek_ref, wv1_ref, wv2_ref, pev_ref, ovt_ref, e_ref,
                       o_ref, ck_s, cv_s, s_s, mrun_s, m_s, acc_s):
    i = pl.program_id(1)

    @pl.when(i == 0)
    def _():
        ck_s[...] = _compress(ckr_ref[0], wk1_ref, wk2_ref, pek_ref).astype(BF16)
        cv_s[...] = _compress(cvr_ref[0], wv1_ref, wv2_ref, pev_ref).astype(BF16)

    kvs = range(KV_HEADS)
    t0 = i * TQ
    rows = GROUP * TQ
    row = lax.broadcasted_iota(jnp.int32, (TQ, LANES), 0)
    col = lax.broadcasted_iota(jnp.int32, (TQ, LANES), 1)
    cvalid = (CMP_STRIDE * col + (CMP_LEN - 1) <= t0 + row) & (col < N_CMP)
    cm = _tile_rows(cvalid, GROUP)
    cm_f = cm.astype(F32)
    sig = _sigmoid(sm_ref[...])

    def gate(branch, kvh):
        cols = [sig[:, branch * Q_HEADS + h:branch * Q_HEADS + h + 1] for h in range(kvh * GROUP, (kvh + 1) * GROUP)]
        return jnp.concatenate([jnp.broadcast_to(c, (TQ, LANES)) for c in cols], axis=0)

    def stack(ref, kvh):
        return jnp.concatenate([ref[:, h * LANES:(h + 1) * LANES] for h in range(kvh * GROUP, (kvh + 1) * GROUP)], axis=0)

    q_raw = [stack(q_ref, k) for k in kvs]
    q_rot = [stack(qr_ref, k) for k in kvs]

    ck, cv = ck_s[...], cv_s[...]
    s = [jnp.where(cm, _mm(q_raw[k], ck, _NT), MASK_VALUE) for k in kvs]
    e = [jnp.exp(s[k] - jnp.max(s[k], axis=1, keepdims=True)) for k in kvs]
    p = [e[k] / jnp.sum(e[k], axis=1, keepdims=True) * cm_f for k in kvs]
    o_sum = [gate(0, k) * _mm(p[k], cv) for k in kvs]
    p_sum = jnp.concatenate([p[k][0:TQ] + p[k][TQ:2 * TQ] + p[k][2 * TQ:3 * TQ] + p[k][3 * TQ:4 * TQ] for k in kvs],
                            axis=0)
    ns_rows = 32
    nq = KV_HEADS * TQ
    imp_t = _mm_exact_lhs(ovt_ref[...], p_sum, _NT)[0:ns_rows]
    n_idx = lax.broadcasted_iota(jnp.int32, (ns_rows, nq), 0)
    qb_t = (t0 + lax.broadcasted_iota(jnp.int32, (ns_rows, nq), 1) % TQ) // SEL_BLOCK
    sel_t = _select_blocks(imp_t, n_idx, qb_t, ns_rows)
    sel_t = jnp.concatenate([sel_t, jnp.zeros((LANES - ns_rows, nq), F32)], axis=0).astype(BF16)
    eye = _eye(TQ, BF16)
    sel = [_mm(eye, sel_t[:, k * TQ:(k + 1) * TQ], _NT).astype(BF16) for k in kvs]

    col_s = lax.broadcasted_iota(jnp.int32, (TQ, CK_SLC), 1)
    row_s = lax.broadcasted_iota(jnp.int32, (TQ, CK_SLC), 0)
    n_chunks = (t0 + TQ + CK_SLC - 1) // CK_SLC
    mrun_s[...] = jnp.full(mrun_s.shape, MASK_VALUE, F32)

    def score_chunks(js):
        k0 = [pl.multiple_of(j * CK_SLC, CK_SLC) for j in js]
        k_c = [kvb_ref[pl.ds(k0[n], CK_SLC), KVB_SK:KVB_SK + LANES] for n in range(len(js))]
        sc = [[_mm(q_rot[k], k_c[n], _NT) for k in kvs] for n in range(len(js))]
        picked = [[_mm(sel[k], e_ref[js[n]]) for k in kvs] for n in range(len(js))]
        for k in kvs:
            run = mrun_s[k]
            for n, j in enumerate(js):
                causal = k0[n] + col_s <= t0 + row_s
                bias = jnp.where((picked[n][k] > 0.5) & causal, 0.0, MASK_VALUE)
                sk = sc[n][k] + _tile_rows(bias, GROUP)
                s_s[k, j] = sk
                run = jnp.maximum(run, jnp.maximum(sk[:, :LANES], sk[:, LANES:]))
            mrun_s[k] = run

    def value_chunks(js):
        k0 = [pl.multiple_of(j * CK_SLC, CK_SLC) for j in js]
        v_c = [kvb_ref[pl.ds(k0[n], CK_SLC), KVB_SV:KVB_SV + 2 * LANES] for n in range(len(js))]
        pr = [[jnp.exp(s_s[k, js[n]] - _tile_lanes(m_s[k], CK_SLC // LANES)) for k in kvs] for n in range(len(js))]
        pv = [[_mm(pr[n][k], v_c[n]) for k in kvs] for n in range(len(js))]
        for k in kvs:
            acc_s[k] += sum(pv[n][k] for n in range(len(js)))

    def paired(fn):
        def pair_body(jj, carry):
            fn([2 * jj, 2 * jj + 1])
            return carry

        lax.fori_loop(0, n_chunks // 2, pair_body, 0)

        @pl.when(n_chunks % 2 == 1)
        def _():
            fn([n_chunks - 1])

    paired(score_chunks)
    for k in kvs:
        m_s[k] = jnp.broadcast_to(jnp.max(mrun_s[k], axis=1, keepdims=True), (rows, LANES))
    acc_s[...] = jnp.zeros(acc_s.shape, F32)

    paired(value_chunks)
    for k in kvs:
        acc = acc_s[k]
        o_sum[k] = o_sum[k] + gate(1, k) * (acc[:, :LANES] / acc[:, LANES:])

    ks0 = pl.multiple_of(jnp.maximum(t0 - WINDOW, 0), TQ)
    k_w = kvb_ref[pl.ds(ks0, WIN_SPAN), KVB_WK:KVB_WK + LANES]
    v_w = kvb_ref[pl.ds(ks0, WIN_SPAN), KVB_WV:KVB_WV + 2 * LANES]
    diff = (t0 + lax.broadcasted_iota(jnp.int32, (TQ, WIN_SPAN), 0)) - (ks0 + lax.broadcasted_iota(jnp.int32, (TQ, WIN_SPAN), 1))
    bias_w = _tile_rows(jnp.where((diff >= 0) & (diff <= WINDOW), 0.0, MASK_VALUE), GROUP)
    sw = [_mm(q_rot[k], k_w, _NT) + bias_w for k in kvs]
    pw = [jnp.exp(sw[k] - jnp.max(sw[k], axis=1, keepdims=True)) for k in kvs]
    rw = [_mm(pw[k], v_w) for k in kvs]
    for k in kvs:
        o_all = o_sum[k] + gate(2, k) * (rw[k][:, :LANES] / rw[k][:, LANES:])
        for pair, slab in enumerate(_merge_heads(o_all, k, TQ)):
            lo = (k * (GROUP // 2) + pair) * LANES
            o_ref[:, lo:lo + LANES] = (slab * _silu(zn_ref[:, lo:lo + LANES])).astype(o_ref.dtype)


def _nsa_prompt(q, qr, kvb, ck, cv, sm, zn, cmp_k_w, cmp_v_w, batch, seq):
    nt = seq // TQ
    ns = seq // SEL_BLOCK
    sub_w = CMP_STRIDE * KV_WIDTH
    keys = np.arange(seq)
    e = (np.arange(LANES)[None, :, None] == (keys // SEL_BLOCK).reshape(seq // CK_SLC, 1, CK_SLC))
    e = jnp.asarray(e, BF16)
    tile = lambda b, i: (b * nt + i, 0)
    per_b = lambda b, i: (b, 0)
    per_b3 = lambda b, i: (b, 0, 0)
    fix2 = lambda b, i: (0, 0)
    fix3 = lambda b, i: (0, 0, 0)
    wspecs = [pl.BlockSpec((sub_w, 4 * CMP_HIDDEN), fix2),
              pl.BlockSpec((2 * CMP_HIDDEN, KV_WIDTH), fix2),
              pl.BlockSpec((SUBLANES, sub_w), fix2)]
    rows = GROUP * TQ
    return pl.pallas_call(
        _nsa_prompt_kernel,
        grid=(batch, nt),
        in_specs=[pl.BlockSpec((TQ, QPAD_WIDTH), tile),
                  pl.BlockSpec((TQ, QPAD_WIDTH), tile),
                  pl.BlockSpec((seq, KVB_WIDTH), per_b),
                  pl.BlockSpec((1, seq // CMP_STRIDE, sub_w), per_b3),
                  pl.BlockSpec((1, seq // CMP_STRIDE, sub_w), per_b3),
                  pl.BlockSpec((TQ, LANES), tile),
                  pl.BlockSpec((TQ, NSA_WIDTH), tile)] + wspecs + wspecs + [
                  pl.BlockSpec((LANES, LANES), fix2),
                  pl.BlockSpec((seq // CK_SLC, LANES, CK_SLC), fix3)],
        out_specs=pl.BlockSpec((TQ, NSA_WIDTH), tile),
        out_shape=jax.ShapeDtypeStruct((batch * seq, NSA_WIDTH), BF16),
        scratch_shapes=[pltpu.VMEM((N_SUB, KV_WIDTH), BF16), pltpu.VMEM((N_SUB, KV_WIDTH), BF16),
                        pltpu.VMEM((KV_HEADS, seq // CK_SLC, rows, CK_SLC), F32),
                        pltpu.VMEM((KV_HEADS, rows, LANES), F32), pltpu.VMEM((KV_HEADS, rows, LANES), F32),
                        pltpu.VMEM((KV_HEADS, rows, 2 * LANES), F32)],
        compiler_params=pltpu.CompilerParams(dimension_semantics=("arbitrary", "arbitrary"),
                                             vmem_limit_bytes=VMEM_LIMIT_BYTES),
        name="nsa_prompt",
    )(q, qr, kvb, ck.reshape(batch, seq // CMP_STRIDE, sub_w), cv.reshape(batch, seq // CMP_STRIDE, sub_w),
      sm, zn, *cmp_k_w, *cmp_v_w, _overlap_t(ns), e)


def _nsa_sample_t_kernel(pt_ref, q_ref, qr_ref, skn_ref, svn_ref, wkn_ref, wvn_ref, sm_ref, zn_ref,
                         wkc_ref, wvc_ref, cmpk_hbm, cmpv_hbm, slck_hbm, slcv_hbm,
                         w1k_ref, w2k_ref, pek_ref, w1v_ref, w2v_ref, pev_ref, ovt_ref, e_ref,
                         o_ref, wko_ref, wvo_ref, kt_s, vt_s, kwt_s, vwt_s, pages_s, sems,
                         *, nb, n_pages, page, past, n_new):
    step = pl.program_id(0)
    last = pl.num_programs(0) - 1
    slot = step % 2
    caches = (cmpk_hbm, cmpv_hbm, slck_hbm, slcv_hbm)
    page_ids = [(j, c, p) for j in range(nb) for c in range(len(caches)) for p in range(n_pages)]

    def page_copy(at_step, at_slot, j, c, p):
        src = caches[c].at[pt_ref[at_step * nb + j, p]]
        return pltpu.make_async_copy(src, pages_s.at[at_slot, j, c, p], sems.at[at_slot, c])

    @pl.when(step == 0)
    def _():
        for ids in page_ids:
            page_copy(0, 0, *ids).start()

    nxt = jnp.minimum(step + 1, last)
    cmpk, cmpv, slck, slcv = ([pages_s.at[slot, j, c, p] for j in range(nb) for p in range(n_pages)]
                              for c in range(len(caches)))
    nr = SAMPLE_ROWS
    rows = Q_HEADS * nr
    win_buf = wkc_ref.shape[2]
    bs = range(nb)
    hid = 2 * CMP_HIDDEN
    zpad = jnp.zeros((LANES - nr, LANES), F32)

    def new_rows(ref, b):
        return jnp.concatenate([ref[b * nr:(b + 1) * nr, :], zpad], axis=0).astype(BF16)

    def stack(ref, b):
        return jnp.concatenate([ref[b * nr:(b + 1) * nr, h * LANES:(h + 1) * LANES] for h in range(Q_HEADS)],
                               axis=0).astype(BF16)

    for b in bs:
        kwt_s[b] = wkc_ref[b].astype(BF16)
        vwt_s[b] = wvc_ref[b].astype(BF16)
    q_raw = [stack(q_ref, b) for b in bs]
    q_rot = [stack(qr_ref, b) for b in bs]
    sw = [jnp.concatenate([_mm(q_rot[b], kwt_s[b]), _mm(q_rot[b], new_rows(wkn_ref, b), _NT)], axis=1) for b in bs]

    for ids in page_ids:
        page_copy(nxt, 1 - slot, *ids).start()
    for ids in page_ids:
        page_copy(step, slot, *ids).wait()

    for b in bs:
        for p in range(n_pages):
            kt_s[b, :, p * page:(p + 1) * page] = slck[b * n_pages + p][...].astype(BF16)
            vt_s[b, :, p * page:(p + 1) * page] = slcv[b * n_pages + p][...].astype(BF16)
    mi = lax.broadcasted_iota(jnp.int32, (page, page), 0)
    ki = lax.broadcasted_iota(jnp.int32, (page, page), 1)
    per_page = page // CMP_STRIDE
    perm = (ki == CMP_STRIDE * (mi % per_page) + mi // per_page).astype(BF16)

    def transposed_pages(pages, b):
        out = []
        for p in range(0, n_pages, 2):
            pair = jnp.concatenate([pages[b * n_pages + p][...], pages[b * n_pages + p + 1][...]], axis=0)
            both = _mm(perm, pair, _NT)
            out += [both[:, :page], both[:, page:]]
        return out

    rows_t = [[transposed_pages(pages, b) for pages in (cmpk, cmpv)] for b in bs]

    ss = [jnp.concatenate([_mm(q_rot[b], kt_s[b]), _mm(q_rot[b], new_rows(skn_ref, b), _NT)], axis=1) for b in bs]

    def sub_blocks(b, c):
        cols = [jnp.concatenate([rows_t[b][c][p][l * per_page:(l + 1) * per_page] for p in range(n_pages)], axis=0)
                for l in range(CMP_STRIDE)]
        return jnp.concatenate(cols, axis=1)

    def activate(t):
        bias = t[0:hid, LANES:LANES + 1] + t[hid:2 * hid, LANES + 1:LANES + 2]
        return _silu(t[0:hid, 0:LANES] + pltpu.roll(t[hid:2 * hid, 0:LANES], N_SUB - 1, 1) + bias)

    ckt, cvt = [], []
    for c, (w1_ref, w2_ref, pe_ref, out) in enumerate(((w1k_ref, w2k_ref, pek_ref, ckt), (w1v_ref, w2v_ref, pev_ref, cvt))):
        pe_rows = pe_ref[...]
        ht = [_mm(w1_ref[...], jnp.concatenate([sub_blocks(b, c).astype(BF16), pe_rows], axis=0), _NT) for b in bs]
        out.extend(_mm(w2_ref[...], activate(ht[b])) for b in bs)

    tok = lax.broadcasted_iota(jnp.int32, (nr, LANES), 0)
    col = lax.broadcasted_iota(jnp.int32, (nr, LANES), 1)
    cm = _tile_rows((CMP_STRIDE * col + (CMP_LEN - 1) <= past + tok) & (col < N_CMP), Q_HEADS)
    cm_f = cm.astype(F32)
    sc = [jnp.where(cm, _mm(q_raw[b], ckt[b]), MASK_VALUE) for b in bs]
    ec = [jnp.exp(sc[b] - jnp.max(sc[b], axis=1, keepdims=True)) for b in bs]
    pc = [ec[b] / jnp.sum(ec[b], axis=1, keepdims=True) * cm_f for b in bs]
    o_cmp = [_mm(pc[b], cvt[b], _NT) for b in bs]
    p_sum = jnp.concatenate(
        [sum(pc[b][(kvh * GROUP + g) * nr:(kvh * GROUP + g + 1) * nr] for g in range(GROUP))
         for b in bs for kvh in range(KV_HEADS)], axis=0)
    ns = (past + n_new + SEL_BLOCK - 1) // SEL_BLOCK
    ns_rows = -(-ns // SUBLANES) * SUBLANES
    nq = nb * KV_HEADS * nr
    imp_t = _mm_exact_lhs(ovt_ref[...], p_sum, _NT)[0:ns_rows]
    n_idx = lax.broadcasted_iota(jnp.int32, (ns_rows, nq), 0)
    qb_t = (past + lax.broadcasted_iota(jnp.int32, (ns_rows, nq), 1) % nr) // SEL_BLOCK
    sel_t = _select_blocks(imp_t, n_idx, qb_t, ns)
    sel_t = jnp.concatenate([sel_t, jnp.zeros((LANES - ns_rows, nq), F32)], axis=0)
    sel = _mm(_eye(nq, BF16), sel_t, _NT)

    all_keys = past + LANES
    q_pos = past + lax.broadcasted_iota(jnp.int32, (rows, all_keys), 0) % nr
    causal = lax.broadcasted_iota(jnp.int32, (rows, all_keys), 1) <= q_pos
    o_slc = []
    for b in bs:
        sel_rows = jnp.concatenate(
            [_tile_rows(sel[(b * KV_HEADS + kvh) * nr:(b * KV_HEADS + kvh + 1) * nr], GROUP) for kvh in range(KV_HEADS)],
            axis=0)
        ok = (_mm(sel_rows, e_ref[...]) > 0.5) & causal
        s = ss[b] + jnp.where(ok, 0.0, MASK_VALUE)
        pr = jnp.exp(s - jnp.max(s, axis=1, keepdims=True))
        o = _mm(pr[:, :past], vt_s[b], _NT) + _mm(pr[:, past:], new_rows(svn_ref, b))
        o_slc.append(o / jnp.sum(pr, axis=1, keepdims=True))

    win_keys = win_buf + LANES
    q_pos = past + lax.broadcasted_iota(jnp.int32, (rows, win_keys), 0) % nr
    diff = q_pos - (past - win_buf + lax.broadcasted_iota(jnp.int32, (rows, win_keys), 1))
    bias_w = jnp.where((diff >= 0) & (diff <= WINDOW), 0.0, MASK_VALUE)
    o_win = []
    for b in bs:
        s = sw[b] + bias_w
        pr = jnp.exp(s - jnp.max(s, axis=1, keepdims=True))
        o = _mm(pr[:, :win_buf], vwt_s[b], _NT) + _mm(pr[:, win_buf:], new_rows(wvn_ref, b))
        o_win.append(o / jnp.sum(pr, axis=1, keepdims=True))

    for b in bs:
        sig = _sigmoid(sm_ref[b * nr:(b + 1) * nr, :])

        def gate(branch):
            cols = [sig[:, branch * Q_HEADS + h:branch * Q_HEADS + h + 1] for h in range(Q_HEADS)]
            return jnp.concatenate([jnp.broadcast_to(c, (nr, LANES)) for c in cols], axis=0)

        o_all = gate(0) * o_cmp[b] + gate(1) * o_slc[b] + gate(2) * o_win[b]
        for kvh in range(KV_HEADS):
            o_kv = o_all[kvh * GROUP * nr:(kvh + 1) * GROUP * nr]
            for pair, slab in enumerate(_merge_heads(o_kv, kvh, nr)):
                lo = (kvh * (GROUP // 2) + pair) * LANES
                o_ref[b * nr:(b + 1) * nr, lo:lo + LANES] = slab * _silu(zn_ref[b * nr:(b + 1) * nr, lo:lo + LANES])

    eye_bf = _eye(LANES, BF16)
    lane = lax.broadcasted_iota(jnp.int32, (KV_WIDTH, LANES), 1)
    for cache_ref, new_ref, out_ref in ((wkc_ref, wkn_ref, wko_ref), (wvc_ref, wvn_ref, wvo_ref)):
        for b in bs:
            shifted = pltpu.roll(cache_ref[b], win_buf - n_new, 1)
            new_pad = jnp.concatenate([new_ref[b * nr:(b + 1) * nr, :], zpad], axis=0)
            new_t = _mm_exact_lhs(eye_bf, new_pad, _NT)
            tail = jnp.where(lane >= LANES - n_new, pltpu.roll(new_t, LANES - n_new, 1), shifted[:, win_buf - LANES:])
            out_ref[b] = jnp.concatenate([shifted[:, :win_buf - LANES], tail], axis=1)

    @pl.when(step == last)
    def _():
        for ids in page_ids:
            page_copy(nxt, 1 - slot, *ids).wait()


def _nsa_sample_t(q, qr, sk, sv, wk, wv, sm, zn, win_k, win_v, cmp_k, cmp_v, slc_k, slc_v, page_table,
                  cmp_k_w, cmp_v_w, n_new, nb):
    batch, n_pages = page_table.shape
    n_pool, page = cmp_k.shape[0], cmp_k.shape[1]
    past = n_pages * page
    nr = SAMPLE_ROWS
    win_buf = win_k.shape[1]
    ns = (past + n_new + SEL_BLOCK - 1) // SEL_BLOCK
    all_keys = past + LANES
    e = jnp.asarray(np.arange(LANES)[:, None] == (np.arange(all_keys) // SEL_BLOCK)[None, :], BF16)
    tview = lambda c: jnp.transpose(c, (0, 2, 3, 1)).reshape(c.shape[0], KV_WIDTH, c.shape[1])
    cmp_k, cmp_v, slc_k, slc_v, win_k, win_v = (tview(c) for c in (cmp_k, cmp_v, slc_k, slc_v, win_k, win_v))

    row = lambda b, pt: (b, 0)
    per_b3 = lambda b, pt: (b, 0, 0)
    fix2 = lambda b, pt: (0, 0)
    sub_w = CMP_STRIDE * KV_WIDTH
    wspecs = [pl.BlockSpec((4 * CMP_HIDDEN, sub_w), fix2),
              pl.BlockSpec((KV_WIDTH, 2 * CMP_HIDDEN), fix2),
              pl.BlockSpec((LANES, sub_w), fix2)]
    in_specs = ([pl.BlockSpec((nb * nr, QPAD_WIDTH), row)] * 2 + [pl.BlockSpec((nb * nr, KV_WIDTH), row)] * 4
                + [pl.BlockSpec((nb * nr, LANES), row), pl.BlockSpec((nb * nr, NSA_WIDTH), row)]
                + [pl.BlockSpec((nb, KV_WIDTH, win_buf), per_b3)] * 2
                + [pl.BlockSpec(memory_space=pl.ANY)] * 4 + wspecs + wspecs
                + [pl.BlockSpec((LANES, LANES), fix2), pl.BlockSpec((LANES, all_keys), fix2)])
    grid_spec = pltpu.PrefetchScalarGridSpec(
        num_scalar_prefetch=1,
        grid=(batch // nb,),
        in_specs=in_specs,
        out_specs=[pl.BlockSpec((nb * nr, NSA_WIDTH), row),
                   pl.BlockSpec((nb, KV_WIDTH, win_buf), per_b3), pl.BlockSpec((nb, KV_WIDTH, win_buf), per_b3)],
        scratch_shapes=[pltpu.VMEM((nb, KV_WIDTH, past), BF16), pltpu.VMEM((nb, KV_WIDTH, past), BF16),
                        pltpu.VMEM((nb, KV_WIDTH, win_buf), BF16), pltpu.VMEM((nb, KV_WIDTH, win_buf), BF16),
                        pltpu.VMEM((2, nb, 4, n_pages, KV_WIDTH, page), F32),
                        pltpu.SemaphoreType.DMA((2, 4))])
    return pl.pallas_call(
        functools.partial(_nsa_sample_t_kernel, nb=nb, n_pages=n_pages, page=page, past=past, n_new=n_new),
        grid_spec=grid_spec,
        out_shape=[jax.ShapeDtypeStruct((batch * nr, NSA_WIDTH), F32),
                   jax.ShapeDtypeStruct((batch, KV_WIDTH, win_buf), F32),
                   jax.ShapeDtypeStruct((batch, KV_WIDTH, win_buf), F32)],
        compiler_params=pltpu.CompilerParams(dimension_semantics=("arbitrary",),
                                             vmem_limit_bytes=VMEM_LIMIT_BYTES),
        name="nsa_sample",
    )(page_table, q, qr, sk, sv, wk, wv, sm, zn, win_k, win_v, cmp_k, cmp_v, slc_k, slc_v,
      *cmp_k_w, *cmp_v_w, _overlap_t(ns), e)


def _gdn_kernel(qkv_ref, sm_ref, zg_ref, conv0_ref, s0_ref, wc_ref, vec_ref, wg_ref,
                go_ref, xp_out_ref, st_ref, xp_s, *, bblk, chunk, n_valid, hi):
    c_idx = pl.program_id(1)

    @pl.when(c_idx == 0)
    def _():
        xp_s[:, 0:SUBLANES, :] = conv0_ref[...]
        st_ref[...] = s0_ref[...]

    wc = wc_ref[...]
    vec = vec_ref[...]
    row1 = lax.broadcasted_iota(jnp.int32, (chunk, LANES), 0)
    rr = lax.broadcasted_iota(jnp.int32, (chunk, chunk), 0)
    cc = lax.broadcasted_iota(jnp.int32, (chunk, chunk), 1)
    tri = rr >= cc
    tri_bf = tri.astype(BF16)
    eye_bf = _eye(LANES, BF16)

    acts, betas, decays, decay_ts, e_decs = [], [], [], [], []
    for b in range(bblk):
        xp_s[b, SUBLANES:SUBLANES + chunk, :] = qkv_ref[b]
        y = xp_s[b, SUBLANES:SUBLANES + chunk, :] * wc[GDN_CONV - 1:GDN_CONV, :]
        for j in range(GDN_CONV - 1):
            lo = SUBLANES - (GDN_CONV - 1) + j
            y = y + xp_s[b, lo:lo + chunk, :] * wc[j:j + 1, :]
        xp_out_ref[b] = xp_s[b]
        xp_s[b, 0:SUBLANES, :] = xp_s[b, chunk:chunk + SUBLANES, :]
        act = _silu(y)
        small = sm_ref[b]
        z = small + vec[1:2, :]
        softplus = jnp.maximum(z, 0.0) + jnp.log1p(jnp.exp(-jnp.abs(z)))
        g_all = -jnp.exp(vec[0:1, :]) * softplus
        if n_valid < chunk:
            valid = row1 < n_valid
            act = act * _tile_lanes(valid.astype(F32), GDN_CONV_CH // LANES)
            g_all = jnp.where(valid, g_all, 0.0)
        acts.append(act)
        betas.append(_sigmoid(small))
        decays.append(_mm_exact_lhs(tri_bf, g_all))
    for b in range(bblk):
        decay_ts.append(_mm_exact_lhs(eye_bf, decays[b], _NT))
        e_decs.append(jnp.exp(decays[b]))

    chains = [(b, h) for b in range(bblk) for h in range(GDN_HEADS)]
    qs, ks, kbs, dmasks, rhs_u, rhs_w, qds, kds, gls = [], [], [], [], [], [], [], [], []
    for b, h in chains:
        act = acts[b]
        qh = act[:, h * GDN_DK:(h + 1) * GDN_DK]
        kh = act[:, GDN_WIDTH + h * GDN_DK:GDN_WIDTH + (h + 1) * GDN_DK]
        vh = act[:, 2 * GDN_WIDTH + h * GDN_DV:2 * GDN_WIDTH + (h + 1) * GDN_DV]
        qh = qh * lax.rsqrt(jnp.sum(qh * qh, axis=-1, keepdims=True) + NORM_EPS) * (GDN_DK ** -0.5)
        kh = kh * lax.rsqrt(jnp.sum(kh * kh, axis=-1, keepdims=True) + NORM_EPS)
        beta = betas[b][:, SM_B + h:SM_B + h + 1]
        dcol = decays[b][:, SM_A + h:SM_A + h + 1]
        drow = decay_ts[b][SM_A + h:SM_A + h + 1, :]
        ed = e_decs[b][:, SM_A + h:SM_A + h + 1]
        dlast = decays[b][chunk - 1:chunk, SM_A + h:SM_A + h + 1]
        kb = kh * beta
        qs.append(qh)
        ks.append(kh)
        kbs.append(kb)
        dmasks.append(jnp.where(tri, jnp.exp(jnp.where(tri, dcol - drow, 0.0)), 0.0))
        rhs_u.append(vh * beta)
        rhs_w.append(kb * ed)
        qds.append(qh * ed)
        kds.append(kh * jnp.exp(dlast - dcol))
        gls.append(jnp.exp(dlast))

    n = len(chains)
    eye_c = _eye(chunk, F32)
    kkts = [_mm(kbs[i], ks[i], _NT, hi=hi) for i in range(n)]
    qks = [_mm(qs[i], ks[i], _NT) for i in range(n)]
    powers = [jnp.where(rr > cc, kkts[i] * dmasks[i], 0.0) for i in range(n)]
    qks = [qks[i] * dmasks[i] for i in range(n)]
    invs = [eye_c - powers[i] for i in range(n)]
    span = 2
    while span < n_valid:
        powers = [_mm(powers[i], powers[i], hi=hi) for i in range(n)]
        invs = [invs[i] + _mm(invs[i], powers[i], hi=hi) for i in range(n)]
        span *= 2
    us = [_mm(invs[i], rhs_u[i], hi=hi) for i in range(n)]
    ws = [_mm(invs[i], rhs_w[i], hi=hi) for i in range(n)]
    sts = [st_ref[b, h] for b, h in chains]
    v_news = [us[i] - _mm(ws[i], sts[i], hi=hi) for i in range(n)]
    os_ = [_mm(qds[i], sts[i]) for i in range(n)]
    os_ = [os_[i] + _mm(qks[i], v_news[i]) for i in range(n)]
    upd = [_mm(kds[i], v_news[i], _TN, hi=hi) for i in range(n)]
    for i, (b, h) in enumerate(chains):
        st_ref[b, h] = sts[i] * gls[i] + upd[i]
        o = os_[i]
        o = o * lax.rsqrt(jnp.mean(o * o, axis=-1, keepdims=True) + NORM_EPS) * wg_ref[...]
        o = o * _silu(zg_ref[b, :, h * GDN_DV:(h + 1) * GDN_DV])
        go_ref[b, :, h * GDN_DV:(h + 1) * GDN_DV] = o.astype(go_ref.dtype)


def _gdn(qkv, sm, zg, conv0, s0, w_conv, a_log, dt_bias, w_gnorm, batch, rows, bblk, chunk, n_valid, hi,
         out_dtype):
    nc = rows // chunk
    tile = lambda b, c: (b, c, 0)
    per_b3 = lambda b, c: (b, 0, 0)
    per_b4 = lambda b, c: (b, 0, 0, 0)
    fix2 = lambda b, c: (0, 0)
    wc = jnp.concatenate([w_conv, jnp.zeros((SUBLANES - GDN_CONV, GDN_CONV_CH), w_conv.dtype)], axis=0)
    vec = jnp.zeros((SUBLANES, LANES), F32)
    vec = vec.at[0, SM_A:SM_A + GDN_HEADS].set(a_log).at[1, SM_A:SM_A + GDN_HEADS].set(dt_bias)
    go, xp, st = pl.pallas_call(
        functools.partial(_gdn_kernel, bblk=bblk, chunk=chunk, n_valid=n_valid, hi=hi),
        grid=(batch // bblk, nc),
        in_specs=[pl.BlockSpec((bblk, chunk, GDN_CONV_CH), tile),
                  pl.BlockSpec((bblk, chunk, LANES), tile),
                  pl.BlockSpec((bblk, chunk, GDN_WIDTH), tile),
                  pl.BlockSpec((bblk, SUBLANES, GDN_CONV_CH), per_b3),
                  pl.BlockSpec((bblk, GDN_HEADS, GDN_DK, GDN_DV), per_b4),
                  pl.BlockSpec((SUBLANES, GDN_CONV_CH), fix2),
                  pl.BlockSpec((SUBLANES, LANES), fix2),
                  pl.BlockSpec((1, GDN_DV), fix2)],
        out_specs=[pl.BlockSpec((bblk, chunk, GDN_WIDTH), tile),
                   pl.BlockSpec((bblk, SUBLANES + chunk, GDN_CONV_CH), per_b3),
                   pl.BlockSpec((bblk, GDN_HEADS, GDN_DK, GDN_DV), per_b4)],
        out_shape=[jax.ShapeDtypeStruct((batch, rows, GDN_WIDTH), out_dtype),
                   jax.ShapeDtypeStruct((batch, SUBLANES + chunk, GDN_CONV_CH), F32),
                   jax.ShapeDtypeStruct((batch, GDN_HEADS, GDN_DK, GDN_DV), F32)],
        scratch_shapes=[pltpu.VMEM((bblk, SUBLANES + chunk, GDN_CONV_CH), F32)],
        compiler_params=pltpu.CompilerParams(dimension_semantics=("arbitrary", "arbitrary"),
                                             vmem_limit_bytes=VMEM_LIMIT_BYTES),
        name="gdn",
    )(qkv.reshape(batch, rows, GDN_CONV_CH), sm.reshape(batch, rows, LANES), zg.reshape(batch, rows, GDN_WIDTH),
      conv0, s0, wc, vec, w_gnorm.reshape(1, GDN_DV))
    return go.reshape(batch * rows, GDN_WIDTH), xp, st


PROMPT_TM = 512
SAMPLE_TM = 256
GDN_PROMPT_BBLK = 8
GDN_SAMPLE_BBLK = 8
NSA_SAMPLE_NB = 2


def _layer_prompt(h, lw, final, w_final, win_buf):
    w_norm, w_pack, cmp_k_w, cmp_v_w, w_conv, a_log, dt_bias, w_gnorm, w_out = lw
    batch, seq, d = h.shape
    x2d = h.reshape(batch * seq, d)
    tables = _rope_tables(jnp.arange(seq, dtype=jnp.int32))
    (q, qr, ck, cv, sk, sv, wk, wv, kvb, zn, qkv, zg, sm, ck_t, cv_t, sk_t, sv_t, wk_t, wv_t) = _project(
        x2d, w_norm, w_pack.astype(BF16), tables, seq, PROMPT_TM, False, BF16, True)
    nsa = _nsa_prompt(q, qr, kvb, ck, cv, sm, zn, cmp_k_w, cmp_v_w, batch, seq)
    conv0 = jnp.zeros((batch, SUBLANES, GDN_CONV_CH), F32)
    s0 = jnp.zeros((batch, GDN_HEADS, GDN_DK, GDN_DV), F32)
    go, xp, st = _gdn(qkv, sm, zg, conv0, s0, w_conv, a_log, dt_bias, w_gnorm,
                      batch, seq, GDN_PROMPT_BBLK, GDN_CHUNK, GDN_CHUNK, False, BF16)
    y = _out_project(x2d, nsa, go, w_out.astype(BF16), w_final, PROMPT_TM, final)
    conv_new = xp[:, SUBLANES + GDN_CHUNK - (GDN_CONV - 1):SUBLANES + GDN_CHUNK]
    from_t = lambda t: jnp.transpose(t.reshape(batch, KV_HEADS, HEAD_DIM, t.shape[-1]), (0, 3, 1, 2))
    lead = ((0, 0), (0, 0), (max(win_buf - seq, 0), 0))
    win = lambda t: from_t(jnp.pad(t, lead)[:, :, -win_buf:])
    return y.reshape(batch, seq, d), (from_t(ck_t), from_t(cv_t), from_t(sk_t), from_t(sv_t), win(wk_t), win(wv_t),
                                      conv_new, st)


def _layer_sample(h8, n_new, caches, page_table, lw, final, w_final):
    w_norm, w_pack, cmp_k_w, cmp_v_w, w_conv, a_log, dt_bias, w_gnorm, w_out = lw
    pe_rows = lambda pe2: jnp.pad(pe2, ((0, LANES - pe2.shape[0]), (0, 0))).astype(BF16)
    cmp_k_t = (cmp_k_w[0].T, cmp_k_w[1].T, pe_rows(cmp_k_w[2]))
    cmp_v_t = (cmp_v_w[0].T, cmp_v_w[1].T, pe_rows(cmp_v_w[2]))
    c_cmp_k, c_cmp_v, c_slc_k, c_slc_v, c_win_k, c_win_v, s_conv, s_gdn = caches
    batch, nr, d = h8.shape
    past = page_table.shape[1] * c_cmp_k.shape[1]
    x2d = h8.reshape(batch * nr, d)
    tables = _rope_tables(past + jnp.arange(nr, dtype=jnp.int32))
    tables = tuple(jnp.tile(t, (SAMPLE_TM // nr, 1)) for t in tables)
    (q, qr, ck, cv, sk, sv, wk, wv, _, zn, qkv, zg, sm) = _project(
        x2d, w_norm, w_pack, tables, SAMPLE_TM, SAMPLE_TM, True, F32, False)
    nsa, win_k_t, win_v_t = _nsa_sample_t(q, qr, sk, sv, wk, wv, sm, zn, c_win_k, c_win_v, c_cmp_k, c_cmp_v,
                                          c_slc_k, c_slc_v, page_table, cmp_k_t, cmp_v_t, n_new, NSA_SAMPLE_NB)
    conv0 = jnp.pad(s_conv, ((0, 0), (SUBLANES - (GDN_CONV - 1), 0), (0, 0)))
    go, xp, st = _gdn(qkv, sm, zg, conv0, s_gdn, w_conv, a_log, dt_bias, w_gnorm,
                      batch, nr, GDN_SAMPLE_BBLK, nr, n_new, True, F32)
    y = _out_project(x2d, nsa, go, w_out.astype(BF16), w_final, SAMPLE_TM, final)
    kv4 = lambda t: t.reshape(batch, nr, KV_HEADS, HEAD_DIM)[:, :n_new]
    from_t = lambda t: jnp.transpose(t.reshape(batch, KV_HEADS, HEAD_DIM, t.shape[-1]), (0, 3, 1, 2))
    conv_new = xp[:, SUBLANES + n_new - (GDN_CONV - 1):SUBLANES + n_new]
    return y.reshape(batch, nr, d), (kv4(ck), kv4(cv), kv4(sk), kv4(sv), from_t(win_k_t), from_t(win_v_t),
                                     conv_new, st)


def kernel(x_prompt, x_sample, cache_cmp_k, cache_cmp_v, cache_slc_k, cache_slc_v, cache_win_k, cache_win_v, state_conv, state_gdn, page_table, w_norm, w_in, pe_cmp_k, w_cmp_k1, w_cmp_k2, pe_cmp_v, w_cmp_v1, w_cmp_v2, w_conv, a_log, dt_bias, w_gdn_norm, w_out, w_final_norm):
    depth = w_in.shape[0]
    n_new = x_sample.shape[1]
    win_buf = cache_win_k.shape[2]
    h_p = x_prompt
    h_s = jnp.pad(x_sample, ((0, 0), (0, SAMPLE_ROWS - n_new), (0, 0)))
    st_p, st_s = [], []
    for layer in range(depth):
        lw = (w_norm[layer], _pack_w_in(w_in[layer]),
              _compress_weights(pe_cmp_k[layer], w_cmp_k1[layer], w_cmp_k2[layer]),
              _compress_weights(pe_cmp_v[layer], w_cmp_v1[layer], w_cmp_v2[layer]),
              w_conv[layer], a_log[layer], dt_bias[layer], w_gdn_norm[layer], w_out[layer])
        final = layer == depth - 1
        h_p, sp = _layer_prompt(h_p, lw, final, w_final_norm, win_buf)
        caches = (cache_cmp_k[layer], cache_cmp_v[layer], cache_slc_k[layer], cache_slc_v[layer],
                  cache_win_k[layer], cache_win_v[layer], state_conv[layer], state_gdn[layer])
        h_s, ss = _layer_sample(h_s, n_new, caches, page_table, lw, final, w_final_norm)
        st_p.append(sp)
        st_s.append(ss)
    outs = [h_p, h_s[:, :n_new]]
    for i in range(8):
        outs.append(jnp.stack([s[i] for s in st_p]))
        outs.append(jnp.stack([s[i] for s in st_s]))
    return tuple(outs)
```
